```python
import math
import jax
import jax.numpy as jnp
from jax import lax
import numpy as np


D_MODEL = 1024
BATCH = 4
SEQ = 4096
DEPTH = 4

GRID_W = 64
CTX_LEN = 256
N_MIXERS = 3
N_LAYERS_A = (DEPTH + 2) // 3
N_LAYERS_B = (DEPTH + 1) // 3
N_LAYERS_C = DEPTH // 3
N_MOD = 6
EPS = 1e-6

A_HEADS = 8
A_HEAD_DIM = D_MODEL // A_HEADS
A_CHUNK = 64
F_MIN = 1e-30
B_HEADS = 16
B_HEAD_DIM = D_MODEL // B_HEADS
NA_ROWS = 8
NA_COLS = 16
C_GROUP = 16
C_GROUPS = D_MODEL // C_GROUP
C_STATE = 64
D_FF = ((8 * D_MODEL // 3 + 127) // 128) * 128
FFN_CONV = 3

kernel_name = 'hybrid_hgrn2_natten_s5_diffusion_trunk'


def rmsnorm(x, g):
    xf = x.astype(jnp.float32)
    y = xf * lax.rsqrt(jnp.mean(xf * xf, axis=-1, keepdims=True) + EPS)
    return (y * g.astype(jnp.float32)).astype(x.dtype)


def dwconv_centred(h, w, b):
    pad = FFN_CONV // 2
    seq = h.shape[1]
    hp = jnp.pad(h, ((0, 0), (pad, pad), (0, 0)))
    out = b
    for k in range(FFN_CONV):
        out = out + hp[:, k:k + seq] * w[k]
    return out


def conv_ffn(h, w_in, conv_w, conv_b, w_out):
    u = dwconv_centred(h @ w_in, conv_w, conv_b)
    a, v = jnp.split(u, 2, axis=-1)
    return (jax.nn.silu(a) * v) @ w_out


def gla_chunk_scan(q, k, v, logf, s0):
    bsz, seq, heads, dk = q.shape
    dv = v.shape[-1]
    n_chunks = seq // A_CHUNK

    def to_chunks(t):
        return jnp.moveaxis(t.reshape(bsz, n_chunks, A_CHUNK, heads, t.shape[-1]), 1, 0)

    lower_tri = jnp.tril(jnp.ones((A_CHUNK, A_CHUNK), dtype=bool))[None, :, :, None, None]

    def step(state, inp):
        qc, kc, vc, gc = inp
        b = jnp.cumsum(gc, axis=1)
        diff = b[:, :, None] - b[:, None, :]
        decay = jnp.where(lower_tri, jnp.exp(jnp.where(lower_tri, diff, 0.0)), 0.0)
        scores = jnp.einsum('bthk,btshk,bshk->bhts', qc, decay, kc)
        o = jnp.einsum('bhts,bshv->bthv', scores, vc)
        o = o + jnp.einsum('bthk,bhkv->bthv', qc * jnp.exp(b), state)
        b_last = b[:, -1]
        k_dec = kc * jnp.exp(b_last[:, None] - b)
        state = jnp.exp(b_last)[..., None] * state + jnp.einsum('bshk,bshv->bhkv', k_dec, vc)
        return state, o

    s_fin, o = lax.scan(step, s0, (to_chunks(q), to_chunks(k), to_chunks(v), to_chunks(logf)))
    return jnp.moveaxis(o, 0, 1).reshape(bsz, seq, heads, dv), s_fin


def hgrn2_mixer(h_ctx, h_lat, w_in, lower, out_g, w_out):
    lb = lower.reshape(A_HEADS, A_HEAD_DIM).astype(jnp.float32)

    def project(h):
        q, i, g, zf, zb = jnp.split(h @ w_in, 5, axis=-1)
        heads = lambda t: t.reshape(t.shape[0], t.shape[1], A_HEADS, A_HEAD_DIM).astype(jnp.float32)
        return heads(q), heads(i), g, heads(zf), heads(zb)

    pc, pl = project(h_ctx), project(h_lat)
    bsz = h_lat.shape[0]
    s_zero = jnp.zeros((bsz, A_HEADS, A_HEAD_DIM, A_HEAD_DIM), jnp.float32)
    o_ctx, o_lat = [], []
    for gate_idx, rev in ((3, False), (4, True)):
        flip = (lambda t: t[:, ::-1]) if rev else (lambda t: t)

        def run(p, s0):
            f = lb + (1.0 - lb) * jax.nn.sigmoid(p[gate_idx])
            logf = jnp.log(jnp.maximum(f, F_MIN))
            o, s = gla_chunk_scan(flip(p[0]), flip(1.0 - f), flip(p[1]), flip(logf), s0)
            return flip(o), s

        oc, s_ctx = run(pc, s_zero)
        ol, _ = run(pl, s_ctx)
        o_ctx.append(oc)
        o_lat.append(ol)

    def readout(o, g):
        o = o * lax.rsqrt(jnp.mean(o * o, axis=-1, keepdims=True) + EPS)
        o = o.reshape(o.shape[0], o.shape[1], D_MODEL) * out_g.astype(jnp.float32)
        return (o * jax.nn.silu(g.astype(jnp.float32))).astype(g.dtype) @ w_out

    return readout(o_ctx[0] + o_ctx[1], pc[2]), readout(o_lat[0] + o_lat[1], pl[2])


def na_mixer(h_ctx, h_lat, w_qkv, rpb, w_out):
    bsz, seq, _ = h_lat.shape
    rows = seq // GRID_W
    kh, kw = min(NA_ROWS, rows), NA_COLS
    scale = B_HEAD_DIM ** -0.5

    def qkv(h):
        q, k, v = jnp.split(h @ w_qkv, 3, axis=-1)
        heads = lambda t: t.reshape(t.shape[0], t.shape[1], B_HEADS, B_HEAD_DIM)
        return heads(q) * scale, heads(k), heads(v)

    qc, kc, vc = qkv(h_ctx)
    ql, kl, vl = qkv(h_lat)
    p_c = jax.nn.softmax(jnp.einsum('bqhd,bkhd->bhqk', qc, kc).astype(jnp.float32), axis=-1)
    o_c = jnp.einsum('bhqk,bkhd->bqhd', p_c.astype(vc.dtype), vc)
    grid = lambda t: t.reshape(bsz, rows, GRID_W, B_HEADS, B_HEAD_DIM)
    qg, kg, vg = grid(ql), grid(kl), grid(vl)
    col = jnp.arange(GRID_W)
    col_idx = jnp.clip(col - kw // 2, 0, GRID_W - kw)[:, None] + jnp.arange(kw)[None]
    col_bias_idx = col_idx - col[:, None] + NA_COLS - 1
    n_win = kh * kw

    def row_block(r):
        r0 = jnp.clip(r - kh // 2, 0, rows - kh)
        row_bias_idx = r0 + jnp.arange(kh) - r + NA_ROWS - 1
        k_win = lax.dynamic_slice_in_dim(kg, r0, kh, axis=1)[:, :, col_idx]
        v_win = lax.dynamic_slice_in_dim(vg, r0, kh, axis=1)[:, :, col_idx]
        q_r = lax.dynamic_index_in_dim(qg, r, axis=1, keepdims=False)
        bias = rpb[:, row_bias_idx[:, None, None], col_bias_idx[None]]
        s_win = jnp.einsum('bwhd,bjwkhd->bhwjk', q_r, k_win) + jnp.transpose(bias, (0, 2, 1, 3))[None]
        s_ctx = jnp.einsum('bwhd,bchd->bhwc', q_r, kc)
        s = jnp.concatenate([s_win.reshape(bsz, B_HEADS, GRID_W, n_win), s_ctx.astype(s_win.dtype)], axis=-1)
        p = jax.nn.softmax(s.astype(jnp.float32), axis=-1).astype(vc.dtype)
        p_win = p[..., :n_win].reshape(bsz, B_HEADS, GRID_W, kh, kw)
        return (jnp.einsum('bhwjk,bjwkhd->bwhd', p_win, v_win)
                + jnp.einsum('bhwc,bchd->bwhd', p[..., n_win:], vc))

    o_l = jnp.moveaxis(lax.map(row_block, jnp.arange(rows)), 0, 1)
    return o_c.reshape(bsz, -1, D_MODEL) @ w_out, o_l.reshape(bsz, seq, D_MODEL) @ w_out


def s5_mixer(h_ctx, h_lat, lam_re, lam_im, log_dt, b_re, b_im, c_re, c_im, d_skip, w_glu):
    f32 = jnp.float32

    def groups(h):
        return h.reshape(h.shape[0], h.shape[1], C_GROUPS, C_GROUP).astype(f32)

    u_c, u_l = groups(h_ctx), groups(h_lat)
    d = d_skip.reshape(C_GROUPS, C_GROUP).astype(f32)
    y_c, y_l = d * u_c, d * u_l

    def combine(e1, e2):
        a1, b1 = e1
        a2, b2 = e2
        return a1 * a2, a2 * b1 + b2

    def scan(u, a_bar, b_bar, x0):
        bu = jnp.einsum('gpc,blgc->blgp', b_bar, u.astype(jnp.complex64))
        bu = bu.at[:, 0].add(a_bar * x0)
        _, xs = lax.associative_scan(combine, (jnp.broadcast_to(a_bar, bu.shape), bu), axis=1)
        return xs

    x_zero = jnp.zeros((u_l.shape[0], C_GROUPS, C_STATE), jnp.complex64)
    for dirn in range(2):
        flip = (lambda t: t[:, ::-1]) if dirn == 1 else (lambda t: t)
        lam = lax.complex(lam_re[dirn].astype(f32), lam_im[dirn].astype(f32))
        dt = jnp.exp(log_dt[dirn].astype(f32))[:, None]
        a_bar = jnp.exp(lam * dt)
        b_bar = ((a_bar - 1.0) / lam)[..., None] * lax.complex(b_re[dirn].astype(f32), b_im[dirn].astype(f32))
        c_mat = lax.complex(c_re[dirn].astype(f32), c_im[dirn].astype(f32))
        xs_c = scan(flip(u_c), a_bar, b_bar, x_zero)
        xs_l = scan(flip(u_l), a_bar, b_bar, xs_c[:, -1])
        read = lambda xs: flip(jnp.real(jnp.einsum('gcp,blgp->blgc', c_mat, xs)))
        y_c = y_c + read(xs_c)
        y_l = y_l + read(xs_l)

    def glu(y, dtype):
        y = jax.nn.gelu(y.reshape(y.shape[0], y.shape[1], D_MODEL)).astype(dtype)
        a, g = jnp.split(y @ w_glu, 2, axis=-1)
        return a * jax.nn.sigmoid(g)

    return glu(y_c, h_ctx.dtype), glu(y_l, h_lat.dtype)


def setup_inputs(seed: int = 0) -> dict:
    key = jax.random.key(seed)
    ks = iter(jax.random.split(key, 32))
    f32 = jnp.float32
    nrm = lambda shape, s: jax.random.normal(next(ks), shape, f32) * s
    D = D_MODEL
    return {
        'x': nrm((BATCH, SEQ, D), 1.0),
        'c': nrm((BATCH, D), 1.0),
        'ctx': nrm((BATCH, CTX_LEN, D), 1.0),
        'c_ctx': nrm((D,), 1.0),
        'w_mod': nrm((DEPTH, D, N_MOD * D), D ** -0.5),
        'b_mod': nrm((DEPTH, N_MOD * D), 0.01),
        'norm_g': 1.0 + nrm((DEPTH, 4, D), 0.02),
        'a_w_in': nrm((N_LAYERS_A, D, 5 * D), D ** -0.5),
        'a_lower_logits': nrm((DEPTH, D), 0.5),
        'a_out_g': 1.0 + nrm((N_LAYERS_A, D), 0.02),
        'a_w_out': nrm((N_LAYERS_A, D, D), D ** -0.5),
        'b_w_qkv': nrm((N_LAYERS_B, D, 3 * D), D ** -0.5),
        'b_rpb': nrm((N_LAYERS_B, B_HEADS, 2 * NA_ROWS - 1, 2 * NA_COLS - 1), 0.2),
        'b_w_out': nrm((N_LAYERS_B, D, D), D ** -0.5),
        'c_lam_re': -0.5 + nrm((N_LAYERS_C, 2, C_GROUPS, C_STATE), 0.01),
        'c_lam_im': jnp.pi * jnp.arange(C_STATE, dtype=f32) + nrm((N_LAYERS_C, 2, C_GROUPS, C_STATE), 0.01),
        'c_log_dt': jax.random.uniform(next(ks), (N_LAYERS_C, 2, C_GROUPS), f32,
                                       minval=math.log(1e-3), maxval=math.log(1e-1)),
        'c_b_re': nrm((N_LAYERS_C, 2, C_GROUPS, C_STATE, C_GROUP), (2 * C_GROUP) ** -0.5),
        'c_b_im': nrm((N_LAYERS_C, 2, C_GROUPS, C_STATE, C_GROUP), (2 * C_GROUP) ** -0.5),
        'c_c_re': nrm((N_LAYERS_C, 2, C_GROUPS, C_GROUP, C_STATE), (2 * C_STATE) ** -0.5),
        'c_c_im': nrm((N_LAYERS_C, 2, C_GROUPS, C_GROUP, C_STATE), (2 * C_STATE) ** -0.5),
        'c_d': nrm((N_LAYERS_C, D), 1.0),
        'c_w_glu': nrm((N_LAYERS_C, D, 2 * D), D ** -0.5),
        'f_w_in': nrm((DEPTH, D, 2 * D_FF), D ** -0.5),
        'f_conv_w': nrm((DEPTH, FFN_CONV, 2 * D_FF), FFN_CONV ** -0.5),
        'f_conv_b': nrm((DEPTH, 2 * D_FF), 0.01),
        'f_w_out': nrm((DEPTH, D_FF, D), D_FF ** -0.5),
    }


def reference(x, c, ctx, c_ctx, w_mod, b_mod, norm_g, a_w_in, a_lower_logits, a_out_g, a_w_out,
              b_w_qkv, b_rpb, b_w_out, c_lam_re, c_lam_im, c_log_dt, c_b_re, c_b_im, c_c_re, c_c_im,
              c_d, c_w_glu, f_w_in, f_conv_w, f_conv_b, f_w_out):
    p = jax.nn.softmax(a_lower_logits.astype(jnp.float32), axis=0)
    lower = jnp.cumsum(p, axis=0) - p[0]
    lat, cx = x, ctx
    s_lat, s_ctx = jax.nn.silu(c), jax.nn.silu(c_ctx)
    for i in range(DEPTH):
        kind, j = i % N_MIXERS, i // N_MIXERS
        last = i == DEPTH - 1
        m_l = jnp.split((s_lat @ w_mod[i] + b_mod[i])[:, None, :], N_MOD, axis=-1)
        m_c = jnp.split((s_ctx @ w_mod[i] + b_mod[i])[None, None, :], N_MOD, axis=-1)
        h_l = rmsnorm(lat, norm_g[i, 0]) * (1.0 + m_l[1]) + m_l[0]
        h_c = rmsnorm(cx, norm_g[i, 0]) * (1.0 + m_c[1]) + m_c[0]
        if kind == 0:
            y_c, y_l = hgrn2_mixer(h_c, h_l, a_w_in[j], lower[i], a_out_g[j], a_w_out[j])
        elif kind == 1:
            y_c, y_l = na_mixer(h_c, h_l, b_w_qkv[j], b_rpb[j], b_w_out[j])
        else:
            y_c, y_l = s5_mixer(h_c, h_l, c_lam_re[j], c_lam_im[j], c_log_dt[j], c_b_re[j], c_b_im[j],
                                c_c_re[j], c_c_im[j], c_d[j], c_w_glu[j])
        lat = lat + m_l[2] * rmsnorm(y_l, norm_g[i, 1])
        h_l = rmsnorm(lat, norm_g[i, 2]) * (1.0 + m_l[4]) + m_l[3]
        lat = lat + m_l[5] * rmsnorm(conv_ffn(h_l, f_w_in[i], f_conv_w[i], f_conv_b[i], f_w_out[i]), norm_g[i, 3])
        if not last:
            cx = cx + m_c[2] * rmsnorm(y_c, norm_g[i, 1])
            h_c = rmsnorm(cx, norm_g[i, 2]) * (1.0 + m_c[4]) + m_c[3]
            cx = cx + m_c[5] * rmsnorm(conv_ffn(h_c, f_w_in[i], f_conv_w[i], f_conv_b[i], f_w_out[i]), norm_g[i, 3])
    return lat
```

```python
import functools

import jax
import jax.numpy as jnp
from jax import lax
from jax.experimental import pallas as pl
from jax.experimental.pallas import tpu as pltpu

F32 = jnp.float32
BF16 = jnp.bfloat16

EPS = 1e-6
F_MIN = 1e-30
N_MOD = 6
N_MIXERS = 3
A_HEADS = 8
GRID_W = 64
S5_CHUNK = 32
NEG_BIG = -1e30
SUBLANES = 8
LANES = 128
VMEM_LIMIT_BYTES = 48 * 1024 * 1024

NT_DIMS = (((1,), (1,)), ((), ()))
TN_DIMS = (((0,), (0,)), ((), ()))


def _params(*sem):
    return pltpu.CompilerParams(dimension_semantics=sem, vmem_limit_bytes=VMEM_LIMIT_BYTES)


def _row_tile(n, want):
    t = min(n, want)
    assert n % t == 0, (n, t)
    return t


def _rms(y, g):
    return y * lax.rsqrt(jnp.mean(y * y, axis=-1, keepdims=True) + EPS) * g


def _mod_body(c_ref, w_ref, b_ref, o_ref):
    c = c_ref[...]
    s = (c * jax.nn.sigmoid(c)).astype(BF16)
    o_ref[0] = jnp.dot(s, w_ref[0].astype(BF16), preferred_element_type=F32) + b_ref[0]


def _modulation(c_rows, w_mod, b_mod):
    depth, d, n = w_mod.shape
    r = c_rows.shape[0]
    tn = n // 4
    return pl.pallas_call(
        _mod_body,
        grid=(depth, n // tn),
        in_specs=[pl.BlockSpec((r, d), lambda i, j: (0, 0)),
                  pl.BlockSpec((1, d, tn), lambda i, j: (i, 0, j)),
                  pl.BlockSpec((1, 1, tn), lambda i, j: (i, 0, j))],
        out_specs=pl.BlockSpec((1, r, tn), lambda i, j: (i, 0, j)),
        out_shape=jax.ShapeDtypeStruct((depth, r, n), F32),
        compiler_params=_params("parallel", "parallel"),
        name="adaln_mod",
    )(c_rows, w_mod, b_mod.reshape(depth, 1, n))


def _norm_mod(x, g, sh, sc):
    return _rms(x, g) * (1.0 + sc) + sh


def _proj_body(x_ref, g_ref, sh_ref, sc_ref, w_ref, o_ref, h_scr):
    @pl.when(pl.program_id(1) == 0)
    def _():
        h_scr[...] = _norm_mod(x_ref[...], g_ref[...], sh_ref[0], sc_ref[0]).astype(BF16)

    o_ref[...] = jnp.dot(h_scr[...], w_ref[...], preferred_element_type=F32).astype(o_ref.dtype)


def _norm_mod_proj(x, g, sh, sc, w, out_dtype):
    b, l, d = x.shape
    n = w.shape[1]
    tm = _row_tile(l, 512)
    tn = _row_tile(n, 512)
    per = l // tm
    out = pl.pallas_call(
        _proj_body,
        grid=(b * per, n // tn),
        in_specs=[pl.BlockSpec((tm, d), lambda i, j: (i, 0)),
                  pl.BlockSpec((1, d), lambda i, j: (0, 0)),
                  pl.BlockSpec((1, 1, d), lambda i, j: (i // per, 0, 0)),
                  pl.BlockSpec((1, 1, d), lambda i, j: (i // per, 0, 0)),
                  pl.BlockSpec((d, tn), lambda i, j: (0, j))],
        out_specs=pl.BlockSpec((tm, tn), lambda i, j: (i, j)),
        out_shape=jax.ShapeDtypeStruct((b * l, n), out_dtype),
        scratch_shapes=[pltpu.VMEM((tm, d), BF16)],
        compiler_params=_params("parallel", "arbitrary"),
        name="norm_mod_proj",
    )(x.reshape(b * l, d), g.reshape(1, d), sh, sc, w)
    return out.reshape(b, l, n)


def _norm_only_body(x_ref, g_ref, sh_ref, sc_ref, o_ref):
    o_ref[...] = _norm_mod(x_ref[...], g_ref[...], sh_ref[0], sc_ref[0])


def _norm_mod_only(x, g, sh, sc):
    b, l, d = x.shape
    tm = _row_tile(l, 512)
    per = l // tm
    out = pl.pallas_call(
        _norm_only_body,
        grid=(b * per,),
        in_specs=[pl.BlockSpec((tm, d), lambda i: (i, 0)),
                  pl.BlockSpec((1, d), lambda i: (0, 0)),
                  pl.BlockSpec((1, 1, d), lambda i: (i // per, 0, 0)),
                  pl.BlockSpec((1, 1, d), lambda i: (i // per, 0, 0))],
        out_specs=pl.BlockSpec((tm, d), lambda i: (i, 0)),
        out_shape=jax.ShapeDtypeStruct((b * l, d), F32),
        compiler_params=_params("parallel"),
        name="norm_mod",
    )(x.reshape(b * l, d), g.reshape(1, d), sh, sc)
    return out.reshape(b, l, d)


def _residual(res_ref, gate_ref, g1_ref, y):
    return res_ref[...] + gate_ref[0] * _rms(y, g1_ref[...])


def _out_plain_body(a_ref, w_ref, res_ref, gate_ref, g1_ref, o_ref):
    y = jnp.dot(a_ref[...], w_ref[...], preferred_element_type=F32)
    o_ref[...] = _residual(res_ref, gate_ref, g1_ref, y)


def _out_hgrn_body(of_ref, ob_ref, gp_ref, og_ref, w_ref, res_ref, gate_ref, g1_ref, o_ref):
    o = of_ref[...] + ob_ref[...]
    parts = []
    for h in range(A_HEADS):
        oh = o[:, h * LANES:(h + 1) * LANES]
        parts.append(oh * lax.rsqrt(jnp.mean(oh * oh, axis=-1, keepdims=True) + EPS))
    gp = gp_ref[...]
    a = jnp.concatenate(parts, axis=-1) * og_ref[...] * (gp * jax.nn.sigmoid(gp))
    y = jnp.dot(a.astype(BF16), w_ref[...], preferred_element_type=F32)
    o_ref[...] = _residual(res_ref, gate_ref, g1_ref, y)


def _gelu_tanh(y):
    return 0.5 * y * (1.0 + jnp.tanh(0.7978845608028654 * (y + 0.044715 * (y * y * y))))


def _out_glu_body(y_ref, w_ref, res_ref, gate_ref, g1_ref, o_ref):
    d = o_ref.shape[-1]
    a = _gelu_tanh(y_ref[...]).astype(BF16)
    ag = jnp.dot(a, w_ref[...], preferred_element_type=F32)
    y = ag[:, :d] * jax.nn.sigmoid(ag[:, d:])
    o_ref[...] = _residual(res_ref, gate_ref, g1_ref, y)


def _out_call(body, row_inputs, const_inputs, res, gate, g1, name):
    b, l, d = res.shape
    tm = _row_tile(l, 256)
    per = l // tm
    in_specs, args = [], []
    for arr, blk, width in row_inputs:
        in_specs.append(pl.BlockSpec((tm, width), lambda i, blk=blk: (i, blk)))
        args.append(arr.reshape(b * l, arr.shape[-1]))
    for arr in const_inputs:
        in_specs.append(pl.BlockSpec(arr.shape, lambda i, nd=arr.ndim: (0,) * nd))
        args.append(arr)
    in_specs += [pl.BlockSpec((tm, d), lambda i: (i, 0)),
                 pl.BlockSpec((1, 1, d), lambda i: (i // per, 0, 0)),
                 pl.BlockSpec((1, d), lambda i: (0, 0))]
    args += [res.reshape(b * l, d), gate, g1.reshape(1, d)]
    out = pl.pallas_call(
        body,
        grid=(b * per,),
        in_specs=in_specs,
        out_specs=pl.BlockSpec((tm, d), lambda i: (i, 0)),
        out_shape=jax.ShapeDtypeStruct((b * l, d), F32),
        compiler_params=_params("parallel"),
        name=name,
    )(*args)
    return out.reshape(b, l, d)


def _hgrn_scan_body(q_ref, v_ref, z_ref, lb_ref, s0_ref, o_ref, sfin_ref, st_scr, *, reverse, n_tiles):
    step = pl.program_id(1)

    @pl.when(step == 0)
    def _():
        st_scr[...] = s0_ref[0]

    pos = lax.broadcasted_iota(jnp.int32, (SUBLANES, LANES), 0)
    if reverse:
        pos = (SUBLANES - 1) - pos

    def earlier(x, j):
        if j == 0:
            return x
        return pltpu.roll(x, (SUBLANES - j) if reverse else j, 0)

    def later(x, j):
        return pltpu.roll(x, j if reverse else (SUBLANES - j), 0)

    def tile(i, carry):
        ti = (n_tiles - 1 - i) if reverse else i
        r0 = pl.multiple_of(ti * SUBLANES, SUBLANES)
        for h in range(A_HEADS):
            sl = slice(h * LANES, (h + 1) * LANES)
            lb = lb_ref[:, sl]
            z = z_ref[0, pl.ds(r0, SUBLANES), sl]
            q = q_ref[0, pl.ds(r0, SUBLANES), sl]
            v = v_ref[0, pl.ds(r0, SUBLANES), sl]
            f = lb + (1.0 - lb) * jax.nn.sigmoid(z)
            fm = jnp.maximum(f, F_MIN)
            kk = 1.0 - f
            pp = fm
            for j in (1, 2, 4):
                pp = pp * jnp.where(pos >= j, earlier(pp, j), 1.0)
            qq = jnp.where(pos <= SUBLANES - 2, later(fm, 1), 1.0)
            for j in (1, 2, 4):
                qq = qq * jnp.where(pos <= SUBLANES - 1 - j, later(qq, j), 1.0)
            dec = pp[0:1] if reverse else pp[SUBLANES - 1:SUBLANES]
            st = st_scr[h]
            o = lax.dot_general((q * pp).astype(BF16), st.astype(BF16), NT_DIMS,
                                preferred_element_type=F32)
            g = fm
            for d in range(SUBLANES):
                if d == 0:
                    e = q * kk
                else:
                    if d > 1:
                        g = g * earlier(fm, d - 1)
                    e = q * g * jnp.where(pos >= d, earlier(kk, d), 0.0)
                o = o + jnp.sum(e, axis=-1, keepdims=True) * earlier(v, d)
            o_ref[0, pl.ds(r0, SUBLANES), sl] = o
            u = lax.dot_general(v.astype(BF16), (kk * qq).astype(BF16), TN_DIMS,
                                preferred_element_type=F32)
            st_scr[h] = dec * st + u
        return carry

    lax.fori_loop(0, n_tiles, tile, 0)

    @pl.when(step == pl.num_programs(1) - 1)
    def _():
        sfin_ref[0] = st_scr[...]


def _hgrn_scan(proj, lb, s0, gate_blk, reverse):
    b, l, _ = proj.shape
    d = A_HEADS * LANES
    t = _row_tile(l, 512)
    nb = l // t
    seq = (lambda s: nb - 1 - s) if reverse else (lambda s: s)
    body = functools.partial(_hgrn_scan_body, reverse=reverse, n_tiles=t // SUBLANES)
    st_shape = (b, A_HEADS, LANES, LANES)
    return pl.pallas_call(
        body,
        grid=(b, nb),
        in_specs=[pl.BlockSpec((1, t, d), lambda i, s: (i, seq(s), 0)),
                  pl.BlockSpec((1, t, d), lambda i, s: (i, seq(s), 1)),
                  pl.BlockSpec((1, t, d), lambda i, s: (i, seq(s), gate_blk)),
                  pl.BlockSpec((1, d), lambda i, s: (0, 0)),
                  pl.BlockSpec((1,) + st_shape[1:], lambda i, s: (i, 0, 0, 0))],
        out_specs=[pl.BlockSpec((1, t, d), lambda i, s: (i, seq(s), 0)),
                   pl.BlockSpec((1,) + st_shape[1:], lambda i, s: (i, 0, 0, 0))],
        out_shape=[jax.ShapeDtypeStruct((b, l, d), F32), jax.ShapeDtypeStruct(st_shape, F32)],
        scratch_shapes=[pltpu.VMEM(st_shape[1:], F32)],
        compiler_params=_params("parallel", "arbitrary"),
        name="hgrn_scan_bwd" if reverse else "hgrn_scan_fwd",
    )(proj, proj, proj, lb, s0)


def _hgrn2_mixer(x_c, x_l, ng, mod_c, mod_l, w_in, lower, out_g, w_out, g1, want_ctx):
    d = x_l.shape[-1]
    p_c = _norm_mod_proj(x_c, ng, mod_c[0], mod_c[1], w_in, F32)
    p_l = _norm_mod_proj(x_l, ng, mod_l[0], mod_l[1], w_in, F32)
    lb = lower.reshape(1, d)
    zero = jnp.zeros((x_l.shape[0], A_HEADS, LANES, LANES), F32)
    o_c, o_l = [], []
    for gate_blk, rev in ((3, False), (4, True)):
        oc, s_ctx = _hgrn_scan(p_c, lb, zero, gate_blk, rev)
        ol, _ = _hgrn_scan(p_l, lb, s_ctx, gate_blk, rev)
        o_c.append(oc)
        o_l.append(ol)
    og = out_g.reshape(1, d)
    new_l = _out_call(_out_hgrn_body, [(o_l[0], 0, d), (o_l[1], 0, d), (p_l, 2, d)], [og, w_out],
                      x_l, mod_l[2], g1, "hgrn_out")
    new_c = None
    if want_ctx:
        new_c = _out_call(_out_hgrn_body, [(o_c[0], 0, d), (o_c[1], 0, d), (p_c, 2, d)], [og, w_out],
                          x_c, mod_c[2], g1, "hgrn_out")
    return new_c, new_l


def _softmax_pv(s_list, v_list):
    mx = functools.reduce(jnp.maximum, [jnp.max(s, axis=-1, keepdims=True) for s in s_list])
    ps = [jnp.exp(s - mx) for s in s_list]
    den = functools.reduce(jnp.add, [jnp.sum(p, axis=-1, keepdims=True) for p in ps])
    acc = functools.reduce(jnp.add, [jnp.dot(p.astype(BF16), v, preferred_element_type=F32)
                                     for p, v in zip(ps, v_list)])
    return acc / den


def _na_lat_body(q_ref, k_ref, v_ref, kc_ref, vc_ref, t2_ref, o_ref, *, rows, kh, scale):
    w = GRID_W
    first = lax.broadcasted_iota(jnp.int32, (w, LANES), 1) < (LANES // 2)
    kc = kc_ref[0]
    vc = vc_ref[0]

    def row(r, carry):
        r0 = jnp.clip(r - kh // 2, 0, rows - kh)
        jb = r0 - r + (t2_ref.shape[1] // 2)
        q2 = q_ref[0, pl.ds(pl.multiple_of(r * w, w), w), :] * scale
        ks = pl.multiple_of(r0 * w, w)
        k2 = k_ref[0, pl.ds(ks, kh * w), :]
        v2 = v_ref[0, pl.ds(ks, kh * w), :]
        outs = []
        for hh in range(2):
            qm = jnp.where(first if hh == 0 else jnp.logical_not(first), q2, jnp.zeros_like(q2))
            bias = jnp.concatenate([t2_ref[hh, jb + 2 * m] for m in range(kh // 2)], axis=-1)
            s_w = lax.dot_general(qm, k2, NT_DIMS, preferred_element_type=F32) + bias
            s_c = lax.dot_general(qm, kc, NT_DIMS, preferred_element_type=F32)
            outs.append(_softmax_pv([s_w, s_c], [v2, vc]))
        o_ref[0, pl.ds(pl.multiple_of(r * w, w), w), :] = jnp.where(first, outs[0], outs[1]).astype(o_ref.dtype)
        return carry

    lax.fori_loop(0, rows, row, 0)


def _na_ctx_body(q_ref, k_ref, v_ref, o_ref, *, scale):
    n = q_ref.shape[1]
    first = lax.broadcasted_iota(jnp.int32, (n, LANES), 1) < (LANES // 2)
    q2 = q_ref[0] * scale
    k2 = k_ref[0]
    v2 = v_ref[0]
    outs = []
    for hh in range(2):
        qm = jnp.where(first if hh == 0 else jnp.logical_not(first), q2, jnp.zeros_like(q2))
        s = lax.dot_general(qm, k2, NT_DIMS, preferred_element_type=F32)
        outs.append(_softmax_pv([s], [v2]))
    o_ref[0] = jnp.where(first, outs[0], outs[1]).astype(o_ref.dtype)


def _na_bias_table(rpb, kh, kw):
    w = jnp.arange(GRID_W)[:, None]
    c = jnp.arange(GRID_W)[None, :]
    c0 = jnp.clip(w - kw // 2, 0, GRID_W - kw)
    inside = (c >= c0) & (c < c0 + kw)
    cb = jnp.clip(c - w + kw - 1, 0, 2 * kw - 2)
    t = jnp.where(inside[None, None], rpb.astype(F32)[:, :, cb], NEG_BIG)
    return jnp.concatenate([t[:, :-1], t[:, 1:]], axis=-1)


def _na_mixer(x_c, x_l, ng, mod_c, mod_l, w_qkv, rpb, w_out, g1, want_ctx):
    b, l, d = x_l.shape
    lc = x_c.shape[1]
    heads = rpb.shape[0]
    na_rows, na_cols = (rpb.shape[1] + 1) // 2, (rpb.shape[2] + 1) // 2
    dh = d // heads
    assert 2 * dh == LANES and l % GRID_W == 0
    rows = l // GRID_W
    kh = min(na_rows, rows)
    assert kh % 2 == 0 and kh == na_rows
    scale = dh ** -0.5
    nhp = heads // 2
    qkv_c = _norm_mod_proj(x_c, ng, mod_c[0], mod_c[1], w_qkv, BF16)
    qkv_l = _norm_mod_proj(x_l, ng, mod_l[0], mod_l[1], w_qkv, BF16)
    t2 = _na_bias_table(rpb, kh, na_cols)
    o_l = pl.pallas_call(
        functools.partial(_na_lat_body, rows=rows, kh=kh, scale=scale),
        grid=(b, nhp),
        in_specs=[pl.BlockSpec((1, l, LANES), lambda i, p: (i, 0, p)),
                  pl.BlockSpec((1, l, LANES), lambda i, p: (i, 0, nhp + p)),
                  pl.BlockSpec((1, l, LANES), lambda i, p: (i, 0, 2 * nhp + p)),
                  pl.BlockSpec((1, lc, LANES), lambda i, p: (i, 0, nhp + p)),
                  pl.BlockSpec((1, lc, LANES), lambda i, p: (i, 0, 2 * nhp + p)),
                  pl.BlockSpec((2,) + t2.shape[1:], lambda i, p: (p, 0, 0, 0))],
        out_specs=pl.BlockSpec((1, l, LANES), lambda i, p: (i, 0, p)),
        out_shape=jax.ShapeDtypeStruct((b, l, d), BF16),
        compiler_params=_params("parallel", "parallel"),
        name="na_latent",
    )(qkv_l, qkv_l, qkv_l, qkv_c, qkv_c, t2)
    new_l = _out_call(_out_plain_body, [(o_l, 0, d)], [w_out], x_l, mod_l[2], g1, "na_out")
    new_c = None
    if want_ctx:
        o_c = pl.pallas_call(
            functools.partial(_na_ctx_body, scale=scale),
            grid=(b, nhp),
            in_specs=[pl.BlockSpec((1, lc, LANES), lambda i, p: (i, 0, p)),
                      pl.BlockSpec((1, lc, LANES), lambda i, p: (i, 0, nhp + p)),
                      pl.BlockSpec((1, lc, LANES), lambda i, p: (i, 0, 2 * nhp + p))],
            out_specs=pl.BlockSpec((1, lc, LANES), lambda i, p: (i, 0, p)),
            out_shape=jax.ShapeDtypeStruct((b, lc, d), BF16),
            compiler_params=_params("parallel", "parallel"),
            name="na_context",
        )(qkv_c, qkv_c, qkv_c)
        new_c = _out_call(_out_plain_body, [(o_c, 0, d)], [w_out], x_c, mod_c[2], g1, "na_out")
    return new_c, new_l


S5_GROUP_BLOCK = 8


def _s5_kernel_body(c_ref, w_ref, o_ref):
    for i in range(c_ref.shape[0]):
        o_ref[i] = jnp.dot(c_ref[i], w_ref[i], preferred_element_type=F32, precision=lax.Precision.HIGHEST)


def _s5_impulse(cmat, wmat):
    n = cmat.shape[0]
    gb = S5_GROUP_BLOCK
    return pl.pallas_call(
        _s5_kernel_body,
        grid=(n // gb,),
        in_specs=[pl.BlockSpec((gb,) + cmat.shape[1:], lambda i: (i, 0, 0)),
                  pl.BlockSpec((gb,) + wmat.shape[1:], lambda i: (i, 0, 0))],
        out_specs=pl.BlockSpec((gb, cmat.shape[1], wmat.shape[2]), lambda i: (i, 0, 0)),
        out_shape=jax.ShapeDtypeStruct((n, cmat.shape[1], wmat.shape[2]), F32),
        compiler_params=_params("parallel"),
        name="s5_impulse",
    )(cmat, wmat)


def _s5_operators(lam_re, lam_im, log_dt, b_re, b_im, c_re, c_im, t):
    _, g, p = lam_re.shape
    cg = b_re.shape[-1]
    lam = lax.complex(lam_re.astype(F32), lam_im.astype(F32))
    ldt = lam * jnp.exp(log_dt.astype(F32))[..., None]
    a = jnp.exp(ldt)
    bbar = ((a - 1.0) / lam)[..., None] * lax.complex(b_re.astype(F32), b_im.astype(F32))
    cm = lax.complex(c_re.astype(F32), c_im.astype(F32))
    apow = jnp.exp(ldt[..., None] * jnp.arange(t + 1, dtype=F32))
    w = apow[..., :t, None] * bbar[:, :, :, None, :]
    wmat = jnp.concatenate([jnp.real(w), jnp.imag(w)], axis=2).reshape(2 * g, 2 * p, t * cg)
    cmat = jnp.concatenate([jnp.real(cm), -jnp.imag(cm)], axis=-1).reshape(2 * g, cg, 2 * p)
    k = _s5_impulse(cmat, wmat).reshape(2, g, cg, t, cg)
    k = jnp.transpose(k, (0, 1, 3, 4, 2))
    kfull = jnp.concatenate([k[1, :, :0:-1], (k[0, :, :1] + k[1, :, :1]), k[0, :, 1:]], axis=1)
    idx = jnp.arange(t)[None, :] - jnp.arange(t)[:, None] + (t - 1)
    toep = jnp.transpose(kfull[:, idx], (0, 1, 3, 2, 4)).reshape(g, t * cg, t * cg)

    def state_in(wd, flip):
        wd = wd[:, :, ::-1] if flip else wd
        m = jnp.transpose(wd, (0, 2, 3, 1)).reshape(g, t * cg, p)
        return jnp.concatenate([jnp.real(m), jnp.imag(m)], axis=-1)

    def state_out(cd, pw):
        n = cd[:, :, :, None] * pw[:, None, :, :]
        n = jnp.transpose(n, (0, 2, 3, 1)).reshape(g, p, t * cg)
        return jnp.concatenate([jnp.real(n), -jnp.imag(n)], axis=1)

    m_f = state_in(w[0], True)
    m_b = state_in(w[1], False)
    n_f = state_out(cm[0], apow[0][..., 1:])
    n_b = state_out(cm[1], apow[1][..., :0:-1])
    at = apow[..., t]
    a1 = jnp.concatenate([jnp.real(at), jnp.real(at)], axis=-1)[:, :, None, :]
    a2 = jnp.concatenate([-jnp.imag(at), jnp.imag(at)], axis=-1)[:, :, None, :]
    bf = lambda m: m.astype(BF16)
    return dict(toep=bf(toep), m_f=bf(m_f), m_b=bf(m_b), n_f=bf(n_f), n_b=bf(n_b), a1=a1, a2=a2)


def _s5_body(u_ref, dsk_ref, toep_ref, mf_ref, mb_ref, nf_ref, nb_ref, a1_ref, a2_ref, x0_ref,
             y_ref, xfin_ref, z_scr, xin_scr):
    gb, nc = u_ref.shape[1], u_ref.shape[2]
    half = LANES // 2
    for gi in range(gb):
        ub = u_ref[0, gi].astype(BF16)
        z_scr[0, :, gi, :] = jnp.dot(ub, mf_ref[gi], preferred_element_type=F32)
        z_scr[1, :, gi, :] = jnp.dot(ub, mb_ref[gi], preferred_element_type=F32)

    def scan(j, xs):
        out = []
        for dr in range(2):
            jj = j if dr == 0 else nc - 1 - j
            x = xs[dr]
            xin_scr[dr, jj] = x
            out.append(a1_ref[dr, :, 0, :] * x + a2_ref[dr, :, 0, :] * pltpu.roll(x, half, 1) + z_scr[dr, jj])
        return tuple(out)

    xf, xb = lax.fori_loop(0, nc, scan, (x0_ref[0, 0], x0_ref[0, 1]))
    xfin_ref[0, 0] = xf
    xfin_ref[0, 1] = xb
    for gi in range(gb):
        u = u_ref[0, gi]
        y = jnp.dot(u.astype(BF16), toep_ref[gi], preferred_element_type=F32) + u * dsk_ref[gi]
        y = y + jnp.dot(xin_scr[0, :, gi, :].astype(BF16), nf_ref[gi], preferred_element_type=F32)
        y = y + jnp.dot(xin_scr[1, :, gi, :].astype(BF16), nb_ref[gi], preferred_element_type=F32)
        y_ref[0, gi] = y


def _s5_scan(h, ops, dsk, x0):
    b, l, d = h.shape
    g = ops["toep"].shape[0]
    cg = d // g
    t = S5_CHUNK
    nc = l // t
    tc = t * cg
    gb = S5_GROUP_BLOCK
    u = jnp.transpose(h.reshape(b, nc, t, g, cg), (0, 3, 1, 2, 4)).reshape(b, g, nc, tc)
    per_g = lambda shape: pl.BlockSpec((gb,) + shape, lambda j, i: (j,) + (0,) * len(shape))
    y, xfin = pl.pallas_call(
        _s5_body,
        grid=(g // gb, b),
        in_specs=[pl.BlockSpec((1, gb, nc, tc), lambda j, i: (i, j, 0, 0)),
                  per_g((1, tc)), per_g((tc, tc)), per_g((tc, LANES)), per_g((tc, LANES)),
                  per_g((LANES, tc)), per_g((LANES, tc)),
                  pl.BlockSpec((2, gb, 1, LANES), lambda j, i: (0, j, 0, 0)),
                  pl.BlockSpec((2, gb, 1, LANES), lambda j, i: (0, j, 0, 0)),
                  pl.BlockSpec((1, 2, gb, LANES), lambda j, i: (i, 0, j, 0))],
        out_specs=[pl.BlockSpec((1, gb, nc, tc), lambda j, i: (i, j, 0, 0)),
                   pl.BlockSpec((1, 2, gb, LANES), lambda j, i: (i, 0, j, 0))],
        out_shape=[jax.ShapeDtypeStruct((b, g, nc, tc), F32), jax.ShapeDtypeStruct((b, 2, g, LANES), F32)],
        scratch_shapes=[pltpu.VMEM((2, nc, gb, LANES), F32), pltpu.VMEM((2, nc, gb, LANES), F32)],
        compiler_params=_params("parallel", "parallel"),
        name="s5_scan",
    )(u, dsk, ops["toep"], ops["m_f"], ops["m_b"], ops["n_f"], ops["n_b"], ops["a1"], ops["a2"], x0)
    y = jnp.transpose(y.reshape(b, g, nc, t, cg), (0, 2, 3, 1, 4)).reshape(b, l, d)
    return y, xfin


def _s5_mixer(x_c, x_l, ng, mod_c, mod_l, lam_re, lam_im, log_dt, b_re, b_im, c_re, c_im, d_skip, w_glu, g1,
              want_ctx):
    b, l, d = x_l.shape
    g, p = lam_re.shape[1], lam_re.shape[2]
    cg = d // g
    assert 2 * p == LANES and g % S5_GROUP_BLOCK == 0
    ops = _s5_operators(lam_re, lam_im, log_dt, b_re, b_im, c_re, c_im, S5_CHUNK)
    dsk = jnp.tile(d_skip.astype(F32).reshape(g, 1, cg), (1, 1, S5_CHUNK))
    h_c = _norm_mod_only(x_c, ng, mod_c[0], mod_c[1])
    h_l = _norm_mod_only(x_l, ng, mod_l[0], mod_l[1])
    y_c, x_ctx = _s5_scan(h_c, ops, dsk, jnp.zeros((b, 2, g, LANES), F32))
    y_l, _ = _s5_scan(h_l, ops, dsk, x_ctx)
    new_l = _out_call(_out_glu_body, [(y_l, 0, d)], [w_glu], x_l, mod_l[2], g1, "s5_glu_out")
    new_c = None
    if want_ctx:
        new_c = _out_call(_out_glu_body, [(y_c, 0, d)], [w_glu], x_c, mod_c[2], g1, "s5_glu_out")
    return new_c, new_l


def _conv_gate_body(ua_ref, uv_ref, wa_ref, wv_ref, ba_ref, bv_ref, o_ref, *, chunk):
    l = ua_ref.shape[1]
    rid = lax.broadcasted_iota(jnp.int32, (chunk, ua_ref.shape[2]), 0)

    def conv(u_ref, w_ref, b_ref, r0):
        x = u_ref[0, r0:r0 + chunk, :]
        zero = jnp.zeros((1, x.shape[1]), F32)
        before = u_ref[0, r0 - SUBLANES:r0, :][SUBLANES - 1:] if r0 > 0 else zero
        after = u_ref[0, r0 + chunk:r0 + chunk + SUBLANES, :][:1] if r0 + chunk < l else zero
        xm = jnp.where(rid == 0, before, pltpu.roll(x, 1, 0))
        xp = jnp.where(rid == chunk - 1, after, pltpu.roll(x, chunk - 1, 0))
        return b_ref[...] + xm * w_ref[0:1] + x * w_ref[1:2] + xp * w_ref[2:3]

    for r0 in range(0, l, chunk):
        a = conv(ua_ref, wa_ref, ba_ref, r0)
        v = conv(uv_ref, wv_ref, bv_ref, r0)
        o_ref[0, r0:r0 + chunk, :] = (a * jax.nn.sigmoid(a) * v).astype(o_ref.dtype)


def _conv_gate(u, conv_w, conv_b):
    b, l, n2 = u.shape
    f = n2 // 2
    tn = 256
    assert f % tn == 0 and conv_w.shape[0] == 3
    nj = f // tn
    chunk = _row_tile(l, 512)
    cb = conv_b.reshape(1, n2)
    return pl.pallas_call(
        functools.partial(_conv_gate_body, chunk=chunk),
        grid=(b, nj),
        in_specs=[pl.BlockSpec((1, l, tn), lambda i, j: (i, 0, j)),
                  pl.BlockSpec((1, l, tn), lambda i, j: (i, 0, nj + j)),
                  pl.BlockSpec((3, tn), lambda i, j: (0, j)),
                  pl.BlockSpec((3, tn), lambda i, j: (0, nj + j)),
                  pl.BlockSpec((1, tn), lambda i, j: (0, j)),
                  pl.BlockSpec((1, tn), lambda i, j: (0, nj + j))],
        out_specs=pl.BlockSpec((1, l, tn), lambda i, j: (i, 0, j)),
        out_shape=jax.ShapeDtypeStruct((b, l, f), BF16),
        compiler_params=_params("parallel", "parallel"),
        name="ffn_conv_gate",
    )(u, u, conv_w, conv_w, cb, cb)


def _conv_ffn_block(x, ng2, mod, w_in, conv_w, conv_b, w_out, ng3):
    d = x.shape[-1]
    u = _norm_mod_proj(x, ng2, mod[3], mod[4], w_in, F32)
    a = _conv_gate(u, conv_w, conv_b)
    return _out_call(_out_plain_body, [(a, 0, a.shape[-1])], [w_out], x, mod[5], ng3, "ffn_out")


def kernel(x, c, ctx, c_ctx, w_mod, b_mod, norm_g, a_w_in, a_lower_logits, a_out_g, a_w_out, b_w_qkv, b_rpb,
           b_w_out, c_lam_re, c_lam_im, c_log_dt, c_b_re, c_b_im, c_c_re, c_c_im, c_d, c_w_glu, f_w_in,
           f_conv_w, f_conv_b, f_w_out):
    bsz, _, d = x.shape
    depth = w_mod.shape[0]
    p = jax.nn.softmax(a_lower_logits.astype(F32), axis=0)
    lower = jnp.cumsum(p, axis=0) - p[0]
    n_rows = -(-(bsz + 1) // SUBLANES) * SUBLANES
    c_rows = jnp.zeros((n_rows, d), F32).at[:bsz].set(c).at[bsz].set(c_ctx)
    mods = _modulation(c_rows, w_mod, b_mod)
    wb = lambda w: w.astype(BF16)
    lat, cx = x, ctx
    for i in range(depth):
        kind, j = i % N_MIXERS, i // N_MIXERS
        last = i == depth - 1
        m = mods[i].reshape(n_rows, N_MOD, d)
        mod_l = [m[:bsz, k][:, None, :] for k in range(N_MOD)]
        mod_c = [jnp.broadcast_to(m[bsz, k][None, None, :], (bsz, 1, d)) for k in range(N_MOD)]
        ng = norm_g[i]
        if kind == 0:
            cx1, lat = _hgrn2_mixer(cx, lat, ng[0], mod_c, mod_l, wb(a_w_in[j]), lower[i], a_out_g[j],
                                    wb(a_w_out[j]), ng[1], not last)
        elif kind == 1:
            cx1, lat = _na_mixer(cx, lat, ng[0], mod_c, mod_l, wb(b_w_qkv[j]), b_rpb[j], wb(b_w_out[j]), ng[1],
                                 not last)
        else:
            cx1, lat = _s5_mixer(cx, lat, ng[0], mod_c, mod_l, c_lam_re[j], c_lam_im[j], c_log_dt[j], c_b_re[j],
                                 c_b_im[j], c_c_re[j], c_c_im[j], c_d[j], wb(c_w_glu[j]), ng[1], not last)
        lat = _conv_ffn_block(lat, ng[2], mod_l, wb(f_w_in[i]), f_conv_w[i], f_conv_b[i], wb(f_w_out[i]), ng[3])
        if not last:
            cx = _conv_ffn_block(cx1, ng[2], mod_c, wb(f_w_in[i]), f_conv_w[i], f_conv_b[i], wb(f_w_out[i]), ng[3])
    return lat
```

```python
import functools

import jax
import jax.numpy as jnp
from jax import lax
from jax.experimental import pallas as pl
from jax.experimental.pallas import tpu as pltpu

F32 = jnp.float32
BF16 = jnp.bfloat16

EPS = 1e-6
F_MIN = 1e-30
N_MOD = 6
N_MIXERS = 3
A_HEADS = 8
GRID_W = 64
S5_CHUNK = 32
NEG_BIG = -1e30
SUBLANES = 8
LANES = 128
VMEM_LIMIT_BYTES = 48 * 1024 * 1024

NT_DIMS = (((1,), (1,)), ((), ()))
TN_DIMS = (((0,), (0,)), ((), ()))


def _params(*sem):
    return pltpu.CompilerParams(dimension_semantics=sem, vmem_limit_bytes=VMEM_LIMIT_BYTES)


def _row_tile(n, want):
    t = min(n, want)
    assert n % t == 0, (n, t)
    return t


def _rms(y, g):
    return y * lax.rsqrt(jnp.mean(y * y, axis=-1, keepdims=True) + EPS) * g


def _mod_body(c_ref, w_ref, b_ref, o_ref):
    c = c_ref[...]
    s = (c * jax.nn.sigmoid(c)).astype(BF16)
    o_ref[0] = jnp.dot(s, w_ref[0].astype(BF16), preferred_element_type=F32) + b_ref[0]


def _modulation(c_rows, w_mod, b_mod):
    depth, d, n = w_mod.shape
    r = c_rows.shape[0]
    tn = n // 4
    return pl.pallas_call(
        _mod_body,
        grid=(depth, n // tn),
        in_specs=[pl.BlockSpec((r, d), lambda i, j: (0, 0)),
                  pl.BlockSpec((1, d, tn), lambda i, j: (i, 0, j)),
                  pl.BlockSpec((1, 1, tn), lambda i, j: (i, 0, j))],
        out_specs=pl.BlockSpec((1, r, tn), lambda i, j: (i, 0, j)),
        out_shape=jax.ShapeDtypeStruct((depth, r, n), F32),
        compiler_params=_params("parallel", "parallel"),
        name="adaln_mod",
    )(c_rows, w_mod, b_mod.reshape(depth, 1, n))


def _norm_mod(x, g, sh, sc):
    return _rms(x, g) * (1.0 + sc) + sh


def _proj_body(x_ref, g_ref, sh_ref, sc_ref, w_ref, o_ref, h_scr):
    @pl.when(pl.program_id(1) == 0)
    def _():
        h_scr[...] = _norm_mod(x_ref[...], g_ref[...], sh_ref[0], sc_ref[0]).astype(BF16)

    o_ref[...] = jnp.dot(h_scr[...], w_ref[...], preferred_element_type=F32).astype(o_ref.dtype)


def _norm_mod_proj(x, g, sh, sc, w, out_dtype):
    b, l, d = x.shape
    n = w.shape[1]
    tm = _row_tile(l, 1024)
    tn = _row_tile(n, 512)
    per = l // tm
    out = pl.pallas_call(
        _proj_body,
        grid=(b * per, n // tn),
        in_specs=[pl.BlockSpec((tm, d), lambda i, j: (i, 0)),
                  pl.BlockSpec((1, d), lambda i, j: (0, 0)),
                  pl.BlockSpec((1, 1, d), lambda i, j: (i // per, 0, 0)),
                  pl.BlockSpec((1, 1, d), lambda i, j: (i // per, 0, 0)),
                  pl.BlockSpec((d, tn), lambda i, j: (0, j))],
        out_specs=pl.BlockSpec((tm, tn), lambda i, j: (i, j)),
        out_shape=jax.ShapeDtypeStruct((b * l, n), out_dtype),
        scratch_shapes=[pltpu.VMEM((tm, d), BF16)],
        compiler_params=_params("parallel", "arbitrary"),
        name="norm_mod_proj",
    )(x.reshape(b * l, d), g.reshape(1, d), sh, sc, w)
    return out.reshape(b, l, n)


def _norm_only_body(x_ref, g_ref, sh_ref, sc_ref, o_ref):
    o_ref[...] = _norm_mod(x_ref[...], g_ref[...], sh_ref[0], sc_ref[0])


def _norm_mod_only(x, g, sh, sc):
    b, l, d = x.shape
    tm = _row_tile(l, 512)
    per = l // tm
    out = pl.pallas_call(
        _norm_only_body,
        grid=(b * per,),
        in_specs=[pl.BlockSpec((tm, d), lambda i: (i, 0)),
                  pl.BlockSpec((1, d), lambda i: (0, 0)),
                  pl.BlockSpec((1, 1, d), lambda i: (i // per, 0, 0)),
                  pl.BlockSpec((1, 1, d), lambda i: (i // per, 0, 0))],
        out_specs=pl.BlockSpec((tm, d), lambda i: (i, 0)),
        out_shape=jax.ShapeDtypeStruct((b * l, d), F32),
        compiler_params=_params("parallel"),
        name="norm_mod",
    )(x.reshape(b * l, d), g.reshape(1, d), sh, sc)
    return out.reshape(b, l, d)


def _residual(res_ref, gate_ref, g1_ref, y):
    return res_ref[...] + gate_ref[0] * _rms(y, g1_ref[...])


def _out_plain_body(a_ref, w_ref, res_ref, gate_ref, g1_ref, o_ref):
    y = jnp.dot(a_ref[...], w_ref[...], preferred_element_type=F32)
    o_ref[...] = _residual(res_ref, gate_ref, g1_ref, y)


def _out_hgrn_body(of_ref, ob_ref, gp_ref, og_ref, w_ref, res_ref, gate_ref, g1_ref, o_ref):
    o = of_ref[...] + ob_ref[...]
    parts = []
    for h in range(A_HEADS):
        oh = o[:, h * LANES:(h + 1) * LANES]
        parts.append(oh * lax.rsqrt(jnp.mean(oh * oh, axis=-1, keepdims=True) + EPS))
    gp = gp_ref[...]
    a = jnp.concatenate(parts, axis=-1) * og_ref[...] * (gp * jax.nn.sigmoid(gp))
    y = jnp.dot(a.astype(BF16), w_ref[...], preferred_element_type=F32)
    o_ref[...] = _residual(res_ref, gate_ref, g1_ref, y)


def _gelu_tanh(y):
    return 0.5 * y * (1.0 + jnp.tanh(0.7978845608028654 * (y + 0.044715 * (y * y * y))))


def _out_glu_body(y_ref, w_ref, res_ref, gate_ref, g1_ref, o_ref):
    d = o_ref.shape[-1]
    a = _gelu_tanh(y_ref[...]).astype(BF16)
    ag = jnp.dot(a, w_ref[...], preferred_element_type=F32)
    y = ag[:, :d] * jax.nn.sigmoid(ag[:, d:])
    o_ref[...] = _residual(res_ref, gate_ref, g1_ref, y)


def _out_call(body, row_inputs, const_inputs, res, gate, g1, name):
    b, l, d = res.shape
    tm = _row_tile(l, 256)
    per = l // tm
    in_specs, args = [], []
    for arr, blk, width in row_inputs:
        in_specs.append(pl.BlockSpec((tm, width), lambda i, blk=blk: (i, blk)))
        args.append(arr.reshape(b * l, arr.shape[-1]))
    for arr in const_inputs:
        in_specs.append(pl.BlockSpec(arr.shape, lambda i, nd=arr.ndim: (0,) * nd))
        args.append(arr)
    in_specs += [pl.BlockSpec((tm, d), lambda i: (i, 0)),
                 pl.BlockSpec((1, 1, d), lambda i: (i // per, 0, 0)),
                 pl.BlockSpec((1, d), lambda i: (0, 0))]
    args += [res.reshape(b * l, d), gate, g1.reshape(1, d)]
    out = pl.pallas_call(
        body,
        grid=(b * per,),
        in_specs=in_specs,
        out_specs=pl.BlockSpec((tm, d), lambda i: (i, 0)),
        out_shape=jax.ShapeDtypeStruct((b * l, d), F32),
        compiler_params=_params("parallel"),
        name=name,
    )(*args)
    return out.reshape(b, l, d)


def _hgrn_scan_body(q_ref, v_ref, z_ref, lb_ref, s0_ref, o_ref, sfin_ref, st_scr, *, reverse, n_tiles):
    step = pl.program_id(1)

    @pl.when(step == 0)
    def _():
        st_scr[...] = s0_ref[0]

    pos = lax.broadcasted_iota(jnp.int32, (SUBLANES, LANES), 0)
    if reverse:
        pos = (SUBLANES - 1) - pos

    def earlier(x, j):
        if j == 0:
            return x
        return pltpu.roll(x, (SUBLANES - j) if reverse else j, 0)

    def later(x, j):
        return pltpu.roll(x, j if reverse else (SUBLANES - j), 0)

    def tile(i, carry):
        ti = (n_tiles - 1 - i) if reverse else i
        r0 = pl.multiple_of(ti * SUBLANES, SUBLANES)
        for h in range(A_HEADS):
            sl = slice(h * LANES, (h + 1) * LANES)
            lb = lb_ref[:, sl]
            z = z_ref[0, pl.ds(r0, SUBLANES), sl]
            q = q_ref[0, pl.ds(r0, SUBLANES), sl]
            v = v_ref[0, pl.ds(r0, SUBLANES), sl]
            f = lb + (1.0 - lb) * jax.nn.sigmoid(z)
            fm = jnp.maximum(f, F_MIN)
            kk = 1.0 - f
            pp = fm
            for j in (1, 2, 4):
                pp = pp * jnp.where(pos >= j, earlier(pp, j), 1.0)
            qq = jnp.where(pos <= SUBLANES - 2, later(fm, 1), 1.0)
            for j in (1, 2, 4):
                qq = qq * jnp.where(pos <= SUBLANES - 1 - j, later(qq, j), 1.0)
            dec = pp[0:1] if reverse else pp[SUBLANES - 1:SUBLANES]
            st = st_scr[h]
            o = lax.dot_general((q * pp).astype(BF16), st.astype(BF16), NT_DIMS,
                                preferred_element_type=F32)
            g = fm
            for d in range(SUBLANES):
                if d == 0:
                    e = q * kk
                else:
                    if d > 1:
                        g = g * earlier(fm, d - 1)
                    e = q * g * jnp.where(pos >= d, earlier(kk, d), 0.0)
                o = o + jnp.sum(e, axis=-1, keepdims=True) * earlier(v, d)
            o_ref[0, pl.ds(r0, SUBLANES), sl] = o
            u = lax.dot_general(v.astype(BF16), (kk * qq).astype(BF16), TN_DIMS,
                                preferred_element_type=F32)
            st_scr[h] = dec * st + u
        return carry

    lax.fori_loop(0, n_tiles, tile, 0)

    @pl.when(step == pl.num_programs(1) - 1)
    def _():
        sfin_ref[0] = st_scr[...]


def _hgrn_scan(proj, lb, s0, gate_blk, reverse):
    b, l, _ = proj.shape
    d = A_HEADS * LANES
    t = _row_tile(l, 512)
    nb = l // t
    seq = (lambda s: nb - 1 - s) if reverse else (lambda s: s)
    body = functools.partial(_hgrn_scan_body, reverse=reverse, n_tiles=t // SUBLANES)
    st_shape = (b, A_HEADS, LANES, LANES)
    return pl.pallas_call(
        body,
        grid=(b, nb),
        in_specs=[pl.BlockSpec((1, t, d), lambda i, s: (i, seq(s), 0)),
                  pl.BlockSpec((1, t, d), lambda i, s: (i, seq(s), 1)),
                  pl.BlockSpec((1, t, d), lambda i, s: (i, seq(s), gate_blk)),
                  pl.BlockSpec((1, d), lambda i, s: (0, 0)),
                  pl.BlockSpec((1,) + st_shape[1:], lambda i, s: (i, 0, 0, 0))],
        out_specs=[pl.BlockSpec((1, t, d), lambda i, s: (i, seq(s), 0)),
                   pl.BlockSpec((1,) + st_shape[1:], lambda i, s: (i, 0, 0, 0))],
        out_shape=[jax.ShapeDtypeStruct((b, l, d), F32), jax.ShapeDtypeStruct(st_shape, F32)],
        scratch_shapes=[pltpu.VMEM(st_shape[1:], F32)],
        compiler_params=_params("parallel", "arbitrary"),
        name="hgrn_scan_bwd" if reverse else "hgrn_scan_fwd",
    )(proj, proj, proj, lb, s0)


def _hgrn2_mixer(x_c, x_l, ng, mod_c, mod_l, w_in, lower, out_g, w_out, g1, want_ctx):
    d = x_l.shape[-1]
    p_c = _norm_mod_proj(x_c, ng, mod_c[0], mod_c[1], w_in, F32)
    p_l = _norm_mod_proj(x_l, ng, mod_l[0], mod_l[1], w_in, F32)
    lb = lower.reshape(1, d)
    zero = jnp.zeros((x_l.shape[0], A_HEADS, LANES, LANES), F32)
    o_c, o_l = [], []
    for gate_blk, rev in ((3, False), (4, True)):
        oc, s_ctx = _hgrn_scan(p_c, lb, zero, gate_blk, rev)
        ol, _ = _hgrn_scan(p_l, lb, s_ctx, gate_blk, rev)
        o_c.append(oc)
        o_l.append(ol)
    og = out_g.reshape(1, d)
    new_l = _out_call(_out_hgrn_body, [(o_l[0], 0, d), (o_l[1], 0, d), (p_l, 2, d)], [og, w_out],
                      x_l, mod_l[2], g1, "hgrn_out")
    new_c = None
    if want_ctx:
        new_c = _out_call(_out_hgrn_body, [(o_c[0], 0, d), (o_c[1], 0, d), (p_c, 2, d)], [og, w_out],
                          x_c, mod_c[2], g1, "hgrn_out")
    return new_c, new_l


def _softmax_pv(s_list, v_list):
    mx = functools.reduce(jnp.maximum, [jnp.max(s, axis=-1, keepdims=True) for s in s_list])
    ps = [jnp.exp(s - mx) for s in s_list]
    den = functools.reduce(jnp.add, [jnp.sum(p, axis=-1, keepdims=True) for p in ps])
    acc = functools.reduce(jnp.add, [jnp.dot(p.astype(BF16), v, preferred_element_type=F32)
                                     for p, v in zip(ps, v_list)])
    return acc / den


def _na_lat_body(q_ref, k_ref, v_ref, kc_ref, vc_ref, t2_ref, o_ref, *, rows, kh, scale):
    w = GRID_W
    first = lax.broadcasted_iota(jnp.int32, (w, LANES), 1) < (LANES // 2)
    kc = kc_ref[0]
    vc = vc_ref[0]

    def row(r, carry):
        r0 = jnp.clip(r - kh // 2, 0, rows - kh)
        jb = r0 - r + (t2_ref.shape[1] // 2)
        q2 = q_ref[0, pl.ds(pl.multiple_of(r * w, w), w), :] * scale
        ks = pl.multiple_of(r0 * w, w)
        k2 = k_ref[0, pl.ds(ks, kh * w), :]
        v2 = v_ref[0, pl.ds(ks, kh * w), :]
        outs = []
        for hh in range(2):
            qm = jnp.where(first if hh == 0 else jnp.logical_not(first), q2, jnp.zeros_like(q2))
            bias = jnp.concatenate([t2_ref[hh, jb + 2 * m] for m in range(kh // 2)], axis=-1)
            s_w = lax.dot_general(qm, k2, NT_DIMS, preferred_element_type=F32) + bias
            s_c = lax.dot_general(qm, kc, NT_DIMS, preferred_element_type=F32)
            outs.append(_softmax_pv([s_w, s_c], [v2, vc]))
        o_ref[0, pl.ds(pl.multiple_of(r * w, w), w), :] = jnp.where(first, outs[0], outs[1]).astype(o_ref.dtype)
        return carry

    lax.fori_loop(0, rows, row, 0)


def _na_ctx_body(q_ref, k_ref, v_ref, o_ref, *, scale):
    n = q_ref.shape[1]
    first = lax.broadcasted_iota(jnp.int32, (n, LANES), 1) < (LANES // 2)
    q2 = q_ref[0] * scale
    k2 = k_ref[0]
    v2 = v_ref[0]
    outs = []
    for hh in range(2):
        qm = jnp.where(first if hh == 0 else jnp.logical_not(first), q2, jnp.zeros_like(q2))
        s = lax.dot_general(qm, k2, NT_DIMS, preferred_element_type=F32)
        outs.append(_softmax_pv([s], [v2]))
    o_ref[0] = jnp.where(first, outs[0], outs[1]).astype(o_ref.dtype)


def _na_bias_table(rpb, kh, kw):
    w = jnp.arange(GRID_W)[:, None]
    c = jnp.arange(GRID_W)[None, :]
    c0 = jnp.clip(w - kw // 2, 0, GRID_W - kw)
    inside = (c >= c0) & (c < c0 + kw)
    cb = jnp.clip(c - w + kw - 1, 0, 2 * kw - 2)
    t = jnp.where(inside[None, None], rpb.astype(F32)[:, :, cb], NEG_BIG)
    return jnp.concatenate([t[:, :-1], t[:, 1:]], axis=-1)


def _na_mixer(x_c, x_l, ng, mod_c, mod_l, w_qkv, rpb, w_out, g1, want_ctx):
    b, l, d = x_l.shape
    lc = x_c.shape[1]
    heads = rpb.shape[0]
    na_rows, na_cols = (rpb.shape[1] + 1) // 2, (rpb.shape[2] + 1) // 2
    dh = d // heads
    assert 2 * dh == LANES and l % GRID_W == 0
    rows = l // GRID_W
    kh = min(na_rows, rows)
    assert kh % 2 == 0 and kh == na_rows
    scale = dh ** -0.5
    nhp = heads // 2
    qkv_c = _norm_mod_proj(x_c, ng, mod_c[0], mod_c[1], w_qkv, BF16)
    qkv_l = _norm_mod_proj(x_l, ng, mod_l[0], mod_l[1], w_qkv, BF16)
    t2 = _na_bias_table(rpb, kh, na_cols)
    o_l = pl.pallas_call(
        functools.partial(_na_lat_body, rows=rows, kh=kh, scale=scale),
        grid=(b, nhp),
        in_specs=[pl.BlockSpec((1, l, LANES), lambda i, p: (i, 0, p)),
                  pl.BlockSpec((1, l, LANES), lambda i, p: (i, 0, nhp + p)),
                  pl.BlockSpec((1, l, LANES), lambda i, p: (i, 0, 2 * nhp + p)),
                  pl.BlockSpec((1, lc, LANES), lambda i, p: (i, 0, nhp + p)),
                  pl.BlockSpec((1, lc, LANES), lambda i, p: (i, 0, 2 * nhp + p)),
                  pl.BlockSpec((2,) + t2.shape[1:], lambda i, p: (p, 0, 0, 0))],
        out_specs=pl.BlockSpec((1, l, LANES), lambda i, p: (i, 0, p)),
        out_shape=jax.ShapeDtypeStruct((b, l, d), BF16),
        compiler_params=_params("parallel", "parallel"),
        name="na_latent",
    )(qkv_l, qkv_l, qkv_l, qkv_c, qkv_c, t2)
    new_l = _out_call(_out_plain_body, [(o_l, 0, d)], [w_out], x_l, mod_l[2], g1, "na_out")
    new_c = None
    if want_ctx:
        o_c = pl.pallas_call(
            functools.partial(_na_ctx_body, scale=scale),
            grid=(b, nhp),
            in_specs=[pl.BlockSpec((1, lc, LANES), lambda i, p: (i, 0, p)),
                      pl.BlockSpec((1, lc, LANES), lambda i, p: (i, 0, nhp + p)),
                      pl.BlockSpec((1, lc, LANES), lambda i, p: (i, 0, 2 * nhp + p))],
            out_specs=pl.BlockSpec((1, lc, LANES), lambda i, p: (i, 0, p)),
            out_shape=jax.ShapeDtypeStruct((b, lc, d), BF16),
            compiler_params=_params("parallel", "parallel"),
            name="na_context",
        )(qkv_c, qkv_c, qkv_c)
        new_c = _out_call(_out_plain_body, [(o_c, 0, d)], [w_out], x_c, mod_c[2], g1, "na_out")
    return new_c, new_l


S5_GROUP_BLOCK = 8


def _s5_kernel_body(c_ref, w_ref, o_ref):
    for i in range(c_ref.shape[0]):
        o_ref[i] = jnp.dot(c_ref[i], w_ref[i], preferred_element_type=F32, precision=lax.Precision.HIGHEST)


def _s5_impulse(cmat, wmat):
    n = cmat.shape[0]
    gb = S5_GROUP_BLOCK
    return pl.pallas_call(
        _s5_kernel_body,
        grid=(n // gb,),
        in_specs=[pl.BlockSpec((gb,) + cmat.shape[1:], lambda i: (i, 0, 0)),
                  pl.BlockSpec((gb,) + wmat.shape[1:], lambda i: (i, 0, 0))],
        out_specs=pl.BlockSpec((gb, cmat.shape[1], wmat.shape[2]), lambda i: (i, 0, 0)),
        out_shape=jax.ShapeDtypeStruct((n, cmat.shape[1], wmat.shape[2]), F32),
        compiler_params=_params("parallel"),
        name="s5_impulse",
    )(cmat, wmat)


def _s5_operators(lam_re, lam_im, log_dt, b_re, b_im, c_re, c_im, t):
    _, g, p = lam_re.shape
    cg = b_re.shape[-1]
    lam = lax.complex(lam_re.astype(F32), lam_im.astype(F32))
    ldt = lam * jnp.exp(log_dt.astype(F32))[..., None]
    a = jnp.exp(ldt)
    bbar = ((a - 1.0) / lam)[..., None] * lax.complex(b_re.astype(F32), b_im.astype(F32))
    cm = lax.complex(c_re.astype(F32), c_im.astype(F32))
    apow = jnp.exp(ldt[..., None] * jnp.arange(t + 1, dtype=F32))
    w = apow[..., :t, None] * bbar[:, :, :, None, :]
    wmat = jnp.concatenate([jnp.real(w), jnp.imag(w)], axis=2).reshape(2 * g, 2 * p, t * cg)
    cmat = jnp.concatenate([jnp.real(cm), -jnp.imag(cm)], axis=-1).reshape(2 * g, cg, 2 * p)
    k = _s5_impulse(cmat, wmat).reshape(2, g, cg, t, cg)
    k = jnp.transpose(k, (0, 1, 3, 4, 2))
    kfull = jnp.concatenate([k[1, :, :0:-1], (k[0, :, :1] + k[1, :, :1]), k[0, :, 1:]], axis=1)
    idx = jnp.arange(t)[None, :] - jnp.arange(t)[:, None] + (t - 1)
    toep = jnp.transpose(kfull[:, idx], (0, 1, 3, 2, 4)).reshape(g, t * cg, t * cg)

    def state_in(wd, flip):
        wd = wd[:, :, ::-1] if flip else wd
        m = jnp.transpose(wd, (0, 2, 3, 1)).reshape(g, t * cg, p)
        return jnp.concatenate([jnp.real(m), jnp.imag(m)], axis=-1)

    def state_out(cd, pw):
        n = cd[:, :, :, None] * pw[:, None, :, :]
        n = jnp.transpose(n, (0, 2, 3, 1)).reshape(g, p, t * cg)
        return jnp.concatenate([jnp.real(n), -jnp.imag(n)], axis=1)

    m_f = state_in(w[0], True)
    m_b = state_in(w[1], False)
    n_f = state_out(cm[0], apow[0][..., 1:])
    n_b = state_out(cm[1], apow[1][..., :0:-1])
    at = apow[..., t]
    a1 = jnp.concatenate([jnp.real(at), jnp.real(at)], axis=-1)[:, :, None, :]
    a2 = jnp.concatenate([-jnp.imag(at), jnp.imag(at)], axis=-1)[:, :, None, :]
    bf = lambda m: m.astype(BF16)
    return dict(toep=bf(toep), m_f=bf(m_f), m_b=bf(m_b), n_f=bf(n_f), n_b=bf(n_b), a1=a1, a2=a2)


def _s5_body(u_ref, dsk_ref, toep_ref, mf_ref, mb_ref, nf_ref, nb_ref, a1_ref, a2_ref, x0_ref,
             y_ref, xfin_ref, z_scr, xin_scr):
    gb, nc = u_ref.shape[1], u_ref.shape[2]
    half = LANES // 2
    for gi in range(gb):
        ub = u_ref[0, gi].astype(BF16)
        z_scr[0, :, gi, :] = jnp.dot(ub, mf_ref[gi], preferred_element_type=F32)
        z_scr[1, :, gi, :] = jnp.dot(ub, mb_ref[gi], preferred_element_type=F32)

    def scan(j, xs):
        out = []
        for dr in range(2):
            jj = j if dr == 0 else nc - 1 - j
            x = xs[dr]
            xin_scr[dr, jj] = x
            out.append(a1_ref[dr, :, 0, :] * x + a2_ref[dr, :, 0, :] * pltpu.roll(x, half, 1) + z_scr[dr, jj])
        return tuple(out)

    xf, xb = lax.fori_loop(0, nc, scan, (x0_ref[0, 0], x0_ref[0, 1]))
    xfin_ref[0, 0] = xf
    xfin_ref[0, 1] = xb
    for gi in range(gb):
        u = u_ref[0, gi]
        y = jnp.dot(u.astype(BF16), toep_ref[gi], preferred_element_type=F32) + u * dsk_ref[gi]
        y = y + jnp.dot(xin_scr[0, :, gi, :].astype(BF16), nf_ref[gi], preferred_element_type=F32)
        y = y + jnp.dot(xin_scr[1, :, gi, :].astype(BF16), nb_ref[gi], preferred_element_type=F32)
        y_ref[0, gi] = y


def _s5_scan(h, ops, dsk, x0):
    b, l, d = h.shape
    g = ops["toep"].shape[0]
    cg = d // g
    t = S5_CHUNK
    nc = l // t
    tc = t * cg
    gb = S5_GROUP_BLOCK
    u = jnp.transpose(h.reshape(b, nc, t, g, cg), (0, 3, 1, 2, 4)).reshape(b, g, nc, tc)
    per_g = lambda shape: pl.BlockSpec((gb,) + shape, lambda j, i: (j,) + (0,) * len(shape))
    y, xfin = pl.pallas_call(
        _s5_body,
        grid=(g // gb, b),
        in_specs=[pl.BlockSpec((1, gb, nc, tc), lambda j, i: (i, j, 0, 0)),
                  per_g((1, tc)), per_g((tc, tc)), per_g((tc, LANES)), per_g((tc, LANES)),
                  per_g((LANES, tc)), per_g((LANES, tc)),
                  pl.BlockSpec((2, gb, 1, LANES), lambda j, i: (0, j, 0, 0)),
                  pl.BlockSpec((2, gb, 1, LANES), lambda j, i: (0, j, 0, 0)),
                  pl.BlockSpec((1, 2, gb, LANES), lambda j, i: (i, 0, j, 0))],
        out_specs=[pl.BlockSpec((1, gb, nc, tc), lambda j, i: (i, j, 0, 0)),
                   pl.BlockSpec((1, 2, gb, LANES), lambda j, i: (i, 0, j, 0))],
        out_shape=[jax.ShapeDtypeStruct((b, g, nc, tc), F32), jax.ShapeDtypeStruct((b, 2, g, LANES), F32)],
        scratch_shapes=[pltpu.VMEM((2, nc, gb, LANES), F32), pltpu.VMEM((2, nc, gb, LANES), F32)],
        compiler_params=_params("parallel", "parallel"),
        name="s5_scan",
    )(u, dsk, ops["toep"], ops["m_f"], ops["m_b"], ops["n_f"], ops["n_b"], ops["a1"], ops["a2"], x0)
    y = jnp.transpose(y.reshape(b, g, nc, t, cg), (0, 2, 3, 1, 4)).reshape(b, l, d)
    return y, xfin


def _s5_mixer(x_c, x_l, ng, mod_c, mod_l, lam_re, lam_im, log_dt, b_re, b_im, c_re, c_im, d_skip, w_glu, g1,
              want_ctx):
    b, l, d = x_l.shape
    g, p = lam_re.shape[1], lam_re.shape[2]
    cg = d // g
    assert 2 * p == LANES and g % S5_GROUP_BLOCK == 0
    ops = _s5_operators(lam_re, lam_im, log_dt, b_re, b_im, c_re, c_im, S5_CHUNK)
    dsk = jnp.tile(d_skip.astype(F32).reshape(g, 1, cg), (1, 1, S5_CHUNK))
    h_c = _norm_mod_only(x_c, ng, mod_c[0], mod_c[1])
    h_l = _norm_mod_only(x_l, ng, mod_l[0], mod_l[1])
    y_c, x_ctx = _s5_scan(h_c, ops, dsk, jnp.zeros((b, 2, g, LANES), F32))
    y_l, _ = _s5_scan(h_l, ops, dsk, x_ctx)
    new_l = _out_call(_out_glu_body, [(y_l, 0, d)], [w_glu], x_l, mod_l[2], g1, "s5_glu_out")
    new_c = None
    if want_ctx:
        new_c = _out_call(_out_glu_body, [(y_c, 0, d)], [w_glu], x_c, mod_c[2], g1, "s5_glu_out")
    return new_c, new_l


FFN_HALO = 16
FFN_TN = 256


def _ffn_body(xp_ref, x_ref, xn_ref, g2_ref, sh_ref, sc_ref, wa_ref, wv_ref, cwa_ref, cwv_ref, cba_ref, cbv_ref,
              wo_ref, gate_ref, g3_ref, o_ref, h_scr, acc_scr, *, per):
    i, j = pl.program_id(0), pl.program_id(1)
    tm = x_ref.shape[0]
    hl = FFN_HALO

    @pl.when(j == 0)
    def _():
        g, sh, sc = g2_ref[...], sh_ref[0], sc_ref[0]
        keep_p = ((i % per) != 0).astype(F32)
        keep_n = ((i % per) != per - 1).astype(F32)
        h_scr[0:hl] = (_norm_mod(xp_ref[...], g, sh, sc) * keep_p).astype(BF16)
        h_scr[hl:hl + tm] = _norm_mod(x_ref[...], g, sh, sc).astype(BF16)
        h_scr[hl + tm:] = (_norm_mod(xn_ref[...], g, sh, sc) * keep_n).astype(BF16)

    h = h_scr[...]
    rows = tm + 2 * hl

    def conv(w_ref, cw_ref, cb_ref):
        u = jnp.dot(h, w_ref[...], preferred_element_type=F32)
        um = pltpu.roll(u, 1, 0)[hl:hl + tm]
        up = pltpu.roll(u, rows - 1, 0)[hl:hl + tm]
        return cb_ref[...] + um * cw_ref[0:1] + u[hl:hl + tm] * cw_ref[1:2] + up * cw_ref[2:3]

    a = conv(wa_ref, cwa_ref, cba_ref)
    v = conv(wv_ref, cwv_ref, cbv_ref)
    part = jnp.dot((a * jax.nn.sigmoid(a) * v).astype(BF16), wo_ref[...], preferred_element_type=F32)

    @pl.when(j == 0)
    def _():
        acc_scr[...] = part

    @pl.when(j > 0)
    def _():
        acc_scr[...] += part

    @pl.when(j == pl.num_programs(1) - 1)
    def _():
        o_ref[...] = x_ref[...] + gate_ref[0] * _rms(acc_scr[...], g3_ref[...])


def _conv_ffn_block(x, ng2, mod, w_in, conv_w, conv_b, w_out, ng3):
    b, l, d = x.shape
    f = w_out.shape[0]
    tn, hl = FFN_TN, FFN_HALO
    assert f % tn == 0 and conv_w.shape[0] == 3 and w_in.shape[1] == 2 * f
    nj = f // tn
    tm = _row_tile(l, 1024)
    per = l // tm
    hb = tm // hl
    last_hb = b * l // hl - 1
    cb = conv_b.reshape(1, 2 * f)
    x2 = x.reshape(b * l, d)
    out = pl.pallas_call(
        functools.partial(_ffn_body, per=per),
        grid=(b * per, nj),
        in_specs=[pl.BlockSpec((hl, d), lambda i, j: (jnp.maximum(i * hb - 1, 0), 0)),
                  pl.BlockSpec((tm, d), lambda i, j: (i, 0)),
                  pl.BlockSpec((hl, d), lambda i, j: (jnp.minimum((i + 1) * hb, last_hb), 0)),
                  pl.BlockSpec((1, d), lambda i, j: (0, 0)),
                  pl.BlockSpec((1, 1, d), lambda i, j: (i // per, 0, 0)),
                  pl.BlockSpec((1, 1, d), lambda i, j: (i // per, 0, 0)),
                  pl.BlockSpec((d, tn), lambda i, j: (0, j)),
                  pl.BlockSpec((d, tn), lambda i, j: (0, nj + j)),
                  pl.BlockSpec((3, tn), lambda i, j: (0, j)),
                  pl.BlockSpec((3, tn), lambda i, j: (0, nj + j)),
                  pl.BlockSpec((1, tn), lambda i, j: (0, j)),
                  pl.BlockSpec((1, tn), lambda i, j: (0, nj + j)),
                  pl.BlockSpec((tn, d), lambda i, j: (j, 0)),
                  pl.BlockSpec((1, 1, d), lambda i, j: (i // per, 0, 0)),
                  pl.BlockSpec((1, d), lambda i, j: (0, 0))],
        out_specs=pl.BlockSpec((tm, d), lambda i, j: (i, 0)),
        out_shape=jax.ShapeDtypeStruct((b * l, d), F32),
        scratch_shapes=[pltpu.VMEM((tm + 2 * hl, d), BF16), pltpu.VMEM((tm, d), F32)],
        compiler_params=_params("parallel", "arbitrary"),
        name="conv_ffn",
    )(x2, x2, x2, ng2.reshape(1, d), mod[3], mod[4], w_in, w_in, conv_w, conv_w, cb, cb, w_out, mod[5],
      ng3.reshape(1, d))
    return out.reshape(b, l, d)


def kernel(x, c, ctx, c_ctx, w_mod, b_mod, norm_g, a_w_in, a_lower_logits, a_out_g, a_w_out, b_w_qkv, b_rpb,
           b_w_out, c_lam_re, c_lam_im, c_log_dt, c_b_re, c_b_im, c_c_re, c_c_im, c_d, c_w_glu, f_w_in,
           f_conv_w, f_conv_b, f_w_out):
    bsz, _, d = x.shape
    depth = w_mod.shape[0]
    p = jax.nn.softmax(a_lower_logits.astype(F32), axis=0)
    lower = jnp.cumsum(p, axis=0) - p[0]
    n_rows = -(-(bsz + 1) // SUBLANES) * SUBLANES
    c_rows = jnp.zeros((n_rows, d), F32).at[:bsz].set(c).at[bsz].set(c_ctx)
    mods = _modulation(c_rows, w_mod, b_mod)
    wb = lambda w: w.astype(BF16)
    lat, cx = x, ctx
    for i in range(depth):
        kind, j = i % N_MIXERS, i // N_MIXERS
        last = i == depth - 1
        m = mods[i].reshape(n_rows, N_MOD, d)
        mod_l = [m[:bsz, k][:, None, :] for k in range(N_MOD)]
        mod_c = [jnp.broadcast_to(m[bsz, k][None, None, :], (bsz, 1, d)) for k in range(N_MOD)]
        ng = norm_g[i]
        if kind == 0:
            cx1, lat = _hgrn2_mixer(cx, lat, ng[0], mod_c, mod_l, wb(a_w_in[j]), lower[i], a_out_g[j],
                                    wb(a_w_out[j]), ng[1], not last)
        elif kind == 1:
            cx1, lat = _na_mixer(cx, lat, ng[0], mod_c, mod_l, wb(b_w_qkv[j]), b_rpb[j], wb(b_w_out[j]), ng[1],
                                 not last)
        else:
            cx1, lat = _s5_mixer(cx, lat, ng[0], mod_c, mod_l, c_lam_re[j], c_lam_im[j], c_log_dt[j], c_b_re[j],
                                 c_b_im[j], c_c_re[j], c_c_im[j], c_d[j], wb(c_w_glu[j]), ng[1], not last)
        lat = _conv_ffn_block(lat, ng[2], mod_l, wb(f_w_in[i]), f_conv_w[i], f_conv_b[i], wb(f_w_out[i]), ng[3])
        if not last:
            cx = _conv_ffn_block(cx1, ng[2], mod_c, wb(f_w_in[i]), f_conv_w[i], f_conv_b[i], wb(f_w_out[i]), ng[3])
    return lat
```

```python
import functools

import jax
import jax.numpy as jnp
from jax import lax
from jax.experimental import pallas as pl
from jax.experimental.pallas import tpu as pltpu

F32 = jnp.float32
BF16 = jnp.bfloat16

EPS = 1e-6
F_MIN = 1e-30
N_MOD = 6
N_MIXERS = 3
A_HEADS = 8
GRID_W = 64
S5_CHUNK = 32
NEG_BIG = -1e30
NA_ROW_UNROLL = 4
SUBLANES = 8
LANES = 128
VMEM_LIMIT_BYTES = 48 * 1024 * 1024

NT_DIMS = (((1,), (1,)), ((), ()))
TN_DIMS = (((0,), (0,)), ((), ()))


def _params(*sem):
    return pltpu.CompilerParams(dimension_semantics=sem, vmem_limit_bytes=VMEM_LIMIT_BYTES)


def _row_tile(n, want):
    t = min(n, want)
    assert n % t == 0, (n, t)
    return t


def _rms(y, g):
    return y * lax.rsqrt(jnp.mean(y * y, axis=-1, keepdims=True) + EPS) * g


def _mod_body(c_ref, w_ref, b_ref, o_ref):
    c = c_ref[...]
    s = (c * jax.nn.sigmoid(c)).astype(BF16)
    o_ref[0] = jnp.dot(s, w_ref[0].astype(BF16), preferred_element_type=F32) + b_ref[0]


def _modulation(c_rows, w_mod, b_mod):
    depth, d, n = w_mod.shape
    r = c_rows.shape[0]
    tn = n // 4
    return pl.pallas_call(
        _mod_body,
        grid=(depth, n // tn),
        in_specs=[pl.BlockSpec((r, d), lambda i, j: (0, 0)),
                  pl.BlockSpec((1, d, tn), lambda i, j: (i, 0, j)),
                  pl.BlockSpec((1, 1, tn), lambda i, j: (i, 0, j))],
        out_specs=pl.BlockSpec((1, r, tn), lambda i, j: (i, 0, j)),
        out_shape=jax.ShapeDtypeStruct((depth, r, n), F32),
        compiler_params=_params("parallel", "parallel"),
        name="adaln_mod",
    )(c_rows, w_mod, b_mod.reshape(depth, 1, n))


def _norm_mod(x, g, sh, sc):
    return _rms(x, g) * (1.0 + sc) + sh


def _proj_body(x_ref, g_ref, sh_ref, sc_ref, w_ref, o_ref, h_scr):
    @pl.when(pl.program_id(1) == 0)
    def _():
        h_scr[...] = _norm_mod(x_ref[...], g_ref[...], sh_ref[0], sc_ref[0]).astype(BF16)

    o_ref[...] = jnp.dot(h_scr[...], w_ref[...], preferred_element_type=F32).astype(o_ref.dtype)


def _norm_mod_proj(x, g, sh, sc, w, out_dtype):
    b, l, d = x.shape
    n = w.shape[1]
    tm = _row_tile(l, 1024)
    tn = _row_tile(n, 512)
    per = l // tm
    out = pl.pallas_call(
        _proj_body,
        grid=(b * per, n // tn),
        in_specs=[pl.BlockSpec((tm, d), lambda i, j: (i, 0)),
                  pl.BlockSpec((1, d), lambda i, j: (0, 0)),
                  pl.BlockSpec((1, 1, d), lambda i, j: (i // per, 0, 0)),
                  pl.BlockSpec((1, 1, d), lambda i, j: (i // per, 0, 0)),
                  pl.BlockSpec((d, tn), lambda i, j: (0, j))],
        out_specs=pl.BlockSpec((tm, tn), lambda i, j: (i, j)),
        out_shape=jax.ShapeDtypeStruct((b * l, n), out_dtype),
        scratch_shapes=[pltpu.VMEM((tm, d), BF16)],
        compiler_params=_params("parallel", "arbitrary"),
        name="norm_mod_proj",
    )(x.reshape(b * l, d), g.reshape(1, d), sh, sc, w)
    return out.reshape(b, l, n)


def _norm_only_body(x_ref, g_ref, sh_ref, sc_ref, o_ref):
    o_ref[...] = _norm_mod(x_ref[...], g_ref[...], sh_ref[0], sc_ref[0])


def _norm_mod_only(x, g, sh, sc):
    b, l, d = x.shape
    tm = _row_tile(l, 512)
    per = l // tm
    out = pl.pallas_call(
        _norm_only_body,
        grid=(b * per,),
        in_specs=[pl.BlockSpec((tm, d), lambda i: (i, 0)),
                  pl.BlockSpec((1, d), lambda i: (0, 0)),
                  pl.BlockSpec((1, 1, d), lambda i: (i // per, 0, 0)),
                  pl.BlockSpec((1, 1, d), lambda i: (i // per, 0, 0))],
        out_specs=pl.BlockSpec((tm, d), lambda i: (i, 0)),
        out_shape=jax.ShapeDtypeStruct((b * l, d), F32),
        compiler_params=_params("parallel"),
        name="norm_mod",
    )(x.reshape(b * l, d), g.reshape(1, d), sh, sc)
    return out.reshape(b, l, d)


def _residual(res_ref, gate_ref, g1_ref, y):
    return res_ref[...] + gate_ref[0] * _rms(y, g1_ref[...])


def _out_plain_body(a_ref, w_ref, res_ref, gate_ref, g1_ref, o_ref):
    y = jnp.dot(a_ref[...], w_ref[...], preferred_element_type=F32)
    o_ref[...] = _residual(res_ref, gate_ref, g1_ref, y)


def _out_hgrn_body(of_ref, ob_ref, gp_ref, og_ref, w_ref, res_ref, gate_ref, g1_ref, o_ref):
    o = of_ref[...] + ob_ref[...]
    parts = []
    for h in range(A_HEADS):
        oh = o[:, h * LANES:(h + 1) * LANES]
        parts.append(oh * lax.rsqrt(jnp.mean(oh * oh, axis=-1, keepdims=True) + EPS))
    gp = gp_ref[...]
    a = jnp.concatenate(parts, axis=-1) * og_ref[...] * (gp * jax.nn.sigmoid(gp))
    y = jnp.dot(a.astype(BF16), w_ref[...], preferred_element_type=F32)
    o_ref[...] = _residual(res_ref, gate_ref, g1_ref, y)


def _gelu_tanh(y):
    return 0.5 * y * (1.0 + jnp.tanh(0.7978845608028654 * (y + 0.044715 * (y * y * y))))


def _out_glu_body(y_ref, w_ref, res_ref, gate_ref, g1_ref, o_ref):
    d = o_ref.shape[-1]
    a = _gelu_tanh(y_ref[...]).astype(BF16)
    ag = jnp.dot(a, w_ref[...], preferred_element_type=F32)
    y = ag[:, :d] * jax.nn.sigmoid(ag[:, d:])
    o_ref[...] = _residual(res_ref, gate_ref, g1_ref, y)


def _out_call(body, row_inputs, const_inputs, res, gate, g1, name):
    b, l, d = res.shape
    tm = _row_tile(l, 256)
    per = l // tm
    in_specs, args = [], []
    for arr, blk, width in row_inputs:
        in_specs.append(pl.BlockSpec((tm, width), lambda i, blk=blk: (i, blk)))
        args.append(arr.reshape(b * l, arr.shape[-1]))
    for arr in const_inputs:
        in_specs.append(pl.BlockSpec(arr.shape, lambda i, nd=arr.ndim: (0,) * nd))
        args.append(arr)
    in_specs += [pl.BlockSpec((tm, d), lambda i: (i, 0)),
                 pl.BlockSpec((1, 1, d), lambda i: (i // per, 0, 0)),
                 pl.BlockSpec((1, d), lambda i: (0, 0))]
    args += [res.reshape(b * l, d), gate, g1.reshape(1, d)]
    out = pl.pallas_call(
        body,
        grid=(b * per,),
        in_specs=in_specs,
        out_specs=pl.BlockSpec((tm, d), lambda i: (i, 0)),
        out_shape=jax.ShapeDtypeStruct((b * l, d), F32),
        compiler_params=_params("parallel"),
        name=name,
    )(*args)
    return out.reshape(b, l, d)


HGRN_CHUNK = 32
HGRN_SAFE_MIN = 1e-35


def _hgrn_gates(z, lb):
    f = lb + (1.0 - lb) * jax.nn.sigmoid(z)
    return jnp.maximum(f, F_MIN), 1.0 - f


def _hgrn_chunk(q, v, z, lb, st, reverse):
    c = q.shape[0]
    fm, kk = _hgrn_gates(z, lb)
    pos = lax.broadcasted_iota(jnp.int32, (c, LANES), 0)
    if reverse:
        pos = (c - 1) - pos
    pp = fm
    j = 1
    while j < c:
        pp = pp * jnp.where(pos >= j, pltpu.roll(pp, (c - j) if reverse else j, 0), 1.0)
        j *= 2
    p_last = pp[0:1] if reverse else pp[c - 1:c]
    qd = (q * pp).astype(BF16)
    kd = kk * (1.0 / pp)
    sc = lax.dot_general(qd, kd.astype(BF16), NT_DIMS, preferred_element_type=F32)
    row = lax.broadcasted_iota(jnp.int32, (c, c), 0)
    col = lax.broadcasted_iota(jnp.int32, (c, c), 1)
    sc = jnp.where((col >= row) if reverse else (col <= row), sc, 0.0)
    vb = v.astype(BF16)
    o = jnp.dot(sc.astype(BF16), vb, preferred_element_type=F32)
    o = o + lax.dot_general(qd, st.astype(BF16), NT_DIMS, preferred_element_type=F32)
    u = lax.dot_general(vb, (kd * p_last).astype(BF16), TN_DIMS, preferred_element_type=F32)
    return o, p_last * st + u, p_last


def _hgrn_exact_block(q_ref, v_ref, z_ref, lb_ref, o_ref, st_scr, dr, reverse, n_tiles):
    pos = lax.broadcasted_iota(jnp.int32, (SUBLANES, LANES), 0)
    if reverse:
        pos = (SUBLANES - 1) - pos

    def earlier(x, j):
        if j == 0:
            return x
        return pltpu.roll(x, (SUBLANES - j) if reverse else j, 0)

    def later(x, j):
        return pltpu.roll(x, j if reverse else (SUBLANES - j), 0)

    def tile(i, carry):
        ti = (n_tiles - 1 - i) if reverse else i
        r0 = pl.multiple_of(ti * SUBLANES, SUBLANES)
        for h in range(A_HEADS):
            sl = slice(h * LANES, (h + 1) * LANES)
            lb = lb_ref[:, sl]
            z = z_ref[0, pl.ds(r0, SUBLANES), sl]
            q = q_ref[0, pl.ds(r0, SUBLANES), sl]
            v = v_ref[0, pl.ds(r0, SUBLANES), sl]
            fm, kk = _hgrn_gates(z, lb)
            pp = fm
            for j in (1, 2, 4):
                pp = pp * jnp.where(pos >= j, earlier(pp, j), 1.0)
            qq = jnp.where(pos <= SUBLANES - 2, later(fm, 1), 1.0)
            for j in (1, 2, 4):
                qq = qq * jnp.where(pos <= SUBLANES - 1 - j, later(qq, j), 1.0)
            dec = pp[0:1] if reverse else pp[SUBLANES - 1:SUBLANES]
            st = st_scr[dr, h]
            o = lax.dot_general((q * pp).astype(BF16), st.astype(BF16), NT_DIMS,
                                preferred_element_type=F32)
            g = fm
            for d in range(SUBLANES):
                if d == 0:
                    e = q * kk
                else:
                    if d > 1:
                        g = g * earlier(fm, d - 1)
                    e = q * g * jnp.where(pos >= d, earlier(kk, d), 0.0)
                o = o + jnp.sum(e, axis=-1, keepdims=True) * earlier(v, d)
            o_ref[0, pl.ds(r0, SUBLANES), sl] = o
            u = lax.dot_general(v.astype(BF16), (kk * qq).astype(BF16), TN_DIMS,
                                preferred_element_type=F32)
            st_scr[dr, h] = dec * st + u
        return carry

    lax.fori_loop(0, n_tiles, tile, 0)


def _hgrn_scan_body(qf_ref, vf_ref, zf_ref, qb_ref, vb_ref, zb_ref, lb_ref, s0_ref, of_ref, ob_ref, sfin_ref,
                    st_scr, save_scr, *, n_chunks):
    step = pl.program_id(1)

    @pl.when(step == 0)
    def _():
        st_scr[...] = s0_ref[0]

    save_scr[...] = st_scr[...]
    dirs = ((qf_ref, vf_ref, zf_ref, of_ref, False), (qb_ref, vb_ref, zb_ref, ob_ref, True))
    c = HGRN_CHUNK

    def chunk(i, p_min):
        for dr, (q_ref, v_ref, z_ref, o_ref, reverse) in enumerate(dirs):
            r0 = pl.multiple_of(((n_chunks - 1 - i) if reverse else i) * c, c)
            for h in range(A_HEADS):
                sl = slice(h * LANES, (h + 1) * LANES)
                o, st_new, p_last = _hgrn_chunk(q_ref[0, pl.ds(r0, c), sl], v_ref[0, pl.ds(r0, c), sl],
                                                z_ref[0, pl.ds(r0, c), sl], lb_ref[:, sl], st_scr[dr, h], reverse)
                o_ref[0, pl.ds(r0, c), sl] = o
                st_scr[dr, h] = st_new
                p_min = jnp.minimum(p_min, p_last)
        return p_min

    p_min = lax.fori_loop(0, n_chunks, chunk, jnp.ones((1, LANES), F32))

    @pl.when(jnp.logical_not(jnp.min(p_min) >= HGRN_SAFE_MIN))
    def _():
        st_scr[...] = save_scr[...]
        for dr, (q_ref, v_ref, z_ref, o_ref, reverse) in enumerate(dirs):
            _hgrn_exact_block(q_ref, v_ref, z_ref, lb_ref, o_ref, st_scr, dr, reverse, n_chunks * c // SUBLANES)

    @pl.when(step == pl.num_programs(1) - 1)
    def _():
        sfin_ref[0] = st_scr[...]


def _hgrn_scan(proj, lb, s0):
    b, l, _ = proj.shape
    d = A_HEADS * LANES
    t = _row_tile(l, 256)
    nb = l // t
    fwd = lambda blk: pl.BlockSpec((1, t, d), lambda i, s: (i, s, blk))
    bwd = lambda blk: pl.BlockSpec((1, t, d), lambda i, s: (i, nb - 1 - s, blk))
    st_spec = pl.BlockSpec((1,) + s0.shape[1:], lambda i, s: (i, 0, 0, 0, 0))
    return pl.pallas_call(
        functools.partial(_hgrn_scan_body, n_chunks=t // HGRN_CHUNK),
        grid=(b, nb),
        in_specs=[fwd(0), fwd(1), fwd(3), bwd(0), bwd(1), bwd(4),
                  pl.BlockSpec((1, d), lambda i, s: (0, 0)), st_spec],
        out_specs=[fwd(0), bwd(0), st_spec],
        out_shape=[jax.ShapeDtypeStruct((b, l, d), F32), jax.ShapeDtypeStruct((b, l, d), F32),
                   jax.ShapeDtypeStruct(s0.shape, F32)],
        scratch_shapes=[pltpu.VMEM(s0.shape[1:], F32), pltpu.VMEM(s0.shape[1:], F32)],
        compiler_params=_params("parallel", "arbitrary"),
        name="hgrn_scan",
    )(proj, proj, proj, proj, proj, proj, lb, s0)


def _hgrn2_mixer(x_c, x_l, ng, mod_c, mod_l, w_in, lower, out_g, w_out, g1, want_ctx):
    d = x_l.shape[-1]
    p_c = _norm_mod_proj(x_c, ng, mod_c[0], mod_c[1], w_in, F32)
    p_l = _norm_mod_proj(x_l, ng, mod_l[0], mod_l[1], w_in, F32)
    lb = lower.reshape(1, d)
    zero = jnp.zeros((x_l.shape[0], 2, A_HEADS, LANES, LANES), F32)
    ocf, ocb, s_ctx = _hgrn_scan(p_c, lb, zero)
    olf, olb, _ = _hgrn_scan(p_l, lb, s_ctx)
    o_c, o_l = (ocf, ocb), (olf, olb)
    og = out_g.reshape(1, d)
    new_l = _out_call(_out_hgrn_body, [(o_l[0], 0, d), (o_l[1], 0, d), (p_l, 2, d)], [og, w_out],
                      x_l, mod_l[2], g1, "hgrn_out")
    new_c = None
    if want_ctx:
        new_c = _out_call(_out_hgrn_body, [(o_c[0], 0, d), (o_c[1], 0, d), (p_c, 2, d)], [og, w_out],
                          x_c, mod_c[2], g1, "hgrn_out")
    return new_c, new_l


def _softmax_pv(s_list, v_list):
    mx = functools.reduce(jnp.maximum, [jnp.max(s, axis=-1, keepdims=True) for s in s_list])
    ps = [jnp.exp(s - mx) for s in s_list]
    den = functools.reduce(jnp.add, [jnp.sum(p, axis=-1, keepdims=True) for p in ps])
    acc = functools.reduce(jnp.add, [jnp.dot(p.astype(BF16), v, preferred_element_type=F32)
                                     for p, v in zip(ps, v_list)])
    return acc / den


def _na_lat_body(q_ref, k_ref, v_ref, kc_ref, vc_ref, t2_ref, o_ref, *, rows, kh, scale):
    w = GRID_W
    first = lax.broadcasted_iota(jnp.int32, (w, LANES), 1) < (LANES // 2)
    kc = kc_ref[0]
    vc = vc_ref[0]

    def row(r, carry):
        r0 = jnp.clip(r - kh // 2, 0, rows - kh)
        jb = r0 - r + (t2_ref.shape[1] // 2)
        q2 = q_ref[0, pl.ds(pl.multiple_of(r * w, w), w), :] * scale
        ks = pl.multiple_of(r0 * w, w)
        k2 = k_ref[0, pl.ds(ks, kh * w), :]
        v2 = v_ref[0, pl.ds(ks, kh * w), :]
        outs = []
        for hh in range(2):
            qm = jnp.where(first if hh == 0 else jnp.logical_not(first), q2, jnp.zeros_like(q2))
            bias = jnp.concatenate([t2_ref[hh, jb + 2 * m] for m in range(kh // 2)], axis=-1)
            s_w = lax.dot_general(qm, k2, NT_DIMS, preferred_element_type=F32) + bias
            s_c = lax.dot_general(qm, kc, NT_DIMS, preferred_element_type=F32)
            outs.append(_softmax_pv([s_w, s_c], [v2, vc]))
        o_ref[0, pl.ds(pl.multiple_of(r * w, w), w), :] = jnp.where(first, outs[0], outs[1]).astype(o_ref.dtype)
        return carry

    lax.fori_loop(0, rows, row, 0, unroll=NA_ROW_UNROLL)


def _na_ctx_body(q_ref, k_ref, v_ref, o_ref, *, scale):
    n = q_ref.shape[1]
    first = lax.broadcasted_iota(jnp.int32, (n, LANES), 1) < (LANES // 2)
    q2 = q_ref[0] * scale
    k2 = k_ref[0]
    v2 = v_ref[0]
    outs = []
    for hh in range(2):
        qm = jnp.where(first if hh == 0 else jnp.logical_not(first), q2, jnp.zeros_like(q2))
        s = lax.dot_general(qm, k2, NT_DIMS, preferred_element_type=F32)
        outs.append(_softmax_pv([s], [v2]))
    o_ref[0] = jnp.where(first, outs[0], outs[1]).astype(o_ref.dtype)


def _na_bias_table(rpb, kh, kw):
    w = jnp.arange(GRID_W)[:, None]
    c = jnp.arange(GRID_W)[None, :]
    c0 = jnp.clip(w - kw // 2, 0, GRID_W - kw)
    inside = (c >= c0) & (c < c0 + kw)
    cb = jnp.clip(c - w + kw - 1, 0, 2 * kw - 2)
    t = jnp.where(inside[None, None], rpb.astype(F32)[:, :, cb], NEG_BIG)
    return jnp.concatenate([t[:, :-1], t[:, 1:]], axis=-1)


def _na_mixer(x_c, x_l, ng, mod_c, mod_l, w_qkv, rpb, w_out, g1, want_ctx):
    b, l, d = x_l.shape
    lc = x_c.shape[1]
    heads = rpb.shape[0]
    na_rows, na_cols = (rpb.shape[1] + 1) // 2, (rpb.shape[2] + 1) // 2
    dh = d // heads
    assert 2 * dh == LANES and l % GRID_W == 0
    rows = l // GRID_W
    kh = min(na_rows, rows)
    assert kh % 2 == 0 and kh == na_rows
    scale = dh ** -0.5
    nhp = heads // 2
    qkv_c = _norm_mod_proj(x_c, ng, mod_c[0], mod_c[1], w_qkv, BF16)
    qkv_l = _norm_mod_proj(x_l, ng, mod_l[0], mod_l[1], w_qkv, BF16)
    t2 = _na_bias_table(rpb, kh, na_cols)
    o_l = pl.pallas_call(
        functools.partial(_na_lat_body, rows=rows, kh=kh, scale=scale),
        grid=(b, nhp),
        in_specs=[pl.BlockSpec((1, l, LANES), lambda i, p: (i, 0, p)),
                  pl.BlockSpec((1, l, LANES), lambda i, p: (i, 0, nhp + p)),
                  pl.BlockSpec((1, l, LANES), lambda i, p: (i, 0, 2 * nhp + p)),
                  pl.BlockSpec((1, lc, LANES), lambda i, p: (i, 0, nhp + p)),
                  pl.BlockSpec((1, lc, LANES), lambda i, p: (i, 0, 2 * nhp + p)),
                  pl.BlockSpec((2,) + t2.shape[1:], lambda i, p: (p, 0, 0, 0))],
        out_specs=pl.BlockSpec((1, l, LANES), lambda i, p: (i, 0, p)),
        out_shape=jax.ShapeDtypeStruct((b, l, d), BF16),
        compiler_params=_params("parallel", "parallel"),
        name="na_latent",
    )(qkv_l, qkv_l, qkv_l, qkv_c, qkv_c, t2)
    new_l = _out_call(_out_plain_body, [(o_l, 0, d)], [w_out], x_l, mod_l[2], g1, "na_out")
    new_c = None
    if want_ctx:
        o_c = pl.pallas_call(
            functools.partial(_na_ctx_body, scale=scale),
            grid=(b, nhp),
            in_specs=[pl.BlockSpec((1, lc, LANES), lambda i, p: (i, 0, p)),
                      pl.BlockSpec((1, lc, LANES), lambda i, p: (i, 0, nhp + p)),
                      pl.BlockSpec((1, lc, LANES), lambda i, p: (i, 0, 2 * nhp + p))],
            out_specs=pl.BlockSpec((1, lc, LANES), lambda i, p: (i, 0, p)),
            out_shape=jax.ShapeDtypeStruct((b, lc, d), BF16),
            compiler_params=_params("parallel", "parallel"),
            name="na_context",
        )(qkv_c, qkv_c, qkv_c)
        new_c = _out_call(_out_plain_body, [(o_c, 0, d)], [w_out], x_c, mod_c[2], g1, "na_out")
    return new_c, new_l


S5_GROUP_BLOCK = 8


def _s5_kernel_body(c_ref, w_ref, o_ref):
    for i in range(c_ref.shape[0]):
        o_ref[i] = jnp.dot(c_ref[i], w_ref[i], preferred_element_type=F32, precision=lax.Precision.HIGHEST)


def _s5_impulse(cmat, wmat):
    n = cmat.shape[0]
    gb = S5_GROUP_BLOCK
    return pl.pallas_call(
        _s5_kernel_body,
        grid=(n // gb,),
        in_specs=[pl.BlockSpec((gb,) + cmat.shape[1:], lambda i: (i, 0, 0)),
                  pl.BlockSpec((gb,) + wmat.shape[1:], lambda i: (i, 0, 0))],
        out_specs=pl.BlockSpec((gb, cmat.shape[1], wmat.shape[2]), lambda i: (i, 0, 0)),
        out_shape=jax.ShapeDtypeStruct((n, cmat.shape[1], wmat.shape[2]), F32),
        compiler_params=_params("parallel"),
        name="s5_impulse",
    )(cmat, wmat)


def _s5_operators(lam_re, lam_im, log_dt, b_re, b_im, c_re, c_im, t):
    _, g, p = lam_re.shape
    cg = b_re.shape[-1]
    lam = lax.complex(lam_re.astype(F32), lam_im.astype(F32))
    ldt = lam * jnp.exp(log_dt.astype(F32))[..., None]
    a = jnp.exp(ldt)
    bbar = ((a - 1.0) / lam)[..., None] * lax.complex(b_re.astype(F32), b_im.astype(F32))
    cm = lax.complex(c_re.astype(F32), c_im.astype(F32))
    apow = jnp.exp(ldt[..., None] * jnp.arange(t + 1, dtype=F32))
    w = apow[..., :t, None] * bbar[:, :, :, None, :]
    wmat = jnp.concatenate([jnp.real(w), jnp.imag(w)], axis=2).reshape(2 * g, 2 * p, t * cg)
    cmat = jnp.concatenate([jnp.real(cm), -jnp.imag(cm)], axis=-1).reshape(2 * g, cg, 2 * p)
    k = _s5_impulse(cmat, wmat).reshape(2, g, cg, t, cg)
    k = jnp.transpose(k, (0, 1, 3, 4, 2))
    kfull = jnp.concatenate([k[1, :, :0:-1], (k[0, :, :1] + k[1, :, :1]), k[0, :, 1:]], axis=1)
    idx = jnp.arange(t)[None, :] - jnp.arange(t)[:, None] + (t - 1)
    toep = jnp.transpose(kfull[:, idx], (0, 1, 3, 2, 4)).reshape(g, t * cg, t * cg)

    def state_in(wd, flip):
        wd = wd[:, :, ::-1] if flip else wd
        m = jnp.transpose(wd, (0, 2, 3, 1)).reshape(g, t * cg, p)
        return jnp.concatenate([jnp.real(m), jnp.imag(m)], axis=-1)

    def state_out(cd, pw):
        n = cd[:, :, :, None] * pw[:, None, :, :]
        n = jnp.transpose(n, (0, 2, 3, 1)).reshape(g, p, t * cg)
        return jnp.concatenate([jnp.real(n), -jnp.imag(n)], axis=1)

    m_f = state_in(w[0], True)
    m_b = state_in(w[1], False)
    n_f = state_out(cm[0], apow[0][..., 1:])
    n_b = state_out(cm[1], apow[1][..., :0:-1])
    at = apow[..., t]
    a1 = jnp.concatenate([jnp.real(at), jnp.real(at)], axis=-1)[:, :, None, :]
    a2 = jnp.concatenate([-jnp.imag(at), jnp.imag(at)], axis=-1)[:, :, None, :]
    bf = lambda m: m.astype(BF16)
    return dict(toep=bf(toep), m_f=bf(m_f), m_b=bf(m_b), n_f=bf(n_f), n_b=bf(n_b), a1=a1, a2=a2)


def _s5_body(u_ref, dsk_ref, toep_ref, mf_ref, mb_ref, nf_ref, nb_ref, a1_ref, a2_ref, x0_ref,
             y_ref, xfin_ref, z_scr, xin_scr):
    gb, nc = u_ref.shape[1], u_ref.shape[2]
    half = LANES // 2
    for gi in range(gb):
        ub = u_ref[0, gi].astype(BF16)
        z_scr[0, :, gi, :] = jnp.dot(ub, mf_ref[gi], preferred_element_type=F32)
        z_scr[1, :, gi, :] = jnp.dot(ub, mb_ref[gi], preferred_element_type=F32)

    def scan(j, xs):
        out = []
        for dr in range(2):
            jj = j if dr == 0 else nc - 1 - j
            x = xs[dr]
            xin_scr[dr, jj] = x
            out.append(a1_ref[dr, :, 0, :] * x + a2_ref[dr, :, 0, :] * pltpu.roll(x, half, 1) + z_scr[dr, jj])
        return tuple(out)

    xf, xb = lax.fori_loop(0, nc, scan, (x0_ref[0, 0], x0_ref[0, 1]))
    xfin_ref[0, 0] = xf
    xfin_ref[0, 1] = xb
    for gi in range(gb):
        u = u_ref[0, gi]
        y = jnp.dot(u.astype(BF16), toep_ref[gi], preferred_element_type=F32) + u * dsk_ref[gi]
        y = y + jnp.dot(xin_scr[0, :, gi, :].astype(BF16), nf_ref[gi], preferred_element_type=F32)
        y = y + jnp.dot(xin_scr[1, :, gi, :].astype(BF16), nb_ref[gi], preferred_element_type=F32)
        y_ref[0, gi] = y


def _s5_scan(h, ops, dsk, x0):
    b, l, d = h.shape
    g = ops["toep"].shape[0]
    cg = d // g
    t = S5_CHUNK
    nc = l // t
    tc = t * cg
    gb = S5_GROUP_BLOCK
    u = jnp.transpose(h.reshape(b, nc, t, g, cg), (0, 3, 1, 2, 4)).reshape(b, g, nc, tc)
    per_g = lambda shape: pl.BlockSpec((gb,) + shape, lambda j, i: (j,) + (0,) * len(shape))
    y, xfin = pl.pallas_call(
        _s5_body,
        grid=(g // gb, b),
        in_specs=[pl.BlockSpec((1, gb, nc, tc), lambda j, i: (i, j, 0, 0)),
                  per_g((1, tc)), per_g((tc, tc)), per_g((tc, LANES)), per_g((tc, LANES)),
                  per_g((LANES, tc)), per_g((LANES, tc)),
                  pl.BlockSpec((2, gb, 1, LANES), lambda j, i: (0, j, 0, 0)),
                  pl.BlockSpec((2, gb, 1, LANES), lambda j, i: (0, j, 0, 0)),
                  pl.BlockSpec((1, 2, gb, LANES), lambda j, i: (i, 0, j, 0))],
        out_specs=[pl.BlockSpec((1, gb, nc, tc), lambda j, i: (i, j, 0, 0)),
                   pl.BlockSpec((1, 2, gb, LANES), lambda j, i: (i, 0, j, 0))],
        out_shape=[jax.ShapeDtypeStruct((b, g, nc, tc), F32), jax.ShapeDtypeStruct((b, 2, g, LANES), F32)],
        scratch_shapes=[pltpu.VMEM((2, nc, gb, LANES), F32), pltpu.VMEM((2, nc, gb, LANES), F32)],
        compiler_params=_params("parallel", "parallel"),
        name="s5_scan",
    )(u, dsk, ops["toep"], ops["m_f"], ops["m_b"], ops["n_f"], ops["n_b"], ops["a1"], ops["a2"], x0)
    y = jnp.transpose(y.reshape(b, g, nc, t, cg), (0, 2, 3, 1, 4)).reshape(b, l, d)
    return y, xfin


def _s5_mixer(x_c, x_l, ng, mod_c, mod_l, lam_re, lam_im, log_dt, b_re, b_im, c_re, c_im, d_skip, w_glu, g1,
              want_ctx):
    b, l, d = x_l.shape
    g, p = lam_re.shape[1], lam_re.shape[2]
    cg = d // g
    assert 2 * p == LANES and g % S5_GROUP_BLOCK == 0
    ops = _s5_operators(lam_re, lam_im, log_dt, b_re, b_im, c_re, c_im, S5_CHUNK)
    dsk = jnp.tile(d_skip.astype(F32).reshape(g, 1, cg), (1, 1, S5_CHUNK))
    h_c = _norm_mod_only(x_c, ng, mod_c[0], mod_c[1])
    h_l = _norm_mod_only(x_l, ng, mod_l[0], mod_l[1])
    y_c, x_ctx = _s5_scan(h_c, ops, dsk, jnp.zeros((b, 2, g, LANES), F32))
    y_l, _ = _s5_scan(h_l, ops, dsk, x_ctx)
    new_l = _out_call(_out_glu_body, [(y_l, 0, d)], [w_glu], x_l, mod_l[2], g1, "s5_glu_out")
    new_c = None
    if want_ctx:
        new_c = _out_call(_out_glu_body, [(y_c, 0, d)], [w_glu], x_c, mod_c[2], g1, "s5_glu_out")
    return new_c, new_l


FFN_HALO = 16
FFN_TN = 256


def _ffn_body(xp_ref, x_ref, xn_ref, g2_ref, sh_ref, sc_ref, wa_ref, wv_ref, cwa_ref, cwv_ref, cba_ref, cbv_ref,
              wo_ref, gate_ref, g3_ref, o_ref, h_scr, acc_scr, *, per):
    i, j = pl.program_id(0), pl.program_id(1)
    tm = x_ref.shape[0]
    hl = FFN_HALO

    @pl.when(j == 0)
    def _():
        g, sh, sc = g2_ref[...], sh_ref[0], sc_ref[0]
        keep_p = ((i % per) != 0).astype(F32)
        keep_n = ((i % per) != per - 1).astype(F32)
        h_scr[0:hl] = (_norm_mod(xp_ref[...], g, sh, sc) * keep_p).astype(BF16)
        h_scr[hl:hl + tm] = _norm_mod(x_ref[...], g, sh, sc).astype(BF16)
        h_scr[hl + tm:] = (_norm_mod(xn_ref[...], g, sh, sc) * keep_n).astype(BF16)

    h = h_scr[...]
    rows = tm + 2 * hl

    def conv(w_ref, cw_ref, cb_ref):
        u = jnp.dot(h, w_ref[...], preferred_element_type=F32)
        um = pltpu.roll(u, 1, 0)[hl:hl + tm]
        up = pltpu.roll(u, rows - 1, 0)[hl:hl + tm]
        return cb_ref[...] + um * cw_ref[0:1] + u[hl:hl + tm] * cw_ref[1:2] + up * cw_ref[2:3]

    a = conv(wa_ref, cwa_ref, cba_ref)
    v = conv(wv_ref, cwv_ref, cbv_ref)
    part = jnp.dot((a * jax.nn.sigmoid(a) * v).astype(BF16), wo_ref[...], preferred_element_type=F32)

    @pl.when(j == 0)
    def _():
        acc_scr[...] = part

    @pl.when(j > 0)
    def _():
        acc_scr[...] += part

    @pl.when(j == pl.num_programs(1) - 1)
    def _():
        o_ref[...] = x_ref[...] + gate_ref[0] * _rms(acc_scr[...], g3_ref[...])


def _conv_ffn_block(x, ng2, mod, w_in, conv_w, conv_b, w_out, ng3):
    b, l, d = x.shape
    f = w_out.shape[0]
    tn, hl = FFN_TN, FFN_HALO
    assert f % tn == 0 and conv_w.shape[0] == 3 and w_in.shape[1] == 2 * f
    nj = f // tn
    tm = _row_tile(l, 1024)
    per = l // tm
    hb = tm // hl
    last_hb = b * l // hl - 1
    cb = conv_b.reshape(1, 2 * f)
    x2 = x.reshape(b * l, d)
    out = pl.pallas_call(
        functools.partial(_ffn_body, per=per),
        grid=(b * per, nj),
        in_specs=[pl.BlockSpec((hl, d), lambda i, j: (jnp.maximum(i * hb - 1, 0), 0)),
                  pl.BlockSpec((tm, d), lambda i, j: (i, 0)),
                  pl.BlockSpec((hl, d), lambda i, j: (jnp.minimum((i + 1) * hb, last_hb), 0)),
                  pl.BlockSpec((1, d), lambda i, j: (0, 0)),
                  pl.BlockSpec((1, 1, d), lambda i, j: (i // per, 0, 0)),
                  pl.BlockSpec((1, 1, d), lambda i, j: (i // per, 0, 0)),
                  pl.BlockSpec((d, tn), lambda i, j: (0, j)),
                  pl.BlockSpec((d, tn), lambda i, j: (0, nj + j)),
                  pl.BlockSpec((3, tn), lambda i, j: (0, j)),
                  pl.BlockSpec((3, tn), lambda i, j: (0, nj + j)),
                  pl.BlockSpec((1, tn), lambda i, j: (0, j)),
                  pl.BlockSpec((1, tn), lambda i, j: (0, nj + j)),
                  pl.BlockSpec((tn, d), lambda i, j: (j, 0)),
                  pl.BlockSpec((1, 1, d), lambda i, j: (i // per, 0, 0)),
                  pl.BlockSpec((1, d), lambda i, j: (0, 0))],
        out_specs=pl.BlockSpec((tm, d), lambda i, j: (i, 0)),
        out_shape=jax.ShapeDtypeStruct((b * l, d), F32),
        scratch_shapes=[pltpu.VMEM((tm + 2 * hl, d), BF16), pltpu.VMEM((tm, d), F32)],
        compiler_params=_params("parallel", "arbitrary"),
        name="conv_ffn",
    )(x2, x2, x2, ng2.reshape(1, d), mod[3], mod[4], w_in, w_in, conv_w, conv_w, cb, cb, w_out, mod[5],
      ng3.reshape(1, d))
    return out.reshape(b, l, d)


def kernel(x, c, ctx, c_ctx, w_mod, b_mod, norm_g, a_w_in, a_lower_logits, a_out_g, a_w_out, b_w_qkv, b_rpb,
           b_w_out, c_lam_re, c_lam_im, c_log_dt, c_b_re, c_b_im, c_c_re, c_c_im, c_d, c_w_glu, f_w_in,
           f_conv_w, f_conv_b, f_w_out):
    bsz, _, d = x.shape
    depth = w_mod.shape[0]
    p = jax.nn.softmax(a_lower_logits.astype(F32), axis=0)
    lower = jnp.cumsum(p, axis=0) - p[0]
    n_rows = -(-(bsz + 1) // SUBLANES) * SUBLANES
    c_rows = jnp.zeros((n_rows, d), F32).at[:bsz].set(c).at[bsz].set(c_ctx)
    mods = _modulation(c_rows, w_mod, b_mod)
    wb = lambda w: w.astype(BF16)
    lat, cx = x, ctx
    for i in range(depth):
        kind, j = i % N_MIXERS, i // N_MIXERS
        last = i == depth - 1
        m = mods[i].reshape(n_rows, N_MOD, d)
        mod_l = [m[:bsz, k][:, None, :] for k in range(N_MOD)]
        mod_c = [jnp.broadcast_to(m[bsz, k][None, None, :], (bsz, 1, d)) for k in range(N_MOD)]
        ng = norm_g[i]
        if kind == 0:
            cx1, lat = _hgrn2_mixer(cx, lat, ng[0], mod_c, mod_l, wb(a_w_in[j]), lower[i], a_out_g[j],
                                    wb(a_w_out[j]), ng[1], not last)
        elif kind == 1:
            cx1, lat = _na_mixer(cx, lat, ng[0], mod_c, mod_l, wb(b_w_qkv[j]), b_rpb[j], wb(b_w_out[j]), ng[1],
                                 not last)
        else:
            cx1, lat = _s5_mixer(cx, lat, ng[0], mod_c, mod_l, c_lam_re[j], c_lam_im[j], c_log_dt[j], c_b_re[j],
                                 c_b_im[j], c_c_re[j], c_c_im[j], c_d[j], wb(c_w_glu[j]), ng[1], not last)
        lat = _conv_ffn_block(lat, ng[2], mod_l, wb(f_w_in[i]), f_conv_w[i], f_conv_b[i], wb(f_w_out[i]), ng[3])
        if not last:
            cx = _conv_ffn_block(cx1, ng[2], mod_c, wb(f_w_in[i]), f_conv_w[i], f_conv_b[i], wb(f_w_out[i]), ng[3])
    return lat
```

```python
import functools

import jax
import jax.numpy as jnp
from jax import lax
from jax.experimental import pallas as pl
from jax.experimental.pallas import tpu as pltpu

F32 = jnp.float32
BF16 = jnp.bfloat16

EPS = 1e-6
F_MIN = 1e-30
N_MOD = 6
N_MIXERS = 3
A_HEADS = 8
GRID_W = 64
S5_CHUNK = 32
NEG_BIG = -1e30
NA_ROW_UNROLL = 4
SUBLANES = 8
LANES = 128
VMEM_LIMIT_BYTES = 48 * 1024 * 1024

NT_DIMS = (((1,), (1,)), ((), ()))
TN_DIMS = (((0,), (0,)), ((), ()))


def _params(*sem):
    return pltpu.CompilerParams(dimension_semantics=sem, vmem_limit_bytes=VMEM_LIMIT_BYTES)


def _row_tile(n, want):
    t = min(n, want)
    assert n % t == 0, (n, t)
    return t


def _rms(y, g):
    return y * lax.rsqrt(jnp.mean(y * y, axis=-1, keepdims=True) + EPS) * g


def _mod_body(c_ref, w_ref, b_ref, o_ref):
    c = c_ref[...]
    s = (c * jax.nn.sigmoid(c)).astype(BF16)
    o_ref[0] = jnp.dot(s, w_ref[0].astype(BF16), preferred_element_type=F32) + b_ref[0]


def _modulation(c_rows, w_mod, b_mod):
    depth, d, n = w_mod.shape
    r = c_rows.shape[0]
    tn = n // 4
    return pl.pallas_call(
        _mod_body,
        grid=(depth, n // tn),
        in_specs=[pl.BlockSpec((r, d), lambda i, j: (0, 0)),
                  pl.BlockSpec((1, d, tn), lambda i, j: (i, 0, j)),
                  pl.BlockSpec((1, 1, tn), lambda i, j: (i, 0, j))],
        out_specs=pl.BlockSpec((1, r, tn), lambda i, j: (i, 0, j)),
        out_shape=jax.ShapeDtypeStruct((depth, r, n), F32),
        compiler_params=_params("parallel", "parallel"),
        name="adaln_mod",
    )(c_rows, w_mod, b_mod.reshape(depth, 1, n))


def _norm_mod(x, g, sh, sc):
    return _rms(x, g) * (1.0 + sc) + sh


def _proj_body(x_ref, g_ref, sh_ref, sc_ref, w_ref, o_ref, h_scr):
    @pl.when(pl.program_id(1) == 0)
    def _():
        h_scr[...] = _norm_mod(x_ref[...], g_ref[...], sh_ref[0], sc_ref[0]).astype(BF16)

    o_ref[...] = jnp.dot(h_scr[...], w_ref[...], preferred_element_type=F32).astype(o_ref.dtype)


def _norm_mod_proj(x, g, sh, sc, w, out_dtype):
    b, l, d = x.shape
    n = w.shape[1]
    tm = _row_tile(l, 1024)
    tn = _row_tile(n, 512)
    per = l // tm
    out = pl.pallas_call(
        _proj_body,
        grid=(b * per, n // tn),
        in_specs=[pl.BlockSpec((tm, d), lambda i, j: (i, 0)),
                  pl.BlockSpec((1, d), lambda i, j: (0, 0)),
                  pl.BlockSpec((1, 1, d), lambda i, j: (i // per, 0, 0)),
                  pl.BlockSpec((1, 1, d), lambda i, j: (i // per, 0, 0)),
                  pl.BlockSpec((d, tn), lambda i, j: (0, j))],
        out_specs=pl.BlockSpec((tm, tn), lambda i, j: (i, j)),
        out_shape=jax.ShapeDtypeStruct((b * l, n), out_dtype),
        scratch_shapes=[pltpu.VMEM((tm, d), BF16)],
        compiler_params=_params("parallel", "arbitrary"),
        name="norm_mod_proj",
    )(x.reshape(b * l, d), g.reshape(1, d), sh, sc, w)
    return out.reshape(b, l, n)


def _norm_only_body(x_ref, g_ref, sh_ref, sc_ref, o_ref):
    o_ref[...] = _norm_mod(x_ref[...], g_ref[...], sh_ref[0], sc_ref[0])


def _norm_mod_only(x, g, sh, sc):
    b, l, d = x.shape
    tm = _row_tile(l, 512)
    per = l // tm
    out = pl.pallas_call(
        _norm_only_body,
        grid=(b * per,),
        in_specs=[pl.BlockSpec((tm, d), lambda i: (i, 0)),
                  pl.BlockSpec((1, d), lambda i: (0, 0)),
                  pl.BlockSpec((1, 1, d), lambda i: (i // per, 0, 0)),
                  pl.BlockSpec((1, 1, d), lambda i: (i // per, 0, 0))],
        out_specs=pl.BlockSpec((tm, d), lambda i: (i, 0)),
        out_shape=jax.ShapeDtypeStruct((b * l, d), F32),
        compiler_params=_params("parallel"),
        name="norm_mod",
    )(x.reshape(b * l, d), g.reshape(1, d), sh, sc)
    return out.reshape(b, l, d)


def _residual(res_ref, gate_ref, g1_ref, y):
    return res_ref[...] + gate_ref[0] * _rms(y, g1_ref[...])


def _out_plain_body(a_ref, w_ref, res_ref, gate_ref, g1_ref, o_ref):
    y = jnp.dot(a_ref[...], w_ref[...], preferred_element_type=F32)
    o_ref[...] = _residual(res_ref, gate_ref, g1_ref, y)


def _out_hgrn_body(of_ref, ob_ref, gp_ref, og_ref, w_ref, res_ref, gate_ref, g1_ref, o_ref):
    o = of_ref[...] + ob_ref[...]
    parts = []
    for h in range(A_HEADS):
        oh = o[:, h * LANES:(h + 1) * LANES]
        parts.append(oh * lax.rsqrt(jnp.mean(oh * oh, axis=-1, keepdims=True) + EPS))
    gp = gp_ref[...]
    a = jnp.concatenate(parts, axis=-1) * og_ref[...] * (gp * jax.nn.sigmoid(gp))
    y = jnp.dot(a.astype(BF16), w_ref[...], preferred_element_type=F32)
    o_ref[...] = _residual(res_ref, gate_ref, g1_ref, y)


def _gelu_tanh(y):
    return 0.5 * y * (1.0 + jnp.tanh(0.7978845608028654 * (y + 0.044715 * (y * y * y))))


def _out_glu_body(y_ref, w_ref, res_ref, gate_ref, g1_ref, o_ref):
    d = o_ref.shape[-1]
    a = _gelu_tanh(y_ref[...]).astype(BF16)
    ag = jnp.dot(a, w_ref[...], preferred_element_type=F32)
    y = ag[:, :d] * jax.nn.sigmoid(ag[:, d:])
    o_ref[...] = _residual(res_ref, gate_ref, g1_ref, y)


def _out_call(body, row_inputs, const_inputs, res, gate, g1, name):
    b, l, d = res.shape
    tm = _row_tile(l, 256)
    per = l // tm
    in_specs, args = [], []
    for arr, blk, width in row_inputs:
        in_specs.append(pl.BlockSpec((tm, width), lambda i, blk=blk: (i, blk)))
        args.append(arr.reshape(b * l, arr.shape[-1]))
    for arr in const_inputs:
        in_specs.append(pl.BlockSpec(arr.shape, lambda i, nd=arr.ndim: (0,) * nd))
        args.append(arr)
    in_specs += [pl.BlockSpec((tm, d), lambda i: (i, 0)),
                 pl.BlockSpec((1, 1, d), lambda i: (i // per, 0, 0)),
                 pl.BlockSpec((1, d), lambda i: (0, 0))]
    args += [res.reshape(b * l, d), gate, g1.reshape(1, d)]
    out = pl.pallas_call(
        body,
        grid=(b * per,),
        in_specs=in_specs,
        out_specs=pl.BlockSpec((tm, d), lambda i: (i, 0)),
        out_shape=jax.ShapeDtypeStruct((b * l, d), F32),
        compiler_params=_params("parallel"),
        name=name,
    )(*args)
    return out.reshape(b, l, d)


HGRN_CHUNK = 32
HGRN_SAFE_MIN = 1e-30
HGRN_Q_HEADROOM = 1e37


def _hgrn_gates(z, lb):
    f = lb + (1.0 - lb) * jax.nn.sigmoid(z)
    return jnp.maximum(f, F_MIN), 1.0 - f


def _hgrn_chunk(q, v, z, lb, st, reverse):
    c = q.shape[0]
    fm, kk = _hgrn_gates(z, lb)
    pos = lax.broadcasted_iota(jnp.int32, (c, LANES), 0)
    if reverse:
        pos = (c - 1) - pos
    pp = fm
    j = 1
    while j < c:
        pp = pp * jnp.where(pos >= j, pltpu.roll(pp, (c - j) if reverse else j, 0), 1.0)
        j *= 2
    p_last = pp[0:1] if reverse else pp[c - 1:c]
    mid = c // 2
    p_mid = pp[mid:mid + 1]
    r = pp * (1.0 / p_mid)
    r_last = r[0:1] if reverse else r[c - 1:c]
    kd = kk * (1.0 / r)
    sc = lax.dot_general((q * r).astype(BF16), kd.astype(BF16), NT_DIMS, preferred_element_type=F32)
    row = lax.broadcasted_iota(jnp.int32, (c, c), 0)
    col = lax.broadcasted_iota(jnp.int32, (c, c), 1)
    sc = jnp.where((col >= row) if reverse else (col <= row), sc, 0.0)
    vb = v.astype(BF16)
    o = jnp.dot(sc.astype(BF16), vb, preferred_element_type=F32)
    o = o + lax.dot_general((q * pp).astype(BF16), st.astype(BF16), NT_DIMS, preferred_element_type=F32)
    u = lax.dot_general(vb, (kd * r_last).astype(BF16), TN_DIMS, preferred_element_type=F32)
    q_max = jnp.max(jnp.abs(q), axis=0, keepdims=True)
    ok = jnp.where(p_mid >= HGRN_SAFE_MIN, r_last, 0.0) >= HGRN_SAFE_MIN
    ok = jnp.where(ok, p_mid * HGRN_Q_HEADROOM, -1.0) >= q_max
    return o, p_last * st + u, jnp.where(ok, 0.0, 1.0)


def _hgrn_exact_block(q_ref, v_ref, z_ref, lb_ref, o_ref, st_scr, dr, reverse, n_tiles):
    pos = lax.broadcasted_iota(jnp.int32, (SUBLANES, LANES), 0)
    if reverse:
        pos = (SUBLANES - 1) - pos

    def earlier(x, j):
        if j == 0:
            return x
        return pltpu.roll(x, (SUBLANES - j) if reverse else j, 0)

    def later(x, j):
        return pltpu.roll(x, j if reverse else (SUBLANES - j), 0)

    def tile(i, carry):
        ti = (n_tiles - 1 - i) if reverse else i
        r0 = pl.multiple_of(ti * SUBLANES, SUBLANES)
        for h in range(A_HEADS):
            sl = slice(h * LANES, (h + 1) * LANES)
            lb = lb_ref[:, sl]
            z = z_ref[0, pl.ds(r0, SUBLANES), sl]
            q = q_ref[0, pl.ds(r0, SUBLANES), sl]
            v = v_ref[0, pl.ds(r0, SUBLANES), sl]
            fm, kk = _hgrn_gates(z, lb)
            pp = fm
            for j in (1, 2, 4):
                pp = pp * jnp.where(pos >= j, earlier(pp, j), 1.0)
            qq = jnp.where(pos <= SUBLANES - 2, later(fm, 1), 1.0)
            for j in (1, 2, 4):
                qq = qq * jnp.where(pos <= SUBLANES - 1 - j, later(qq, j), 1.0)
            dec = pp[0:1] if reverse else pp[SUBLANES - 1:SUBLANES]
            st = st_scr[dr, h]
            o = lax.dot_general((q * pp).astype(BF16), st.astype(BF16), NT_DIMS,
                                preferred_element_type=F32)
            g = fm
            for d in range(SUBLANES):
                if d == 0:
                    e = q * kk
                else:
                    if d > 1:
                        g = g * earlier(fm, d - 1)
                    e = q * g * jnp.where(pos >= d, earlier(kk, d), 0.0)
                o = o + jnp.sum(e, axis=-1, keepdims=True) * earlier(v, d)
            o_ref[0, pl.ds(r0, SUBLANES), sl] = o
            u = lax.dot_general(v.astype(BF16), (kk * qq).astype(BF16), TN_DIMS,
                                preferred_element_type=F32)
            st_scr[dr, h] = dec * st + u
        return carry

    lax.fori_loop(0, n_tiles, tile, 0)


def _hgrn_scan_body(qf_ref, vf_ref, zf_ref, qb_ref, vb_ref, zb_ref, lb_ref, s0_ref, of_ref, ob_ref, sfin_ref,
                    st_scr, save_scr, *, n_chunks):
    step = pl.program_id(1)

    @pl.when(step == 0)
    def _():
        st_scr[...] = s0_ref[0]

    save_scr[...] = st_scr[...]
    dirs = ((qf_ref, vf_ref, zf_ref, of_ref, False), (qb_ref, vb_ref, zb_ref, ob_ref, True))
    c = HGRN_CHUNK

    def chunk(i, bad):
        for dr, (q_ref, v_ref, z_ref, o_ref, reverse) in enumerate(dirs):
            r0 = pl.multiple_of(((n_chunks - 1 - i) if reverse else i) * c, c)
            for h in range(A_HEADS):
                sl = slice(h * LANES, (h + 1) * LANES)
                o, st_new, bad_h = _hgrn_chunk(q_ref[0, pl.ds(r0, c), sl], v_ref[0, pl.ds(r0, c), sl],
                                               z_ref[0, pl.ds(r0, c), sl], lb_ref[:, sl], st_scr[dr, h], reverse)
                o_ref[0, pl.ds(r0, c), sl] = o
                st_scr[dr, h] = st_new
                bad = jnp.maximum(bad, bad_h)
        return bad

    bad = lax.fori_loop(0, n_chunks, chunk, jnp.zeros((1, LANES), F32))

    @pl.when(jnp.max(bad) > 0.0)
    def _():
        st_scr[...] = save_scr[...]
        for dr, (q_ref, v_ref, z_ref, o_ref, reverse) in enumerate(dirs):
            _hgrn_exact_block(q_ref, v_ref, z_ref, lb_ref, o_ref, st_scr, dr, reverse, n_chunks * c // SUBLANES)

    @pl.when(step == pl.num_programs(1) - 1)
    def _():
        sfin_ref[0] = st_scr[...]


def _hgrn_scan(proj, lb, s0):
    b, l, _ = proj.shape
    d = A_HEADS * LANES
    t = _row_tile(l, 256)
    nb = l // t
    fwd = lambda blk: pl.BlockSpec((1, t, d), lambda i, s: (i, s, blk))
    bwd = lambda blk: pl.BlockSpec((1, t, d), lambda i, s: (i, nb - 1 - s, blk))
    st_spec = pl.BlockSpec((1,) + s0.shape[1:], lambda i, s: (i, 0, 0, 0, 0))
    return pl.pallas_call(
        functools.partial(_hgrn_scan_body, n_chunks=t // HGRN_CHUNK),
        grid=(b, nb),
        in_specs=[fwd(0), fwd(1), fwd(3), bwd(0), bwd(1), bwd(4),
                  pl.BlockSpec((1, d), lambda i, s: (0, 0)), st_spec],
        out_specs=[fwd(0), bwd(0), st_spec],
        out_shape=[jax.ShapeDtypeStruct((b, l, d), F32), jax.ShapeDtypeStruct((b, l, d), F32),
                   jax.ShapeDtypeStruct(s0.shape, F32)],
        scratch_shapes=[pltpu.VMEM(s0.shape[1:], F32), pltpu.VMEM(s0.shape[1:], F32)],
        compiler_params=_params("parallel", "arbitrary"),
        name="hgrn_scan",
    )(proj, proj, proj, proj, proj, proj, lb, s0)


def _hgrn2_mixer(x_c, x_l, ng, mod_c, mod_l, w_in, lower, out_g, w_out, g1, want_ctx):
    d = x_l.shape[-1]
    p_c = _norm_mod_proj(x_c, ng, mod_c[0], mod_c[1], w_in, F32)
    p_l = _norm_mod_proj(x_l, ng, mod_l[0], mod_l[1], w_in, F32)
    lb = lower.reshape(1, d)
    zero = jnp.zeros((x_l.shape[0], 2, A_HEADS, LANES, LANES), F32)
    ocf, ocb, s_ctx = _hgrn_scan(p_c, lb, zero)
    olf, olb, _ = _hgrn_scan(p_l, lb, s_ctx)
    o_c, o_l = (ocf, ocb), (olf, olb)
    og = out_g.reshape(1, d)
    new_l = _out_call(_out_hgrn_body, [(o_l[0], 0, d), (o_l[1], 0, d), (p_l, 2, d)], [og, w_out],
                      x_l, mod_l[2], g1, "hgrn_out")
    new_c = None
    if want_ctx:
        new_c = _out_call(_out_hgrn_body, [(o_c[0], 0, d), (o_c[1], 0, d), (p_c, 2, d)], [og, w_out],
                          x_c, mod_c[2], g1, "hgrn_out")
    return new_c, new_l


def _softmax_pv(s_list, v_list):
    mx = functools.reduce(jnp.maximum, [jnp.max(s, axis=-1, keepdims=True) for s in s_list])
    ps = [jnp.exp(s - mx) for s in s_list]
    den = functools.reduce(jnp.add, [jnp.sum(p, axis=-1, keepdims=True) for p in ps])
    acc = functools.reduce(jnp.add, [jnp.dot(p.astype(BF16), v, preferred_element_type=F32)
                                     for p, v in zip(ps, v_list)])
    return acc / den


def _na_lat_body(q_ref, k_ref, v_ref, kc_ref, vc_ref, t2_ref, o_ref, *, rows, kh, scale):
    w = GRID_W
    first = lax.broadcasted_iota(jnp.int32, (w, LANES), 1) < (LANES // 2)
    kc = kc_ref[0]
    vc = vc_ref[0]

    def row(r, carry):
        r0 = jnp.clip(r - kh // 2, 0, rows - kh)
        jb = r0 - r + (t2_ref.shape[1] // 2)
        q2 = q_ref[0, pl.ds(pl.multiple_of(r * w, w), w), :] * scale
        ks = pl.multiple_of(r0 * w, w)
        k2 = k_ref[0, pl.ds(ks, kh * w), :]
        v2 = v_ref[0, pl.ds(ks, kh * w), :]
        outs = []
        for hh in range(2):
            qm = jnp.where(first if hh == 0 else jnp.logical_not(first), q2, jnp.zeros_like(q2))
            bias = jnp.concatenate([t2_ref[hh, jb + 2 * m] for m in range(kh // 2)], axis=-1)
            s_w = lax.dot_general(qm, k2, NT_DIMS, preferred_element_type=F32) + bias
            s_c = lax.dot_general(qm, kc, NT_DIMS, preferred_element_type=F32)
            outs.append(_softmax_pv([s_w, s_c], [v2, vc]))
        o_ref[0, pl.ds(pl.multiple_of(r * w, w), w), :] = jnp.where(first, outs[0], outs[1]).astype(o_ref.dtype)
        return carry

    lax.fori_loop(0, rows, row, 0, unroll=NA_ROW_UNROLL)


def _na_ctx_body(q_ref, k_ref, v_ref, o_ref, *, scale):
    n = q_ref.shape[1]
    first = lax.broadcasted_iota(jnp.int32, (n, LANES), 1) < (LANES // 2)
    q2 = q_ref[0] * scale
    k2 = k_ref[0]
    v2 = v_ref[0]
    outs = []
    for hh in range(2):
        qm = jnp.where(first if hh == 0 else jnp.logical_not(first), q2, jnp.zeros_like(q2))
        s = lax.dot_general(qm, k2, NT_DIMS, preferred_element_type=F32)
        outs.append(_softmax_pv([s], [v2]))
    o_ref[0] = jnp.where(first, outs[0], outs[1]).astype(o_ref.dtype)


def _na_bias_table(rpb, kh, kw):
    w = jnp.arange(GRID_W)[:, None]
    c = jnp.arange(GRID_W)[None, :]
    c0 = jnp.clip(w - kw // 2, 0, GRID_W - kw)
    inside = (c >= c0) & (c < c0 + kw)
    cb = jnp.clip(c - w + kw - 1, 0, 2 * kw - 2)
    t = jnp.where(inside[None, None], rpb.astype(F32)[:, :, cb], NEG_BIG)
    return jnp.concatenate([t[:, :-1], t[:, 1:]], axis=-1)


def _na_mixer(x_c, x_l, ng, mod_c, mod_l, w_qkv, rpb, w_out, g1, want_ctx):
    b, l, d = x_l.shape
    lc = x_c.shape[1]
    heads = rpb.shape[0]
    na_rows, na_cols = (rpb.shape[1] + 1) // 2, (rpb.shape[2] + 1) // 2
    dh = d // heads
    assert 2 * dh == LANES and l % GRID_W == 0
    rows = l // GRID_W
    kh = min(na_rows, rows)
    assert kh % 2 == 0 and kh == na_rows
    scale = dh ** -0.5
    nhp = heads // 2
    qkv_c = _norm_mod_proj(x_c, ng, mod_c[0], mod_c[1], w_qkv, BF16)
    qkv_l = _norm_mod_proj(x_l, ng, mod_l[0], mod_l[1], w_qkv, BF16)
    t2 = _na_bias_table(rpb, kh, na_cols)
    o_l = pl.pallas_call(
        functools.partial(_na_lat_body, rows=rows, kh=kh, scale=scale),
        grid=(b, nhp),
        in_specs=[pl.BlockSpec((1, l, LANES), lambda i, p: (i, 0, p)),
                  pl.BlockSpec((1, l, LANES), lambda i, p: (i, 0, nhp + p)),
                  pl.BlockSpec((1, l, LANES), lambda i, p: (i, 0, 2 * nhp + p)),
                  pl.BlockSpec((1, lc, LANES), lambda i, p: (i, 0, nhp + p)),
                  pl.BlockSpec((1, lc, LANES), lambda i, p: (i, 0, 2 * nhp + p)),
                  pl.BlockSpec((2,) + t2.shape[1:], lambda i, p: (p, 0, 0, 0))],
        out_specs=pl.BlockSpec((1, l, LANES), lambda i, p: (i, 0, p)),
        out_shape=jax.ShapeDtypeStruct((b, l, d), BF16),
        compiler_params=_params("parallel", "parallel"),
        name="na_latent",
    )(qkv_l, qkv_l, qkv_l, qkv_c, qkv_c, t2)
    new_l = _out_call(_out_plain_body, [(o_l, 0, d)], [w_out], x_l, mod_l[2], g1, "na_out")
    new_c = None
    if want_ctx:
        o_c = pl.pallas_call(
            functools.partial(_na_ctx_body, scale=scale),
            grid=(b, nhp),
            in_specs=[pl.BlockSpec((1, lc, LANES), lambda i, p: (i, 0, p)),
                      pl.BlockSpec((1, lc, LANES), lambda i, p: (i, 0, nhp + p)),
                      pl.BlockSpec((1, lc, LANES), lambda i, p: (i, 0, 2 * nhp + p))],
            out_specs=pl.BlockSpec((1, lc, LANES), lambda i, p: (i, 0, p)),
            out_shape=jax.ShapeDtypeStruct((b, lc, d), BF16),
            compiler_params=_params("parallel", "parallel"),
            name="na_context",
        )(qkv_c, qkv_c, qkv_c)
        new_c = _out_call(_out_plain_body, [(o_c, 0, d)], [w_out], x_c, mod_c[2], g1, "na_out")
    return new_c, new_l


S5_GROUP_BLOCK = 8


def _s5_kernel_body(c_ref, w_ref, o_ref):
    for i in range(c_ref.shape[0]):
        o_ref[i] = jnp.dot(c_ref[i], w_ref[i], preferred_element_type=F32, precision=lax.Precision.HIGHEST)


def _s5_impulse(cmat, wmat):
    n = cmat.shape[0]
    gb = S5_GROUP_BLOCK
    return pl.pallas_call(
        _s5_kernel_body,
        grid=(n // gb,),
        in_specs=[pl.BlockSpec((gb,) + cmat.shape[1:], lambda i: (i, 0, 0)),
                  pl.BlockSpec((gb,) + wmat.shape[1:], lambda i: (i, 0, 0))],
        out_specs=pl.BlockSpec((gb, cmat.shape[1], wmat.shape[2]), lambda i: (i, 0, 0)),
        out_shape=jax.ShapeDtypeStruct((n, cmat.shape[1], wmat.shape[2]), F32),
        compiler_params=_params("parallel"),
        name="s5_impulse",
    )(cmat, wmat)


def _s5_operators(lam_re, lam_im, log_dt, b_re, b_im, c_re, c_im, t):
    _, g, p = lam_re.shape
    cg = b_re.shape[-1]
    lam = lax.complex(lam_re.astype(F32), lam_im.astype(F32))
    ldt = lam * jnp.exp(log_dt.astype(F32))[..., None]
    a = jnp.exp(ldt)
    bbar = ((a - 1.0) / lam)[..., None] * lax.complex(b_re.astype(F32), b_im.astype(F32))
    cm = lax.complex(c_re.astype(F32), c_im.astype(F32))
    apow = jnp.exp(ldt[..., None] * jnp.arange(t + 1, dtype=F32))
    w = apow[..., :t, None] * bbar[:, :, :, None, :]
    wmat = jnp.concatenate([jnp.real(w), jnp.imag(w)], axis=2).reshape(2 * g, 2 * p, t * cg)
    cmat = jnp.concatenate([jnp.real(cm), -jnp.imag(cm)], axis=-1).reshape(2 * g, cg, 2 * p)
    k = _s5_impulse(cmat, wmat).reshape(2, g, cg, t, cg)
    k = jnp.transpose(k, (0, 1, 3, 4, 2))
    kfull = jnp.concatenate([k[1, :, :0:-1], (k[0, :, :1] + k[1, :, :1]), k[0, :, 1:]], axis=1)
    idx = jnp.arange(t)[None, :] - jnp.arange(t)[:, None] + (t - 1)
    toep = jnp.transpose(kfull[:, idx], (0, 1, 3, 2, 4)).reshape(g, t * cg, t * cg)

    def state_in(wd, flip):
        wd = wd[:, :, ::-1] if flip else wd
        m = jnp.transpose(wd, (0, 2, 3, 1)).reshape(g, t * cg, p)
        return jnp.concatenate([jnp.real(m), jnp.imag(m)], axis=-1)

    def state_out(cd, pw):
        n = cd[:, :, :, None] * pw[:, None, :, :]
        n = jnp.transpose(n, (0, 2, 3, 1)).reshape(g, p, t * cg)
        return jnp.concatenate([jnp.real(n), -jnp.imag(n)], axis=1)

    m_f = state_in(w[0], True)
    m_b = state_in(w[1], False)
    n_f = state_out(cm[0], apow[0][..., 1:])
    n_b = state_out(cm[1], apow[1][..., :0:-1])
    at = apow[..., t]
    a1 = jnp.concatenate([jnp.real(at), jnp.real(at)], axis=-1)[:, :, None, :]
    a2 = jnp.concatenate([-jnp.imag(at), jnp.imag(at)], axis=-1)[:, :, None, :]
    bf = lambda m: m.astype(BF16)
    return dict(toep=bf(toep), m_f=bf(m_f), m_b=bf(m_b), n_f=bf(n_f), n_b=bf(n_b), a1=a1, a2=a2)


def _s5_body(u_ref, dsk_ref, toep_ref, mf_ref, mb_ref, nf_ref, nb_ref, a1_ref, a2_ref, x0_ref,
             y_ref, xfin_ref, z_scr, xin_scr):
    gb, nc = u_ref.shape[1], u_ref.shape[2]
    half = LANES // 2
    for gi in range(gb):
        ub = u_ref[0, gi].astype(BF16)
        z_scr[0, :, gi, :] = jnp.dot(ub, mf_ref[gi], preferred_element_type=F32)
        z_scr[1, :, gi, :] = jnp.dot(ub, mb_ref[gi], preferred_element_type=F32)

    def scan(j, xs):
        out = []
        for dr in range(2):
            jj = j if dr == 0 else nc - 1 - j
            x = xs[dr]
            xin_scr[dr, jj] = x
            out.append(a1_ref[dr, :, 0, :] * x + a2_ref[dr, :, 0, :] * pltpu.roll(x, half, 1) + z_scr[dr, jj])
        return tuple(out)

    xf, xb = lax.fori_loop(0, nc, scan, (x0_ref[0, 0], x0_ref[0, 1]))
    xfin_ref[0, 0] = xf
    xfin_ref[0, 1] = xb
    for gi in range(gb):
        u = u_ref[0, gi]
        y = jnp.dot(u.astype(BF16), toep_ref[gi], preferred_element_type=F32) + u * dsk_ref[gi]
        y = y + jnp.dot(xin_scr[0, :, gi, :].astype(BF16), nf_ref[gi], preferred_element_type=F32)
        y = y + jnp.dot(xin_scr[1, :, gi, :].astype(BF16), nb_ref[gi], preferred_element_type=F32)
        y_ref[0, gi] = y


def _s5_scan(h, ops, dsk, x0):
    b, l, d = h.shape
    g = ops["toep"].shape[0]
    cg = d // g
    t = S5_CHUNK
    nc = l // t
    tc = t * cg
    gb = S5_GROUP_BLOCK
    u = jnp.transpose(h.reshape(b, nc, t, g, cg), (0, 3, 1, 2, 4)).reshape(b, g, nc, tc)
    per_g = lambda shape: pl.BlockSpec((gb,) + shape, lambda j, i: (j,) + (0,) * len(shape))
    y, xfin = pl.pallas_call(
        _s5_body,
        grid=(g // gb, b),
        in_specs=[pl.BlockSpec((1, gb, nc, tc), lambda j, i: (i, j, 0, 0)),
                  per_g((1, tc)), per_g((tc, tc)), per_g((tc, LANES)), per_g((tc, LANES)),
                  per_g((LANES, tc)), per_g((LANES, tc)),
                  pl.BlockSpec((2, gb, 1, LANES), lambda j, i: (0, j, 0, 0)),
                  pl.BlockSpec((2, gb, 1, LANES), lambda j, i: (0, j, 0, 0)),
                  pl.BlockSpec((1, 2, gb, LANES), lambda j, i: (i, 0, j, 0))],
        out_specs=[pl.BlockSpec((1, gb, nc, tc), lambda j, i: (i, j, 0, 0)),
                   pl.BlockSpec((1, 2, gb, LANES), lambda j, i: (i, 0, j, 0))],
        out_shape=[jax.ShapeDtypeStruct((b, g, nc, tc), F32), jax.ShapeDtypeStruct((b, 2, g, LANES), F32)],
        scratch_shapes=[pltpu.VMEM((2, nc, gb, LANES), F32), pltpu.VMEM((2, nc, gb, LANES), F32)],
        compiler_params=_params("parallel", "parallel"),
        name="s5_scan",
    )(u, dsk, ops["toep"], ops["m_f"], ops["m_b"], ops["n_f"], ops["n_b"], ops["a1"], ops["a2"], x0)
    y = jnp.transpose(y.reshape(b, g, nc, t, cg), (0, 2, 3, 1, 4)).reshape(b, l, d)
    return y, xfin


def _s5_mixer(x_c, x_l, ng, mod_c, mod_l, lam_re, lam_im, log_dt, b_re, b_im, c_re, c_im, d_skip, w_glu, g1,
              want_ctx):
    b, l, d = x_l.shape
    g, p = lam_re.shape[1], lam_re.shape[2]
    cg = d // g
    assert 2 * p == LANES and g % S5_GROUP_BLOCK == 0
    ops = _s5_operators(lam_re, lam_im, log_dt, b_re, b_im, c_re, c_im, S5_CHUNK)
    dsk = jnp.tile(d_skip.astype(F32).reshape(g, 1, cg), (1, 1, S5_CHUNK))
    h_c = _norm_mod_only(x_c, ng, mod_c[0], mod_c[1])
    h_l = _norm_mod_only(x_l, ng, mod_l[0], mod_l[1])
    y_c, x_ctx = _s5_scan(h_c, ops, dsk, jnp.zeros((b, 2, g, LANES), F32))
    y_l, _ = _s5_scan(h_l, ops, dsk, x_ctx)
    new_l = _out_call(_out_glu_body, [(y_l, 0, d)], [w_glu], x_l, mod_l[2], g1, "s5_glu_out")
    new_c = None
    if want_ctx:
        new_c = _out_call(_out_glu_body, [(y_c, 0, d)], [w_glu], x_c, mod_c[2], g1, "s5_glu_out")
    return new_c, new_l


FFN_HALO = 16
FFN_TN = 256


def _ffn_body(xp_ref, x_ref, xn_ref, g2_ref, sh_ref, sc_ref, wa_ref, wv_ref, cwa_ref, cwv_ref, cba_ref, cbv_ref,
              wo_ref, gate_ref, g3_ref, o_ref, h_scr, acc_scr, *, per):
    i, j = pl.program_id(0), pl.program_id(1)
    tm = x_ref.shape[0]
    hl = FFN_HALO

    @pl.when(j == 0)
    def _():
        g, sh, sc = g2_ref[...], sh_ref[0], sc_ref[0]
        keep_p = ((i % per) != 0).astype(F32)
        keep_n = ((i % per) != per - 1).astype(F32)
        h_scr[0:hl] = (_norm_mod(xp_ref[...], g, sh, sc) * keep_p).astype(BF16)
        h_scr[hl:hl + tm] = _norm_mod(x_ref[...], g, sh, sc).astype(BF16)
        h_scr[hl + tm:] = (_norm_mod(xn_ref[...], g, sh, sc) * keep_n).astype(BF16)

    h = h_scr[...]
    rows = tm + 2 * hl

    def conv(w_ref, cw_ref, cb_ref):
        u = jnp.dot(h, w_ref[...], preferred_element_type=F32)
        um = pltpu.roll(u, 1, 0)[hl:hl + tm]
        up = pltpu.roll(u, rows - 1, 0)[hl:hl + tm]
        return cb_ref[...] + um * cw_ref[0:1] + u[hl:hl + tm] * cw_ref[1:2] + up * cw_ref[2:3]

    a = conv(wa_ref, cwa_ref, cba_ref)
    v = conv(wv_ref, cwv_ref, cbv_ref)
    part = jnp.dot((a * jax.nn.sigmoid(a) * v).astype(BF16), wo_ref[...], preferred_element_type=F32)

    @pl.when(j == 0)
    def _():
        acc_scr[...] = part

    @pl.when(j > 0)
    def _():
        acc_scr[...] += part

    @pl.when(j == pl.num_programs(1) - 1)
    def _():
        o_ref[...] = x_ref[...] + gate_ref[0] * _rms(acc_scr[...], g3_ref[...])


def _conv_ffn_block(x, ng2, mod, w_in, conv_w, conv_b, w_out, ng3):
    b, l, d = x.shape
    f = w_out.shape[0]
    tn, hl = FFN_TN, FFN_HALO
    assert f % tn == 0 and conv_w.shape[0] == 3 and w_in.shape[1] == 2 * f
    nj = f // tn
    tm = _row_tile(l, 1024)
    per = l // tm
    hb = tm // hl
    last_hb = b * l // hl - 1
    cb = conv_b.reshape(1, 2 * f)
    x2 = x.reshape(b * l, d)
    out = pl.pallas_call(
        functools.partial(_ffn_body, per=per),
        grid=(b * per, nj),
        in_specs=[pl.BlockSpec((hl, d), lambda i, j: (jnp.maximum(i * hb - 1, 0), 0)),
                  pl.BlockSpec((tm, d), lambda i, j: (i, 0)),
                  pl.BlockSpec((hl, d), lambda i, j: (jnp.minimum((i + 1) * hb, last_hb), 0)),
                  pl.BlockSpec((1, d), lambda i, j: (0, 0)),
                  pl.BlockSpec((1, 1, d), lambda i, j: (i // per, 0, 0)),
                  pl.BlockSpec((1, 1, d), lambda i, j: (i // per, 0, 0)),
                  pl.BlockSpec((d, tn), lambda i, j: (0, j)),
                  pl.BlockSpec((d, tn), lambda i, j: (0, nj + j)),
                  pl.BlockSpec((3, tn), lambda i, j: (0, j)),
                  pl.BlockSpec((3, tn), lambda i, j: (0, nj + j)),
                  pl.BlockSpec((1, tn), lambda i, j: (0, j)),
                  pl.BlockSpec((1, tn), lambda i, j: (0, nj + j)),
                  pl.BlockSpec((tn, d), lambda i, j: (j, 0)),
                  pl.BlockSpec((1, 1, d), lambda i, j: (i // per, 0, 0)),
                  pl.BlockSpec((1, d), lambda i, j: (0, 0))],
        out_specs=pl.BlockSpec((tm, d), lambda i, j: (i, 0)),
        out_shape=jax.ShapeDtypeStruct((b * l, d), F32),
        scratch_shapes=[pltpu.VMEM((tm + 2 * hl, d), BF16), pltpu.VMEM((tm, d), F32)],
        compiler_params=_params("parallel", "arbitrary"),
        name="conv_ffn",
    )(x2, x2, x2, ng2.reshape(1, d), mod[3], mod[4], w_in, w_in, conv_w, conv_w, cb, cb, w_out, mod[5],
      ng3.reshape(1, d))
    return out.reshape(b, l, d)


def kernel(x, c, ctx, c_ctx, w_mod, b_mod, norm_g, a_w_in, a_lower_logits, a_out_g, a_w_out, b_w_qkv, b_rpb,
           b_w_out, c_lam_re, c_lam_im, c_log_dt, c_b_re, c_b_im, c_c_re, c_c_im, c_d, c_w_glu, f_w_in,
           f_conv_w, f_conv_b, f_w_out):
    bsz, _, d = x.shape
    depth = w_mod.shape[0]
    p = jax.nn.softmax(a_lower_logits.astype(F32), axis=0)
    lower = jnp.cumsum(p, axis=0) - p[0]
    n_rows = -(-(bsz + 1) // SUBLANES) * SUBLANES
    c_rows = jnp.zeros((n_rows, d), F32).at[:bsz].set(c).at[bsz].set(c_ctx)
    mods = _modulation(c_rows, w_mod, b_mod)
    wb = lambda w: w.astype(BF16)
    lat, cx = x, ctx
    for i in range(depth):
        kind, j = i % N_MIXERS, i // N_MIXERS
        last = i == depth - 1
        m = mods[i].reshape(n_rows, N_MOD, d)
        mod_l = [m[:bsz, k][:, None, :] for k in range(N_MOD)]
        mod_c = [jnp.broadcast_to(m[bsz, k][None, None, :], (bsz, 1, d)) for k in range(N_MOD)]
        ng = norm_g[i]
        if kind == 0:
            cx1, lat = _hgrn2_mixer(cx, lat, ng[0], mod_c, mod_l, wb(a_w_in[j]), lower[i], a_out_g[j],
                                    wb(a_w_out[j]), ng[1], not last)
        elif kind == 1:
            cx1, lat = _na_mixer(cx, lat, ng[0], mod_c, mod_l, wb(b_w_qkv[j]), b_rpb[j], wb(b_w_out[j]), ng[1],
                                 not last)
        else:
            cx1, lat = _s5_mixer(cx, lat, ng[0], mod_c, mod_l, c_lam_re[j], c_lam_im[j], c_log_dt[j], c_b_re[j],
                                 c_b_im[j], c_c_re[j], c_c_im[j], c_d[j], wb(c_w_glu[j]), ng[1], not last)
        lat = _conv_ffn_block(lat, ng[2], mod_l, wb(f_w_in[i]), f_conv_w[i], f_conv_b[i], wb(f_w_out[i]), ng[3])
        if not last:
            cx = _conv_ffn_block(cx1, ng[2], mod_c, wb(f_w_in[i]), f_conv_w[i], f_conv_b[i], wb(f_w_out[i]), ng[3])
    return lat
```

```python
import functools

import jax
import jax.numpy as jnp
from jax import lax
from jax.experimental import pallas as pl
from jax.experimental.pallas import tpu as pltpu

F32 = jnp.float32
BF16 = jnp.bfloat16

EPS = 1e-6
F_MIN = 1e-30
N_MOD = 6
N_MIXERS = 3
A_HEADS = 8
GRID_W = 64
S5_CHUNK = 32
NEG_BIG = -1e30
NA_ROW_BLOCK = 4
SUBLANES = 8
LANES = 128
VMEM_LIMIT_BYTES = 48 * 1024 * 1024

NT_DIMS = (((1,), (1,)), ((), ()))
TN_DIMS = (((0,), (0,)), ((), ()))


def _params(*sem):
    return pltpu.CompilerParams(dimension_semantics=sem, vmem_limit_bytes=VMEM_LIMIT_BYTES)


def _row_tile(n, want):
    t = min(n, want)
    assert n % t == 0, (n, t)
    return t


def _rms(y, g):
    return y * lax.rsqrt(jnp.mean(y * y, axis=-1, keepdims=True) + EPS) * g


def _mod_body(c_ref, w_ref, b_ref, o_ref):
    c = c_ref[...]
    s = (c * jax.nn.sigmoid(c)).astype(BF16)
    o_ref[0] = jnp.dot(s, w_ref[0].astype(BF16), preferred_element_type=F32) + b_ref[0]


def _modulation(c_rows, w_mod, b_mod):
    depth, d, n = w_mod.shape
    r = c_rows.shape[0]
    tn = n // 4
    return pl.pallas_call(
        _mod_body,
        grid=(depth, n // tn),
        in_specs=[pl.BlockSpec((r, d), lambda i, j: (0, 0)),
                  pl.BlockSpec((1, d, tn), lambda i, j: (i, 0, j)),
                  pl.BlockSpec((1, 1, tn), lambda i, j: (i, 0, j))],
        out_specs=pl.BlockSpec((1, r, tn), lambda i, j: (i, 0, j)),
        out_shape=jax.ShapeDtypeStruct((depth, r, n), F32),
        compiler_params=_params("parallel", "parallel"),
        name="adaln_mod",
    )(c_rows, w_mod, b_mod.reshape(depth, 1, n))


def _norm_mod(x, g, sh, sc):
    return _rms(x, g) * (1.0 + sc) + sh


def _proj_body(x_ref, g_ref, sh_ref, sc_ref, w_ref, o_ref, h_scr):
    @pl.when(pl.program_id(1) == 0)
    def _():
        h_scr[...] = _norm_mod(x_ref[...], g_ref[...], sh_ref[0], sc_ref[0]).astype(BF16)

    o_ref[...] = jnp.dot(h_scr[...], w_ref[...], preferred_element_type=F32).astype(o_ref.dtype)


def _norm_mod_proj(x, g, sh, sc, w, out_dtype):
    b, l, d = x.shape
    n = w.shape[1]
    tm = _row_tile(l, 1024)
    tn = _row_tile(n, 512)
    per = l // tm
    out = pl.pallas_call(
        _proj_body,
        grid=(b * per, n // tn),
        in_specs=[pl.BlockSpec((tm, d), lambda i, j: (i, 0)),
                  pl.BlockSpec((1, d), lambda i, j: (0, 0)),
                  pl.BlockSpec((1, 1, d), lambda i, j: (i // per, 0, 0)),
                  pl.BlockSpec((1, 1, d), lambda i, j: (i // per, 0, 0)),
                  pl.BlockSpec((d, tn), lambda i, j: (0, j))],
        out_specs=pl.BlockSpec((tm, tn), lambda i, j: (i, j)),
        out_shape=jax.ShapeDtypeStruct((b * l, n), out_dtype),
        scratch_shapes=[pltpu.VMEM((tm, d), BF16)],
        compiler_params=_params("parallel", "arbitrary"),
        name="norm_mod_proj",
    )(x.reshape(b * l, d), g.reshape(1, d), sh, sc, w)
    return out.reshape(b, l, n)


def _norm_only_body(x_ref, g_ref, sh_ref, sc_ref, o_ref):
    o_ref[...] = _norm_mod(x_ref[...], g_ref[...], sh_ref[0], sc_ref[0])


def _norm_mod_only(x, g, sh, sc):
    b, l, d = x.shape
    tm = _row_tile(l, 512)
    per = l // tm
    out = pl.pallas_call(
        _norm_only_body,
        grid=(b * per,),
        in_specs=[pl.BlockSpec((tm, d), lambda i: (i, 0)),
                  pl.BlockSpec((1, d), lambda i: (0, 0)),
                  pl.BlockSpec((1, 1, d), lambda i: (i // per, 0, 0)),
                  pl.BlockSpec((1, 1, d), lambda i: (i // per, 0, 0))],
        out_specs=pl.BlockSpec((tm, d), lambda i: (i, 0)),
        out_shape=jax.ShapeDtypeStruct((b * l, d), F32),
        compiler_params=_params("parallel"),
        name="norm_mod",
    )(x.reshape(b * l, d), g.reshape(1, d), sh, sc)
    return out.reshape(b, l, d)


def _residual(res_ref, gate_ref, g1_ref, y):
    return res_ref[...] + gate_ref[0] * _rms(y, g1_ref[...])


def _out_plain_body(a_ref, w_ref, res_ref, gate_ref, g1_ref, o_ref):
    y = jnp.dot(a_ref[...], w_ref[...], preferred_element_type=F32)
    o_ref[...] = _residual(res_ref, gate_ref, g1_ref, y)


def _out_hgrn_body(of_ref, ob_ref, gp_ref, og_ref, w_ref, res_ref, gate_ref, g1_ref, o_ref):
    o = of_ref[...] + ob_ref[...]
    parts = []
    for h in range(A_HEADS):
        oh = o[:, h * LANES:(h + 1) * LANES]
        parts.append(oh * lax.rsqrt(jnp.mean(oh * oh, axis=-1, keepdims=True) + EPS))
    gp = gp_ref[...]
    a = jnp.concatenate(parts, axis=-1) * og_ref[...] * (gp * jax.nn.sigmoid(gp))
    y = jnp.dot(a.astype(BF16), w_ref[...], preferred_element_type=F32)
    o_ref[...] = _residual(res_ref, gate_ref, g1_ref, y)


def _gelu_tanh(y):
    return 0.5 * y * (1.0 + jnp.tanh(0.7978845608028654 * (y + 0.044715 * (y * y * y))))


def _out_glu_body(y_ref, w_ref, res_ref, gate_ref, g1_ref, o_ref):
    d = o_ref.shape[-1]
    a = _gelu_tanh(y_ref[...]).astype(BF16)
    ag = jnp.dot(a, w_ref[...], preferred_element_type=F32)
    y = ag[:, :d] * jax.nn.sigmoid(ag[:, d:])
    o_ref[...] = _residual(res_ref, gate_ref, g1_ref, y)


def _out_call(body, row_inputs, const_inputs, res, gate, g1, name):
    b, l, d = res.shape
    tm = _row_tile(l, 256)
    per = l // tm
    in_specs, args = [], []
    for arr, blk, width in row_inputs:
        in_specs.append(pl.BlockSpec((tm, width), lambda i, blk=blk: (i, blk)))
        args.append(arr.reshape(b * l, arr.shape[-1]))
    for arr in const_inputs:
        in_specs.append(pl.BlockSpec(arr.shape, lambda i, nd=arr.ndim: (0,) * nd))
        args.append(arr)
    in_specs += [pl.BlockSpec((tm, d), lambda i: (i, 0)),
                 pl.BlockSpec((1, 1, d), lambda i: (i // per, 0, 0)),
                 pl.BlockSpec((1, d), lambda i: (0, 0))]
    args += [res.reshape(b * l, d), gate, g1.reshape(1, d)]
    out = pl.pallas_call(
        body,
        grid=(b * per,),
        in_specs=in_specs,
        out_specs=pl.BlockSpec((tm, d), lambda i: (i, 0)),
        out_shape=jax.ShapeDtypeStruct((b * l, d), F32),
        compiler_params=_params("parallel"),
        name=name,
    )(*args)
    return out.reshape(b, l, d)


HGRN_CHUNK = 32
HGRN_SAFE_MIN = 1e-30
HGRN_Q_HEADROOM = 1e37


def _hgrn_gates(z, lb):
    f = lb + (1.0 - lb) * jax.nn.sigmoid(z)
    return jnp.maximum(f, F_MIN), 1.0 - f


def _hgrn_chunk(q, v, z, lb, st, reverse):
    c = q.shape[0]
    half = c // 2
    fm, kk = _hgrn_gates(z, lb)
    pos = lax.broadcasted_iota(jnp.int32, (c, LANES), 0)
    if reverse:
        pos = (c - 1) - pos
    second = pos >= half
    hpos = jnp.where(second, pos - half, pos)
    ph = fm
    j = 1
    while j < half:
        ph = ph * jnp.where(hpos >= j, pltpu.roll(ph, (c - j) if reverse else j, 0), 1.0)
        j *= 2
    a_last = ph[half:half + 1] if reverse else ph[half - 1:half]
    r_last = ph[0:1] if reverse else ph[c - 1:c]
    p_mid = a_last
    p_last = a_last * r_last
    pp = jnp.where(second, ph * a_last, ph)
    r = jnp.where(second, ph, ph * (1.0 / a_last))
    kd = kk * (1.0 / r)
    sc = lax.dot_general((q * r).astype(BF16), kd.astype(BF16), NT_DIMS, preferred_element_type=F32)
    row = lax.broadcasted_iota(jnp.int32, (c, c), 0)
    col = lax.broadcasted_iota(jnp.int32, (c, c), 1)
    sc = jnp.where((col >= row) if reverse else (col <= row), sc, 0.0)
    vb = v.astype(BF16)
    o = jnp.dot(sc.astype(BF16), vb, preferred_element_type=F32)
    o = o + lax.dot_general((q * pp).astype(BF16), st.astype(BF16), NT_DIMS, preferred_element_type=F32)
    u = lax.dot_general(vb, (kd * r_last).astype(BF16), TN_DIMS, preferred_element_type=F32)
    q_max = jnp.max(jnp.abs(q), axis=0, keepdims=True)
    ok = jnp.where(p_mid >= HGRN_SAFE_MIN, r_last, 0.0) >= HGRN_SAFE_MIN
    ok = jnp.where(ok, p_mid * HGRN_Q_HEADROOM, -1.0) >= q_max
    return o, p_last * st + u, jnp.where(ok, 0.0, 1.0)


def _hgrn_exact_block(q_ref, v_ref, z_ref, lb_ref, o_ref, st_scr, dr, reverse, n_tiles):
    pos = lax.broadcasted_iota(jnp.int32, (SUBLANES, LANES), 0)
    if reverse:
        pos = (SUBLANES - 1) - pos

    def earlier(x, j):
        if j == 0:
            return x
        return pltpu.roll(x, (SUBLANES - j) if reverse else j, 0)

    def later(x, j):
        return pltpu.roll(x, j if reverse else (SUBLANES - j), 0)

    def tile(i, carry):
        ti = (n_tiles - 1 - i) if reverse else i
        r0 = pl.multiple_of(ti * SUBLANES, SUBLANES)
        for h in range(A_HEADS):
            sl = slice(h * LANES, (h + 1) * LANES)
            lb = lb_ref[:, sl]
            z = z_ref[0, pl.ds(r0, SUBLANES), sl]
            q = q_ref[0, pl.ds(r0, SUBLANES), sl]
            v = v_ref[0, pl.ds(r0, SUBLANES), sl]
            fm, kk = _hgrn_gates(z, lb)
            pp = fm
            for j in (1, 2, 4):
                pp = pp * jnp.where(pos >= j, earlier(pp, j), 1.0)
            qq = jnp.where(pos <= SUBLANES - 2, later(fm, 1), 1.0)
            for j in (1, 2, 4):
                qq = qq * jnp.where(pos <= SUBLANES - 1 - j, later(qq, j), 1.0)
            dec = pp[0:1] if reverse else pp[SUBLANES - 1:SUBLANES]
            st = st_scr[dr, h]
            o = lax.dot_general((q * pp).astype(BF16), st.astype(BF16), NT_DIMS,
                                preferred_element_type=F32)
            g = fm
            for d in range(SUBLANES):
                if d == 0:
                    e = q * kk
                else:
                    if d > 1:
                        g = g * earlier(fm, d - 1)
                    e = q * g * jnp.where(pos >= d, earlier(kk, d), 0.0)
                o = o + jnp.sum(e, axis=-1, keepdims=True) * earlier(v, d)
            o_ref[0, pl.ds(r0, SUBLANES), sl] = o
            u = lax.dot_general(v.astype(BF16), (kk * qq).astype(BF16), TN_DIMS,
                                preferred_element_type=F32)
            st_scr[dr, h] = dec * st + u
        return carry

    lax.fori_loop(0, n_tiles, tile, 0)


def _hgrn_scan_body(qf_ref, vf_ref, zf_ref, qb_ref, vb_ref, zb_ref, lb_ref, s0_ref, of_ref, ob_ref, sfin_ref,
                    st_scr, save_scr, *, n_chunks):
    step = pl.program_id(1)

    @pl.when(step == 0)
    def _():
        st_scr[...] = s0_ref[0]

    save_scr[...] = st_scr[...]
    dirs = ((qf_ref, vf_ref, zf_ref, of_ref, False), (qb_ref, vb_ref, zb_ref, ob_ref, True))
    c = HGRN_CHUNK

    def chunk(i, bad):
        for dr, (q_ref, v_ref, z_ref, o_ref, reverse) in enumerate(dirs):
            r0 = pl.multiple_of(((n_chunks - 1 - i) if reverse else i) * c, c)
            for h in range(A_HEADS):
                sl = slice(h * LANES, (h + 1) * LANES)
                o, st_new, bad_h = _hgrn_chunk(q_ref[0, pl.ds(r0, c), sl], v_ref[0, pl.ds(r0, c), sl],
                                               z_ref[0, pl.ds(r0, c), sl], lb_ref[:, sl], st_scr[dr, h], reverse)
                o_ref[0, pl.ds(r0, c), sl] = o
                st_scr[dr, h] = st_new
                bad = jnp.maximum(bad, bad_h)
        return bad

    bad = lax.fori_loop(0, n_chunks, chunk, jnp.zeros((1, LANES), F32))

    @pl.when(jnp.max(bad) > 0.0)
    def _():
        st_scr[...] = save_scr[...]
        for dr, (q_ref, v_ref, z_ref, o_ref, reverse) in enumerate(dirs):
            _hgrn_exact_block(q_ref, v_ref, z_ref, lb_ref, o_ref, st_scr, dr, reverse, n_chunks * c // SUBLANES)

    @pl.when(step == pl.num_programs(1) - 1)
    def _():
        sfin_ref[0] = st_scr[...]


def _hgrn_scan(proj, lb, s0):
    b, l, _ = proj.shape
    d = A_HEADS * LANES
    t = _row_tile(l, 256)
    nb = l // t
    fwd = lambda blk: pl.BlockSpec((1, t, d), lambda i, s: (i, s, blk))
    bwd = lambda blk: pl.BlockSpec((1, t, d), lambda i, s: (i, nb - 1 - s, blk))
    st_spec = pl.BlockSpec((1,) + s0.shape[1:], lambda i, s: (i, 0, 0, 0, 0))
    return pl.pallas_call(
        functools.partial(_hgrn_scan_body, n_chunks=t // HGRN_CHUNK),
        grid=(b, nb),
        in_specs=[fwd(0), fwd(1), fwd(3), bwd(0), bwd(1), bwd(4),
                  pl.BlockSpec((1, d), lambda i, s: (0, 0)), st_spec],
        out_specs=[fwd(0), bwd(0), st_spec],
        out_shape=[jax.ShapeDtypeStruct((b, l, d), F32), jax.ShapeDtypeStruct((b, l, d), F32),
                   jax.ShapeDtypeStruct(s0.shape, F32)],
        scratch_shapes=[pltpu.VMEM(s0.shape[1:], F32), pltpu.VMEM(s0.shape[1:], F32)],
        compiler_params=_params("parallel", "arbitrary"),
        name="hgrn_scan",
    )(proj, proj, proj, proj, proj, proj, lb, s0)


def _hgrn2_mixer(x_c, x_l, ng, mod_c, mod_l, w_in, lower, out_g, w_out, g1, want_ctx):
    d = x_l.shape[-1]
    p_c = _norm_mod_proj(x_c, ng, mod_c[0], mod_c[1], w_in, F32)
    p_l = _norm_mod_proj(x_l, ng, mod_l[0], mod_l[1], w_in, F32)
    lb = lower.reshape(1, d)
    zero = jnp.zeros((x_l.shape[0], 2, A_HEADS, LANES, LANES), F32)
    ocf, ocb, s_ctx = _hgrn_scan(p_c, lb, zero)
    olf, olb, _ = _hgrn_scan(p_l, lb, s_ctx)
    o_c, o_l = (ocf, ocb), (olf, olb)
    og = out_g.reshape(1, d)
    new_l = _out_call(_out_hgrn_body, [(o_l[0], 0, d), (o_l[1], 0, d), (p_l, 2, d)], [og, w_out],
                      x_l, mod_l[2], g1, "hgrn_out")
    new_c = None
    if want_ctx:
        new_c = _out_call(_out_hgrn_body, [(o_c[0], 0, d), (o_c[1], 0, d), (p_c, 2, d)], [og, w_out],
                          x_c, mod_c[2], g1, "hgrn_out")
    return new_c, new_l


def _softmax_pv(s_list, v_list):
    mx = functools.reduce(jnp.maximum, [jnp.max(s, axis=-1, keepdims=True) for s in s_list])
    ps = [jnp.exp(s - mx) for s in s_list]
    den = functools.reduce(jnp.add, [jnp.sum(p, axis=-1, keepdims=True) for p in ps])
    acc = functools.reduce(jnp.add, [jnp.dot(p.astype(BF16), v, preferred_element_type=F32)
                                     for p, v in zip(ps, v_list)])
    return acc / den


def _na_lat_body(q_ref, k_ref, v_ref, kc_ref, vc_ref, t2_ref, o_ref, *, rows, kh, scale):
    w = GRID_W
    rb = NA_ROW_BLOCK
    nk = kh + rb
    n_rel = 2 * kh - 1
    first = lax.broadcasted_iota(jnp.int32, (rb * w, LANES), 1) < (LANES // 2)
    kc = kc_ref[0]
    vc = vc_ref[0]

    def bias_index(r, ks, m):
        r0 = jnp.clip(r - kh // 2, 0, rows - kh)
        ka = ks + 2 * m
        rel_a = ka - r + (kh - 1)
        in_a = (ka >= r0) & (ka < r0 + kh)
        in_b = (ka + 1 >= r0) & (ka + 1 < r0 + kh)
        both, only_b, only_a = rel_a, (n_rel - 1) + rel_a + 1, (2 * n_rel - 1) + rel_a
        return jnp.where(in_a, jnp.where(in_b, both, only_a), jnp.where(in_b, only_b, 3 * n_rel - 1))

    def block(i, carry):
        rq = i * rb
        ks = jnp.clip(rq - kh // 2, 0, rows - nk)
        q2 = q_ref[0, pl.ds(pl.multiple_of(rq * w, rb * w), rb * w), :] * scale
        k2 = k_ref[0, pl.ds(pl.multiple_of(ks * w, w), nk * w), :]
        v2 = v_ref[0, pl.ds(pl.multiple_of(ks * w, w), nk * w), :]
        outs = []
        for hh in range(2):
            qm = jnp.where(first if hh == 0 else jnp.logical_not(first), q2, jnp.zeros_like(q2))
            bias = jnp.concatenate(
                [jnp.concatenate([t2_ref[hh, bias_index(rq + a, ks, m)] for m in range(nk // 2)], axis=-1)
                 for a in range(rb)], axis=0)
            s_w = lax.dot_general(qm, k2, NT_DIMS, preferred_element_type=F32) + bias
            s_c = lax.dot_general(qm, kc, NT_DIMS, preferred_element_type=F32)
            outs.append(_softmax_pv([s_w, s_c], [v2, vc]))
        o_ref[0, pl.ds(pl.multiple_of(rq * w, rb * w), rb * w), :] = (
            jnp.where(first, outs[0], outs[1]).astype(o_ref.dtype))
        return carry

    lax.fori_loop(0, rows // rb, block, 0)


def _na_ctx_body(q_ref, k_ref, v_ref, o_ref, *, scale):
    n = q_ref.shape[1]
    first = lax.broadcasted_iota(jnp.int32, (n, LANES), 1) < (LANES // 2)
    q2 = q_ref[0] * scale
    k2 = k_ref[0]
    v2 = v_ref[0]
    outs = []
    for hh in range(2):
        qm = jnp.where(first if hh == 0 else jnp.logical_not(first), q2, jnp.zeros_like(q2))
        s = lax.dot_general(qm, k2, NT_DIMS, preferred_element_type=F32)
        outs.append(_softmax_pv([s], [v2]))
    o_ref[0] = jnp.where(first, outs[0], outs[1]).astype(o_ref.dtype)


def _na_bias_table(rpb, kh, kw):
    w = jnp.arange(GRID_W)[:, None]
    c = jnp.arange(GRID_W)[None, :]
    c0 = jnp.clip(w - kw // 2, 0, GRID_W - kw)
    inside = (c >= c0) & (c < c0 + kw)
    cb = jnp.clip(c - w + kw - 1, 0, 2 * kw - 2)
    t = jnp.where(inside[None, None], rpb.astype(F32)[:, :, cb], NEG_BIG)
    off = jnp.full_like(t, NEG_BIG)
    pair = lambda a, b: jnp.concatenate([a, b], axis=-1)
    return jnp.concatenate([pair(t[:, :-1], t[:, 1:]), pair(off, t), pair(t, off), pair(off, off)[:, :1]], axis=1)


def _na_mixer(x_c, x_l, ng, mod_c, mod_l, w_qkv, rpb, w_out, g1, want_ctx):
    b, l, d = x_l.shape
    lc = x_c.shape[1]
    heads = rpb.shape[0]
    na_rows, na_cols = (rpb.shape[1] + 1) // 2, (rpb.shape[2] + 1) // 2
    dh = d // heads
    assert 2 * dh == LANES and l % GRID_W == 0
    rows = l // GRID_W
    kh = min(na_rows, rows)
    assert kh == na_rows and (kh + NA_ROW_BLOCK) % 2 == 0
    assert rows % NA_ROW_BLOCK == 0 and rows >= kh + NA_ROW_BLOCK
    scale = dh ** -0.5
    nhp = heads // 2
    qkv_c = _norm_mod_proj(x_c, ng, mod_c[0], mod_c[1], w_qkv, BF16)
    qkv_l = _norm_mod_proj(x_l, ng, mod_l[0], mod_l[1], w_qkv, BF16)
    t2 = _na_bias_table(rpb, kh, na_cols)
    o_l = pl.pallas_call(
        functools.partial(_na_lat_body, rows=rows, kh=kh, scale=scale),
        grid=(b, nhp),
        in_specs=[pl.BlockSpec((1, l, LANES), lambda i, p: (i, 0, p)),
                  pl.BlockSpec((1, l, LANES), lambda i, p: (i, 0, nhp + p)),
                  pl.BlockSpec((1, l, LANES), lambda i, p: (i, 0, 2 * nhp + p)),
                  pl.BlockSpec((1, lc, LANES), lambda i, p: (i, 0, nhp + p)),
                  pl.BlockSpec((1, lc, LANES), lambda i, p: (i, 0, 2 * nhp + p)),
                  pl.BlockSpec((2,) + t2.shape[1:], lambda i, p: (p, 0, 0, 0))],
        out_specs=pl.BlockSpec((1, l, LANES), lambda i, p: (i, 0, p)),
        out_shape=jax.ShapeDtypeStruct((b, l, d), BF16),
        compiler_params=_params("parallel", "parallel"),
        name="na_latent",
    )(qkv_l, qkv_l, qkv_l, qkv_c, qkv_c, t2)
    new_l = _out_call(_out_plain_body, [(o_l, 0, d)], [w_out], x_l, mod_l[2], g1, "na_out")
    new_c = None
    if want_ctx:
        o_c = pl.pallas_call(
            functools.partial(_na_ctx_body, scale=scale),
            grid=(b, nhp),
            in_specs=[pl.BlockSpec((1, lc, LANES), lambda i, p: (i, 0, p)),
                      pl.BlockSpec((1, lc, LANES), lambda i, p: (i, 0, nhp + p)),
                      pl.BlockSpec((1, lc, LANES), lambda i, p: (i, 0, 2 * nhp + p))],
            out_specs=pl.BlockSpec((1, lc, LANES), lambda i, p: (i, 0, p)),
            out_shape=jax.ShapeDtypeStruct((b, lc, d), BF16),
            compiler_params=_params("parallel", "parallel"),
            name="na_context",
        )(qkv_c, qkv_c, qkv_c)
        new_c = _out_call(_out_plain_body, [(o_c, 0, d)], [w_out], x_c, mod_c[2], g1, "na_out")
    return new_c, new_l


S5_GROUP_BLOCK = 8


def _s5_kernel_body(c_ref, w_ref, o_ref):
    for i in range(c_ref.shape[0]):
        o_ref[i] = jnp.dot(c_ref[i], w_ref[i], preferred_element_type=F32, precision=lax.Precision.HIGHEST)


def _s5_impulse(cmat, wmat):
    n = cmat.shape[0]
    gb = S5_GROUP_BLOCK
    return pl.pallas_call(
        _s5_kernel_body,
        grid=(n // gb,),
        in_specs=[pl.BlockSpec((gb,) + cmat.shape[1:], lambda i: (i, 0, 0)),
                  pl.BlockSpec((gb,) + wmat.shape[1:], lambda i: (i, 0, 0))],
        out_specs=pl.BlockSpec((gb, cmat.shape[1], wmat.shape[2]), lambda i: (i, 0, 0)),
        out_shape=jax.ShapeDtypeStruct((n, cmat.shape[1], wmat.shape[2]), F32),
        compiler_params=_params("parallel"),
        name="s5_impulse",
    )(cmat, wmat)


def _s5_operators(lam_re, lam_im, log_dt, b_re, b_im, c_re, c_im, t):
    _, g, p = lam_re.shape
    cg = b_re.shape[-1]
    lam = lax.complex(lam_re.astype(F32), lam_im.astype(F32))
    ldt = lam * jnp.exp(log_dt.astype(F32))[..., None]
    a = jnp.exp(ldt)
    bbar = ((a - 1.0) / lam)[..., None] * lax.complex(b_re.astype(F32), b_im.astype(F32))
    cm = lax.complex(c_re.astype(F32), c_im.astype(F32))
    apow = jnp.exp(ldt[..., None] * jnp.arange(t + 1, dtype=F32))
    w = apow[..., :t, None] * bbar[:, :, :, None, :]
    wmat = jnp.concatenate([jnp.real(w), jnp.imag(w)], axis=2).reshape(2 * g, 2 * p, t * cg)
    cmat = jnp.concatenate([jnp.real(cm), -jnp.imag(cm)], axis=-1).reshape(2 * g, cg, 2 * p)
    k = _s5_impulse(cmat, wmat).reshape(2, g, cg, t, cg)
    k = jnp.transpose(k, (0, 1, 3, 4, 2))
    kfull = jnp.concatenate([k[1, :, :0:-1], (k[0, :, :1] + k[1, :, :1]), k[0, :, 1:]], axis=1)
    idx = jnp.arange(t)[None, :] - jnp.arange(t)[:, None] + (t - 1)
    toep = jnp.transpose(kfull[:, idx], (0, 1, 3, 2, 4)).reshape(g, t * cg, t * cg)

    def state_in(wd, flip):
        wd = wd[:, :, ::-1] if flip else wd
        m = jnp.transpose(wd, (0, 2, 3, 1)).reshape(g, t * cg, p)
        return jnp.concatenate([jnp.real(m), jnp.imag(m)], axis=-1)

    def state_out(cd, pw):
        n = cd[:, :, :, None] * pw[:, None, :, :]
        n = jnp.transpose(n, (0, 2, 3, 1)).reshape(g, p, t * cg)
        return jnp.concatenate([jnp.real(n), -jnp.imag(n)], axis=1)

    m_f = state_in(w[0], True)
    m_b = state_in(w[1], False)
    n_f = state_out(cm[0], apow[0][..., 1:])
    n_b = state_out(cm[1], apow[1][..., :0:-1])
    at = apow[..., t]
    a1 = jnp.concatenate([jnp.real(at), jnp.real(at)], axis=-1)[:, :, None, :]
    a2 = jnp.concatenate([-jnp.imag(at), jnp.imag(at)], axis=-1)[:, :, None, :]
    bf = lambda m: m.astype(BF16)
    return dict(toep=bf(toep), m_f=bf(m_f), m_b=bf(m_b), n_f=bf(n_f), n_b=bf(n_b), a1=a1, a2=a2)


def _s5_body(u_ref, dsk_ref, toep_ref, mf_ref, mb_ref, nf_ref, nb_ref, a1_ref, a2_ref, x0_ref,
             y_ref, xfin_ref, z_scr, xin_scr):
    gb, nc = u_ref.shape[1], u_ref.shape[2]
    half = LANES // 2
    for gi in range(gb):
        ub = u_ref[0, gi].astype(BF16)
        z_scr[0, :, gi, :] = jnp.dot(ub, mf_ref[gi], preferred_element_type=F32)
        z_scr[1, :, gi, :] = jnp.dot(ub, mb_ref[gi], preferred_element_type=F32)

    def scan(j, xs):
        out = []
        for dr in range(2):
            jj = j if dr == 0 else nc - 1 - j
            x = xs[dr]
            xin_scr[dr, jj] = x
            out.append(a1_ref[dr, :, 0, :] * x + a2_ref[dr, :, 0, :] * pltpu.roll(x, half, 1) + z_scr[dr, jj])
        return tuple(out)

    xf, xb = lax.fori_loop(0, nc, scan, (x0_ref[0, 0], x0_ref[0, 1]))
    xfin_ref[0, 0] = xf
    xfin_ref[0, 1] = xb
    for gi in range(gb):
        u = u_ref[0, gi]
        y = jnp.dot(u.astype(BF16), toep_ref[gi], preferred_element_type=F32) + u * dsk_ref[gi]
        y = y + jnp.dot(xin_scr[0, :, gi, :].astype(BF16), nf_ref[gi], preferred_element_type=F32)
        y = y + jnp.dot(xin_scr[1, :, gi, :].astype(BF16), nb_ref[gi], preferred_element_type=F32)
        y_ref[0, gi] = y


def _s5_scan(h, ops, dsk, x0):
    b, l, d = h.shape
    g = ops["toep"].shape[0]
    cg = d // g
    t = S5_CHUNK
    nc = l // t
    tc = t * cg
    gb = S5_GROUP_BLOCK
    u = jnp.transpose(h.reshape(b, nc, t, g, cg), (0, 3, 1, 2, 4)).reshape(b, g, nc, tc)
    per_g = lambda shape: pl.BlockSpec((gb,) + shape, lambda j, i: (j,) + (0,) * len(shape))
    y, xfin = pl.pallas_call(
        _s5_body,
        grid=(g // gb, b),
        in_specs=[pl.BlockSpec((1, gb, nc, tc), lambda j, i: (i, j, 0, 0)),
                  per_g((1, tc)), per_g((tc, tc)), per_g((tc, LANES)), per_g((tc, LANES)),
                  per_g((LANES, tc)), per_g((LANES, tc)),
                  pl.BlockSpec((2, gb, 1, LANES), lambda j, i: (0, j, 0, 0)),
                  pl.BlockSpec((2, gb, 1, LANES), lambda j, i: (0, j, 0, 0)),
                  pl.BlockSpec((1, 2, gb, LANES), lambda j, i: (i, 0, j, 0))],
        out_specs=[pl.BlockSpec((1, gb, nc, tc), lambda j, i: (i, j, 0, 0)),
                   pl.BlockSpec((1, 2, gb, LANES), lambda j, i: (i, 0, j, 0))],
        out_shape=[jax.ShapeDtypeStruct((b, g, nc, tc), F32), jax.ShapeDtypeStruct((b, 2, g, LANES), F32)],
        scratch_shapes=[pltpu.VMEM((2, nc, gb, LANES), F32), pltpu.VMEM((2, nc, gb, LANES), F32)],
        compiler_params=_params("parallel", "parallel"),
        name="s5_scan",
    )(u, dsk, ops["toep"], ops["m_f"], ops["m_b"], ops["n_f"], ops["n_b"], ops["a1"], ops["a2"], x0)
    y = jnp.transpose(y.reshape(b, g, nc, t, cg), (0, 2, 3, 1, 4)).reshape(b, l, d)
    return y, xfin


def _s5_mixer(x_c, x_l, ng, mod_c, mod_l, lam_re, lam_im, log_dt, b_re, b_im, c_re, c_im, d_skip, w_glu, g1,
              want_ctx):
    b, l, d = x_l.shape
    g, p = lam_re.shape[1], lam_re.shape[2]
    cg = d // g
    assert 2 * p == LANES and g % S5_GROUP_BLOCK == 0
    ops = _s5_operators(lam_re, lam_im, log_dt, b_re, b_im, c_re, c_im, S5_CHUNK)
    dsk = jnp.tile(d_skip.astype(F32).reshape(g, 1, cg), (1, 1, S5_CHUNK))
    h_c = _norm_mod_only(x_c, ng, mod_c[0], mod_c[1])
    h_l = _norm_mod_only(x_l, ng, mod_l[0], mod_l[1])
    y_c, x_ctx = _s5_scan(h_c, ops, dsk, jnp.zeros((b, 2, g, LANES), F32))
    y_l, _ = _s5_scan(h_l, ops, dsk, x_ctx)
    new_l = _out_call(_out_glu_body, [(y_l, 0, d)], [w_glu], x_l, mod_l[2], g1, "s5_glu_out")
    new_c = None
    if want_ctx:
        new_c = _out_call(_out_glu_body, [(y_c, 0, d)], [w_glu], x_c, mod_c[2], g1, "s5_glu_out")
    return new_c, new_l


FFN_HALO = 16
FFN_TN = 256


def _ffn_body(xp_ref, x_ref, xn_ref, g2_ref, sh_ref, sc_ref, wa_ref, wv_ref, cwa_ref, cwv_ref, cba_ref, cbv_ref,
              wo_ref, gate_ref, g3_ref, o_ref, h_scr, acc_scr, *, per):
    i, j = pl.program_id(0), pl.program_id(1)
    tm = x_ref.shape[0]
    hl = FFN_HALO

    @pl.when(j == 0)
    def _():
        g, sh, sc = g2_ref[...], sh_ref[0], sc_ref[0]
        keep_p = ((i % per) != 0).astype(F32)
        keep_n = ((i % per) != per - 1).astype(F32)
        h_scr[0:hl] = (_norm_mod(xp_ref[...], g, sh, sc) * keep_p).astype(BF16)
        h_scr[hl:hl + tm] = _norm_mod(x_ref[...], g, sh, sc).astype(BF16)
        h_scr[hl + tm:] = (_norm_mod(xn_ref[...], g, sh, sc) * keep_n).astype(BF16)

    h = h_scr[...]
    rows = tm + 2 * hl

    def conv(w_ref, cw_ref, cb_ref):
        u = jnp.dot(h, w_ref[...], preferred_element_type=F32)
        um = pltpu.roll(u, 1, 0)[hl:hl + tm]
        up = pltpu.roll(u, rows - 1, 0)[hl:hl + tm]
        return cb_ref[...] + um * cw_ref[0:1] + u[hl:hl + tm] * cw_ref[1:2] + up * cw_ref[2:3]

    a = conv(wa_ref, cwa_ref, cba_ref)
    v = conv(wv_ref, cwv_ref, cbv_ref)
    part = jnp.dot((a * jax.nn.sigmoid(a) * v).astype(BF16), wo_ref[...], preferred_element_type=F32)

    @pl.when(j == 0)
    def _():
        acc_scr[...] = part

    @pl.when(j > 0)
    def _():
        acc_scr[...] += part

    @pl.when(j == pl.num_programs(1) - 1)
    def _():
        o_ref[...] = x_ref[...] + gate_ref[0] * _rms(acc_scr[...], g3_ref[...])


def _conv_ffn_block(x, ng2, mod, w_in, conv_w, conv_b, w_out, ng3):
    b, l, d = x.shape
    f = w_out.shape[0]
    tn, hl = FFN_TN, FFN_HALO
    assert f % tn == 0 and conv_w.shape[0] == 3 and w_in.shape[1] == 2 * f
    nj = f // tn
    tm = _row_tile(l, 1024)
    per = l // tm
    hb = tm // hl
    last_hb = b * l // hl - 1
    cb = conv_b.reshape(1, 2 * f)
    x2 = x.reshape(b * l, d)
    out = pl.pallas_call(
        functools.partial(_ffn_body, per=per),
        grid=(b * per, nj),
        in_specs=[pl.BlockSpec((hl, d), lambda i, j: (jnp.maximum(i * hb - 1, 0), 0)),
                  pl.BlockSpec((tm, d), lambda i, j: (i, 0)),
                  pl.BlockSpec((hl, d), lambda i, j: (jnp.minimum((i + 1) * hb, last_hb), 0)),
                  pl.BlockSpec((1, d), lambda i, j: (0, 0)),
                  pl.BlockSpec((1, 1, d), lambda i, j: (i // per, 0, 0)),
                  pl.BlockSpec((1, 1, d), lambda i, j: (i // per, 0, 0)),
                  pl.BlockSpec((d, tn), lambda i, j: (0, j)),
                  pl.BlockSpec((d, tn), lambda i, j: (0, nj + j)),
                  pl.BlockSpec((3, tn), lambda i, j: (0, j)),
                  pl.BlockSpec((3, tn), lambda i, j: (0, nj + j)),
                  pl.BlockSpec((1, tn), lambda i, j: (0, j)),
                  pl.BlockSpec((1, tn), lambda i, j: (0, nj + j)),
                  pl.BlockSpec((tn, d), lambda i, j: (j, 0)),
                  pl.BlockSpec((1, 1, d), lambda i, j: (i // per, 0, 0)),
                  pl.BlockSpec((1, d), lambda i, j: (0, 0))],
        out_specs=pl.BlockSpec((tm, d), lambda i, j: (i, 0)),
        out_shape=jax.ShapeDtypeStruct((b * l, d), F32),
        scratch_shapes=[pltpu.VMEM((tm + 2 * hl, d), BF16), pltpu.VMEM((tm, d), F32)],
        compiler_params=_params("parallel", "arbitrary"),
        name="conv_ffn",
    )(x2, x2, x2, ng2.reshape(1, d), mod[3], mod[4], w_in, w_in, conv_w, conv_w, cb, cb, w_out, mod[5],
      ng3.reshape(1, d))
    return out.reshape(b, l, d)


def kernel(x, c, ctx, c_ctx, w_mod, b_mod, norm_g, a_w_in, a_lower_logits, a_out_g, a_w_out, b_w_qkv, b_rpb,
           b_w_out, c_lam_re, c_lam_im, c_log_dt, c_b_re, c_b_im, c_c_re, c_c_im, c_d, c_w_glu, f_w_in,
           f_conv_w, f_conv_b, f_w_out):
    bsz, _, d = x.shape
    depth = w_mod.shape[0]
    p = jax.nn.softmax(a_lower_logits.astype(F32), axis=0)
    lower = jnp.cumsum(p, axis=0) - p[0]
    n_rows = -(-(bsz + 1) // SUBLANES) * SUBLANES
    c_rows = jnp.zeros((n_rows, d), F32).at[:bsz].set(c).at[bsz].set(c_ctx)
    mods = _modulation(c_rows, w_mod, b_mod)
    wb = lambda w: w.astype(BF16)
    lat, cx = x, ctx
    for i in range(depth):
        kind, j = i % N_MIXERS, i // N_MIXERS
        last = i == depth - 1
        m = mods[i].reshape(n_rows, N_MOD, d)
        mod_l = [m[:bsz, k][:, None, :] for k in range(N_MOD)]
        mod_c = [jnp.broadcast_to(m[bsz, k][None, None, :], (bsz, 1, d)) for k in range(N_MOD)]
        ng = norm_g[i]
        if kind == 0:
            cx1, lat = _hgrn2_mixer(cx, lat, ng[0], mod_c, mod_l, wb(a_w_in[j]), lower[i], a_out_g[j],
                                    wb(a_w_out[j]), ng[1], not last)
        elif kind == 1:
            cx1, lat = _na_mixer(cx, lat, ng[0], mod_c, mod_l, wb(b_w_qkv[j]), b_rpb[j], wb(b_w_out[j]), ng[1],
                                 not last)
        else:
            cx1, lat = _s5_mixer(cx, lat, ng[0], mod_c, mod_l, c_lam_re[j], c_lam_im[j], c_log_dt[j], c_b_re[j],
                                 c_b_im[j], c_c_re[j], c_c_im[j], c_d[j], wb(c_w_glu[j]), ng[1], not last)
        lat = _conv_ffn_block(lat, ng[2], mod_l, wb(f_w_in[i]), f_conv_w[i], f_conv_b[i], wb(f_w_out[i]), ng[3])
        if not last:
            cx = _conv_ffn_block(cx1, ng[2], mod_c, wb(f_w_in[i]), f_conv_w[i], f_conv_b[i], wb(f_w_out[i]), ng[3])
    return lat
```

```python
import functools

import jax
import jax.numpy as jnp
from jax import lax
from jax.experimental import pallas as pl
from jax.experimental.pallas import tpu as pltpu

F32 = jnp.float32
BF16 = jnp.bfloat16

EPS = 1e-6
F_MIN = 1e-30
N_MOD = 6
N_MIXERS = 3
A_HEADS = 8
GRID_W = 64
S5_CHUNK = 32
NEG_BIG = -1e30
NA_ROW_BLOCK = 4
SUBLANES = 8
LANES = 128
VMEM_LIMIT_BYTES = 56 * 1024 * 1024

NT_DIMS = (((1,), (1,)), ((), ()))
TN_DIMS = (((0,), (0,)), ((), ()))


def _params(*sem):
    return pltpu.CompilerParams(dimension_semantics=sem, vmem_limit_bytes=VMEM_LIMIT_BYTES)


def _row_tile(n, want):
    t = min(n, want)
    assert n % t == 0, (n, t)
    return t


def _rms(y, g):
    return y * lax.rsqrt(jnp.mean(y * y, axis=-1, keepdims=True) + EPS) * g


def _mod_body(c_ref, w_ref, b_ref, o_ref):
    c = c_ref[...]
    s = (c * jax.nn.sigmoid(c)).astype(BF16)
    o_ref[0] = jnp.dot(s, w_ref[0].astype(BF16), preferred_element_type=F32) + b_ref[0]


def _modulation(c_rows, w_mod, b_mod):
    depth, d, n = w_mod.shape
    r = c_rows.shape[0]
    tn = n // 4
    return pl.pallas_call(
        _mod_body,
        grid=(depth, n // tn),
        in_specs=[pl.BlockSpec((r, d), lambda i, j: (0, 0)),
                  pl.BlockSpec((1, d, tn), lambda i, j: (i, 0, j)),
                  pl.BlockSpec((1, 1, tn), lambda i, j: (i, 0, j))],
        out_specs=pl.BlockSpec((1, r, tn), lambda i, j: (i, 0, j)),
        out_shape=jax.ShapeDtypeStruct((depth, r, n), F32),
        compiler_params=_params("parallel", "parallel"),
        name="adaln_mod",
    )(c_rows, w_mod, b_mod.reshape(depth, 1, n))


def _norm_mod(x, g, sh, sc):
    return _rms(x, g) * (1.0 + sc) + sh


def _proj_body(x_ref, g_ref, sh_ref, sc_ref, w_ref, o_ref, h_scr):
    @pl.when(pl.program_id(1) == 0)
    def _():
        h_scr[...] = _norm_mod(x_ref[...], g_ref[...], sh_ref[0], sc_ref[0]).astype(BF16)

    o_ref[...] = jnp.dot(h_scr[...], w_ref[...], preferred_element_type=F32).astype(o_ref.dtype)


def _norm_mod_proj(x, g, sh, sc, w, out_dtype):
    b, l, d = x.shape
    n = w.shape[1]
    tm = _row_tile(l, 1024)
    tn = _row_tile(n, 512)
    per = l // tm
    out = pl.pallas_call(
        _proj_body,
        grid=(b * per, n // tn),
        in_specs=[pl.BlockSpec((tm, d), lambda i, j: (i, 0)),
                  pl.BlockSpec((1, d), lambda i, j: (0, 0)),
                  pl.BlockSpec((1, 1, d), lambda i, j: (i // per, 0, 0)),
                  pl.BlockSpec((1, 1, d), lambda i, j: (i // per, 0, 0)),
                  pl.BlockSpec((d, tn), lambda i, j: (0, j))],
        out_specs=pl.BlockSpec((tm, tn), lambda i, j: (i, j)),
        out_shape=jax.ShapeDtypeStruct((b * l, n), out_dtype),
        scratch_shapes=[pltpu.VMEM((tm, d), BF16)],
        compiler_params=_params("parallel", "arbitrary"),
        name="norm_mod_proj",
    )(x.reshape(b * l, d), g.reshape(1, d), sh, sc, w)
    return out.reshape(b, l, n)


def _norm_only_body(x_ref, g_ref, sh_ref, sc_ref, o_ref):
    o_ref[...] = _norm_mod(x_ref[...], g_ref[...], sh_ref[0], sc_ref[0])


def _norm_mod_only(x, g, sh, sc):
    b, l, d = x.shape
    tm = _row_tile(l, 512)
    per = l // tm
    out = pl.pallas_call(
        _norm_only_body,
        grid=(b * per,),
        in_specs=[pl.BlockSpec((tm, d), lambda i: (i, 0)),
                  pl.BlockSpec((1, d), lambda i: (0, 0)),
                  pl.BlockSpec((1, 1, d), lambda i: (i // per, 0, 0)),
                  pl.BlockSpec((1, 1, d), lambda i: (i // per, 0, 0))],
        out_specs=pl.BlockSpec((tm, d), lambda i: (i, 0)),
        out_shape=jax.ShapeDtypeStruct((b * l, d), F32),
        compiler_params=_params("parallel"),
        name="norm_mod",
    )(x.reshape(b * l, d), g.reshape(1, d), sh, sc)
    return out.reshape(b, l, d)


def _residual(res_ref, gate_ref, g1_ref, y):
    return res_ref[...] + gate_ref[0] * _rms(y, g1_ref[...])


def _out_plain_body(a_ref, w_ref, res_ref, gate_ref, g1_ref, o_ref):
    y = jnp.dot(a_ref[...], w_ref[...], preferred_element_type=F32)
    o_ref[...] = _residual(res_ref, gate_ref, g1_ref, y)


def _out_hgrn_body(of_ref, ob_ref, gp_ref, og_ref, w_ref, res_ref, gate_ref, g1_ref, o_ref):
    o = of_ref[...] + ob_ref[...]
    parts = []
    for h in range(A_HEADS):
        oh = o[:, h * LANES:(h + 1) * LANES]
        parts.append(oh * lax.rsqrt(jnp.mean(oh * oh, axis=-1, keepdims=True) + EPS))
    gp = gp_ref[...]
    a = jnp.concatenate(parts, axis=-1) * og_ref[...] * (gp * jax.nn.sigmoid(gp))
    y = jnp.dot(a.astype(BF16), w_ref[...], preferred_element_type=F32)
    o_ref[...] = _residual(res_ref, gate_ref, g1_ref, y)


def _gelu_tanh(y):
    return 0.5 * y * (1.0 + jnp.tanh(0.7978845608028654 * (y + 0.044715 * (y * y * y))))


def _out_glu_body(y_ref, w_ref, res_ref, gate_ref, g1_ref, o_ref):
    d = o_ref.shape[-1]
    a = _gelu_tanh(y_ref[...]).astype(BF16)
    ag = jnp.dot(a, w_ref[...], preferred_element_type=F32)
    y = ag[:, :d] * jax.nn.sigmoid(ag[:, d:])
    o_ref[...] = _residual(res_ref, gate_ref, g1_ref, y)


def _out_call(body, row_inputs, const_inputs, res, gate, g1, name):
    b, l, d = res.shape
    tm = _row_tile(l, 256)
    per = l // tm
    in_specs, args = [], []
    for arr, blk, width in row_inputs:
        in_specs.append(pl.BlockSpec((tm, width), lambda i, blk=blk: (i, blk)))
        args.append(arr.reshape(b * l, arr.shape[-1]))
    for arr in const_inputs:
        in_specs.append(pl.BlockSpec(arr.shape, lambda i, nd=arr.ndim: (0,) * nd))
        args.append(arr)
    in_specs += [pl.BlockSpec((tm, d), lambda i: (i, 0)),
                 pl.BlockSpec((1, 1, d), lambda i: (i // per, 0, 0)),
                 pl.BlockSpec((1, d), lambda i: (0, 0))]
    args += [res.reshape(b * l, d), gate, g1.reshape(1, d)]
    out = pl.pallas_call(
        body,
        grid=(b * per,),
        in_specs=in_specs,
        out_specs=pl.BlockSpec((tm, d), lambda i: (i, 0)),
        out_shape=jax.ShapeDtypeStruct((b * l, d), F32),
        compiler_params=_params("parallel"),
        name=name,
    )(*args)
    return out.reshape(b, l, d)


HGRN_CHUNK = 32
HGRN_SAFE_MIN = 1e-30
HGRN_Q_HEADROOM = 1e37


def _hgrn_gates(z, lb):
    f = lb + (1.0 - lb) * jax.nn.sigmoid(z)
    return jnp.maximum(f, F_MIN), 1.0 - f


def _hgrn_chunk(q, v, z, lb, st, reverse):
    c = q.shape[0]
    half = c // 2
    fm, kk = _hgrn_gates(z, lb)
    pos = lax.broadcasted_iota(jnp.int32, (c, LANES), 0)
    if reverse:
        pos = (c - 1) - pos
    second = pos >= half
    hpos = jnp.where(second, pos - half, pos)
    ph = fm
    j = 1
    while j < half:
        ph = ph * jnp.where(hpos >= j, pltpu.roll(ph, (c - j) if reverse else j, 0), 1.0)
        j *= 2
    a_last = ph[half:half + 1] if reverse else ph[half - 1:half]
    r_last = ph[0:1] if reverse else ph[c - 1:c]
    p_mid = a_last
    p_last = a_last * r_last
    pp = jnp.where(second, ph * a_last, ph)
    r = jnp.where(second, ph, ph * (1.0 / a_last))
    kd = kk * (1.0 / r)
    sc = lax.dot_general((q * r).astype(BF16), kd.astype(BF16), NT_DIMS, preferred_element_type=F32)
    row = lax.broadcasted_iota(jnp.int32, (c, c), 0)
    col = lax.broadcasted_iota(jnp.int32, (c, c), 1)
    sc = jnp.where((col >= row) if reverse else (col <= row), sc, 0.0)
    vb = v.astype(BF16)
    o = jnp.dot(sc.astype(BF16), vb, preferred_element_type=F32)
    o = o + lax.dot_general((q * pp).astype(BF16), st.astype(BF16), NT_DIMS, preferred_element_type=F32)
    u = lax.dot_general(vb, (kd * r_last).astype(BF16), TN_DIMS, preferred_element_type=F32)
    q_max = jnp.max(jnp.abs(q), axis=0, keepdims=True)
    ok = jnp.where(p_mid >= HGRN_SAFE_MIN, r_last, 0.0) >= HGRN_SAFE_MIN
    ok = jnp.where(ok, p_mid * HGRN_Q_HEADROOM, -1.0) >= q_max
    return o, p_last * st + u, jnp.where(ok, 0.0, 1.0)


def _hgrn_exact_block(q_ref, v_ref, z_ref, lb_ref, o_ref, st_scr, dr, reverse, n_tiles):
    pos = lax.broadcasted_iota(jnp.int32, (SUBLANES, LANES), 0)
    if reverse:
        pos = (SUBLANES - 1) - pos

    def earlier(x, j):
        if j == 0:
            return x
        return pltpu.roll(x, (SUBLANES - j) if reverse else j, 0)

    def later(x, j):
        return pltpu.roll(x, j if reverse else (SUBLANES - j), 0)

    def tile(i, carry):
        ti = (n_tiles - 1 - i) if reverse else i
        r0 = pl.multiple_of(ti * SUBLANES, SUBLANES)
        for h in range(A_HEADS):
            sl = slice(h * LANES, (h + 1) * LANES)
            lb = lb_ref[:, sl]
            z = z_ref[0, pl.ds(r0, SUBLANES), sl]
            q = q_ref[0, pl.ds(r0, SUBLANES), sl]
            v = v_ref[0, pl.ds(r0, SUBLANES), sl]
            fm, kk = _hgrn_gates(z, lb)
            pp = fm
            for j in (1, 2, 4):
                pp = pp * jnp.where(pos >= j, earlier(pp, j), 1.0)
            qq = jnp.where(pos <= SUBLANES - 2, later(fm, 1), 1.0)
            for j in (1, 2, 4):
                qq = qq * jnp.where(pos <= SUBLANES - 1 - j, later(qq, j), 1.0)
            dec = pp[0:1] if reverse else pp[SUBLANES - 1:SUBLANES]
            st = st_scr[dr, h]
            o = lax.dot_general((q * pp).astype(BF16), st.astype(BF16), NT_DIMS,
                                preferred_element_type=F32)
            g = fm
            for d in range(SUBLANES):
                if d == 0:
                    e = q * kk
                else:
                    if d > 1:
                        g = g * earlier(fm, d - 1)
                    e = q * g * jnp.where(pos >= d, earlier(kk, d), 0.0)
                o = o + jnp.sum(e, axis=-1, keepdims=True) * earlier(v, d)
            o_ref[0, pl.ds(r0, SUBLANES), sl] = o
            u = lax.dot_general(v.astype(BF16), (kk * qq).astype(BF16), TN_DIMS,
                                preferred_element_type=F32)
            st_scr[dr, h] = dec * st + u
        return carry

    lax.fori_loop(0, n_tiles, tile, 0)


def _hgrn_scan_body(qf_ref, vf_ref, zf_ref, qb_ref, vb_ref, zb_ref, lb_ref, s0_ref, of_ref, ob_ref, sfin_ref,
                    st_scr, save_scr, *, n_chunks):
    step = pl.program_id(1)

    @pl.when(step == 0)
    def _():
        st_scr[...] = s0_ref[0]

    save_scr[...] = st_scr[...]
    dirs = ((qf_ref, vf_ref, zf_ref, of_ref, False), (qb_ref, vb_ref, zb_ref, ob_ref, True))
    c = HGRN_CHUNK

    def chunk(i, bad):
        for dr, (q_ref, v_ref, z_ref, o_ref, reverse) in enumerate(dirs):
            r0 = pl.multiple_of(((n_chunks - 1 - i) if reverse else i) * c, c)
            for h in range(A_HEADS):
                sl = slice(h * LANES, (h + 1) * LANES)
                o, st_new, bad_h = _hgrn_chunk(q_ref[0, pl.ds(r0, c), sl], v_ref[0, pl.ds(r0, c), sl],
                                               z_ref[0, pl.ds(r0, c), sl], lb_ref[:, sl], st_scr[dr, h], reverse)
                o_ref[0, pl.ds(r0, c), sl] = o
                st_scr[dr, h] = st_new
                bad = jnp.maximum(bad, bad_h)
        return bad

    bad = lax.fori_loop(0, n_chunks, chunk, jnp.zeros((1, LANES), F32))

    @pl.when(jnp.max(bad) > 0.0)
    def _():
        st_scr[...] = save_scr[...]
        for dr, (q_ref, v_ref, z_ref, o_ref, reverse) in enumerate(dirs):
            _hgrn_exact_block(q_ref, v_ref, z_ref, lb_ref, o_ref, st_scr, dr, reverse, n_chunks * c // SUBLANES)

    @pl.when(step == pl.num_programs(1) - 1)
    def _():
        sfin_ref[0] = st_scr[...]


def _hgrn_scan(proj, lb, s0):
    b, l, _ = proj.shape
    d = A_HEADS * LANES
    t = _row_tile(l, 256)
    nb = l // t
    fwd = lambda blk: pl.BlockSpec((1, t, d), lambda i, s: (i, s, blk))
    bwd = lambda blk: pl.BlockSpec((1, t, d), lambda i, s: (i, nb - 1 - s, blk))
    st_spec = pl.BlockSpec((1,) + s0.shape[1:], lambda i, s: (i, 0, 0, 0, 0))
    return pl.pallas_call(
        functools.partial(_hgrn_scan_body, n_chunks=t // HGRN_CHUNK),
        grid=(b, nb),
        in_specs=[fwd(0), fwd(1), fwd(3), bwd(0), bwd(1), bwd(4),
                  pl.BlockSpec((1, d), lambda i, s: (0, 0)), st_spec],
        out_specs=[fwd(0), bwd(0), st_spec],
        out_shape=[jax.ShapeDtypeStruct((b, l, d), F32), jax.ShapeDtypeStruct((b, l, d), F32),
                   jax.ShapeDtypeStruct(s0.shape, F32)],
        scratch_shapes=[pltpu.VMEM(s0.shape[1:], F32), pltpu.VMEM(s0.shape[1:], F32)],
        compiler_params=_params("parallel", "arbitrary"),
        name="hgrn_scan",
    )(proj, proj, proj, proj, proj, proj, lb, s0)


def _hgrn2_mixer(x_c, x_l, ng, mod_c, mod_l, w_in, lower, out_g, w_out, g1, want_ctx):
    d = x_l.shape[-1]
    p_c = _norm_mod_proj(x_c, ng, mod_c[0], mod_c[1], w_in, F32)
    p_l = _norm_mod_proj(x_l, ng, mod_l[0], mod_l[1], w_in, F32)
    lb = lower.reshape(1, d)
    zero = jnp.zeros((x_l.shape[0], 2, A_HEADS, LANES, LANES), F32)
    ocf, ocb, s_ctx = _hgrn_scan(p_c, lb, zero)
    olf, olb, _ = _hgrn_scan(p_l, lb, s_ctx)
    o_c, o_l = (ocf, ocb), (olf, olb)
    og = out_g.reshape(1, d)
    new_l = _out_call(_out_hgrn_body, [(o_l[0], 0, d), (o_l[1], 0, d), (p_l, 2, d)], [og, w_out],
                      x_l, mod_l[2], g1, "hgrn_out")
    new_c = None
    if want_ctx:
        new_c = _out_call(_out_hgrn_body, [(o_c[0], 0, d), (o_c[1], 0, d), (p_c, 2, d)], [og, w_out],
                          x_c, mod_c[2], g1, "hgrn_out")
    return new_c, new_l


def _softmax_pv(s_list, v_list):
    mx = functools.reduce(jnp.maximum, [jnp.max(s, axis=-1, keepdims=True) for s in s_list])
    ps = [jnp.exp(s - mx) for s in s_list]
    den = functools.reduce(jnp.add, [jnp.sum(p, axis=-1, keepdims=True) for p in ps])
    acc = functools.reduce(jnp.add, [jnp.dot(p.astype(BF16), v, preferred_element_type=F32)
                                     for p, v in zip(ps, v_list)])
    return acc / den


def _na_lat_body(q_ref, k_ref, v_ref, kc_ref, vc_ref, t2_ref, o_ref, *, rows, kh, scale):
    w = GRID_W
    rb = NA_ROW_BLOCK
    nk = kh + rb
    n_rel = 2 * kh - 1
    first = lax.broadcasted_iota(jnp.int32, (rb * w, LANES), 1) < (LANES // 2)
    kc = kc_ref[0]
    vc = vc_ref[0]

    def bias_index(r, ks, m):
        r0 = jnp.clip(r - kh // 2, 0, rows - kh)
        ka = ks + 2 * m
        rel_a = ka - r + (kh - 1)
        in_a = (ka >= r0) & (ka < r0 + kh)
        in_b = (ka + 1 >= r0) & (ka + 1 < r0 + kh)
        both, only_b, only_a = rel_a, (n_rel - 1) + rel_a + 1, (2 * n_rel - 1) + rel_a
        return jnp.where(in_a, jnp.where(in_b, both, only_a), jnp.where(in_b, only_b, 3 * n_rel - 1))

    def block(i, carry):
        rq = i * rb
        ks = jnp.clip(rq - kh // 2, 0, rows - nk)
        q2 = q_ref[0, pl.ds(pl.multiple_of(rq * w, rb * w), rb * w), :] * scale
        k2 = k_ref[0, pl.ds(pl.multiple_of(ks * w, w), nk * w), :]
        v2 = v_ref[0, pl.ds(pl.multiple_of(ks * w, w), nk * w), :]
        outs = []
        for hh in range(2):
            qm = jnp.where(first if hh == 0 else jnp.logical_not(first), q2, jnp.zeros_like(q2))
            bias = jnp.concatenate(
                [jnp.concatenate([t2_ref[hh, bias_index(rq + a, ks, m)] for m in range(nk // 2)], axis=-1)
                 for a in range(rb)], axis=0)
            s_w = lax.dot_general(qm, k2, NT_DIMS, preferred_element_type=F32) + bias
            s_c = lax.dot_general(qm, kc, NT_DIMS, preferred_element_type=F32)
            outs.append(_softmax_pv([s_w, s_c], [v2, vc]))
        o_ref[0, pl.ds(pl.multiple_of(rq * w, rb * w), rb * w), :] = (
            jnp.where(first, outs[0], outs[1]).astype(o_ref.dtype))
        return carry

    lax.fori_loop(0, rows // rb, block, 0)


def _na_ctx_body(q_ref, k_ref, v_ref, o_ref, *, scale):
    n = q_ref.shape[1]
    first = lax.broadcasted_iota(jnp.int32, (n, LANES), 1) < (LANES // 2)
    q2 = q_ref[0] * scale
    k2 = k_ref[0]
    v2 = v_ref[0]
    outs = []
    for hh in range(2):
        qm = jnp.where(first if hh == 0 else jnp.logical_not(first), q2, jnp.zeros_like(q2))
        s = lax.dot_general(qm, k2, NT_DIMS, preferred_element_type=F32)
        outs.append(_softmax_pv([s], [v2]))
    o_ref[0] = jnp.where(first, outs[0], outs[1]).astype(o_ref.dtype)


def _na_bias_table(rpb, kh, kw):
    w = jnp.arange(GRID_W)[:, None]
    c = jnp.arange(GRID_W)[None, :]
    c0 = jnp.clip(w - kw // 2, 0, GRID_W - kw)
    inside = (c >= c0) & (c < c0 + kw)
    cb = jnp.clip(c - w + kw - 1, 0, 2 * kw - 2)
    t = jnp.where(inside[None, None], rpb.astype(F32)[:, :, cb], NEG_BIG)
    off = jnp.full_like(t, NEG_BIG)
    pair = lambda a, b: jnp.concatenate([a, b], axis=-1)
    return jnp.concatenate([pair(t[:, :-1], t[:, 1:]), pair(off, t), pair(t, off), pair(off, off)[:, :1]], axis=1)


def _na_mixer(x_c, x_l, ng, mod_c, mod_l, w_qkv, rpb, w_out, g1, want_ctx):
    b, l, d = x_l.shape
    lc = x_c.shape[1]
    heads = rpb.shape[0]
    na_rows, na_cols = (rpb.shape[1] + 1) // 2, (rpb.shape[2] + 1) // 2
    dh = d // heads
    assert 2 * dh == LANES and l % GRID_W == 0
    rows = l // GRID_W
    kh = min(na_rows, rows)
    assert kh == na_rows and (kh + NA_ROW_BLOCK) % 2 == 0
    assert rows % NA_ROW_BLOCK == 0 and rows >= kh + NA_ROW_BLOCK
    scale = dh ** -0.5
    nhp = heads // 2
    qkv_c = _norm_mod_proj(x_c, ng, mod_c[0], mod_c[1], w_qkv, BF16)
    qkv_l = _norm_mod_proj(x_l, ng, mod_l[0], mod_l[1], w_qkv, BF16)
    t2 = _na_bias_table(rpb, kh, na_cols)
    o_l = pl.pallas_call(
        functools.partial(_na_lat_body, rows=rows, kh=kh, scale=scale),
        grid=(b, nhp),
        in_specs=[pl.BlockSpec((1, l, LANES), lambda i, p: (i, 0, p)),
                  pl.BlockSpec((1, l, LANES), lambda i, p: (i, 0, nhp + p)),
                  pl.BlockSpec((1, l, LANES), lambda i, p: (i, 0, 2 * nhp + p)),
                  pl.BlockSpec((1, lc, LANES), lambda i, p: (i, 0, nhp + p)),
                  pl.BlockSpec((1, lc, LANES), lambda i, p: (i, 0, 2 * nhp + p)),
                  pl.BlockSpec((2,) + t2.shape[1:], lambda i, p: (p, 0, 0, 0))],
        out_specs=pl.BlockSpec((1, l, LANES), lambda i, p: (i, 0, p)),
        out_shape=jax.ShapeDtypeStruct((b, l, d), BF16),
        compiler_params=_params("parallel", "parallel"),
        name="na_latent",
    )(qkv_l, qkv_l, qkv_l, qkv_c, qkv_c, t2)
    new_l = _out_call(_out_plain_body, [(o_l, 0, d)], [w_out], x_l, mod_l[2], g1, "na_out")
    new_c = None
    if want_ctx:
        o_c = pl.pallas_call(
            functools.partial(_na_ctx_body, scale=scale),
            grid=(b, nhp),
            in_specs=[pl.BlockSpec((1, lc, LANES), lambda i, p: (i, 0, p)),
                      pl.BlockSpec((1, lc, LANES), lambda i, p: (i, 0, nhp + p)),
                      pl.BlockSpec((1, lc, LANES), lambda i, p: (i, 0, 2 * nhp + p))],
            out_specs=pl.BlockSpec((1, lc, LANES), lambda i, p: (i, 0, p)),
            out_shape=jax.ShapeDtypeStruct((b, lc, d), BF16),
            compiler_params=_params("parallel", "parallel"),
            name="na_context",
        )(qkv_c, qkv_c, qkv_c)
        new_c = _out_call(_out_plain_body, [(o_c, 0, d)], [w_out], x_c, mod_c[2], g1, "na_out")
    return new_c, new_l


S5_GROUP_BLOCK = 8


def _s5_kernel_body(c_ref, w_ref, o_ref):
    for i in range(c_ref.shape[0]):
        o_ref[i] = jnp.dot(c_ref[i], w_ref[i], preferred_element_type=F32, precision=lax.Precision.HIGHEST)


def _s5_impulse(cmat, wmat):
    n = cmat.shape[0]
    gb = S5_GROUP_BLOCK
    return pl.pallas_call(
        _s5_kernel_body,
        grid=(n // gb,),
        in_specs=[pl.BlockSpec((gb,) + cmat.shape[1:], lambda i: (i, 0, 0)),
                  pl.BlockSpec((gb,) + wmat.shape[1:], lambda i: (i, 0, 0))],
        out_specs=pl.BlockSpec((gb, cmat.shape[1], wmat.shape[2]), lambda i: (i, 0, 0)),
        out_shape=jax.ShapeDtypeStruct((n, cmat.shape[1], wmat.shape[2]), F32),
        compiler_params=_params("parallel"),
        name="s5_impulse",
    )(cmat, wmat)


def _s5_operators(lam_re, lam_im, log_dt, b_re, b_im, c_re, c_im, t):
    _, g, p = lam_re.shape
    cg = b_re.shape[-1]
    lam = lax.complex(lam_re.astype(F32), lam_im.astype(F32))
    ldt = lam * jnp.exp(log_dt.astype(F32))[..., None]
    a = jnp.exp(ldt)
    bbar = ((a - 1.0) / lam)[..., None] * lax.complex(b_re.astype(F32), b_im.astype(F32))
    cm = lax.complex(c_re.astype(F32), c_im.astype(F32))
    apow = jnp.exp(ldt[..., None] * jnp.arange(t + 1, dtype=F32))
    w = apow[..., :t, None] * bbar[:, :, :, None, :]
    wmat = jnp.concatenate([jnp.real(w), jnp.imag(w)], axis=2).reshape(2 * g, 2 * p, t * cg)
    cmat = jnp.concatenate([jnp.real(cm), -jnp.imag(cm)], axis=-1).reshape(2 * g, cg, 2 * p)
    k = _s5_impulse(cmat, wmat).reshape(2, g, cg, t, cg)
    k = jnp.transpose(k, (0, 1, 3, 4, 2))
    kfull = jnp.concatenate([k[1, :, :0:-1], (k[0, :, :1] + k[1, :, :1]), k[0, :, 1:]], axis=1)
    idx = jnp.arange(t)[None, :] - jnp.arange(t)[:, None] + (t - 1)
    toep = jnp.transpose(kfull[:, idx], (0, 1, 3, 2, 4)).reshape(g, t * cg, t * cg)

    def state_in(wd, flip):
        wd = wd[:, :, ::-1] if flip else wd
        m = jnp.transpose(wd, (0, 2, 3, 1)).reshape(g, t * cg, p)
        return jnp.concatenate([jnp.real(m), jnp.imag(m)], axis=-1)

    def state_out(cd, pw):
        n = cd[:, :, :, None] * pw[:, None, :, :]
        n = jnp.transpose(n, (0, 2, 3, 1)).reshape(g, p, t * cg)
        return jnp.concatenate([jnp.real(n), -jnp.imag(n)], axis=1)

    m_f = state_in(w[0], True)
    m_b = state_in(w[1], False)
    n_f = state_out(cm[0], apow[0][..., 1:])
    n_b = state_out(cm[1], apow[1][..., :0:-1])
    at = apow[..., t]
    a1 = jnp.concatenate([jnp.real(at), jnp.real(at)], axis=-1)[:, :, None, :]
    a2 = jnp.concatenate([-jnp.imag(at), jnp.imag(at)], axis=-1)[:, :, None, :]
    bf = lambda m: m.astype(BF16)
    return dict(toep=bf(toep), m_f=bf(m_f), m_b=bf(m_b), n_f=bf(n_f), n_b=bf(n_b), a1=a1, a2=a2)


def _s5_body(u_ref, dsk_ref, toep_ref, mf_ref, mb_ref, nf_ref, nb_ref, a1_ref, a2_ref, x0_ref,
             y_ref, xfin_ref, z_scr, xin_scr):
    gb, nc = u_ref.shape[1], u_ref.shape[2]
    half = LANES // 2
    for gi in range(gb):
        ub = u_ref[0, gi].astype(BF16)
        z_scr[0, :, gi, :] = jnp.dot(ub, mf_ref[gi], preferred_element_type=F32)
        z_scr[1, :, gi, :] = jnp.dot(ub, mb_ref[gi], preferred_element_type=F32)

    def scan(j, xs):
        out = []
        for dr in range(2):
            jj = j if dr == 0 else nc - 1 - j
            x = xs[dr]
            xin_scr[dr, jj] = x
            out.append(a1_ref[dr, :, 0, :] * x + a2_ref[dr, :, 0, :] * pltpu.roll(x, half, 1) + z_scr[dr, jj])
        return tuple(out)

    xf, xb = lax.fori_loop(0, nc, scan, (x0_ref[0, 0], x0_ref[0, 1]))
    xfin_ref[0, 0] = xf
    xfin_ref[0, 1] = xb
    for gi in range(gb):
        u = u_ref[0, gi]
        y = jnp.dot(u.astype(BF16), toep_ref[gi], preferred_element_type=F32) + u * dsk_ref[gi]
        y = y + jnp.dot(xin_scr[0, :, gi, :].astype(BF16), nf_ref[gi], preferred_element_type=F32)
        y = y + jnp.dot(xin_scr[1, :, gi, :].astype(BF16), nb_ref[gi], preferred_element_type=F32)
        y_ref[0, gi] = y


def _s5_scan(h, ops, dsk, x0):
    b, l, d = h.shape
    g = ops["toep"].shape[0]
    cg = d // g
    t = S5_CHUNK
    nc = l // t
    tc = t * cg
    gb = S5_GROUP_BLOCK
    u = jnp.transpose(h.reshape(b, nc, t, g, cg), (0, 3, 1, 2, 4)).reshape(b, g, nc, tc)
    per_g = lambda shape: pl.BlockSpec((gb,) + shape, lambda j, i: (j,) + (0,) * len(shape))
    y, xfin = pl.pallas_call(
        _s5_body,
        grid=(g // gb, b),
        in_specs=[pl.BlockSpec((1, gb, nc, tc), lambda j, i: (i, j, 0, 0)),
                  per_g((1, tc)), per_g((tc, tc)), per_g((tc, LANES)), per_g((tc, LANES)),
                  per_g((LANES, tc)), per_g((LANES, tc)),
                  pl.BlockSpec((2, gb, 1, LANES), lambda j, i: (0, j, 0, 0)),
                  pl.BlockSpec((2, gb, 1, LANES), lambda j, i: (0, j, 0, 0)),
                  pl.BlockSpec((1, 2, gb, LANES), lambda j, i: (i, 0, j, 0))],
        out_specs=[pl.BlockSpec((1, gb, nc, tc), lambda j, i: (i, j, 0, 0)),
                   pl.BlockSpec((1, 2, gb, LANES), lambda j, i: (i, 0, j, 0))],
        out_shape=[jax.ShapeDtypeStruct((b, g, nc, tc), F32), jax.ShapeDtypeStruct((b, 2, g, LANES), F32)],
        scratch_shapes=[pltpu.VMEM((2, nc, gb, LANES), F32), pltpu.VMEM((2, nc, gb, LANES), F32)],
        compiler_params=_params("parallel", "parallel"),
        name="s5_scan",
    )(u, dsk, ops["toep"], ops["m_f"], ops["m_b"], ops["n_f"], ops["n_b"], ops["a1"], ops["a2"], x0)
    y = jnp.transpose(y.reshape(b, g, nc, t, cg), (0, 2, 3, 1, 4)).reshape(b, l, d)
    return y, xfin


def _s5_mixer(x_c, x_l, ng, mod_c, mod_l, lam_re, lam_im, log_dt, b_re, b_im, c_re, c_im, d_skip, w_glu, g1,
              want_ctx):
    b, l, d = x_l.shape
    g, p = lam_re.shape[1], lam_re.shape[2]
    cg = d // g
    assert 2 * p == LANES and g % S5_GROUP_BLOCK == 0
    ops = _s5_operators(lam_re, lam_im, log_dt, b_re, b_im, c_re, c_im, S5_CHUNK)
    dsk = jnp.tile(d_skip.astype(F32).reshape(g, 1, cg), (1, 1, S5_CHUNK))
    h_c = _norm_mod_only(x_c, ng, mod_c[0], mod_c[1])
    h_l = _norm_mod_only(x_l, ng, mod_l[0], mod_l[1])
    y_c, x_ctx = _s5_scan(h_c, ops, dsk, jnp.zeros((b, 2, g, LANES), F32))
    y_l, _ = _s5_scan(h_l, ops, dsk, x_ctx)
    new_l = _out_call(_out_glu_body, [(y_l, 0, d)], [w_glu], x_l, mod_l[2], g1, "s5_glu_out")
    new_c = None
    if want_ctx:
        new_c = _out_call(_out_glu_body, [(y_c, 0, d)], [w_glu], x_c, mod_c[2], g1, "s5_glu_out")
    return new_c, new_l


FFN_HALO = 16
FFN_TN = 256
FFN_ROW_SPLIT = 256


def _ffn_body(xp_ref, x_ref, xn_ref, g2_ref, sh_ref, sc_ref, wa_ref, wv_ref, cwa_ref, cwv_ref, cba_ref, cbv_ref,
              wo_ref, gate_ref, g3_ref, o_ref, h_scr, gat_scr, *, per):
    i, j = pl.program_id(0), pl.program_id(1)
    tm = x_ref.shape[0]
    hl = FFN_HALO

    @pl.when(j == 0)
    def _():
        g, sh, sc = g2_ref[...], sh_ref[0], sc_ref[0]
        keep_p = ((i % per) != 0).astype(F32)
        keep_n = ((i % per) != per - 1).astype(F32)
        h_scr[0:hl] = (_norm_mod(xp_ref[...], g, sh, sc) * keep_p).astype(BF16)
        h_scr[hl:hl + tm] = _norm_mod(x_ref[...], g, sh, sc).astype(BF16)
        h_scr[hl + tm:] = (_norm_mod(xn_ref[...], g, sh, sc) * keep_n).astype(BF16)

    sub = FFN_ROW_SPLIT if tm % FFN_ROW_SPLIT == 0 else tm
    ns = tm // sub
    edge = SUBLANES
    tn = wa_ref.shape[1]
    cols = pl.ds(pl.multiple_of(j * tn, tn), tn)

    def pieces(w_ref):
        out = []
        for s in range(ns):
            lo = hl + s * sub - (hl if s == 0 else 0)
            hi = hl + (s + 1) * sub + (hl if s == ns - 1 else 0)
            out.append(jnp.dot(h_scr[lo:hi], w_ref[...], preferred_element_type=F32))
        return out

    def conv(p, s, cw_ref, cb_ref):
        base = hl if s == 0 else 0
        left = p[s][hl - edge:hl] if s == 0 else p[s - 1][-edge:]
        right = p[s][base + sub:base + sub + edge] if s == ns - 1 else p[s + 1][:edge]
        ext = jnp.concatenate([left, p[s][base:base + sub], right], axis=0)
        um = pltpu.roll(ext, 1, 0)[edge:edge + sub]
        up = pltpu.roll(ext, sub + 2 * edge - 1, 0)[edge:edge + sub]
        return cb_ref[...] + um * cw_ref[0:1] + ext[edge:edge + sub] * cw_ref[1:2] + up * cw_ref[2:3]

    pa, pv = pieces(wa_ref), pieces(wv_ref)
    for s in range(ns):
        a = conv(pa, s, cwa_ref, cba_ref)
        v = conv(pv, s, cwv_ref, cbv_ref)
        gat_scr[s * sub:(s + 1) * sub, cols] = (a * jax.nn.sigmoid(a) * v).astype(BF16)

    @pl.when(j == pl.num_programs(1) - 1)
    def _():
        y = jnp.dot(gat_scr[...], wo_ref[...], preferred_element_type=F32)
        o_ref[...] = x_ref[...] + gate_ref[0] * _rms(y, g3_ref[...])


def _conv_ffn_block(x, ng2, mod, w_in, conv_w, conv_b, w_out, ng3):
    b, l, d = x.shape
    f = w_out.shape[0]
    tn, hl = FFN_TN, FFN_HALO
    assert f % tn == 0 and conv_w.shape[0] == 3 and w_in.shape[1] == 2 * f
    nj = f // tn
    tm = _row_tile(l, 1024)
    per = l // tm
    hb = tm // hl
    last_hb = b * l // hl - 1
    cb = conv_b.reshape(1, 2 * f)
    x2 = x.reshape(b * l, d)
    out = pl.pallas_call(
        functools.partial(_ffn_body, per=per),
        grid=(b * per, nj),
        in_specs=[pl.BlockSpec((hl, d), lambda i, j: (jnp.maximum(i * hb - 1, 0), 0)),
                  pl.BlockSpec((tm, d), lambda i, j: (i, 0)),
                  pl.BlockSpec((hl, d), lambda i, j: (jnp.minimum((i + 1) * hb, last_hb), 0)),
                  pl.BlockSpec((1, d), lambda i, j: (0, 0)),
                  pl.BlockSpec((1, 1, d), lambda i, j: (i // per, 0, 0)),
                  pl.BlockSpec((1, 1, d), lambda i, j: (i // per, 0, 0)),
                  pl.BlockSpec((d, tn), lambda i, j: (0, j)),
                  pl.BlockSpec((d, tn), lambda i, j: (0, nj + j)),
                  pl.BlockSpec((3, tn), lambda i, j: (0, j)),
                  pl.BlockSpec((3, tn), lambda i, j: (0, nj + j)),
                  pl.BlockSpec((1, tn), lambda i, j: (0, j)),
                  pl.BlockSpec((1, tn), lambda i, j: (0, nj + j)),
                  pl.BlockSpec((f, d), lambda i, j: (0, 0), pipeline_mode=pl.Buffered(1)),
                  pl.BlockSpec((1, 1, d), lambda i, j: (i // per, 0, 0)),
                  pl.BlockSpec((1, d), lambda i, j: (0, 0))],
        out_specs=pl.BlockSpec((tm, d), lambda i, j: (i, 0)),
        out_shape=jax.ShapeDtypeStruct((b * l, d), F32),
        scratch_shapes=[pltpu.VMEM((tm + 2 * hl, d), BF16), pltpu.VMEM((tm, f), BF16)],
        compiler_params=_params("parallel", "arbitrary"),
        name="conv_ffn",
    )(x2, x2, x2, ng2.reshape(1, d), mod[3], mod[4], w_in, w_in, conv_w, conv_w, cb, cb, w_out, mod[5],
      ng3.reshape(1, d))
    return out.reshape(b, l, d)


def kernel(x, c, ctx, c_ctx, w_mod, b_mod, norm_g, a_w_in, a_lower_logits, a_out_g, a_w_out, b_w_qkv, b_rpb,
           b_w_out, c_lam_re, c_lam_im, c_log_dt, c_b_re, c_b_im, c_c_re, c_c_im, c_d, c_w_glu, f_w_in,
           f_conv_w, f_conv_b, f_w_out):
    bsz, _, d = x.shape
    depth = w_mod.shape[0]
    p = jax.nn.softmax(a_lower_logits.astype(F32), axis=0)
    lower = jnp.cumsum(p, axis=0) - p[0]
    n_rows = -(-(bsz + 1) // SUBLANES) * SUBLANES
    c_rows = jnp.zeros((n_rows, d), F32).at[:bsz].set(c).at[bsz].set(c_ctx)
    mods = _modulation(c_rows, w_mod, b_mod)
    wb = lambda w: w.astype(BF16)
    lat, cx = x, ctx
    for i in range(depth):
        kind, j = i % N_MIXERS, i // N_MIXERS
        last = i == depth - 1
        m = mods[i].reshape(n_rows, N_MOD, d)
        mod_l = [m[:bsz, k][:, None, :] for k in range(N_MOD)]
        mod_c = [jnp.broadcast_to(m[bsz, k][None, None, :], (bsz, 1, d)) for k in range(N_MOD)]
        ng = norm_g[i]
        if kind == 0:
            cx1, lat = _hgrn2_mixer(cx, lat, ng[0], mod_c, mod_l, wb(a_w_in[j]), lower[i], a_out_g[j],
                                    wb(a_w_out[j]), ng[1], not last)
        elif kind == 1:
            cx1, lat = _na_mixer(cx, lat, ng[0], mod_c, mod_l, wb(b_w_qkv[j]), b_rpb[j], wb(b_w_out[j]), ng[1],
                                 not last)
        else:
            cx1, lat = _s5_mixer(cx, lat, ng[0], mod_c, mod_l, c_lam_re[j], c_lam_im[j], c_log_dt[j], c_b_re[j],
                                 c_b_im[j], c_c_re[j], c_c_im[j], c_d[j], wb(c_w_glu[j]), ng[1], not last)
        lat = _conv_ffn_block(lat, ng[2], mod_l, wb(f_w_in[i]), f_conv_w[i], f_conv_b[i], wb(f_w_out[i]), ng[3])
        if not last:
            cx = _conv_ffn_block(cx1, ng[2], mod_c, wb(f_w_in[i]), f_conv_w[i], f_conv_b[i], wb(f_w_out[i]), ng[3])
    return lat
```

```python
import functools

import jax
import jax.numpy as jnp
from jax import lax
from jax.experimental import pallas as pl
from jax.experimental.pallas import tpu as pltpu

F32 = jnp.float32
BF16 = jnp.bfloat16

EPS = 1e-6
F_MIN = 1e-30
N_MOD = 6
N_MIXERS = 3
A_HEADS = 8
GRID_W = 64
S5_CHUNK = 32
NEG_BIG = -1e30
NA_ROW_BLOCK = 4
SUBLANES = 8
LANES = 128
VMEM_LIMIT_BYTES = 56 * 1024 * 1024

NT_DIMS = (((1,), (1,)), ((), ()))
TN_DIMS = (((0,), (0,)), ((), ()))


def _params(*sem):
    return pltpu.CompilerParams(dimension_semantics=sem, vmem_limit_bytes=VMEM_LIMIT_BYTES)


def _row_tile(n, want):
    t = min(n, want)
    assert n % t == 0, (n, t)
    return t


def _rms(y, g):
    return y * lax.rsqrt(jnp.mean(y * y, axis=-1, keepdims=True) + EPS) * g


def _mod_body(c_ref, w_ref, b_ref, o_ref):
    c = c_ref[...]
    s = (c * jax.nn.sigmoid(c)).astype(BF16)
    o_ref[0] = jnp.dot(s, w_ref[0].astype(BF16), preferred_element_type=F32) + b_ref[0]


def _modulation(c_rows, w_mod, b_mod):
    depth, d, n = w_mod.shape
    r = c_rows.shape[0]
    tn = n // 4
    return pl.pallas_call(
        _mod_body,
        grid=(depth, n // tn),
        in_specs=[pl.BlockSpec((r, d), lambda i, j: (0, 0)),
                  pl.BlockSpec((1, d, tn), lambda i, j: (i, 0, j)),
                  pl.BlockSpec((1, 1, tn), lambda i, j: (i, 0, j))],
        out_specs=pl.BlockSpec((1, r, tn), lambda i, j: (i, 0, j)),
        out_shape=jax.ShapeDtypeStruct((depth, r, n), F32),
        compiler_params=_params("parallel", "parallel"),
        name="adaln_mod",
    )(c_rows, w_mod, b_mod.reshape(depth, 1, n))


def _norm_mod(x, g, sh, sc):
    return _rms(x, g) * (1.0 + sc) + sh


def _proj_body(x_ref, g_ref, sh_ref, sc_ref, w_ref, o_ref, h_scr):
    @pl.when(pl.program_id(1) == 0)
    def _():
        h_scr[...] = _norm_mod(x_ref[...], g_ref[...], sh_ref[0], sc_ref[0]).astype(BF16)

    o_ref[...] = jnp.dot(h_scr[...], w_ref[...], preferred_element_type=F32).astype(o_ref.dtype)


def _norm_mod_proj(x, g, sh, sc, w, out_dtype):
    b, l, d = x.shape
    n = w.shape[1]
    tm = _row_tile(l, 1024)
    tn = _row_tile(n, 1024)
    per = l // tm
    out = pl.pallas_call(
        _proj_body,
        grid=(b * per, n // tn),
        in_specs=[pl.BlockSpec((tm, d), lambda i, j: (i, 0)),
                  pl.BlockSpec((1, d), lambda i, j: (0, 0)),
                  pl.BlockSpec((1, 1, d), lambda i, j: (i // per, 0, 0)),
                  pl.BlockSpec((1, 1, d), lambda i, j: (i // per, 0, 0)),
                  pl.BlockSpec((d, tn), lambda i, j: (0, j))],
        out_specs=pl.BlockSpec((tm, tn), lambda i, j: (i, j)),
        out_shape=jax.ShapeDtypeStruct((b * l, n), out_dtype),
        scratch_shapes=[pltpu.VMEM((tm, d), BF16)],
        compiler_params=_params("parallel", "arbitrary"),
        name="norm_mod_proj",
    )(x.reshape(b * l, d), g.reshape(1, d), sh, sc, w)
    return out.reshape(b, l, n)


def _norm_only_body(x_ref, g_ref, sh_ref, sc_ref, o_ref):
    o_ref[...] = _norm_mod(x_ref[...], g_ref[...], sh_ref[0], sc_ref[0])


def _norm_mod_only(x, g, sh, sc):
    b, l, d = x.shape
    tm = _row_tile(l, 512)
    per = l // tm
    out = pl.pallas_call(
        _norm_only_body,
        grid=(b * per,),
        in_specs=[pl.BlockSpec((tm, d), lambda i: (i, 0)),
                  pl.BlockSpec((1, d), lambda i: (0, 0)),
                  pl.BlockSpec((1, 1, d), lambda i: (i // per, 0, 0)),
                  pl.BlockSpec((1, 1, d), lambda i: (i // per, 0, 0))],
        out_specs=pl.BlockSpec((tm, d), lambda i: (i, 0)),
        out_shape=jax.ShapeDtypeStruct((b * l, d), F32),
        compiler_params=_params("parallel"),
        name="norm_mod",
    )(x.reshape(b * l, d), g.reshape(1, d), sh, sc)
    return out.reshape(b, l, d)


def _residual(res_ref, gate_ref, g1_ref, y):
    return res_ref[...] + gate_ref[0] * _rms(y, g1_ref[...])


def _out_plain_body(a_ref, w_ref, res_ref, gate_ref, g1_ref, o_ref):
    y = jnp.dot(a_ref[...], w_ref[...], preferred_element_type=F32)
    o_ref[...] = _residual(res_ref, gate_ref, g1_ref, y)


def _out_hgrn_body(of_ref, ob_ref, gp_ref, og_ref, w_ref, res_ref, gate_ref, g1_ref, o_ref):
    o = of_ref[...] + ob_ref[...]
    parts = []
    for h in range(A_HEADS):
        oh = o[:, h * LANES:(h + 1) * LANES]
        parts.append(oh * lax.rsqrt(jnp.mean(oh * oh, axis=-1, keepdims=True) + EPS))
    gp = gp_ref[...]
    a = jnp.concatenate(parts, axis=-1) * og_ref[...] * (gp * jax.nn.sigmoid(gp))
    y = jnp.dot(a.astype(BF16), w_ref[...], preferred_element_type=F32)
    o_ref[...] = _residual(res_ref, gate_ref, g1_ref, y)


def _gelu_tanh(y):
    return 0.5 * y * (1.0 + jnp.tanh(0.7978845608028654 * (y + 0.044715 * (y * y * y))))


def _out_glu_body(y_ref, w_ref, res_ref, gate_ref, g1_ref, o_ref):
    d = o_ref.shape[-1]
    a = _gelu_tanh(y_ref[...]).astype(BF16)
    ag = jnp.dot(a, w_ref[...], preferred_element_type=F32)
    y = ag[:, :d] * jax.nn.sigmoid(ag[:, d:])
    o_ref[...] = _residual(res_ref, gate_ref, g1_ref, y)


def _out_call(body, row_inputs, const_inputs, res, gate, g1, name):
    b, l, d = res.shape
    tm = _row_tile(l, 512)
    per = l // tm
    in_specs, args = [], []
    for arr, blk, width in row_inputs:
        in_specs.append(pl.BlockSpec((tm, width), lambda i, blk=blk: (i, blk)))
        args.append(arr.reshape(b * l, arr.shape[-1]))
    for arr in const_inputs:
        in_specs.append(pl.BlockSpec(arr.shape, lambda i, nd=arr.ndim: (0,) * nd))
        args.append(arr)
    in_specs += [pl.BlockSpec((tm, d), lambda i: (i, 0)),
                 pl.BlockSpec((1, 1, d), lambda i: (i // per, 0, 0)),
                 pl.BlockSpec((1, d), lambda i: (0, 0))]
    args += [res.reshape(b * l, d), gate, g1.reshape(1, d)]
    out = pl.pallas_call(
        body,
        grid=(b * per,),
        in_specs=in_specs,
        out_specs=pl.BlockSpec((tm, d), lambda i: (i, 0)),
        out_shape=jax.ShapeDtypeStruct((b * l, d), F32),
        compiler_params=_params("parallel"),
        name=name,
    )(*args)
    return out.reshape(b, l, d)


HGRN_CHUNK = 32
HGRN_SAFE_MIN = 1e-30
HGRN_Q_HEADROOM = 1e37


def _hgrn_gates(z, lb):
    f = lb + (1.0 - lb) * jax.nn.sigmoid(z)
    return jnp.maximum(f, F_MIN), 1.0 - f


def _hgrn_chunk_prep(q, v, z, lb, reverse):
    c = q.shape[0]
    half = c // 2
    fm, kk = _hgrn_gates(z, lb)
    pos = lax.broadcasted_iota(jnp.int32, (c, LANES), 0)
    if reverse:
        pos = (c - 1) - pos
    second = pos >= half
    hpos = jnp.where(second, pos - half, pos)
    ph = fm
    j = 1
    while j < half:
        ph = ph * jnp.where(hpos >= j, pltpu.roll(ph, (c - j) if reverse else j, 0), 1.0)
        j *= 2
    a_last = ph[half:half + 1] if reverse else ph[half - 1:half]
    r_last = ph[0:1] if reverse else ph[c - 1:c]
    p_mid = a_last
    p_last = a_last * r_last
    pp = jnp.where(second, ph * a_last, ph)
    r = jnp.where(second, ph, ph * (1.0 / a_last))
    kd = kk * (1.0 / r)
    q_max = jnp.max(jnp.abs(q), axis=0, keepdims=True)
    ok = jnp.where(p_mid >= HGRN_SAFE_MIN, r_last, 0.0) >= HGRN_SAFE_MIN
    ok = jnp.where(ok, p_mid * HGRN_Q_HEADROOM, -1.0) >= q_max
    return dict(qr=(q * r).astype(BF16), kd=kd.astype(BF16), qp=(q * pp).astype(BF16),
                kl=(kd * r_last).astype(BF16), vb=v.astype(BF16), dec=p_last, bad=jnp.where(ok, 0.0, 1.0))


def _hgrn_chunk_dots(p, st):
    sc = lax.dot_general(p["qr"], p["kd"], NT_DIMS, preferred_element_type=F32)
    o_st = lax.dot_general(p["qp"], st.astype(BF16), NT_DIMS, preferred_element_type=F32)
    u = lax.dot_general(p["vb"], p["kl"], TN_DIMS, preferred_element_type=F32)
    return sc, o_st, u


def _hgrn_chunk_out(p, sc, o_st, reverse):
    c = sc.shape[0]
    row = lax.broadcasted_iota(jnp.int32, (c, c), 0)
    col = lax.broadcasted_iota(jnp.int32, (c, c), 1)
    sc = jnp.where((col >= row) if reverse else (col <= row), sc, 0.0)
    return jnp.dot(sc.astype(BF16), p["vb"], preferred_element_type=F32) + o_st


def _hgrn_exact_block(q_ref, v_ref, z_ref, lb_ref, o_ref, st_scr, dr, reverse, n_tiles):
    pos = lax.broadcasted_iota(jnp.int32, (SUBLANES, LANES), 0)
    if reverse:
        pos = (SUBLANES - 1) - pos

    def earlier(x, j):
        if j == 0:
            return x
        return pltpu.roll(x, (SUBLANES - j) if reverse else j, 0)

    def later(x, j):
        return pltpu.roll(x, j if reverse else (SUBLANES - j), 0)

    def tile(i, carry):
        ti = (n_tiles - 1 - i) if reverse else i
        r0 = pl.multiple_of(ti * SUBLANES, SUBLANES)
        for h in range(A_HEADS):
            sl = slice(h * LANES, (h + 1) * LANES)
            lb = lb_ref[:, sl]
            z = z_ref[0, pl.ds(r0, SUBLANES), sl]
            q = q_ref[0, pl.ds(r0, SUBLANES), sl]
            v = v_ref[0, pl.ds(r0, SUBLANES), sl]
            fm, kk = _hgrn_gates(z, lb)
            pp = fm
            for j in (1, 2, 4):
                pp = pp * jnp.where(pos >= j, earlier(pp, j), 1.0)
            qq = jnp.where(pos <= SUBLANES - 2, later(fm, 1), 1.0)
            for j in (1, 2, 4):
                qq = qq * jnp.where(pos <= SUBLANES - 1 - j, later(qq, j), 1.0)
            dec = pp[0:1] if reverse else pp[SUBLANES - 1:SUBLANES]
            st = st_scr[dr, h]
            o = lax.dot_general((q * pp).astype(BF16), st.astype(BF16), NT_DIMS,
                                preferred_element_type=F32)
            g = fm
            for d in range(SUBLANES):
                if d == 0:
                    e = q * kk
                else:
                    if d > 1:
                        g = g * earlier(fm, d - 1)
                    e = q * g * jnp.where(pos >= d, earlier(kk, d), 0.0)
                o = o + jnp.sum(e, axis=-1, keepdims=True) * earlier(v, d)
            o_ref[0, pl.ds(r0, SUBLANES), sl] = o
            u = lax.dot_general(v.astype(BF16), (kk * qq).astype(BF16), TN_DIMS,
                                preferred_element_type=F32)
            st_scr[dr, h] = dec * st + u
        return carry

    lax.fori_loop(0, n_tiles, tile, 0)


def _hgrn_scan_body(qf_ref, vf_ref, zf_ref, qb_ref, vb_ref, zb_ref, lb_ref, s0_ref, of_ref, ob_ref, sfin_ref,
                    st_scr, save_scr, *, n_chunks):
    step = pl.program_id(1)

    @pl.when(step == 0)
    def _():
        st_scr[...] = s0_ref[0]

    save_scr[...] = st_scr[...]
    dirs = ((qf_ref, vf_ref, zf_ref, of_ref, False), (qb_ref, vb_ref, zb_ref, ob_ref, True))
    c = HGRN_CHUNK

    def chunk(i, bad):
        work = []
        for dr, (q_ref, v_ref, z_ref, o_ref, reverse) in enumerate(dirs):
            rows = pl.ds(pl.multiple_of(((n_chunks - 1 - i) if reverse else i) * c, c), c)
            for h in range(A_HEADS):
                sl = slice(h * LANES, (h + 1) * LANES)
                p = _hgrn_chunk_prep(q_ref[0, rows, sl], v_ref[0, rows, sl], z_ref[0, rows, sl], lb_ref[:, sl],
                                     reverse)
                bad = jnp.maximum(bad, p["bad"])
                work.append((dr, h, o_ref, rows, sl, reverse, p))
        dots = [_hgrn_chunk_dots(p, st_scr[dr, h]) for dr, h, _, _, _, _, p in work]
        for (dr, h, o_ref, rows, sl, reverse, p), (sc, o_st, u) in zip(work, dots):
            o_ref[0, rows, sl] = _hgrn_chunk_out(p, sc, o_st, reverse)
            st_scr[dr, h] = p["dec"] * st_scr[dr, h] + u
        return bad

    bad = lax.fori_loop(0, n_chunks, chunk, jnp.zeros((1, LANES), F32))

    @pl.when(jnp.max(bad) > 0.0)
    def _():
        st_scr[...] = save_scr[...]
        for dr, (q_ref, v_ref, z_ref, o_ref, reverse) in enumerate(dirs):
            _hgrn_exact_block(q_ref, v_ref, z_ref, lb_ref, o_ref, st_scr, dr, reverse, n_chunks * c // SUBLANES)

    @pl.when(step == pl.num_programs(1) - 1)
    def _():
        sfin_ref[0] = st_scr[...]


def _hgrn_scan(proj, lb, s0):
    b, l, _ = proj.shape
    d = A_HEADS * LANES
    t = _row_tile(l, 256)
    nb = l // t
    fwd = lambda blk: pl.BlockSpec((1, t, d), lambda i, s: (i, s, blk))
    bwd = lambda blk: pl.BlockSpec((1, t, d), lambda i, s: (i, nb - 1 - s, blk))
    st_spec = pl.BlockSpec((1,) + s0.shape[1:], lambda i, s: (i, 0, 0, 0, 0))
    return pl.pallas_call(
        functools.partial(_hgrn_scan_body, n_chunks=t // HGRN_CHUNK),
        grid=(b, nb),
        in_specs=[fwd(0), fwd(1), fwd(3), bwd(0), bwd(1), bwd(4),
                  pl.BlockSpec((1, d), lambda i, s: (0, 0)), st_spec],
        out_specs=[fwd(0), bwd(0), st_spec],
        out_shape=[jax.ShapeDtypeStruct((b, l, d), F32), jax.ShapeDtypeStruct((b, l, d), F32),
                   jax.ShapeDtypeStruct(s0.shape, F32)],
        scratch_shapes=[pltpu.VMEM(s0.shape[1:], F32), pltpu.VMEM(s0.shape[1:], F32)],
        compiler_params=_params("parallel", "arbitrary"),
        name="hgrn_scan",
    )(proj, proj, proj, proj, proj, proj, lb, s0)


def _hgrn2_mixer(x_c, x_l, ng, mod_c, mod_l, w_in, lower, out_g, w_out, g1, want_ctx):
    d = x_l.shape[-1]
    p_c = _norm_mod_proj(x_c, ng, mod_c[0], mod_c[1], w_in, F32)
    p_l = _norm_mod_proj(x_l, ng, mod_l[0], mod_l[1], w_in, F32)
    lb = lower.reshape(1, d)
    zero = jnp.zeros((x_l.shape[0], 2, A_HEADS, LANES, LANES), F32)
    ocf, ocb, s_ctx = _hgrn_scan(p_c, lb, zero)
    olf, olb, _ = _hgrn_scan(p_l, lb, s_ctx)
    o_c, o_l = (ocf, ocb), (olf, olb)
    og = out_g.reshape(1, d)
    new_l = _out_call(_out_hgrn_body, [(o_l[0], 0, d), (o_l[1], 0, d), (p_l, 2, d)], [og, w_out],
                      x_l, mod_l[2], g1, "hgrn_out")
    new_c = None
    if want_ctx:
        new_c = _out_call(_out_hgrn_body, [(o_c[0], 0, d), (o_c[1], 0, d), (p_c, 2, d)], [og, w_out],
                          x_c, mod_c[2], g1, "hgrn_out")
    return new_c, new_l


def _softmax_pv(s_list, v_list):
    mx = functools.reduce(jnp.maximum, [jnp.max(s, axis=-1, keepdims=True) for s in s_list])
    ps = [jnp.exp(s - mx) for s in s_list]
    den = functools.reduce(jnp.add, [jnp.sum(p, axis=-1, keepdims=True) for p in ps])
    acc = functools.reduce(jnp.add, [jnp.dot(p.astype(BF16), v, preferred_element_type=F32)
                                     for p, v in zip(ps, v_list)])
    return acc / den


def _na_lat_body(q_ref, k_ref, v_ref, kc_ref, vc_ref, t2_ref, o_ref, *, rows, kh, scale):
    w = GRID_W
    rb = NA_ROW_BLOCK
    nk = kh + rb
    n_rel = 2 * kh - 1
    first = lax.broadcasted_iota(jnp.int32, (rb * w, LANES), 1) < (LANES // 2)
    kc = kc_ref[0]
    vc = vc_ref[0]

    def bias_index(r, ks, m):
        r0 = jnp.clip(r - kh // 2, 0, rows - kh)
        ka = ks + 2 * m
        rel_a = ka - r + (kh - 1)
        in_a = (ka >= r0) & (ka < r0 + kh)
        in_b = (ka + 1 >= r0) & (ka + 1 < r0 + kh)
        both, only_b, only_a = rel_a, (n_rel - 1) + rel_a + 1, (2 * n_rel - 1) + rel_a
        return jnp.where(in_a, jnp.where(in_b, both, only_a), jnp.where(in_b, only_b, 3 * n_rel - 1))

    def block(i, carry):
        rq = i * rb
        ks = jnp.clip(rq - kh // 2, 0, rows - nk)
        q2 = q_ref[0, pl.ds(pl.multiple_of(rq * w, rb * w), rb * w), :] * scale
        k2 = k_ref[0, pl.ds(pl.multiple_of(ks * w, w), nk * w), :]
        v2 = v_ref[0, pl.ds(pl.multiple_of(ks * w, w), nk * w), :]
        outs = []
        for hh in range(2):
            qm = jnp.where(first if hh == 0 else jnp.logical_not(first), q2, jnp.zeros_like(q2))
            bias = jnp.concatenate(
                [jnp.concatenate([t2_ref[hh, bias_index(rq + a, ks, m)] for m in range(nk // 2)], axis=-1)
                 for a in range(rb)], axis=0)
            s_w = lax.dot_general(qm, k2, NT_DIMS, preferred_element_type=F32) + bias
            s_c = lax.dot_general(qm, kc, NT_DIMS, preferred_element_type=F32)
            outs.append(_softmax_pv([s_w, s_c], [v2, vc]))
        o_ref[0, pl.ds(pl.multiple_of(rq * w, rb * w), rb * w), :] = (
            jnp.where(first, outs[0], outs[1]).astype(o_ref.dtype))
        return carry

    lax.fori_loop(0, rows // rb, block, 0)


def _na_ctx_body(q_ref, k_ref, v_ref, o_ref, *, scale):
    n = q_ref.shape[1]
    first = lax.broadcasted_iota(jnp.int32, (n, LANES), 1) < (LANES // 2)
    q2 = q_ref[0] * scale
    k2 = k_ref[0]
    v2 = v_ref[0]
    outs = []
    for hh in range(2):
        qm = jnp.where(first if hh == 0 else jnp.logical_not(first), q2, jnp.zeros_like(q2))
        s = lax.dot_general(qm, k2, NT_DIMS, preferred_element_type=F32)
        outs.append(_softmax_pv([s], [v2]))
    o_ref[0] = jnp.where(first, outs[0], outs[1]).astype(o_ref.dtype)


def _na_bias_table(rpb, kh, kw):
    w = jnp.arange(GRID_W)[:, None]
    c = jnp.arange(GRID_W)[None, :]
    c0 = jnp.clip(w - kw // 2, 0, GRID_W - kw)
    inside = (c >= c0) & (c < c0 + kw)
    cb = jnp.clip(c - w + kw - 1, 0, 2 * kw - 2)
    t = jnp.where(inside[None, None], rpb.astype(F32)[:, :, cb], NEG_BIG)
    off = jnp.full_like(t, NEG_BIG)
    pair = lambda a, b: jnp.concatenate([a, b], axis=-1)
    return jnp.concatenate([pair(t[:, :-1], t[:, 1:]), pair(off, t), pair(t, off), pair(off, off)[:, :1]], axis=1)


def _na_mixer(x_c, x_l, ng, mod_c, mod_l, w_qkv, rpb, w_out, g1, want_ctx):
    b, l, d = x_l.shape
    lc = x_c.shape[1]
    heads = rpb.shape[0]
    na_rows, na_cols = (rpb.shape[1] + 1) // 2, (rpb.shape[2] + 1) // 2
    dh = d // heads
    assert 2 * dh == LANES and l % GRID_W == 0
    rows = l // GRID_W
    kh = min(na_rows, rows)
    assert kh == na_rows and (kh + NA_ROW_BLOCK) % 2 == 0
    assert rows % NA_ROW_BLOCK == 0 and rows >= kh + NA_ROW_BLOCK
    scale = dh ** -0.5
    nhp = heads // 2
    qkv_c = _norm_mod_proj(x_c, ng, mod_c[0], mod_c[1], w_qkv, BF16)
    qkv_l = _norm_mod_proj(x_l, ng, mod_l[0], mod_l[1], w_qkv, BF16)
    t2 = _na_bias_table(rpb, kh, na_cols)
    o_l = pl.pallas_call(
        functools.partial(_na_lat_body, rows=rows, kh=kh, scale=scale),
        grid=(b, nhp),
        in_specs=[pl.BlockSpec((1, l, LANES), lambda i, p: (i, 0, p)),
                  pl.BlockSpec((1, l, LANES), lambda i, p: (i, 0, nhp + p)),
                  pl.BlockSpec((1, l, LANES), lambda i, p: (i, 0, 2 * nhp + p)),
                  pl.BlockSpec((1, lc, LANES), lambda i, p: (i, 0, nhp + p)),
                  pl.BlockSpec((1, lc, LANES), lambda i, p: (i, 0, 2 * nhp + p)),
                  pl.BlockSpec((2,) + t2.shape[1:], lambda i, p: (p, 0, 0, 0))],
        out_specs=pl.BlockSpec((1, l, LANES), lambda i, p: (i, 0, p)),
        out_shape=jax.ShapeDtypeStruct((b, l, d), BF16),
        compiler_params=_params("parallel", "parallel"),
        name="na_latent",
    )(qkv_l, qkv_l, qkv_l, qkv_c, qkv_c, t2)
    new_l = _out_call(_out_plain_body, [(o_l, 0, d)], [w_out], x_l, mod_l[2], g1, "na_out")
    new_c = None
    if want_ctx:
        o_c = pl.pallas_call(
            functools.partial(_na_ctx_body, scale=scale),
            grid=(b, nhp),
            in_specs=[pl.BlockSpec((1, lc, LANES), lambda i, p: (i, 0, p)),
                      pl.BlockSpec((1, lc, LANES), lambda i, p: (i, 0, nhp + p)),
                      pl.BlockSpec((1, lc, LANES), lambda i, p: (i, 0, 2 * nhp + p))],
            out_specs=pl.BlockSpec((1, lc, LANES), lambda i, p: (i, 0, p)),
            out_shape=jax.ShapeDtypeStruct((b, lc, d), BF16),
            compiler_params=_params("parallel", "parallel"),
            name="na_context",
        )(qkv_c, qkv_c, qkv_c)
        new_c = _out_call(_out_plain_body, [(o_c, 0, d)], [w_out], x_c, mod_c[2], g1, "na_out")
    return new_c, new_l


S5_GROUP_BLOCK = 8


def _s5_kernel_body(c_ref, w_ref, o_ref):
    for i in range(c_ref.shape[0]):
        o_ref[i] = jnp.dot(c_ref[i], w_ref[i], preferred_element_type=F32, precision=lax.Precision.HIGHEST)


def _s5_impulse(cmat, wmat):
    n = cmat.shape[0]
    gb = S5_GROUP_BLOCK
    return pl.pallas_call(
        _s5_kernel_body,
        grid=(n // gb,),
        in_specs=[pl.BlockSpec((gb,) + cmat.shape[1:], lambda i: (i, 0, 0)),
                  pl.BlockSpec((gb,) + wmat.shape[1:], lambda i: (i, 0, 0))],
        out_specs=pl.BlockSpec((gb, cmat.shape[1], wmat.shape[2]), lambda i: (i, 0, 0)),
        out_shape=jax.ShapeDtypeStruct((n, cmat.shape[1], wmat.shape[2]), F32),
        compiler_params=_params("parallel"),
        name="s5_impulse",
    )(cmat, wmat)


def _s5_operators(lam_re, lam_im, log_dt, b_re, b_im, c_re, c_im, t):
    _, g, p = lam_re.shape
    cg = b_re.shape[-1]
    lam = lax.complex(lam_re.astype(F32), lam_im.astype(F32))
    ldt = lam * jnp.exp(log_dt.astype(F32))[..., None]
    a = jnp.exp(ldt)
    bbar = ((a - 1.0) / lam)[..., None] * lax.complex(b_re.astype(F32), b_im.astype(F32))
    cm = lax.complex(c_re.astype(F32), c_im.astype(F32))
    apow = jnp.exp(ldt[..., None] * jnp.arange(t + 1, dtype=F32))
    w = apow[..., :t, None] * bbar[:, :, :, None, :]
    wmat = jnp.concatenate([jnp.real(w), jnp.imag(w)], axis=2).reshape(2 * g, 2 * p, t * cg)
    cmat = jnp.concatenate([jnp.real(cm), -jnp.imag(cm)], axis=-1).reshape(2 * g, cg, 2 * p)
    k = _s5_impulse(cmat, wmat).reshape(2, g, cg, t, cg)
    k = jnp.transpose(k, (0, 1, 3, 4, 2))
    kfull = jnp.concatenate([k[1, :, :0:-1], (k[0, :, :1] + k[1, :, :1]), k[0, :, 1:]], axis=1)
    idx = jnp.arange(t)[None, :] - jnp.arange(t)[:, None] + (t - 1)
    toep = jnp.transpose(kfull[:, idx], (0, 1, 3, 2, 4)).reshape(g, t * cg, t * cg)

    def state_in(wd, flip):
        wd = wd[:, :, ::-1] if flip else wd
        m = jnp.transpose(wd, (0, 2, 3, 1)).reshape(g, t * cg, p)
        return jnp.concatenate([jnp.real(m), jnp.imag(m)], axis=-1)

    def state_out(cd, pw):
        n = cd[:, :, :, None] * pw[:, None, :, :]
        n = jnp.transpose(n, (0, 2, 3, 1)).reshape(g, p, t * cg)
        return jnp.concatenate([jnp.real(n), -jnp.imag(n)], axis=1)

    m_f = state_in(w[0], True)
    m_b = state_in(w[1], False)
    n_f = state_out(cm[0], apow[0][..., 1:])
    n_b = state_out(cm[1], apow[1][..., :0:-1])
    at = apow[..., t]
    a1 = jnp.concatenate([jnp.real(at), jnp.real(at)], axis=-1)
    a2 = jnp.concatenate([-jnp.imag(at), jnp.imag(at)], axis=-1)
    bf = lambda m: m.astype(BF16)
    return dict(toep=bf(toep), m_f=bf(m_f), m_b=bf(m_b), n_f=bf(n_f), n_b=bf(n_b), a1=a1, a2=a2)


def _s5_body(u_ref, dsk_ref, toep_ref, mf_ref, mb_ref, nf_ref, nb_ref, a1_ref, a2_ref, x0_ref,
             y_ref, xfin_ref, z_scr, zs_scr, xin_scr):
    gb, nc = u_ref.shape[1], u_ref.shape[2]
    half = LANES // 2
    for gi in range(gb):
        ub = u_ref[0, gi].astype(BF16)
        z_scr[0, :, gi, :] = jnp.dot(ub, mf_ref[gi], preferred_element_type=F32)
        z_scr[1, :, gi, :] = jnp.dot(ub, mb_ref[gi], preferred_element_type=F32)

    swap = lambda t: pltpu.roll(t, half, t.ndim - 1)
    for dr in range(2):
        zs_scr[dr] = swap(z_scr[dr].reshape(nc * gb, LANES)).reshape(nc, gb, LANES)

    coef = [(a1_ref[dr], a2_ref[dr]) for dr in range(2)]

    def scan(j, xs):
        out = []
        for dr in range(2):
            jj = j if dr == 0 else nc - 1 - j
            x, xw = xs[2 * dr], xs[2 * dr + 1]
            a1, a2 = coef[dr]
            xin_scr[dr, jj] = x
            out.append(a1 * x + a2 * xw + z_scr[dr, jj])
            out.append(a1 * xw - a2 * x + zs_scr[dr, jj])
        return tuple(out)

    x0f, x0b = x0_ref[0, 0], x0_ref[0, 1]
    xf, _, xb, _ = lax.fori_loop(0, nc, scan, (x0f, swap(x0f), x0b, swap(x0b)))
    xfin_ref[0, 0] = xf
    xfin_ref[0, 1] = xb
    for gi in range(gb):
        u = u_ref[0, gi]
        y = jnp.dot(u.astype(BF16), toep_ref[gi], preferred_element_type=F32) + u * dsk_ref[gi]
        y = y + jnp.dot(xin_scr[0, :, gi, :].astype(BF16), nf_ref[gi], preferred_element_type=F32)
        y = y + jnp.dot(xin_scr[1, :, gi, :].astype(BF16), nb_ref[gi], preferred_element_type=F32)
        y_ref[0, gi] = y


def _s5_scan(h, ops, dsk, x0):
    b, l, d = h.shape
    g = ops["toep"].shape[0]
    cg = d // g
    t = S5_CHUNK
    nc = l // t
    tc = t * cg
    gb = S5_GROUP_BLOCK
    u = jnp.transpose(h.reshape(b, nc, t, g, cg), (0, 3, 1, 2, 4)).reshape(b, g, nc, tc)
    per_g = lambda shape: pl.BlockSpec((gb,) + shape, lambda j, i: (j,) + (0,) * len(shape))
    y, xfin = pl.pallas_call(
        _s5_body,
        grid=(g // gb, b),
        in_specs=[pl.BlockSpec((1, gb, nc, tc), lambda j, i: (i, j, 0, 0)),
                  per_g((1, tc)), per_g((tc, tc)), per_g((tc, LANES)), per_g((tc, LANES)),
                  per_g((LANES, tc)), per_g((LANES, tc)),
                  pl.BlockSpec((2, gb, LANES), lambda j, i: (0, j, 0)),
                  pl.BlockSpec((2, gb, LANES), lambda j, i: (0, j, 0)),
                  pl.BlockSpec((1, 2, gb, LANES), lambda j, i: (i, 0, j, 0))],
        out_specs=[pl.BlockSpec((1, gb, nc, tc), lambda j, i: (i, j, 0, 0)),
                   pl.BlockSpec((1, 2, gb, LANES), lambda j, i: (i, 0, j, 0))],
        out_shape=[jax.ShapeDtypeStruct((b, g, nc, tc), F32), jax.ShapeDtypeStruct((b, 2, g, LANES), F32)],
        scratch_shapes=[pltpu.VMEM((2, nc, gb, LANES), F32)] * 3,
        compiler_params=_params("parallel", "parallel"),
        name="s5_scan",
    )(u, dsk, ops["toep"], ops["m_f"], ops["m_b"], ops["n_f"], ops["n_b"], ops["a1"], ops["a2"], x0)
    y = jnp.transpose(y.reshape(b, g, nc, t, cg), (0, 2, 3, 1, 4)).reshape(b, l, d)
    return y, xfin


def _s5_mixer(x_c, x_l, ng, mod_c, mod_l, lam_re, lam_im, log_dt, b_re, b_im, c_re, c_im, d_skip, w_glu, g1,
              want_ctx):
    b, l, d = x_l.shape
    g, p = lam_re.shape[1], lam_re.shape[2]
    cg = d // g
    assert 2 * p == LANES and g % S5_GROUP_BLOCK == 0
    ops = _s5_operators(lam_re, lam_im, log_dt, b_re, b_im, c_re, c_im, S5_CHUNK)
    dsk = jnp.tile(d_skip.astype(F32).reshape(g, 1, cg), (1, 1, S5_CHUNK))
    h_c = _norm_mod_only(x_c, ng, mod_c[0], mod_c[1])
    h_l = _norm_mod_only(x_l, ng, mod_l[0], mod_l[1])
    y_c, x_ctx = _s5_scan(h_c, ops, dsk, jnp.zeros((b, 2, g, LANES), F32))
    y_l, _ = _s5_scan(h_l, ops, dsk, x_ctx)
    new_l = _out_call(_out_glu_body, [(y_l, 0, d)], [w_glu], x_l, mod_l[2], g1, "s5_glu_out")
    new_c = None
    if want_ctx:
        new_c = _out_call(_out_glu_body, [(y_c, 0, d)], [w_glu], x_c, mod_c[2], g1, "s5_glu_out")
    return new_c, new_l


FFN_HALO = 16
FFN_TN = 256
FFN_ROW_SPLIT = 256


def _ffn_body(xp_ref, x_ref, xn_ref, g2_ref, sh_ref, sc_ref, wa_ref, wv_ref, cwa_ref, cwv_ref, cba_ref, cbv_ref,
              wo_ref, gate_ref, g3_ref, o_ref, h_scr, gat_scr, *, per):
    i, j = pl.program_id(0), pl.program_id(1)
    tm = x_ref.shape[0]
    hl = FFN_HALO

    @pl.when(j == 0)
    def _():
        g, sh, sc = g2_ref[...], sh_ref[0], sc_ref[0]
        keep_p = ((i % per) != 0).astype(F32)
        keep_n = ((i % per) != per - 1).astype(F32)
        h_scr[0:hl] = (_norm_mod(xp_ref[...], g, sh, sc) * keep_p).astype(BF16)
        h_scr[hl:hl + tm] = _norm_mod(x_ref[...], g, sh, sc).astype(BF16)
        h_scr[hl + tm:] = (_norm_mod(xn_ref[...], g, sh, sc) * keep_n).astype(BF16)

    sub = FFN_ROW_SPLIT if tm % FFN_ROW_SPLIT == 0 else tm
    ns = tm // sub
    edge = SUBLANES
    tn = wa_ref.shape[1]
    cols = pl.ds(pl.multiple_of(j * tn, tn), tn)

    def pieces(w_ref):
        out = []
        for s in range(ns):
            lo = hl + s * sub - (hl if s == 0 else 0)
            hi = hl + (s + 1) * sub + (hl if s == ns - 1 else 0)
            out.append(jnp.dot(h_scr[lo:hi], w_ref[...], preferred_element_type=F32))
        return out

    def conv(p, s, cw_ref, cb_ref):
        base = hl if s == 0 else 0
        left = p[s][hl - edge:hl] if s == 0 else p[s - 1][-edge:]
        right = p[s][base + sub:base + sub + edge] if s == ns - 1 else p[s + 1][:edge]
        ext = jnp.concatenate([left, p[s][base:base + sub], right], axis=0)
        um = pltpu.roll(ext, 1, 0)[edge:edge + sub]
        up = pltpu.roll(ext, sub + 2 * edge - 1, 0)[edge:edge + sub]
        return cb_ref[...] + um * cw_ref[0:1] + ext[edge:edge + sub] * cw_ref[1:2] + up * cw_ref[2:3]

    pa, pv = pieces(wa_ref), pieces(wv_ref)
    for s in range(ns):
        a = conv(pa, s, cwa_ref, cba_ref)
        v = conv(pv, s, cwv_ref, cbv_ref)
        gat_scr[s * sub:(s + 1) * sub, cols] = (a * jax.nn.sigmoid(a) * v).astype(BF16)

    @pl.when(j == pl.num_programs(1) - 1)
    def _():
        y = jnp.dot(gat_scr[...], wo_ref[...], preferred_element_type=F32)
        o_ref[...] = x_ref[...] + gate_ref[0] * _rms(y, g3_ref[...])


def _conv_ffn_block(x, ng2, mod, w_in, conv_w, conv_b, w_out, ng3):
    b, l, d = x.shape
    f = w_out.shape[0]
    tn, hl = FFN_TN, FFN_HALO
    assert f % tn == 0 and conv_w.shape[0] == 3 and w_in.shape[1] == 2 * f
    nj = f // tn
    tm = _row_tile(l, 1024)
    per = l // tm
    hb = tm // hl
    last_hb = b * l // hl - 1
    cb = conv_b.reshape(1, 2 * f)
    x2 = x.reshape(b * l, d)
    out = pl.pallas_call(
        functools.partial(_ffn_body, per=per),
        grid=(b * per, nj),
        in_specs=[pl.BlockSpec((hl, d), lambda i, j: (jnp.maximum(i * hb - 1, 0), 0)),
                  pl.BlockSpec((tm, d), lambda i, j: (i, 0)),
                  pl.BlockSpec((hl, d), lambda i, j: (jnp.minimum((i + 1) * hb, last_hb), 0)),
                  pl.BlockSpec((1, d), lambda i, j: (0, 0)),
                  pl.BlockSpec((1, 1, d), lambda i, j: (i // per, 0, 0)),
                  pl.BlockSpec((1, 1, d), lambda i, j: (i // per, 0, 0)),
                  pl.BlockSpec((d, tn), lambda i, j: (0, j)),
                  pl.BlockSpec((d, tn), lambda i, j: (0, nj + j)),
                  pl.BlockSpec((3, tn), lambda i, j: (0, j)),
                  pl.BlockSpec((3, tn), lambda i, j: (0, nj + j)),
                  pl.BlockSpec((1, tn), lambda i, j: (0, j)),
                  pl.BlockSpec((1, tn), lambda i, j: (0, nj + j)),
                  pl.BlockSpec((f, d), lambda i, j: (0, 0), pipeline_mode=pl.Buffered(1)),
                  pl.BlockSpec((1, 1, d), lambda i, j: (i // per, 0, 0)),
                  pl.BlockSpec((1, d), lambda i, j: (0, 0))],
        out_specs=pl.BlockSpec((tm, d), lambda i, j: (i, 0)),
        out_shape=jax.ShapeDtypeStruct((b * l, d), F32),
        scratch_shapes=[pltpu.VMEM((tm + 2 * hl, d), BF16), pltpu.VMEM((tm, f), BF16)],
        compiler_params=_params("parallel", "arbitrary"),
        name="conv_ffn",
    )(x2, x2, x2, ng2.reshape(1, d), mod[3], mod[4], w_in, w_in, conv_w, conv_w, cb, cb, w_out, mod[5],
      ng3.reshape(1, d))
    return out.reshape(b, l, d)


def kernel(x, c, ctx, c_ctx, w_mod, b_mod, norm_g, a_w_in, a_lower_logits, a_out_g, a_w_out, b_w_qkv, b_rpb,
           b_w_out, c_lam_re, c_lam_im, c_log_dt, c_b_re, c_b_im, c_c_re, c_c_im, c_d, c_w_glu, f_w_in,
           f_conv_w, f_conv_b, f_w_out):
    bsz, _, d = x.shape
    depth = w_mod.shape[0]
    p = jax.nn.softmax(a_lower_logits.astype(F32), axis=0)
    lower = jnp.cumsum(p, axis=0) - p[0]
    n_rows = -(-(bsz + 1) // SUBLANES) * SUBLANES
    c_rows = jnp.zeros((n_rows, d), F32).at[:bsz].set(c).at[bsz].set(c_ctx)
    mods = _modulation(c_rows, w_mod, b_mod)
    wb = lambda w: w.astype(BF16)
    lat, cx = x, ctx
    for i in range(depth):
        kind, j = i % N_MIXERS, i // N_MIXERS
        last = i == depth - 1
        m = mods[i].reshape(n_rows, N_MOD, d)
        mod_l = [m[:bsz, k][:, None, :] for k in range(N_MOD)]
        mod_c = [jnp.broadcast_to(m[bsz, k][None, None, :], (bsz, 1, d)) for k in range(N_MOD)]
        ng = norm_g[i]
        if kind == 0:
            cx1, lat = _hgrn2_mixer(cx, lat, ng[0], mod_c, mod_l, wb(a_w_in[j]), lower[i], a_out_g[j],
                                    wb(a_w_out[j]), ng[1], not last)
        elif kind == 1:
            cx1, lat = _na_mixer(cx, lat, ng[0], mod_c, mod_l, wb(b_w_qkv[j]), b_rpb[j], wb(b_w_out[j]), ng[1],
                                 not last)
        else:
            cx1, lat = _s5_mixer(cx, lat, ng[0], mod_c, mod_l, c_lam_re[j], c_lam_im[j], c_log_dt[j], c_b_re[j],
                                 c_b_im[j], c_c_re[j], c_c_im[j], c_d[j], wb(c_w_glu[j]), ng[1], not last)
        lat = _conv_ffn_block(lat, ng[2], mod_l, wb(f_w_in[i]), f_conv_w[i], f_conv_b[i], wb(f_w_out[i]), ng[3])
        if not last:
            cx = _conv_ffn_block(cx1, ng[2], mod_c, wb(f_w_in[i]), f_conv_w[i], f_conv_b[i], wb(f_w_out[i]), ng[3])
    return lat
```

```python
import functools

import jax
import jax.numpy as jnp
from jax import lax
from jax.experimental import pallas as pl
from jax.experimental.pallas import tpu as pltpu

F32 = jnp.float32
BF16 = jnp.bfloat16

EPS = 1e-6
F_MIN = 1e-30
N_MOD = 6
N_MIXERS = 3
A_HEADS = 8
GRID_W = 64
S5_CHUNK = 32
NEG_BIG = -1e30
NA_ROW_BLOCK = 4
SUBLANES = 8
LANES = 128
VMEM_LIMIT_BYTES = 56 * 1024 * 1024

NT_DIMS = (((1,), (1,)), ((), ()))
TN_DIMS = (((0,), (0,)), ((), ()))


def _params(*sem):
    return pltpu.CompilerParams(dimension_semantics=sem, vmem_limit_bytes=VMEM_LIMIT_BYTES)


def _row_tile(n, want):
    t = min(n, want)
    assert n % t == 0, (n, t)
    return t


def _rms(y, g):
    return y * lax.rsqrt(jnp.mean(y * y, axis=-1, keepdims=True) + EPS) * g


def _mod_body(c_ref, w_ref, b_ref, o_ref):
    c = c_ref[...]
    s = (c * jax.nn.sigmoid(c)).astype(BF16)
    o_ref[0] = jnp.dot(s, w_ref[0].astype(BF16), preferred_element_type=F32) + b_ref[0]


def _modulation(c_rows, w_mod, b_mod):
    depth, d, n = w_mod.shape
    r = c_rows.shape[0]
    tn = n // 4
    return pl.pallas_call(
        _mod_body,
        grid=(depth, n // tn),
        in_specs=[pl.BlockSpec((r, d), lambda i, j: (0, 0)),
                  pl.BlockSpec((1, d, tn), lambda i, j: (i, 0, j)),
                  pl.BlockSpec((1, 1, tn), lambda i, j: (i, 0, j))],
        out_specs=pl.BlockSpec((1, r, tn), lambda i, j: (i, 0, j)),
        out_shape=jax.ShapeDtypeStruct((depth, r, n), F32),
        compiler_params=_params("parallel", "parallel"),
        name="adaln_mod",
    )(c_rows, w_mod, b_mod.reshape(depth, 1, n))


def _norm_mod(x, g, sh, sc):
    return _rms(x, g) * (1.0 + sc) + sh


def _proj_body(x_ref, g_ref, sh_ref, sc_ref, w_ref, o_ref, h_scr):
    @pl.when(pl.program_id(1) == 0)
    def _():
        h_scr[...] = _norm_mod(x_ref[...], g_ref[...], sh_ref[0], sc_ref[0]).astype(BF16)

    o_ref[...] = jnp.dot(h_scr[...], w_ref[...], preferred_element_type=F32).astype(o_ref.dtype)


def _norm_mod_proj(x, g, sh, sc, w, out_dtype):
    b, l, d = x.shape
    n = w.shape[1]
    tm = _row_tile(l, 1024)
    tn = _row_tile(n, 1024)
    per = l // tm
    out = pl.pallas_call(
        _proj_body,
        grid=(b * per, n // tn),
        in_specs=[pl.BlockSpec((tm, d), lambda i, j: (i, 0)),
                  pl.BlockSpec((1, d), lambda i, j: (0, 0)),
                  pl.BlockSpec((1, 1, d), lambda i, j: (i // per, 0, 0)),
                  pl.BlockSpec((1, 1, d), lambda i, j: (i // per, 0, 0)),
                  pl.BlockSpec((d, tn), lambda i, j: (0, j))],
        out_specs=pl.BlockSpec((tm, tn), lambda i, j: (i, j)),
        out_shape=jax.ShapeDtypeStruct((b * l, n), out_dtype),
        scratch_shapes=[pltpu.VMEM((tm, d), BF16)],
        compiler_params=_params("parallel", "arbitrary"),
        name="norm_mod_proj",
    )(x.reshape(b * l, d), g.reshape(1, d), sh, sc, w)
    return out.reshape(b, l, n)


def _norm_only_body(x_ref, g_ref, sh_ref, sc_ref, o_ref):
    o_ref[...] = _norm_mod(x_ref[...], g_ref[...], sh_ref[0], sc_ref[0])


def _norm_mod_only(x, g, sh, sc):
    b, l, d = x.shape
    tm = _row_tile(l, 512)
    per = l // tm
    out = pl.pallas_call(
        _norm_only_body,
        grid=(b * per,),
        in_specs=[pl.BlockSpec((tm, d), lambda i: (i, 0)),
                  pl.BlockSpec((1, d), lambda i: (0, 0)),
                  pl.BlockSpec((1, 1, d), lambda i: (i // per, 0, 0)),
                  pl.BlockSpec((1, 1, d), lambda i: (i // per, 0, 0))],
        out_specs=pl.BlockSpec((tm, d), lambda i: (i, 0)),
        out_shape=jax.ShapeDtypeStruct((b * l, d), F32),
        compiler_params=_params("parallel"),
        name="norm_mod",
    )(x.reshape(b * l, d), g.reshape(1, d), sh, sc)
    return out.reshape(b, l, d)


def _residual(res_ref, gate_ref, g1_ref, y):
    return res_ref[...] + gate_ref[0] * _rms(y, g1_ref[...])


def _out_plain_body(a_ref, w_ref, res_ref, gate_ref, g1_ref, o_ref):
    y = jnp.dot(a_ref[...], w_ref[...], preferred_element_type=F32)
    o_ref[...] = _residual(res_ref, gate_ref, g1_ref, y)


def _out_hgrn_body(of_ref, ob_ref, gp_ref, og_ref, w_ref, res_ref, gate_ref, g1_ref, o_ref):
    o = of_ref[...] + ob_ref[...]
    parts = []
    for h in range(A_HEADS):
        oh = o[:, h * LANES:(h + 1) * LANES]
        parts.append(oh * lax.rsqrt(jnp.mean(oh * oh, axis=-1, keepdims=True) + EPS))
    gp = gp_ref[...]
    a = jnp.concatenate(parts, axis=-1) * og_ref[...] * (gp * jax.nn.sigmoid(gp))
    y = jnp.dot(a.astype(BF16), w_ref[...], preferred_element_type=F32)
    o_ref[...] = _residual(res_ref, gate_ref, g1_ref, y)


def _gelu_tanh(y):
    return 0.5 * y * (1.0 + jnp.tanh(0.7978845608028654 * (y + 0.044715 * (y * y * y))))


def _out_glu_body(y_ref, w_ref, res_ref, gate_ref, g1_ref, o_ref):
    d = o_ref.shape[-1]
    a = _gelu_tanh(y_ref[...]).astype(BF16)
    ag = jnp.dot(a, w_ref[...], preferred_element_type=F32)
    y = ag[:, :d] * jax.nn.sigmoid(ag[:, d:])
    o_ref[...] = _residual(res_ref, gate_ref, g1_ref, y)


def _out_call(body, row_inputs, const_inputs, res, gate, g1, name):
    b, l, d = res.shape
    tm = _row_tile(l, 512)
    per = l // tm
    in_specs, args = [], []
    for arr, blk, width in row_inputs:
        in_specs.append(pl.BlockSpec((tm, width), lambda i, blk=blk: (i, blk)))
        args.append(arr.reshape(b * l, arr.shape[-1]))
    for arr in const_inputs:
        in_specs.append(pl.BlockSpec(arr.shape, lambda i, nd=arr.ndim: (0,) * nd))
        args.append(arr)
    in_specs += [pl.BlockSpec((tm, d), lambda i: (i, 0)),
                 pl.BlockSpec((1, 1, d), lambda i: (i // per, 0, 0)),
                 pl.BlockSpec((1, d), lambda i: (0, 0))]
    args += [res.reshape(b * l, d), gate, g1.reshape(1, d)]
    out = pl.pallas_call(
        body,
        grid=(b * per,),
        in_specs=in_specs,
        out_specs=pl.BlockSpec((tm, d), lambda i: (i, 0)),
        out_shape=jax.ShapeDtypeStruct((b * l, d), F32),
        compiler_params=_params("parallel"),
        name=name,
    )(*args)
    return out.reshape(b, l, d)


HGRN_CHUNK = 32
HGRN_SAFE_MIN = 1e-30
HGRN_Q_HEADROOM = 1e37


def _hgrn_gates(z, lb):
    f = lb + (1.0 - lb) * jax.nn.sigmoid(z)
    return jnp.maximum(f, F_MIN), 1.0 - f


def _hgrn_chunk_prep(q, v, z, lb, reverse):
    c = q.shape[0]
    half = c // 2
    fm, kk = _hgrn_gates(z, lb)
    pos = lax.broadcasted_iota(jnp.int32, (c, LANES), 0)
    if reverse:
        pos = (c - 1) - pos
    second = pos >= half
    hpos = jnp.where(second, pos - half, pos)
    ph = fm
    j = 1
    while j < half:
        ph = ph * jnp.where(hpos >= j, pltpu.roll(ph, (c - j) if reverse else j, 0), 1.0)
        j *= 2
    a_last = ph[half:half + 1] if reverse else ph[half - 1:half]
    r_last = ph[0:1] if reverse else ph[c - 1:c]
    p_mid = a_last
    p_last = a_last * r_last
    pp = jnp.where(second, ph * a_last, ph)
    r = jnp.where(second, ph, ph * (1.0 / a_last))
    kd = kk * (1.0 / r)
    q_max = jnp.max(jnp.abs(q), axis=0, keepdims=True)
    ok = jnp.where(p_mid >= HGRN_SAFE_MIN, r_last, 0.0) >= HGRN_SAFE_MIN
    ok = jnp.where(ok, p_mid * HGRN_Q_HEADROOM, -1.0) >= q_max
    return dict(qr=(q * r).astype(BF16), kd=kd.astype(BF16), qp=(q * pp).astype(BF16),
                kl=(kd * r_last).astype(BF16), vb=v.astype(BF16), dec=p_last, bad=jnp.where(ok, 0.0, 1.0))


def _hgrn_chunk_dots(p, st):
    sc = lax.dot_general(p["qr"], p["kd"], NT_DIMS, preferred_element_type=F32)
    o_st = lax.dot_general(p["qp"], st.astype(BF16), NT_DIMS, preferred_element_type=F32)
    u = lax.dot_general(p["vb"], p["kl"], TN_DIMS, preferred_element_type=F32)
    return sc, o_st, u


def _hgrn_chunk_out(p, sc, o_st, reverse):
    c = sc.shape[0]
    row = lax.broadcasted_iota(jnp.int32, (c, c), 0)
    col = lax.broadcasted_iota(jnp.int32, (c, c), 1)
    sc = jnp.where((col >= row) if reverse else (col <= row), sc, 0.0)
    return jnp.dot(sc.astype(BF16), p["vb"], preferred_element_type=F32) + o_st


def _hgrn_exact_block(q_ref, v_ref, z_ref, lb_ref, o_ref, st_scr, dr, reverse, n_tiles):
    pos = lax.broadcasted_iota(jnp.int32, (SUBLANES, LANES), 0)
    if reverse:
        pos = (SUBLANES - 1) - pos

    def earlier(x, j):
        if j == 0:
            return x
        return pltpu.roll(x, (SUBLANES - j) if reverse else j, 0)

    def later(x, j):
        return pltpu.roll(x, j if reverse else (SUBLANES - j), 0)

    def tile(i, carry):
        ti = (n_tiles - 1 - i) if reverse else i
        r0 = pl.multiple_of(ti * SUBLANES, SUBLANES)
        for h in range(A_HEADS):
            sl = slice(h * LANES, (h + 1) * LANES)
            lb = lb_ref[:, sl]
            z = z_ref[0, pl.ds(r0, SUBLANES), sl]
            q = q_ref[0, pl.ds(r0, SUBLANES), sl]
            v = v_ref[0, pl.ds(r0, SUBLANES), sl]
            fm, kk = _hgrn_gates(z, lb)
            pp = fm
            for j in (1, 2, 4):
                pp = pp * jnp.where(pos >= j, earlier(pp, j), 1.0)
            qq = jnp.where(pos <= SUBLANES - 2, later(fm, 1), 1.0)
            for j in (1, 2, 4):
                qq = qq * jnp.where(pos <= SUBLANES - 1 - j, later(qq, j), 1.0)
            dec = pp[0:1] if reverse else pp[SUBLANES - 1:SUBLANES]
            st = st_scr[dr, h]
            o = lax.dot_general((q * pp).astype(BF16), st.astype(BF16), NT_DIMS,
                                preferred_element_type=F32)
            g = fm
            for d in range(SUBLANES):
                if d == 0:
                    e = q * kk
                else:
                    if d > 1:
                        g = g * earlier(fm, d - 1)
                    e = q * g * jnp.where(pos >= d, earlier(kk, d), 0.0)
                o = o + jnp.sum(e, axis=-1, keepdims=True) * earlier(v, d)
            o_ref[0, pl.ds(r0, SUBLANES), sl] = o
            u = lax.dot_general(v.astype(BF16), (kk * qq).astype(BF16), TN_DIMS,
                                preferred_element_type=F32)
            st_scr[dr, h] = dec * st + u
        return carry

    lax.fori_loop(0, n_tiles, tile, 0)


def _hgrn_scan_body(qf_ref, vf_ref, zf_ref, qb_ref, vb_ref, zb_ref, lb_ref, s0_ref, of_ref, ob_ref, sfin_ref,
                    st_scr, save_scr, *, n_chunks):
    step = pl.program_id(1)

    @pl.when(step == 0)
    def _():
        st_scr[...] = s0_ref[0]

    save_scr[...] = st_scr[...]
    dirs = ((qf_ref, vf_ref, zf_ref, of_ref, False), (qb_ref, vb_ref, zb_ref, ob_ref, True))
    c = HGRN_CHUNK

    def chunk(i, bad):
        work = []
        for dr, (q_ref, v_ref, z_ref, o_ref, reverse) in enumerate(dirs):
            rows = pl.ds(pl.multiple_of(((n_chunks - 1 - i) if reverse else i) * c, c), c)
            for h in range(A_HEADS):
                sl = slice(h * LANES, (h + 1) * LANES)
                p = _hgrn_chunk_prep(q_ref[0, rows, sl], v_ref[0, rows, sl], z_ref[0, rows, sl], lb_ref[:, sl],
                                     reverse)
                bad = jnp.maximum(bad, p["bad"])
                work.append((dr, h, o_ref, rows, sl, reverse, p))
        dots = [_hgrn_chunk_dots(p, st_scr[dr, h]) for dr, h, _, _, _, _, p in work]
        for (dr, h, o_ref, rows, sl, reverse, p), (sc, o_st, u) in zip(work, dots):
            o_ref[0, rows, sl] = _hgrn_chunk_out(p, sc, o_st, reverse)
            st_scr[dr, h] = p["dec"] * st_scr[dr, h] + u
        return bad

    bad = lax.fori_loop(0, n_chunks, chunk, jnp.zeros((1, LANES), F32))

    @pl.when(jnp.max(bad) > 0.0)
    def _():
        st_scr[...] = save_scr[...]
        for dr, (q_ref, v_ref, z_ref, o_ref, reverse) in enumerate(dirs):
            _hgrn_exact_block(q_ref, v_ref, z_ref, lb_ref, o_ref, st_scr, dr, reverse, n_chunks * c // SUBLANES)

    @pl.when(step == pl.num_programs(1) - 1)
    def _():
        sfin_ref[0] = st_scr[...]


def _hgrn_scan(proj, lb, s0):
    b, l, _ = proj.shape
    d = A_HEADS * LANES
    t = _row_tile(l, 256)
    nb = l // t
    fwd = lambda blk: pl.BlockSpec((1, t, d), lambda i, s: (i, s, blk))
    bwd = lambda blk: pl.BlockSpec((1, t, d), lambda i, s: (i, nb - 1 - s, blk))
    st_spec = pl.BlockSpec((1,) + s0.shape[1:], lambda i, s: (i, 0, 0, 0, 0))
    return pl.pallas_call(
        functools.partial(_hgrn_scan_body, n_chunks=t // HGRN_CHUNK),
        grid=(b, nb),
        in_specs=[fwd(0), fwd(1), fwd(3), bwd(0), bwd(1), bwd(4),
                  pl.BlockSpec((1, d), lambda i, s: (0, 0)), st_spec],
        out_specs=[fwd(0), bwd(0), st_spec],
        out_shape=[jax.ShapeDtypeStruct((b, l, d), F32), jax.ShapeDtypeStruct((b, l, d), F32),
                   jax.ShapeDtypeStruct(s0.shape, F32)],
        scratch_shapes=[pltpu.VMEM(s0.shape[1:], F32), pltpu.VMEM(s0.shape[1:], F32)],
        compiler_params=_params("parallel", "arbitrary"),
        name="hgrn_scan",
    )(proj, proj, proj, proj, proj, proj, lb, s0)


def _hgrn2_mixer(x_c, x_l, ng, mod_c, mod_l, w_in, lower, out_g, w_out, g1, want_ctx):
    d = x_l.shape[-1]
    p_c = _norm_mod_proj(x_c, ng, mod_c[0], mod_c[1], w_in, F32)
    p_l = _norm_mod_proj(x_l, ng, mod_l[0], mod_l[1], w_in, F32)
    lb = lower.reshape(1, d)
    zero = jnp.zeros((x_l.shape[0], 2, A_HEADS, LANES, LANES), F32)
    ocf, ocb, s_ctx = _hgrn_scan(p_c, lb, zero)
    olf, olb, _ = _hgrn_scan(p_l, lb, s_ctx)
    o_c, o_l = (ocf, ocb), (olf, olb)
    og = out_g.reshape(1, d)
    new_l = _out_call(_out_hgrn_body, [(o_l[0], 0, d), (o_l[1], 0, d), (p_l, 2, d)], [og, w_out],
                      x_l, mod_l[2], g1, "hgrn_out")
    new_c = None
    if want_ctx:
        new_c = _out_call(_out_hgrn_body, [(o_c[0], 0, d), (o_c[1], 0, d), (p_c, 2, d)], [og, w_out],
                          x_c, mod_c[2], g1, "hgrn_out")
    return new_c, new_l


def _softmax_pv(s_list, v_list):
    mx = functools.reduce(jnp.maximum, [jnp.max(s, axis=-1, keepdims=True) for s in s_list])
    ps = [jnp.exp(s - mx) for s in s_list]
    den = functools.reduce(jnp.add, [jnp.sum(p, axis=-1, keepdims=True) for p in ps])
    acc = functools.reduce(jnp.add, [jnp.dot(p.astype(BF16), v, preferred_element_type=F32)
                                     for p, v in zip(ps, v_list)])
    return acc / den


def _na_lat_body(q_ref, k_ref, v_ref, kc_ref, vc_ref, t2_ref, o_ref, *, rows, kh, scale):
    w = GRID_W
    rb = NA_ROW_BLOCK
    nk = kh + rb
    n_rel = 2 * kh - 1
    first = lax.broadcasted_iota(jnp.int32, (rb * w, LANES), 1) < (LANES // 2)
    kc = kc_ref[0]
    vc = vc_ref[0]

    def bias_index(r, ks, m):
        r0 = jnp.clip(r - kh // 2, 0, rows - kh)
        ka = ks + 2 * m
        rel_a = ka - r + (kh - 1)
        in_a = (ka >= r0) & (ka < r0 + kh)
        in_b = (ka + 1 >= r0) & (ka + 1 < r0 + kh)
        both, only_b, only_a = rel_a, (n_rel - 1) + rel_a + 1, (2 * n_rel - 1) + rel_a
        return jnp.where(in_a, jnp.where(in_b, both, only_a), jnp.where(in_b, only_b, 3 * n_rel - 1))

    def block(i, carry):
        rq = i * rb
        ks = jnp.clip(rq - kh // 2, 0, rows - nk)
        q2 = q_ref[0, pl.ds(pl.multiple_of(rq * w, rb * w), rb * w), :] * scale
        k2 = k_ref[0, pl.ds(pl.multiple_of(ks * w, w), nk * w), :]
        v2 = v_ref[0, pl.ds(pl.multiple_of(ks * w, w), nk * w), :]
        outs = []
        for hh in range(2):
            qm = jnp.where(first if hh == 0 else jnp.logical_not(first), q2, jnp.zeros_like(q2))
            bias = jnp.concatenate(
                [jnp.concatenate([t2_ref[hh, bias_index(rq + a, ks, m)] for m in range(nk // 2)], axis=-1)
                 for a in range(rb)], axis=0)
            s_w = lax.dot_general(qm, k2, NT_DIMS, preferred_element_type=F32) + bias
            s_c = lax.dot_general(qm, kc, NT_DIMS, preferred_element_type=F32)
            outs.append(_softmax_pv([s_w, s_c], [v2, vc]))
        o_ref[0, pl.ds(pl.multiple_of(rq * w, rb * w), rb * w), :] = (
            jnp.where(first, outs[0], outs[1]).astype(o_ref.dtype))
        return carry

    lax.fori_loop(0, rows // rb, block, 0, unroll=2)


def _na_ctx_body(q_ref, k_ref, v_ref, o_ref, *, scale):
    n = q_ref.shape[1]
    first = lax.broadcasted_iota(jnp.int32, (n, LANES), 1) < (LANES // 2)
    q2 = q_ref[0] * scale
    k2 = k_ref[0]
    v2 = v_ref[0]
    outs = []
    for hh in range(2):
        qm = jnp.where(first if hh == 0 else jnp.logical_not(first), q2, jnp.zeros_like(q2))
        s = lax.dot_general(qm, k2, NT_DIMS, preferred_element_type=F32)
        outs.append(_softmax_pv([s], [v2]))
    o_ref[0] = jnp.where(first, outs[0], outs[1]).astype(o_ref.dtype)


def _na_bias_table(rpb, kh, kw):
    w = jnp.arange(GRID_W)[:, None]
    c = jnp.arange(GRID_W)[None, :]
    c0 = jnp.clip(w - kw // 2, 0, GRID_W - kw)
    inside = (c >= c0) & (c < c0 + kw)
    cb = jnp.clip(c - w + kw - 1, 0, 2 * kw - 2)
    t = jnp.where(inside[None, None], rpb.astype(F32)[:, :, cb], NEG_BIG)
    off = jnp.full_like(t, NEG_BIG)
    pair = lambda a, b: jnp.concatenate([a, b], axis=-1)
    return jnp.concatenate([pair(t[:, :-1], t[:, 1:]), pair(off, t), pair(t, off), pair(off, off)[:, :1]], axis=1)


def _na_mixer(x_c, x_l, ng, mod_c, mod_l, w_qkv, rpb, w_out, g1, want_ctx):
    b, l, d = x_l.shape
    lc = x_c.shape[1]
    heads = rpb.shape[0]
    na_rows, na_cols = (rpb.shape[1] + 1) // 2, (rpb.shape[2] + 1) // 2
    dh = d // heads
    assert 2 * dh == LANES and l % GRID_W == 0
    rows = l // GRID_W
    kh = min(na_rows, rows)
    assert kh == na_rows and (kh + NA_ROW_BLOCK) % 2 == 0
    assert rows % NA_ROW_BLOCK == 0 and rows >= kh + NA_ROW_BLOCK
    scale = dh ** -0.5
    nhp = heads // 2
    qkv_c = _norm_mod_proj(x_c, ng, mod_c[0], mod_c[1], w_qkv, BF16)
    qkv_l = _norm_mod_proj(x_l, ng, mod_l[0], mod_l[1], w_qkv, BF16)
    t2 = _na_bias_table(rpb, kh, na_cols)
    o_l = pl.pallas_call(
        functools.partial(_na_lat_body, rows=rows, kh=kh, scale=scale),
        grid=(b, nhp),
        in_specs=[pl.BlockSpec((1, l, LANES), lambda i, p: (i, 0, p)),
                  pl.BlockSpec((1, l, LANES), lambda i, p: (i, 0, nhp + p)),
                  pl.BlockSpec((1, l, LANES), lambda i, p: (i, 0, 2 * nhp + p)),
                  pl.BlockSpec((1, lc, LANES), lambda i, p: (i, 0, nhp + p)),
                  pl.BlockSpec((1, lc, LANES), lambda i, p: (i, 0, 2 * nhp + p)),
                  pl.BlockSpec((2,) + t2.shape[1:], lambda i, p: (p, 0, 0, 0))],
        out_specs=pl.BlockSpec((1, l, LANES), lambda i, p: (i, 0, p)),
        out_shape=jax.ShapeDtypeStruct((b, l, d), BF16),
        compiler_params=_params("parallel", "parallel"),
        name="na_latent",
    )(qkv_l, qkv_l, qkv_l, qkv_c, qkv_c, t2)
    new_l = _out_call(_out_plain_body, [(o_l, 0, d)], [w_out], x_l, mod_l[2], g1, "na_out")
    new_c = None
    if want_ctx:
        o_c = pl.pallas_call(
            functools.partial(_na_ctx_body, scale=scale),
            grid=(b, nhp),
            in_specs=[pl.BlockSpec((1, lc, LANES), lambda i, p: (i, 0, p)),
                      pl.BlockSpec((1, lc, LANES), lambda i, p: (i, 0, nhp + p)),
                      pl.BlockSpec((1, lc, LANES), lambda i, p: (i, 0, 2 * nhp + p))],
            out_specs=pl.BlockSpec((1, lc, LANES), lambda i, p: (i, 0, p)),
            out_shape=jax.ShapeDtypeStruct((b, lc, d), BF16),
            compiler_params=_params("parallel", "parallel"),
            name="na_context",
        )(qkv_c, qkv_c, qkv_c)
        new_c = _out_call(_out_plain_body, [(o_c, 0, d)], [w_out], x_c, mod_c[2], g1, "na_out")
    return new_c, new_l


S5_GROUP_BLOCK = 8


def _s5_kernel_body(c_ref, w_ref, o_ref):
    for i in range(c_ref.shape[0]):
        o_ref[i] = jnp.dot(c_ref[i], w_ref[i], preferred_element_type=F32, precision=lax.Precision.HIGHEST)


def _s5_impulse(cmat, wmat):
    n = cmat.shape[0]
    gb = S5_GROUP_BLOCK
    return pl.pallas_call(
        _s5_kernel_body,
        grid=(n // gb,),
        in_specs=[pl.BlockSpec((gb,) + cmat.shape[1:], lambda i: (i, 0, 0)),
                  pl.BlockSpec((gb,) + wmat.shape[1:], lambda i: (i, 0, 0))],
        out_specs=pl.BlockSpec((gb, cmat.shape[1], wmat.shape[2]), lambda i: (i, 0, 0)),
        out_shape=jax.ShapeDtypeStruct((n, cmat.shape[1], wmat.shape[2]), F32),
        compiler_params=_params("parallel"),
        name="s5_impulse",
    )(cmat, wmat)


def _s5_operators(lam_re, lam_im, log_dt, b_re, b_im, c_re, c_im, t):
    _, g, p = lam_re.shape
    cg = b_re.shape[-1]
    lam = lax.complex(lam_re.astype(F32), lam_im.astype(F32))
    ldt = lam * jnp.exp(log_dt.astype(F32))[..., None]
    a = jnp.exp(ldt)
    bbar = ((a - 1.0) / lam)[..., None] * lax.complex(b_re.astype(F32), b_im.astype(F32))
    cm = lax.complex(c_re.astype(F32), c_im.astype(F32))
    apow = jnp.exp(ldt[..., None] * jnp.arange(t + 1, dtype=F32))
    w = apow[..., :t, None] * bbar[:, :, :, None, :]
    wmat = jnp.concatenate([jnp.real(w), jnp.imag(w)], axis=2).reshape(2 * g, 2 * p, t * cg)
    cmat = jnp.concatenate([jnp.real(cm), -jnp.imag(cm)], axis=-1).reshape(2 * g, cg, 2 * p)
    k = _s5_impulse(cmat, wmat).reshape(2, g, cg, t, cg)
    k = jnp.transpose(k, (0, 1, 3, 4, 2))
    kfull = jnp.concatenate([k[1, :, :0:-1], (k[0, :, :1] + k[1, :, :1]), k[0, :, 1:]], axis=1)
    lagvec = jnp.transpose(kfull, (0, 2, 1, 3)).reshape(g, cg, (2 * t - 1) * cg).astype(BF16)
    window = lambda s: lax.dynamic_slice_in_dim(lagvec, (t - 1 - s) * cg, t * cg, axis=2)
    toep = jax.vmap(window, out_axes=1)(jnp.arange(t)).reshape(g, t * cg, t * cg)

    def state_in(wd, flip):
        wd = wd[:, :, ::-1] if flip else wd
        m = jnp.transpose(wd, (0, 2, 3, 1)).reshape(g, t * cg, p)
        return jnp.concatenate([jnp.real(m), jnp.imag(m)], axis=-1)

    def state_out(cd, pw):
        n = cd[:, :, :, None] * pw[:, None, :, :]
        n = jnp.transpose(n, (0, 2, 3, 1)).reshape(g, p, t * cg)
        return jnp.concatenate([jnp.real(n), -jnp.imag(n)], axis=1)

    m_f = state_in(w[0], True)
    m_b = state_in(w[1], False)
    n_f = state_out(cm[0], apow[0][..., 1:])
    n_b = state_out(cm[1], apow[1][..., :0:-1])
    at = apow[..., t]
    a1 = jnp.concatenate([jnp.real(at), jnp.real(at)], axis=-1)
    a2 = jnp.concatenate([-jnp.imag(at), jnp.imag(at)], axis=-1)
    bf = lambda m: m.astype(BF16)
    return dict(toep=bf(toep), m_f=bf(m_f), m_b=bf(m_b), n_f=bf(n_f), n_b=bf(n_b), a1=a1, a2=a2)


def _s5_body(u_ref, dsk_ref, toep_ref, mf_ref, mb_ref, nf_ref, nb_ref, a1_ref, a2_ref, x0_ref,
             y_ref, xfin_ref, z_scr, zs_scr, xin_scr):
    gb, nc = u_ref.shape[1], u_ref.shape[2]
    half = LANES // 2
    for gi in range(gb):
        ub = u_ref[0, gi].astype(BF16)
        z_scr[0, :, gi, :] = jnp.dot(ub, mf_ref[gi], preferred_element_type=F32)
        z_scr[1, :, gi, :] = jnp.dot(ub, mb_ref[gi], preferred_element_type=F32)

    swap = lambda t: pltpu.roll(t, half, t.ndim - 1)
    for dr in range(2):
        zs_scr[dr] = swap(z_scr[dr].reshape(nc * gb, LANES)).reshape(nc, gb, LANES)

    coef = [(a1_ref[dr], a2_ref[dr]) for dr in range(2)]

    def scan(j, xs):
        out = []
        for dr in range(2):
            jj = j if dr == 0 else nc - 1 - j
            x, xw = xs[2 * dr], xs[2 * dr + 1]
            a1, a2 = coef[dr]
            xin_scr[dr, jj] = x
            out.append(a1 * x + a2 * xw + z_scr[dr, jj])
            out.append(a1 * xw - a2 * x + zs_scr[dr, jj])
        return tuple(out)

    x0f, x0b = x0_ref[0, 0], x0_ref[0, 1]
    xf, _, xb, _ = lax.fori_loop(0, nc, scan, (x0f, swap(x0f), x0b, swap(x0b)))
    xfin_ref[0, 0] = xf
    xfin_ref[0, 1] = xb
    for gi in range(gb):
        u = u_ref[0, gi]
        y = jnp.dot(u.astype(BF16), toep_ref[gi], preferred_element_type=F32) + u * dsk_ref[gi]
        y = y + jnp.dot(xin_scr[0, :, gi, :].astype(BF16), nf_ref[gi], preferred_element_type=F32)
        y = y + jnp.dot(xin_scr[1, :, gi, :].astype(BF16), nb_ref[gi], preferred_element_type=F32)
        y_ref[0, gi] = y


def _s5_scan(h, ops, dsk, x0):
    b, l, d = h.shape
    g = ops["toep"].shape[0]
    cg = d // g
    t = S5_CHUNK
    nc = l // t
    tc = t * cg
    gb = S5_GROUP_BLOCK
    u = jnp.transpose(h.reshape(b, nc, t, g, cg), (0, 3, 1, 2, 4)).reshape(b, g, nc, tc)
    per_g = lambda shape: pl.BlockSpec((gb,) + shape, lambda j, i: (j,) + (0,) * len(shape))
    y, xfin = pl.pallas_call(
        _s5_body,
        grid=(g // gb, b),
        in_specs=[pl.BlockSpec((1, gb, nc, tc), lambda j, i: (i, j, 0, 0)),
                  per_g((1, tc)), per_g((tc, tc)), per_g((tc, LANES)), per_g((tc, LANES)),
                  per_g((LANES, tc)), per_g((LANES, tc)),
                  pl.BlockSpec((2, gb, LANES), lambda j, i: (0, j, 0)),
                  pl.BlockSpec((2, gb, LANES), lambda j, i: (0, j, 0)),
                  pl.BlockSpec((1, 2, gb, LANES), lambda j, i: (i, 0, j, 0))],
        out_specs=[pl.BlockSpec((1, gb, nc, tc), lambda j, i: (i, j, 0, 0)),
                   pl.BlockSpec((1, 2, gb, LANES), lambda j, i: (i, 0, j, 0))],
        out_shape=[jax.ShapeDtypeStruct((b, g, nc, tc), F32), jax.ShapeDtypeStruct((b, 2, g, LANES), F32)],
        scratch_shapes=[pltpu.VMEM((2, nc, gb, LANES), F32)] * 3,
        compiler_params=_params("parallel", "parallel"),
        name="s5_scan",
    )(u, dsk, ops["toep"], ops["m_f"], ops["m_b"], ops["n_f"], ops["n_b"], ops["a1"], ops["a2"], x0)
    y = jnp.transpose(y.reshape(b, g, nc, t, cg), (0, 2, 3, 1, 4)).reshape(b, l, d)
    return y, xfin


def _s5_mixer(x_c, x_l, ng, mod_c, mod_l, lam_re, lam_im, log_dt, b_re, b_im, c_re, c_im, d_skip, w_glu, g1,
              want_ctx):
    b, l, d = x_l.shape
    g, p = lam_re.shape[1], lam_re.shape[2]
    cg = d // g
    assert 2 * p == LANES and g % S5_GROUP_BLOCK == 0
    ops = _s5_operators(lam_re, lam_im, log_dt, b_re, b_im, c_re, c_im, S5_CHUNK)
    dsk = jnp.tile(d_skip.astype(F32).reshape(g, 1, cg), (1, 1, S5_CHUNK))
    h_c = _norm_mod_only(x_c, ng, mod_c[0], mod_c[1])
    h_l = _norm_mod_only(x_l, ng, mod_l[0], mod_l[1])
    y_c, x_ctx = _s5_scan(h_c, ops, dsk, jnp.zeros((b, 2, g, LANES), F32))
    y_l, _ = _s5_scan(h_l, ops, dsk, x_ctx)
    new_l = _out_call(_out_glu_body, [(y_l, 0, d)], [w_glu], x_l, mod_l[2], g1, "s5_glu_out")
    new_c = None
    if want_ctx:
        new_c = _out_call(_out_glu_body, [(y_c, 0, d)], [w_glu], x_c, mod_c[2], g1, "s5_glu_out")
    return new_c, new_l


FFN_HALO = 16
FFN_TN = 256
FFN_TILES_PER_STEP = 11
FFN_ROW_SPLIT = 256


def _ffn_body(xp_ref, x_ref, xn_ref, g2_ref, sh_ref, sc_ref, *rest, per, n_tiles, tps):
    w_refs = rest[:2 * tps]
    cw_ref, cb_ref, wo_ref, gate_ref, g3_ref, o_ref, h_scr, gat_scr = rest[2 * tps:]
    i, j = pl.program_id(0), pl.program_id(1)
    tm = x_ref.shape[0]
    hl = FFN_HALO
    f = wo_ref.shape[0]
    n_steps = pl.cdiv(n_tiles, tps)
    when = (lambda cond: lambda fn: fn()) if n_steps == 1 else pl.when

    @when(j == 0)
    def _():
        g, sh, sc = g2_ref[...], sh_ref[0], sc_ref[0]
        keep_p = ((i % per) != 0).astype(F32)
        keep_n = ((i % per) != per - 1).astype(F32)
        h_scr[0:hl] = (_norm_mod(xp_ref[...], g, sh, sc) * keep_p).astype(BF16)
        h_scr[hl:hl + tm] = _norm_mod(x_ref[...], g, sh, sc).astype(BF16)
        h_scr[hl + tm:] = (_norm_mod(xn_ref[...], g, sh, sc) * keep_n).astype(BF16)

    sub = FFN_ROW_SPLIT if tm % FFN_ROW_SPLIT == 0 else tm
    ns = tm // sub
    edge = SUBLANES
    tn = w_refs[0].shape[1]

    def pieces(w_ref):
        out = []
        for s in range(ns):
            lo = hl + s * sub - (hl if s == 0 else 0)
            hi = hl + (s + 1) * sub + (hl if s == ns - 1 else 0)
            out.append(jnp.dot(h_scr[lo:hi], w_ref[...], preferred_element_type=F32))
        return out

    def conv(p, s, cw, cb):
        base = hl if s == 0 else 0
        left = p[s][hl - edge:hl] if s == 0 else p[s - 1][-edge:]
        right = p[s][base + sub:base + sub + edge] if s == ns - 1 else p[s + 1][:edge]
        ext = jnp.concatenate([left, p[s][base:base + sub], right], axis=0)
        um = pltpu.roll(ext, 1, 0)[edge:edge + sub]
        up = pltpu.roll(ext, sub + 2 * edge - 1, 0)[edge:edge + sub]
        return cb + um * cw[0:1] + ext[edge:edge + sub] * cw[1:2] + up * cw[2:3]

    def tile(k):
        col = k * tn if n_steps == 1 else pl.multiple_of((j * tps + k) * tn, tn)
        cols_a, cols_v = pl.ds(col, tn), pl.ds(pl.multiple_of(f + col, tn), tn)
        pa, pv = pieces(w_refs[2 * k]), pieces(w_refs[2 * k + 1])
        for s in range(ns):
            a = conv(pa, s, cw_ref[:, cols_a], cb_ref[:, cols_a])
            v = conv(pv, s, cw_ref[:, cols_v], cb_ref[:, cols_v])
            gat_scr[s * sub:(s + 1) * sub, cols_a] = (a * jax.nn.sigmoid(a) * v).astype(BF16)

    always = n_tiles - tps * (n_steps - 1)
    for k in range(tps):
        if k < always:
            tile(k)
        else:
            pl.when(j * tps + k < n_tiles)(functools.partial(tile, k))

    @when(j == n_steps - 1)
    def _():
        y = jnp.dot(gat_scr[...], wo_ref[...], preferred_element_type=F32)
        o_ref[...] = x_ref[...] + gate_ref[0] * _rms(y, g3_ref[...])


def _conv_ffn_block(x, ng2, mod, w_in, conv_w, conv_b, w_out, ng3):
    b, l, d = x.shape
    f = w_out.shape[0]
    tn, hl = FFN_TN, FFN_HALO
    assert f % tn == 0 and conv_w.shape[0] == 3 and w_in.shape[1] == 2 * f
    nj = f // tn
    tm = _row_tile(l, 1024)
    per = l // tm
    hb = tm // hl
    last_hb = b * l // hl - 1
    cb = conv_b.reshape(1, 2 * f)
    x2 = x.reshape(b * l, d)
    tps = min(FFN_TILES_PER_STEP, nj)
    once = dict(pipeline_mode=pl.Buffered(1)) if tps == nj else {}
    w_specs = []
    for k in range(tps):
        w_specs.append(pl.BlockSpec((d, tn), lambda i, j, k=k: (0, jnp.minimum(j * tps + k, nj - 1)), **once))
        w_specs.append(pl.BlockSpec((d, tn), lambda i, j, k=k: (0, nj + jnp.minimum(j * tps + k, nj - 1)), **once))
    whole = lambda arr: pl.BlockSpec(arr.shape, lambda i, j: (0,) * arr.ndim)
    out = pl.pallas_call(
        functools.partial(_ffn_body, per=per, n_tiles=nj, tps=tps),
        grid=(b * per, pl.cdiv(nj, tps)),
        in_specs=[pl.BlockSpec((hl, d), lambda i, j: (jnp.maximum(i * hb - 1, 0), 0)),
                  pl.BlockSpec((tm, d), lambda i, j: (i, 0)),
                  pl.BlockSpec((hl, d), lambda i, j: (jnp.minimum((i + 1) * hb, last_hb), 0)),
                  pl.BlockSpec((1, d), lambda i, j: (0, 0)),
                  pl.BlockSpec((1, 1, d), lambda i, j: (i // per, 0, 0)),
                  pl.BlockSpec((1, 1, d), lambda i, j: (i // per, 0, 0)),
                  *w_specs, whole(conv_w), whole(cb),
                  pl.BlockSpec((f, d), lambda i, j: (0, 0), pipeline_mode=pl.Buffered(1)),
                  pl.BlockSpec((1, 1, d), lambda i, j: (i // per, 0, 0)),
                  pl.BlockSpec((1, d), lambda i, j: (0, 0))],
        out_specs=pl.BlockSpec((tm, d), lambda i, j: (i, 0)),
        out_shape=jax.ShapeDtypeStruct((b * l, d), F32),
        scratch_shapes=[pltpu.VMEM((tm + 2 * hl, d), BF16), pltpu.VMEM((tm, f), BF16)],
        compiler_params=_params("parallel", "arbitrary"),
        name="conv_ffn",
    )(x2, x2, x2, ng2.reshape(1, d), mod[3], mod[4], *([w_in] * (2 * tps)), conv_w, cb, w_out, mod[5],
      ng3.reshape(1, d))
    return out.reshape(b, l, d)


def kernel(x, c, ctx, c_ctx, w_mod, b_mod, norm_g, a_w_in, a_lower_logits, a_out_g, a_w_out, b_w_qkv, b_rpb,
           b_w_out, c_lam_re, c_lam_im, c_log_dt, c_b_re, c_b_im, c_c_re, c_c_im, c_d, c_w_glu, f_w_in,
           f_conv_w, f_conv_b, f_w_out):
    bsz, _, d = x.shape
    depth = w_mod.shape[0]
    p = jax.nn.softmax(a_lower_logits.astype(F32), axis=0)
    lower = jnp.cumsum(p, axis=0) - p[0]
    n_rows = -(-(bsz + 1) // SUBLANES) * SUBLANES
    c_rows = jnp.zeros((n_rows, d), F32).at[:bsz].set(c).at[bsz].set(c_ctx)
    mods = _modulation(c_rows, w_mod, b_mod)
    wb = lambda w: w.astype(BF16)
    lat, cx = x, ctx
    for i in range(depth):
        kind, j = i % N_MIXERS, i // N_MIXERS
        last = i == depth - 1
        m = mods[i].reshape(n_rows, N_MOD, d)
        mod_l = [m[:bsz, k][:, None, :] for k in range(N_MOD)]
        mod_c = [jnp.broadcast_to(m[bsz, k][None, None, :], (bsz, 1, d)) for k in range(N_MOD)]
        ng = norm_g[i]
        if kind == 0:
            cx1, lat = _hgrn2_mixer(cx, lat, ng[0], mod_c, mod_l, wb(a_w_in[j]), lower[i], a_out_g[j],
                                    wb(a_w_out[j]), ng[1], not last)
        elif kind == 1:
            cx1, lat = _na_mixer(cx, lat, ng[0], mod_c, mod_l, wb(b_w_qkv[j]), b_rpb[j], wb(b_w_out[j]), ng[1],
                                 not last)
        else:
            cx1, lat = _s5_mixer(cx, lat, ng[0], mod_c, mod_l, c_lam_re[j], c_lam_im[j], c_log_dt[j], c_b_re[j],
                                 c_b_im[j], c_c_re[j], c_c_im[j], c_d[j], wb(c_w_glu[j]), ng[1], not last)
        lat = _conv_ffn_block(lat, ng[2], mod_l, wb(f_w_in[i]), f_conv_w[i], f_conv_b[i], wb(f_w_out[i]), ng[3])
        if not last:
            cx = _conv_ffn_block(cx1, ng[2], mod_c, wb(f_w_in[i]), f_conv_w[i], f_conv_b[i], wb(f_w_out[i]), ng[3])
    return lat
```

```python
import functools

import jax
import jax.numpy as jnp
from jax import lax
from jax.experimental import pallas as pl
from jax.experimental.pallas import tpu as pltpu

F32 = jnp.float32
BF16 = jnp.bfloat16

EPS = 1e-6
F_MIN = 1e-30
N_MOD = 6
N_MIXERS = 3
A_HEADS = 8
GRID_W = 64
S5_CHUNK = 32
NEG_BIG = -1e30
NA_ROW_BLOCK = 4
SUBLANES = 8
LANES = 128
VMEM_LIMIT_BYTES = 56 * 1024 * 1024

NT_DIMS = (((1,), (1,)), ((), ()))
TN_DIMS = (((0,), (0,)), ((), ()))


def _params(*sem):
    return pltpu.CompilerParams(dimension_semantics=sem, vmem_limit_bytes=VMEM_LIMIT_BYTES)


def _row_tile(n, want):
    t = min(n, want)
    assert n % t == 0, (n, t)
    return t


def _rms(y, g):
    return y * lax.rsqrt(jnp.mean(y * y, axis=-1, keepdims=True) + EPS) * g


def _mod_body(c_ref, w_ref, b_ref, o_ref):
    c = c_ref[...]
    s = (c * jax.nn.sigmoid(c)).astype(BF16)
    o_ref[0] = jnp.dot(s, w_ref[0].astype(BF16), preferred_element_type=F32) + b_ref[0]


def _modulation(c_rows, w_mod, b_mod):
    depth, d, n = w_mod.shape
    r = c_rows.shape[0]
    tn = n // 4
    return pl.pallas_call(
        _mod_body,
        grid=(depth, n // tn),
        in_specs=[pl.BlockSpec((r, d), lambda i, j: (0, 0)),
                  pl.BlockSpec((1, d, tn), lambda i, j: (i, 0, j)),
                  pl.BlockSpec((1, 1, tn), lambda i, j: (i, 0, j))],
        out_specs=pl.BlockSpec((1, r, tn), lambda i, j: (i, 0, j)),
        out_shape=jax.ShapeDtypeStruct((depth, r, n), F32),
        compiler_params=_params("parallel", "parallel"),
        name="adaln_mod",
    )(c_rows, w_mod, b_mod.reshape(depth, 1, n))


def _norm_mod(x, g, sh, sc):
    return _rms(x, g) * (1.0 + sc) + sh


def _proj_body(x_ref, g_ref, sh_ref, sc_ref, w_ref, o_ref, h_scr):
    @pl.when(pl.program_id(1) == 0)
    def _():
        h_scr[...] = _norm_mod(x_ref[...], g_ref[...], sh_ref[0], sc_ref[0]).astype(BF16)

    o_ref[...] = jnp.dot(h_scr[...], w_ref[...], preferred_element_type=F32).astype(o_ref.dtype)


def _norm_mod_proj(x, g, sh, sc, w, out_dtype):
    b, l, d = x.shape
    n = w.shape[1]
    tm = _row_tile(l, 1024)
    tn = _row_tile(n, 1024)
    per = l // tm
    out = pl.pallas_call(
        _proj_body,
        grid=(b * per, n // tn),
        in_specs=[pl.BlockSpec((tm, d), lambda i, j: (i, 0)),
                  pl.BlockSpec((1, d), lambda i, j: (0, 0)),
                  pl.BlockSpec((1, 1, d), lambda i, j: (i // per, 0, 0)),
                  pl.BlockSpec((1, 1, d), lambda i, j: (i // per, 0, 0)),
                  pl.BlockSpec((d, tn), lambda i, j: (0, j))],
        out_specs=pl.BlockSpec((tm, tn), lambda i, j: (i, j)),
        out_shape=jax.ShapeDtypeStruct((b * l, n), out_dtype),
        scratch_shapes=[pltpu.VMEM((tm, d), BF16)],
        compiler_params=_params("parallel", "arbitrary"),
        name="norm_mod_proj",
    )(x.reshape(b * l, d), g.reshape(1, d), sh, sc, w)
    return out.reshape(b, l, n)


def _residual(res_ref, gate_ref, g1_ref, y):
    return res_ref[...] + gate_ref[0] * _rms(y, g1_ref[...])


def _out_plain_body(a_ref, w_ref, res_ref, gate_ref, g1_ref, o_ref):
    y = jnp.dot(a_ref[...], w_ref[...], preferred_element_type=F32)
    o_ref[...] = _residual(res_ref, gate_ref, g1_ref, y)


def _out_hgrn_body(of_ref, ob_ref, gp_ref, og_ref, w_ref, res_ref, gate_ref, g1_ref, o_ref):
    o = of_ref[...] + ob_ref[...]
    parts = []
    for h in range(A_HEADS):
        oh = o[:, h * LANES:(h + 1) * LANES]
        parts.append(oh * lax.rsqrt(jnp.mean(oh * oh, axis=-1, keepdims=True) + EPS))
    gp = gp_ref[...]
    a = jnp.concatenate(parts, axis=-1) * og_ref[...] * (gp * jax.nn.sigmoid(gp))
    y = jnp.dot(a.astype(BF16), w_ref[...], preferred_element_type=F32)
    o_ref[...] = _residual(res_ref, gate_ref, g1_ref, y)


def _gelu_tanh(y):
    return 0.5 * y * (1.0 + jnp.tanh(0.7978845608028654 * (y + 0.044715 * (y * y * y))))


def _out_call(body, row_inputs, const_inputs, res, gate, g1, name):
    b, l, d = res.shape
    tm = _row_tile(l, 512)
    per = l // tm
    in_specs, args = [], []
    for arr, blk, width in row_inputs:
        in_specs.append(pl.BlockSpec((tm, width), lambda i, blk=blk: (i, blk)))
        args.append(arr.reshape(b * l, arr.shape[-1]))
    for arr in const_inputs:
        in_specs.append(pl.BlockSpec(arr.shape, lambda i, nd=arr.ndim: (0,) * nd))
        args.append(arr)
    in_specs += [pl.BlockSpec((tm, d), lambda i: (i, 0)),
                 pl.BlockSpec((1, 1, d), lambda i: (i // per, 0, 0)),
                 pl.BlockSpec((1, d), lambda i: (0, 0))]
    args += [res.reshape(b * l, d), gate, g1.reshape(1, d)]
    out = pl.pallas_call(
        body,
        grid=(b * per,),
        in_specs=in_specs,
        out_specs=pl.BlockSpec((tm, d), lambda i: (i, 0)),
        out_shape=jax.ShapeDtypeStruct((b * l, d), F32),
        compiler_params=_params("parallel"),
        name=name,
    )(*args)
    return out.reshape(b, l, d)


HGRN_CHUNK = 32
HGRN_SAFE_MIN = 1e-30
HGRN_Q_HEADROOM = 1e37


def _hgrn_gates(z, lb):
    f = lb + (1.0 - lb) * jax.nn.sigmoid(z)
    return jnp.maximum(f, F_MIN), 1.0 - f


def _hgrn_chunk_prep(q, v, z, lb, reverse):
    c = q.shape[0]
    half = c // 2
    fm, kk = _hgrn_gates(z, lb)
    pos = lax.broadcasted_iota(jnp.int32, (c, LANES), 0)
    if reverse:
        pos = (c - 1) - pos
    second = pos >= half
    hpos = jnp.where(second, pos - half, pos)
    ph = fm
    j = 1
    while j < half:
        ph = ph * jnp.where(hpos >= j, pltpu.roll(ph, (c - j) if reverse else j, 0), 1.0)
        j *= 2
    a_last = ph[half:half + 1] if reverse else ph[half - 1:half]
    r_last = ph[0:1] if reverse else ph[c - 1:c]
    p_mid = a_last
    p_last = a_last * r_last
    pp = jnp.where(second, ph * a_last, ph)
    r = jnp.where(second, ph, ph * (1.0 / a_last))
    kd = kk * (1.0 / r)
    q_max = jnp.max(jnp.abs(q), axis=0, keepdims=True)
    ok = jnp.where(p_mid >= HGRN_SAFE_MIN, r_last, 0.0) >= HGRN_SAFE_MIN
    ok = jnp.where(ok, p_mid * HGRN_Q_HEADROOM, -1.0) >= q_max
    return dict(qr=(q * r).astype(BF16), kd=kd.astype(BF16), qp=(q * pp).astype(BF16),
                kl=(kd * r_last).astype(BF16), vb=v.astype(BF16), dec=p_last, bad=jnp.where(ok, 0.0, 1.0))


def _hgrn_chunk_dots(p, st):
    sc = lax.dot_general(p["qr"], p["kd"], NT_DIMS, preferred_element_type=F32)
    o_st = lax.dot_general(p["qp"], st.astype(BF16), NT_DIMS, preferred_element_type=F32)
    u = lax.dot_general(p["vb"], p["kl"], TN_DIMS, preferred_element_type=F32)
    return sc, o_st, u


def _hgrn_chunk_out(p, sc, o_st, reverse):
    c = sc.shape[0]
    row = lax.broadcasted_iota(jnp.int32, (c, c), 0)
    col = lax.broadcasted_iota(jnp.int32, (c, c), 1)
    sc = jnp.where((col >= row) if reverse else (col <= row), sc, 0.0)
    return jnp.dot(sc.astype(BF16), p["vb"], preferred_element_type=F32) + o_st


def _hgrn_exact_block(q_ref, v_ref, z_ref, lb_ref, o_ref, st_scr, dr, reverse, n_tiles):
    pos = lax.broadcasted_iota(jnp.int32, (SUBLANES, LANES), 0)
    if reverse:
        pos = (SUBLANES - 1) - pos

    def earlier(x, j):
        if j == 0:
            return x
        return pltpu.roll(x, (SUBLANES - j) if reverse else j, 0)

    def later(x, j):
        return pltpu.roll(x, j if reverse else (SUBLANES - j), 0)

    def tile(i, carry):
        ti = (n_tiles - 1 - i) if reverse else i
        r0 = pl.multiple_of(ti * SUBLANES, SUBLANES)
        for h in range(A_HEADS):
            sl = slice(h * LANES, (h + 1) * LANES)
            lb = lb_ref[:, sl]
            z = z_ref[0, pl.ds(r0, SUBLANES), sl]
            q = q_ref[0, pl.ds(r0, SUBLANES), sl]
            v = v_ref[0, pl.ds(r0, SUBLANES), sl]
            fm, kk = _hgrn_gates(z, lb)
            pp = fm
            for j in (1, 2, 4):
                pp = pp * jnp.where(pos >= j, earlier(pp, j), 1.0)
            qq = jnp.where(pos <= SUBLANES - 2, later(fm, 1), 1.0)
            for j in (1, 2, 4):
                qq = qq * jnp.where(pos <= SUBLANES - 1 - j, later(qq, j), 1.0)
            dec = pp[0:1] if reverse else pp[SUBLANES - 1:SUBLANES]
            st = st_scr[dr, h]
            o = lax.dot_general((q * pp).astype(BF16), st.astype(BF16), NT_DIMS,
                                preferred_element_type=F32)
            g = fm
            for d in range(SUBLANES):
                if d == 0:
                    e = q * kk
                else:
                    if d > 1:
                        g = g * earlier(fm, d - 1)
                    e = q * g * jnp.where(pos >= d, earlier(kk, d), 0.0)
                o = o + jnp.sum(e, axis=-1, keepdims=True) * earlier(v, d)
            o_ref[0, pl.ds(r0, SUBLANES), sl] = o
            u = lax.dot_general(v.astype(BF16), (kk * qq).astype(BF16), TN_DIMS,
                                preferred_element_type=F32)
            st_scr[dr, h] = dec * st + u
        return carry

    lax.fori_loop(0, n_tiles, tile, 0)


def _hgrn_scan_body(qf_ref, vf_ref, zf_ref, qb_ref, vb_ref, zb_ref, lb_ref, s0_ref, of_ref, ob_ref, sfin_ref,
                    st_scr, save_scr, *, n_chunks):
    step = pl.program_id(1)

    @pl.when(step == 0)
    def _():
        st_scr[...] = s0_ref[0]

    save_scr[...] = st_scr[...]
    dirs = ((qf_ref, vf_ref, zf_ref, of_ref, False), (qb_ref, vb_ref, zb_ref, ob_ref, True))
    c = HGRN_CHUNK

    def chunk(i, bad):
        work = []
        for dr, (q_ref, v_ref, z_ref, o_ref, reverse) in enumerate(dirs):
            rows = pl.ds(pl.multiple_of(((n_chunks - 1 - i) if reverse else i) * c, c), c)
            for h in range(A_HEADS):
                sl = slice(h * LANES, (h + 1) * LANES)
                p = _hgrn_chunk_prep(q_ref[0, rows, sl], v_ref[0, rows, sl], z_ref[0, rows, sl], lb_ref[:, sl],
                                     reverse)
                bad = jnp.maximum(bad, p["bad"])
                work.append((dr, h, o_ref, rows, sl, reverse, p))
        dots = [_hgrn_chunk_dots(p, st_scr[dr, h]) for dr, h, _, _, _, _, p in work]
        for (dr, h, o_ref, rows, sl, reverse, p), (sc, o_st, u) in zip(work, dots):
            o_ref[0, rows, sl] = _hgrn_chunk_out(p, sc, o_st, reverse)
            st_scr[dr, h] = p["dec"] * st_scr[dr, h] + u
        return bad

    bad = lax.fori_loop(0, n_chunks, chunk, jnp.zeros((1, LANES), F32))

    @pl.when(jnp.max(bad) > 0.0)
    def _():
        st_scr[...] = save_scr[...]
        for dr, (q_ref, v_ref, z_ref, o_ref, reverse) in enumerate(dirs):
            _hgrn_exact_block(q_ref, v_ref, z_ref, lb_ref, o_ref, st_scr, dr, reverse, n_chunks * c // SUBLANES)

    @pl.when(step == pl.num_programs(1) - 1)
    def _():
        sfin_ref[0] = st_scr[...]


def _hgrn_scan(proj, lb, s0):
    b, l, _ = proj.shape
    d = A_HEADS * LANES
    t = _row_tile(l, 256)
    nb = l // t
    fwd = lambda blk: pl.BlockSpec((1, t, d), lambda i, s: (i, s, blk))
    bwd = lambda blk: pl.BlockSpec((1, t, d), lambda i, s: (i, nb - 1 - s, blk))
    st_spec = pl.BlockSpec((1,) + s0.shape[1:], lambda i, s: (i, 0, 0, 0, 0))
    return pl.pallas_call(
        functools.partial(_hgrn_scan_body, n_chunks=t // HGRN_CHUNK),
        grid=(b, nb),
        in_specs=[fwd(0), fwd(1), fwd(3), bwd(0), bwd(1), bwd(4),
                  pl.BlockSpec((1, d), lambda i, s: (0, 0)), st_spec],
        out_specs=[fwd(0), bwd(0), st_spec],
        out_shape=[jax.ShapeDtypeStruct((b, l, d), F32), jax.ShapeDtypeStruct((b, l, d), F32),
                   jax.ShapeDtypeStruct(s0.shape, F32)],
        scratch_shapes=[pltpu.VMEM(s0.shape[1:], F32), pltpu.VMEM(s0.shape[1:], F32)],
        compiler_params=_params("parallel", "arbitrary"),
        name="hgrn_scan",
    )(proj, proj, proj, proj, proj, proj, lb, s0)


def _hgrn2_mixer(x_c, x_l, ng, mod_c, mod_l, w_in, lower, out_g, w_out, g1, want_ctx):
    d = x_l.shape[-1]
    p_c = _norm_mod_proj(x_c, ng, mod_c[0], mod_c[1], w_in, F32)
    p_l = _norm_mod_proj(x_l, ng, mod_l[0], mod_l[1], w_in, F32)
    lb = lower.reshape(1, d)
    zero = jnp.zeros((x_l.shape[0], 2, A_HEADS, LANES, LANES), F32)
    ocf, ocb, s_ctx = _hgrn_scan(p_c, lb, zero)
    olf, olb, _ = _hgrn_scan(p_l, lb, s_ctx)
    o_c, o_l = (ocf, ocb), (olf, olb)
    og = out_g.reshape(1, d)
    new_l = _out_call(_out_hgrn_body, [(o_l[0], 0, d), (o_l[1], 0, d), (p_l, 2, d)], [og, w_out],
                      x_l, mod_l[2], g1, "hgrn_out")
    new_c = None
    if want_ctx:
        new_c = _out_call(_out_hgrn_body, [(o_c[0], 0, d), (o_c[1], 0, d), (p_c, 2, d)], [og, w_out],
                          x_c, mod_c[2], g1, "hgrn_out")
    return new_c, new_l


def _softmax_pv(s_list, v_list):
    mx = functools.reduce(jnp.maximum, [jnp.max(s, axis=-1, keepdims=True) for s in s_list])
    ps = [jnp.exp(s - mx) for s in s_list]
    den = functools.reduce(jnp.add, [jnp.sum(p, axis=-1, keepdims=True) for p in ps])
    acc = functools.reduce(jnp.add, [jnp.dot(p.astype(BF16), v, preferred_element_type=F32)
                                     for p, v in zip(ps, v_list)])
    return acc / den


def _na_lat_body(q_ref, k_ref, v_ref, kc_ref, vc_ref, t2_ref, o_ref, *, rows, kh, scale):
    w = GRID_W
    rb = NA_ROW_BLOCK
    nk = kh + rb
    n_rel = 2 * kh - 1
    first = lax.broadcasted_iota(jnp.int32, (rb * w, LANES), 1) < (LANES // 2)
    kc = kc_ref[0]
    vc = vc_ref[0]

    def bias_index(r, ks, m):
        r0 = jnp.clip(r - kh // 2, 0, rows - kh)
        ka = ks + 2 * m
        rel_a = ka - r + (kh - 1)
        in_a = (ka >= r0) & (ka < r0 + kh)
        in_b = (ka + 1 >= r0) & (ka + 1 < r0 + kh)
        both, only_b, only_a = rel_a, (n_rel - 1) + rel_a + 1, (2 * n_rel - 1) + rel_a
        return jnp.where(in_a, jnp.where(in_b, both, only_a), jnp.where(in_b, only_b, 3 * n_rel - 1))

    def block(i, carry):
        rq = i * rb
        ks = jnp.clip(rq - kh // 2, 0, rows - nk)
        q2 = q_ref[0, pl.ds(pl.multiple_of(rq * w, rb * w), rb * w), :] * scale
        k2 = k_ref[0, pl.ds(pl.multiple_of(ks * w, w), nk * w), :]
        v2 = v_ref[0, pl.ds(pl.multiple_of(ks * w, w), nk * w), :]
        outs = []
        for hh in range(2):
            qm = jnp.where(first if hh == 0 else jnp.logical_not(first), q2, jnp.zeros_like(q2))
            bias = jnp.concatenate(
                [jnp.concatenate([t2_ref[hh, bias_index(rq + a, ks, m)] for m in range(nk // 2)], axis=-1)
                 for a in range(rb)], axis=0)
            s_w = lax.dot_general(qm, k2, NT_DIMS, preferred_element_type=F32) + bias
            s_c = lax.dot_general(qm, kc, NT_DIMS, preferred_element_type=F32)
            outs.append(_softmax_pv([s_w, s_c], [v2, vc]))
        o_ref[0, pl.ds(pl.multiple_of(rq * w, rb * w), rb * w), :] = (
            jnp.where(first, outs[0], outs[1]).astype(o_ref.dtype))
        return carry

    lax.fori_loop(0, rows // rb, block, 0, unroll=2)


def _na_ctx_body(q_ref, k_ref, v_ref, o_ref, *, scale):
    n = q_ref.shape[1]
    first = lax.broadcasted_iota(jnp.int32, (n, LANES), 1) < (LANES // 2)
    q2 = q_ref[0] * scale
    k2 = k_ref[0]
    v2 = v_ref[0]
    outs = []
    for hh in range(2):
        qm = jnp.where(first if hh == 0 else jnp.logical_not(first), q2, jnp.zeros_like(q2))
        s = lax.dot_general(qm, k2, NT_DIMS, preferred_element_type=F32)
        outs.append(_softmax_pv([s], [v2]))
    o_ref[0] = jnp.where(first, outs[0], outs[1]).astype(o_ref.dtype)


def _na_bias_table(rpb, kh, kw):
    w = jnp.arange(GRID_W)[:, None]
    c = jnp.arange(GRID_W)[None, :]
    c0 = jnp.clip(w - kw // 2, 0, GRID_W - kw)
    inside = (c >= c0) & (c < c0 + kw)
    cb = jnp.clip(c - w + kw - 1, 0, 2 * kw - 2)
    t = jnp.where(inside[None, None], rpb.astype(F32)[:, :, cb], NEG_BIG)
    off = jnp.full_like(t, NEG_BIG)
    pair = lambda a, b: jnp.concatenate([a, b], axis=-1)
    return jnp.concatenate([pair(t[:, :-1], t[:, 1:]), pair(off, t), pair(t, off), pair(off, off)[:, :1]], axis=1)


def _na_mixer(x_c, x_l, ng, mod_c, mod_l, w_qkv, rpb, w_out, g1, want_ctx):
    b, l, d = x_l.shape
    lc = x_c.shape[1]
    heads = rpb.shape[0]
    na_rows, na_cols = (rpb.shape[1] + 1) // 2, (rpb.shape[2] + 1) // 2
    dh = d // heads
    assert 2 * dh == LANES and l % GRID_W == 0
    rows = l // GRID_W
    kh = min(na_rows, rows)
    assert kh == na_rows and (kh + NA_ROW_BLOCK) % 2 == 0
    assert rows % NA_ROW_BLOCK == 0 and rows >= kh + NA_ROW_BLOCK
    scale = dh ** -0.5
    nhp = heads // 2
    qkv_c = _norm_mod_proj(x_c, ng, mod_c[0], mod_c[1], w_qkv, BF16)
    qkv_l = _norm_mod_proj(x_l, ng, mod_l[0], mod_l[1], w_qkv, BF16)
    t2 = _na_bias_table(rpb, kh, na_cols)
    o_l = pl.pallas_call(
        functools.partial(_na_lat_body, rows=rows, kh=kh, scale=scale),
        grid=(b, nhp),
        in_specs=[pl.BlockSpec((1, l, LANES), lambda i, p: (i, 0, p)),
                  pl.BlockSpec((1, l, LANES), lambda i, p: (i, 0, nhp + p)),
                  pl.BlockSpec((1, l, LANES), lambda i, p: (i, 0, 2 * nhp + p)),
                  pl.BlockSpec((1, lc, LANES), lambda i, p: (i, 0, nhp + p)),
                  pl.BlockSpec((1, lc, LANES), lambda i, p: (i, 0, 2 * nhp + p)),
                  pl.BlockSpec((2,) + t2.shape[1:], lambda i, p: (p, 0, 0, 0))],
        out_specs=pl.BlockSpec((1, l, LANES), lambda i, p: (i, 0, p)),
        out_shape=jax.ShapeDtypeStruct((b, l, d), BF16),
        compiler_params=_params("parallel", "parallel"),
        name="na_latent",
    )(qkv_l, qkv_l, qkv_l, qkv_c, qkv_c, t2)
    new_l = _out_call(_out_plain_body, [(o_l, 0, d)], [w_out], x_l, mod_l[2], g1, "na_out")
    new_c = None
    if want_ctx:
        o_c = pl.pallas_call(
            functools.partial(_na_ctx_body, scale=scale),
            grid=(b, nhp),
            in_specs=[pl.BlockSpec((1, lc, LANES), lambda i, p: (i, 0, p)),
                      pl.BlockSpec((1, lc, LANES), lambda i, p: (i, 0, nhp + p)),
                      pl.BlockSpec((1, lc, LANES), lambda i, p: (i, 0, 2 * nhp + p))],
            out_specs=pl.BlockSpec((1, lc, LANES), lambda i, p: (i, 0, p)),
            out_shape=jax.ShapeDtypeStruct((b, lc, d), BF16),
            compiler_params=_params("parallel", "parallel"),
            name="na_context",
        )(qkv_c, qkv_c, qkv_c)
        new_c = _out_call(_out_plain_body, [(o_c, 0, d)], [w_out], x_c, mod_c[2], g1, "na_out")
    return new_c, new_l


S5_GROUP_BLOCK = 8


def _s5_kernel_body(c_ref, w_ref, o_ref):
    for i in range(c_ref.shape[0]):
        o_ref[i] = jnp.dot(c_ref[i], w_ref[i], preferred_element_type=F32, precision=lax.Precision.HIGHEST)


def _s5_impulse(cmat, wmat):
    n = cmat.shape[0]
    gb = S5_GROUP_BLOCK
    return pl.pallas_call(
        _s5_kernel_body,
        grid=(n // gb,),
        in_specs=[pl.BlockSpec((gb,) + cmat.shape[1:], lambda i: (i, 0, 0)),
                  pl.BlockSpec((gb,) + wmat.shape[1:], lambda i: (i, 0, 0))],
        out_specs=pl.BlockSpec((gb, cmat.shape[1], wmat.shape[2]), lambda i: (i, 0, 0)),
        out_shape=jax.ShapeDtypeStruct((n, cmat.shape[1], wmat.shape[2]), F32),
        compiler_params=_params("parallel"),
        name="s5_impulse",
    )(cmat, wmat)


def _s5_operators(lam_re, lam_im, log_dt, b_re, b_im, c_re, c_im, t):
    _, g, p = lam_re.shape
    cg = b_re.shape[-1]
    lam = lax.complex(lam_re.astype(F32), lam_im.astype(F32))
    ldt = lam * jnp.exp(log_dt.astype(F32))[..., None]
    a = jnp.exp(ldt)
    bbar = ((a - 1.0) / lam)[..., None] * lax.complex(b_re.astype(F32), b_im.astype(F32))
    cm = lax.complex(c_re.astype(F32), c_im.astype(F32))
    apow = jnp.exp(ldt[..., None] * jnp.arange(t + 1, dtype=F32))
    w = apow[..., :t, None] * bbar[:, :, :, None, :]
    wmat = jnp.concatenate([jnp.real(w), jnp.imag(w)], axis=2).reshape(2 * g, 2 * p, t * cg)
    cmat = jnp.concatenate([jnp.real(cm), -jnp.imag(cm)], axis=-1).reshape(2 * g, cg, 2 * p)
    k = _s5_impulse(cmat, wmat).reshape(2, g, cg, t, cg)
    k = jnp.transpose(k, (0, 1, 3, 4, 2))
    kfull = jnp.concatenate([k[1, :, :0:-1], (k[0, :, :1] + k[1, :, :1]), k[0, :, 1:]], axis=1)
    lagvec = jnp.transpose(kfull, (0, 2, 1, 3)).reshape(g, cg, (2 * t - 1) * cg).astype(BF16)
    toep = jnp.stack([lagvec[:, :, (t - 1 - s) * cg:(2 * t - 1 - s) * cg] for s in range(t)], axis=1)
    toep = toep.reshape(g, t * cg, t * cg)

    def state_in(wd, flip):
        wd = wd[:, :, ::-1] if flip else wd
        m = jnp.transpose(wd, (0, 2, 3, 1)).reshape(g, t * cg, p)
        return jnp.concatenate([jnp.real(m), jnp.imag(m)], axis=-1)

    def state_out(cd, pw):
        n = cd[:, :, :, None] * pw[:, None, :, :]
        n = jnp.transpose(n, (0, 2, 3, 1)).reshape(g, p, t * cg)
        return jnp.concatenate([jnp.real(n), -jnp.imag(n)], axis=1)

    m_f = state_in(w[0], True)
    m_b = state_in(w[1], False)
    n_f = state_out(cm[0], apow[0][..., 1:])
    n_b = state_out(cm[1], apow[1][..., :0:-1])
    at = apow[..., t]
    a1 = jnp.concatenate([jnp.real(at), jnp.real(at)], axis=-1)
    a2 = jnp.concatenate([-jnp.imag(at), jnp.imag(at)], axis=-1)
    bf = lambda m: m.astype(BF16)
    return dict(toep=bf(toep), m_f=bf(m_f), m_b=bf(m_b), n_f=bf(n_f), n_b=bf(n_b), a1=a1, a2=a2)


def _s5_body(u_ref, dsk_ref, toep_ref, mf_ref, mb_ref, nf_ref, nb_ref, a1_ref, a2_ref, x0_ref,
             y_ref, xfin_ref, z_scr, zs_scr, xin_scr):
    gb, nc = u_ref.shape[1], u_ref.shape[2]
    half = LANES // 2
    for gi in range(gb):
        ub = u_ref[0, gi].astype(BF16)
        z_scr[0, :, gi, :] = jnp.dot(ub, mf_ref[gi], preferred_element_type=F32)
        z_scr[1, :, gi, :] = jnp.dot(ub, mb_ref[gi], preferred_element_type=F32)

    swap = lambda t: pltpu.roll(t, half, t.ndim - 1)
    for dr in range(2):
        zs_scr[dr] = swap(z_scr[dr].reshape(nc * gb, LANES)).reshape(nc, gb, LANES)

    coef = [(a1_ref[dr], a2_ref[dr]) for dr in range(2)]

    def scan(j, xs):
        out = []
        for dr in range(2):
            jj = j if dr == 0 else nc - 1 - j
            x, xw = xs[2 * dr], xs[2 * dr + 1]
            a1, a2 = coef[dr]
            xin_scr[dr, jj] = x
            out.append(a1 * x + a2 * xw + z_scr[dr, jj])
            out.append(a1 * xw - a2 * x + zs_scr[dr, jj])
        return tuple(out)

    x0f, x0b = x0_ref[0, 0], x0_ref[0, 1]
    xf, _, xb, _ = lax.fori_loop(0, nc, scan, (x0f, swap(x0f), x0b, swap(x0b)))
    xfin_ref[0, 0] = xf
    xfin_ref[0, 1] = xb
    for gi in range(gb):
        u = u_ref[0, gi]
        y = jnp.dot(u.astype(BF16), toep_ref[gi], preferred_element_type=F32) + u * dsk_ref[gi]
        y = y + jnp.dot(xin_scr[0, :, gi, :].astype(BF16), nf_ref[gi], preferred_element_type=F32)
        y = y + jnp.dot(xin_scr[1, :, gi, :].astype(BF16), nb_ref[gi], preferred_element_type=F32)
        y_ref[0, gi] = y


def _piece_transpose(src, piece_of, cg):
    n = len(src)
    out = [None] * n
    for s in range(n):
        c = src[s % n]
        for p in range(1, n):
            c = jnp.where(piece_of == p, src[(p + s) % n], c)
        r = c if s == 0 else pltpu.roll(c, s * cg, 1)
        for p in range(n):
            out[p] = r if s == 0 else jnp.where(piece_of == (p + s) % n, r, out[p])
    return out


def _to_chunks(h_scr, o_ref, t, cg):
    ncb = h_scr.shape[1] // t
    pieces = LANES // cg
    piece_of = lax.broadcasted_iota(jnp.int32, (ncb, LANES), 1) // cg

    def lane_block(lb, carry):
        for tq in range(t // pieces):
            steps = [h_scr[lb, pl.ds(tq * pieces + tr, ncb, stride=t), :] for tr in range(pieces)]
            for gl, col in enumerate(_piece_transpose(steps, piece_of, cg)):
                o_ref[0, lb * pieces + gl, :, tq * LANES:(tq + 1) * LANES] = col
        return carry

    lax.fori_loop(0, h_scr.shape[0], lane_block, 0)


def _from_chunks(y_ref, y_scr, t, cg):
    ncb = y_ref.shape[2]
    pieces = LANES // cg
    piece_of = lax.broadcasted_iota(jnp.int32, (ncb, LANES), 1) // cg

    def lane_block(lb, carry):
        for tq in range(t // pieces):
            cols = [y_ref[0, lb * pieces + gl, :, tq * LANES:(tq + 1) * LANES] for gl in range(pieces)]
            for tr, step in enumerate(_piece_transpose(cols, piece_of, cg)):
                y_scr[lb, pl.ds(tq * pieces + tr, ncb, stride=t), :] = step
        return carry

    lax.fori_loop(0, y_scr.shape[0], lane_block, 0)


def _norm_chunks_body(x_ref, g_ref, sh_ref, sc_ref, o_ref, h_scr, *, t, cg):
    h = _norm_mod(x_ref[...], g_ref[...], sh_ref[0], sc_ref[0])
    for lb in range(h_scr.shape[0]):
        h_scr[lb] = h[:, lb * LANES:(lb + 1) * LANES]
    _to_chunks(h_scr, o_ref, t, cg)


def _norm_mod_chunks(x, g, sh, sc, groups, t):
    b, l, d = x.shape
    cg = d // groups
    tm = _row_tile(l, 512)
    per = l // tm
    assert tm % t == 0 and (tm // t) % SUBLANES == 0 and LANES % cg == 0 and t % (LANES // cg) == 0
    return pl.pallas_call(
        functools.partial(_norm_chunks_body, t=t, cg=cg),
        grid=(b * per,),
        in_specs=[pl.BlockSpec((tm, d), lambda i: (i, 0)),
                  pl.BlockSpec((1, d), lambda i: (0, 0)),
                  pl.BlockSpec((1, 1, d), lambda i: (i // per, 0, 0)),
                  pl.BlockSpec((1, 1, d), lambda i: (i // per, 0, 0))],
        out_specs=pl.BlockSpec((1, groups, tm // t, t * cg), lambda i: (i // per, 0, i % per, 0)),
        out_shape=jax.ShapeDtypeStruct((b, groups, l // t, t * cg), F32),
        scratch_shapes=[pltpu.VMEM((d // LANES, tm, LANES), F32)],
        compiler_params=_params("parallel"),
        name="norm_mod_chunks",
    )(x.reshape(b * l, d), g.reshape(1, d), sh, sc)


def _glu_chunks_body(y_ref, w_ref, res_ref, gate_ref, g1_ref, o_ref, y_scr, *, t, cg):
    _from_chunks(y_ref, y_scr, t, cg)
    d = o_ref.shape[-1]
    y = jnp.concatenate([y_scr[lb] for lb in range(y_scr.shape[0])], axis=-1)
    a = _gelu_tanh(y).astype(BF16)
    ag = jnp.dot(a, w_ref[...], preferred_element_type=F32)
    o_ref[...] = _residual(res_ref, gate_ref, g1_ref, ag[:, :d] * jax.nn.sigmoid(ag[:, d:]))


def _glu_out_chunks(y, w_glu, res, gate, g1, t):
    b, l, d = res.shape
    groups = y.shape[1]
    cg = d // groups
    tm = _row_tile(l, 512)
    per = l // tm
    out = pl.pallas_call(
        functools.partial(_glu_chunks_body, t=t, cg=cg),
        grid=(b * per,),
        in_specs=[pl.BlockSpec((1, groups, tm // t, t * cg), lambda i: (i // per, 0, i % per, 0)),
                  pl.BlockSpec(w_glu.shape, lambda i: (0, 0)),
                  pl.BlockSpec((tm, d), lambda i: (i, 0)),
                  pl.BlockSpec((1, 1, d), lambda i: (i // per, 0, 0)),
                  pl.BlockSpec((1, d), lambda i: (0, 0))],
        out_specs=pl.BlockSpec((tm, d), lambda i: (i, 0)),
        out_shape=jax.ShapeDtypeStruct((b * l, d), F32),
        scratch_shapes=[pltpu.VMEM((d // LANES, tm, LANES), F32)],
        compiler_params=_params("parallel"),
        name="s5_glu_out",
    )(y, w_glu, res.reshape(b * l, d), gate, g1.reshape(1, d))
    return out.reshape(b, l, d)


def _s5_scan(u, ops, dsk, x0):
    b, g, nc, tc = u.shape
    gb = S5_GROUP_BLOCK
    per_g = lambda shape: pl.BlockSpec((gb,) + shape, lambda j, i: (j,) + (0,) * len(shape))
    y, xfin = pl.pallas_call(
        _s5_body,
        grid=(g // gb, b),
        in_specs=[pl.BlockSpec((1, gb, nc, tc), lambda j, i: (i, j, 0, 0)),
                  per_g((1, tc)), per_g((tc, tc)), per_g((tc, LANES)), per_g((tc, LANES)),
                  per_g((LANES, tc)), per_g((LANES, tc)),
                  pl.BlockSpec((2, gb, LANES), lambda j, i: (0, j, 0)),
                  pl.BlockSpec((2, gb, LANES), lambda j, i: (0, j, 0)),
                  pl.BlockSpec((1, 2, gb, LANES), lambda j, i: (i, 0, j, 0))],
        out_specs=[pl.BlockSpec((1, gb, nc, tc), lambda j, i: (i, j, 0, 0)),
                   pl.BlockSpec((1, 2, gb, LANES), lambda j, i: (i, 0, j, 0))],
        out_shape=[jax.ShapeDtypeStruct((b, g, nc, tc), F32), jax.ShapeDtypeStruct((b, 2, g, LANES), F32)],
        scratch_shapes=[pltpu.VMEM((2, nc, gb, LANES), F32)] * 3,
        compiler_params=_params("parallel", "parallel"),
        name="s5_scan",
    )(u, dsk, ops["toep"], ops["m_f"], ops["m_b"], ops["n_f"], ops["n_b"], ops["a1"], ops["a2"], x0)
    return y, xfin


def _s5_mixer(x_c, x_l, ng, mod_c, mod_l, lam_re, lam_im, log_dt, b_re, b_im, c_re, c_im, d_skip, w_glu, g1,
              want_ctx):
    b, l, d = x_l.shape
    g, p = lam_re.shape[1], lam_re.shape[2]
    cg = d // g
    assert 2 * p == LANES and g % S5_GROUP_BLOCK == 0
    ops = _s5_operators(lam_re, lam_im, log_dt, b_re, b_im, c_re, c_im, S5_CHUNK)
    dsk = jnp.tile(d_skip.astype(F32).reshape(g, 1, cg), (1, 1, S5_CHUNK))
    u_c = _norm_mod_chunks(x_c, ng, mod_c[0], mod_c[1], g, S5_CHUNK)
    u_l = _norm_mod_chunks(x_l, ng, mod_l[0], mod_l[1], g, S5_CHUNK)
    y_c, x_ctx = _s5_scan(u_c, ops, dsk, jnp.zeros((b, 2, g, LANES), F32))
    y_l, _ = _s5_scan(u_l, ops, dsk, x_ctx)
    new_l = _glu_out_chunks(y_l, w_glu, x_l, mod_l[2], g1, S5_CHUNK)
    new_c = _glu_out_chunks(y_c, w_glu, x_c, mod_c[2], g1, S5_CHUNK) if want_ctx else None
    return new_c, new_l


FFN_HALO = 16
FFN_TN = 256
FFN_TILES_PER_STEP = 11
FFN_ROW_SPLIT = 256


def _ffn_body(xp_ref, x_ref, xn_ref, g2_ref, sh_ref, sc_ref, *rest, per, n_tiles, tps):
    w_refs = rest[:2 * tps]
    cw_ref, cb_ref, wo_ref, gate_ref, g3_ref, o_ref, h_scr, gat_scr = rest[2 * tps:]
    i, j = pl.program_id(0), pl.program_id(1)
    tm = x_ref.shape[0]
    hl = FFN_HALO
    f = wo_ref.shape[0]
    n_steps = pl.cdiv(n_tiles, tps)
    when = (lambda cond: lambda fn: fn()) if n_steps == 1 else pl.when

    @when(j == 0)
    def _():
        g, sh, sc = g2_ref[...], sh_ref[0], sc_ref[0]
        keep_p = ((i % per) != 0).astype(F32)
        keep_n = ((i % per) != per - 1).astype(F32)
        h_scr[0:hl] = (_norm_mod(xp_ref[...], g, sh, sc) * keep_p).astype(BF16)
        h_scr[hl:hl + tm] = _norm_mod(x_ref[...], g, sh, sc).astype(BF16)
        h_scr[hl + tm:] = (_norm_mod(xn_ref[...], g, sh, sc) * keep_n).astype(BF16)

    sub = FFN_ROW_SPLIT if tm % FFN_ROW_SPLIT == 0 else tm
    ns = tm // sub
    edge = SUBLANES
    tn = w_refs[0].shape[1]

    def pieces(w_ref):
        out = []
        for s in range(ns):
            lo = hl + s * sub - (hl if s == 0 else 0)
            hi = hl + (s + 1) * sub + (hl if s == ns - 1 else 0)
            out.append(jnp.dot(h_scr[lo:hi], w_ref[...], preferred_element_type=F32))
        return out

    def conv(p, s, cw, cb):
        base = hl if s == 0 else 0
        left = p[s][hl - edge:hl] if s == 0 else p[s - 1][-edge:]
        right = p[s][base + sub:base + sub + edge] if s == ns - 1 else p[s + 1][:edge]
        ext = jnp.concatenate([left, p[s][base:base + sub], right], axis=0)
        um = pltpu.roll(ext, 1, 0)[edge:edge + sub]
        up = pltpu.roll(ext, sub + 2 * edge - 1, 0)[edge:edge + sub]
        return cb + um * cw[0:1] + ext[edge:edge + sub] * cw[1:2] + up * cw[2:3]

    def tile(k):
        col = k * tn if n_steps == 1 else pl.multiple_of((j * tps + k) * tn, tn)
        cols_a, cols_v = pl.ds(col, tn), pl.ds(pl.multiple_of(f + col, tn), tn)
        pa, pv = pieces(w_refs[2 * k]), pieces(w_refs[2 * k + 1])
        for s in range(ns):
            a = conv(pa, s, cw_ref[:, cols_a], cb_ref[:, cols_a])
            v = conv(pv, s, cw_ref[:, cols_v], cb_ref[:, cols_v])
            gat_scr[s * sub:(s + 1) * sub, cols_a] = (a * jax.nn.sigmoid(a) * v).astype(BF16)

    always = n_tiles - tps * (n_steps - 1)
    for k in range(tps):
        if k < always:
            tile(k)
        else:
            pl.when(j * tps + k < n_tiles)(functools.partial(tile, k))

    @when(j == n_steps - 1)
    def _():
        y = jnp.dot(gat_scr[...], wo_ref[...], preferred_element_type=F32)
        o_ref[...] = x_ref[...] + gate_ref[0] * _rms(y, g3_ref[...])


def _conv_ffn_block(x, ng2, mod, w_in, conv_w, conv_b, w_out, ng3):
    b, l, d = x.shape
    f = w_out.shape[0]
    tn, hl = FFN_TN, FFN_HALO
    assert f % tn == 0 and conv_w.shape[0] == 3 and w_in.shape[1] == 2 * f
    nj = f // tn
    tm = _row_tile(l, 1024)
    per = l // tm
    hb = tm // hl
    last_hb = b * l // hl - 1
    cb = conv_b.reshape(1, 2 * f)
    x2 = x.reshape(b * l, d)
    tps = min(FFN_TILES_PER_STEP, nj)
    once = dict(pipeline_mode=pl.Buffered(1)) if tps == nj else {}
    w_specs = []
    for k in range(tps):
        w_specs.append(pl.BlockSpec((d, tn), lambda i, j, k=k: (0, jnp.minimum(j * tps + k, nj - 1)), **once))
        w_specs.append(pl.BlockSpec((d, tn), lambda i, j, k=k: (0, nj + jnp.minimum(j * tps + k, nj - 1)), **once))
    whole = lambda arr: pl.BlockSpec(arr.shape, lambda i, j: (0,) * arr.ndim)
    out = pl.pallas_call(
        functools.partial(_ffn_body, per=per, n_tiles=nj, tps=tps),
        grid=(b * per, pl.cdiv(nj, tps)),
        in_specs=[pl.BlockSpec((hl, d), lambda i, j: (jnp.maximum(i * hb - 1, 0), 0)),
                  pl.BlockSpec((tm, d), lambda i, j: (i, 0)),
                  pl.BlockSpec((hl, d), lambda i, j: (jnp.minimum((i + 1) * hb, last_hb), 0)),
                  pl.BlockSpec((1, d), lambda i, j: (0, 0)),
                  pl.BlockSpec((1, 1, d), lambda i, j: (i // per, 0, 0)),
                  pl.BlockSpec((1, 1, d), lambda i, j: (i // per, 0, 0)),
                  *w_specs, whole(conv_w), whole(cb),
                  pl.BlockSpec((f, d), lambda i, j: (0, 0), pipeline_mode=pl.Buffered(1)),
                  pl.BlockSpec((1, 1, d), lambda i, j: (i // per, 0, 0)),
                  pl.BlockSpec((1, d), lambda i, j: (0, 0))],
        out_specs=pl.BlockSpec((tm, d), lambda i, j: (i, 0)),
        out_shape=jax.ShapeDtypeStruct((b * l, d), F32),
        scratch_shapes=[pltpu.VMEM((tm + 2 * hl, d), BF16), pltpu.VMEM((tm, f), BF16)],
        compiler_params=_params("parallel", "arbitrary"),
        name="conv_ffn",
    )(x2, x2, x2, ng2.reshape(1, d), mod[3], mod[4], *([w_in] * (2 * tps)), conv_w, cb, w_out, mod[5],
      ng3.reshape(1, d))
    return out.reshape(b, l, d)


def kernel(x, c, ctx, c_ctx, w_mod, b_mod, norm_g, a_w_in, a_lower_logits, a_out_g, a_w_out, b_w_qkv, b_rpb,
           b_w_out, c_lam_re, c_lam_im, c_log_dt, c_b_re, c_b_im, c_c_re, c_c_im, c_d, c_w_glu, f_w_in,
           f_conv_w, f_conv_b, f_w_out):
    bsz, _, d = x.shape
    depth = w_mod.shape[0]
    p = jax.nn.softmax(a_lower_logits.astype(F32), axis=0)
    lower = jnp.cumsum(p, axis=0) - p[0]
    n_rows = -(-(bsz + 1) // SUBLANES) * SUBLANES
    c_rows = jnp.zeros((n_rows, d), F32).at[:bsz].set(c).at[bsz].set(c_ctx)
    mods = _modulation(c_rows, w_mod, b_mod)
    wb = lambda w: w.astype(BF16)
    lat, cx = x, ctx
    for i in range(depth):
        kind, j = i % N_MIXERS, i // N_MIXERS
        last = i == depth - 1
        m = mods[i].reshape(n_rows, N_MOD, d)
        mod_l = [m[:bsz, k][:, None, :] for k in range(N_MOD)]
        mod_c = [jnp.broadcast_to(m[bsz, k][None, None, :], (bsz, 1, d)) for k in range(N_MOD)]
        ng = norm_g[i]
        if kind == 0:
            cx1, lat = _hgrn2_mixer(cx, lat, ng[0], mod_c, mod_l, wb(a_w_in[j]), lower[i], a_out_g[j],
                                    wb(a_w_out[j]), ng[1], not last)
        elif kind == 1:
            cx1, lat = _na_mixer(cx, lat, ng[0], mod_c, mod_l, wb(b_w_qkv[j]), b_rpb[j], wb(b_w_out[j]), ng[1],
                                 not last)
        else:
            cx1, lat = _s5_mixer(cx, lat, ng[0], mod_c, mod_l, c_lam_re[j], c_lam_im[j], c_log_dt[j], c_b_re[j],
                                 c_b_im[j], c_c_re[j], c_c_im[j], c_d[j], wb(c_w_glu[j]), ng[1], not last)
        lat = _conv_ffn_block(lat, ng[2], mod_l, wb(f_w_in[i]), f_conv_w[i], f_conv_b[i], wb(f_w_out[i]), ng[3])
        if not last:
            cx = _conv_ffn_block(cx1, ng[2], mod_c, wb(f_w_in[i]), f_conv_w[i], f_conv_b[i], wb(f_w_out[i]), ng[3])
    return lat
```

```python
import functools
from typing import NamedTuple

import jax
import jax.numpy as jnp
from jax import lax
from jax.experimental import pallas as pl
from jax.experimental.pallas import tpu as pltpu

F32 = jnp.float32
BF16 = jnp.bfloat16

EPS = 1e-6
F_MIN = 1e-30
N_MOD = 6
N_MIXERS = 3
A_HEADS = 8
GRID_W = 64
S5_CHUNK = 32
NEG_BIG = -1e30
NA_ROW_BLOCK = 4
SUBLANES = 8
LANES = 128
VMEM_LIMIT_BYTES = 56 * 1024 * 1024

NT_DIMS = (((1,), (1,)), ((), ()))
TN_DIMS = (((0,), (0,)), ((), ()))


def _params(*sem):
    return pltpu.CompilerParams(dimension_semantics=sem, vmem_limit_bytes=VMEM_LIMIT_BYTES)


def _row_tile(n, want):
    t = min(n, want)
    assert n % t == 0, (n, t)
    return t


def _rms(y, g):
    return y * lax.rsqrt(jnp.mean(y * y, axis=-1, keepdims=True) + EPS) * g


def _mod_body(c_ref, w_ref, b_ref, o_ref):
    c = c_ref[...]
    s = (c * jax.nn.sigmoid(c)).astype(BF16)
    o_ref[0] = jnp.dot(s, w_ref[0].astype(BF16), preferred_element_type=F32) + b_ref[0]


def _modulation(c_rows, w_mod, b_mod):
    depth, d, n = w_mod.shape
    r = c_rows.shape[0]
    tn = n // 4
    return pl.pallas_call(
        _mod_body,
        grid=(depth, n // tn),
        in_specs=[pl.BlockSpec((r, d), lambda i, j: (0, 0)),
                  pl.BlockSpec((1, d, tn), lambda i, j: (i, 0, j)),
                  pl.BlockSpec((1, 1, tn), lambda i, j: (i, 0, j))],
        out_specs=pl.BlockSpec((1, r, tn), lambda i, j: (i, 0, j)),
        out_shape=jax.ShapeDtypeStruct((depth, r, n), F32),
        compiler_params=_params("parallel", "parallel"),
        name="adaln_mod",
    )(c_rows, w_mod, b_mod.reshape(depth, 1, n))


def _norm_mod(x, g, sh, sc):
    return _rms(x, g) * (1.0 + sc) + sh


class ModRow(NamedTuple):
    table: jax.Array
    first: int
    step: int


def _mod_spec(m, per):
    return pl.BlockSpec((1, 1, m.table.shape[-1]), lambda *idx: (m.first + m.step * (idx[0] // per), 0, 0))


def _proj_body(x_ref, g_ref, sh_ref, sc_ref, w_ref, o_ref):
    h = _norm_mod(x_ref[...], g_ref[...], sh_ref[0], sc_ref[0]).astype(BF16)
    o_ref[...] = jnp.dot(h, w_ref[...], preferred_element_type=F32).astype(o_ref.dtype)


def _norm_mod_proj(x, g, sh, sc, w, out_dtype):
    b, l, d = x.shape
    n = w.shape[1]
    tm = _row_tile(l, 512)
    per = l // tm
    out = pl.pallas_call(
        _proj_body,
        grid=(b * per,),
        in_specs=[pl.BlockSpec((tm, d), lambda i: (i, 0)),
                  pl.BlockSpec((1, d), lambda i: (0, 0)),
                  _mod_spec(sh, per), _mod_spec(sc, per),
                  pl.BlockSpec((d, n), lambda i: (0, 0), pipeline_mode=pl.Buffered(1))],
        out_specs=pl.BlockSpec((tm, n), lambda i: (i, 0)),
        out_shape=jax.ShapeDtypeStruct((b * l, n), out_dtype),
        compiler_params=_params("parallel"),
        name="norm_mod_proj",
    )(x.reshape(b * l, d), g.reshape(1, d), sh.table, sc.table, w)
    return out.reshape(b, l, n)


def _residual(res_ref, gate_ref, g1_ref, y):
    return res_ref[...] + gate_ref[0] * _rms(y, g1_ref[...])


def _out_plain_body(a_ref, w_ref, res_ref, gate_ref, g1_ref, o_ref):
    y = jnp.dot(a_ref[...], w_ref[...], preferred_element_type=F32)
    o_ref[...] = _residual(res_ref, gate_ref, g1_ref, y)


def _out_hgrn_body(of_ref, ob_ref, gp_ref, og_ref, w_ref, res_ref, gate_ref, g1_ref, o_ref):
    o = of_ref[...] + ob_ref[...]
    parts = []
    for h in range(A_HEADS):
        oh = o[:, h * LANES:(h + 1) * LANES]
        parts.append(oh * lax.rsqrt(jnp.mean(oh * oh, axis=-1, keepdims=True) + EPS))
    gp = gp_ref[...]
    a = jnp.concatenate(parts, axis=-1) * og_ref[...] * (gp * jax.nn.sigmoid(gp))
    y = jnp.dot(a.astype(BF16), w_ref[...], preferred_element_type=F32)
    o_ref[...] = _residual(res_ref, gate_ref, g1_ref, y)


def _gelu_tanh(y):
    return 0.5 * y * (1.0 + jnp.tanh(0.7978845608028654 * (y + 0.044715 * (y * y * y))))


def _out_call(body, row_inputs, const_inputs, res, gate, g1, name):
    b, l, d = res.shape
    tm = _row_tile(l, 512)
    per = l // tm
    in_specs, args = [], []
    for arr, blk, width in row_inputs:
        in_specs.append(pl.BlockSpec((tm, width), lambda i, blk=blk: (i, blk)))
        args.append(arr.reshape(b * l, arr.shape[-1]))
    for arr in const_inputs:
        in_specs.append(pl.BlockSpec(arr.shape, lambda i, nd=arr.ndim: (0,) * nd))
        args.append(arr)
    in_specs += [pl.BlockSpec((tm, d), lambda i: (i, 0)),
                 _mod_spec(gate, per),
                 pl.BlockSpec((1, d), lambda i: (0, 0))]
    args += [res.reshape(b * l, d), gate.table, g1.reshape(1, d)]
    out = pl.pallas_call(
        body,
        grid=(b * per,),
        in_specs=in_specs,
        out_specs=pl.BlockSpec((tm, d), lambda i: (i, 0)),
        out_shape=jax.ShapeDtypeStruct((b * l, d), F32),
        compiler_params=_params("parallel"),
        name=name,
    )(*args)
    return out.reshape(b, l, d)


HGRN_CHUNK = 32
HGRN_SAFE_MIN = 1e-30
HGRN_Q_HEADROOM = 1e37


def _hgrn_gates(z, lb):
    f = lb + (1.0 - lb) * jax.nn.sigmoid(z)
    return jnp.maximum(f, F_MIN), 1.0 - f


def _hgrn_chunk_prep(q, v, z, lb, reverse):
    c = q.shape[0]
    half = c // 2
    fm, kk = _hgrn_gates(z, lb)
    pos = lax.broadcasted_iota(jnp.int32, (c, LANES), 0)
    if reverse:
        pos = (c - 1) - pos
    second = pos >= half
    hpos = jnp.where(second, pos - half, pos)
    ph = fm
    j = 1
    while j < half:
        ph = ph * jnp.where(hpos >= j, pltpu.roll(ph, (c - j) if reverse else j, 0), 1.0)
        j *= 2
    a_last = ph[half:half + 1] if reverse else ph[half - 1:half]
    r_last = ph[0:1] if reverse else ph[c - 1:c]
    p_mid = a_last
    p_last = a_last * r_last
    pp = jnp.where(second, ph * a_last, ph)
    r = jnp.where(second, ph, ph * (1.0 / a_last))
    kd = kk * (1.0 / r)
    q_max = jnp.max(jnp.abs(q), axis=0, keepdims=True)
    ok = jnp.where(p_mid >= HGRN_SAFE_MIN, r_last, 0.0) >= HGRN_SAFE_MIN
    ok = jnp.where(ok, p_mid * HGRN_Q_HEADROOM, -1.0) >= q_max
    return dict(qr=(q * r).astype(BF16), kd=kd.astype(BF16), qp=(q * pp).astype(BF16),
                kl=(kd * r_last).astype(BF16), vb=v.astype(BF16), dec=p_last, bad=jnp.where(ok, 0.0, 1.0))


def _hgrn_chunk_dots(p, st):
    sc = lax.dot_general(p["qr"], p["kd"], NT_DIMS, preferred_element_type=F32)
    o_st = lax.dot_general(p["qp"], st.astype(BF16), NT_DIMS, preferred_element_type=F32)
    u = lax.dot_general(p["vb"], p["kl"], TN_DIMS, preferred_element_type=F32)
    return sc, o_st, u


def _hgrn_chunk_out(p, sc, o_st, reverse):
    c = sc.shape[0]
    row = lax.broadcasted_iota(jnp.int32, (c, c), 0)
    col = lax.broadcasted_iota(jnp.int32, (c, c), 1)
    sc = jnp.where((col >= row) if reverse else (col <= row), sc, 0.0)
    return jnp.dot(sc.astype(BF16), p["vb"], preferred_element_type=F32) + o_st


def _hgrn_exact_block(q_ref, v_ref, z_ref, lb_ref, o_ref, st_scr, dr, reverse, n_tiles):
    pos = lax.broadcasted_iota(jnp.int32, (SUBLANES, LANES), 0)
    if reverse:
        pos = (SUBLANES - 1) - pos

    def earlier(x, j):
        if j == 0:
            return x
        return pltpu.roll(x, (SUBLANES - j) if reverse else j, 0)

    def later(x, j):
        return pltpu.roll(x, j if reverse else (SUBLANES - j), 0)

    def tile(i, carry):
        ti = (n_tiles - 1 - i) if reverse else i
        r0 = pl.multiple_of(ti * SUBLANES, SUBLANES)
        for h in range(A_HEADS):
            sl = slice(h * LANES, (h + 1) * LANES)
            lb = lb_ref[:, sl]
            z = z_ref[0, pl.ds(r0, SUBLANES), sl]
            q = q_ref[0, pl.ds(r0, SUBLANES), sl]
            v = v_ref[0, pl.ds(r0, SUBLANES), sl]
            fm, kk = _hgrn_gates(z, lb)
            pp = fm
            for j in (1, 2, 4):
                pp = pp * jnp.where(pos >= j, earlier(pp, j), 1.0)
            qq = jnp.where(pos <= SUBLANES - 2, later(fm, 1), 1.0)
            for j in (1, 2, 4):
                qq = qq * jnp.where(pos <= SUBLANES - 1 - j, later(qq, j), 1.0)
            dec = pp[0:1] if reverse else pp[SUBLANES - 1:SUBLANES]
            st = st_scr[dr, h]
            o = lax.dot_general((q * pp).astype(BF16), st.astype(BF16), NT_DIMS,
                                preferred_element_type=F32)
            g = fm
            for d in range(SUBLANES):
                if d == 0:
                    e = q * kk
                else:
                    if d > 1:
                        g = g * earlier(fm, d - 1)
                    e = q * g * jnp.where(pos >= d, earlier(kk, d), 0.0)
                o = o + jnp.sum(e, axis=-1, keepdims=True) * earlier(v, d)
            o_ref[0, pl.ds(r0, SUBLANES), sl] = o
            u = lax.dot_general(v.astype(BF16), (kk * qq).astype(BF16), TN_DIMS,
                                preferred_element_type=F32)
            st_scr[dr, h] = dec * st + u
        return carry

    lax.fori_loop(0, n_tiles, tile, 0)


def _hgrn_scan_body(qf_ref, vf_ref, zf_ref, qb_ref, vb_ref, zb_ref, lb_ref, s0_ref, of_ref, ob_ref, sfin_ref,
                    st_scr, save_scr, *, n_chunks):
    step = pl.program_id(1)

    @pl.when(step == 0)
    def _():
        st_scr[...] = s0_ref[0]

    save_scr[...] = st_scr[...]
    dirs = ((qf_ref, vf_ref, zf_ref, of_ref, False), (qb_ref, vb_ref, zb_ref, ob_ref, True))
    c = HGRN_CHUNK

    def chunk(i, bad):
        work = []
        for dr, (q_ref, v_ref, z_ref, o_ref, reverse) in enumerate(dirs):
            rows = pl.ds(pl.multiple_of(((n_chunks - 1 - i) if reverse else i) * c, c), c)
            for h in range(A_HEADS):
                sl = slice(h * LANES, (h + 1) * LANES)
                p = _hgrn_chunk_prep(q_ref[0, rows, sl], v_ref[0, rows, sl], z_ref[0, rows, sl], lb_ref[:, sl],
                                     reverse)
                bad = jnp.maximum(bad, p["bad"])
                work.append((dr, h, o_ref, rows, sl, reverse, p))
        dots = [_hgrn_chunk_dots(p, st_scr[dr, h]) for dr, h, _, _, _, _, p in work]
        for (dr, h, o_ref, rows, sl, reverse, p), (sc, o_st, u) in zip(work, dots):
            o_ref[0, rows, sl] = _hgrn_chunk_out(p, sc, o_st, reverse)
            st_scr[dr, h] = p["dec"] * st_scr[dr, h] + u
        return bad

    bad = lax.fori_loop(0, n_chunks, chunk, jnp.zeros((1, LANES), F32))

    @pl.when(jnp.max(bad) > 0.0)
    def _():
        st_scr[...] = save_scr[...]
        for dr, (q_ref, v_ref, z_ref, o_ref, reverse) in enumerate(dirs):
            _hgrn_exact_block(q_ref, v_ref, z_ref, lb_ref, o_ref, st_scr, dr, reverse, n_chunks * c // SUBLANES)

    @pl.when(step == pl.num_programs(1) - 1)
    def _():
        sfin_ref[0] = st_scr[...]


def _hgrn_scan(proj, lb, s0):
    b, l, _ = proj.shape
    d = A_HEADS * LANES
    t = _row_tile(l, 256)
    nb = l // t
    fwd = lambda blk: pl.BlockSpec((1, t, d), lambda i, s: (i, s, blk))
    bwd = lambda blk: pl.BlockSpec((1, t, d), lambda i, s: (i, nb - 1 - s, blk))
    st_spec = pl.BlockSpec((1,) + s0.shape[1:], lambda i, s: (i, 0, 0, 0, 0))
    return pl.pallas_call(
        functools.partial(_hgrn_scan_body, n_chunks=t // HGRN_CHUNK),
        grid=(b, nb),
        in_specs=[fwd(0), fwd(1), fwd(3), bwd(0), bwd(1), bwd(4),
                  pl.BlockSpec((1, d), lambda i, s: (0, 0)), st_spec],
        out_specs=[fwd(0), bwd(0), st_spec],
        out_shape=[jax.ShapeDtypeStruct((b, l, d), F32), jax.ShapeDtypeStruct((b, l, d), F32),
                   jax.ShapeDtypeStruct(s0.shape, F32)],
        scratch_shapes=[pltpu.VMEM(s0.shape[1:], F32), pltpu.VMEM(s0.shape[1:], F32)],
        compiler_params=_params("parallel", "arbitrary"),
        name="hgrn_scan",
    )(proj, proj, proj, proj, proj, proj, lb, s0)


def _hgrn2_mixer(x_c, x_l, ng, mod_c, mod_l, w_in, lower, out_g, w_out, g1, want_ctx):
    d = x_l.shape[-1]
    p_c = _norm_mod_proj(x_c, ng, mod_c[0], mod_c[1], w_in, F32)
    p_l = _norm_mod_proj(x_l, ng, mod_l[0], mod_l[1], w_in, F32)
    lb = lower.reshape(1, d)
    zero = jnp.zeros((x_l.shape[0], 2, A_HEADS, LANES, LANES), F32)
    ocf, ocb, s_ctx = _hgrn_scan(p_c, lb, zero)
    olf, olb, _ = _hgrn_scan(p_l, lb, s_ctx)
    o_c, o_l = (ocf, ocb), (olf, olb)
    og = out_g.reshape(1, d)
    new_l = _out_call(_out_hgrn_body, [(o_l[0], 0, d), (o_l[1], 0, d), (p_l, 2, d)], [og, w_out],
                      x_l, mod_l[2], g1, "hgrn_out")
    new_c = None
    if want_ctx:
        new_c = _out_call(_out_hgrn_body, [(o_c[0], 0, d), (o_c[1], 0, d), (p_c, 2, d)], [og, w_out],
                          x_c, mod_c[2], g1, "hgrn_out")
    return new_c, new_l


def _softmax_pv(s_list, v_list):
    mx = functools.reduce(jnp.maximum, [jnp.max(s, axis=-1, keepdims=True) for s in s_list])
    ps = [jnp.exp(s - mx) for s in s_list]
    den = functools.reduce(jnp.add, [jnp.sum(p, axis=-1, keepdims=True) for p in ps])
    acc = functools.reduce(jnp.add, [jnp.dot(p.astype(BF16), v, preferred_element_type=F32)
                                     for p, v in zip(ps, v_list)])
    return acc / den


def _na_lat_body(q_ref, k_ref, v_ref, kc_ref, vc_ref, t2_ref, o_ref, *, rows, kh, scale):
    w = GRID_W
    rb = NA_ROW_BLOCK
    nk = kh + rb
    n_rel = 2 * kh - 1
    first = lax.broadcasted_iota(jnp.int32, (rb * w, LANES), 1) < (LANES // 2)
    kc = kc_ref[0]
    vc = vc_ref[0]

    def bias_index(r, ks, m):
        r0 = jnp.clip(r - kh // 2, 0, rows - kh)
        ka = ks + 2 * m
        rel_a = ka - r + (kh - 1)
        in_a = (ka >= r0) & (ka < r0 + kh)
        in_b = (ka + 1 >= r0) & (ka + 1 < r0 + kh)
        both, only_b, only_a = rel_a, (n_rel - 1) + rel_a + 1, (2 * n_rel - 1) + rel_a
        return jnp.where(in_a, jnp.where(in_b, both, only_a), jnp.where(in_b, only_b, 3 * n_rel - 1))

    def block(i, carry):
        rq = i * rb
        ks = jnp.clip(rq - kh // 2, 0, rows - nk)
        q2 = q_ref[0, pl.ds(pl.multiple_of(rq * w, rb * w), rb * w), :] * scale
        k2 = k_ref[0, pl.ds(pl.multiple_of(ks * w, w), nk * w), :]
        v2 = v_ref[0, pl.ds(pl.multiple_of(ks * w, w), nk * w), :]
        outs = []
        for hh in range(2):
            qm = jnp.where(first if hh == 0 else jnp.logical_not(first), q2, jnp.zeros_like(q2))
            bias = jnp.concatenate(
                [jnp.concatenate([t2_ref[hh, bias_index(rq + a, ks, m)] for m in range(nk // 2)], axis=-1)
                 for a in range(rb)], axis=0)
            s_w = lax.dot_general(qm, k2, NT_DIMS, preferred_element_type=F32) + bias
            s_c = lax.dot_general(qm, kc, NT_DIMS, preferred_element_type=F32)
            outs.append(_softmax_pv([s_w, s_c], [v2, vc]))
        o_ref[0, pl.ds(pl.multiple_of(rq * w, rb * w), rb * w), :] = (
            jnp.where(first, outs[0], outs[1]).astype(o_ref.dtype))
        return carry

    lax.fori_loop(0, rows // rb, block, 0, unroll=2)


def _na_ctx_body(q_ref, k_ref, v_ref, o_ref, *, scale):
    n = q_ref.shape[1]
    first = lax.broadcasted_iota(jnp.int32, (n, LANES), 1) < (LANES // 2)
    q2 = q_ref[0] * scale
    k2 = k_ref[0]
    v2 = v_ref[0]
    outs = []
    for hh in range(2):
        qm = jnp.where(first if hh == 0 else jnp.logical_not(first), q2, jnp.zeros_like(q2))
        s = lax.dot_general(qm, k2, NT_DIMS, preferred_element_type=F32)
        outs.append(_softmax_pv([s], [v2]))
    o_ref[0] = jnp.where(first, outs[0], outs[1]).astype(o_ref.dtype)


def _na_bias_table(rpb, kh, kw):
    w = jnp.arange(GRID_W)[:, None]
    c = jnp.arange(GRID_W)[None, :]
    c0 = jnp.clip(w - kw // 2, 0, GRID_W - kw)
    inside = (c >= c0) & (c < c0 + kw)
    cb = jnp.clip(c - w + kw - 1, 0, 2 * kw - 2)
    t = jnp.where(inside[None, None], rpb.astype(F32)[:, :, cb], NEG_BIG)
    off = jnp.full_like(t, NEG_BIG)
    pair = lambda a, b: jnp.concatenate([a, b], axis=-1)
    return jnp.concatenate([pair(t[:, :-1], t[:, 1:]), pair(off, t), pair(t, off), pair(off, off)[:, :1]], axis=1)


def _na_mixer(x_c, x_l, ng, mod_c, mod_l, w_qkv, rpb, w_out, g1, want_ctx):
    b, l, d = x_l.shape
    lc = x_c.shape[1]
    heads = rpb.shape[0]
    na_rows, na_cols = (rpb.shape[1] + 1) // 2, (rpb.shape[2] + 1) // 2
    dh = d // heads
    assert 2 * dh == LANES and l % GRID_W == 0
    rows = l // GRID_W
    kh = min(na_rows, rows)
    assert kh == na_rows and (kh + NA_ROW_BLOCK) % 2 == 0
    assert rows % NA_ROW_BLOCK == 0 and rows >= kh + NA_ROW_BLOCK
    scale = dh ** -0.5
    nhp = heads // 2
    qkv_c = _norm_mod_proj(x_c, ng, mod_c[0], mod_c[1], w_qkv, BF16)
    qkv_l = _norm_mod_proj(x_l, ng, mod_l[0], mod_l[1], w_qkv, BF16)
    t2 = _na_bias_table(rpb, kh, na_cols)
    o_l = pl.pallas_call(
        functools.partial(_na_lat_body, rows=rows, kh=kh, scale=scale),
        grid=(b, nhp),
        in_specs=[pl.BlockSpec((1, l, LANES), lambda i, p: (i, 0, p)),
                  pl.BlockSpec((1, l, LANES), lambda i, p: (i, 0, nhp + p)),
                  pl.BlockSpec((1, l, LANES), lambda i, p: (i, 0, 2 * nhp + p)),
                  pl.BlockSpec((1, lc, LANES), lambda i, p: (i, 0, nhp + p)),
                  pl.BlockSpec((1, lc, LANES), lambda i, p: (i, 0, 2 * nhp + p)),
                  pl.BlockSpec((2,) + t2.shape[1:], lambda i, p: (p, 0, 0, 0))],
        out_specs=pl.BlockSpec((1, l, LANES), lambda i, p: (i, 0, p)),
        out_shape=jax.ShapeDtypeStruct((b, l, d), BF16),
        compiler_params=_params("parallel", "parallel"),
        name="na_latent",
    )(qkv_l, qkv_l, qkv_l, qkv_c, qkv_c, t2)
    new_l = _out_call(_out_plain_body, [(o_l, 0, d)], [w_out], x_l, mod_l[2], g1, "na_out")
    new_c = None
    if want_ctx:
        o_c = pl.pallas_call(
            functools.partial(_na_ctx_body, scale=scale),
            grid=(b, nhp),
            in_specs=[pl.BlockSpec((1, lc, LANES), lambda i, p: (i, 0, p)),
                      pl.BlockSpec((1, lc, LANES), lambda i, p: (i, 0, nhp + p)),
                      pl.BlockSpec((1, lc, LANES), lambda i, p: (i, 0, 2 * nhp + p))],
            out_specs=pl.BlockSpec((1, lc, LANES), lambda i, p: (i, 0, p)),
            out_shape=jax.ShapeDtypeStruct((b, lc, d), BF16),
            compiler_params=_params("parallel", "parallel"),
            name="na_context",
        )(qkv_c, qkv_c, qkv_c)
        new_c = _out_call(_out_plain_body, [(o_c, 0, d)], [w_out], x_c, mod_c[2], g1, "na_out")
    return new_c, new_l


S5_GROUP_BLOCK = 8


def _s5_kernel_body(c_ref, w_ref, o_ref):
    for i in range(c_ref.shape[0]):
        o_ref[i] = jnp.dot(c_ref[i], w_ref[i], preferred_element_type=F32, precision=lax.Precision.HIGHEST)


def _s5_impulse(cmat, wmat):
    n = cmat.shape[0]
    gb = S5_GROUP_BLOCK
    return pl.pallas_call(
        _s5_kernel_body,
        grid=(n // gb,),
        in_specs=[pl.BlockSpec((gb,) + cmat.shape[1:], lambda i: (i, 0, 0)),
                  pl.BlockSpec((gb,) + wmat.shape[1:], lambda i: (i, 0, 0))],
        out_specs=pl.BlockSpec((gb, cmat.shape[1], wmat.shape[2]), lambda i: (i, 0, 0)),
        out_shape=jax.ShapeDtypeStruct((n, cmat.shape[1], wmat.shape[2]), F32),
        compiler_params=_params("parallel"),
        name="s5_impulse",
    )(cmat, wmat)


def _s5_operators(lam_re, lam_im, log_dt, b_re, b_im, c_re, c_im, t):
    _, g, p = lam_re.shape
    cg = b_re.shape[-1]
    lam = lax.complex(lam_re.astype(F32), lam_im.astype(F32))
    ldt = lam * jnp.exp(log_dt.astype(F32))[..., None]
    a = jnp.exp(ldt)
    bbar = ((a - 1.0) / lam)[..., None] * lax.complex(b_re.astype(F32), b_im.astype(F32))
    cm = lax.complex(c_re.astype(F32), c_im.astype(F32))
    apow = jnp.exp(ldt[..., None] * jnp.arange(t + 1, dtype=F32))
    w = apow[..., :t, None] * bbar[:, :, :, None, :]
    wmat = jnp.concatenate([jnp.real(w), jnp.imag(w)], axis=2).reshape(2 * g, 2 * p, t * cg)
    cmat = jnp.concatenate([jnp.real(cm), -jnp.imag(cm)], axis=-1).reshape(2 * g, cg, 2 * p)
    k = _s5_impulse(cmat, wmat).reshape(2, g, cg, t, cg)
    k = jnp.transpose(k, (0, 1, 3, 4, 2))
    kfull = jnp.concatenate([k[1, :, :0:-1], (k[0, :, :1] + k[1, :, :1]), k[0, :, 1:]], axis=1)
    lagvec = jnp.transpose(kfull, (0, 2, 1, 3)).reshape(g, cg, (2 * t - 1) * cg).astype(BF16)
    toep = jnp.stack([lagvec[:, :, (t - 1 - s) * cg:(2 * t - 1 - s) * cg] for s in range(t)], axis=1)
    toep = toep.reshape(g, t * cg, t * cg)

    def state_in(wd, flip):
        wd = wd[:, :, ::-1] if flip else wd
        m = jnp.transpose(wd, (0, 2, 3, 1)).reshape(g, t * cg, p)
        return jnp.concatenate([jnp.real(m), jnp.imag(m)], axis=-1)

    def state_out(cd, pw):
        n = cd[:, :, :, None] * pw[:, None, :, :]
        n = jnp.transpose(n, (0, 2, 3, 1)).reshape(g, p, t * cg)
        return jnp.concatenate([jnp.real(n), -jnp.imag(n)], axis=1)

    m_f = state_in(w[0], True)
    m_b = state_in(w[1], False)
    n_f = state_out(cm[0], apow[0][..., 1:])
    n_b = state_out(cm[1], apow[1][..., :0:-1])
    at = apow[..., t]
    a1 = jnp.concatenate([jnp.real(at), jnp.real(at)], axis=-1)
    a2 = jnp.concatenate([-jnp.imag(at), jnp.imag(at)], axis=-1)
    bf = lambda m: m.astype(BF16)
    return dict(toep=bf(toep), m_f=bf(m_f), m_b=bf(m_b), n_f=bf(n_f), n_b=bf(n_b), a1=a1, a2=a2)


def _s5_body(u_ref, dsk_ref, toep_ref, mf_ref, mb_ref, nf_ref, nb_ref, a1_ref, a2_ref, x0_ref,
             y_ref, xfin_ref, z_scr, zs_scr, xin_scr):
    gb, nc = u_ref.shape[1], u_ref.shape[2]
    half = LANES // 2
    for gi in range(gb):
        ub = u_ref[0, gi].astype(BF16)
        z_scr[0, :, gi, :] = jnp.dot(ub, mf_ref[gi], preferred_element_type=F32)
        z_scr[1, :, gi, :] = jnp.dot(ub, mb_ref[gi], preferred_element_type=F32)

    swap = lambda t: pltpu.roll(t, half, t.ndim - 1)
    for dr in range(2):
        zs_scr[dr] = swap(z_scr[dr].reshape(nc * gb, LANES)).reshape(nc, gb, LANES)

    coef = [(a1_ref[dr], a2_ref[dr]) for dr in range(2)]

    def scan(j, xs):
        out = []
        for dr in range(2):
            jj = j if dr == 0 else nc - 1 - j
            x, xw = xs[2 * dr], xs[2 * dr + 1]
            a1, a2 = coef[dr]
            xin_scr[dr, jj] = x
            out.append(a1 * x + a2 * xw + z_scr[dr, jj])
            out.append(a1 * xw - a2 * x + zs_scr[dr, jj])
        return tuple(out)

    x0f, x0b = x0_ref[0, 0], x0_ref[0, 1]
    xf, _, xb, _ = lax.fori_loop(0, nc, scan, (x0f, swap(x0f), x0b, swap(x0b)))
    xfin_ref[0, 0] = xf
    xfin_ref[0, 1] = xb
    for gi in range(gb):
        u = u_ref[0, gi]
        y = jnp.dot(u.astype(BF16), toep_ref[gi], preferred_element_type=F32) + u * dsk_ref[gi]
        y = y + jnp.dot(xin_scr[0, :, gi, :].astype(BF16), nf_ref[gi], preferred_element_type=F32)
        y = y + jnp.dot(xin_scr[1, :, gi, :].astype(BF16), nb_ref[gi], preferred_element_type=F32)
        y_ref[0, gi] = y


def _piece_transpose(src, piece_of, cg):
    n = len(src)
    out = [None] * n
    for s in range(n):
        c = src[s % n]
        for p in range(1, n):
            c = jnp.where(piece_of == p, src[(p + s) % n], c)
        r = c if s == 0 else pltpu.roll(c, s * cg, 1)
        for p in range(n):
            out[p] = r if s == 0 else jnp.where(piece_of == (p + s) % n, r, out[p])
    return out


def _to_chunks(h_scr, o_ref, t, cg):
    ncb = h_scr.shape[1] // t
    pieces = LANES // cg
    piece_of = lax.broadcasted_iota(jnp.int32, (ncb, LANES), 1) // cg

    def lane_block(lb, carry):
        for tq in range(t // pieces):
            steps = [h_scr[lb, pl.ds(tq * pieces + tr, ncb, stride=t), :] for tr in range(pieces)]
            for gl, col in enumerate(_piece_transpose(steps, piece_of, cg)):
                o_ref[0, lb * pieces + gl, :, tq * LANES:(tq + 1) * LANES] = col
        return carry

    lax.fori_loop(0, h_scr.shape[0], lane_block, 0)


def _from_chunks(y_ref, y_scr, t, cg):
    ncb = y_ref.shape[2]
    pieces = LANES // cg
    piece_of = lax.broadcasted_iota(jnp.int32, (ncb, LANES), 1) // cg

    def lane_block(lb, carry):
        for tq in range(t // pieces):
            cols = [y_ref[0, lb * pieces + gl, :, tq * LANES:(tq + 1) * LANES] for gl in range(pieces)]
            for tr, step in enumerate(_piece_transpose(cols, piece_of, cg)):
                y_scr[lb, pl.ds(tq * pieces + tr, ncb, stride=t), :] = step
        return carry

    lax.fori_loop(0, y_scr.shape[0], lane_block, 0)


def _norm_chunks_body(x_ref, g_ref, sh_ref, sc_ref, o_ref, h_scr, *, t, cg):
    h = _norm_mod(x_ref[...], g_ref[...], sh_ref[0], sc_ref[0])
    for lb in range(h_scr.shape[0]):
        h_scr[lb] = h[:, lb * LANES:(lb + 1) * LANES]
    _to_chunks(h_scr, o_ref, t, cg)


def _norm_mod_chunks(x, g, sh, sc, groups, t):
    b, l, d = x.shape
    cg = d // groups
    tm = _row_tile(l, 512)
    per = l // tm
    assert tm % t == 0 and (tm // t) % SUBLANES == 0 and LANES % cg == 0 and t % (LANES // cg) == 0
    return pl.pallas_call(
        functools.partial(_norm_chunks_body, t=t, cg=cg),
        grid=(b * per,),
        in_specs=[pl.BlockSpec((tm, d), lambda i: (i, 0)),
                  pl.BlockSpec((1, d), lambda i: (0, 0)),
                  _mod_spec(sh, per), _mod_spec(sc, per)],
        out_specs=pl.BlockSpec((1, groups, tm // t, t * cg), lambda i: (i // per, 0, i % per, 0)),
        out_shape=jax.ShapeDtypeStruct((b, groups, l // t, t * cg), F32),
        scratch_shapes=[pltpu.VMEM((d // LANES, tm, LANES), F32)],
        compiler_params=_params("parallel"),
        name="norm_mod_chunks",
    )(x.reshape(b * l, d), g.reshape(1, d), sh.table, sc.table)


def _glu_chunks_body(y_ref, w_ref, res_ref, gate_ref, g1_ref, o_ref, y_scr, *, t, cg):
    _from_chunks(y_ref, y_scr, t, cg)
    d = o_ref.shape[-1]
    y = jnp.concatenate([y_scr[lb] for lb in range(y_scr.shape[0])], axis=-1)
    a = _gelu_tanh(y).astype(BF16)
    ag = jnp.dot(a, w_ref[...], preferred_element_type=F32)
    o_ref[...] = _residual(res_ref, gate_ref, g1_ref, ag[:, :d] * jax.nn.sigmoid(ag[:, d:]))


def _glu_out_chunks(y, w_glu, res, gate, g1, t):
    b, l, d = res.shape
    groups = y.shape[1]
    cg = d // groups
    tm = _row_tile(l, 512)
    per = l // tm
    out = pl.pallas_call(
        functools.partial(_glu_chunks_body, t=t, cg=cg),
        grid=(b * per,),
        in_specs=[pl.BlockSpec((1, groups, tm // t, t * cg), lambda i: (i // per, 0, i % per, 0)),
                  pl.BlockSpec(w_glu.shape, lambda i: (0, 0)),
                  pl.BlockSpec((tm, d), lambda i: (i, 0)),
                  _mod_spec(gate, per),
                  pl.BlockSpec((1, d), lambda i: (0, 0))],
        out_specs=pl.BlockSpec((tm, d), lambda i: (i, 0)),
        out_shape=jax.ShapeDtypeStruct((b * l, d), F32),
        scratch_shapes=[pltpu.VMEM((d // LANES, tm, LANES), F32)],
        compiler_params=_params("parallel"),
        name="s5_glu_out",
    )(y, w_glu, res.reshape(b * l, d), gate.table, g1.reshape(1, d))
    return out.reshape(b, l, d)


def _s5_scan(u, ops, dsk, x0):
    b, g, nc, tc = u.shape
    gb = S5_GROUP_BLOCK
    per_g = lambda shape: pl.BlockSpec((gb,) + shape, lambda j, i: (j,) + (0,) * len(shape))
    y, xfin = pl.pallas_call(
        _s5_body,
        grid=(g // gb, b),
        in_specs=[pl.BlockSpec((1, gb, nc, tc), lambda j, i: (i, j, 0, 0)),
                  per_g((1, tc)), per_g((tc, tc)), per_g((tc, LANES)), per_g((tc, LANES)),
                  per_g((LANES, tc)), per_g((LANES, tc)),
                  pl.BlockSpec((2, gb, LANES), lambda j, i: (0, j, 0)),
                  pl.BlockSpec((2, gb, LANES), lambda j, i: (0, j, 0)),
                  pl.BlockSpec((1, 2, gb, LANES), lambda j, i: (i, 0, j, 0))],
        out_specs=[pl.BlockSpec((1, gb, nc, tc), lambda j, i: (i, j, 0, 0)),
                   pl.BlockSpec((1, 2, gb, LANES), lambda j, i: (i, 0, j, 0))],
        out_shape=[jax.ShapeDtypeStruct((b, g, nc, tc), F32), jax.ShapeDtypeStruct((b, 2, g, LANES), F32)],
        scratch_shapes=[pltpu.VMEM((2, nc, gb, LANES), F32)] * 3,
        compiler_params=_params("parallel", "parallel"),
        name="s5_scan",
    )(u, dsk, ops["toep"], ops["m_f"], ops["m_b"], ops["n_f"], ops["n_b"], ops["a1"], ops["a2"], x0)
    return y, xfin


def _s5_mixer(x_c, x_l, ng, mod_c, mod_l, lam_re, lam_im, log_dt, b_re, b_im, c_re, c_im, d_skip, w_glu, g1,
              want_ctx):
    b, l, d = x_l.shape
    g, p = lam_re.shape[1], lam_re.shape[2]
    cg = d // g
    assert 2 * p == LANES and g % S5_GROUP_BLOCK == 0
    ops = _s5_operators(lam_re, lam_im, log_dt, b_re, b_im, c_re, c_im, S5_CHUNK)
    dsk = jnp.tile(d_skip.astype(F32).reshape(g, 1, cg), (1, 1, S5_CHUNK))
    u_c = _norm_mod_chunks(x_c, ng, mod_c[0], mod_c[1], g, S5_CHUNK)
    u_l = _norm_mod_chunks(x_l, ng, mod_l[0], mod_l[1], g, S5_CHUNK)
    y_c, x_ctx = _s5_scan(u_c, ops, dsk, jnp.zeros((b, 2, g, LANES), F32))
    y_l, _ = _s5_scan(u_l, ops, dsk, x_ctx)
    new_l = _glu_out_chunks(y_l, w_glu, x_l, mod_l[2], g1, S5_CHUNK)
    new_c = _glu_out_chunks(y_c, w_glu, x_c, mod_c[2], g1, S5_CHUNK) if want_ctx else None
    return new_c, new_l


FFN_HALO = 16
FFN_TN = 256
FFN_TILES_PER_STEP = 11
FFN_ROW_SPLIT = 256


def _ffn_body(xp_ref, x_ref, xn_ref, g2_ref, sh_ref, sc_ref, *rest, per, n_tiles, tps):
    w_refs = rest[:2 * tps]
    cw_ref, cb_ref, wo_ref, gate_ref, g3_ref, o_ref, h_scr, gat_scr = rest[2 * tps:]
    i, j = pl.program_id(0), pl.program_id(1)
    tm = x_ref.shape[0]
    hl = FFN_HALO
    f = wo_ref.shape[0]
    n_steps = pl.cdiv(n_tiles, tps)
    when = (lambda cond: lambda fn: fn()) if n_steps == 1 else pl.when

    @when(j == 0)
    def _():
        g, sh, sc = g2_ref[...], sh_ref[0], sc_ref[0]
        keep_p = ((i % per) != 0).astype(F32)
        keep_n = ((i % per) != per - 1).astype(F32)
        h_scr[0:hl] = (_norm_mod(xp_ref[...], g, sh, sc) * keep_p).astype(BF16)
        h_scr[hl:hl + tm] = _norm_mod(x_ref[...], g, sh, sc).astype(BF16)
        h_scr[hl + tm:] = (_norm_mod(xn_ref[...], g, sh, sc) * keep_n).astype(BF16)

    sub = FFN_ROW_SPLIT if tm % FFN_ROW_SPLIT == 0 else tm
    ns = tm // sub
    edge = SUBLANES
    tn = w_refs[0].shape[1]

    def pieces(w_ref):
        out = []
        for s in range(ns):
            lo = hl + s * sub - (hl if s == 0 else 0)
            hi = hl + (s + 1) * sub + (hl if s == ns - 1 else 0)
            out.append(jnp.dot(h_scr[lo:hi], w_ref[...], preferred_element_type=F32))
        return out

    def conv(p, s, cw, cb):
        base = hl if s == 0 else 0
        left = p[s][hl - edge:hl] if s == 0 else p[s - 1][-edge:]
        right = p[s][base + sub:base + sub + edge] if s == ns - 1 else p[s + 1][:edge]
        ext = jnp.concatenate([left, p[s][base:base + sub], right], axis=0)
        um = pltpu.roll(ext, 1, 0)[edge:edge + sub]
        up = pltpu.roll(ext, sub + 2 * edge - 1, 0)[edge:edge + sub]
        return cb + um * cw[0:1] + ext[edge:edge + sub] * cw[1:2] + up * cw[2:3]

    def tile(k):
        col = k * tn if n_steps == 1 else pl.multiple_of((j * tps + k) * tn, tn)
        cols_a, cols_v = pl.ds(col, tn), pl.ds(pl.multiple_of(f + col, tn), tn)
        pa, pv = pieces(w_refs[2 * k]), pieces(w_refs[2 * k + 1])
        for s in range(ns):
            a = conv(pa, s, cw_ref[:, cols_a], cb_ref[:, cols_a])
            v = conv(pv, s, cw_ref[:, cols_v], cb_ref[:, cols_v])
            gat_scr[s * sub:(s + 1) * sub, cols_a] = (a * jax.nn.sigmoid(a) * v).astype(BF16)

    always = n_tiles - tps * (n_steps - 1)
    for k in range(tps):
        if k < always:
            tile(k)
        else:
            pl.when(j * tps + k < n_tiles)(functools.partial(tile, k))

    @when(j == n_steps - 1)
    def _():
        y = jnp.dot(gat_scr[...], wo_ref[...], preferred_element_type=F32)
        o_ref[...] = x_ref[...] + gate_ref[0] * _rms(y, g3_ref[...])


def _conv_ffn_block(x, ng2, mod, w_in, conv_w, conv_b, w_out, ng3):
    b, l, d = x.shape
    f = w_out.shape[0]
    tn, hl = FFN_TN, FFN_HALO
    assert f % tn == 0 and conv_w.shape[0] == 3 and w_in.shape[1] == 2 * f
    nj = f // tn
    tm = _row_tile(l, 1024)
    per = l // tm
    hb = tm // hl
    last_hb = b * l // hl - 1
    cb = conv_b.reshape(1, 2 * f)
    x2 = x.reshape(b * l, d)
    tps = min(FFN_TILES_PER_STEP, nj)
    once = dict(pipeline_mode=pl.Buffered(1)) if tps == nj else {}
    w_specs = []
    for k in range(tps):
        w_specs.append(pl.BlockSpec((d, tn), lambda i, j, k=k: (0, jnp.minimum(j * tps + k, nj - 1)), **once))
        w_specs.append(pl.BlockSpec((d, tn), lambda i, j, k=k: (0, nj + jnp.minimum(j * tps + k, nj - 1)), **once))
    whole = lambda arr: pl.BlockSpec(arr.shape, lambda i, j: (0,) * arr.ndim)
    out = pl.pallas_call(
        functools.partial(_ffn_body, per=per, n_tiles=nj, tps=tps),
        grid=(b * per, pl.cdiv(nj, tps)),
        in_specs=[pl.BlockSpec((hl, d), lambda i, j: (jnp.maximum(i * hb - 1, 0), 0)),
                  pl.BlockSpec((tm, d), lambda i, j: (i, 0)),
                  pl.BlockSpec((hl, d), lambda i, j: (jnp.minimum((i + 1) * hb, last_hb), 0)),
                  pl.BlockSpec((1, d), lambda i, j: (0, 0)),
                  _mod_spec(mod[3], per), _mod_spec(mod[4], per),
                  *w_specs, whole(conv_w), whole(cb),
                  pl.BlockSpec((f, d), lambda i, j: (0, 0), pipeline_mode=pl.Buffered(1)),
                  _mod_spec(mod[5], per),
                  pl.BlockSpec((1, d), lambda i, j: (0, 0))],
        out_specs=pl.BlockSpec((tm, d), lambda i, j: (i, 0)),
        out_shape=jax.ShapeDtypeStruct((b * l, d), F32),
        scratch_shapes=[pltpu.VMEM((tm + 2 * hl, d), BF16), pltpu.VMEM((tm, f), BF16)],
        compiler_params=_params("parallel", "arbitrary"),
        name="conv_ffn",
    )(x2, x2, x2, ng2.reshape(1, d), mod[3].table, mod[4].table, *([w_in] * (2 * tps)), conv_w, cb, w_out,
      mod[5].table, ng3.reshape(1, d))
    return out.reshape(b, l, d)


def kernel(x, c, ctx, c_ctx, w_mod, b_mod, norm_g, a_w_in, a_lower_logits, a_out_g, a_w_out, b_w_qkv, b_rpb,
           b_w_out, c_lam_re, c_lam_im, c_log_dt, c_b_re, c_b_im, c_c_re, c_c_im, c_d, c_w_glu, f_w_in,
           f_conv_w, f_conv_b, f_w_out):
    bsz, _, d = x.shape
    depth = w_mod.shape[0]
    p = jax.nn.softmax(a_lower_logits.astype(F32), axis=0)
    lower = jnp.cumsum(p, axis=0) - p[0]
    n_rows = -(-(bsz + 1) // SUBLANES) * SUBLANES
    c_rows = jnp.zeros((n_rows, d), F32).at[:bsz].set(c).at[bsz].set(c_ctx)
    table = _modulation(c_rows, w_mod, b_mod).reshape(depth * n_rows * N_MOD, 1, d)
    wb = lambda w: w.astype(BF16)
    lat, cx = x, ctx
    for i in range(depth):
        kind, j = i % N_MIXERS, i // N_MIXERS
        last = i == depth - 1
        mod_l = [ModRow(table, i * n_rows * N_MOD + k, N_MOD) for k in range(N_MOD)]
        mod_c = [ModRow(table, (i * n_rows + bsz) * N_MOD + k, 0) for k in range(N_MOD)]
        ng = norm_g[i]
        if kind == 0:
            cx1, lat = _hgrn2_mixer(cx, lat, ng[0], mod_c, mod_l, wb(a_w_in[j]), lower[i], a_out_g[j],
                                    wb(a_w_out[j]), ng[1], not last)
        elif kind == 1:
            cx1, lat = _na_mixer(cx, lat, ng[0], mod_c, mod_l, wb(b_w_qkv[j]), b_rpb[j], wb(b_w_out[j]), ng[1],
                                 not last)
        else:
            cx1, lat = _s5_mixer(cx, lat, ng[0], mod_c, mod_l, c_lam_re[j], c_lam_im[j], c_log_dt[j], c_b_re[j],
                                 c_b_im[j], c_c_re[j], c_c_im[j], c_d[j], wb(c_w_glu[j]), ng[1], not last)
        lat = _conv_ffn_block(lat, ng[2], mod_l, wb(f_w_in[i]), f_conv_w[i], f_conv_b[i], wb(f_w_out[i]), ng[3])
        if not last:
            cx = _conv_ffn_block(cx1, ng[2], mod_c, wb(f_w_in[i]), f_conv_w[i], f_conv_b[i], wb(f_w_out[i]), ng[3])
    return lat
```

```python
import functools
from typing import NamedTuple

import jax
import jax.numpy as jnp
from jax import lax
from jax.experimental import pallas as pl
from jax.experimental.pallas import tpu as pltpu

F32 = jnp.float32
BF16 = jnp.bfloat16

EPS = 1e-6
F_MIN = 1e-30
N_MOD = 6
N_MIXERS = 3
A_HEADS = 8
GRID_W = 64
S5_CHUNK = 32
NEG_BIG = -1e30
NA_ROW_BLOCK = 4
SUBLANES = 8
LANES = 128
VMEM_LIMIT_BYTES = 56 * 1024 * 1024

NT_DIMS = (((1,), (1,)), ((), ()))
TN_DIMS = (((0,), (0,)), ((), ()))


def _params(*sem):
    return pltpu.CompilerParams(dimension_semantics=sem, vmem_limit_bytes=VMEM_LIMIT_BYTES)


def _row_tile(n, want):
    t = min(n, want)
    assert n % t == 0, (n, t)
    return t


def _rms(y, g):
    return y * lax.rsqrt(jnp.mean(y * y, axis=-1, keepdims=True) + EPS) * g


def _mod_body(c_ref, w_ref, b_ref, o_ref):
    c = c_ref[...]
    s = (c * jax.nn.sigmoid(c)).astype(BF16)
    o_ref[0] = jnp.dot(s, w_ref[0].astype(BF16), preferred_element_type=F32) + b_ref[0]


def _modulation(c_rows, w_mod, b_mod):
    depth, d, n = w_mod.shape
    r = c_rows.shape[0]
    tn = n // 4
    return pl.pallas_call(
        _mod_body,
        grid=(depth, n // tn),
        in_specs=[pl.BlockSpec((r, d), lambda i, j: (0, 0)),
                  pl.BlockSpec((1, d, tn), lambda i, j: (i, 0, j)),
                  pl.BlockSpec((1, 1, tn), lambda i, j: (i, 0, j))],
        out_specs=pl.BlockSpec((1, r, tn), lambda i, j: (i, 0, j)),
        out_shape=jax.ShapeDtypeStruct((depth, r, n), F32),
        compiler_params=_params("parallel", "parallel"),
        name="adaln_mod",
    )(c_rows, w_mod, b_mod.reshape(depth, 1, n))


def _norm_mod(x, g, sh, sc):
    return _rms(x, g) * (1.0 + sc) + sh


class ModRow(NamedTuple):
    table: jax.Array
    first: int
    step: int


def _mod_spec(m, per):
    return pl.BlockSpec((1, 1, m.table.shape[-1]), lambda *idx: (m.first + m.step * (idx[0] // per), 0, 0))


def _proj_body(x_ref, g_ref, sh_ref, sc_ref, w_ref, o_ref):
    h = _norm_mod(x_ref[...], g_ref[...], sh_ref[0], sc_ref[0]).astype(BF16)
    o_ref[...] = jnp.dot(h, w_ref[...], preferred_element_type=F32).astype(o_ref.dtype)


def _norm_mod_proj(x, g, sh, sc, w, out_dtype):
    b, l, d = x.shape
    n = w.shape[1]
    tm = _row_tile(l, 512)
    per = l // tm
    out = pl.pallas_call(
        _proj_body,
        grid=(b * per,),
        in_specs=[pl.BlockSpec((tm, d), lambda i: (i, 0)),
                  pl.BlockSpec((1, d), lambda i: (0, 0)),
                  _mod_spec(sh, per), _mod_spec(sc, per),
                  pl.BlockSpec((d, n), lambda i: (0, 0), pipeline_mode=pl.Buffered(1))],
        out_specs=pl.BlockSpec((tm, n), lambda i: (i, 0)),
        out_shape=jax.ShapeDtypeStruct((b * l, n), out_dtype),
        compiler_params=_params("parallel"),
        name="norm_mod_proj",
    )(x.reshape(b * l, d), g.reshape(1, d), sh.table, sc.table, w)
    return out.reshape(b, l, n)


def _residual(res_ref, gate_ref, g1_ref, y):
    return res_ref[...] + gate_ref[0] * _rms(y, g1_ref[...])


def _out_plain_body(a_ref, w_ref, res_ref, gate_ref, g1_ref, o_ref):
    y = jnp.dot(a_ref[...], w_ref[...], preferred_element_type=F32)
    o_ref[...] = _residual(res_ref, gate_ref, g1_ref, y)


def _out_hgrn_body(of_ref, ob_ref, gp_ref, og_ref, w_ref, res_ref, gate_ref, g1_ref, o_ref):
    o = of_ref[...].astype(F32) + ob_ref[...].astype(F32)
    parts = []
    for h in range(A_HEADS):
        oh = o[:, h * LANES:(h + 1) * LANES]
        parts.append(oh * lax.rsqrt(jnp.mean(oh * oh, axis=-1, keepdims=True) + EPS))
    gp = gp_ref[...]
    a = jnp.concatenate(parts, axis=-1) * og_ref[...] * (gp * jax.nn.sigmoid(gp))
    y = jnp.dot(a.astype(BF16), w_ref[...], preferred_element_type=F32)
    o_ref[...] = _residual(res_ref, gate_ref, g1_ref, y)


def _gelu_tanh(y):
    return 0.5 * y * (1.0 + jnp.tanh(0.7978845608028654 * (y + 0.044715 * (y * y * y))))


def _out_call(body, row_inputs, const_inputs, res, gate, g1, name):
    b, l, d = res.shape
    tm = _row_tile(l, 512)
    per = l // tm
    in_specs, args = [], []
    for arr, blk, width in row_inputs:
        in_specs.append(pl.BlockSpec((tm, width), lambda i, blk=blk: (i, blk)))
        args.append(arr.reshape(b * l, arr.shape[-1]))
    for arr in const_inputs:
        in_specs.append(pl.BlockSpec(arr.shape, lambda i, nd=arr.ndim: (0,) * nd))
        args.append(arr)
    in_specs += [pl.BlockSpec((tm, d), lambda i: (i, 0)),
                 _mod_spec(gate, per),
                 pl.BlockSpec((1, d), lambda i: (0, 0))]
    args += [res.reshape(b * l, d), gate.table, g1.reshape(1, d)]
    out = pl.pallas_call(
        body,
        grid=(b * per,),
        in_specs=in_specs,
        out_specs=pl.BlockSpec((tm, d), lambda i: (i, 0)),
        out_shape=jax.ShapeDtypeStruct((b * l, d), F32),
        compiler_params=_params("parallel"),
        name=name,
    )(*args)
    return out.reshape(b, l, d)


HGRN_CHUNK = 32
HGRN_SAFE_MIN = 1e-30
HGRN_Q_HEADROOM = 1e37


def _hgrn_gates(z, lb):
    f = lb + (1.0 - lb) * jax.nn.sigmoid(z)
    return jnp.maximum(f, F_MIN), 1.0 - f


def _hgrn_chunk_prep(q, v, z, lb, reverse):
    c = q.shape[0]
    half = c // 2
    fm, kk = _hgrn_gates(z, lb)
    pos = lax.broadcasted_iota(jnp.int32, (c, LANES), 0)
    if reverse:
        pos = (c - 1) - pos
    second = pos >= half
    hpos = jnp.where(second, pos - half, pos)
    ph = fm
    j = 1
    while j < half:
        ph = ph * jnp.where(hpos >= j, pltpu.roll(ph, (c - j) if reverse else j, 0), 1.0)
        j *= 2
    a_last = ph[half:half + 1] if reverse else ph[half - 1:half]
    r_last = ph[0:1] if reverse else ph[c - 1:c]
    p_mid = a_last
    p_last = a_last * r_last
    pp = jnp.where(second, ph * a_last, ph)
    r = jnp.where(second, ph, ph * (1.0 / a_last))
    kd = kk * (1.0 / r)
    q_max = jnp.max(jnp.abs(q), axis=0, keepdims=True)
    ok = jnp.where(p_mid >= HGRN_SAFE_MIN, r_last, 0.0) >= HGRN_SAFE_MIN
    ok = jnp.where(ok, p_mid * HGRN_Q_HEADROOM, -1.0) >= q_max
    return dict(qr=(q * r).astype(BF16), kd=kd.astype(BF16), qp=(q * pp).astype(BF16),
                kl=(kd * r_last).astype(BF16), vb=v.astype(BF16), dec=p_last, bad=jnp.where(ok, 0.0, 1.0))


def _hgrn_chunk_dots(p, st):
    sc = lax.dot_general(p["qr"], p["kd"], NT_DIMS, preferred_element_type=F32)
    o_st = lax.dot_general(p["qp"], st.astype(BF16), NT_DIMS, preferred_element_type=F32)
    u = lax.dot_general(p["vb"], p["kl"], TN_DIMS, preferred_element_type=F32)
    return sc, o_st, u


def _hgrn_chunk_out(p, sc, o_st, reverse):
    c = sc.shape[0]
    row = lax.broadcasted_iota(jnp.int32, (c, c), 0)
    col = lax.broadcasted_iota(jnp.int32, (c, c), 1)
    sc = jnp.where((col >= row) if reverse else (col <= row), sc, 0.0)
    return jnp.dot(sc.astype(BF16), p["vb"], preferred_element_type=F32) + o_st


def _hgrn_exact_block(q_ref, v_ref, z_ref, lb_ref, o_scr, st_scr, dr, reverse, n_tiles):
    pos = lax.broadcasted_iota(jnp.int32, (SUBLANES, LANES), 0)
    if reverse:
        pos = (SUBLANES - 1) - pos

    def earlier(x, j):
        if j == 0:
            return x
        return pltpu.roll(x, (SUBLANES - j) if reverse else j, 0)

    def later(x, j):
        return pltpu.roll(x, j if reverse else (SUBLANES - j), 0)

    def tile(i, carry):
        ti = (n_tiles - 1 - i) if reverse else i
        r0 = pl.multiple_of(ti * SUBLANES, SUBLANES)
        for h in range(A_HEADS):
            sl = slice(h * LANES, (h + 1) * LANES)
            lb = lb_ref[:, sl]
            z = z_ref[0, pl.ds(r0, SUBLANES), sl]
            q = q_ref[0, pl.ds(r0, SUBLANES), sl]
            v = v_ref[0, pl.ds(r0, SUBLANES), sl]
            fm, kk = _hgrn_gates(z, lb)
            pp = fm
            for j in (1, 2, 4):
                pp = pp * jnp.where(pos >= j, earlier(pp, j), 1.0)
            qq = jnp.where(pos <= SUBLANES - 2, later(fm, 1), 1.0)
            for j in (1, 2, 4):
                qq = qq * jnp.where(pos <= SUBLANES - 1 - j, later(qq, j), 1.0)
            dec = pp[0:1] if reverse else pp[SUBLANES - 1:SUBLANES]
            st = st_scr[dr, h]
            o = lax.dot_general((q * pp).astype(BF16), st.astype(BF16), NT_DIMS,
                                preferred_element_type=F32)
            g = fm
            for d in range(SUBLANES):
                if d == 0:
                    e = q * kk
                else:
                    if d > 1:
                        g = g * earlier(fm, d - 1)
                    e = q * g * jnp.where(pos >= d, earlier(kk, d), 0.0)
                o = o + jnp.sum(e, axis=-1, keepdims=True) * earlier(v, d)
            o_scr[pl.ds(r0, SUBLANES), sl] = o
            u = lax.dot_general(v.astype(BF16), (kk * qq).astype(BF16), TN_DIMS,
                                preferred_element_type=F32)
            st_scr[dr, h] = dec * st + u
        return carry

    lax.fori_loop(0, n_tiles, tile, 0)


def _hgrn_scan_body(qf_ref, vf_ref, zf_ref, qb_ref, vb_ref, zb_ref, lb_ref, s0_ref, of_ref, ob_ref, sfin_ref,
                    st_scr, save_scr, ox_scr, *, n_chunks):
    step = pl.program_id(1)

    @pl.when(step == 0)
    def _():
        st_scr[...] = s0_ref[0]

    save_scr[...] = st_scr[...]
    dirs = ((qf_ref, vf_ref, zf_ref, of_ref, False), (qb_ref, vb_ref, zb_ref, ob_ref, True))
    c = HGRN_CHUNK

    def chunk(i, bad):
        work, dots = [], []
        for dr, (q_ref, v_ref, z_ref, o_ref, reverse) in enumerate(dirs):
            rows = pl.ds(pl.multiple_of(((n_chunks - 1 - i) if reverse else i) * c, c), c)
            group = []
            for h in range(A_HEADS):
                sl = slice(h * LANES, (h + 1) * LANES)
                p = _hgrn_chunk_prep(q_ref[0, rows, sl], v_ref[0, rows, sl], z_ref[0, rows, sl], lb_ref[:, sl],
                                     reverse)
                bad = jnp.maximum(bad, p["bad"])
                group.append((dr, h, o_ref, rows, sl, reverse, p))
            dots += [_hgrn_chunk_dots(p, st_scr[dr, h]) for dr, h, _, _, _, _, p in group]
            work += group
        for (dr, h, _, _, _, _, p), (_, _, u) in zip(work, dots):
            st_scr[dr, h] = p["dec"] * st_scr[dr, h] + u
        for (dr, h, o_ref, rows, sl, reverse, p), (sc, o_st, _) in zip(work, dots):
            o_ref[0, rows, sl] = _hgrn_chunk_out(p, sc, o_st, reverse).astype(o_ref.dtype)
        return bad

    bad = lax.fori_loop(0, n_chunks, chunk, jnp.zeros((1, LANES), F32))

    @pl.when(jnp.max(bad) > 0.0)
    def _():
        st_scr[...] = save_scr[...]
        for dr, (q_ref, v_ref, z_ref, o_ref, reverse) in enumerate(dirs):
            _hgrn_exact_block(q_ref, v_ref, z_ref, lb_ref, ox_scr, st_scr, dr, reverse, n_chunks * c // SUBLANES)
            o_ref[0] = ox_scr[...].astype(o_ref.dtype)

    @pl.when(step == pl.num_programs(1) - 1)
    def _():
        sfin_ref[0] = st_scr[...]


def _hgrn_scan(proj, lb, s0):
    b, l, _ = proj.shape
    d = A_HEADS * LANES
    t = _row_tile(l, 256)
    nb = l // t
    fwd = lambda blk: pl.BlockSpec((1, t, d), lambda i, s: (i, s, blk))
    bwd = lambda blk: pl.BlockSpec((1, t, d), lambda i, s: (i, nb - 1 - s, blk))
    st_spec = pl.BlockSpec((1,) + s0.shape[1:], lambda i, s: (i, 0, 0, 0, 0))
    return pl.pallas_call(
        functools.partial(_hgrn_scan_body, n_chunks=t // HGRN_CHUNK),
        grid=(b, nb),
        in_specs=[fwd(0), fwd(1), fwd(3), bwd(0), bwd(1), bwd(4),
                  pl.BlockSpec((1, d), lambda i, s: (0, 0)), st_spec],
        out_specs=[fwd(0), bwd(0), st_spec],
        out_shape=[jax.ShapeDtypeStruct((b, l, d), BF16), jax.ShapeDtypeStruct((b, l, d), BF16),
                   jax.ShapeDtypeStruct(s0.shape, F32)],
        scratch_shapes=[pltpu.VMEM(s0.shape[1:], F32), pltpu.VMEM(s0.shape[1:], F32), pltpu.VMEM((t, d), F32)],
        compiler_params=_params("parallel", "arbitrary"),
        name="hgrn_scan",
    )(proj, proj, proj, proj, proj, proj, lb, s0)


def _hgrn2_mixer(x_c, x_l, ng, mod_c, mod_l, w_in, lower, out_g, w_out, g1, want_ctx):
    d = x_l.shape[-1]
    p_c = _norm_mod_proj(x_c, ng, mod_c[0], mod_c[1], w_in, F32)
    p_l = _norm_mod_proj(x_l, ng, mod_l[0], mod_l[1], w_in, F32)
    lb = lower.reshape(1, d)
    zero = jnp.zeros((x_l.shape[0], 2, A_HEADS, LANES, LANES), F32)
    ocf, ocb, s_ctx = _hgrn_scan(p_c, lb, zero)
    olf, olb, _ = _hgrn_scan(p_l, lb, s_ctx)
    o_c, o_l = (ocf, ocb), (olf, olb)
    og = out_g.reshape(1, d)
    new_l = _out_call(_out_hgrn_body, [(o_l[0], 0, d), (o_l[1], 0, d), (p_l, 2, d)], [og, w_out],
                      x_l, mod_l[2], g1, "hgrn_out")
    new_c = None
    if want_ctx:
        new_c = _out_call(_out_hgrn_body, [(o_c[0], 0, d), (o_c[1], 0, d), (p_c, 2, d)], [og, w_out],
                          x_c, mod_c[2], g1, "hgrn_out")
    return new_c, new_l


def _softmax_pv(s_list, v_list):
    mx = functools.reduce(jnp.maximum, [jnp.max(s, axis=-1, keepdims=True) for s in s_list])
    ps = [jnp.exp(s - mx) for s in s_list]
    den = functools.reduce(jnp.add, [jnp.sum(p, axis=-1, keepdims=True) for p in ps])
    acc = functools.reduce(jnp.add, [jnp.dot(p.astype(BF16), v, preferred_element_type=F32)
                                     for p, v in zip(ps, v_list)])
    return acc / den


def _na_lat_body(q_ref, k_ref, v_ref, kc_ref, vc_ref, t2_ref, o_ref, *, rows, kh, scale):
    w = GRID_W
    rb = NA_ROW_BLOCK
    nk = kh + rb
    n_rel = 2 * kh - 1
    first = lax.broadcasted_iota(jnp.int32, (rb * w, LANES), 1) < (LANES // 2)
    kc = kc_ref[0]
    vc = vc_ref[0]

    def bias_index(r, ks, m):
        r0 = jnp.clip(r - kh // 2, 0, rows - kh)
        ka = ks + 2 * m
        rel_a = ka - r + (kh - 1)
        in_a = (ka >= r0) & (ka < r0 + kh)
        in_b = (ka + 1 >= r0) & (ka + 1 < r0 + kh)
        both, only_b, only_a = rel_a, (n_rel - 1) + rel_a + 1, (2 * n_rel - 1) + rel_a
        return jnp.where(in_a, jnp.where(in_b, both, only_a), jnp.where(in_b, only_b, 3 * n_rel - 1))

    def block(i, carry):
        rq = i * rb
        ks = jnp.clip(rq - kh // 2, 0, rows - nk)
        q2 = q_ref[0, pl.ds(pl.multiple_of(rq * w, rb * w), rb * w), :] * scale
        k2 = k_ref[0, pl.ds(pl.multiple_of(ks * w, w), nk * w), :]
        v2 = v_ref[0, pl.ds(pl.multiple_of(ks * w, w), nk * w), :]
        outs = []
        for hh in range(2):
            qm = jnp.where(first if hh == 0 else jnp.logical_not(first), q2, jnp.zeros_like(q2))
            bias = jnp.concatenate(
                [jnp.concatenate([t2_ref[hh, bias_index(rq + a, ks, m)] for m in range(nk // 2)], axis=-1)
                 for a in range(rb)], axis=0)
            s_w = lax.dot_general(qm, k2, NT_DIMS, preferred_element_type=F32) + bias
            s_c = lax.dot_general(qm, kc, NT_DIMS, preferred_element_type=F32)
            outs.append(_softmax_pv([s_w, s_c], [v2, vc]))
        o_ref[0, pl.ds(pl.multiple_of(rq * w, rb * w), rb * w), :] = (
            jnp.where(first, outs[0], outs[1]).astype(o_ref.dtype))
        return carry

    lax.fori_loop(0, rows // rb, block, 0, unroll=2)


def _na_ctx_body(q_ref, k_ref, v_ref, o_ref, *, scale):
    n = q_ref.shape[1]
    first = lax.broadcasted_iota(jnp.int32, (n, LANES), 1) < (LANES // 2)
    q2 = q_ref[0] * scale
    k2 = k_ref[0]
    v2 = v_ref[0]
    outs = []
    for hh in range(2):
        qm = jnp.where(first if hh == 0 else jnp.logical_not(first), q2, jnp.zeros_like(q2))
        s = lax.dot_general(qm, k2, NT_DIMS, preferred_element_type=F32)
        outs.append(_softmax_pv([s], [v2]))
    o_ref[0] = jnp.where(first, outs[0], outs[1]).astype(o_ref.dtype)


def _na_bias_table(rpb, kh, kw):
    w = jnp.arange(GRID_W)[:, None]
    c = jnp.arange(GRID_W)[None, :]
    c0 = jnp.clip(w - kw // 2, 0, GRID_W - kw)
    inside = (c >= c0) & (c < c0 + kw)
    pad = GRID_W - kw
    padded = jnp.pad(rpb.astype(F32), ((0, 0), (0, 0), (pad, pad)))
    shifted = jnp.stack([padded[:, :, pad + kw - 1 - q:pad + kw - 1 - q + GRID_W] for q in range(GRID_W)], axis=2)
    t = jnp.where(inside[None, None], shifted, NEG_BIG)
    off = jnp.full_like(t, NEG_BIG)
    pair = lambda a, b: jnp.concatenate([a, b], axis=-1)
    return jnp.concatenate([pair(t[:, :-1], t[:, 1:]), pair(off, t), pair(t, off), pair(off, off)[:, :1]], axis=1)


def _na_mixer(x_c, x_l, ng, mod_c, mod_l, w_qkv, rpb, w_out, g1, want_ctx):
    b, l, d = x_l.shape
    lc = x_c.shape[1]
    heads = rpb.shape[0]
    na_rows, na_cols = (rpb.shape[1] + 1) // 2, (rpb.shape[2] + 1) // 2
    dh = d // heads
    assert 2 * dh == LANES and l % GRID_W == 0
    rows = l // GRID_W
    kh = min(na_rows, rows)
    assert kh == na_rows and (kh + NA_ROW_BLOCK) % 2 == 0
    assert rows % NA_ROW_BLOCK == 0 and rows >= kh + NA_ROW_BLOCK
    scale = dh ** -0.5
    nhp = heads // 2
    qkv_c = _norm_mod_proj(x_c, ng, mod_c[0], mod_c[1], w_qkv, BF16)
    qkv_l = _norm_mod_proj(x_l, ng, mod_l[0], mod_l[1], w_qkv, BF16)
    t2 = _na_bias_table(rpb, kh, na_cols)
    o_l = pl.pallas_call(
        functools.partial(_na_lat_body, rows=rows, kh=kh, scale=scale),
        grid=(b, nhp),
        in_specs=[pl.BlockSpec((1, l, LANES), lambda i, p: (i, 0, p)),
                  pl.BlockSpec((1, l, LANES), lambda i, p: (i, 0, nhp + p)),
                  pl.BlockSpec((1, l, LANES), lambda i, p: (i, 0, 2 * nhp + p)),
                  pl.BlockSpec((1, lc, LANES), lambda i, p: (i, 0, nhp + p)),
                  pl.BlockSpec((1, lc, LANES), lambda i, p: (i, 0, 2 * nhp + p)),
                  pl.BlockSpec((2,) + t2.shape[1:], lambda i, p: (p, 0, 0, 0))],
        out_specs=pl.BlockSpec((1, l, LANES), lambda i, p: (i, 0, p)),
        out_shape=jax.ShapeDtypeStruct((b, l, d), BF16),
        compiler_params=_params("parallel", "parallel"),
        name="na_latent",
    )(qkv_l, qkv_l, qkv_l, qkv_c, qkv_c, t2)
    new_l = _out_call(_out_plain_body, [(o_l, 0, d)], [w_out], x_l, mod_l[2], g1, "na_out")
    new_c = None
    if want_ctx:
        o_c = pl.pallas_call(
            functools.partial(_na_ctx_body, scale=scale),
            grid=(b, nhp),
            in_specs=[pl.BlockSpec((1, lc, LANES), lambda i, p: (i, 0, p)),
                      pl.BlockSpec((1, lc, LANES), lambda i, p: (i, 0, nhp + p)),
                      pl.BlockSpec((1, lc, LANES), lambda i, p: (i, 0, 2 * nhp + p))],
            out_specs=pl.BlockSpec((1, lc, LANES), lambda i, p: (i, 0, p)),
            out_shape=jax.ShapeDtypeStruct((b, lc, d), BF16),
            compiler_params=_params("parallel", "parallel"),
            name="na_context",
        )(qkv_c, qkv_c, qkv_c)
        new_c = _out_call(_out_plain_body, [(o_c, 0, d)], [w_out], x_c, mod_c[2], g1, "na_out")
    return new_c, new_l


S5_GROUP_BLOCK = 8


def _s5_kernel_body(c_ref, w_ref, o_ref):
    for i in range(c_ref.shape[0]):
        o_ref[i] = jnp.dot(c_ref[i], w_ref[i], preferred_element_type=F32, precision=lax.Precision.HIGHEST)


def _s5_impulse(cmat, wmat):
    n = cmat.shape[0]
    gb = S5_GROUP_BLOCK
    return pl.pallas_call(
        _s5_kernel_body,
        grid=(n // gb,),
        in_specs=[pl.BlockSpec((gb,) + cmat.shape[1:], lambda i: (i, 0, 0)),
                  pl.BlockSpec((gb,) + wmat.shape[1:], lambda i: (i, 0, 0))],
        out_specs=pl.BlockSpec((gb, cmat.shape[1], wmat.shape[2]), lambda i: (i, 0, 0)),
        out_shape=jax.ShapeDtypeStruct((n, cmat.shape[1], wmat.shape[2]), F32),
        compiler_params=_params("parallel"),
        name="s5_impulse",
    )(cmat, wmat)


def _s5_operators(lam_re, lam_im, log_dt, b_re, b_im, c_re, c_im, t):
    _, g, p = lam_re.shape
    cg = b_re.shape[-1]
    lam = lax.complex(lam_re.astype(F32), lam_im.astype(F32))
    ldt = lam * jnp.exp(log_dt.astype(F32))[..., None]
    a = jnp.exp(ldt)
    bbar = ((a - 1.0) / lam)[..., None] * lax.complex(b_re.astype(F32), b_im.astype(F32))
    cm = lax.complex(c_re.astype(F32), c_im.astype(F32))
    apow = jnp.exp(ldt[..., None] * jnp.arange(t + 1, dtype=F32))
    w = apow[..., :t, None] * bbar[:, :, :, None, :]
    wmat = jnp.concatenate([jnp.real(w), jnp.imag(w)], axis=2).reshape(2 * g, 2 * p, t * cg)
    cmat = jnp.concatenate([jnp.real(cm), -jnp.imag(cm)], axis=-1).reshape(2 * g, cg, 2 * p)
    k = _s5_impulse(cmat, wmat).reshape(2, g, cg, t, cg)
    k = jnp.transpose(k, (0, 1, 3, 4, 2))
    kfull = jnp.concatenate([k[1, :, :0:-1], (k[0, :, :1] + k[1, :, :1]), k[0, :, 1:]], axis=1)
    lagvec = jnp.transpose(kfull, (0, 2, 1, 3)).reshape(g, cg, (2 * t - 1) * cg).astype(BF16)
    toep = jnp.stack([lagvec[:, :, (t - 1 - s) * cg:(2 * t - 1 - s) * cg] for s in range(t)], axis=1)
    toep = toep.reshape(g, t * cg, t * cg)

    def state_in(wd, flip):
        wd = wd[:, :, ::-1] if flip else wd
        m = jnp.transpose(wd, (0, 2, 3, 1)).reshape(g, t * cg, p)
        return jnp.concatenate([jnp.real(m), jnp.imag(m)], axis=-1)

    def state_out(cd, pw):
        n = cd[:, :, :, None] * pw[:, None, :, :]
        n = jnp.transpose(n, (0, 2, 3, 1)).reshape(g, p, t * cg)
        return jnp.concatenate([jnp.real(n), -jnp.imag(n)], axis=1)

    m_f = state_in(w[0], True)
    m_b = state_in(w[1], False)
    n_f = state_out(cm[0], apow[0][..., 1:])
    n_b = state_out(cm[1], apow[1][..., :0:-1])
    at = apow[..., t]
    a1 = jnp.concatenate([jnp.real(at), jnp.real(at)], axis=-1)
    a2 = jnp.concatenate([-jnp.imag(at), jnp.imag(at)], axis=-1)
    bf = lambda m: m.astype(BF16)
    return dict(toep=bf(toep), m_f=bf(m_f), m_b=bf(m_b), n_f=bf(n_f), n_b=bf(n_b), a1=a1, a2=a2)


def _s5_body(u_ref, dsk_ref, toep_ref, mf_ref, mb_ref, nf_ref, nb_ref, a1_ref, a2_ref, x0_ref,
             y_ref, xfin_ref, z_scr, zs_scr, xin_scr):
    gb, nc = u_ref.shape[1], u_ref.shape[2]
    half = LANES // 2
    for gi in range(gb):
        ub = u_ref[0, gi].astype(BF16)
        z_scr[0, :, gi, :] = jnp.dot(ub, mf_ref[gi], preferred_element_type=F32)
        z_scr[1, :, gi, :] = jnp.dot(ub, mb_ref[gi], preferred_element_type=F32)

    swap = lambda t: pltpu.roll(t, half, t.ndim - 1)
    for dr in range(2):
        zs_scr[dr] = swap(z_scr[dr].reshape(nc * gb, LANES)).reshape(nc, gb, LANES)

    coef = [(a1_ref[dr], a2_ref[dr]) for dr in range(2)]

    def scan(j, xs):
        out = []
        for dr in range(2):
            jj = j if dr == 0 else nc - 1 - j
            x, xw = xs[2 * dr], xs[2 * dr + 1]
            a1, a2 = coef[dr]
            xin_scr[dr, jj] = x
            out.append(a1 * x + a2 * xw + z_scr[dr, jj])
            out.append(a1 * xw - a2 * x + zs_scr[dr, jj])
        return tuple(out)

    x0f, x0b = x0_ref[0, 0], x0_ref[0, 1]
    xf, _, xb, _ = lax.fori_loop(0, nc, scan, (x0f, swap(x0f), x0b, swap(x0b)))
    xfin_ref[0, 0] = xf
    xfin_ref[0, 1] = xb
    for gi in range(gb):
        u = u_ref[0, gi]
        y = jnp.dot(u.astype(BF16), toep_ref[gi], preferred_element_type=F32) + u * dsk_ref[gi]
        y = y + jnp.dot(xin_scr[0, :, gi, :].astype(BF16), nf_ref[gi], preferred_element_type=F32)
        y = y + jnp.dot(xin_scr[1, :, gi, :].astype(BF16), nb_ref[gi], preferred_element_type=F32)
        y_ref[0, gi] = y


def _piece_transpose(src, piece_of, cg):
    n = len(src)
    out = [None] * n
    for s in range(n):
        c = src[s % n]
        for p in range(1, n):
            c = jnp.where(piece_of == p, src[(p + s) % n], c)
        r = c if s == 0 else pltpu.roll(c, s * cg, 1)
        for p in range(n):
            out[p] = r if s == 0 else jnp.where(piece_of == (p + s) % n, r, out[p])
    return out


def _to_chunks(h_scr, o_ref, t, cg):
    ncb = h_scr.shape[1] // t
    pieces = LANES // cg
    piece_of = lax.broadcasted_iota(jnp.int32, (ncb, LANES), 1) // cg

    def lane_block(lb, carry):
        for tq in range(t // pieces):
            steps = [h_scr[lb, pl.ds(tq * pieces + tr, ncb, stride=t), :] for tr in range(pieces)]
            for gl, col in enumerate(_piece_transpose(steps, piece_of, cg)):
                o_ref[0, lb * pieces + gl, :, tq * LANES:(tq + 1) * LANES] = col
        return carry

    lax.fori_loop(0, h_scr.shape[0], lane_block, 0)


def _from_chunks(y_ref, y_scr, t, cg):
    ncb = y_ref.shape[2]
    pieces = LANES // cg
    piece_of = lax.broadcasted_iota(jnp.int32, (ncb, LANES), 1) // cg

    def lane_block(lb, carry):
        for tq in range(t // pieces):
            cols = [y_ref[0, lb * pieces + gl, :, tq * LANES:(tq + 1) * LANES] for gl in range(pieces)]
            for tr, step in enumerate(_piece_transpose(cols, piece_of, cg)):
                y_scr[lb, pl.ds(tq * pieces + tr, ncb, stride=t), :] = step
        return carry

    lax.fori_loop(0, y_scr.shape[0], lane_block, 0)


def _norm_chunks_body(x_ref, g_ref, sh_ref, sc_ref, o_ref, h_scr, *, t, cg):
    h = _norm_mod(x_ref[...], g_ref[...], sh_ref[0], sc_ref[0])
    for lb in range(h_scr.shape[0]):
        h_scr[lb] = h[:, lb * LANES:(lb + 1) * LANES]
    _to_chunks(h_scr, o_ref, t, cg)


def _norm_mod_chunks(x, g, sh, sc, groups, t):
    b, l, d = x.shape
    cg = d // groups
    tm = _row_tile(l, 512)
    per = l // tm
    assert tm % t == 0 and (tm // t) % SUBLANES == 0 and LANES % cg == 0 and t % (LANES // cg) == 0
    return pl.pallas_call(
        functools.partial(_norm_chunks_body, t=t, cg=cg),
        grid=(b * per,),
        in_specs=[pl.BlockSpec((tm, d), lambda i: (i, 0)),
                  pl.BlockSpec((1, d), lambda i: (0, 0)),
                  _mod_spec(sh, per), _mod_spec(sc, per)],
        out_specs=pl.BlockSpec((1, groups, tm // t, t * cg), lambda i: (i // per, 0, i % per, 0)),
        out_shape=jax.ShapeDtypeStruct((b, groups, l // t, t * cg), F32),
        scratch_shapes=[pltpu.VMEM((d // LANES, tm, LANES), F32)],
        compiler_params=_params("parallel"),
        name="norm_mod_chunks",
    )(x.reshape(b * l, d), g.reshape(1, d), sh.table, sc.table)


def _glu_chunks_body(y_ref, w_ref, res_ref, gate_ref, g1_ref, o_ref, y_scr, *, t, cg):
    _from_chunks(y_ref, y_scr, t, cg)
    d = o_ref.shape[-1]
    y = jnp.concatenate([y_scr[lb] for lb in range(y_scr.shape[0])], axis=-1)
    a = _gelu_tanh(y).astype(BF16)
    ag = jnp.dot(a, w_ref[...], preferred_element_type=F32)
    o_ref[...] = _residual(res_ref, gate_ref, g1_ref, ag[:, :d] * jax.nn.sigmoid(ag[:, d:]))


def _glu_out_chunks(y, w_glu, res, gate, g1, t):
    b, l, d = res.shape
    groups = y.shape[1]
    cg = d // groups
    tm = _row_tile(l, 512)
    per = l // tm
    out = pl.pallas_call(
        functools.partial(_glu_chunks_body, t=t, cg=cg),
        grid=(b * per,),
        in_specs=[pl.BlockSpec((1, groups, tm // t, t * cg), lambda i: (i // per, 0, i % per, 0)),
                  pl.BlockSpec(w_glu.shape, lambda i: (0, 0)),
                  pl.BlockSpec((tm, d), lambda i: (i, 0)),
                  _mod_spec(gate, per),
                  pl.BlockSpec((1, d), lambda i: (0, 0))],
        out_specs=pl.BlockSpec((tm, d), lambda i: (i, 0)),
        out_shape=jax.ShapeDtypeStruct((b * l, d), F32),
        scratch_shapes=[pltpu.VMEM((d // LANES, tm, LANES), F32)],
        compiler_params=_params("parallel"),
        name="s5_glu_out",
    )(y, w_glu, res.reshape(b * l, d), gate.table, g1.reshape(1, d))
    return out.reshape(b, l, d)


def _s5_scan(u, ops, dsk, x0):
    b, g, nc, tc = u.shape
    gb = S5_GROUP_BLOCK
    per_g = lambda shape: pl.BlockSpec((gb,) + shape, lambda j, i: (j,) + (0,) * len(shape))
    y, xfin = pl.pallas_call(
        _s5_body,
        grid=(g // gb, b),
        in_specs=[pl.BlockSpec((1, gb, nc, tc), lambda j, i: (i, j, 0, 0)),
                  per_g((1, tc)), per_g((tc, tc)), per_g((tc, LANES)), per_g((tc, LANES)),
                  per_g((LANES, tc)), per_g((LANES, tc)),
                  pl.BlockSpec((2, gb, LANES), lambda j, i: (0, j, 0)),
                  pl.BlockSpec((2, gb, LANES), lambda j, i: (0, j, 0)),
                  pl.BlockSpec((1, 2, gb, LANES), lambda j, i: (i, 0, j, 0))],
        out_specs=[pl.BlockSpec((1, gb, nc, tc), lambda j, i: (i, j, 0, 0)),
                   pl.BlockSpec((1, 2, gb, LANES), lambda j, i: (i, 0, j, 0))],
        out_shape=[jax.ShapeDtypeStruct((b, g, nc, tc), F32), jax.ShapeDtypeStruct((b, 2, g, LANES), F32)],
        scratch_shapes=[pltpu.VMEM((2, nc, gb, LANES), F32)] * 3,
        compiler_params=_params("parallel", "parallel"),
        name="s5_scan",
    )(u, dsk, ops["toep"], ops["m_f"], ops["m_b"], ops["n_f"], ops["n_b"], ops["a1"], ops["a2"], x0)
    return y, xfin


def _s5_mixer(x_c, x_l, ng, mod_c, mod_l, lam_re, lam_im, log_dt, b_re, b_im, c_re, c_im, d_skip, w_glu, g1,
              want_ctx):
    b, l, d = x_l.shape
    g, p = lam_re.shape[1], lam_re.shape[2]
    cg = d // g
    assert 2 * p == LANES and g % S5_GROUP_BLOCK == 0
    ops = _s5_operators(lam_re, lam_im, log_dt, b_re, b_im, c_re, c_im, S5_CHUNK)
    dsk = jnp.tile(d_skip.astype(F32).reshape(g, 1, cg), (1, 1, S5_CHUNK))
    u_c = _norm_mod_chunks(x_c, ng, mod_c[0], mod_c[1], g, S5_CHUNK)
    u_l = _norm_mod_chunks(x_l, ng, mod_l[0], mod_l[1], g, S5_CHUNK)
    y_c, x_ctx = _s5_scan(u_c, ops, dsk, jnp.zeros((b, 2, g, LANES), F32))
    y_l, _ = _s5_scan(u_l, ops, dsk, x_ctx)
    new_l = _glu_out_chunks(y_l, w_glu, x_l, mod_l[2], g1, S5_CHUNK)
    new_c = _glu_out_chunks(y_c, w_glu, x_c, mod_c[2], g1, S5_CHUNK) if want_ctx else None
    return new_c, new_l


FFN_HALO = 16
FFN_TN = 256
FFN_TILES_PER_STEP = 11
FFN_ROW_SPLIT = 256


def _ffn_body(xp_ref, x_ref, xn_ref, g2_ref, sh_ref, sc_ref, *rest, per, n_tiles, tps):
    w_refs = rest[:2 * tps]
    cw_ref, cb_ref, wo_ref, gate_ref, g3_ref, o_ref, h_scr, gat_scr = rest[2 * tps:]
    i, j = pl.program_id(0), pl.program_id(1)
    tm = x_ref.shape[0]
    hl = FFN_HALO
    f = wo_ref.shape[0]
    n_steps = pl.cdiv(n_tiles, tps)
    when = (lambda cond: lambda fn: fn()) if n_steps == 1 else pl.when

    @when(j == 0)
    def _():
        g, sh, sc = g2_ref[...], sh_ref[0], sc_ref[0]
        keep_p = ((i % per) != 0).astype(F32)
        keep_n = ((i % per) != per - 1).astype(F32)
        h_scr[0:hl] = (_norm_mod(xp_ref[...], g, sh, sc) * keep_p).astype(BF16)
        h_scr[hl:hl + tm] = _norm_mod(x_ref[...], g, sh, sc).astype(BF16)
        h_scr[hl + tm:] = (_norm_mod(xn_ref[...], g, sh, sc) * keep_n).astype(BF16)

    sub = FFN_ROW_SPLIT if tm % FFN_ROW_SPLIT == 0 else tm
    ns = tm // sub
    edge = SUBLANES
    tn = w_refs[0].shape[1]

    def pieces(w_ref):
        out = []
        for s in range(ns):
            lo = hl + s * sub - (hl if s == 0 else 0)
            hi = hl + (s + 1) * sub + (hl if s == ns - 1 else 0)
            out.append(jnp.dot(h_scr[lo:hi], w_ref[...], preferred_element_type=F32))
        return out

    def conv(p, s, cw, cb):
        base = hl if s == 0 else 0
        left = p[s][hl - edge:hl] if s == 0 else p[s - 1][-edge:]
        right = p[s][base + sub:base + sub + edge] if s == ns - 1 else p[s + 1][:edge]
        ext = jnp.concatenate([left, p[s][base:base + sub], right], axis=0)
        um = pltpu.roll(ext, 1, 0)[edge:edge + sub]
        up = pltpu.roll(ext, sub + 2 * edge - 1, 0)[edge:edge + sub]
        return cb + um * cw[0:1] + ext[edge:edge + sub] * cw[1:2] + up * cw[2:3]

    def tile(k):
        col = k * tn if n_steps == 1 else pl.multiple_of((j * tps + k) * tn, tn)
        cols_a, cols_v = pl.ds(col, tn), pl.ds(pl.multiple_of(f + col, tn), tn)
        pa, pv = pieces(w_refs[2 * k]), pieces(w_refs[2 * k + 1])
        for s in range(ns):
            a = conv(pa, s, cw_ref[:, cols_a], cb_ref[:, cols_a])
            v = conv(pv, s, cw_ref[:, cols_v], cb_ref[:, cols_v])
            gat_scr[s * sub:(s + 1) * sub, cols_a] = (a * jax.nn.sigmoid(a) * v).astype(BF16)

    always = n_tiles - tps * (n_steps - 1)
    for k in range(tps):
        if k < always:
            tile(k)
        else:
            pl.when(j * tps + k < n_tiles)(functools.partial(tile, k))

    @when(j == n_steps - 1)
    def _():
        y = jnp.dot(gat_scr[...], wo_ref[...], preferred_element_type=F32)
        o_ref[...] = x_ref[...] + gate_ref[0] * _rms(y, g3_ref[...])


def _conv_ffn_block(x, ng2, mod, w_in, conv_w, conv_b, w_out, ng3):
    b, l, d = x.shape
    f = w_out.shape[0]
    tn, hl = FFN_TN, FFN_HALO
    assert f % tn == 0 and conv_w.shape[0] == 3 and w_in.shape[1] == 2 * f
    nj = f // tn
    tm = _row_tile(l, 1024)
    per = l // tm
    hb = tm // hl
    last_hb = b * l // hl - 1
    cb = conv_b.reshape(1, 2 * f)
    x2 = x.reshape(b * l, d)
    tps = min(FFN_TILES_PER_STEP, nj)
    once = dict(pipeline_mode=pl.Buffered(1)) if tps == nj else {}
    w_specs = []
    for k in range(tps):
        w_specs.append(pl.BlockSpec((d, tn), lambda i, j, k=k: (0, jnp.minimum(j * tps + k, nj - 1)), **once))
        w_specs.append(pl.BlockSpec((d, tn), lambda i, j, k=k: (0, nj + jnp.minimum(j * tps + k, nj - 1)), **once))
    whole = lambda arr: pl.BlockSpec(arr.shape, lambda i, j: (0,) * arr.ndim)
    out = pl.pallas_call(
        functools.partial(_ffn_body, per=per, n_tiles=nj, tps=tps),
        grid=(b * per, pl.cdiv(nj, tps)),
        in_specs=[pl.BlockSpec((hl, d), lambda i, j: (jnp.maximum(i * hb - 1, 0), 0)),
                  pl.BlockSpec((tm, d), lambda i, j: (i, 0)),
                  pl.BlockSpec((hl, d), lambda i, j: (jnp.minimum((i + 1) * hb, last_hb), 0)),
                  pl.BlockSpec((1, d), lambda i, j: (0, 0)),
                  _mod_spec(mod[3], per), _mod_spec(mod[4], per),
                  *w_specs, whole(conv_w), whole(cb),
                  pl.BlockSpec((f, d), lambda i, j: (0, 0), pipeline_mode=pl.Buffered(1)),
                  _mod_spec(mod[5], per),
                  pl.BlockSpec((1, d), lambda i, j: (0, 0))],
        out_specs=pl.BlockSpec((tm, d), lambda i, j: (i, 0)),
        out_shape=jax.ShapeDtypeStruct((b * l, d), F32),
        scratch_shapes=[pltpu.VMEM((tm + 2 * hl, d), BF16), pltpu.VMEM((tm, f), BF16)],
        compiler_params=_params("parallel", "arbitrary"),
        name="conv_ffn",
    )(x2, x2, x2, ng2.reshape(1, d), mod[3].table, mod[4].table, *([w_in] * (2 * tps)), conv_w, cb, w_out,
      mod[5].table, ng3.reshape(1, d))
    return out.reshape(b, l, d)


def kernel(x, c, ctx, c_ctx, w_mod, b_mod, norm_g, a_w_in, a_lower_logits, a_out_g, a_w_out, b_w_qkv, b_rpb,
           b_w_out, c_lam_re, c_lam_im, c_log_dt, c_b_re, c_b_im, c_c_re, c_c_im, c_d, c_w_glu, f_w_in,
           f_conv_w, f_conv_b, f_w_out):
    bsz, _, d = x.shape
    depth = w_mod.shape[0]
    p = jax.nn.softmax(a_lower_logits.astype(F32), axis=0)
    lower = jnp.cumsum(p, axis=0) - p[0]
    n_rows = -(-(bsz + 1) // SUBLANES) * SUBLANES
    c_rows = jnp.zeros((n_rows, d), F32).at[:bsz].set(c).at[bsz].set(c_ctx)
    table = _modulation(c_rows, w_mod, b_mod).reshape(depth * n_rows * N_MOD, 1, d)
    wb = lambda w: w.astype(BF16)
    lat, cx = x, ctx
    for i in range(depth):
        kind, j = i % N_MIXERS, i // N_MIXERS
        last = i == depth - 1
        mod_l = [ModRow(table, i * n_rows * N_MOD + k, N_MOD) for k in range(N_MOD)]
        mod_c = [ModRow(table, (i * n_rows + bsz) * N_MOD + k, 0) for k in range(N_MOD)]
        ng = norm_g[i]
        if kind == 0:
            cx1, lat = _hgrn2_mixer(cx, lat, ng[0], mod_c, mod_l, wb(a_w_in[j]), lower[i], a_out_g[j],
                                    wb(a_w_out[j]), ng[1], not last)
        elif kind == 1:
            cx1, lat = _na_mixer(cx, lat, ng[0], mod_c, mod_l, wb(b_w_qkv[j]), b_rpb[j], wb(b_w_out[j]), ng[1],
                                 not last)
        else:
            cx1, lat = _s5_mixer(cx, lat, ng[0], mod_c, mod_l, c_lam_re[j], c_lam_im[j], c_log_dt[j], c_b_re[j],
                                 c_b_im[j], c_c_re[j], c_c_im[j], c_d[j], wb(c_w_glu[j]), ng[1], not last)
        lat = _conv_ffn_block(lat, ng[2], mod_l, wb(f_w_in[i]), f_conv_w[i], f_conv_b[i], wb(f_w_out[i]), ng[3])
        if not last:
            cx = _conv_ffn_block(cx1, ng[2], mod_c, wb(f_w_in[i]), f_conv_w[i], f_conv_b[i], wb(f_w_out[i]), ng[3])
    return lat
```

```python
import functools
from typing import NamedTuple

import jax
import jax.numpy as jnp
from jax import lax
from jax.experimental import pallas as pl
from jax.experimental.pallas import tpu as pltpu

F32 = jnp.float32
BF16 = jnp.bfloat16

EPS = 1e-6
F_MIN = 1e-30
N_MOD = 6
N_MIXERS = 3
A_HEADS = 8
GRID_W = 64
S5_CHUNK = 16
NEG_BIG = -1e30
NA_ROW_BLOCK = 4
SUBLANES = 8
LANES = 128
VMEM_LIMIT_BYTES = 56 * 1024 * 1024

NT_DIMS = (((1,), (1,)), ((), ()))
TN_DIMS = (((0,), (0,)), ((), ()))


def _params(*sem):
    return pltpu.CompilerParams(dimension_semantics=sem, vmem_limit_bytes=VMEM_LIMIT_BYTES)


def _row_tile(n, want):
    t = min(n, want)
    assert n % t == 0, (n, t)
    return t


def _rms(y, g):
    return y * lax.rsqrt(jnp.mean(y * y, axis=-1, keepdims=True) + EPS) * g


def _mod_body(c_ref, w_ref, b_ref, o_ref):
    c = c_ref[...]
    s = (c * jax.nn.sigmoid(c)).astype(BF16)
    o_ref[0] = jnp.dot(s, w_ref[0].astype(BF16), preferred_element_type=F32) + b_ref[0]


def _modulation(c_rows, w_mod, b_mod):
    depth, d, n = w_mod.shape
    r = c_rows.shape[0]
    tn = n // 4
    return pl.pallas_call(
        _mod_body,
        grid=(depth, n // tn),
        in_specs=[pl.BlockSpec((r, d), lambda i, j: (0, 0)),
                  pl.BlockSpec((1, d, tn), lambda i, j: (i, 0, j)),
                  pl.BlockSpec((1, 1, tn), lambda i, j: (i, 0, j))],
        out_specs=pl.BlockSpec((1, r, tn), lambda i, j: (i, 0, j)),
        out_shape=jax.ShapeDtypeStruct((depth, r, n), F32),
        compiler_params=_params("parallel", "parallel"),
        name="adaln_mod",
    )(c_rows, w_mod, b_mod.reshape(depth, 1, n))


def _norm_mod(x, g, sh, sc):
    return _rms(x, g) * (1.0 + sc) + sh


class ModRow(NamedTuple):
    table: jax.Array
    first: int
    step: int


def _mod_spec(m, per):
    return pl.BlockSpec((1, 1, m.table.shape[-1]), lambda *idx: (m.first + m.step * (idx[0] // per), 0, 0))


def _proj_body(x_ref, g_ref, sh_ref, sc_ref, w_ref, o_ref):
    h = _norm_mod(x_ref[...], g_ref[...], sh_ref[0], sc_ref[0]).astype(BF16)
    o_ref[...] = jnp.dot(h, w_ref[...], preferred_element_type=F32).astype(o_ref.dtype)


def _norm_mod_proj(x, g, sh, sc, w, out_dtype):
    b, l, d = x.shape
    n = w.shape[1]
    tm = _row_tile(l, 512)
    per = l // tm
    out = pl.pallas_call(
        _proj_body,
        grid=(b * per,),
        in_specs=[pl.BlockSpec((tm, d), lambda i: (i, 0)),
                  pl.BlockSpec((1, d), lambda i: (0, 0)),
                  _mod_spec(sh, per), _mod_spec(sc, per),
                  pl.BlockSpec((d, n), lambda i: (0, 0), pipeline_mode=pl.Buffered(1))],
        out_specs=pl.BlockSpec((tm, n), lambda i: (i, 0)),
        out_shape=jax.ShapeDtypeStruct((b * l, n), out_dtype),
        compiler_params=_params("parallel"),
        name="norm_mod_proj",
    )(x.reshape(b * l, d), g.reshape(1, d), sh.table, sc.table, w)
    return out.reshape(b, l, n)


def _residual(res_ref, gate_ref, g1_ref, y):
    return res_ref[...] + gate_ref[0] * _rms(y, g1_ref[...])


def _out_plain_body(a_ref, w_ref, res_ref, gate_ref, g1_ref, o_ref):
    y = jnp.dot(a_ref[...], w_ref[...], preferred_element_type=F32)
    o_ref[...] = _residual(res_ref, gate_ref, g1_ref, y)


def _out_hgrn_body(of_ref, ob_ref, gp_ref, og_ref, w_ref, res_ref, gate_ref, g1_ref, o_ref):
    o = of_ref[...].astype(F32) + ob_ref[...].astype(F32)
    parts = []
    for h in range(A_HEADS):
        oh = o[:, h * LANES:(h + 1) * LANES]
        parts.append(oh * lax.rsqrt(jnp.mean(oh * oh, axis=-1, keepdims=True) + EPS))
    gp = gp_ref[...]
    a = jnp.concatenate(parts, axis=-1) * og_ref[...] * (gp * jax.nn.sigmoid(gp))
    y = jnp.dot(a.astype(BF16), w_ref[...], preferred_element_type=F32)
    o_ref[...] = _residual(res_ref, gate_ref, g1_ref, y)


def _gelu_tanh(y):
    return 0.5 * y * (1.0 + jnp.tanh(0.7978845608028654 * (y + 0.044715 * (y * y * y))))


def _out_call(body, row_inputs, const_inputs, res, gate, g1, name):
    b, l, d = res.shape
    tm = _row_tile(l, 512)
    per = l // tm
    in_specs, args = [], []
    for arr, blk, width in row_inputs:
        in_specs.append(pl.BlockSpec((tm, width), lambda i, blk=blk: (i, blk)))
        args.append(arr.reshape(b * l, arr.shape[-1]))
    for arr in const_inputs:
        in_specs.append(pl.BlockSpec(arr.shape, lambda i, nd=arr.ndim: (0,) * nd))
        args.append(arr)
    in_specs += [pl.BlockSpec((tm, d), lambda i: (i, 0)),
                 _mod_spec(gate, per),
                 pl.BlockSpec((1, d), lambda i: (0, 0))]
    args += [res.reshape(b * l, d), gate.table, g1.reshape(1, d)]
    out = pl.pallas_call(
        body,
        grid=(b * per,),
        in_specs=in_specs,
        out_specs=pl.BlockSpec((tm, d), lambda i: (i, 0)),
        out_shape=jax.ShapeDtypeStruct((b * l, d), F32),
        compiler_params=_params("parallel"),
        name=name,
    )(*args)
    return out.reshape(b, l, d)


HGRN_CHUNK = 32
HGRN_SAFE_MIN = 1e-30
HGRN_Q_HEADROOM = 1e37


def _hgrn_gates(z, lb):
    f = lb + (1.0 - lb) * jax.nn.sigmoid(z)
    return jnp.maximum(f, F_MIN), 1.0 - f


def _hgrn_chunk_prep(q, v, z, lb, reverse):
    c = q.shape[0]
    half = c // 2
    fm, kk = _hgrn_gates(z, lb)
    pos = lax.broadcasted_iota(jnp.int32, (c, LANES), 0)
    if reverse:
        pos = (c - 1) - pos
    second = pos >= half
    hpos = jnp.where(second, pos - half, pos)
    ph = fm
    j = 1
    while j < half:
        ph = ph * jnp.where(hpos >= j, pltpu.roll(ph, (c - j) if reverse else j, 0), 1.0)
        j *= 2
    a_last = ph[half:half + 1] if reverse else ph[half - 1:half]
    r_last = ph[0:1] if reverse else ph[c - 1:c]
    p_mid = a_last
    p_last = a_last * r_last
    pp = jnp.where(second, ph * a_last, ph)
    r = jnp.where(second, ph, ph * (1.0 / a_last))
    kd = kk * (1.0 / r)
    q_max = jnp.max(jnp.abs(q), axis=0, keepdims=True)
    ok = jnp.where(p_mid >= HGRN_SAFE_MIN, r_last, 0.0) >= HGRN_SAFE_MIN
    ok = jnp.where(ok, p_mid * HGRN_Q_HEADROOM, -1.0) >= q_max
    return dict(qr=(q * r).astype(BF16), kd=kd.astype(BF16), qp=(q * pp).astype(BF16),
                kl=(kd * r_last).astype(BF16), vb=v.astype(BF16), dec=p_last, bad=jnp.where(ok, 0.0, 1.0))


def _hgrn_chunk_dots(p, st):
    sc = lax.dot_general(p["qr"], p["kd"], NT_DIMS, preferred_element_type=F32)
    o_st = lax.dot_general(p["qp"], st.astype(BF16), NT_DIMS, preferred_element_type=F32)
    u = lax.dot_general(p["vb"], p["kl"], TN_DIMS, preferred_element_type=F32)
    return sc, o_st, u


def _hgrn_chunk_out(p, sc, o_st, reverse):
    c = sc.shape[0]
    row = lax.broadcasted_iota(jnp.int32, (c, c), 0)
    col = lax.broadcasted_iota(jnp.int32, (c, c), 1)
    sc = jnp.where((col >= row) if reverse else (col <= row), sc, 0.0)
    return jnp.dot(sc.astype(BF16), p["vb"], preferred_element_type=F32) + o_st


def _hgrn_exact_block(q_ref, v_ref, z_ref, lb_ref, o_scr, st_scr, dr, reverse, n_tiles):
    pos = lax.broadcasted_iota(jnp.int32, (SUBLANES, LANES), 0)
    if reverse:
        pos = (SUBLANES - 1) - pos

    def earlier(x, j):
        if j == 0:
            return x
        return pltpu.roll(x, (SUBLANES - j) if reverse else j, 0)

    def later(x, j):
        return pltpu.roll(x, j if reverse else (SUBLANES - j), 0)

    def tile(i, carry):
        ti = (n_tiles - 1 - i) if reverse else i
        r0 = pl.multiple_of(ti * SUBLANES, SUBLANES)
        for h in range(A_HEADS):
            sl = slice(h * LANES, (h + 1) * LANES)
            lb = lb_ref[:, sl]
            z = z_ref[0, pl.ds(r0, SUBLANES), sl]
            q = q_ref[0, pl.ds(r0, SUBLANES), sl]
            v = v_ref[0, pl.ds(r0, SUBLANES), sl]
            fm, kk = _hgrn_gates(z, lb)
            pp = fm
            for j in (1, 2, 4):
                pp = pp * jnp.where(pos >= j, earlier(pp, j), 1.0)
            qq = jnp.where(pos <= SUBLANES - 2, later(fm, 1), 1.0)
            for j in (1, 2, 4):
                qq = qq * jnp.where(pos <= SUBLANES - 1 - j, later(qq, j), 1.0)
            dec = pp[0:1] if reverse else pp[SUBLANES - 1:SUBLANES]
            st = st_scr[dr, h]
            o = lax.dot_general((q * pp).astype(BF16), st.astype(BF16), NT_DIMS,
                                preferred_element_type=F32)
            g = fm
            for d in range(SUBLANES):
                if d == 0:
                    e = q * kk
                else:
                    if d > 1:
                        g = g * earlier(fm, d - 1)
                    e = q * g * jnp.where(pos >= d, earlier(kk, d), 0.0)
                o = o + jnp.sum(e, axis=-1, keepdims=True) * earlier(v, d)
            o_scr[pl.ds(r0, SUBLANES), sl] = o
            u = lax.dot_general(v.astype(BF16), (kk * qq).astype(BF16), TN_DIMS,
                                preferred_element_type=F32)
            st_scr[dr, h] = dec * st + u
        return carry

    lax.fori_loop(0, n_tiles, tile, 0)


def _hgrn_scan_body(qf_ref, vf_ref, zf_ref, qb_ref, vb_ref, zb_ref, lb_ref, s0_ref, of_ref, ob_ref, sfin_ref,
                    st_scr, save_scr, ox_scr, *, n_chunks):
    step = pl.program_id(1)

    @pl.when(step == 0)
    def _():
        st_scr[...] = s0_ref[0]

    save_scr[...] = st_scr[...]
    dirs = ((qf_ref, vf_ref, zf_ref, of_ref, False), (qb_ref, vb_ref, zb_ref, ob_ref, True))
    c = HGRN_CHUNK

    def chunk(i, bad):
        work, dots = [], []
        for dr, (q_ref, v_ref, z_ref, o_ref, reverse) in enumerate(dirs):
            rows = pl.ds(pl.multiple_of(((n_chunks - 1 - i) if reverse else i) * c, c), c)
            group = []
            for h in range(A_HEADS):
                sl = slice(h * LANES, (h + 1) * LANES)
                p = _hgrn_chunk_prep(q_ref[0, rows, sl], v_ref[0, rows, sl], z_ref[0, rows, sl], lb_ref[:, sl],
                                     reverse)
                bad = jnp.maximum(bad, p["bad"])
                group.append((dr, h, o_ref, rows, sl, reverse, p))
            dots += [_hgrn_chunk_dots(p, st_scr[dr, h]) for dr, h, _, _, _, _, p in group]
            work += group
        for (dr, h, _, _, _, _, p), (_, _, u) in zip(work, dots):
            st_scr[dr, h] = p["dec"] * st_scr[dr, h] + u
        for (dr, h, o_ref, rows, sl, reverse, p), (sc, o_st, _) in zip(work, dots):
            o_ref[0, rows, sl] = _hgrn_chunk_out(p, sc, o_st, reverse).astype(o_ref.dtype)
        return bad

    bad = lax.fori_loop(0, n_chunks, chunk, jnp.zeros((1, LANES), F32))

    @pl.when(jnp.max(bad) > 0.0)
    def _():
        st_scr[...] = save_scr[...]
        for dr, (q_ref, v_ref, z_ref, o_ref, reverse) in enumerate(dirs):
            _hgrn_exact_block(q_ref, v_ref, z_ref, lb_ref, ox_scr, st_scr, dr, reverse, n_chunks * c // SUBLANES)
            o_ref[0] = ox_scr[...].astype(o_ref.dtype)

    @pl.when(step == pl.num_programs(1) - 1)
    def _():
        sfin_ref[0] = st_scr[...]


def _hgrn_scan(proj, lb, s0):
    b, l, _ = proj.shape
    d = A_HEADS * LANES
    t = _row_tile(l, 256)
    nb = l // t
    fwd = lambda blk: pl.BlockSpec((1, t, d), lambda i, s: (i, s, blk))
    bwd = lambda blk: pl.BlockSpec((1, t, d), lambda i, s: (i, nb - 1 - s, blk))
    st_spec = pl.BlockSpec((1,) + s0.shape[1:], lambda i, s: (i, 0, 0, 0, 0))
    return pl.pallas_call(
        functools.partial(_hgrn_scan_body, n_chunks=t // HGRN_CHUNK),
        grid=(b, nb),
        in_specs=[fwd(0), fwd(1), fwd(3), bwd(0), bwd(1), bwd(4),
                  pl.BlockSpec((1, d), lambda i, s: (0, 0)), st_spec],
        out_specs=[fwd(0), bwd(0), st_spec],
        out_shape=[jax.ShapeDtypeStruct((b, l, d), BF16), jax.ShapeDtypeStruct((b, l, d), BF16),
                   jax.ShapeDtypeStruct(s0.shape, F32)],
        scratch_shapes=[pltpu.VMEM(s0.shape[1:], F32), pltpu.VMEM(s0.shape[1:], F32), pltpu.VMEM((t, d), F32)],
        compiler_params=_params("parallel", "arbitrary"),
        name="hgrn_scan",
    )(proj, proj, proj, proj, proj, proj, lb, s0)


def _hgrn2_mixer(x_c, x_l, ng, mod_c, mod_l, w_in, lower, out_g, w_out, g1, want_ctx):
    d = x_l.shape[-1]
    p_c = _norm_mod_proj(x_c, ng, mod_c[0], mod_c[1], w_in, F32)
    p_l = _norm_mod_proj(x_l, ng, mod_l[0], mod_l[1], w_in, F32)
    lb = lower.reshape(1, d)
    zero = jnp.zeros((x_l.shape[0], 2, A_HEADS, LANES, LANES), F32)
    ocf, ocb, s_ctx = _hgrn_scan(p_c, lb, zero)
    olf, olb, _ = _hgrn_scan(p_l, lb, s_ctx)
    o_c, o_l = (ocf, ocb), (olf, olb)
    og = out_g.reshape(1, d)
    new_l = _out_call(_out_hgrn_body, [(o_l[0], 0, d), (o_l[1], 0, d), (p_l, 2, d)], [og, w_out],
                      x_l, mod_l[2], g1, "hgrn_out")
    new_c = None
    if want_ctx:
        new_c = _out_call(_out_hgrn_body, [(o_c[0], 0, d), (o_c[1], 0, d), (p_c, 2, d)], [og, w_out],
                          x_c, mod_c[2], g1, "hgrn_out")
    return new_c, new_l


def _softmax_pv(s_list, v_list):
    mx = functools.reduce(jnp.maximum, [jnp.max(s, axis=-1, keepdims=True) for s in s_list])
    ps = [jnp.exp(s - mx) for s in s_list]
    den = functools.reduce(jnp.add, [jnp.sum(p, axis=-1, keepdims=True) for p in ps])
    acc = functools.reduce(jnp.add, [jnp.dot(p.astype(BF16), v, preferred_element_type=F32)
                                     for p, v in zip(ps, v_list)])
    return acc / den


def _na_lat_body(q_ref, k_ref, v_ref, kc_ref, vc_ref, t2_ref, o_ref, *, rows, kh, scale):
    w = GRID_W
    rb = NA_ROW_BLOCK
    nk = kh + rb
    n_rel = 2 * kh - 1
    first = lax.broadcasted_iota(jnp.int32, (rb * w, LANES), 1) < (LANES // 2)
    kc = kc_ref[0]
    vc = vc_ref[0]

    def bias_index(r, ks, m):
        r0 = jnp.clip(r - kh // 2, 0, rows - kh)
        ka = ks + 2 * m
        rel_a = ka - r + (kh - 1)
        in_a = (ka >= r0) & (ka < r0 + kh)
        in_b = (ka + 1 >= r0) & (ka + 1 < r0 + kh)
        both, only_b, only_a = rel_a, (n_rel - 1) + rel_a + 1, (2 * n_rel - 1) + rel_a
        return jnp.where(in_a, jnp.where(in_b, both, only_a), jnp.where(in_b, only_b, 3 * n_rel - 1))

    def block(i, carry):
        rq = i * rb
        ks = jnp.clip(rq - kh // 2, 0, rows - nk)
        q2 = q_ref[0, pl.ds(pl.multiple_of(rq * w, rb * w), rb * w), :] * scale
        k2 = k_ref[0, pl.ds(pl.multiple_of(ks * w, w), nk * w), :]
        v2 = v_ref[0, pl.ds(pl.multiple_of(ks * w, w), nk * w), :]
        outs = []
        for hh in range(2):
            qm = jnp.where(first if hh == 0 else jnp.logical_not(first), q2, jnp.zeros_like(q2))
            bias = jnp.concatenate(
                [jnp.concatenate([t2_ref[hh, bias_index(rq + a, ks, m)] for m in range(nk // 2)], axis=-1)
                 for a in range(rb)], axis=0)
            s_w = lax.dot_general(qm, k2, NT_DIMS, preferred_element_type=F32) + bias
            s_c = lax.dot_general(qm, kc, NT_DIMS, preferred_element_type=F32)
            outs.append(_softmax_pv([s_w, s_c], [v2, vc]))
        o_ref[0, pl.ds(pl.multiple_of(rq * w, rb * w), rb * w), :] = (
            jnp.where(first, outs[0], outs[1]).astype(o_ref.dtype))
        return carry

    lax.fori_loop(0, rows // rb, block, 0, unroll=2)


def _na_ctx_body(q_ref, k_ref, v_ref, o_ref, *, scale):
    n = q_ref.shape[1]
    first = lax.broadcasted_iota(jnp.int32, (n, LANES), 1) < (LANES // 2)
    q2 = q_ref[0] * scale
    k2 = k_ref[0]
    v2 = v_ref[0]
    outs = []
    for hh in range(2):
        qm = jnp.where(first if hh == 0 else jnp.logical_not(first), q2, jnp.zeros_like(q2))
        s = lax.dot_general(qm, k2, NT_DIMS, preferred_element_type=F32)
        outs.append(_softmax_pv([s], [v2]))
    o_ref[0] = jnp.where(first, outs[0], outs[1]).astype(o_ref.dtype)


def _na_bias_table(rpb, kh, kw):
    w = jnp.arange(GRID_W)[:, None]
    c = jnp.arange(GRID_W)[None, :]
    c0 = jnp.clip(w - kw // 2, 0, GRID_W - kw)
    inside = (c >= c0) & (c < c0 + kw)
    pad = GRID_W - kw
    padded = jnp.pad(rpb.astype(F32), ((0, 0), (0, 0), (pad, pad)))
    shifted = jnp.stack([padded[:, :, pad + kw - 1 - q:pad + kw - 1 - q + GRID_W] for q in range(GRID_W)], axis=2)
    t = jnp.where(inside[None, None], shifted, NEG_BIG)
    off = jnp.full_like(t, NEG_BIG)
    pair = lambda a, b: jnp.concatenate([a, b], axis=-1)
    return jnp.concatenate([pair(t[:, :-1], t[:, 1:]), pair(off, t), pair(t, off), pair(off, off)[:, :1]], axis=1)


def _na_mixer(x_c, x_l, ng, mod_c, mod_l, w_qkv, rpb, w_out, g1, want_ctx):
    b, l, d = x_l.shape
    lc = x_c.shape[1]
    heads = rpb.shape[0]
    na_rows, na_cols = (rpb.shape[1] + 1) // 2, (rpb.shape[2] + 1) // 2
    dh = d // heads
    assert 2 * dh == LANES and l % GRID_W == 0
    rows = l // GRID_W
    kh = min(na_rows, rows)
    assert kh == na_rows and (kh + NA_ROW_BLOCK) % 2 == 0
    assert rows % NA_ROW_BLOCK == 0 and rows >= kh + NA_ROW_BLOCK
    scale = dh ** -0.5
    nhp = heads // 2
    qkv_c = _norm_mod_proj(x_c, ng, mod_c[0], mod_c[1], w_qkv, BF16)
    qkv_l = _norm_mod_proj(x_l, ng, mod_l[0], mod_l[1], w_qkv, BF16)
    t2 = _na_bias_table(rpb, kh, na_cols)
    o_l = pl.pallas_call(
        functools.partial(_na_lat_body, rows=rows, kh=kh, scale=scale),
        grid=(b, nhp),
        in_specs=[pl.BlockSpec((1, l, LANES), lambda i, p: (i, 0, p)),
                  pl.BlockSpec((1, l, LANES), lambda i, p: (i, 0, nhp + p)),
                  pl.BlockSpec((1, l, LANES), lambda i, p: (i, 0, 2 * nhp + p)),
                  pl.BlockSpec((1, lc, LANES), lambda i, p: (i, 0, nhp + p)),
                  pl.BlockSpec((1, lc, LANES), lambda i, p: (i, 0, 2 * nhp + p)),
                  pl.BlockSpec((2,) + t2.shape[1:], lambda i, p: (p, 0, 0, 0))],
        out_specs=pl.BlockSpec((1, l, LANES), lambda i, p: (i, 0, p)),
        out_shape=jax.ShapeDtypeStruct((b, l, d), BF16),
        compiler_params=_params("parallel", "parallel"),
        name="na_latent",
    )(qkv_l, qkv_l, qkv_l, qkv_c, qkv_c, t2)
    new_l = _out_call(_out_plain_body, [(o_l, 0, d)], [w_out], x_l, mod_l[2], g1, "na_out")
    new_c = None
    if want_ctx:
        o_c = pl.pallas_call(
            functools.partial(_na_ctx_body, scale=scale),
            grid=(b, nhp),
            in_specs=[pl.BlockSpec((1, lc, LANES), lambda i, p: (i, 0, p)),
                      pl.BlockSpec((1, lc, LANES), lambda i, p: (i, 0, nhp + p)),
                      pl.BlockSpec((1, lc, LANES), lambda i, p: (i, 0, 2 * nhp + p))],
            out_specs=pl.BlockSpec((1, lc, LANES), lambda i, p: (i, 0, p)),
            out_shape=jax.ShapeDtypeStruct((b, lc, d), BF16),
            compiler_params=_params("parallel", "parallel"),
            name="na_context",
        )(qkv_c, qkv_c, qkv_c)
        new_c = _out_call(_out_plain_body, [(o_c, 0, d)], [w_out], x_c, mod_c[2], g1, "na_out")
    return new_c, new_l


S5_GROUP_BLOCK = 8


def _s5_kernel_body(c_ref, w_ref, o_ref):
    for i in range(c_ref.shape[0]):
        o_ref[i] = jnp.dot(c_ref[i], w_ref[i], preferred_element_type=F32, precision=lax.Precision.HIGHEST)


def _s5_impulse(cmat, wmat):
    n = cmat.shape[0]
    gb = S5_GROUP_BLOCK
    return pl.pallas_call(
        _s5_kernel_body,
        grid=(n // gb,),
        in_specs=[pl.BlockSpec((gb,) + cmat.shape[1:], lambda i: (i, 0, 0)),
                  pl.BlockSpec((gb,) + wmat.shape[1:], lambda i: (i, 0, 0))],
        out_specs=pl.BlockSpec((gb, cmat.shape[1], wmat.shape[2]), lambda i: (i, 0, 0)),
        out_shape=jax.ShapeDtypeStruct((n, cmat.shape[1], wmat.shape[2]), F32),
        compiler_params=_params("parallel"),
        name="s5_impulse",
    )(cmat, wmat)


def _s5_operators(lam_re, lam_im, log_dt, b_re, b_im, c_re, c_im, t):
    _, g, p = lam_re.shape
    cg = b_re.shape[-1]
    lam = lax.complex(lam_re.astype(F32), lam_im.astype(F32))
    ldt = lam * jnp.exp(log_dt.astype(F32))[..., None]
    a = jnp.exp(ldt)
    bbar = ((a - 1.0) / lam)[..., None] * lax.complex(b_re.astype(F32), b_im.astype(F32))
    cm = lax.complex(c_re.astype(F32), c_im.astype(F32))
    apow = jnp.exp(ldt[..., None] * jnp.arange(t + 1, dtype=F32))
    w = apow[..., :t, None] * bbar[:, :, :, None, :]
    wmat = jnp.concatenate([jnp.real(w), jnp.imag(w)], axis=2).reshape(2 * g, 2 * p, t * cg)
    cmat = jnp.concatenate([jnp.real(cm), -jnp.imag(cm)], axis=-1).reshape(2 * g, cg, 2 * p)
    k = _s5_impulse(cmat, wmat).reshape(2, g, cg, t, cg)
    k = jnp.transpose(k, (0, 1, 3, 4, 2))
    kfull = jnp.concatenate([k[1, :, :0:-1], (k[0, :, :1] + k[1, :, :1]), k[0, :, 1:]], axis=1)
    lagvec = jnp.transpose(kfull, (0, 2, 1, 3)).reshape(g, cg, (2 * t - 1) * cg).astype(BF16)
    toep = jnp.stack([lagvec[:, :, (t - 1 - s) * cg:(2 * t - 1 - s) * cg] for s in range(t)], axis=1)
    toep = toep.reshape(g, t * cg, t * cg)

    def state_in(wd, flip):
        wd = wd[:, :, ::-1] if flip else wd
        m = jnp.transpose(wd, (0, 2, 3, 1)).reshape(g, t * cg, p)
        return jnp.concatenate([jnp.real(m), jnp.imag(m)], axis=-1)

    def state_out(cd, pw):
        n = cd[:, :, :, None] * pw[:, None, :, :]
        n = jnp.transpose(n, (0, 2, 3, 1)).reshape(g, p, t * cg)
        return jnp.concatenate([jnp.real(n), -jnp.imag(n)], axis=1)

    m_f = state_in(w[0], True)
    m_b = state_in(w[1], False)
    n_f = state_out(cm[0], apow[0][..., 1:])
    n_b = state_out(cm[1], apow[1][..., :0:-1])
    at = apow[..., t]
    a1 = jnp.concatenate([jnp.real(at), jnp.real(at)], axis=-1)
    a2 = jnp.concatenate([-jnp.imag(at), jnp.imag(at)], axis=-1)
    bf = lambda m: m.astype(BF16)
    return dict(toep=bf(toep), m_f=bf(m_f), m_b=bf(m_b), n_f=bf(n_f), n_b=bf(n_b), a1=a1, a2=a2)


def _s5_body(u_ref, dsk_ref, toep_ref, mf_ref, mb_ref, nf_ref, nb_ref, a1_ref, a2_ref, x0_ref,
             y_ref, xfin_ref, z_scr, zs_scr, xin_scr):
    gb, nc = u_ref.shape[1], u_ref.shape[2]
    half = LANES // 2
    for gi in range(gb):
        ub = u_ref[0, gi].astype(BF16)
        z_scr[0, :, gi, :] = jnp.dot(ub, mf_ref[gi], preferred_element_type=F32)
        z_scr[1, :, gi, :] = jnp.dot(ub, mb_ref[gi], preferred_element_type=F32)

    swap = lambda t: pltpu.roll(t, half, t.ndim - 1)
    for dr in range(2):
        zs_scr[dr] = swap(z_scr[dr].reshape(nc * gb, LANES)).reshape(nc, gb, LANES)

    coef = [(a1_ref[dr], a2_ref[dr]) for dr in range(2)]

    def scan(j, xs):
        out = []
        for dr in range(2):
            jj = j if dr == 0 else nc - 1 - j
            x, xw = xs[2 * dr], xs[2 * dr + 1]
            a1, a2 = coef[dr]
            xin_scr[dr, jj] = x
            out.append(a1 * x + a2 * xw + z_scr[dr, jj])
            out.append(a1 * xw - a2 * x + zs_scr[dr, jj])
        return tuple(out)

    x0f, x0b = x0_ref[0, 0], x0_ref[0, 1]
    xf, _, xb, _ = lax.fori_loop(0, nc, scan, (x0f, swap(x0f), x0b, swap(x0b)))
    xfin_ref[0, 0] = xf
    xfin_ref[0, 1] = xb
    for gi in range(gb):
        u = u_ref[0, gi]
        y = jnp.dot(u.astype(BF16), toep_ref[gi], preferred_element_type=F32) + u * dsk_ref[gi]
        y = y + jnp.dot(xin_scr[0, :, gi, :].astype(BF16), nf_ref[gi], preferred_element_type=F32)
        y = y + jnp.dot(xin_scr[1, :, gi, :].astype(BF16), nb_ref[gi], preferred_element_type=F32)
        y_ref[0, gi] = y


def _piece_transpose(src, piece_of, cg):
    n = len(src)
    out = [None] * n
    for s in range(n):
        c = src[s % n]
        for p in range(1, n):
            c = jnp.where(piece_of == p, src[(p + s) % n], c)
        r = c if s == 0 else pltpu.roll(c, s * cg, 1)
        for p in range(n):
            out[p] = r if s == 0 else jnp.where(piece_of == (p + s) % n, r, out[p])
    return out


def _to_chunks(h_scr, o_ref, t, cg):
    ncb = h_scr.shape[1] // t
    pieces = LANES // cg
    piece_of = lax.broadcasted_iota(jnp.int32, (ncb, LANES), 1) // cg

    def lane_block(lb, carry):
        for tq in range(t // pieces):
            steps = [h_scr[lb, pl.ds(tq * pieces + tr, ncb, stride=t), :] for tr in range(pieces)]
            for gl, col in enumerate(_piece_transpose(steps, piece_of, cg)):
                o_ref[0, lb * pieces + gl, :, tq * LANES:(tq + 1) * LANES] = col
        return carry

    lax.fori_loop(0, h_scr.shape[0], lane_block, 0)


def _from_chunks(y_ref, y_scr, t, cg):
    ncb = y_ref.shape[2]
    pieces = LANES // cg
    piece_of = lax.broadcasted_iota(jnp.int32, (ncb, LANES), 1) // cg

    def lane_block(lb, carry):
        for tq in range(t // pieces):
            cols = [y_ref[0, lb * pieces + gl, :, tq * LANES:(tq + 1) * LANES] for gl in range(pieces)]
            for tr, step in enumerate(_piece_transpose(cols, piece_of, cg)):
                y_scr[lb, pl.ds(tq * pieces + tr, ncb, stride=t), :] = step
        return carry

    lax.fori_loop(0, y_scr.shape[0], lane_block, 0)


def _norm_chunks_body(x_ref, g_ref, sh_ref, sc_ref, o_ref, h_scr, *, t, cg):
    h = _norm_mod(x_ref[...], g_ref[...], sh_ref[0], sc_ref[0])
    for lb in range(h_scr.shape[0]):
        h_scr[lb] = h[:, lb * LANES:(lb + 1) * LANES]
    _to_chunks(h_scr, o_ref, t, cg)


def _norm_mod_chunks(x, g, sh, sc, groups, t):
    b, l, d = x.shape
    cg = d // groups
    tm = _row_tile(l, 512)
    per = l // tm
    assert tm % t == 0 and (tm // t) % SUBLANES == 0 and LANES % cg == 0 and t % (LANES // cg) == 0
    return pl.pallas_call(
        functools.partial(_norm_chunks_body, t=t, cg=cg),
        grid=(b * per,),
        in_specs=[pl.BlockSpec((tm, d), lambda i: (i, 0)),
                  pl.BlockSpec((1, d), lambda i: (0, 0)),
                  _mod_spec(sh, per), _mod_spec(sc, per)],
        out_specs=pl.BlockSpec((1, groups, tm // t, t * cg), lambda i: (i // per, 0, i % per, 0)),
        out_shape=jax.ShapeDtypeStruct((b, groups, l // t, t * cg), F32),
        scratch_shapes=[pltpu.VMEM((d // LANES, tm, LANES), F32)],
        compiler_params=_params("parallel"),
        name="norm_mod_chunks",
    )(x.reshape(b * l, d), g.reshape(1, d), sh.table, sc.table)


def _glu_chunks_body(y_ref, w_ref, res_ref, gate_ref, g1_ref, o_ref, y_scr, *, t, cg):
    _from_chunks(y_ref, y_scr, t, cg)
    d = o_ref.shape[-1]
    y = jnp.concatenate([y_scr[lb] for lb in range(y_scr.shape[0])], axis=-1)
    a = _gelu_tanh(y).astype(BF16)
    ag = jnp.dot(a, w_ref[...], preferred_element_type=F32)
    o_ref[...] = _residual(res_ref, gate_ref, g1_ref, ag[:, :d] * jax.nn.sigmoid(ag[:, d:]))


def _glu_out_chunks(y, w_glu, res, gate, g1, t):
    b, l, d = res.shape
    groups = y.shape[1]
    cg = d // groups
    tm = _row_tile(l, 512)
    per = l // tm
    out = pl.pallas_call(
        functools.partial(_glu_chunks_body, t=t, cg=cg),
        grid=(b * per,),
        in_specs=[pl.BlockSpec((1, groups, tm // t, t * cg), lambda i: (i // per, 0, i % per, 0)),
                  pl.BlockSpec(w_glu.shape, lambda i: (0, 0)),
                  pl.BlockSpec((tm, d), lambda i: (i, 0)),
                  _mod_spec(gate, per),
                  pl.BlockSpec((1, d), lambda i: (0, 0))],
        out_specs=pl.BlockSpec((tm, d), lambda i: (i, 0)),
        out_shape=jax.ShapeDtypeStruct((b * l, d), F32),
        scratch_shapes=[pltpu.VMEM((d // LANES, tm, LANES), F32)],
        compiler_params=_params("parallel"),
        name="s5_glu_out",
    )(y, w_glu, res.reshape(b * l, d), gate.table, g1.reshape(1, d))
    return out.reshape(b, l, d)


def _s5_scan(u, ops, dsk, x0):
    b, g, nc, tc = u.shape
    gb = S5_GROUP_BLOCK
    per_g = lambda shape: pl.BlockSpec((gb,) + shape, lambda j, i: (j,) + (0,) * len(shape))
    y, xfin = pl.pallas_call(
        _s5_body,
        grid=(g // gb, b),
        in_specs=[pl.BlockSpec((1, gb, nc, tc), lambda j, i: (i, j, 0, 0)),
                  per_g((1, tc)), per_g((tc, tc)), per_g((tc, LANES)), per_g((tc, LANES)),
                  per_g((LANES, tc)), per_g((LANES, tc)),
                  pl.BlockSpec((2, gb, LANES), lambda j, i: (0, j, 0)),
                  pl.BlockSpec((2, gb, LANES), lambda j, i: (0, j, 0)),
                  pl.BlockSpec((1, 2, gb, LANES), lambda j, i: (i, 0, j, 0))],
        out_specs=[pl.BlockSpec((1, gb, nc, tc), lambda j, i: (i, j, 0, 0)),
                   pl.BlockSpec((1, 2, gb, LANES), lambda j, i: (i, 0, j, 0))],
        out_shape=[jax.ShapeDtypeStruct((b, g, nc, tc), F32), jax.ShapeDtypeStruct((b, 2, g, LANES), F32)],
        scratch_shapes=[pltpu.VMEM((2, nc, gb, LANES), F32)] * 3,
        compiler_params=_params("parallel", "parallel"),
        name="s5_scan",
    )(u, dsk, ops["toep"], ops["m_f"], ops["m_b"], ops["n_f"], ops["n_b"], ops["a1"], ops["a2"], x0)
    return y, xfin


def _s5_mixer(x_c, x_l, ng, mod_c, mod_l, lam_re, lam_im, log_dt, b_re, b_im, c_re, c_im, d_skip, w_glu, g1,
              want_ctx):
    b, l, d = x_l.shape
    g, p = lam_re.shape[1], lam_re.shape[2]
    cg = d // g
    assert 2 * p == LANES and g % S5_GROUP_BLOCK == 0
    ops = _s5_operators(lam_re, lam_im, log_dt, b_re, b_im, c_re, c_im, S5_CHUNK)
    dsk = jnp.tile(d_skip.astype(F32).reshape(g, 1, cg), (1, 1, S5_CHUNK))
    u_c = _norm_mod_chunks(x_c, ng, mod_c[0], mod_c[1], g, S5_CHUNK)
    u_l = _norm_mod_chunks(x_l, ng, mod_l[0], mod_l[1], g, S5_CHUNK)
    y_c, x_ctx = _s5_scan(u_c, ops, dsk, jnp.zeros((b, 2, g, LANES), F32))
    y_l, _ = _s5_scan(u_l, ops, dsk, x_ctx)
    new_l = _glu_out_chunks(y_l, w_glu, x_l, mod_l[2], g1, S5_CHUNK)
    new_c = _glu_out_chunks(y_c, w_glu, x_c, mod_c[2], g1, S5_CHUNK) if want_ctx else None
    return new_c, new_l


FFN_HALO = 16
FFN_TN = 256
FFN_TILES_PER_STEP = 11
FFN_ROW_SPLIT = 256


def _ffn_body(xp_ref, x_ref, xn_ref, g2_ref, sh_ref, sc_ref, *rest, per, n_tiles, tps):
    w_refs = rest[:2 * tps]
    cw_ref, cb_ref, wo_ref, gate_ref, g3_ref, o_ref, h_scr, gat_scr = rest[2 * tps:]
    i, j = pl.program_id(0), pl.program_id(1)
    tm = x_ref.shape[0]
    hl = FFN_HALO
    f = wo_ref.shape[0]
    n_steps = pl.cdiv(n_tiles, tps)
    when = (lambda cond: lambda fn: fn()) if n_steps == 1 else pl.when

    @when(j == 0)
    def _():
        g, sh, sc = g2_ref[...], sh_ref[0], sc_ref[0]
        keep_p = ((i % per) != 0).astype(F32)
        keep_n = ((i % per) != per - 1).astype(F32)
        h_scr[0:hl] = (_norm_mod(xp_ref[...], g, sh, sc) * keep_p).astype(BF16)
        h_scr[hl:hl + tm] = _norm_mod(x_ref[...], g, sh, sc).astype(BF16)
        h_scr[hl + tm:] = (_norm_mod(xn_ref[...], g, sh, sc) * keep_n).astype(BF16)

    sub = FFN_ROW_SPLIT if tm % FFN_ROW_SPLIT == 0 else tm
    ns = tm // sub
    edge = SUBLANES
    tn = w_refs[0].shape[1]

    def pieces(w_ref):
        out = []
        for s in range(ns):
            lo = hl + s * sub - (hl if s == 0 else 0)
            hi = hl + (s + 1) * sub + (hl if s == ns - 1 else 0)
            out.append(jnp.dot(h_scr[lo:hi], w_ref[...], preferred_element_type=F32))
        return out

    def conv(p, s, cw, cb):
        base = hl if s == 0 else 0
        left = p[s][hl - edge:hl] if s == 0 else p[s - 1][-edge:]
        right = p[s][base + sub:base + sub + edge] if s == ns - 1 else p[s + 1][:edge]
        ext = jnp.concatenate([left, p[s][base:base + sub], right], axis=0)
        um = pltpu.roll(ext, 1, 0)[edge:edge + sub]
        up = pltpu.roll(ext, sub + 2 * edge - 1, 0)[edge:edge + sub]
        return cb + um * cw[0:1] + ext[edge:edge + sub] * cw[1:2] + up * cw[2:3]

    def tile(k):
        col = k * tn if n_steps == 1 else pl.multiple_of((j * tps + k) * tn, tn)
        cols_a, cols_v = pl.ds(col, tn), pl.ds(pl.multiple_of(f + col, tn), tn)
        pa, pv = pieces(w_refs[2 * k]), pieces(w_refs[2 * k + 1])
        for s in range(ns):
            a = conv(pa, s, cw_ref[:, cols_a], cb_ref[:, cols_a])
            v = conv(pv, s, cw_ref[:, cols_v], cb_ref[:, cols_v])
            gat_scr[s * sub:(s + 1) * sub, cols_a] = (a * jax.nn.sigmoid(a) * v).astype(BF16)

    always = n_tiles - tps * (n_steps - 1)
    for k in range(tps):
        if k < always:
            tile(k)
        else:
            pl.when(j * tps + k < n_tiles)(functools.partial(tile, k))

    @when(j == n_steps - 1)
    def _():
        y = jnp.dot(gat_scr[...], wo_ref[...], preferred_element_type=F32)
        o_ref[...] = x_ref[...] + gate_ref[0] * _rms(y, g3_ref[...])


def _conv_ffn_block(x, ng2, mod, w_in, conv_w, conv_b, w_out, ng3):
    b, l, d = x.shape
    f = w_out.shape[0]
    tn, hl = FFN_TN, FFN_HALO
    assert f % tn == 0 and conv_w.shape[0] == 3 and w_in.shape[1] == 2 * f
    nj = f // tn
    tm = _row_tile(l, 1024)
    per = l // tm
    hb = tm // hl
    last_hb = b * l // hl - 1
    cb = conv_b.reshape(1, 2 * f)
    x2 = x.reshape(b * l, d)
    tps = min(FFN_TILES_PER_STEP, nj)
    once = dict(pipeline_mode=pl.Buffered(1)) if tps == nj else {}
    w_specs = []
    for k in range(tps):
        w_specs.append(pl.BlockSpec((d, tn), lambda i, j, k=k: (0, jnp.minimum(j * tps + k, nj - 1)), **once))
        w_specs.append(pl.BlockSpec((d, tn), lambda i, j, k=k: (0, nj + jnp.minimum(j * tps + k, nj - 1)), **once))
    whole = lambda arr: pl.BlockSpec(arr.shape, lambda i, j: (0,) * arr.ndim)
    out = pl.pallas_call(
        functools.partial(_ffn_body, per=per, n_tiles=nj, tps=tps),
        grid=(b * per, pl.cdiv(nj, tps)),
        in_specs=[pl.BlockSpec((hl, d), lambda i, j: (jnp.maximum(i * hb - 1, 0), 0)),
                  pl.BlockSpec((tm, d), lambda i, j: (i, 0)),
                  pl.BlockSpec((hl, d), lambda i, j: (jnp.minimum((i + 1) * hb, last_hb), 0)),
                  pl.BlockSpec((1, d), lambda i, j: (0, 0)),
                  _mod_spec(mod[3], per), _mod_spec(mod[4], per),
                  *w_specs, whole(conv_w), whole(cb),
                  pl.BlockSpec((f, d), lambda i, j: (0, 0), pipeline_mode=pl.Buffered(1)),
                  _mod_spec(mod[5], per),
                  pl.BlockSpec((1, d), lambda i, j: (0, 0))],
        out_specs=pl.BlockSpec((tm, d), lambda i, j: (i, 0)),
        out_shape=jax.ShapeDtypeStruct((b * l, d), F32),
        scratch_shapes=[pltpu.VMEM((tm + 2 * hl, d), BF16), pltpu.VMEM((tm, f), BF16)],
        compiler_params=_params("parallel", "arbitrary"),
        name="conv_ffn",
    )(x2, x2, x2, ng2.reshape(1, d), mod[3].table, mod[4].table, *([w_in] * (2 * tps)), conv_w, cb, w_out,
      mod[5].table, ng3.reshape(1, d))
    return out.reshape(b, l, d)


def kernel(x, c, ctx, c_ctx, w_mod, b_mod, norm_g, a_w_in, a_lower_logits, a_out_g, a_w_out, b_w_qkv, b_rpb,
           b_w_out, c_lam_re, c_lam_im, c_log_dt, c_b_re, c_b_im, c_c_re, c_c_im, c_d, c_w_glu, f_w_in,
           f_conv_w, f_conv_b, f_w_out):
    bsz, _, d = x.shape
    depth = w_mod.shape[0]
    p = jax.nn.softmax(a_lower_logits.astype(F32), axis=0)
    lower = jnp.cumsum(p, axis=0) - p[0]
    n_rows = -(-(bsz + 1) // SUBLANES) * SUBLANES
    c_rows = jnp.zeros((n_rows, d), F32).at[:bsz].set(c).at[bsz].set(c_ctx)
    table = _modulation(c_rows, w_mod, b_mod).reshape(depth * n_rows * N_MOD, 1, d)
    wb = lambda w: w.astype(BF16)
    lat, cx = x, ctx
    for i in range(depth):
        kind, j = i % N_MIXERS, i // N_MIXERS
        last = i == depth - 1
        mod_l = [ModRow(table, i * n_rows * N_MOD + k, N_MOD) for k in range(N_MOD)]
        mod_c = [ModRow(table, (i * n_rows + bsz) * N_MOD + k, 0) for k in range(N_MOD)]
        ng = norm_g[i]
        if kind == 0:
            cx1, lat = _hgrn2_mixer(cx, lat, ng[0], mod_c, mod_l, wb(a_w_in[j]), lower[i], a_out_g[j],
                                    wb(a_w_out[j]), ng[1], not last)
        elif kind == 1:
            cx1, lat = _na_mixer(cx, lat, ng[0], mod_c, mod_l, wb(b_w_qkv[j]), b_rpb[j], wb(b_w_out[j]), ng[1],
                                 not last)
        else:
            cx1, lat = _s5_mixer(cx, lat, ng[0], mod_c, mod_l, c_lam_re[j], c_lam_im[j], c_log_dt[j], c_b_re[j],
                                 c_b_im[j], c_c_re[j], c_c_im[j], c_d[j], wb(c_w_glu[j]), ng[1], not last)
        lat = _conv_ffn_block(lat, ng[2], mod_l, wb(f_w_in[i]), f_conv_w[i], f_conv_b[i], wb(f_w_out[i]), ng[3])
        if not last:
            cx = _conv_ffn_block(cx1, ng[2], mod_c, wb(f_w_in[i]), f_conv_w[i], f_conv_b[i], wb(f_w_out[i]), ng[3])
    return lat
```

```python
import functools
from typing import NamedTuple

import jax
import jax.numpy as jnp
from jax import lax
from jax.experimental import pallas as pl
from jax.experimental.pallas import tpu as pltpu

F32 = jnp.float32
BF16 = jnp.bfloat16

EPS = 1e-6
F_MIN = 1e-30
N_MOD = 6
N_MIXERS = 3
A_HEADS = 8
GRID_W = 64
S5_CHUNK = 16
NEG_BIG = -1e30
NA_ROW_BLOCK = 4
SUBLANES = 8
LANES = 128
VMEM_LIMIT_BYTES = 56 * 1024 * 1024

NT_DIMS = (((1,), (1,)), ((), ()))
TN_DIMS = (((0,), (0,)), ((), ()))


def _params(*sem):
    return pltpu.CompilerParams(dimension_semantics=sem, vmem_limit_bytes=VMEM_LIMIT_BYTES)


def _row_tile(n, want):
    t = min(n, want)
    assert n % t == 0, (n, t)
    return t


def _rms(y, g):
    return y * lax.rsqrt(jnp.mean(y * y, axis=-1, keepdims=True) + EPS) * g


def _mod_body(c_ref, w_ref, b_ref, o_ref):
    c = c_ref[...]
    s = (c * jax.nn.sigmoid(c)).astype(BF16)
    o_ref[0] = jnp.dot(s, w_ref[0].astype(BF16), preferred_element_type=F32) + b_ref[0]


def _modulation(c_rows, w_mod, b_mod):
    depth, d, n = w_mod.shape
    r = c_rows.shape[0]
    tn = n // 4
    return pl.pallas_call(
        _mod_body,
        grid=(depth, n // tn),
        in_specs=[pl.BlockSpec((r, d), lambda i, j: (0, 0)),
                  pl.BlockSpec((1, d, tn), lambda i, j: (i, 0, j)),
                  pl.BlockSpec((1, 1, tn), lambda i, j: (i, 0, j))],
        out_specs=pl.BlockSpec((1, r, tn), lambda i, j: (i, 0, j)),
        out_shape=jax.ShapeDtypeStruct((depth, r, n), F32),
        compiler_params=_params("parallel", "parallel"),
        name="adaln_mod",
    )(c_rows, w_mod, b_mod.reshape(depth, 1, n))


def _norm_mod(x, g, sh, sc):
    return _rms(x, g) * (1.0 + sc) + sh


class ModRow(NamedTuple):
    table: jax.Array
    first: int
    step: int


def _mod_spec(m, per):
    return pl.BlockSpec((1, 1, m.table.shape[-1]), lambda *idx: (m.first + m.step * (idx[0] // per), 0, 0))


def _proj_body(x_ref, g_ref, sh_ref, sc_ref, w_ref, o_ref):
    h = _norm_mod(x_ref[...], g_ref[...], sh_ref[0], sc_ref[0]).astype(BF16)
    o_ref[...] = jnp.dot(h, w_ref[...], preferred_element_type=F32).astype(o_ref.dtype)


def _norm_mod_proj(x, g, sh, sc, w, out_dtype):
    b, l, d = x.shape
    n = w.shape[1]
    tm = _row_tile(l, 512)
    per = l // tm
    out = pl.pallas_call(
        _proj_body,
        grid=(b * per,),
        in_specs=[pl.BlockSpec((tm, d), lambda i: (i, 0)),
                  pl.BlockSpec((1, d), lambda i: (0, 0)),
                  _mod_spec(sh, per), _mod_spec(sc, per),
                  pl.BlockSpec((d, n), lambda i: (0, 0), pipeline_mode=pl.Buffered(1))],
        out_specs=pl.BlockSpec((tm, n), lambda i: (i, 0)),
        out_shape=jax.ShapeDtypeStruct((b * l, n), out_dtype),
        compiler_params=_params("parallel"),
        name="norm_mod_proj",
    )(x.reshape(b * l, d), g.reshape(1, d), sh.table, sc.table, w)
    return out.reshape(b, l, n)


def _residual(res_ref, gate_ref, g1_ref, y):
    return res_ref[...] + gate_ref[0] * _rms(y, g1_ref[...])


def _out_plain_body(a_ref, w_ref, res_ref, gate_ref, g1_ref, o_ref):
    y = jnp.dot(a_ref[...], w_ref[...], preferred_element_type=F32)
    o_ref[...] = _residual(res_ref, gate_ref, g1_ref, y)


def _out_hgrn_body(of_ref, ob_ref, gp_ref, og_ref, w_ref, res_ref, gate_ref, g1_ref, o_ref):
    o = of_ref[...].astype(F32) + ob_ref[...].astype(F32)
    parts = []
    for h in range(A_HEADS):
        oh = o[:, h * LANES:(h + 1) * LANES]
        parts.append(oh * lax.rsqrt(jnp.mean(oh * oh, axis=-1, keepdims=True) + EPS))
    gp = gp_ref[...]
    a = jnp.concatenate(parts, axis=-1) * og_ref[...] * (gp * jax.nn.sigmoid(gp))
    y = jnp.dot(a.astype(BF16), w_ref[...], preferred_element_type=F32)
    o_ref[...] = _residual(res_ref, gate_ref, g1_ref, y)


def _gelu_tanh(y):
    return 0.5 * y * (1.0 + jnp.tanh(0.7978845608028654 * (y + 0.044715 * (y * y * y))))


def _out_call(body, row_inputs, const_inputs, res, gate, g1, name):
    b, l, d = res.shape
    tm = _row_tile(l, 512)
    per = l // tm
    in_specs, args = [], []
    for arr, blk, width in row_inputs:
        in_specs.append(pl.BlockSpec((tm, width), lambda i, blk=blk: (i, blk)))
        args.append(arr.reshape(b * l, arr.shape[-1]))
    for arr in const_inputs:
        in_specs.append(pl.BlockSpec(arr.shape, lambda i, nd=arr.ndim: (0,) * nd))
        args.append(arr)
    in_specs += [pl.BlockSpec((tm, d), lambda i: (i, 0)),
                 _mod_spec(gate, per),
                 pl.BlockSpec((1, d), lambda i: (0, 0))]
    args += [res.reshape(b * l, d), gate.table, g1.reshape(1, d)]
    out = pl.pallas_call(
        body,
        grid=(b * per,),
        in_specs=in_specs,
        out_specs=pl.BlockSpec((tm, d), lambda i: (i, 0)),
        out_shape=jax.ShapeDtypeStruct((b * l, d), F32),
        compiler_params=_params("parallel"),
        name=name,
    )(*args)
    return out.reshape(b, l, d)


HGRN_CHUNK = 32
HGRN_SAFE_MIN = 1e-30
HGRN_Q_HEADROOM = 1e37


def _hgrn_gates(z, lb):
    f = lb + (1.0 - lb) * jax.nn.sigmoid(z)
    return jnp.maximum(f, F_MIN), 1.0 - f


def _hgrn_chunk_prep(q, v, z, lb, reverse):
    c = q.shape[0]
    half = c // 2
    fm, kk = _hgrn_gates(z, lb)
    pos = lax.broadcasted_iota(jnp.int32, (c, LANES), 0)
    if reverse:
        pos = (c - 1) - pos
    second = pos >= half
    hpos = jnp.where(second, pos - half, pos)
    ph = fm
    j = 1
    while j < half:
        ph = ph * jnp.where(hpos >= j, pltpu.roll(ph, (c - j) if reverse else j, 0), 1.0)
        j *= 2
    a_last = ph[half:half + 1] if reverse else ph[half - 1:half]
    r_last = ph[0:1] if reverse else ph[c - 1:c]
    p_mid = a_last
    p_last = a_last * r_last
    pp = jnp.where(second, ph * a_last, ph)
    r = jnp.where(second, ph, ph * (1.0 / a_last))
    kd = kk * (1.0 / r)
    q_max = jnp.max(jnp.abs(q), axis=0, keepdims=True)
    ok = jnp.where(p_mid >= HGRN_SAFE_MIN, r_last, 0.0) >= HGRN_SAFE_MIN
    ok = jnp.where(ok, p_mid * HGRN_Q_HEADROOM, -1.0) >= q_max
    return dict(qr=(q * r).astype(BF16), kd=kd.astype(BF16), qp=(q * pp).astype(BF16),
                kl=(kd * r_last).astype(BF16), vb=v.astype(BF16), dec=p_last, bad=jnp.where(ok, 0.0, 1.0))


def _hgrn_chunk_dots(p, st):
    sc = lax.dot_general(p["qr"], p["kd"], NT_DIMS, preferred_element_type=F32)
    o_st = lax.dot_general(p["qp"], st.astype(BF16), NT_DIMS, preferred_element_type=F32)
    u = lax.dot_general(p["vb"], p["kl"], TN_DIMS, preferred_element_type=F32)
    return sc, o_st, u


def _hgrn_chunk_out(p, sc, o_st, reverse):
    c = sc.shape[0]
    row = lax.broadcasted_iota(jnp.int32, (c, c), 0)
    col = lax.broadcasted_iota(jnp.int32, (c, c), 1)
    sc = jnp.where((col >= row) if reverse else (col <= row), sc, 0.0)
    return jnp.dot(sc.astype(BF16), p["vb"], preferred_element_type=F32) + o_st


def _hgrn_exact_block(q_ref, v_ref, z_ref, lb_ref, o_scr, st_scr, dr, reverse, n_tiles):
    pos = lax.broadcasted_iota(jnp.int32, (SUBLANES, LANES), 0)
    if reverse:
        pos = (SUBLANES - 1) - pos

    def earlier(x, j):
        if j == 0:
            return x
        return pltpu.roll(x, (SUBLANES - j) if reverse else j, 0)

    def later(x, j):
        return pltpu.roll(x, j if reverse else (SUBLANES - j), 0)

    def tile(i, carry):
        ti = (n_tiles - 1 - i) if reverse else i
        r0 = pl.multiple_of(ti * SUBLANES, SUBLANES)
        for h in range(A_HEADS):
            sl = slice(h * LANES, (h + 1) * LANES)
            lb = lb_ref[:, sl]
            z = z_ref[0, pl.ds(r0, SUBLANES), sl]
            q = q_ref[0, pl.ds(r0, SUBLANES), sl]
            v = v_ref[0, pl.ds(r0, SUBLANES), sl]
            fm, kk = _hgrn_gates(z, lb)
            pp = fm
            for j in (1, 2, 4):
                pp = pp * jnp.where(pos >= j, earlier(pp, j), 1.0)
            qq = jnp.where(pos <= SUBLANES - 2, later(fm, 1), 1.0)
            for j in (1, 2, 4):
                qq = qq * jnp.where(pos <= SUBLANES - 1 - j, later(qq, j), 1.0)
            dec = pp[0:1] if reverse else pp[SUBLANES - 1:SUBLANES]
            st = st_scr[dr, h]
            o = lax.dot_general((q * pp).astype(BF16), st.astype(BF16), NT_DIMS,
                                preferred_element_type=F32)
            g = fm
            for d in range(SUBLANES):
                if d == 0:
                    e = q * kk
                else:
                    if d > 1:
                        g = g * earlier(fm, d - 1)
                    e = q * g * jnp.where(pos >= d, earlier(kk, d), 0.0)
                o = o + jnp.sum(e, axis=-1, keepdims=True) * earlier(v, d)
            o_scr[pl.ds(r0, SUBLANES), sl] = o
            u = lax.dot_general(v.astype(BF16), (kk * qq).astype(BF16), TN_DIMS,
                                preferred_element_type=F32)
            st_scr[dr, h] = dec * st + u
        return carry

    lax.fori_loop(0, n_tiles, tile, 0)


def _hgrn_scan_body(qf_ref, vf_ref, zf_ref, qb_ref, vb_ref, zb_ref, lb_ref, s0_ref, of_ref, ob_ref, sfin_ref,
                    st_scr, save_scr, ox_scr, *, n_chunks):
    step = pl.program_id(1)

    @pl.when(step == 0)
    def _():
        st_scr[...] = s0_ref[0]

    save_scr[...] = st_scr[...]
    dirs = ((qf_ref, vf_ref, zf_ref, of_ref, False), (qb_ref, vb_ref, zb_ref, ob_ref, True))
    c = HGRN_CHUNK

    def chunk(i, bad):
        work, dots = [], []
        for dr, (q_ref, v_ref, z_ref, o_ref, reverse) in enumerate(dirs):
            rows = pl.ds(pl.multiple_of(((n_chunks - 1 - i) if reverse else i) * c, c), c)
            group = []
            for h in range(A_HEADS):
                sl = slice(h * LANES, (h + 1) * LANES)
                p = _hgrn_chunk_prep(q_ref[0, rows, sl], v_ref[0, rows, sl], z_ref[0, rows, sl], lb_ref[:, sl],
                                     reverse)
                bad = jnp.maximum(bad, p["bad"])
                group.append((dr, h, o_ref, rows, sl, reverse, p))
            dots += [_hgrn_chunk_dots(p, st_scr[dr, h]) for dr, h, _, _, _, _, p in group]
            work += group
        for (dr, h, _, _, _, _, p), (_, _, u) in zip(work, dots):
            st_scr[dr, h] = p["dec"] * st_scr[dr, h] + u
        for (dr, h, o_ref, rows, sl, reverse, p), (sc, o_st, _) in zip(work, dots):
            o_ref[0, rows, sl] = _hgrn_chunk_out(p, sc, o_st, reverse).astype(o_ref.dtype)
        return bad

    bad = lax.fori_loop(0, n_chunks, chunk, jnp.zeros((1, LANES), F32))

    @pl.when(jnp.max(bad) > 0.0)
    def _():
        st_scr[...] = save_scr[...]
        for dr, (q_ref, v_ref, z_ref, o_ref, reverse) in enumerate(dirs):
            _hgrn_exact_block(q_ref, v_ref, z_ref, lb_ref, ox_scr, st_scr, dr, reverse, n_chunks * c // SUBLANES)
            o_ref[0] = ox_scr[...].astype(o_ref.dtype)

    @pl.when(step == pl.num_programs(1) - 1)
    def _():
        sfin_ref[0] = st_scr[...]


def _hgrn_scan(proj, lb, s0):
    b, l, _ = proj.shape
    d = A_HEADS * LANES
    t = _row_tile(l, 512)
    nb = l // t
    fwd = lambda blk: pl.BlockSpec((1, t, d), lambda i, s: (i, s, blk))
    bwd = lambda blk: pl.BlockSpec((1, t, d), lambda i, s: (i, nb - 1 - s, blk))
    st_spec = pl.BlockSpec((1,) + s0.shape[1:], lambda i, s: (i, 0, 0, 0, 0))
    return pl.pallas_call(
        functools.partial(_hgrn_scan_body, n_chunks=t // HGRN_CHUNK),
        grid=(b, nb),
        in_specs=[fwd(0), fwd(1), fwd(3), bwd(0), bwd(1), bwd(4),
                  pl.BlockSpec((1, d), lambda i, s: (0, 0)), st_spec],
        out_specs=[fwd(0), bwd(0), st_spec],
        out_shape=[jax.ShapeDtypeStruct((b, l, d), BF16), jax.ShapeDtypeStruct((b, l, d), BF16),
                   jax.ShapeDtypeStruct(s0.shape, F32)],
        scratch_shapes=[pltpu.VMEM(s0.shape[1:], F32), pltpu.VMEM(s0.shape[1:], F32), pltpu.VMEM((t, d), F32)],
        compiler_params=_params("parallel", "arbitrary"),
        name="hgrn_scan",
    )(proj, proj, proj, proj, proj, proj, lb, s0)


def _hgrn2_mixer(x_c, x_l, ng, mod_c, mod_l, w_in, lower, out_g, w_out, g1, want_ctx):
    d = x_l.shape[-1]
    p_c = _norm_mod_proj(x_c, ng, mod_c[0], mod_c[1], w_in, F32)
    p_l = _norm_mod_proj(x_l, ng, mod_l[0], mod_l[1], w_in, F32)
    lb = lower.reshape(1, d)
    zero = jnp.zeros((x_l.shape[0], 2, A_HEADS, LANES, LANES), F32)
    ocf, ocb, s_ctx = _hgrn_scan(p_c, lb, zero)
    olf, olb, _ = _hgrn_scan(p_l, lb, s_ctx)
    o_c, o_l = (ocf, ocb), (olf, olb)
    og = out_g.reshape(1, d)
    new_l = _out_call(_out_hgrn_body, [(o_l[0], 0, d), (o_l[1], 0, d), (p_l, 2, d)], [og, w_out],
                      x_l, mod_l[2], g1, "hgrn_out")
    new_c = None
    if want_ctx:
        new_c = _out_call(_out_hgrn_body, [(o_c[0], 0, d), (o_c[1], 0, d), (p_c, 2, d)], [og, w_out],
                          x_c, mod_c[2], g1, "hgrn_out")
    return new_c, new_l


def _softmax_pv(s_list, v_list):
    mx = functools.reduce(jnp.maximum, [jnp.max(s, axis=-1, keepdims=True) for s in s_list])
    ps = [jnp.exp(s - mx) for s in s_list]
    den = functools.reduce(jnp.add, [jnp.sum(p, axis=-1, keepdims=True) for p in ps])
    acc = functools.reduce(jnp.add, [jnp.dot(p.astype(BF16), v, preferred_element_type=F32)
                                     for p, v in zip(ps, v_list)])
    return acc / den


def _na_lat_body(q_ref, k_ref, v_ref, kc_ref, vc_ref, t2_ref, o_ref, *, rows, kh, scale):
    w = GRID_W
    rb = NA_ROW_BLOCK
    nk = kh + rb
    n_rel = 2 * kh - 1
    first = lax.broadcasted_iota(jnp.int32, (rb * w, LANES), 1) < (LANES // 2)
    kc = kc_ref[0]
    vc = vc_ref[0]

    def bias_index(r, ks, m):
        r0 = jnp.clip(r - kh // 2, 0, rows - kh)
        ka = ks + 2 * m
        rel_a = ka - r + (kh - 1)
        in_a = (ka >= r0) & (ka < r0 + kh)
        in_b = (ka + 1 >= r0) & (ka + 1 < r0 + kh)
        both, only_b, only_a = rel_a, (n_rel - 1) + rel_a + 1, (2 * n_rel - 1) + rel_a
        return jnp.where(in_a, jnp.where(in_b, both, only_a), jnp.where(in_b, only_b, 3 * n_rel - 1))

    def block(i, carry):
        rq = i * rb
        ks = jnp.clip(rq - kh // 2, 0, rows - nk)
        q2 = q_ref[0, pl.ds(pl.multiple_of(rq * w, rb * w), rb * w), :] * scale
        k2 = k_ref[0, pl.ds(pl.multiple_of(ks * w, w), nk * w), :]
        v2 = v_ref[0, pl.ds(pl.multiple_of(ks * w, w), nk * w), :]
        outs = []
        for hh in range(2):
            qm = jnp.where(first if hh == 0 else jnp.logical_not(first), q2, jnp.zeros_like(q2))
            bias = jnp.concatenate(
                [jnp.concatenate([t2_ref[hh, bias_index(rq + a, ks, m)] for m in range(nk // 2)], axis=-1)
                 for a in range(rb)], axis=0)
            s_w = lax.dot_general(qm, k2, NT_DIMS, preferred_element_type=F32) + bias
            s_c = lax.dot_general(qm, kc, NT_DIMS, preferred_element_type=F32)
            outs.append(_softmax_pv([s_w, s_c], [v2, vc]))
        o_ref[0, pl.ds(pl.multiple_of(rq * w, rb * w), rb * w), :] = (
            jnp.where(first, outs[0], outs[1]).astype(o_ref.dtype))
        return carry

    lax.fori_loop(0, rows // rb, block, 0, unroll=2)


def _na_ctx_body(q_ref, k_ref, v_ref, o_ref, *, scale):
    n = q_ref.shape[1]
    first = lax.broadcasted_iota(jnp.int32, (n, LANES), 1) < (LANES // 2)
    q2 = q_ref[0] * scale
    k2 = k_ref[0]
    v2 = v_ref[0]
    outs = []
    for hh in range(2):
        qm = jnp.where(first if hh == 0 else jnp.logical_not(first), q2, jnp.zeros_like(q2))
        s = lax.dot_general(qm, k2, NT_DIMS, preferred_element_type=F32)
        outs.append(_softmax_pv([s], [v2]))
    o_ref[0] = jnp.where(first, outs[0], outs[1]).astype(o_ref.dtype)


def _na_bias_table(rpb, kh, kw):
    w = jnp.arange(GRID_W)[:, None]
    c = jnp.arange(GRID_W)[None, :]
    c0 = jnp.clip(w - kw // 2, 0, GRID_W - kw)
    inside = (c >= c0) & (c < c0 + kw)
    pad = GRID_W - kw
    padded = jnp.pad(rpb.astype(F32), ((0, 0), (0, 0), (pad, pad)))
    shifted = jnp.stack([padded[:, :, pad + kw - 1 - q:pad + kw - 1 - q + GRID_W] for q in range(GRID_W)], axis=2)
    t = jnp.where(inside[None, None], shifted, NEG_BIG)
    off = jnp.full_like(t, NEG_BIG)
    pair = lambda a, b: jnp.concatenate([a, b], axis=-1)
    return jnp.concatenate([pair(t[:, :-1], t[:, 1:]), pair(off, t), pair(t, off), pair(off, off)[:, :1]], axis=1)


def _na_mixer(x_c, x_l, ng, mod_c, mod_l, w_qkv, rpb, w_out, g1, want_ctx):
    b, l, d = x_l.shape
    lc = x_c.shape[1]
    heads = rpb.shape[0]
    na_rows, na_cols = (rpb.shape[1] + 1) // 2, (rpb.shape[2] + 1) // 2
    dh = d // heads
    assert 2 * dh == LANES and l % GRID_W == 0
    rows = l // GRID_W
    kh = min(na_rows, rows)
    assert kh == na_rows and (kh + NA_ROW_BLOCK) % 2 == 0
    assert rows % NA_ROW_BLOCK == 0 and rows >= kh + NA_ROW_BLOCK
    scale = dh ** -0.5
    nhp = heads // 2
    qkv_c = _norm_mod_proj(x_c, ng, mod_c[0], mod_c[1], w_qkv, BF16)
    qkv_l = _norm_mod_proj(x_l, ng, mod_l[0], mod_l[1], w_qkv, BF16)
    t2 = _na_bias_table(rpb, kh, na_cols)
    o_l = pl.pallas_call(
        functools.partial(_na_lat_body, rows=rows, kh=kh, scale=scale),
        grid=(b, nhp),
        in_specs=[pl.BlockSpec((1, l, LANES), lambda i, p: (i, 0, p)),
                  pl.BlockSpec((1, l, LANES), lambda i, p: (i, 0, nhp + p)),
                  pl.BlockSpec((1, l, LANES), lambda i, p: (i, 0, 2 * nhp + p)),
                  pl.BlockSpec((1, lc, LANES), lambda i, p: (i, 0, nhp + p)),
                  pl.BlockSpec((1, lc, LANES), lambda i, p: (i, 0, 2 * nhp + p)),
                  pl.BlockSpec((2,) + t2.shape[1:], lambda i, p: (p, 0, 0, 0))],
        out_specs=pl.BlockSpec((1, l, LANES), lambda i, p: (i, 0, p)),
        out_shape=jax.ShapeDtypeStruct((b, l, d), BF16),
        compiler_params=_params("parallel", "parallel"),
        name="na_latent",
    )(qkv_l, qkv_l, qkv_l, qkv_c, qkv_c, t2)
    new_l = _out_call(_out_plain_body, [(o_l, 0, d)], [w_out], x_l, mod_l[2], g1, "na_out")
    new_c = None
    if want_ctx:
        o_c = pl.pallas_call(
            functools.partial(_na_ctx_body, scale=scale),
            grid=(b, nhp),
            in_specs=[pl.BlockSpec((1, lc, LANES), lambda i, p: (i, 0, p)),
                      pl.BlockSpec((1, lc, LANES), lambda i, p: (i, 0, nhp + p)),
                      pl.BlockSpec((1, lc, LANES), lambda i, p: (i, 0, 2 * nhp + p))],
            out_specs=pl.BlockSpec((1, lc, LANES), lambda i, p: (i, 0, p)),
            out_shape=jax.ShapeDtypeStruct((b, lc, d), BF16),
            compiler_params=_params("parallel", "parallel"),
            name="na_context",
        )(qkv_c, qkv_c, qkv_c)
        new_c = _out_call(_out_plain_body, [(o_c, 0, d)], [w_out], x_c, mod_c[2], g1, "na_out")
    return new_c, new_l


S5_GROUP_BLOCK = 8


def _s5_kernel_body(c_ref, w_ref, o_ref):
    for i in range(c_ref.shape[0]):
        o_ref[i] = jnp.dot(c_ref[i], w_ref[i], preferred_element_type=F32, precision=lax.Precision.HIGHEST)


def _s5_impulse(cmat, wmat):
    n = cmat.shape[0]
    gb = S5_GROUP_BLOCK
    return pl.pallas_call(
        _s5_kernel_body,
        grid=(n // gb,),
        in_specs=[pl.BlockSpec((gb,) + cmat.shape[1:], lambda i: (i, 0, 0)),
                  pl.BlockSpec((gb,) + wmat.shape[1:], lambda i: (i, 0, 0))],
        out_specs=pl.BlockSpec((gb, cmat.shape[1], wmat.shape[2]), lambda i: (i, 0, 0)),
        out_shape=jax.ShapeDtypeStruct((n, cmat.shape[1], wmat.shape[2]), F32),
        compiler_params=_params("parallel"),
        name="s5_impulse",
    )(cmat, wmat)


def _s5_operators(lam_re, lam_im, log_dt, b_re, b_im, c_re, c_im, t):
    _, g, p = lam_re.shape
    cg = b_re.shape[-1]
    lam = lax.complex(lam_re.astype(F32), lam_im.astype(F32))
    ldt = lam * jnp.exp(log_dt.astype(F32))[..., None]
    a = jnp.exp(ldt)
    bbar = ((a - 1.0) / lam)[..., None] * lax.complex(b_re.astype(F32), b_im.astype(F32))
    cm = lax.complex(c_re.astype(F32), c_im.astype(F32))
    apow = jnp.exp(ldt[..., None] * jnp.arange(t + 1, dtype=F32))
    w = apow[..., :t, None] * bbar[:, :, :, None, :]
    wmat = jnp.concatenate([jnp.real(w), jnp.imag(w)], axis=2).reshape(2 * g, 2 * p, t * cg)
    cmat = jnp.concatenate([jnp.real(cm), -jnp.imag(cm)], axis=-1).reshape(2 * g, cg, 2 * p)
    k = _s5_impulse(cmat, wmat).reshape(2, g, cg, t, cg)
    k = jnp.transpose(k, (0, 1, 3, 4, 2))
    kfull = jnp.concatenate([k[1, :, :0:-1], (k[0, :, :1] + k[1, :, :1]), k[0, :, 1:]], axis=1)
    lagvec = jnp.transpose(kfull, (0, 2, 1, 3)).reshape(g, cg, (2 * t - 1) * cg).astype(BF16)
    toep = jnp.stack([lagvec[:, :, (t - 1 - s) * cg:(2 * t - 1 - s) * cg] for s in range(t)], axis=1)
    toep = toep.reshape(g, t * cg, t * cg)

    def state_in(wd, flip):
        wd = wd[:, :, ::-1] if flip else wd
        m = jnp.transpose(wd, (0, 2, 3, 1)).reshape(g, t * cg, p)
        return jnp.concatenate([jnp.real(m), jnp.imag(m)], axis=-1)

    def state_out(cd, pw):
        n = cd[:, :, :, None] * pw[:, None, :, :]
        n = jnp.transpose(n, (0, 2, 3, 1)).reshape(g, p, t * cg)
        return jnp.concatenate([jnp.real(n), -jnp.imag(n)], axis=1)

    m_f = state_in(w[0], True)
    m_b = state_in(w[1], False)
    n_f = state_out(cm[0], apow[0][..., 1:])
    n_b = state_out(cm[1], apow[1][..., :0:-1])
    at = apow[..., t]
    a1 = jnp.concatenate([jnp.real(at), jnp.real(at)], axis=-1)
    a2 = jnp.concatenate([-jnp.imag(at), jnp.imag(at)], axis=-1)
    bf = lambda m: m.astype(BF16)
    return dict(toep=bf(toep), m_f=bf(m_f), m_b=bf(m_b), n_f=bf(n_f), n_b=bf(n_b), a1=a1, a2=a2)


def _s5_body(u_ref, dsk_ref, toep_ref, mf_ref, mb_ref, nf_ref, nb_ref, a1_ref, a2_ref, x0_ref,
             y_ref, xfin_ref, z_scr, zs_scr, xin_scr):
    gb, nc = u_ref.shape[1], u_ref.shape[2]
    half = LANES // 2
    ubs = [u_ref[0, gi].astype(BF16) for gi in range(gb)]
    for dr, m_ref in enumerate((mf_ref, mb_ref)):
        z = jnp.stack([jnp.dot(ubs[gi], m_ref[gi], preferred_element_type=F32) for gi in range(gb)])
        z_scr[dr] = jnp.swapaxes(z, 0, 1)

    swap = lambda t: pltpu.roll(t, half, t.ndim - 1)
    for dr in range(2):
        zs_scr[dr] = swap(z_scr[dr].reshape(nc * gb, LANES)).reshape(nc, gb, LANES)

    coef = [(a1_ref[dr], a2_ref[dr]) for dr in range(2)]

    def scan(j, xs):
        out = []
        for dr in range(2):
            jj = j if dr == 0 else nc - 1 - j
            x, xw = xs[2 * dr], xs[2 * dr + 1]
            a1, a2 = coef[dr]
            xin_scr[dr, jj] = x
            out.append(a1 * x + a2 * xw + z_scr[dr, jj])
            out.append(a1 * xw - a2 * x + zs_scr[dr, jj])
        return tuple(out)

    x0f, x0b = x0_ref[0, 0], x0_ref[0, 1]
    xf, _, xb, _ = lax.fori_loop(0, nc, scan, (x0f, swap(x0f), x0b, swap(x0b)))
    xfin_ref[0, 0] = xf
    xfin_ref[0, 1] = xb
    xin = [jnp.swapaxes(xin_scr[dr], 0, 1).astype(BF16) for dr in range(2)]
    for gi in range(gb):
        u = u_ref[0, gi]
        y = jnp.dot(ubs[gi], toep_ref[gi], preferred_element_type=F32) + u * dsk_ref[gi]
        y = y + jnp.dot(xin[0][gi], nf_ref[gi], preferred_element_type=F32)
        y = y + jnp.dot(xin[1][gi], nb_ref[gi], preferred_element_type=F32)
        y_ref[0, gi] = y


def _piece_transpose(src, piece_of, cg):
    n = len(src)
    out = [None] * n
    for s in range(n):
        c = src[s % n]
        for p in range(1, n):
            c = jnp.where(piece_of == p, src[(p + s) % n], c)
        r = c if s == 0 else pltpu.roll(c, s * cg, 1)
        for p in range(n):
            out[p] = r if s == 0 else jnp.where(piece_of == (p + s) % n, r, out[p])
    return out


def _to_chunks(h_scr, o_ref, t, cg):
    ncb = h_scr.shape[1] // t
    pieces = LANES // cg
    piece_of = lax.broadcasted_iota(jnp.int32, (ncb, LANES), 1) // cg

    def lane_block(lb, carry):
        for tq in range(t // pieces):
            steps = [h_scr[lb, pl.ds(tq * pieces + tr, ncb, stride=t), :] for tr in range(pieces)]
            for gl, col in enumerate(_piece_transpose(steps, piece_of, cg)):
                o_ref[0, lb * pieces + gl, :, tq * LANES:(tq + 1) * LANES] = col
        return carry

    lax.fori_loop(0, h_scr.shape[0], lane_block, 0)


def _from_chunks(y_ref, y_scr, t, cg):
    ncb = y_ref.shape[2]
    pieces = LANES // cg
    piece_of = lax.broadcasted_iota(jnp.int32, (ncb, LANES), 1) // cg

    def lane_block(lb, carry):
        for tq in range(t // pieces):
            cols = [y_ref[0, lb * pieces + gl, :, tq * LANES:(tq + 1) * LANES] for gl in range(pieces)]
            for tr, step in enumerate(_piece_transpose(cols, piece_of, cg)):
                y_scr[lb, pl.ds(tq * pieces + tr, ncb, stride=t), :] = step
        return carry

    lax.fori_loop(0, y_scr.shape[0], lane_block, 0)


def _norm_chunks_body(x_ref, g_ref, sh_ref, sc_ref, o_ref, h_scr, *, t, cg):
    h = _norm_mod(x_ref[...], g_ref[...], sh_ref[0], sc_ref[0])
    for lb in range(h_scr.shape[0]):
        h_scr[lb] = h[:, lb * LANES:(lb + 1) * LANES]
    _to_chunks(h_scr, o_ref, t, cg)


def _norm_mod_chunks(x, g, sh, sc, groups, t):
    b, l, d = x.shape
    cg = d // groups
    tm = _row_tile(l, 512)
    per = l // tm
    assert tm % t == 0 and (tm // t) % SUBLANES == 0 and LANES % cg == 0 and t % (LANES // cg) == 0
    return pl.pallas_call(
        functools.partial(_norm_chunks_body, t=t, cg=cg),
        grid=(b * per,),
        in_specs=[pl.BlockSpec((tm, d), lambda i: (i, 0)),
                  pl.BlockSpec((1, d), lambda i: (0, 0)),
                  _mod_spec(sh, per), _mod_spec(sc, per)],
        out_specs=pl.BlockSpec((1, groups, tm // t, t * cg), lambda i: (i // per, 0, i % per, 0)),
        out_shape=jax.ShapeDtypeStruct((b, groups, l // t, t * cg), F32),
        scratch_shapes=[pltpu.VMEM((d // LANES, tm, LANES), F32)],
        compiler_params=_params("parallel"),
        name="norm_mod_chunks",
    )(x.reshape(b * l, d), g.reshape(1, d), sh.table, sc.table)


def _glu_chunks_body(y_ref, w_ref, res_ref, gate_ref, g1_ref, o_ref, y_scr, *, t, cg):
    _from_chunks(y_ref, y_scr, t, cg)
    d = o_ref.shape[-1]
    y = jnp.concatenate([y_scr[lb] for lb in range(y_scr.shape[0])], axis=-1)
    a = _gelu_tanh(y).astype(BF16)
    ag = jnp.dot(a, w_ref[...], preferred_element_type=F32)
    o_ref[...] = _residual(res_ref, gate_ref, g1_ref, ag[:, :d] * jax.nn.sigmoid(ag[:, d:]))


def _glu_out_chunks(y, w_glu, res, gate, g1, t):
    b, l, d = res.shape
    groups = y.shape[1]
    cg = d // groups
    tm = _row_tile(l, 512)
    per = l // tm
    out = pl.pallas_call(
        functools.partial(_glu_chunks_body, t=t, cg=cg),
        grid=(b * per,),
        in_specs=[pl.BlockSpec((1, groups, tm // t, t * cg), lambda i: (i // per, 0, i % per, 0)),
                  pl.BlockSpec(w_glu.shape, lambda i: (0, 0)),
                  pl.BlockSpec((tm, d), lambda i: (i, 0)),
                  _mod_spec(gate, per),
                  pl.BlockSpec((1, d), lambda i: (0, 0))],
        out_specs=pl.BlockSpec((tm, d), lambda i: (i, 0)),
        out_shape=jax.ShapeDtypeStruct((b * l, d), F32),
        scratch_shapes=[pltpu.VMEM((d // LANES, tm, LANES), F32)],
        compiler_params=_params("parallel"),
        name="s5_glu_out",
    )(y, w_glu, res.reshape(b * l, d), gate.table, g1.reshape(1, d))
    return out.reshape(b, l, d)


def _s5_scan(u, ops, dsk, x0):
    b, g, nc, tc = u.shape
    gb = S5_GROUP_BLOCK
    per_g = lambda shape: pl.BlockSpec((gb,) + shape, lambda j, i: (j,) + (0,) * len(shape))
    y, xfin = pl.pallas_call(
        _s5_body,
        grid=(g // gb, b),
        in_specs=[pl.BlockSpec((1, gb, nc, tc), lambda j, i: (i, j, 0, 0)),
                  per_g((1, tc)), per_g((tc, tc)), per_g((tc, LANES)), per_g((tc, LANES)),
                  per_g((LANES, tc)), per_g((LANES, tc)),
                  pl.BlockSpec((2, gb, LANES), lambda j, i: (0, j, 0)),
                  pl.BlockSpec((2, gb, LANES), lambda j, i: (0, j, 0)),
                  pl.BlockSpec((1, 2, gb, LANES), lambda j, i: (i, 0, j, 0))],
        out_specs=[pl.BlockSpec((1, gb, nc, tc), lambda j, i: (i, j, 0, 0)),
                   pl.BlockSpec((1, 2, gb, LANES), lambda j, i: (i, 0, j, 0))],
        out_shape=[jax.ShapeDtypeStruct((b, g, nc, tc), F32), jax.ShapeDtypeStruct((b, 2, g, LANES), F32)],
        scratch_shapes=[pltpu.VMEM((2, nc, gb, LANES), F32)] * 3,
        compiler_params=_params("parallel", "parallel"),
        name="s5_scan",
    )(u, dsk, ops["toep"], ops["m_f"], ops["m_b"], ops["n_f"], ops["n_b"], ops["a1"], ops["a2"], x0)
    return y, xfin


def _s5_mixer(x_c, x_l, ng, mod_c, mod_l, lam_re, lam_im, log_dt, b_re, b_im, c_re, c_im, d_skip, w_glu, g1,
              want_ctx):
    b, l, d = x_l.shape
    g, p = lam_re.shape[1], lam_re.shape[2]
    cg = d // g
    assert 2 * p == LANES and g % S5_GROUP_BLOCK == 0
    ops = _s5_operators(lam_re, lam_im, log_dt, b_re, b_im, c_re, c_im, S5_CHUNK)
    dsk = jnp.tile(d_skip.astype(F32).reshape(g, 1, cg), (1, 1, S5_CHUNK))
    u_c = _norm_mod_chunks(x_c, ng, mod_c[0], mod_c[1], g, S5_CHUNK)
    u_l = _norm_mod_chunks(x_l, ng, mod_l[0], mod_l[1], g, S5_CHUNK)
    y_c, x_ctx = _s5_scan(u_c, ops, dsk, jnp.zeros((b, 2, g, LANES), F32))
    y_l, _ = _s5_scan(u_l, ops, dsk, x_ctx)
    new_l = _glu_out_chunks(y_l, w_glu, x_l, mod_l[2], g1, S5_CHUNK)
    new_c = _glu_out_chunks(y_c, w_glu, x_c, mod_c[2], g1, S5_CHUNK) if want_ctx else None
    return new_c, new_l


FFN_HALO = 16
FFN_TN = 256
FFN_TILES_PER_STEP = 11
FFN_ROW_SPLIT = 256


def _ffn_body(xp_ref, x_ref, xn_ref, g2_ref, sh_ref, sc_ref, *rest, per, n_tiles, tps):
    w_refs = rest[:2 * tps]
    cw_ref, cb_ref, wo_ref, gate_ref, g3_ref, o_ref, h_scr, gat_scr = rest[2 * tps:]
    i, j = pl.program_id(0), pl.program_id(1)
    tm = x_ref.shape[0]
    hl = FFN_HALO
    f = wo_ref.shape[0]
    n_steps = pl.cdiv(n_tiles, tps)
    when = (lambda cond: lambda fn: fn()) if n_steps == 1 else pl.when

    @when(j == 0)
    def _():
        g, sh, sc = g2_ref[...], sh_ref[0], sc_ref[0]
        keep_p = ((i % per) != 0).astype(F32)
        keep_n = ((i % per) != per - 1).astype(F32)
        h_scr[0:hl] = (_norm_mod(xp_ref[...], g, sh, sc) * keep_p).astype(BF16)
        h_scr[hl:hl + tm] = _norm_mod(x_ref[...], g, sh, sc).astype(BF16)
        h_scr[hl + tm:] = (_norm_mod(xn_ref[...], g, sh, sc) * keep_n).astype(BF16)

    sub = FFN_ROW_SPLIT if tm % FFN_ROW_SPLIT == 0 else tm
    ns = tm // sub
    edge = SUBLANES
    tn = w_refs[0].shape[1]

    def pieces(w_ref):
        out = []
        for s in range(ns):
            lo = hl + s * sub - (hl if s == 0 else 0)
            hi = hl + (s + 1) * sub + (hl if s == ns - 1 else 0)
            out.append(jnp.dot(h_scr[lo:hi], w_ref[...], preferred_element_type=F32))
        return out

    def conv(p, s, cw, cb):
        base = hl if s == 0 else 0
        left = p[s][hl - edge:hl] if s == 0 else p[s - 1][-edge:]
        right = p[s][base + sub:base + sub + edge] if s == ns - 1 else p[s + 1][:edge]
        ext = jnp.concatenate([left, p[s][base:base + sub], right], axis=0)
        um = pltpu.roll(ext, 1, 0)[edge:edge + sub]
        up = pltpu.roll(ext, sub + 2 * edge - 1, 0)[edge:edge + sub]
        return cb + um * cw[0:1] + ext[edge:edge + sub] * cw[1:2] + up * cw[2:3]

    def tile(k):
        col = k * tn if n_steps == 1 else pl.multiple_of((j * tps + k) * tn, tn)
        cols_a, cols_v = pl.ds(col, tn), pl.ds(pl.multiple_of(f + col, tn), tn)
        pa, pv = pieces(w_refs[2 * k]), pieces(w_refs[2 * k + 1])
        for s in range(ns):
            a = conv(pa, s, cw_ref[:, cols_a], cb_ref[:, cols_a])
            v = conv(pv, s, cw_ref[:, cols_v], cb_ref[:, cols_v])
            gat_scr[s * sub:(s + 1) * sub, cols_a] = (a * jax.nn.sigmoid(a) * v).astype(BF16)

    always = n_tiles - tps * (n_steps - 1)
    for k in range(tps):
        if k < always:
            tile(k)
        else:
            pl.when(j * tps + k < n_tiles)(functools.partial(tile, k))

    @when(j == n_steps - 1)
    def _():
        y = jnp.dot(gat_scr[...], wo_ref[...], preferred_element_type=F32)
        o_ref[...] = x_ref[...] + gate_ref[0] * _rms(y, g3_ref[...])


def _conv_ffn_block(x, ng2, mod, w_in, conv_w, conv_b, w_out, ng3):
    b, l, d = x.shape
    f = w_out.shape[0]
    tn, hl = FFN_TN, FFN_HALO
    assert f % tn == 0 and conv_w.shape[0] == 3 and w_in.shape[1] == 2 * f
    nj = f // tn
    tm = _row_tile(l, 1024)
    per = l // tm
    hb = tm // hl
    last_hb = b * l // hl - 1
    cb = conv_b.reshape(1, 2 * f)
    x2 = x.reshape(b * l, d)
    tps = min(FFN_TILES_PER_STEP, nj)
    once = dict(pipeline_mode=pl.Buffered(1)) if tps == nj else {}
    w_specs = []
    for k in range(tps):
        w_specs.append(pl.BlockSpec((d, tn), lambda i, j, k=k: (0, jnp.minimum(j * tps + k, nj - 1)), **once))
        w_specs.append(pl.BlockSpec((d, tn), lambda i, j, k=k: (0, nj + jnp.minimum(j * tps + k, nj - 1)), **once))
    whole = lambda arr: pl.BlockSpec(arr.shape, lambda i, j: (0,) * arr.ndim)
    out = pl.pallas_call(
        functools.partial(_ffn_body, per=per, n_tiles=nj, tps=tps),
        grid=(b * per, pl.cdiv(nj, tps)),
        in_specs=[pl.BlockSpec((hl, d), lambda i, j: (jnp.maximum(i * hb - 1, 0), 0)),
                  pl.BlockSpec((tm, d), lambda i, j: (i, 0)),
                  pl.BlockSpec((hl, d), lambda i, j: (jnp.minimum((i + 1) * hb, last_hb), 0)),
                  pl.BlockSpec((1, d), lambda i, j: (0, 0)),
                  _mod_spec(mod[3], per), _mod_spec(mod[4], per),
                  *w_specs, whole(conv_w), whole(cb),
                  pl.BlockSpec((f, d), lambda i, j: (0, 0), pipeline_mode=pl.Buffered(1)),
                  _mod_spec(mod[5], per),
                  pl.BlockSpec((1, d), lambda i, j: (0, 0))],
        out_specs=pl.BlockSpec((tm, d), lambda i, j: (i, 0)),
        out_shape=jax.ShapeDtypeStruct((b * l, d), F32),
        scratch_shapes=[pltpu.VMEM((tm + 2 * hl, d), BF16), pltpu.VMEM((tm, f), BF16)],
        compiler_params=_params("parallel", "arbitrary"),
        name="conv_ffn",
    )(x2, x2, x2, ng2.reshape(1, d), mod[3].table, mod[4].table, *([w_in] * (2 * tps)), conv_w, cb, w_out,
      mod[5].table, ng3.reshape(1, d))
    return out.reshape(b, l, d)


def kernel(x, c, ctx, c_ctx, w_mod, b_mod, norm_g, a_w_in, a_lower_logits, a_out_g, a_w_out, b_w_qkv, b_rpb,
           b_w_out, c_lam_re, c_lam_im, c_log_dt, c_b_re, c_b_im, c_c_re, c_c_im, c_d, c_w_glu, f_w_in,
           f_conv_w, f_conv_b, f_w_out):
    bsz, _, d = x.shape
    depth = w_mod.shape[0]
    p = jax.nn.softmax(a_lower_logits.astype(F32), axis=0)
    lower = jnp.cumsum(p, axis=0) - p[0]
    n_rows = -(-(bsz + 1) // SUBLANES) * SUBLANES
    c_rows = jnp.zeros((n_rows, d), F32).at[:bsz].set(c).at[bsz].set(c_ctx)
    table = _modulation(c_rows, w_mod, b_mod).reshape(depth * n_rows * N_MOD, 1, d)
    wb = lambda w: w.astype(BF16)
    lat, cx = x, ctx
    for i in range(depth):
        kind, j = i % N_MIXERS, i // N_MIXERS
        last = i == depth - 1
        mod_l = [ModRow(table, i * n_rows * N_MOD + k, N_MOD) for k in range(N_MOD)]
        mod_c = [ModRow(table, (i * n_rows + bsz) * N_MOD + k, 0) for k in range(N_MOD)]
        ng = norm_g[i]
        if kind == 0:
            cx1, lat = _hgrn2_mixer(cx, lat, ng[0], mod_c, mod_l, wb(a_w_in[j]), lower[i], a_out_g[j],
                                    wb(a_w_out[j]), ng[1], not last)
        elif kind == 1:
            cx1, lat = _na_mixer(cx, lat, ng[0], mod_c, mod_l, wb(b_w_qkv[j]), b_rpb[j], wb(b_w_out[j]), ng[1],
                                 not last)
        else:
            cx1, lat = _s5_mixer(cx, lat, ng[0], mod_c, mod_l, c_lam_re[j], c_lam_im[j], c_log_dt[j], c_b_re[j],
                                 c_b_im[j], c_c_re[j], c_c_im[j], c_d[j], wb(c_w_glu[j]), ng[1], not last)
        lat = _conv_ffn_block(lat, ng[2], mod_l, wb(f_w_in[i]), f_conv_w[i], f_conv_b[i], wb(f_w_out[i]), ng[3])
        if not last:
            cx = _conv_ffn_block(cx1, ng[2], mod_c, wb(f_w_in[i]), f_conv_w[i], f_conv_b[i], wb(f_w_out[i]), ng[3])
    return lat
```

```python
import functools
from typing import NamedTuple

import jax
import jax.numpy as jnp
from jax import lax
from jax.experimental import pallas as pl
from jax.experimental.pallas import tpu as pltpu

F32 = jnp.float32
BF16 = jnp.bfloat16

EPS = 1e-6
F_MIN = 1e-30
N_MOD = 6
N_MIXERS = 3
A_HEADS = 8
GRID_W = 64
S5_CHUNK = 16
NEG_BIG = -1e30
NA_ROW_BLOCK = 4
SUBLANES = 8
LANES = 128
VMEM_LIMIT_BYTES = 56 * 1024 * 1024

NT_DIMS = (((1,), (1,)), ((), ()))
TN_DIMS = (((0,), (0,)), ((), ()))


def _params(*sem):
    return pltpu.CompilerParams(dimension_semantics=sem, vmem_limit_bytes=VMEM_LIMIT_BYTES)


def _row_tile(n, want):
    t = min(n, want)
    assert n % t == 0, (n, t)
    return t


def _rms(y, g):
    return y * lax.rsqrt(jnp.mean(y * y, axis=-1, keepdims=True) + EPS) * g


def _mod_body(c_ref, w_ref, b_ref, o_ref):
    c = c_ref[...]
    s = (c * jax.nn.sigmoid(c)).astype(BF16)
    o_ref[0] = jnp.dot(s, w_ref[0].astype(BF16), preferred_element_type=F32) + b_ref[0]


def _modulation(c_rows, w_mod, b_mod):
    depth, d, n = w_mod.shape
    r = c_rows.shape[0]
    tn = n // 4
    return pl.pallas_call(
        _mod_body,
        grid=(depth, n // tn),
        in_specs=[pl.BlockSpec((r, d), lambda i, j: (0, 0)),
                  pl.BlockSpec((1, d, tn), lambda i, j: (i, 0, j)),
                  pl.BlockSpec((1, 1, tn), lambda i, j: (i, 0, j))],
        out_specs=pl.BlockSpec((1, r, tn), lambda i, j: (i, 0, j)),
        out_shape=jax.ShapeDtypeStruct((depth, r, n), F32),
        compiler_params=_params("parallel", "parallel"),
        name="adaln_mod",
    )(c_rows, w_mod, b_mod.reshape(depth, 1, n))


def _norm_mod(x, g, sh, sc):
    return _rms(x, g) * (1.0 + sc) + sh


class ModRow(NamedTuple):
    table: jax.Array
    first: int
    step: int


def _mod_spec(m, per):
    return pl.BlockSpec((1, 1, m.table.shape[-1]), lambda *idx: (m.first + m.step * (idx[0] // per), 0, 0))


class LayerOf(NamedTuple):
    stack: jax.Array
    layer: int

    @property
    def shape(self):
        return self.stack.shape[1:]


def _layer_spec(w, block, index_map, **kw):
    return pl.BlockSpec((None,) + tuple(block), lambda *idx: (w.layer,) + tuple(index_map(*idx)), **kw)


def _proj_body(x_ref, g_ref, sh_ref, sc_ref, w_ref, o_ref):
    h = _norm_mod(x_ref[...], g_ref[...], sh_ref[0], sc_ref[0]).astype(BF16)
    o_ref[...] = jnp.dot(h, w_ref[...], preferred_element_type=F32).astype(o_ref.dtype)


def _norm_mod_proj(x, g, sh, sc, w, out_dtype):
    b, l, d = x.shape
    n = w.shape[1]
    tm = _row_tile(l, 512)
    per = l // tm
    out = pl.pallas_call(
        _proj_body,
        grid=(b * per,),
        in_specs=[pl.BlockSpec((tm, d), lambda i: (i, 0)),
                  pl.BlockSpec((1, d), lambda i: (0, 0)),
                  _mod_spec(sh, per), _mod_spec(sc, per),
                  _layer_spec(w, (d, n), lambda i: (0, 0), pipeline_mode=pl.Buffered(1))],
        out_specs=pl.BlockSpec((tm, n), lambda i: (i, 0)),
        out_shape=jax.ShapeDtypeStruct((b * l, n), out_dtype),
        compiler_params=_params("parallel"),
        name="norm_mod_proj",
    )(x.reshape(b * l, d), g.reshape(1, d), sh.table, sc.table, w.stack)
    return out.reshape(b, l, n)


def _residual(res_ref, gate_ref, g1_ref, y):
    return res_ref[...] + gate_ref[0] * _rms(y, g1_ref[...])


def _out_plain_body(a_ref, w_ref, res_ref, gate_ref, g1_ref, o_ref):
    y = jnp.dot(a_ref[...], w_ref[...], preferred_element_type=F32)
    o_ref[...] = _residual(res_ref, gate_ref, g1_ref, y)


def _out_hgrn_body(of_ref, ob_ref, gp_ref, og_ref, w_ref, res_ref, gate_ref, g1_ref, o_ref):
    o = of_ref[...].astype(F32) + ob_ref[...].astype(F32)
    parts = []
    for h in range(A_HEADS):
        oh = o[:, h * LANES:(h + 1) * LANES]
        parts.append(oh * lax.rsqrt(jnp.mean(oh * oh, axis=-1, keepdims=True) + EPS))
    gp = gp_ref[...]
    a = jnp.concatenate(parts, axis=-1) * og_ref[...] * (gp * jax.nn.sigmoid(gp))
    y = jnp.dot(a.astype(BF16), w_ref[...], preferred_element_type=F32)
    o_ref[...] = _residual(res_ref, gate_ref, g1_ref, y)


def _gelu_tanh(y):
    return 0.5 * y * (1.0 + jnp.tanh(0.7978845608028654 * (y + 0.044715 * (y * y * y))))


def _out_call(body, row_inputs, const_inputs, res, gate, g1, name):
    b, l, d = res.shape
    tm = _row_tile(l, 512)
    per = l // tm
    in_specs, args = [], []
    for arr, blk, width in row_inputs:
        in_specs.append(pl.BlockSpec((tm, width), lambda i, blk=blk: (i, blk)))
        args.append(arr.reshape(b * l, arr.shape[-1]))
    for arr in const_inputs:
        if isinstance(arr, LayerOf):
            in_specs.append(_layer_spec(arr, arr.shape, lambda i, nd=len(arr.shape): (0,) * nd))
            args.append(arr.stack)
        else:
            in_specs.append(pl.BlockSpec(arr.shape, lambda i, nd=arr.ndim: (0,) * nd))
            args.append(arr)
    in_specs += [pl.BlockSpec((tm, d), lambda i: (i, 0)),
                 _mod_spec(gate, per),
                 pl.BlockSpec((1, d), lambda i: (0, 0))]
    args += [res.reshape(b * l, d), gate.table, g1.reshape(1, d)]
    out = pl.pallas_call(
        body,
        grid=(b * per,),
        in_specs=in_specs,
        out_specs=pl.BlockSpec((tm, d), lambda i: (i, 0)),
        out_shape=jax.ShapeDtypeStruct((b * l, d), F32),
        compiler_params=_params("parallel"),
        name=name,
    )(*args)
    return out.reshape(b, l, d)


HGRN_CHUNK = 32
HGRN_SAFE_MIN = 1e-30
HGRN_Q_HEADROOM = 1e37


def _hgrn_gates(z, lb):
    f = lb + (1.0 - lb) * jax.nn.sigmoid(z)
    return jnp.maximum(f, F_MIN), 1.0 - f


def _hgrn_chunk_prep(q, v, z, lb, reverse):
    c = q.shape[0]
    half = c // 2
    fm, kk = _hgrn_gates(z, lb)
    pos = lax.broadcasted_iota(jnp.int32, (c, LANES), 0)
    if reverse:
        pos = (c - 1) - pos
    second = pos >= half
    hpos = jnp.where(second, pos - half, pos)
    ph = fm
    j = 1
    while j < half:
        ph = ph * jnp.where(hpos >= j, pltpu.roll(ph, (c - j) if reverse else j, 0), 1.0)
        j *= 2
    a_last = ph[half:half + 1] if reverse else ph[half - 1:half]
    r_last = ph[0:1] if reverse else ph[c - 1:c]
    p_mid = a_last
    p_last = a_last * r_last
    pp = jnp.where(second, ph * a_last, ph)
    r = jnp.where(second, ph, ph * (1.0 / a_last))
    kd = kk * (1.0 / r)
    q_max = jnp.max(jnp.abs(q), axis=0, keepdims=True)
    ok = jnp.where(p_mid >= HGRN_SAFE_MIN, r_last, 0.0) >= HGRN_SAFE_MIN
    ok = jnp.where(ok, p_mid * HGRN_Q_HEADROOM, -1.0) >= q_max
    return dict(qr=(q * r).astype(BF16), kd=kd.astype(BF16), qp=(q * pp).astype(BF16),
                kl=(kd * r_last).astype(BF16), vb=v.astype(BF16), dec=p_last, bad=jnp.where(ok, 0.0, 1.0))


def _hgrn_chunk_dots(p, st):
    sc = lax.dot_general(p["qr"], p["kd"], NT_DIMS, preferred_element_type=F32)
    o_st = lax.dot_general(p["qp"], st.astype(BF16), NT_DIMS, preferred_element_type=F32)
    u = lax.dot_general(p["vb"], p["kl"], TN_DIMS, preferred_element_type=F32)
    return sc, o_st, u


def _hgrn_chunk_out(p, sc, o_st, reverse):
    c = sc.shape[0]
    row = lax.broadcasted_iota(jnp.int32, (c, c), 0)
    col = lax.broadcasted_iota(jnp.int32, (c, c), 1)
    sc = jnp.where((col >= row) if reverse else (col <= row), sc, 0.0)
    return jnp.dot(sc.astype(BF16), p["vb"], preferred_element_type=F32) + o_st


def _hgrn_exact_block(q_ref, v_ref, z_ref, lb_ref, o_scr, st_scr, dr, reverse, n_tiles):
    pos = lax.broadcasted_iota(jnp.int32, (SUBLANES, LANES), 0)
    if reverse:
        pos = (SUBLANES - 1) - pos

    def earlier(x, j):
        if j == 0:
            return x
        return pltpu.roll(x, (SUBLANES - j) if reverse else j, 0)

    def later(x, j):
        return pltpu.roll(x, j if reverse else (SUBLANES - j), 0)

    def tile(i, carry):
        ti = (n_tiles - 1 - i) if reverse else i
        r0 = pl.multiple_of(ti * SUBLANES, SUBLANES)
        for h in range(A_HEADS):
            sl = slice(h * LANES, (h + 1) * LANES)
            lb = lb_ref[:, sl]
            z = z_ref[0, pl.ds(r0, SUBLANES), sl]
            q = q_ref[0, pl.ds(r0, SUBLANES), sl]
            v = v_ref[0, pl.ds(r0, SUBLANES), sl]
            fm, kk = _hgrn_gates(z, lb)
            pp = fm
            for j in (1, 2, 4):
                pp = pp * jnp.where(pos >= j, earlier(pp, j), 1.0)
            qq = jnp.where(pos <= SUBLANES - 2, later(fm, 1), 1.0)
            for j in (1, 2, 4):
                qq = qq * jnp.where(pos <= SUBLANES - 1 - j, later(qq, j), 1.0)
            dec = pp[0:1] if reverse else pp[SUBLANES - 1:SUBLANES]
            st = st_scr[dr, h]
            o = lax.dot_general((q * pp).astype(BF16), st.astype(BF16), NT_DIMS,
                                preferred_element_type=F32)
            g = fm
            for d in range(SUBLANES):
                if d == 0:
                    e = q * kk
                else:
                    if d > 1:
                        g = g * earlier(fm, d - 1)
                    e = q * g * jnp.where(pos >= d, earlier(kk, d), 0.0)
                o = o + jnp.sum(e, axis=-1, keepdims=True) * earlier(v, d)
            o_scr[pl.ds(r0, SUBLANES), sl] = o
            u = lax.dot_general(v.astype(BF16), (kk * qq).astype(BF16), TN_DIMS,
                                preferred_element_type=F32)
            st_scr[dr, h] = dec * st + u
        return carry

    lax.fori_loop(0, n_tiles, tile, 0)


def _hgrn_scan_body(qf_ref, vf_ref, zf_ref, qb_ref, vb_ref, zb_ref, lb_ref, s0_ref, of_ref, ob_ref, sfin_ref,
                    st_scr, save_scr, ox_scr, *, n_chunks):
    step = pl.program_id(1)

    @pl.when(step == 0)
    def _():
        st_scr[...] = s0_ref[0]

    save_scr[...] = st_scr[...]
    dirs = ((qf_ref, vf_ref, zf_ref, of_ref, False), (qb_ref, vb_ref, zb_ref, ob_ref, True))
    c = HGRN_CHUNK

    def chunk(i, bad):
        work, dots = [], []
        for dr, (q_ref, v_ref, z_ref, o_ref, reverse) in enumerate(dirs):
            rows = pl.ds(pl.multiple_of(((n_chunks - 1 - i) if reverse else i) * c, c), c)
            group = []
            for h in range(A_HEADS):
                sl = slice(h * LANES, (h + 1) * LANES)
                p = _hgrn_chunk_prep(q_ref[0, rows, sl], v_ref[0, rows, sl], z_ref[0, rows, sl], lb_ref[:, sl],
                                     reverse)
                bad = jnp.maximum(bad, p["bad"])
                group.append((dr, h, o_ref, rows, sl, reverse, p))
            dots += [_hgrn_chunk_dots(p, st_scr[dr, h]) for dr, h, _, _, _, _, p in group]
            work += group
        for (dr, h, _, _, _, _, p), (_, _, u) in zip(work, dots):
            st_scr[dr, h] = p["dec"] * st_scr[dr, h] + u
        for (dr, h, o_ref, rows, sl, reverse, p), (sc, o_st, _) in zip(work, dots):
            o_ref[0, rows, sl] = _hgrn_chunk_out(p, sc, o_st, reverse).astype(o_ref.dtype)
        return bad

    bad = lax.fori_loop(0, n_chunks, chunk, jnp.zeros((1, LANES), F32))

    @pl.when(jnp.max(bad) > 0.0)
    def _():
        st_scr[...] = save_scr[...]
        for dr, (q_ref, v_ref, z_ref, o_ref, reverse) in enumerate(dirs):
            _hgrn_exact_block(q_ref, v_ref, z_ref, lb_ref, ox_scr, st_scr, dr, reverse, n_chunks * c // SUBLANES)
            o_ref[0] = ox_scr[...].astype(o_ref.dtype)

    @pl.when(step == pl.num_programs(1) - 1)
    def _():
        sfin_ref[0] = st_scr[...]


def _hgrn_scan(proj, lb, s0):
    b, l, _ = proj.shape
    d = A_HEADS * LANES
    t = _row_tile(l, 512)
    nb = l // t
    fwd = lambda blk: pl.BlockSpec((1, t, d), lambda i, s: (i, s, blk))
    bwd = lambda blk: pl.BlockSpec((1, t, d), lambda i, s: (i, nb - 1 - s, blk))
    st_spec = pl.BlockSpec((1,) + s0.shape[1:], lambda i, s: (i, 0, 0, 0, 0))
    return pl.pallas_call(
        functools.partial(_hgrn_scan_body, n_chunks=t // HGRN_CHUNK),
        grid=(b, nb),
        in_specs=[fwd(0), fwd(1), fwd(3), bwd(0), bwd(1), bwd(4),
                  pl.BlockSpec((1, d), lambda i, s: (0, 0)), st_spec],
        out_specs=[fwd(0), bwd(0), st_spec],
        out_shape=[jax.ShapeDtypeStruct((b, l, d), BF16), jax.ShapeDtypeStruct((b, l, d), BF16),
                   jax.ShapeDtypeStruct(s0.shape, F32)],
        scratch_shapes=[pltpu.VMEM(s0.shape[1:], F32), pltpu.VMEM(s0.shape[1:], F32), pltpu.VMEM((t, d), F32)],
        compiler_params=_params("parallel", "arbitrary"),
        name="hgrn_scan",
    )(proj, proj, proj, proj, proj, proj, lb, s0)


def _hgrn2_mixer(x_c, x_l, ng, mod_c, mod_l, w_in, lower, out_g, w_out, g1, want_ctx):
    d = x_l.shape[-1]
    p_c = _norm_mod_proj(x_c, ng, mod_c[0], mod_c[1], w_in, F32)
    p_l = _norm_mod_proj(x_l, ng, mod_l[0], mod_l[1], w_in, F32)
    lb = lower.reshape(1, d)
    zero = jnp.zeros((x_l.shape[0], 2, A_HEADS, LANES, LANES), F32)
    ocf, ocb, s_ctx = _hgrn_scan(p_c, lb, zero)
    olf, olb, _ = _hgrn_scan(p_l, lb, s_ctx)
    o_c, o_l = (ocf, ocb), (olf, olb)
    og = out_g.reshape(1, d)
    new_l = _out_call(_out_hgrn_body, [(o_l[0], 0, d), (o_l[1], 0, d), (p_l, 2, d)], [og, w_out],
                      x_l, mod_l[2], g1, "hgrn_out")
    new_c = None
    if want_ctx:
        new_c = _out_call(_out_hgrn_body, [(o_c[0], 0, d), (o_c[1], 0, d), (p_c, 2, d)], [og, w_out],
                          x_c, mod_c[2], g1, "hgrn_out")
    return new_c, new_l


def _softmax_pv(s_list, v_list):
    mx = functools.reduce(jnp.maximum, [jnp.max(s, axis=-1, keepdims=True) for s in s_list])
    ps = [jnp.exp(s - mx) for s in s_list]
    den = functools.reduce(jnp.add, [jnp.sum(p, axis=-1, keepdims=True) for p in ps])
    acc = functools.reduce(jnp.add, [jnp.dot(p.astype(BF16), v, preferred_element_type=F32)
                                     for p, v in zip(ps, v_list)])
    return acc / den


def _na_lat_body(q_ref, k_ref, v_ref, kc_ref, vc_ref, t2_ref, o_ref, *, rows, kh, scale):
    w = GRID_W
    rb = NA_ROW_BLOCK
    nk = kh + rb
    n_rel = 2 * kh - 1
    first = lax.broadcasted_iota(jnp.int32, (rb * w, LANES), 1) < (LANES // 2)
    kc = kc_ref[0]
    vc = vc_ref[0]

    def bias_index(r, ks, m):
        r0 = jnp.clip(r - kh // 2, 0, rows - kh)
        ka = ks + 2 * m
        rel_a = ka - r + (kh - 1)
        in_a = (ka >= r0) & (ka < r0 + kh)
        in_b = (ka + 1 >= r0) & (ka + 1 < r0 + kh)
        both, only_b, only_a = rel_a, (n_rel - 1) + rel_a + 1, (2 * n_rel - 1) + rel_a
        return jnp.where(in_a, jnp.where(in_b, both, only_a), jnp.where(in_b, only_b, 3 * n_rel - 1))

    def block(i, carry):
        rq = i * rb
        ks = jnp.clip(rq - kh // 2, 0, rows - nk)
        q2 = q_ref[0, pl.ds(pl.multiple_of(rq * w, rb * w), rb * w), :] * scale
        k2 = k_ref[0, pl.ds(pl.multiple_of(ks * w, w), nk * w), :]
        v2 = v_ref[0, pl.ds(pl.multiple_of(ks * w, w), nk * w), :]
        outs = []
        for hh in range(2):
            qm = jnp.where(first if hh == 0 else jnp.logical_not(first), q2, jnp.zeros_like(q2))
            bias = jnp.concatenate(
                [jnp.concatenate([t2_ref[hh, bias_index(rq + a, ks, m)] for m in range(nk // 2)], axis=-1)
                 for a in range(rb)], axis=0)
            s_w = lax.dot_general(qm, k2, NT_DIMS, preferred_element_type=F32) + bias
            s_c = lax.dot_general(qm, kc, NT_DIMS, preferred_element_type=F32)
            outs.append(_softmax_pv([s_w, s_c], [v2, vc]))
        o_ref[0, pl.ds(pl.multiple_of(rq * w, rb * w), rb * w), :] = (
            jnp.where(first, outs[0], outs[1]).astype(o_ref.dtype))
        return carry

    lax.fori_loop(0, rows // rb, block, 0, unroll=2)


def _na_ctx_body(q_ref, k_ref, v_ref, o_ref, *, scale):
    n = q_ref.shape[1]
    first = lax.broadcasted_iota(jnp.int32, (n, LANES), 1) < (LANES // 2)
    q2 = q_ref[0] * scale
    k2 = k_ref[0]
    v2 = v_ref[0]
    outs = []
    for hh in range(2):
        qm = jnp.where(first if hh == 0 else jnp.logical_not(first), q2, jnp.zeros_like(q2))
        s = lax.dot_general(qm, k2, NT_DIMS, preferred_element_type=F32)
        outs.append(_softmax_pv([s], [v2]))
    o_ref[0] = jnp.where(first, outs[0], outs[1]).astype(o_ref.dtype)


def _na_bias_table(rpb, kh, kw):
    w = jnp.arange(GRID_W)[:, None]
    c = jnp.arange(GRID_W)[None, :]
    c0 = jnp.clip(w - kw // 2, 0, GRID_W - kw)
    inside = (c >= c0) & (c < c0 + kw)
    pad = GRID_W - kw
    padded = jnp.pad(rpb.astype(F32), ((0, 0), (0, 0), (pad, pad)))
    shifted = jnp.stack([padded[:, :, pad + kw - 1 - q:pad + kw - 1 - q + GRID_W] for q in range(GRID_W)], axis=2)
    t = jnp.where(inside[None, None], shifted, NEG_BIG)
    off = jnp.full_like(t, NEG_BIG)
    pair = lambda a, b: jnp.concatenate([a, b], axis=-1)
    return jnp.concatenate([pair(t[:, :-1], t[:, 1:]), pair(off, t), pair(t, off), pair(off, off)[:, :1]], axis=1)


def _na_mixer(x_c, x_l, ng, mod_c, mod_l, w_qkv, rpb, w_out, g1, want_ctx):
    b, l, d = x_l.shape
    lc = x_c.shape[1]
    heads = rpb.shape[0]
    na_rows, na_cols = (rpb.shape[1] + 1) // 2, (rpb.shape[2] + 1) // 2
    dh = d // heads
    assert 2 * dh == LANES and l % GRID_W == 0
    rows = l // GRID_W
    kh = min(na_rows, rows)
    assert kh == na_rows and (kh + NA_ROW_BLOCK) % 2 == 0
    assert rows % NA_ROW_BLOCK == 0 and rows >= kh + NA_ROW_BLOCK
    scale = dh ** -0.5
    nhp = heads // 2
    qkv_c = _norm_mod_proj(x_c, ng, mod_c[0], mod_c[1], w_qkv, BF16)
    qkv_l = _norm_mod_proj(x_l, ng, mod_l[0], mod_l[1], w_qkv, BF16)
    t2 = _na_bias_table(rpb, kh, na_cols)
    o_l = pl.pallas_call(
        functools.partial(_na_lat_body, rows=rows, kh=kh, scale=scale),
        grid=(b, nhp),
        in_specs=[pl.BlockSpec((1, l, LANES), lambda i, p: (i, 0, p)),
                  pl.BlockSpec((1, l, LANES), lambda i, p: (i, 0, nhp + p)),
                  pl.BlockSpec((1, l, LANES), lambda i, p: (i, 0, 2 * nhp + p)),
                  pl.BlockSpec((1, lc, LANES), lambda i, p: (i, 0, nhp + p)),
                  pl.BlockSpec((1, lc, LANES), lambda i, p: (i, 0, 2 * nhp + p)),
                  pl.BlockSpec((2,) + t2.shape[1:], lambda i, p: (p, 0, 0, 0))],
        out_specs=pl.BlockSpec((1, l, LANES), lambda i, p: (i, 0, p)),
        out_shape=jax.ShapeDtypeStruct((b, l, d), BF16),
        compiler_params=_params("parallel", "parallel"),
        name="na_latent",
    )(qkv_l, qkv_l, qkv_l, qkv_c, qkv_c, t2)
    new_l = _out_call(_out_plain_body, [(o_l, 0, d)], [w_out], x_l, mod_l[2], g1, "na_out")
    new_c = None
    if want_ctx:
        o_c = pl.pallas_call(
            functools.partial(_na_ctx_body, scale=scale),
            grid=(b, nhp),
            in_specs=[pl.BlockSpec((1, lc, LANES), lambda i, p: (i, 0, p)),
                      pl.BlockSpec((1, lc, LANES), lambda i, p: (i, 0, nhp + p)),
                      pl.BlockSpec((1, lc, LANES), lambda i, p: (i, 0, 2 * nhp + p))],
            out_specs=pl.BlockSpec((1, lc, LANES), lambda i, p: (i, 0, p)),
            out_shape=jax.ShapeDtypeStruct((b, lc, d), BF16),
            compiler_params=_params("parallel", "parallel"),
            name="na_context",
        )(qkv_c, qkv_c, qkv_c)
        new_c = _out_call(_out_plain_body, [(o_c, 0, d)], [w_out], x_c, mod_c[2], g1, "na_out")
    return new_c, new_l


S5_GROUP_BLOCK = 8


def _s5_kernel_body(c_ref, w_ref, o_ref):
    for i in range(c_ref.shape[0]):
        o_ref[i] = jnp.dot(c_ref[i], w_ref[i], preferred_element_type=F32, precision=lax.Precision.HIGHEST)


def _s5_impulse(cmat, wmat):
    n = cmat.shape[0]
    gb = S5_GROUP_BLOCK
    return pl.pallas_call(
        _s5_kernel_body,
        grid=(n // gb,),
        in_specs=[pl.BlockSpec((gb,) + cmat.shape[1:], lambda i: (i, 0, 0)),
                  pl.BlockSpec((gb,) + wmat.shape[1:], lambda i: (i, 0, 0))],
        out_specs=pl.BlockSpec((gb, cmat.shape[1], wmat.shape[2]), lambda i: (i, 0, 0)),
        out_shape=jax.ShapeDtypeStruct((n, cmat.shape[1], wmat.shape[2]), F32),
        compiler_params=_params("parallel"),
        name="s5_impulse",
    )(cmat, wmat)


def _s5_operators(lam_re, lam_im, log_dt, b_re, b_im, c_re, c_im, t):
    _, g, p = lam_re.shape
    cg = b_re.shape[-1]
    lam = lax.complex(lam_re.astype(F32), lam_im.astype(F32))
    ldt = lam * jnp.exp(log_dt.astype(F32))[..., None]
    a = jnp.exp(ldt)
    bbar = ((a - 1.0) / lam)[..., None] * lax.complex(b_re.astype(F32), b_im.astype(F32))
    cm = lax.complex(c_re.astype(F32), c_im.astype(F32))
    apow = jnp.exp(ldt[..., None] * jnp.arange(t + 1, dtype=F32))
    w = apow[..., :t, None] * bbar[:, :, :, None, :]
    wmat = jnp.concatenate([jnp.real(w), jnp.imag(w)], axis=2).reshape(2 * g, 2 * p, t * cg)
    cmat = jnp.concatenate([jnp.real(cm), -jnp.imag(cm)], axis=-1).reshape(2 * g, cg, 2 * p)
    k = _s5_impulse(cmat, wmat).reshape(2, g, cg, t, cg)
    k = jnp.transpose(k, (0, 1, 3, 4, 2))
    kfull = jnp.concatenate([k[1, :, :0:-1], (k[0, :, :1] + k[1, :, :1]), k[0, :, 1:]], axis=1)
    lagvec = jnp.transpose(kfull, (0, 2, 1, 3)).reshape(g, cg, (2 * t - 1) * cg).astype(BF16)
    toep = jnp.stack([lagvec[:, :, (t - 1 - s) * cg:(2 * t - 1 - s) * cg] for s in range(t)], axis=1)
    toep = toep.reshape(g, t * cg, t * cg)

    def state_in(wd, flip):
        wd = wd[:, :, ::-1] if flip else wd
        m = jnp.transpose(wd, (0, 2, 3, 1)).reshape(g, t * cg, p)
        return jnp.concatenate([jnp.real(m), jnp.imag(m)], axis=-1)

    def state_out(cd, pw):
        n = cd[:, :, :, None] * pw[:, None, :, :]
        n = jnp.transpose(n, (0, 2, 3, 1)).reshape(g, p, t * cg)
        return jnp.concatenate([jnp.real(n), -jnp.imag(n)], axis=1)

    m_f = state_in(w[0], True)
    m_b = state_in(w[1], False)
    n_f = state_out(cm[0], apow[0][..., 1:])
    n_b = state_out(cm[1], apow[1][..., :0:-1])
    at = apow[..., t]
    a1 = jnp.concatenate([jnp.real(at), jnp.real(at)], axis=-1)
    a2 = jnp.concatenate([-jnp.imag(at), jnp.imag(at)], axis=-1)
    bf = lambda m: m.astype(BF16)
    return dict(toep=bf(toep), m_f=bf(m_f), m_b=bf(m_b), n_f=bf(n_f), n_b=bf(n_b), a1=a1, a2=a2)


def _s5_body(u_ref, dsk_ref, toep_ref, mf_ref, mb_ref, nf_ref, nb_ref, a1_ref, a2_ref, x0_ref,
             y_ref, xfin_ref, z_scr, zs_scr, xin_scr):
    gb, nc = u_ref.shape[1], u_ref.shape[2]
    half = LANES // 2
    ubs = [u_ref[0, gi].astype(BF16) for gi in range(gb)]
    for dr, m_ref in enumerate((mf_ref, mb_ref)):
        z = jnp.stack([jnp.dot(ubs[gi], m_ref[gi], preferred_element_type=F32) for gi in range(gb)])
        z_scr[dr] = jnp.swapaxes(z, 0, 1)

    swap = lambda t: pltpu.roll(t, half, t.ndim - 1)
    for dr in range(2):
        zs_scr[dr] = swap(z_scr[dr].reshape(nc * gb, LANES)).reshape(nc, gb, LANES)

    coef = [(a1_ref[dr], a2_ref[dr]) for dr in range(2)]

    def scan(j, xs):
        out = []
        for dr in range(2):
            jj = j if dr == 0 else nc - 1 - j
            x, xw = xs[2 * dr], xs[2 * dr + 1]
            a1, a2 = coef[dr]
            xin_scr[dr, jj] = x
            out.append(a1 * x + a2 * xw + z_scr[dr, jj])
            out.append(a1 * xw - a2 * x + zs_scr[dr, jj])
        return tuple(out)

    x0f, x0b = x0_ref[0, 0], x0_ref[0, 1]
    xf, _, xb, _ = lax.fori_loop(0, nc, scan, (x0f, swap(x0f), x0b, swap(x0b)))
    xfin_ref[0, 0] = xf
    xfin_ref[0, 1] = xb
    xin = [jnp.swapaxes(xin_scr[dr], 0, 1).astype(BF16) for dr in range(2)]
    for gi in range(gb):
        u = u_ref[0, gi]
        y = jnp.dot(ubs[gi], toep_ref[gi], preferred_element_type=F32) + u * dsk_ref[gi]
        y = y + jnp.dot(xin[0][gi], nf_ref[gi], preferred_element_type=F32)
        y = y + jnp.dot(xin[1][gi], nb_ref[gi], preferred_element_type=F32)
        y_ref[0, gi] = y


def _piece_transpose(src, piece_of, cg):
    n = len(src)
    out = [None] * n
    for s in range(n):
        c = src[s % n]
        for p in range(1, n):
            c = jnp.where(piece_of == p, src[(p + s) % n], c)
        r = c if s == 0 else pltpu.roll(c, s * cg, 1)
        for p in range(n):
            out[p] = r if s == 0 else jnp.where(piece_of == (p + s) % n, r, out[p])
    return out


def _to_chunks(h_scr, o_ref, t, cg):
    ncb = h_scr.shape[1] // t
    pieces = LANES // cg
    piece_of = lax.broadcasted_iota(jnp.int32, (ncb, LANES), 1) // cg

    def lane_block(lb, carry):
        for tq in range(t // pieces):
            steps = [h_scr[lb, pl.ds(tq * pieces + tr, ncb, stride=t), :] for tr in range(pieces)]
            for gl, col in enumerate(_piece_transpose(steps, piece_of, cg)):
                o_ref[0, lb * pieces + gl, :, tq * LANES:(tq + 1) * LANES] = col
        return carry

    lax.fori_loop(0, h_scr.shape[0], lane_block, 0)


def _from_chunks(y_ref, y_scr, t, cg):
    ncb = y_ref.shape[2]
    pieces = LANES // cg
    piece_of = lax.broadcasted_iota(jnp.int32, (ncb, LANES), 1) // cg

    def lane_block(lb, carry):
        for tq in range(t // pieces):
            cols = [y_ref[0, lb * pieces + gl, :, tq * LANES:(tq + 1) * LANES] for gl in range(pieces)]
            for tr, step in enumerate(_piece_transpose(cols, piece_of, cg)):
                y_scr[lb, pl.ds(tq * pieces + tr, ncb, stride=t), :] = step
        return carry

    lax.fori_loop(0, y_scr.shape[0], lane_block, 0)


def _norm_chunks_body(x_ref, g_ref, sh_ref, sc_ref, o_ref, h_scr, *, t, cg):
    h = _norm_mod(x_ref[...], g_ref[...], sh_ref[0], sc_ref[0])
    for lb in range(h_scr.shape[0]):
        h_scr[lb] = h[:, lb * LANES:(lb + 1) * LANES]
    _to_chunks(h_scr, o_ref, t, cg)


def _norm_mod_chunks(x, g, sh, sc, groups, t):
    b, l, d = x.shape
    cg = d // groups
    tm = _row_tile(l, 512)
    per = l // tm
    assert tm % t == 0 and (tm // t) % SUBLANES == 0 and LANES % cg == 0 and t % (LANES // cg) == 0
    return pl.pallas_call(
        functools.partial(_norm_chunks_body, t=t, cg=cg),
        grid=(b * per,),
        in_specs=[pl.BlockSpec((tm, d), lambda i: (i, 0)),
                  pl.BlockSpec((1, d), lambda i: (0, 0)),
                  _mod_spec(sh, per), _mod_spec(sc, per)],
        out_specs=pl.BlockSpec((1, groups, tm // t, t * cg), lambda i: (i // per, 0, i % per, 0)),
        out_shape=jax.ShapeDtypeStruct((b, groups, l // t, t * cg), F32),
        scratch_shapes=[pltpu.VMEM((d // LANES, tm, LANES), F32)],
        compiler_params=_params("parallel"),
        name="norm_mod_chunks",
    )(x.reshape(b * l, d), g.reshape(1, d), sh.table, sc.table)


def _glu_chunks_body(y_ref, w_ref, res_ref, gate_ref, g1_ref, o_ref, y_scr, *, t, cg):
    _from_chunks(y_ref, y_scr, t, cg)
    d = o_ref.shape[-1]
    y = jnp.concatenate([y_scr[lb] for lb in range(y_scr.shape[0])], axis=-1)
    a = _gelu_tanh(y).astype(BF16)
    ag = jnp.dot(a, w_ref[...], preferred_element_type=F32)
    o_ref[...] = _residual(res_ref, gate_ref, g1_ref, ag[:, :d] * jax.nn.sigmoid(ag[:, d:]))


def _glu_out_chunks(y, w_glu, res, gate, g1, t):
    b, l, d = res.shape
    groups = y.shape[1]
    cg = d // groups
    tm = _row_tile(l, 512)
    per = l // tm
    out = pl.pallas_call(
        functools.partial(_glu_chunks_body, t=t, cg=cg),
        grid=(b * per,),
        in_specs=[pl.BlockSpec((1, groups, tm // t, t * cg), lambda i: (i // per, 0, i % per, 0)),
                  _layer_spec(w_glu, w_glu.shape, lambda i: (0, 0)),
                  pl.BlockSpec((tm, d), lambda i: (i, 0)),
                  _mod_spec(gate, per),
                  pl.BlockSpec((1, d), lambda i: (0, 0))],
        out_specs=pl.BlockSpec((tm, d), lambda i: (i, 0)),
        out_shape=jax.ShapeDtypeStruct((b * l, d), F32),
        scratch_shapes=[pltpu.VMEM((d // LANES, tm, LANES), F32)],
        compiler_params=_params("parallel"),
        name="s5_glu_out",
    )(y, w_glu.stack, res.reshape(b * l, d), gate.table, g1.reshape(1, d))
    return out.reshape(b, l, d)


def _s5_scan(u, ops, dsk, x0):
    b, g, nc, tc = u.shape
    gb = S5_GROUP_BLOCK
    per_g = lambda shape: pl.BlockSpec((gb,) + shape, lambda j, i: (j,) + (0,) * len(shape))
    y, xfin = pl.pallas_call(
        _s5_body,
        grid=(g // gb, b),
        in_specs=[pl.BlockSpec((1, gb, nc, tc), lambda j, i: (i, j, 0, 0)),
                  per_g((1, tc)), per_g((tc, tc)), per_g((tc, LANES)), per_g((tc, LANES)),
                  per_g((LANES, tc)), per_g((LANES, tc)),
                  pl.BlockSpec((2, gb, LANES), lambda j, i: (0, j, 0)),
                  pl.BlockSpec((2, gb, LANES), lambda j, i: (0, j, 0)),
                  pl.BlockSpec((1, 2, gb, LANES), lambda j, i: (i, 0, j, 0))],
        out_specs=[pl.BlockSpec((1, gb, nc, tc), lambda j, i: (i, j, 0, 0)),
                   pl.BlockSpec((1, 2, gb, LANES), lambda j, i: (i, 0, j, 0))],
        out_shape=[jax.ShapeDtypeStruct((b, g, nc, tc), F32), jax.ShapeDtypeStruct((b, 2, g, LANES), F32)],
        scratch_shapes=[pltpu.VMEM((2, nc, gb, LANES), F32)] * 3,
        compiler_params=_params("parallel", "parallel"),
        name="s5_scan",
    )(u, dsk, ops["toep"], ops["m_f"], ops["m_b"], ops["n_f"], ops["n_b"], ops["a1"], ops["a2"], x0)
    return y, xfin


def _s5_mixer(x_c, x_l, ng, mod_c, mod_l, lam_re, lam_im, log_dt, b_re, b_im, c_re, c_im, d_skip, w_glu, g1,
              want_ctx):
    b, l, d = x_l.shape
    g, p = lam_re.shape[1], lam_re.shape[2]
    cg = d // g
    assert 2 * p == LANES and g % S5_GROUP_BLOCK == 0
    ops = _s5_operators(lam_re, lam_im, log_dt, b_re, b_im, c_re, c_im, S5_CHUNK)
    dsk = jnp.tile(d_skip.astype(F32).reshape(g, 1, cg), (1, 1, S5_CHUNK))
    u_c = _norm_mod_chunks(x_c, ng, mod_c[0], mod_c[1], g, S5_CHUNK)
    u_l = _norm_mod_chunks(x_l, ng, mod_l[0], mod_l[1], g, S5_CHUNK)
    y_c, x_ctx = _s5_scan(u_c, ops, dsk, jnp.zeros((b, 2, g, LANES), F32))
    y_l, _ = _s5_scan(u_l, ops, dsk, x_ctx)
    new_l = _glu_out_chunks(y_l, w_glu, x_l, mod_l[2], g1, S5_CHUNK)
    new_c = _glu_out_chunks(y_c, w_glu, x_c, mod_c[2], g1, S5_CHUNK) if want_ctx else None
    return new_c, new_l


FFN_HALO = 16
FFN_TN = 256
FFN_TILES_PER_STEP = 11
FFN_ROW_SPLIT = 256


def _ffn_body(xp_ref, x_ref, xn_ref, g2_ref, sh_ref, sc_ref, *rest, per, n_tiles, tps):
    w_refs = rest[:2 * tps]
    cw_ref, cb_ref, wo_ref, gate_ref, g3_ref, o_ref, h_scr, gat_scr = rest[2 * tps:]
    i, j = pl.program_id(0), pl.program_id(1)
    tm = x_ref.shape[0]
    hl = FFN_HALO
    f = wo_ref.shape[0]
    n_steps = pl.cdiv(n_tiles, tps)
    when = (lambda cond: lambda fn: fn()) if n_steps == 1 else pl.when

    @when(j == 0)
    def _():
        g, sh, sc = g2_ref[...], sh_ref[0], sc_ref[0]
        keep_p = ((i % per) != 0).astype(F32)
        keep_n = ((i % per) != per - 1).astype(F32)
        h_scr[0:hl] = (_norm_mod(xp_ref[...], g, sh, sc) * keep_p).astype(BF16)
        h_scr[hl:hl + tm] = _norm_mod(x_ref[...], g, sh, sc).astype(BF16)
        h_scr[hl + tm:] = (_norm_mod(xn_ref[...], g, sh, sc) * keep_n).astype(BF16)

    sub = FFN_ROW_SPLIT if tm % FFN_ROW_SPLIT == 0 else tm
    ns = tm // sub
    edge = SUBLANES
    tn = w_refs[0].shape[1]

    def pieces(w_ref):
        out = []
        for s in range(ns):
            lo = hl + s * sub - (hl if s == 0 else 0)
            hi = hl + (s + 1) * sub + (hl if s == ns - 1 else 0)
            out.append(jnp.dot(h_scr[lo:hi], w_ref[...], preferred_element_type=F32))
        return out

    def conv(p, s, cw, cb):
        base = hl if s == 0 else 0
        left = p[s][hl - edge:hl] if s == 0 else p[s - 1][-edge:]
        right = p[s][base + sub:base + sub + edge] if s == ns - 1 else p[s + 1][:edge]
        ext = jnp.concatenate([left, p[s][base:base + sub], right], axis=0)
        um = pltpu.roll(ext, 1, 0)[edge:edge + sub]
        up = pltpu.roll(ext, sub + 2 * edge - 1, 0)[edge:edge + sub]
        return cb + um * cw[0:1] + ext[edge:edge + sub] * cw[1:2] + up * cw[2:3]

    def tile(k):
        col = k * tn if n_steps == 1 else pl.multiple_of((j * tps + k) * tn, tn)
        cols_a, cols_v = pl.ds(col, tn), pl.ds(pl.multiple_of(f + col, tn), tn)
        pa, pv = pieces(w_refs[2 * k]), pieces(w_refs[2 * k + 1])
        for s in range(ns):
            a = conv(pa, s, cw_ref[:, cols_a], cb_ref[:, cols_a])
            v = conv(pv, s, cw_ref[:, cols_v], cb_ref[:, cols_v])
            gat_scr[s * sub:(s + 1) * sub, cols_a] = (a * jax.nn.sigmoid(a) * v).astype(BF16)

    always = n_tiles - tps * (n_steps - 1)
    for k in range(tps):
        if k < always:
            tile(k)
        else:
            pl.when(j * tps + k < n_tiles)(functools.partial(tile, k))

    @when(j == n_steps - 1)
    def _():
        y = jnp.dot(gat_scr[...], wo_ref[...], preferred_element_type=F32)
        o_ref[...] = x_ref[...] + gate_ref[0] * _rms(y, g3_ref[...])


def _conv_ffn_block(x, ng2, mod, w_in, conv_w, conv_b, w_out, ng3):
    b, l, d = x.shape
    f = w_out.shape[0]
    tn, hl = FFN_TN, FFN_HALO
    assert f % tn == 0 and conv_w.shape[0] == 3 and w_in.shape[1] == 2 * f
    nj = f // tn
    tm = _row_tile(l, 1024)
    per = l // tm
    hb = tm // hl
    last_hb = b * l // hl - 1
    cb = conv_b.reshape(1, 2 * f)
    x2 = x.reshape(b * l, d)
    tps = min(FFN_TILES_PER_STEP, nj)
    once = dict(pipeline_mode=pl.Buffered(1)) if tps == nj else {}
    w_specs = []
    for k in range(tps):
        w_specs.append(_layer_spec(w_in, (d, tn), lambda i, j, k=k: (0, jnp.minimum(j * tps + k, nj - 1)), **once))
        w_specs.append(_layer_spec(w_in, (d, tn), lambda i, j, k=k: (0, nj + jnp.minimum(j * tps + k, nj - 1)),
                                   **once))
    whole = lambda arr: pl.BlockSpec(arr.shape, lambda i, j: (0,) * arr.ndim)
    out = pl.pallas_call(
        functools.partial(_ffn_body, per=per, n_tiles=nj, tps=tps),
        grid=(b * per, pl.cdiv(nj, tps)),
        in_specs=[pl.BlockSpec((hl, d), lambda i, j: (jnp.maximum(i * hb - 1, 0), 0)),
                  pl.BlockSpec((tm, d), lambda i, j: (i, 0)),
                  pl.BlockSpec((hl, d), lambda i, j: (jnp.minimum((i + 1) * hb, last_hb), 0)),
                  pl.BlockSpec((1, d), lambda i, j: (0, 0)),
                  _mod_spec(mod[3], per), _mod_spec(mod[4], per),
                  *w_specs, whole(conv_w), whole(cb),
                  _layer_spec(w_out, (f, d), lambda i, j: (0, 0), pipeline_mode=pl.Buffered(1)),
                  _mod_spec(mod[5], per),
                  pl.BlockSpec((1, d), lambda i, j: (0, 0))],
        out_specs=pl.BlockSpec((tm, d), lambda i, j: (i, 0)),
        out_shape=jax.ShapeDtypeStruct((b * l, d), F32),
        scratch_shapes=[pltpu.VMEM((tm + 2 * hl, d), BF16), pltpu.VMEM((tm, f), BF16)],
        compiler_params=_params("parallel", "arbitrary"),
        name="conv_ffn",
    )(x2, x2, x2, ng2.reshape(1, d), mod[3].table, mod[4].table, *([w_in.stack] * (2 * tps)), conv_w, cb,
      w_out.stack,
      mod[5].table, ng3.reshape(1, d))
    return out.reshape(b, l, d)


def kernel(x, c, ctx, c_ctx, w_mod, b_mod, norm_g, a_w_in, a_lower_logits, a_out_g, a_w_out, b_w_qkv, b_rpb,
           b_w_out, c_lam_re, c_lam_im, c_log_dt, c_b_re, c_b_im, c_c_re, c_c_im, c_d, c_w_glu, f_w_in,
           f_conv_w, f_conv_b, f_w_out):
    bsz, _, d = x.shape
    depth = w_mod.shape[0]
    p = jax.nn.softmax(a_lower_logits.astype(F32), axis=0)
    lower = jnp.cumsum(p, axis=0) - p[0]
    n_rows = -(-(bsz + 1) // SUBLANES) * SUBLANES
    c_rows = jnp.zeros((n_rows, d), F32).at[:bsz].set(c).at[bsz].set(c_ctx)
    table = _modulation(c_rows, w_mod, b_mod).reshape(depth * n_rows * N_MOD, 1, d)
    wa_in, wa_out, wb_qkv, wb_out, wc_glu, wf_in, wf_out = (
        w.astype(BF16) for w in (a_w_in, a_w_out, b_w_qkv, b_w_out, c_w_glu, f_w_in, f_w_out))
    lat, cx = x, ctx
    for i in range(depth):
        kind, j = i % N_MIXERS, i // N_MIXERS
        last = i == depth - 1
        mod_l = [ModRow(table, i * n_rows * N_MOD + k, N_MOD) for k in range(N_MOD)]
        mod_c = [ModRow(table, (i * n_rows + bsz) * N_MOD + k, 0) for k in range(N_MOD)]
        ng = norm_g[i]
        if kind == 0:
            cx1, lat = _hgrn2_mixer(cx, lat, ng[0], mod_c, mod_l, LayerOf(wa_in, j), lower[i], a_out_g[j],
                                    LayerOf(wa_out, j), ng[1], not last)
        elif kind == 1:
            cx1, lat = _na_mixer(cx, lat, ng[0], mod_c, mod_l, LayerOf(wb_qkv, j), b_rpb[j], LayerOf(wb_out, j),
                                 ng[1], not last)
        else:
            cx1, lat = _s5_mixer(cx, lat, ng[0], mod_c, mod_l, c_lam_re[j], c_lam_im[j], c_log_dt[j], c_b_re[j],
                                 c_b_im[j], c_c_re[j], c_c_im[j], c_d[j], LayerOf(wc_glu, j), ng[1], not last)
        ffn = (LayerOf(wf_in, i), f_conv_w[i], f_conv_b[i], LayerOf(wf_out, i), ng[3])
        lat = _conv_ffn_block(lat, ng[2], mod_l, *ffn)
        if not last:
            cx = _conv_ffn_block(cx1, ng[2], mod_c, *ffn)
    return lat
```

```python
import functools
from typing import NamedTuple

import jax
import jax.numpy as jnp
from jax import lax
from jax.experimental import pallas as pl
from jax.experimental.pallas import tpu as pltpu

F32 = jnp.float32
BF16 = jnp.bfloat16

EPS = 1e-6
F_MIN = 1e-30
N_MOD = 6
N_MIXERS = 3
A_HEADS = 8
GRID_W = 64
S5_CHUNK = 16
NEG_BIG = -1e30
NA_ROW_BLOCK = 4
SUBLANES = 8
LANES = 128
VMEM_LIMIT_BYTES = 56 * 1024 * 1024

NT_DIMS = (((1,), (1,)), ((), ()))
TN_DIMS = (((0,), (0,)), ((), ()))


def _params(*sem):
    return pltpu.CompilerParams(dimension_semantics=sem, vmem_limit_bytes=VMEM_LIMIT_BYTES)


def _row_tile(n, want):
    t = min(n, want)
    assert n % t == 0, (n, t)
    return t


def _rms(y, g):
    return y * lax.rsqrt(jnp.mean(y * y, axis=-1, keepdims=True) + EPS) * g


def _mod_body(c_ref, w_ref, b_ref, o_ref):
    c = c_ref[...]
    s = (c * jax.nn.sigmoid(c)).astype(BF16)
    o_ref[0] = jnp.dot(s, w_ref[0].astype(BF16), preferred_element_type=F32) + b_ref[0]


def _modulation(c_rows, w_mod, b_mod):
    depth, d, n = w_mod.shape
    r = c_rows.shape[0]
    tn = n // 4
    return pl.pallas_call(
        _mod_body,
        grid=(depth, n // tn),
        in_specs=[pl.BlockSpec((r, d), lambda i, j: (0, 0)),
                  pl.BlockSpec((1, d, tn), lambda i, j: (i, 0, j)),
                  pl.BlockSpec((1, 1, tn), lambda i, j: (i, 0, j))],
        out_specs=pl.BlockSpec((1, r, tn), lambda i, j: (i, 0, j)),
        out_shape=jax.ShapeDtypeStruct((depth, r, n), F32),
        compiler_params=_params("parallel", "parallel"),
        name="adaln_mod",
    )(c_rows, w_mod, b_mod.reshape(depth, 1, n))


def _norm_mod(x, g, sh, sc):
    return _rms(x, g) * (1.0 + sc) + sh


class ModRow(NamedTuple):
    table: jax.Array
    first: int
    step: int


def _mod_spec(m, per):
    return pl.BlockSpec((1, 1, m.table.shape[-1]), lambda *idx: (m.first + m.step * (idx[0] // per), 0, 0))


class LayerOf(NamedTuple):
    stack: jax.Array
    layer: int

    @property
    def shape(self):
        return self.stack.shape[1:]


def _layer_spec(w, block, index_map, **kw):
    return pl.BlockSpec((None,) + tuple(block), lambda *idx: (w.layer,) + tuple(index_map(*idx)), **kw)


def _proj_body(x_ref, g_ref, sh_ref, sc_ref, w_ref, o_ref):
    h = _norm_mod(x_ref[...], g_ref[...], sh_ref[0], sc_ref[0]).astype(BF16)
    o_ref[...] = jnp.dot(h, w_ref[...], preferred_element_type=F32).astype(o_ref.dtype)


def _norm_mod_proj(x, g, sh, sc, w, out_dtype):
    b, l, d = x.shape
    n = w.shape[1]
    tm = _row_tile(l, 512)
    per = l // tm
    out = pl.pallas_call(
        _proj_body,
        grid=(b * per,),
        in_specs=[pl.BlockSpec((tm, d), lambda i: (i, 0)),
                  pl.BlockSpec((1, d), lambda i: (0, 0)),
                  _mod_spec(sh, per), _mod_spec(sc, per),
                  _layer_spec(w, (d, n), lambda i: (0, 0), pipeline_mode=pl.Buffered(1))],
        out_specs=pl.BlockSpec((tm, n), lambda i: (i, 0)),
        out_shape=jax.ShapeDtypeStruct((b * l, n), out_dtype),
        compiler_params=_params("parallel"),
        name="norm_mod_proj",
    )(x.reshape(b * l, d), g.reshape(1, d), sh.table, sc.table, w.stack)
    return out.reshape(b, l, n)


def _residual(res_ref, gate_ref, g1_ref, y):
    return res_ref[...] + gate_ref[0] * _rms(y, g1_ref[...])


def _out_plain_body(a_ref, w_ref, res_ref, gate_ref, g1_ref, o_ref):
    y = jnp.dot(a_ref[...], w_ref[...], preferred_element_type=F32)
    o_ref[...] = _residual(res_ref, gate_ref, g1_ref, y)


def _out_hgrn_body(of_ref, ob_ref, gp_ref, og_ref, w_ref, res_ref, gate_ref, g1_ref, o_ref):
    o = of_ref[...].astype(F32) + ob_ref[...].astype(F32)
    parts = []
    for h in range(A_HEADS):
        oh = o[:, h * LANES:(h + 1) * LANES]
        parts.append(oh * lax.rsqrt(jnp.mean(oh * oh, axis=-1, keepdims=True) + EPS))
    gp = gp_ref[...]
    a = jnp.concatenate(parts, axis=-1) * og_ref[...] * (gp * jax.nn.sigmoid(gp))
    y = jnp.dot(a.astype(BF16), w_ref[...], preferred_element_type=F32)
    o_ref[...] = _residual(res_ref, gate_ref, g1_ref, y)


def _gelu_tanh(y):
    return 0.5 * y * (1.0 + jnp.tanh(0.7978845608028654 * (y + 0.044715 * (y * y * y))))


def _out_call(body, row_inputs, const_inputs, res, gate, g1, name):
    b, l, d = res.shape
    tm = _row_tile(l, 512)
    per = l // tm
    in_specs, args = [], []
    for arr, blk, width in row_inputs:
        in_specs.append(pl.BlockSpec((tm, width), lambda i, blk=blk: (i, blk)))
        args.append(arr.reshape(b * l, arr.shape[-1]))
    for arr in const_inputs:
        if isinstance(arr, LayerOf):
            in_specs.append(_layer_spec(arr, arr.shape, lambda i, nd=len(arr.shape): (0,) * nd))
            args.append(arr.stack)
        else:
            in_specs.append(pl.BlockSpec(arr.shape, lambda i, nd=arr.ndim: (0,) * nd))
            args.append(arr)
    in_specs += [pl.BlockSpec((tm, d), lambda i: (i, 0)),
                 _mod_spec(gate, per),
                 pl.BlockSpec((1, d), lambda i: (0, 0))]
    args += [res.reshape(b * l, d), gate.table, g1.reshape(1, d)]
    out = pl.pallas_call(
        body,
        grid=(b * per,),
        in_specs=in_specs,
        out_specs=pl.BlockSpec((tm, d), lambda i: (i, 0)),
        out_shape=jax.ShapeDtypeStruct((b * l, d), F32),
        compiler_params=_params("parallel"),
        name=name,
    )(*args)
    return out.reshape(b, l, d)


HGRN_CHUNK = 32
HGRN_SAFE_MIN = 1e-30
HGRN_Q_HEADROOM = 1e37


def _hgrn_gates(z, lb):
    f = lb + (1.0 - lb) * jax.nn.sigmoid(z)
    return jnp.maximum(f, F_MIN), 1.0 - f


def _hgrn_chunk_prep(q, v, z, lb, reverse):
    c = q.shape[0]
    half = c // 2
    fm, kk = _hgrn_gates(z, lb)
    pos = lax.broadcasted_iota(jnp.int32, (c, LANES), 0)
    if reverse:
        pos = (c - 1) - pos
    second = pos >= half
    hpos = jnp.where(second, pos - half, pos)
    ph = fm
    j = 1
    while j < half:
        ph = ph * jnp.where(hpos >= j, pltpu.roll(ph, (c - j) if reverse else j, 0), 1.0)
        j *= 2
    a_last = ph[half:half + 1] if reverse else ph[half - 1:half]
    r_last = ph[0:1] if reverse else ph[c - 1:c]
    p_mid = a_last
    p_last = a_last * r_last
    pp = jnp.where(second, ph * a_last, ph)
    r = jnp.where(second, ph, ph * (1.0 / a_last))
    kd = kk * (1.0 / r)
    q_max = jnp.max(jnp.abs(q), axis=0, keepdims=True)
    ok = jnp.where(p_mid >= HGRN_SAFE_MIN, r_last, 0.0) >= HGRN_SAFE_MIN
    ok = jnp.where(ok, p_mid * HGRN_Q_HEADROOM, -1.0) >= q_max
    return dict(qr=(q * r).astype(BF16), kd=kd.astype(BF16), qp=(q * pp).astype(BF16),
                kl=(kd * r_last).astype(BF16), vb=v.astype(BF16), dec=p_last, bad=jnp.where(ok, 0.0, 1.0))


def _hgrn_chunk_dots(p, st):
    sc = lax.dot_general(p["qr"], p["kd"], NT_DIMS, preferred_element_type=F32)
    o_st = lax.dot_general(p["qp"], st.astype(BF16), NT_DIMS, preferred_element_type=F32)
    u = lax.dot_general(p["vb"], p["kl"], TN_DIMS, preferred_element_type=F32)
    return sc, o_st, u


def _hgrn_chunk_out(p, sc, o_st, reverse):
    c = sc.shape[0]
    row = lax.broadcasted_iota(jnp.int32, (c, c), 0)
    col = lax.broadcasted_iota(jnp.int32, (c, c), 1)
    sc = jnp.where((col >= row) if reverse else (col <= row), sc, 0.0)
    return jnp.dot(sc.astype(BF16), p["vb"], preferred_element_type=F32) + o_st


def _hgrn_exact_block(q_ref, v_ref, z_ref, lb_ref, o_scr, st_scr, dr, reverse, n_tiles):
    pos = lax.broadcasted_iota(jnp.int32, (SUBLANES, LANES), 0)
    if reverse:
        pos = (SUBLANES - 1) - pos

    def earlier(x, j):
        if j == 0:
            return x
        return pltpu.roll(x, (SUBLANES - j) if reverse else j, 0)

    def later(x, j):
        return pltpu.roll(x, j if reverse else (SUBLANES - j), 0)

    def tile(i, carry):
        ti = (n_tiles - 1 - i) if reverse else i
        r0 = pl.multiple_of(ti * SUBLANES, SUBLANES)
        for h in range(A_HEADS):
            sl = slice(h * LANES, (h + 1) * LANES)
            lb = lb_ref[:, sl]
            z = z_ref[0, pl.ds(r0, SUBLANES), sl]
            q = q_ref[0, pl.ds(r0, SUBLANES), sl]
            v = v_ref[0, pl.ds(r0, SUBLANES), sl]
            fm, kk = _hgrn_gates(z, lb)
            pp = fm
            for j in (1, 2, 4):
                pp = pp * jnp.where(pos >= j, earlier(pp, j), 1.0)
            qq = jnp.where(pos <= SUBLANES - 2, later(fm, 1), 1.0)
            for j in (1, 2, 4):
                qq = qq * jnp.where(pos <= SUBLANES - 1 - j, later(qq, j), 1.0)
            dec = pp[0:1] if reverse else pp[SUBLANES - 1:SUBLANES]
            st = st_scr[dr, h]
            o = lax.dot_general((q * pp).astype(BF16), st.astype(BF16), NT_DIMS,
                                preferred_element_type=F32)
            g = fm
            for d in range(SUBLANES):
                if d == 0:
                    e = q * kk
                else:
                    if d > 1:
                        g = g * earlier(fm, d - 1)
                    e = q * g * jnp.where(pos >= d, earlier(kk, d), 0.0)
                o = o + jnp.sum(e, axis=-1, keepdims=True) * earlier(v, d)
            o_scr[pl.ds(r0, SUBLANES), sl] = o
            u = lax.dot_general(v.astype(BF16), (kk * qq).astype(BF16), TN_DIMS,
                                preferred_element_type=F32)
            st_scr[dr, h] = dec * st + u
        return carry

    lax.fori_loop(0, n_tiles, tile, 0)


def _hgrn_scan_body(qf_ref, vf_ref, zf_ref, qb_ref, vb_ref, zb_ref, lb_ref, s0_ref, of_ref, ob_ref, sfin_ref,
                    st_scr, save_scr, ox_scr, *, n_chunks):
    step = pl.program_id(1)

    @pl.when(step == 0)
    def _():
        st_scr[...] = s0_ref[0]

    save_scr[...] = st_scr[...]
    dirs = ((qf_ref, vf_ref, zf_ref, of_ref, False), (qb_ref, vb_ref, zb_ref, ob_ref, True))
    c = HGRN_CHUNK

    def chunk(i, bad):
        work, dots = [], []
        for dr, (q_ref, v_ref, z_ref, o_ref, reverse) in enumerate(dirs):
            rows = pl.ds(pl.multiple_of(((n_chunks - 1 - i) if reverse else i) * c, c), c)
            group = []
            for h in range(A_HEADS):
                sl = slice(h * LANES, (h + 1) * LANES)
                p = _hgrn_chunk_prep(q_ref[0, rows, sl], v_ref[0, rows, sl], z_ref[0, rows, sl], lb_ref[:, sl],
                                     reverse)
                bad = jnp.maximum(bad, p["bad"])
                group.append((dr, h, o_ref, rows, sl, reverse, p))
            dots += [_hgrn_chunk_dots(p, st_scr[dr, h]) for dr, h, _, _, _, _, p in group]
            work += group
        for (dr, h, _, _, _, _, p), (_, _, u) in zip(work, dots):
            st_scr[dr, h] = p["dec"] * st_scr[dr, h] + u
        for (dr, h, o_ref, rows, sl, reverse, p), (sc, o_st, _) in zip(work, dots):
            o_ref[0, rows, sl] = _hgrn_chunk_out(p, sc, o_st, reverse).astype(o_ref.dtype)
        return bad

    bad = lax.fori_loop(0, n_chunks, chunk, jnp.zeros((1, LANES), F32))

    @pl.when(jnp.max(bad) > 0.0)
    def _():
        st_scr[...] = save_scr[...]
        for dr, (q_ref, v_ref, z_ref, o_ref, reverse) in enumerate(dirs):
            _hgrn_exact_block(q_ref, v_ref, z_ref, lb_ref, ox_scr, st_scr, dr, reverse, n_chunks * c // SUBLANES)
            o_ref[0] = ox_scr[...].astype(o_ref.dtype)

    @pl.when(step == pl.num_programs(1) - 1)
    def _():
        sfin_ref[0] = st_scr[...]


def _hgrn_scan(proj, lb, s0):
    b, l, _ = proj.shape
    d = A_HEADS * LANES
    t = _row_tile(l, 512)
    nb = l // t
    fwd = lambda blk: pl.BlockSpec((1, t, d), lambda i, s: (i, s, blk))
    bwd = lambda blk: pl.BlockSpec((1, t, d), lambda i, s: (i, nb - 1 - s, blk))
    st_spec = pl.BlockSpec((1,) + s0.shape[1:], lambda i, s: (i, 0, 0, 0, 0))
    return pl.pallas_call(
        functools.partial(_hgrn_scan_body, n_chunks=t // HGRN_CHUNK),
        grid=(b, nb),
        in_specs=[fwd(0), fwd(1), fwd(3), bwd(0), bwd(1), bwd(4),
                  pl.BlockSpec((1, d), lambda i, s: (0, 0)), st_spec],
        out_specs=[fwd(0), bwd(0), st_spec],
        out_shape=[jax.ShapeDtypeStruct((b, l, d), BF16), jax.ShapeDtypeStruct((b, l, d), BF16),
                   jax.ShapeDtypeStruct(s0.shape, F32)],
        scratch_shapes=[pltpu.VMEM(s0.shape[1:], F32), pltpu.VMEM(s0.shape[1:], F32), pltpu.VMEM((t, d), F32)],
        compiler_params=_params("parallel", "arbitrary"),
        name="hgrn_scan",
    )(proj, proj, proj, proj, proj, proj, lb, s0)


def _hgrn2_mixer(x_c, x_l, ng, mod_c, mod_l, w_in, lower, out_g, w_out, g1, want_ctx):
    d = x_l.shape[-1]
    p_c = _norm_mod_proj(x_c, ng, mod_c[0], mod_c[1], w_in, F32)
    p_l = _norm_mod_proj(x_l, ng, mod_l[0], mod_l[1], w_in, F32)
    lb = lower.reshape(1, d)
    zero = jnp.zeros((x_l.shape[0], 2, A_HEADS, LANES, LANES), F32)
    ocf, ocb, s_ctx = _hgrn_scan(p_c, lb, zero)
    olf, olb, _ = _hgrn_scan(p_l, lb, s_ctx)
    o_c, o_l = (ocf, ocb), (olf, olb)
    og = out_g.reshape(1, d)
    new_l = _out_call(_out_hgrn_body, [(o_l[0], 0, d), (o_l[1], 0, d), (p_l, 2, d)], [og, w_out],
                      x_l, mod_l[2], g1, "hgrn_out")
    new_c = None
    if want_ctx:
        new_c = _out_call(_out_hgrn_body, [(o_c[0], 0, d), (o_c[1], 0, d), (p_c, 2, d)], [og, w_out],
                          x_c, mod_c[2], g1, "hgrn_out")
    return new_c, new_l


def _softmax_pv(s_list, v_list):
    mx = functools.reduce(jnp.maximum, [jnp.max(s, axis=-1, keepdims=True) for s in s_list])
    ps = [jnp.exp(s - mx) for s in s_list]
    den = functools.reduce(jnp.add, [jnp.sum(p, axis=-1, keepdims=True) for p in ps])
    acc = functools.reduce(jnp.add, [jnp.dot(p.astype(BF16), v, preferred_element_type=F32)
                                     for p, v in zip(ps, v_list)])
    return acc / den


def _na_lat_body(q_ref, k_ref, v_ref, kc_ref, vc_ref, t2_ref, o_ref, *, rows, kh, scale):
    w = GRID_W
    rb = NA_ROW_BLOCK
    nk = kh + rb
    n_rel = 2 * kh - 1
    first = lax.broadcasted_iota(jnp.int32, (rb * w, LANES), 1) < (LANES // 2)
    kc = kc_ref[0]
    vc = vc_ref[0]

    def bias_index(r, ks, m):
        r0 = jnp.clip(r - kh // 2, 0, rows - kh)
        ka = ks + 2 * m
        rel_a = ka - r + (kh - 1)
        in_a = (ka >= r0) & (ka < r0 + kh)
        in_b = (ka + 1 >= r0) & (ka + 1 < r0 + kh)
        both, only_b, only_a = rel_a, (n_rel - 1) + rel_a + 1, (2 * n_rel - 1) + rel_a
        return jnp.where(in_a, jnp.where(in_b, both, only_a), jnp.where(in_b, only_b, 3 * n_rel - 1))

    def block(i, carry):
        rq = i * rb
        ks = jnp.clip(rq - kh // 2, 0, rows - nk)
        q2 = q_ref[0, pl.ds(pl.multiple_of(rq * w, rb * w), rb * w), :] * scale
        k2 = k_ref[0, pl.ds(pl.multiple_of(ks * w, w), nk * w), :]
        v2 = v_ref[0, pl.ds(pl.multiple_of(ks * w, w), nk * w), :]
        outs = []
        for hh in range(2):
            qm = jnp.where(first if hh == 0 else jnp.logical_not(first), q2, jnp.zeros_like(q2))
            bias = jnp.concatenate(
                [jnp.concatenate([t2_ref[hh, bias_index(rq + a, ks, m)] for m in range(nk // 2)], axis=-1)
                 for a in range(rb)], axis=0)
            s_w = lax.dot_general(qm, k2, NT_DIMS, preferred_element_type=F32) + bias
            s_c = lax.dot_general(qm, kc, NT_DIMS, preferred_element_type=F32)
            outs.append(_softmax_pv([s_w, s_c], [v2, vc]))
        o_ref[0, pl.ds(pl.multiple_of(rq * w, rb * w), rb * w), :] = (
            jnp.where(first, outs[0], outs[1]).astype(o_ref.dtype))
        return carry

    lax.fori_loop(0, rows // rb, block, 0, unroll=2)


def _na_ctx_body(q_ref, k_ref, v_ref, o_ref, *, scale):
    n = q_ref.shape[1]
    first = lax.broadcasted_iota(jnp.int32, (n, LANES), 1) < (LANES // 2)
    q2 = q_ref[0] * scale
    k2 = k_ref[0]
    v2 = v_ref[0]
    outs = []
    for hh in range(2):
        qm = jnp.where(first if hh == 0 else jnp.logical_not(first), q2, jnp.zeros_like(q2))
        s = lax.dot_general(qm, k2, NT_DIMS, preferred_element_type=F32)
        outs.append(_softmax_pv([s], [v2]))
    o_ref[0] = jnp.where(first, outs[0], outs[1]).astype(o_ref.dtype)


def _na_bias_table(rpb, kh, kw):
    w = jnp.arange(GRID_W)[:, None]
    c = jnp.arange(GRID_W)[None, :]
    c0 = jnp.clip(w - kw // 2, 0, GRID_W - kw)
    inside = (c >= c0) & (c < c0 + kw)
    pad = GRID_W - kw
    padded = jnp.pad(rpb.astype(F32), ((0, 0), (0, 0), (pad, pad)))
    shifted = jnp.stack([padded[:, :, pad + kw - 1 - q:pad + kw - 1 - q + GRID_W] for q in range(GRID_W)], axis=2)
    t = jnp.where(inside[None, None], shifted, NEG_BIG)
    off = jnp.full_like(t, NEG_BIG)
    pair = lambda a, b: jnp.concatenate([a, b], axis=-1)
    return jnp.concatenate([pair(t[:, :-1], t[:, 1:]), pair(off, t), pair(t, off), pair(off, off)[:, :1]], axis=1)


def _na_mixer(x_c, x_l, ng, mod_c, mod_l, w_qkv, rpb, w_out, g1, want_ctx):
    b, l, d = x_l.shape
    lc = x_c.shape[1]
    heads = rpb.shape[0]
    na_rows, na_cols = (rpb.shape[1] + 1) // 2, (rpb.shape[2] + 1) // 2
    dh = d // heads
    assert 2 * dh == LANES and l % GRID_W == 0
    rows = l // GRID_W
    kh = min(na_rows, rows)
    assert kh == na_rows and (kh + NA_ROW_BLOCK) % 2 == 0
    assert rows % NA_ROW_BLOCK == 0 and rows >= kh + NA_ROW_BLOCK
    scale = dh ** -0.5
    nhp = heads // 2
    qkv_c = _norm_mod_proj(x_c, ng, mod_c[0], mod_c[1], w_qkv, BF16)
    qkv_l = _norm_mod_proj(x_l, ng, mod_l[0], mod_l[1], w_qkv, BF16)
    t2 = _na_bias_table(rpb, kh, na_cols)
    o_l = pl.pallas_call(
        functools.partial(_na_lat_body, rows=rows, kh=kh, scale=scale),
        grid=(b, nhp),
        in_specs=[pl.BlockSpec((1, l, LANES), lambda i, p: (i, 0, p)),
                  pl.BlockSpec((1, l, LANES), lambda i, p: (i, 0, nhp + p)),
                  pl.BlockSpec((1, l, LANES), lambda i, p: (i, 0, 2 * nhp + p)),
                  pl.BlockSpec((1, lc, LANES), lambda i, p: (i, 0, nhp + p)),
                  pl.BlockSpec((1, lc, LANES), lambda i, p: (i, 0, 2 * nhp + p)),
                  pl.BlockSpec((2,) + t2.shape[1:], lambda i, p: (p, 0, 0, 0))],
        out_specs=pl.BlockSpec((1, l, LANES), lambda i, p: (i, 0, p)),
        out_shape=jax.ShapeDtypeStruct((b, l, d), BF16),
        compiler_params=_params("parallel", "parallel"),
        name="na_latent",
    )(qkv_l, qkv_l, qkv_l, qkv_c, qkv_c, t2)
    new_l = _out_call(_out_plain_body, [(o_l, 0, d)], [w_out], x_l, mod_l[2], g1, "na_out")
    new_c = None
    if want_ctx:
        o_c = pl.pallas_call(
            functools.partial(_na_ctx_body, scale=scale),
            grid=(b, nhp),
            in_specs=[pl.BlockSpec((1, lc, LANES), lambda i, p: (i, 0, p)),
                      pl.BlockSpec((1, lc, LANES), lambda i, p: (i, 0, nhp + p)),
                      pl.BlockSpec((1, lc, LANES), lambda i, p: (i, 0, 2 * nhp + p))],
            out_specs=pl.BlockSpec((1, lc, LANES), lambda i, p: (i, 0, p)),
            out_shape=jax.ShapeDtypeStruct((b, lc, d), BF16),
            compiler_params=_params("parallel", "parallel"),
            name="na_context",
        )(qkv_c, qkv_c, qkv_c)
        new_c = _out_call(_out_plain_body, [(o_c, 0, d)], [w_out], x_c, mod_c[2], g1, "na_out")
    return new_c, new_l


S5_GROUP_BLOCK = 8


def _s5_kernel_body(c_ref, w_ref, o_ref):
    for i in range(c_ref.shape[0]):
        o_ref[i] = jnp.dot(c_ref[i], w_ref[i], preferred_element_type=F32, precision=lax.Precision.HIGHEST)


def _s5_impulse(cmat, wmat):
    n = cmat.shape[0]
    gb = S5_GROUP_BLOCK
    return pl.pallas_call(
        _s5_kernel_body,
        grid=(n // gb,),
        in_specs=[pl.BlockSpec((gb,) + cmat.shape[1:], lambda i: (i, 0, 0)),
                  pl.BlockSpec((gb,) + wmat.shape[1:], lambda i: (i, 0, 0))],
        out_specs=pl.BlockSpec((gb, cmat.shape[1], wmat.shape[2]), lambda i: (i, 0, 0)),
        out_shape=jax.ShapeDtypeStruct((n, cmat.shape[1], wmat.shape[2]), F32),
        compiler_params=_params("parallel"),
        name="s5_impulse",
    )(cmat, wmat)


def _s5_operators(lam_re, lam_im, log_dt, b_re, b_im, c_re, c_im, t):
    _, g, p = lam_re.shape
    cg = b_re.shape[-1]
    lam = lax.complex(lam_re.astype(F32), lam_im.astype(F32))
    ldt = lam * jnp.exp(log_dt.astype(F32))[..., None]
    a = jnp.exp(ldt)
    bbar = ((a - 1.0) / lam)[..., None] * lax.complex(b_re.astype(F32), b_im.astype(F32))
    cm = lax.complex(c_re.astype(F32), c_im.astype(F32))
    apow = jnp.exp(ldt[..., None] * jnp.arange(t + 1, dtype=F32))
    w = apow[..., :t, None] * bbar[:, :, :, None, :]
    wmat = jnp.concatenate([jnp.real(w), jnp.imag(w)], axis=2).reshape(2 * g, 2 * p, t * cg)
    cmat = jnp.concatenate([jnp.real(cm), -jnp.imag(cm)], axis=-1).reshape(2 * g, cg, 2 * p)
    k = _s5_impulse(cmat, wmat).reshape(2, g, cg, t, cg)
    k = jnp.transpose(k, (0, 1, 3, 4, 2))
    kfull = jnp.concatenate([k[1, :, :0:-1], (k[0, :, :1] + k[1, :, :1]), k[0, :, 1:]], axis=1)
    lagvec = jnp.transpose(kfull, (0, 2, 1, 3)).reshape(g, cg, (2 * t - 1) * cg).astype(BF16)
    toep = jnp.stack([lagvec[:, :, (t - 1 - s) * cg:(2 * t - 1 - s) * cg] for s in range(t)], axis=1)
    toep = toep.reshape(g, t * cg, t * cg)

    def state_in(wd, flip):
        wd = wd[:, :, ::-1] if flip else wd
        m = jnp.transpose(wd, (0, 2, 3, 1)).reshape(g, t * cg, p)
        return jnp.concatenate([jnp.real(m), jnp.imag(m)], axis=-1)

    def state_out(cd, pw):
        n = cd[:, :, :, None] * pw[:, None, :, :]
        n = jnp.transpose(n, (0, 2, 3, 1)).reshape(g, p, t * cg)
        return jnp.concatenate([jnp.real(n), -jnp.imag(n)], axis=1)

    m_f = state_in(w[0], True)
    m_b = state_in(w[1], False)
    n_f = state_out(cm[0], apow[0][..., 1:])
    n_b = state_out(cm[1], apow[1][..., :0:-1])
    at = apow[..., t]
    a1 = jnp.concatenate([jnp.real(at), jnp.real(at)], axis=-1)
    a2 = jnp.concatenate([-jnp.imag(at), jnp.imag(at)], axis=-1)
    bf = lambda m: m.astype(BF16)
    return dict(toep=bf(toep), m_f=bf(m_f), m_b=bf(m_b), n_f=bf(n_f), n_b=bf(n_b), a1=a1, a2=a2)


def _s5_body(u_ref, dsk_ref, toep_ref, mf_ref, mb_ref, nf_ref, nb_ref, a1_ref, a2_ref, x0_ref,
             y_ref, xfin_ref, z_scr, zs_scr, xin_scr):
    gb, nc = u_ref.shape[1], u_ref.shape[2]
    half = LANES // 2
    ubs = [u_ref[0, gi].astype(BF16) for gi in range(gb)]
    for dr, m_ref in enumerate((mf_ref, mb_ref)):
        z = jnp.stack([jnp.dot(ubs[gi], m_ref[gi], preferred_element_type=F32) for gi in range(gb)])
        z_scr[dr] = jnp.swapaxes(z, 0, 1)

    swap = lambda t: pltpu.roll(t, half, t.ndim - 1)
    for dr in range(2):
        zs_scr[dr] = swap(z_scr[dr].reshape(nc * gb, LANES)).reshape(nc, gb, LANES)

    coef = [(a1_ref[dr], a2_ref[dr]) for dr in range(2)]

    def scan(j, xs):
        out = []
        for dr in range(2):
            jj = j if dr == 0 else nc - 1 - j
            x, xw = xs[2 * dr], xs[2 * dr + 1]
            a1, a2 = coef[dr]
            xin_scr[dr, jj] = x
            out.append(a1 * x + a2 * xw + z_scr[dr, jj])
            out.append(a1 * xw - a2 * x + zs_scr[dr, jj])
        return tuple(out)

    x0f, x0b = x0_ref[0, 0], x0_ref[0, 1]
    xf, _, xb, _ = lax.fori_loop(0, nc, scan, (x0f, swap(x0f), x0b, swap(x0b)))
    xfin_ref[0, 0] = xf
    xfin_ref[0, 1] = xb
    xin = [jnp.swapaxes(xin_scr[dr], 0, 1).astype(BF16) for dr in range(2)]
    for gi in range(gb):
        u = u_ref[0, gi]
        y = jnp.dot(ubs[gi], toep_ref[gi], preferred_element_type=F32) + u * dsk_ref[gi]
        y = y + jnp.dot(xin[0][gi], nf_ref[gi], preferred_element_type=F32)
        y = y + jnp.dot(xin[1][gi], nb_ref[gi], preferred_element_type=F32)
        y_ref[0, gi] = y


def _piece_transpose(src, piece_of, cg):
    n = len(src)
    out = [None] * n
    for s in range(n):
        c = src[s % n]
        for p in range(1, n):
            c = jnp.where(piece_of == p, src[(p + s) % n], c)
        r = c if s == 0 else pltpu.roll(c, s * cg, 1)
        for p in range(n):
            out[p] = r if s == 0 else jnp.where(piece_of == (p + s) % n, r, out[p])
    return out


def _to_chunks(h_scr, o_ref, t, cg):
    ncb = h_scr.shape[1] // t
    pieces = LANES // cg
    piece_of = lax.broadcasted_iota(jnp.int32, (ncb, LANES), 1) // cg

    def lane_block(lb, carry):
        for tq in range(t // pieces):
            steps = [h_scr[lb, pl.ds(tq * pieces + tr, ncb, stride=t), :] for tr in range(pieces)]
            for gl, col in enumerate(_piece_transpose(steps, piece_of, cg)):
                o_ref[0, lb * pieces + gl, :, tq * LANES:(tq + 1) * LANES] = col
        return carry

    lax.fori_loop(0, h_scr.shape[0], lane_block, 0)


def _from_chunks(y_ref, y_scr, t, cg):
    ncb = y_ref.shape[2]
    pieces = LANES // cg
    piece_of = lax.broadcasted_iota(jnp.int32, (ncb, LANES), 1) // cg

    def lane_block(lb, carry):
        for tq in range(t // pieces):
            cols = [y_ref[0, lb * pieces + gl, :, tq * LANES:(tq + 1) * LANES] for gl in range(pieces)]
            for tr, step in enumerate(_piece_transpose(cols, piece_of, cg)):
                y_scr[lb, pl.ds(tq * pieces + tr, ncb, stride=t), :] = step
        return carry

    lax.fori_loop(0, y_scr.shape[0], lane_block, 0)


def _norm_chunks_body(x_ref, g_ref, sh_ref, sc_ref, o_ref, h_scr, *, t, cg):
    h = _norm_mod(x_ref[...], g_ref[...], sh_ref[0], sc_ref[0])
    for lb in range(h_scr.shape[0]):
        h_scr[lb] = h[:, lb * LANES:(lb + 1) * LANES]
    _to_chunks(h_scr, o_ref, t, cg)


def _norm_mod_chunks(x, g, sh, sc, groups, t):
    b, l, d = x.shape
    cg = d // groups
    tm = _row_tile(l, 512)
    per = l // tm
    assert tm % t == 0 and (tm // t) % SUBLANES == 0 and LANES % cg == 0 and t % (LANES // cg) == 0
    return pl.pallas_call(
        functools.partial(_norm_chunks_body, t=t, cg=cg),
        grid=(b * per,),
        in_specs=[pl.BlockSpec((tm, d), lambda i: (i, 0)),
                  pl.BlockSpec((1, d), lambda i: (0, 0)),
                  _mod_spec(sh, per), _mod_spec(sc, per)],
        out_specs=pl.BlockSpec((1, groups, tm // t, t * cg), lambda i: (i // per, 0, i % per, 0)),
        out_shape=jax.ShapeDtypeStruct((b, groups, l // t, t * cg), F32),
        scratch_shapes=[pltpu.VMEM((d // LANES, tm, LANES), F32)],
        compiler_params=_params("parallel"),
        name="norm_mod_chunks",
    )(x.reshape(b * l, d), g.reshape(1, d), sh.table, sc.table)


def _glu_chunks_body(y_ref, w_ref, res_ref, gate_ref, g1_ref, o_ref, y_scr, *, t, cg):
    _from_chunks(y_ref, y_scr, t, cg)
    d = o_ref.shape[-1]
    y = jnp.concatenate([y_scr[lb] for lb in range(y_scr.shape[0])], axis=-1)
    a = _gelu_tanh(y).astype(BF16)
    ag = jnp.dot(a, w_ref[...], preferred_element_type=F32)
    o_ref[...] = _residual(res_ref, gate_ref, g1_ref, ag[:, :d] * jax.nn.sigmoid(ag[:, d:]))


def _glu_out_chunks(y, w_glu, res, gate, g1, t):
    b, l, d = res.shape
    groups = y.shape[1]
    cg = d // groups
    tm = _row_tile(l, 512)
    per = l // tm
    out = pl.pallas_call(
        functools.partial(_glu_chunks_body, t=t, cg=cg),
        grid=(b * per,),
        in_specs=[pl.BlockSpec((1, groups, tm // t, t * cg), lambda i: (i // per, 0, i % per, 0)),
                  _layer_spec(w_glu, w_glu.shape, lambda i: (0, 0)),
                  pl.BlockSpec((tm, d), lambda i: (i, 0)),
                  _mod_spec(gate, per),
                  pl.BlockSpec((1, d), lambda i: (0, 0))],
        out_specs=pl.BlockSpec((tm, d), lambda i: (i, 0)),
        out_shape=jax.ShapeDtypeStruct((b * l, d), F32),
        scratch_shapes=[pltpu.VMEM((d // LANES, tm, LANES), F32)],
        compiler_params=_params("parallel"),
        name="s5_glu_out",
    )(y, w_glu.stack, res.reshape(b * l, d), gate.table, g1.reshape(1, d))
    return out.reshape(b, l, d)


def _s5_scan(u, ops, dsk, x0):
    b, g, nc, tc = u.shape
    gb = S5_GROUP_BLOCK
    per_g = lambda shape: pl.BlockSpec((gb,) + shape, lambda j, i: (j,) + (0,) * len(shape))
    y, xfin = pl.pallas_call(
        _s5_body,
        grid=(g // gb, b),
        in_specs=[pl.BlockSpec((1, gb, nc, tc), lambda j, i: (i, j, 0, 0)),
                  per_g((1, tc)), per_g((tc, tc)), per_g((tc, LANES)), per_g((tc, LANES)),
                  per_g((LANES, tc)), per_g((LANES, tc)),
                  pl.BlockSpec((2, gb, LANES), lambda j, i: (0, j, 0)),
                  pl.BlockSpec((2, gb, LANES), lambda j, i: (0, j, 0)),
                  pl.BlockSpec((1, 2, gb, LANES), lambda j, i: (i, 0, j, 0))],
        out_specs=[pl.BlockSpec((1, gb, nc, tc), lambda j, i: (i, j, 0, 0)),
                   pl.BlockSpec((1, 2, gb, LANES), lambda j, i: (i, 0, j, 0))],
        out_shape=[jax.ShapeDtypeStruct((b, g, nc, tc), F32), jax.ShapeDtypeStruct((b, 2, g, LANES), F32)],
        scratch_shapes=[pltpu.VMEM((2, nc, gb, LANES), F32)] * 3,
        compiler_params=_params("parallel", "parallel"),
        name="s5_scan",
    )(u, dsk, ops["toep"], ops["m_f"], ops["m_b"], ops["n_f"], ops["n_b"], ops["a1"], ops["a2"], x0)
    return y, xfin


def _s5_mixer(x_c, x_l, ng, mod_c, mod_l, lam_re, lam_im, log_dt, b_re, b_im, c_re, c_im, d_skip, w_glu, g1,
              want_ctx):
    b, l, d = x_l.shape
    g, p = lam_re.shape[1], lam_re.shape[2]
    cg = d // g
    assert 2 * p == LANES and g % S5_GROUP_BLOCK == 0
    ops = _s5_operators(lam_re, lam_im, log_dt, b_re, b_im, c_re, c_im, S5_CHUNK)
    dsk = jnp.tile(d_skip.astype(F32).reshape(g, 1, cg), (1, 1, S5_CHUNK))
    u_c = _norm_mod_chunks(x_c, ng, mod_c[0], mod_c[1], g, S5_CHUNK)
    u_l = _norm_mod_chunks(x_l, ng, mod_l[0], mod_l[1], g, S5_CHUNK)
    y_c, x_ctx = _s5_scan(u_c, ops, dsk, jnp.zeros((b, 2, g, LANES), F32))
    y_l, _ = _s5_scan(u_l, ops, dsk, x_ctx)
    new_l = _glu_out_chunks(y_l, w_glu, x_l, mod_l[2], g1, S5_CHUNK)
    new_c = _glu_out_chunks(y_c, w_glu, x_c, mod_c[2], g1, S5_CHUNK) if want_ctx else None
    return new_c, new_l


FFN_HALO = 16
FFN_TN = 256
FFN_TILES_PER_STEP = 11
FFN_ROW_SPLIT = 128


def _ffn_body(xp_ref, x_ref, xn_ref, g2_ref, sh_ref, sc_ref, *rest, per, n_tiles, tps):
    w_refs = rest[:2 * tps]
    cw_ref, cb_ref, wo_ref, gate_ref, g3_ref, o_ref, h_scr, gat_scr = rest[2 * tps:]
    i, j = pl.program_id(0), pl.program_id(1)
    tm = x_ref.shape[0]
    hl = FFN_HALO
    f = wo_ref.shape[0]
    n_steps = pl.cdiv(n_tiles, tps)
    when = (lambda cond: lambda fn: fn()) if n_steps == 1 else pl.when

    @when(j == 0)
    def _():
        g, sh, sc = g2_ref[...], sh_ref[0], sc_ref[0]
        keep_p = ((i % per) != 0).astype(F32)
        keep_n = ((i % per) != per - 1).astype(F32)
        h_scr[0:hl] = (_norm_mod(xp_ref[...], g, sh, sc) * keep_p).astype(BF16)
        h_scr[hl:hl + tm] = _norm_mod(x_ref[...], g, sh, sc).astype(BF16)
        h_scr[hl + tm:] = (_norm_mod(xn_ref[...], g, sh, sc) * keep_n).astype(BF16)

    sub = FFN_ROW_SPLIT if tm % FFN_ROW_SPLIT == 0 else tm
    ns = tm // sub
    edge = SUBLANES
    tn = w_refs[0].shape[1]

    def pieces(w_ref):
        out = []
        for s in range(ns):
            lo = hl + s * sub - (hl if s == 0 else 0)
            hi = hl + (s + 1) * sub + (hl if s == ns - 1 else 0)
            out.append(jnp.dot(h_scr[lo:hi], w_ref[...], preferred_element_type=F32))
        return out

    def conv(p, s, cw, cb):
        base = hl if s == 0 else 0
        left = p[s][hl - edge:hl] if s == 0 else p[s - 1][-edge:]
        right = p[s][base + sub:base + sub + edge] if s == ns - 1 else p[s + 1][:edge]
        ext = jnp.concatenate([left, p[s][base:base + sub], right], axis=0)
        um = pltpu.roll(ext, 1, 0)[edge:edge + sub]
        up = pltpu.roll(ext, sub + 2 * edge - 1, 0)[edge:edge + sub]
        return cb + um * cw[0:1] + ext[edge:edge + sub] * cw[1:2] + up * cw[2:3]

    def tile(k):
        col = k * tn if n_steps == 1 else pl.multiple_of((j * tps + k) * tn, tn)
        cols_a, cols_v = pl.ds(col, tn), pl.ds(pl.multiple_of(f + col, tn), tn)
        pa, pv = pieces(w_refs[2 * k]), pieces(w_refs[2 * k + 1])
        for s in range(ns):
            a = conv(pa, s, cw_ref[:, cols_a], cb_ref[:, cols_a])
            v = conv(pv, s, cw_ref[:, cols_v], cb_ref[:, cols_v])
            gat_scr[s * sub:(s + 1) * sub, cols_a] = (a * jax.nn.sigmoid(a) * v).astype(BF16)

    always = n_tiles - tps * (n_steps - 1)
    for k in range(tps):
        if k < always:
            tile(k)
        else:
            pl.when(j * tps + k < n_tiles)(functools.partial(tile, k))

    @when(j == n_steps - 1)
    def _():
        y = jnp.dot(gat_scr[...], wo_ref[...], preferred_element_type=F32)
        o_ref[...] = x_ref[...] + gate_ref[0] * _rms(y, g3_ref[...])


def _conv_ffn_block(x, ng2, mod, w_in, conv_w, conv_b, w_out, ng3):
    b, l, d = x.shape
    f = w_out.shape[0]
    tn, hl = FFN_TN, FFN_HALO
    assert f % tn == 0 and conv_w.shape[0] == 3 and w_in.shape[1] == 2 * f
    nj = f // tn
    tm = _row_tile(l, 1024)
    per = l // tm
    hb = tm // hl
    last_hb = b * l // hl - 1
    cb = conv_b.reshape(1, 2 * f)
    x2 = x.reshape(b * l, d)
    tps = min(FFN_TILES_PER_STEP, nj)
    once = dict(pipeline_mode=pl.Buffered(1)) if tps == nj else {}
    w_specs = []
    for k in range(tps):
        w_specs.append(_layer_spec(w_in, (d, tn), lambda i, j, k=k: (0, jnp.minimum(j * tps + k, nj - 1)), **once))
        w_specs.append(_layer_spec(w_in, (d, tn), lambda i, j, k=k: (0, nj + jnp.minimum(j * tps + k, nj - 1)),
                                   **once))
    whole = lambda arr: pl.BlockSpec(arr.shape, lambda i, j: (0,) * arr.ndim)
    out = pl.pallas_call(
        functools.partial(_ffn_body, per=per, n_tiles=nj, tps=tps),
        grid=(b * per, pl.cdiv(nj, tps)),
        in_specs=[pl.BlockSpec((hl, d), lambda i, j: (jnp.maximum(i * hb - 1, 0), 0)),
                  pl.BlockSpec((tm, d), lambda i, j: (i, 0)),
                  pl.BlockSpec((hl, d), lambda i, j: (jnp.minimum((i + 1) * hb, last_hb), 0)),
                  pl.BlockSpec((1, d), lambda i, j: (0, 0)),
                  _mod_spec(mod[3], per), _mod_spec(mod[4], per),
                  *w_specs, whole(conv_w), whole(cb),
                  _layer_spec(w_out, (f, d), lambda i, j: (0, 0), pipeline_mode=pl.Buffered(1)),
                  _mod_spec(mod[5], per),
                  pl.BlockSpec((1, d), lambda i, j: (0, 0))],
        out_specs=pl.BlockSpec((tm, d), lambda i, j: (i, 0)),
        out_shape=jax.ShapeDtypeStruct((b * l, d), F32),
        scratch_shapes=[pltpu.VMEM((tm + 2 * hl, d), BF16), pltpu.VMEM((tm, f), BF16)],
        compiler_params=_params("parallel", "arbitrary"),
        name="conv_ffn",
    )(x2, x2, x2, ng2.reshape(1, d), mod[3].table, mod[4].table, *([w_in.stack] * (2 * tps)), conv_w, cb,
      w_out.stack,
      mod[5].table, ng3.reshape(1, d))
    return out.reshape(b, l, d)


def kernel(x, c, ctx, c_ctx, w_mod, b_mod, norm_g, a_w_in, a_lower_logits, a_out_g, a_w_out, b_w_qkv, b_rpb,
           b_w_out, c_lam_re, c_lam_im, c_log_dt, c_b_re, c_b_im, c_c_re, c_c_im, c_d, c_w_glu, f_w_in,
           f_conv_w, f_conv_b, f_w_out):
    bsz, _, d = x.shape
    depth = w_mod.shape[0]
    p = jax.nn.softmax(a_lower_logits.astype(F32), axis=0)
    lower = jnp.cumsum(p, axis=0) - p[0]
    n_rows = -(-(bsz + 1) // SUBLANES) * SUBLANES
    c_rows = jnp.zeros((n_rows, d), F32).at[:bsz].set(c).at[bsz].set(c_ctx)
    table = _modulation(c_rows, w_mod, b_mod).reshape(depth * n_rows * N_MOD, 1, d)
    wa_in, wa_out, wb_qkv, wb_out, wc_glu, wf_in, wf_out = (
        w.astype(BF16) for w in (a_w_in, a_w_out, b_w_qkv, b_w_out, c_w_glu, f_w_in, f_w_out))
    lat, cx = x, ctx
    for i in range(depth):
        kind, j = i % N_MIXERS, i // N_MIXERS
        last = i == depth - 1
        mod_l = [ModRow(table, i * n_rows * N_MOD + k, N_MOD) for k in range(N_MOD)]
        mod_c = [ModRow(table, (i * n_rows + bsz) * N_MOD + k, 0) for k in range(N_MOD)]
        ng = norm_g[i]
        if kind == 0:
            cx1, lat = _hgrn2_mixer(cx, lat, ng[0], mod_c, mod_l, LayerOf(wa_in, j), lower[i], a_out_g[j],
                                    LayerOf(wa_out, j), ng[1], not last)
        elif kind == 1:
            cx1, lat = _na_mixer(cx, lat, ng[0], mod_c, mod_l, LayerOf(wb_qkv, j), b_rpb[j], LayerOf(wb_out, j),
                                 ng[1], not last)
        else:
            cx1, lat = _s5_mixer(cx, lat, ng[0], mod_c, mod_l, c_lam_re[j], c_lam_im[j], c_log_dt[j], c_b_re[j],
                                 c_b_im[j], c_c_re[j], c_c_im[j], c_d[j], LayerOf(wc_glu, j), ng[1], not last)
        ffn = (LayerOf(wf_in, i), f_conv_w[i], f_conv_b[i], LayerOf(wf_out, i), ng[3])
        lat = _conv_ffn_block(lat, ng[2], mod_l, *ffn)
        if not last:
            cx = _conv_ffn_block(cx1, ng[2], mod_c, *ffn)
    return lat
```

```python
import functools
from typing import NamedTuple

import jax
import jax.numpy as jnp
from jax import lax
from jax.experimental import pallas as pl
from jax.experimental.pallas import tpu as pltpu

F32 = jnp.float32
BF16 = jnp.bfloat16

EPS = 1e-6
F_MIN = 1e-30
N_MOD = 6
N_MIXERS = 3
A_HEADS = 8
GRID_W = 64
S5_CHUNK = 16
NEG_BIG = -1e30
NA_ROW_BLOCK = 4
SUBLANES = 8
LANES = 128
VMEM_LIMIT_BYTES = 56 * 1024 * 1024

NT_DIMS = (((1,), (1,)), ((), ()))
TN_DIMS = (((0,), (0,)), ((), ()))


def _params(*sem):
    return pltpu.CompilerParams(dimension_semantics=sem, vmem_limit_bytes=VMEM_LIMIT_BYTES)


def _row_tile(n, want):
    t = min(n, want)
    assert n % t == 0, (n, t)
    return t


def _rms(y, g):
    return y * lax.rsqrt(jnp.mean(y * y, axis=-1, keepdims=True) + EPS) * g


def _mod_body(c_ref, w_ref, b_ref, o_ref):
    c = c_ref[...]
    s = (c * jax.nn.sigmoid(c)).astype(BF16)
    o_ref[0] = jnp.dot(s, w_ref[0].astype(BF16), preferred_element_type=F32) + b_ref[0]


def _modulation(c_rows, w_mod, b_mod):
    depth, d, n = w_mod.shape
    r = c_rows.shape[0]
    tn = n // 4
    return pl.pallas_call(
        _mod_body,
        grid=(depth, n // tn),
        in_specs=[pl.BlockSpec((r, d), lambda i, j: (0, 0)),
                  pl.BlockSpec((1, d, tn), lambda i, j: (i, 0, j)),
                  pl.BlockSpec((1, 1, tn), lambda i, j: (i, 0, j))],
        out_specs=pl.BlockSpec((1, r, tn), lambda i, j: (i, 0, j)),
        out_shape=jax.ShapeDtypeStruct((depth, r, n), F32),
        compiler_params=_params("parallel", "parallel"),
        name="adaln_mod",
    )(c_rows, w_mod, b_mod.reshape(depth, 1, n))


def _norm_mod(x, g, sh, sc):
    return _rms(x, g) * (1.0 + sc) + sh


class ModRow(NamedTuple):
    table: jax.Array
    first: int
    step: int


def _mod_spec(m, per):
    return pl.BlockSpec((1, 1, m.table.shape[-1]), lambda *idx: (m.first + m.step * (idx[0] // per), 0, 0))


class LayerOf(NamedTuple):
    stack: jax.Array
    layer: int

    @property
    def shape(self):
        return self.stack.shape[1:]


def _layer_spec(w, block, index_map, **kw):
    return pl.BlockSpec((None,) + tuple(block), lambda *idx: (w.layer,) + tuple(index_map(*idx)), **kw)


def _proj_body(x_ref, g_ref, sh_ref, sc_ref, w_ref, o_ref):
    h = _norm_mod(x_ref[...], g_ref[...], sh_ref[0], sc_ref[0]).astype(BF16)
    o_ref[...] = jnp.dot(h, w_ref[...], preferred_element_type=F32).astype(o_ref.dtype)


def _norm_mod_proj(x, g, sh, sc, w, out_dtype):
    b, l, d = x.shape
    n = w.shape[1]
    tm = _row_tile(l, 512)
    per = l // tm
    out = pl.pallas_call(
        _proj_body,
        grid=(b * per,),
        in_specs=[pl.BlockSpec((tm, d), lambda i: (i, 0)),
                  pl.BlockSpec((1, d), lambda i: (0, 0)),
                  _mod_spec(sh, per), _mod_spec(sc, per),
                  _layer_spec(w, (d, n), lambda i: (0, 0), pipeline_mode=pl.Buffered(1))],
        out_specs=pl.BlockSpec((tm, n), lambda i: (i, 0)),
        out_shape=jax.ShapeDtypeStruct((b * l, n), out_dtype),
        compiler_params=_params("parallel"),
        name="norm_mod_proj",
    )(x.reshape(b * l, d), g.reshape(1, d), sh.table, sc.table, w.stack)
    return out.reshape(b, l, n)


def _residual(res_ref, gate_ref, g1_ref, y):
    return res_ref[...] + gate_ref[0] * _rms(y, g1_ref[...])


def _out_plain_body(a_ref, w_ref, res_ref, gate_ref, g1_ref, o_ref):
    y = jnp.dot(a_ref[...], w_ref[...], preferred_element_type=F32)
    o_ref[...] = _residual(res_ref, gate_ref, g1_ref, y)


def _out_hgrn_body(of_ref, ob_ref, gp_ref, og_ref, w_ref, res_ref, gate_ref, g1_ref, o_ref):
    o = of_ref[...].astype(F32) + ob_ref[...].astype(F32)
    parts = []
    for h in range(A_HEADS):
        oh = o[:, h * LANES:(h + 1) * LANES]
        parts.append(oh * lax.rsqrt(jnp.mean(oh * oh, axis=-1, keepdims=True) + EPS))
    gp = gp_ref[...]
    a = jnp.concatenate(parts, axis=-1) * og_ref[...] * (gp * jax.nn.sigmoid(gp))
    y = jnp.dot(a.astype(BF16), w_ref[...], preferred_element_type=F32)
    o_ref[...] = _residual(res_ref, gate_ref, g1_ref, y)


def _gelu_tanh(y):
    return 0.5 * y * (1.0 + jnp.tanh(0.7978845608028654 * (y + 0.044715 * (y * y * y))))


def _out_call(body, row_inputs, const_inputs, res, gate, g1, name):
    b, l, d = res.shape
    tm = _row_tile(l, 512)
    per = l // tm
    in_specs, args = [], []
    for arr, blk, width in row_inputs:
        in_specs.append(pl.BlockSpec((tm, width), lambda i, blk=blk: (i, blk)))
        args.append(arr.reshape(b * l, arr.shape[-1]))
    for arr in const_inputs:
        if isinstance(arr, LayerOf):
            in_specs.append(_layer_spec(arr, arr.shape, lambda i, nd=len(arr.shape): (0,) * nd))
            args.append(arr.stack)
        else:
            in_specs.append(pl.BlockSpec(arr.shape, lambda i, nd=arr.ndim: (0,) * nd))
            args.append(arr)
    in_specs += [pl.BlockSpec((tm, d), lambda i: (i, 0)),
                 _mod_spec(gate, per),
                 pl.BlockSpec((1, d), lambda i: (0, 0))]
    args += [res.reshape(b * l, d), gate.table, g1.reshape(1, d)]
    out = pl.pallas_call(
        body,
        grid=(b * per,),
        in_specs=in_specs,
        out_specs=pl.BlockSpec((tm, d), lambda i: (i, 0)),
        out_shape=jax.ShapeDtypeStruct((b * l, d), F32),
        compiler_params=_params("parallel"),
        name=name,
    )(*args)
    return out.reshape(b, l, d)


HGRN_CHUNK = 32
HGRN_SAFE_MIN = 1e-30
HGRN_Q_HEADROOM = 1e37


def _hgrn_gates(z, lb):
    f = lb + (1.0 - lb) * jax.nn.sigmoid(z)
    return jnp.maximum(f, F_MIN), 1.0 - f


def _hgrn_chunk_prep(q, v, z, lb, reverse):
    c = q.shape[0]
    half = c // 2
    fm, kk = _hgrn_gates(z, lb)
    pos = lax.broadcasted_iota(jnp.int32, (c, LANES), 0)
    if reverse:
        pos = (c - 1) - pos
    second = pos >= half
    hpos = jnp.where(second, pos - half, pos)
    ph = fm
    j = 1
    while j < half:
        ph = ph * jnp.where(hpos >= j, pltpu.roll(ph, (c - j) if reverse else j, 0), 1.0)
        j *= 2
    a_last = ph[half:half + 1] if reverse else ph[half - 1:half]
    r_last = ph[0:1] if reverse else ph[c - 1:c]
    p_mid = a_last
    p_last = a_last * r_last
    pp = jnp.where(second, ph * a_last, ph)
    r = jnp.where(second, ph, ph * (1.0 / a_last))
    kd = kk * (1.0 / r)
    q_max = jnp.max(jnp.abs(q), axis=0, keepdims=True)
    ok = jnp.where(p_mid >= HGRN_SAFE_MIN, r_last, 0.0) >= HGRN_SAFE_MIN
    ok = jnp.where(ok, p_mid * HGRN_Q_HEADROOM, -1.0) >= q_max
    return dict(qr=(q * r).astype(BF16), kd=kd.astype(BF16), qp=q * pp, kl=kd * r_last, vb=v.astype(BF16),
                dec=p_last, bad=jnp.where(ok, 0.0, 1.0))


def _hgrn_pair_dots(p1, p2, st):
    bf = lambda t: t.astype(BF16)
    sc1 = lax.dot_general(p1["qr"], p1["kd"], NT_DIMS, preferred_element_type=F32)
    sc2 = lax.dot_general(p2["qr"], p2["kd"], NT_DIMS, preferred_element_type=F32)
    cross = lax.dot_general(bf(p2["qp"]), bf(p1["kl"]), NT_DIMS, preferred_element_type=F32)
    q_all = jnp.concatenate([bf(p1["qp"]), bf(p2["qp"] * p1["dec"])], axis=0)
    o_st = lax.dot_general(q_all, st.astype(BF16), NT_DIMS, preferred_element_type=F32)
    k_all = jnp.concatenate([bf(p1["kl"] * p2["dec"]), bf(p2["kl"])], axis=0)
    u = lax.dot_general(jnp.concatenate([p1["vb"], p2["vb"]], axis=0), k_all, TN_DIMS, preferred_element_type=F32)
    return sc1, sc2, cross, o_st, u


def _hgrn_pair_out(p1, p2, sc1, sc2, cross, o_st, reverse):
    c = sc1.shape[0]
    row = lax.broadcasted_iota(jnp.int32, (c, c), 0)
    col = lax.broadcasted_iota(jnp.int32, (c, c), 1)
    keep = (col >= row) if reverse else (col <= row)
    tri = lambda sc: jnp.where(keep, sc, 0.0).astype(BF16)
    o1 = jnp.dot(tri(sc1), p1["vb"], preferred_element_type=F32) + o_st[:c]
    o2 = (jnp.dot(tri(sc2), p2["vb"], preferred_element_type=F32)
          + jnp.dot(cross.astype(BF16), p1["vb"], preferred_element_type=F32) + o_st[c:])
    return o1, o2


def _hgrn_exact_block(q_ref, v_ref, z_ref, lb_ref, o_scr, st_scr, dr, reverse, n_tiles):
    pos = lax.broadcasted_iota(jnp.int32, (SUBLANES, LANES), 0)
    if reverse:
        pos = (SUBLANES - 1) - pos

    def earlier(x, j):
        if j == 0:
            return x
        return pltpu.roll(x, (SUBLANES - j) if reverse else j, 0)

    def later(x, j):
        return pltpu.roll(x, j if reverse else (SUBLANES - j), 0)

    def tile(i, carry):
        ti = (n_tiles - 1 - i) if reverse else i
        r0 = pl.multiple_of(ti * SUBLANES, SUBLANES)
        for h in range(A_HEADS):
            sl = slice(h * LANES, (h + 1) * LANES)
            lb = lb_ref[:, sl]
            z = z_ref[0, pl.ds(r0, SUBLANES), sl]
            q = q_ref[0, pl.ds(r0, SUBLANES), sl]
            v = v_ref[0, pl.ds(r0, SUBLANES), sl]
            fm, kk = _hgrn_gates(z, lb)
            pp = fm
            for j in (1, 2, 4):
                pp = pp * jnp.where(pos >= j, earlier(pp, j), 1.0)
            qq = jnp.where(pos <= SUBLANES - 2, later(fm, 1), 1.0)
            for j in (1, 2, 4):
                qq = qq * jnp.where(pos <= SUBLANES - 1 - j, later(qq, j), 1.0)
            dec = pp[0:1] if reverse else pp[SUBLANES - 1:SUBLANES]
            st = st_scr[dr, h]
            o = lax.dot_general((q * pp).astype(BF16), st.astype(BF16), NT_DIMS,
                                preferred_element_type=F32)
            g = fm
            for d in range(SUBLANES):
                if d == 0:
                    e = q * kk
                else:
                    if d > 1:
                        g = g * earlier(fm, d - 1)
                    e = q * g * jnp.where(pos >= d, earlier(kk, d), 0.0)
                o = o + jnp.sum(e, axis=-1, keepdims=True) * earlier(v, d)
            o_scr[pl.ds(r0, SUBLANES), sl] = o
            u = lax.dot_general(v.astype(BF16), (kk * qq).astype(BF16), TN_DIMS,
                                preferred_element_type=F32)
            st_scr[dr, h] = dec * st + u
        return carry

    lax.fori_loop(0, n_tiles, tile, 0)


def _hgrn_scan_body(qf_ref, vf_ref, zf_ref, qb_ref, vb_ref, zb_ref, lb_ref, s0_ref, of_ref, ob_ref, sfin_ref,
                    st_scr, save_scr, ox_scr, *, n_chunks):
    step = pl.program_id(1)

    @pl.when(step == 0)
    def _():
        st_scr[...] = s0_ref[0]

    save_scr[...] = st_scr[...]
    dirs = ((qf_ref, vf_ref, zf_ref, of_ref, False), (qb_ref, vb_ref, zb_ref, ob_ref, True))
    c = HGRN_CHUNK

    def pair(i, bad):
        work, dots = [], []
        for dr, (q_ref, v_ref, z_ref, o_ref, reverse) in enumerate(dirs):
            base = pl.multiple_of(((n_chunks // 2 - 1 - i) if reverse else i) * 2 * c, 2 * c)
            rows = (pl.ds(base + c, c), pl.ds(base, c)) if reverse else (pl.ds(base, c), pl.ds(base + c, c))
            group = []
            for h in range(A_HEADS):
                sl = slice(h * LANES, (h + 1) * LANES)
                ps = [_hgrn_chunk_prep(q_ref[0, r, sl], v_ref[0, r, sl], z_ref[0, r, sl], lb_ref[:, sl], reverse)
                      for r in rows]
                bad = jnp.maximum(bad, jnp.maximum(ps[0]["bad"], ps[1]["bad"]))
                group.append((dr, h, o_ref, rows, sl, reverse, ps))
            dots += [_hgrn_pair_dots(ps[0], ps[1], st_scr[dr, h]) for dr, h, _, _, _, _, ps in group]
            work += group
        for (dr, h, _, _, _, _, ps), d in zip(work, dots):
            st_scr[dr, h] = (ps[0]["dec"] * ps[1]["dec"]) * st_scr[dr, h] + d[4]
        for (dr, h, o_ref, rows, sl, reverse, ps), d in zip(work, dots):
            o1, o2 = _hgrn_pair_out(ps[0], ps[1], *d[:4], reverse)
            o_ref[0, rows[0], sl] = o1.astype(o_ref.dtype)
            o_ref[0, rows[1], sl] = o2.astype(o_ref.dtype)
        return bad

    bad = lax.fori_loop(0, n_chunks // 2, pair, jnp.zeros((1, LANES), F32))

    @pl.when(jnp.max(bad) > 0.0)
    def _():
        st_scr[...] = save_scr[...]
        for dr, (q_ref, v_ref, z_ref, o_ref, reverse) in enumerate(dirs):
            _hgrn_exact_block(q_ref, v_ref, z_ref, lb_ref, ox_scr, st_scr, dr, reverse, n_chunks * c // SUBLANES)
            o_ref[0] = ox_scr[...].astype(o_ref.dtype)

    @pl.when(step == pl.num_programs(1) - 1)
    def _():
        sfin_ref[0] = st_scr[...]


def _hgrn_scan(proj, lb, s0):
    b, l, _ = proj.shape
    d = A_HEADS * LANES
    t = _row_tile(l, 512)
    nb = l // t
    fwd = lambda blk: pl.BlockSpec((1, t, d), lambda i, s: (i, s, blk))
    bwd = lambda blk: pl.BlockSpec((1, t, d), lambda i, s: (i, nb - 1 - s, blk))
    st_spec = pl.BlockSpec((1,) + s0.shape[1:], lambda i, s: (i, 0, 0, 0, 0))
    return pl.pallas_call(
        functools.partial(_hgrn_scan_body, n_chunks=t // HGRN_CHUNK),
        grid=(b, nb),
        in_specs=[fwd(0), fwd(1), fwd(3), bwd(0), bwd(1), bwd(4),
                  pl.BlockSpec((1, d), lambda i, s: (0, 0)), st_spec],
        out_specs=[fwd(0), bwd(0), st_spec],
        out_shape=[jax.ShapeDtypeStruct((b, l, d), BF16), jax.ShapeDtypeStruct((b, l, d), BF16),
                   jax.ShapeDtypeStruct(s0.shape, F32)],
        scratch_shapes=[pltpu.VMEM(s0.shape[1:], F32), pltpu.VMEM(s0.shape[1:], F32), pltpu.VMEM((t, d), F32)],
        compiler_params=_params("parallel", "arbitrary"),
        name="hgrn_scan",
    )(proj, proj, proj, proj, proj, proj, lb, s0)


def _hgrn2_mixer(x_c, x_l, ng, mod_c, mod_l, w_in, lower, out_g, w_out, g1, want_ctx):
    d = x_l.shape[-1]
    p_c = _norm_mod_proj(x_c, ng, mod_c[0], mod_c[1], w_in, F32)
    p_l = _norm_mod_proj(x_l, ng, mod_l[0], mod_l[1], w_in, F32)
    lb = lower.reshape(1, d)
    zero = jnp.zeros((x_l.shape[0], 2, A_HEADS, LANES, LANES), F32)
    ocf, ocb, s_ctx = _hgrn_scan(p_c, lb, zero)
    olf, olb, _ = _hgrn_scan(p_l, lb, s_ctx)
    o_c, o_l = (ocf, ocb), (olf, olb)
    og = out_g.reshape(1, d)
    new_l = _out_call(_out_hgrn_body, [(o_l[0], 0, d), (o_l[1], 0, d), (p_l, 2, d)], [og, w_out],
                      x_l, mod_l[2], g1, "hgrn_out")
    new_c = None
    if want_ctx:
        new_c = _out_call(_out_hgrn_body, [(o_c[0], 0, d), (o_c[1], 0, d), (p_c, 2, d)], [og, w_out],
                          x_c, mod_c[2], g1, "hgrn_out")
    return new_c, new_l


def _softmax_pv(s_list, v_list):
    mx = functools.reduce(jnp.maximum, [jnp.max(s, axis=-1, keepdims=True) for s in s_list])
    ps = [jnp.exp(s - mx) for s in s_list]
    den = functools.reduce(jnp.add, [jnp.sum(p, axis=-1, keepdims=True) for p in ps])
    acc = functools.reduce(jnp.add, [jnp.dot(p.astype(BF16), v, preferred_element_type=F32)
                                     for p, v in zip(ps, v_list)])
    return acc / den


def _na_lat_body(q_ref, k_ref, v_ref, kc_ref, vc_ref, t2_ref, o_ref, *, rows, kh, scale):
    w = GRID_W
    rb = NA_ROW_BLOCK
    nk = kh + rb
    n_rel = 2 * kh - 1
    first = lax.broadcasted_iota(jnp.int32, (rb * w, LANES), 1) < (LANES // 2)
    kc = kc_ref[0]
    vc = vc_ref[0]

    def bias_index(r, ks, m):
        r0 = jnp.clip(r - kh // 2, 0, rows - kh)
        ka = ks + 2 * m
        rel_a = ka - r + (kh - 1)
        in_a = (ka >= r0) & (ka < r0 + kh)
        in_b = (ka + 1 >= r0) & (ka + 1 < r0 + kh)
        both, only_b, only_a = rel_a, (n_rel - 1) + rel_a + 1, (2 * n_rel - 1) + rel_a
        return jnp.where(in_a, jnp.where(in_b, both, only_a), jnp.where(in_b, only_b, 3 * n_rel - 1))

    def block(i, carry):
        rq = i * rb
        ks = jnp.clip(rq - kh // 2, 0, rows - nk)
        q2 = q_ref[0, pl.ds(pl.multiple_of(rq * w, rb * w), rb * w), :] * scale
        k2 = k_ref[0, pl.ds(pl.multiple_of(ks * w, w), nk * w), :]
        v2 = v_ref[0, pl.ds(pl.multiple_of(ks * w, w), nk * w), :]
        outs = []
        for hh in range(2):
            qm = jnp.where(first if hh == 0 else jnp.logical_not(first), q2, jnp.zeros_like(q2))
            bias = jnp.concatenate(
                [jnp.concatenate([t2_ref[hh, bias_index(rq + a, ks, m)] for m in range(nk // 2)], axis=-1)
                 for a in range(rb)], axis=0)
            s_w = lax.dot_general(qm, k2, NT_DIMS, preferred_element_type=F32) + bias
            s_c = lax.dot_general(qm, kc, NT_DIMS, preferred_element_type=F32)
            outs.append(_softmax_pv([s_w, s_c], [v2, vc]))
        o_ref[0, pl.ds(pl.multiple_of(rq * w, rb * w), rb * w), :] = (
            jnp.where(first, outs[0], outs[1]).astype(o_ref.dtype))
        return carry

    lax.fori_loop(0, rows // rb, block, 0, unroll=2)


def _na_ctx_body(q_ref, k_ref, v_ref, o_ref, *, scale):
    n = q_ref.shape[1]
    first = lax.broadcasted_iota(jnp.int32, (n, LANES), 1) < (LANES // 2)
    q2 = q_ref[0] * scale
    k2 = k_ref[0]
    v2 = v_ref[0]
    outs = []
    for hh in range(2):
        qm = jnp.where(first if hh == 0 else jnp.logical_not(first), q2, jnp.zeros_like(q2))
        s = lax.dot_general(qm, k2, NT_DIMS, preferred_element_type=F32)
        outs.append(_softmax_pv([s], [v2]))
    o_ref[0] = jnp.where(first, outs[0], outs[1]).astype(o_ref.dtype)


def _na_bias_table(rpb, kh, kw):
    w = jnp.arange(GRID_W)[:, None]
    c = jnp.arange(GRID_W)[None, :]
    c0 = jnp.clip(w - kw // 2, 0, GRID_W - kw)
    inside = (c >= c0) & (c < c0 + kw)
    pad = GRID_W - kw
    padded = jnp.pad(rpb.astype(F32), ((0, 0), (0, 0), (pad, pad)))
    shifted = jnp.stack([padded[:, :, pad + kw - 1 - q:pad + kw - 1 - q + GRID_W] for q in range(GRID_W)], axis=2)
    t = jnp.where(inside[None, None], shifted, NEG_BIG)
    off = jnp.full_like(t, NEG_BIG)
    pair = lambda a, b: jnp.concatenate([a, b], axis=-1)
    return jnp.concatenate([pair(t[:, :-1], t[:, 1:]), pair(off, t), pair(t, off), pair(off, off)[:, :1]], axis=1)


def _na_mixer(x_c, x_l, ng, mod_c, mod_l, w_qkv, rpb, w_out, g1, want_ctx):
    b, l, d = x_l.shape
    lc = x_c.shape[1]
    heads = rpb.shape[0]
    na_rows, na_cols = (rpb.shape[1] + 1) // 2, (rpb.shape[2] + 1) // 2
    dh = d // heads
    assert 2 * dh == LANES and l % GRID_W == 0
    rows = l // GRID_W
    kh = min(na_rows, rows)
    assert kh == na_rows and (kh + NA_ROW_BLOCK) % 2 == 0
    assert rows % NA_ROW_BLOCK == 0 and rows >= kh + NA_ROW_BLOCK
    scale = dh ** -0.5
    nhp = heads // 2
    qkv_c = _norm_mod_proj(x_c, ng, mod_c[0], mod_c[1], w_qkv, BF16)
    qkv_l = _norm_mod_proj(x_l, ng, mod_l[0], mod_l[1], w_qkv, BF16)
    t2 = _na_bias_table(rpb, kh, na_cols)
    o_l = pl.pallas_call(
        functools.partial(_na_lat_body, rows=rows, kh=kh, scale=scale),
        grid=(b, nhp),
        in_specs=[pl.BlockSpec((1, l, LANES), lambda i, p: (i, 0, p)),
                  pl.BlockSpec((1, l, LANES), lambda i, p: (i, 0, nhp + p)),
                  pl.BlockSpec((1, l, LANES), lambda i, p: (i, 0, 2 * nhp + p)),
                  pl.BlockSpec((1, lc, LANES), lambda i, p: (i, 0, nhp + p)),
                  pl.BlockSpec((1, lc, LANES), lambda i, p: (i, 0, 2 * nhp + p)),
                  pl.BlockSpec((2,) + t2.shape[1:], lambda i, p: (p, 0, 0, 0))],
        out_specs=pl.BlockSpec((1, l, LANES), lambda i, p: (i, 0, p)),
        out_shape=jax.ShapeDtypeStruct((b, l, d), BF16),
        compiler_params=_params("parallel", "parallel"),
        name="na_latent",
    )(qkv_l, qkv_l, qkv_l, qkv_c, qkv_c, t2)
    new_l = _out_call(_out_plain_body, [(o_l, 0, d)], [w_out], x_l, mod_l[2], g1, "na_out")
    new_c = None
    if want_ctx:
        o_c = pl.pallas_call(
            functools.partial(_na_ctx_body, scale=scale),
            grid=(b, nhp),
            in_specs=[pl.BlockSpec((1, lc, LANES), lambda i, p: (i, 0, p)),
                      pl.BlockSpec((1, lc, LANES), lambda i, p: (i, 0, nhp + p)),
                      pl.BlockSpec((1, lc, LANES), lambda i, p: (i, 0, 2 * nhp + p))],
            out_specs=pl.BlockSpec((1, lc, LANES), lambda i, p: (i, 0, p)),
            out_shape=jax.ShapeDtypeStruct((b, lc, d), BF16),
            compiler_params=_params("parallel", "parallel"),
            name="na_context",
        )(qkv_c, qkv_c, qkv_c)
        new_c = _out_call(_out_plain_body, [(o_c, 0, d)], [w_out], x_c, mod_c[2], g1, "na_out")
    return new_c, new_l


S5_GROUP_BLOCK = 8


def _s5_kernel_body(c_ref, w_ref, o_ref):
    for i in range(c_ref.shape[0]):
        o_ref[i] = jnp.dot(c_ref[i], w_ref[i], preferred_element_type=F32, precision=lax.Precision.HIGHEST)


def _s5_impulse(cmat, wmat):
    n = cmat.shape[0]
    gb = S5_GROUP_BLOCK
    return pl.pallas_call(
        _s5_kernel_body,
        grid=(n // gb,),
        in_specs=[pl.BlockSpec((gb,) + cmat.shape[1:], lambda i: (i, 0, 0)),
                  pl.BlockSpec((gb,) + wmat.shape[1:], lambda i: (i, 0, 0))],
        out_specs=pl.BlockSpec((gb, cmat.shape[1], wmat.shape[2]), lambda i: (i, 0, 0)),
        out_shape=jax.ShapeDtypeStruct((n, cmat.shape[1], wmat.shape[2]), F32),
        compiler_params=_params("parallel"),
        name="s5_impulse",
    )(cmat, wmat)


def _s5_operators(lam_re, lam_im, log_dt, b_re, b_im, c_re, c_im, t):
    _, g, p = lam_re.shape
    cg = b_re.shape[-1]
    lam = lax.complex(lam_re.astype(F32), lam_im.astype(F32))
    ldt = lam * jnp.exp(log_dt.astype(F32))[..., None]
    a = jnp.exp(ldt)
    bbar = ((a - 1.0) / lam)[..., None] * lax.complex(b_re.astype(F32), b_im.astype(F32))
    cm = lax.complex(c_re.astype(F32), c_im.astype(F32))
    apow = jnp.exp(ldt[..., None] * jnp.arange(t + 1, dtype=F32))
    w = apow[..., :t, None] * bbar[:, :, :, None, :]
    wmat = jnp.concatenate([jnp.real(w), jnp.imag(w)], axis=2).reshape(2 * g, 2 * p, t * cg)
    cmat = jnp.concatenate([jnp.real(cm), -jnp.imag(cm)], axis=-1).reshape(2 * g, cg, 2 * p)
    k = _s5_impulse(cmat, wmat).reshape(2, g, cg, t, cg)
    k = jnp.transpose(k, (0, 1, 3, 4, 2))
    kfull = jnp.concatenate([k[1, :, :0:-1], (k[0, :, :1] + k[1, :, :1]), k[0, :, 1:]], axis=1)
    lagvec = jnp.transpose(kfull, (0, 2, 1, 3)).reshape(g, cg, (2 * t - 1) * cg).astype(BF16)
    toep = jnp.stack([lagvec[:, :, (t - 1 - s) * cg:(2 * t - 1 - s) * cg] for s in range(t)], axis=1)
    toep = toep.reshape(g, t * cg, t * cg)

    def state_in(wd, flip):
        wd = wd[:, :, ::-1] if flip else wd
        m = jnp.transpose(wd, (0, 2, 3, 1)).reshape(g, t * cg, p)
        return jnp.concatenate([jnp.real(m), jnp.imag(m)], axis=-1)

    def state_out(cd, pw):
        n = cd[:, :, :, None] * pw[:, None, :, :]
        n = jnp.transpose(n, (0, 2, 3, 1)).reshape(g, p, t * cg)
        return jnp.concatenate([jnp.real(n), -jnp.imag(n)], axis=1)

    m_f = state_in(w[0], True)
    m_b = state_in(w[1], False)
    n_f = state_out(cm[0], apow[0][..., 1:])
    n_b = state_out(cm[1], apow[1][..., :0:-1])
    at = apow[..., t]
    a1 = jnp.concatenate([jnp.real(at), jnp.real(at)], axis=-1)
    a2 = jnp.concatenate([-jnp.imag(at), jnp.imag(at)], axis=-1)
    bf = lambda m: m.astype(BF16)
    return dict(toep=bf(toep), m_f=bf(m_f), m_b=bf(m_b), n_f=bf(n_f), n_b=bf(n_b), a1=a1, a2=a2)


def _s5_body(u_ref, dsk_ref, toep_ref, mf_ref, mb_ref, nf_ref, nb_ref, a1_ref, a2_ref, x0_ref,
             y_ref, xfin_ref, z_scr, zs_scr, xin_scr):
    gb, nc = u_ref.shape[1], u_ref.shape[2]
    half = LANES // 2
    ubs = [u_ref[0, gi].astype(BF16) for gi in range(gb)]
    for dr, m_ref in enumerate((mf_ref, mb_ref)):
        z = jnp.stack([jnp.dot(ubs[gi], m_ref[gi], preferred_element_type=F32) for gi in range(gb)])
        z_scr[dr] = jnp.swapaxes(z, 0, 1)

    swap = lambda t: pltpu.roll(t, half, t.ndim - 1)
    for dr in range(2):
        zs_scr[dr] = swap(z_scr[dr].reshape(nc * gb, LANES)).reshape(nc, gb, LANES)

    coef = [(a1_ref[dr], a2_ref[dr]) for dr in range(2)]

    def scan(j, xs):
        out = []
        for dr in range(2):
            jj = j if dr == 0 else nc - 1 - j
            x, xw = xs[2 * dr], xs[2 * dr + 1]
            a1, a2 = coef[dr]
            xin_scr[dr, jj] = x
            out.append(a1 * x + a2 * xw + z_scr[dr, jj])
            out.append(a1 * xw - a2 * x + zs_scr[dr, jj])
        return tuple(out)

    x0f, x0b = x0_ref[0, 0], x0_ref[0, 1]
    xf, _, xb, _ = lax.fori_loop(0, nc, scan, (x0f, swap(x0f), x0b, swap(x0b)))
    xfin_ref[0, 0] = xf
    xfin_ref[0, 1] = xb
    xin = [jnp.swapaxes(xin_scr[dr], 0, 1).astype(BF16) for dr in range(2)]
    for gi in range(gb):
        u = u_ref[0, gi]
        y = jnp.dot(ubs[gi], toep_ref[gi], preferred_element_type=F32) + u * dsk_ref[gi]
        y = y + jnp.dot(xin[0][gi], nf_ref[gi], preferred_element_type=F32)
        y = y + jnp.dot(xin[1][gi], nb_ref[gi], preferred_element_type=F32)
        y_ref[0, gi] = y


def _piece_transpose(src, piece_of, cg):
    n = len(src)
    out = [None] * n
    for s in range(n):
        c = src[s % n]
        for p in range(1, n):
            c = jnp.where(piece_of == p, src[(p + s) % n], c)
        r = c if s == 0 else pltpu.roll(c, s * cg, 1)
        for p in range(n):
            out[p] = r if s == 0 else jnp.where(piece_of == (p + s) % n, r, out[p])
    return out


def _to_chunks(h_scr, o_ref, t, cg):
    ncb = h_scr.shape[1] // t
    pieces = LANES // cg
    piece_of = lax.broadcasted_iota(jnp.int32, (ncb, LANES), 1) // cg

    def lane_block(lb, carry):
        for tq in range(t // pieces):
            steps = [h_scr[lb, pl.ds(tq * pieces + tr, ncb, stride=t), :] for tr in range(pieces)]
            for gl, col in enumerate(_piece_transpose(steps, piece_of, cg)):
                o_ref[0, lb * pieces + gl, :, tq * LANES:(tq + 1) * LANES] = col
        return carry

    lax.fori_loop(0, h_scr.shape[0], lane_block, 0)


def _from_chunks(y_ref, y_scr, t, cg):
    ncb = y_ref.shape[2]
    pieces = LANES // cg
    piece_of = lax.broadcasted_iota(jnp.int32, (ncb, LANES), 1) // cg

    def lane_block(lb, carry):
        for tq in range(t // pieces):
            cols = [y_ref[0, lb * pieces + gl, :, tq * LANES:(tq + 1) * LANES] for gl in range(pieces)]
            for tr, step in enumerate(_piece_transpose(cols, piece_of, cg)):
                y_scr[lb, pl.ds(tq * pieces + tr, ncb, stride=t), :] = step
        return carry

    lax.fori_loop(0, y_scr.shape[0], lane_block, 0)


def _norm_chunks_body(x_ref, g_ref, sh_ref, sc_ref, o_ref, h_scr, *, t, cg):
    h = _norm_mod(x_ref[...], g_ref[...], sh_ref[0], sc_ref[0])
    for lb in range(h_scr.shape[0]):
        h_scr[lb] = h[:, lb * LANES:(lb + 1) * LANES]
    _to_chunks(h_scr, o_ref, t, cg)


def _norm_mod_chunks(x, g, sh, sc, groups, t):
    b, l, d = x.shape
    cg = d // groups
    tm = _row_tile(l, 512)
    per = l // tm
    assert tm % t == 0 and (tm // t) % SUBLANES == 0 and LANES % cg == 0 and t % (LANES // cg) == 0
    return pl.pallas_call(
        functools.partial(_norm_chunks_body, t=t, cg=cg),
        grid=(b * per,),
        in_specs=[pl.BlockSpec((tm, d), lambda i: (i, 0)),
                  pl.BlockSpec((1, d), lambda i: (0, 0)),
                  _mod_spec(sh, per), _mod_spec(sc, per)],
        out_specs=pl.BlockSpec((1, groups, tm // t, t * cg), lambda i: (i // per, 0, i % per, 0)),
        out_shape=jax.ShapeDtypeStruct((b, groups, l // t, t * cg), F32),
        scratch_shapes=[pltpu.VMEM((d // LANES, tm, LANES), F32)],
        compiler_params=_params("parallel"),
        name="norm_mod_chunks",
    )(x.reshape(b * l, d), g.reshape(1, d), sh.table, sc.table)


def _glu_chunks_body(y_ref, w_ref, res_ref, gate_ref, g1_ref, o_ref, y_scr, *, t, cg):
    _from_chunks(y_ref, y_scr, t, cg)
    d = o_ref.shape[-1]
    y = jnp.concatenate([y_scr[lb] for lb in range(y_scr.shape[0])], axis=-1)
    a = _gelu_tanh(y).astype(BF16)
    ag = jnp.dot(a, w_ref[...], preferred_element_type=F32)
    o_ref[...] = _residual(res_ref, gate_ref, g1_ref, ag[:, :d] * jax.nn.sigmoid(ag[:, d:]))


def _glu_out_chunks(y, w_glu, res, gate, g1, t):
    b, l, d = res.shape
    groups = y.shape[1]
    cg = d // groups
    tm = _row_tile(l, 512)
    per = l // tm
    out = pl.pallas_call(
        functools.partial(_glu_chunks_body, t=t, cg=cg),
        grid=(b * per,),
        in_specs=[pl.BlockSpec((1, groups, tm // t, t * cg), lambda i: (i // per, 0, i % per, 0)),
                  _layer_spec(w_glu, w_glu.shape, lambda i: (0, 0)),
                  pl.BlockSpec((tm, d), lambda i: (i, 0)),
                  _mod_spec(gate, per),
                  pl.BlockSpec((1, d), lambda i: (0, 0))],
        out_specs=pl.BlockSpec((tm, d), lambda i: (i, 0)),
        out_shape=jax.ShapeDtypeStruct((b * l, d), F32),
        scratch_shapes=[pltpu.VMEM((d // LANES, tm, LANES), F32)],
        compiler_params=_params("parallel"),
        name="s5_glu_out",
    )(y, w_glu.stack, res.reshape(b * l, d), gate.table, g1.reshape(1, d))
    return out.reshape(b, l, d)


def _s5_scan(u, ops, dsk, x0):
    b, g, nc, tc = u.shape
    gb = S5_GROUP_BLOCK
    per_g = lambda shape: pl.BlockSpec((gb,) + shape, lambda j, i: (j,) + (0,) * len(shape))
    y, xfin = pl.pallas_call(
        _s5_body,
        grid=(g // gb, b),
        in_specs=[pl.BlockSpec((1, gb, nc, tc), lambda j, i: (i, j, 0, 0)),
                  per_g((1, tc)), per_g((tc, tc)), per_g((tc, LANES)), per_g((tc, LANES)),
                  per_g((LANES, tc)), per_g((LANES, tc)),
                  pl.BlockSpec((2, gb, LANES), lambda j, i: (0, j, 0)),
                  pl.BlockSpec((2, gb, LANES), lambda j, i: (0, j, 0)),
                  pl.BlockSpec((1, 2, gb, LANES), lambda j, i: (i, 0, j, 0))],
        out_specs=[pl.BlockSpec((1, gb, nc, tc), lambda j, i: (i, j, 0, 0)),
                   pl.BlockSpec((1, 2, gb, LANES), lambda j, i: (i, 0, j, 0))],
        out_shape=[jax.ShapeDtypeStruct((b, g, nc, tc), F32), jax.ShapeDtypeStruct((b, 2, g, LANES), F32)],
        scratch_shapes=[pltpu.VMEM((2, nc, gb, LANES), F32)] * 3,
        compiler_params=_params("parallel", "parallel"),
        name="s5_scan",
    )(u, dsk, ops["toep"], ops["m_f"], ops["m_b"], ops["n_f"], ops["n_b"], ops["a1"], ops["a2"], x0)
    return y, xfin


def _s5_mixer(x_c, x_l, ng, mod_c, mod_l, lam_re, lam_im, log_dt, b_re, b_im, c_re, c_im, d_skip, w_glu, g1,
              want_ctx):
    b, l, d = x_l.shape
    g, p = lam_re.shape[1], lam_re.shape[2]
    cg = d // g
    assert 2 * p == LANES and g % S5_GROUP_BLOCK == 0
    ops = _s5_operators(lam_re, lam_im, log_dt, b_re, b_im, c_re, c_im, S5_CHUNK)
    dsk = jnp.tile(d_skip.astype(F32).reshape(g, 1, cg), (1, 1, S5_CHUNK))
    u_c = _norm_mod_chunks(x_c, ng, mod_c[0], mod_c[1], g, S5_CHUNK)
    u_l = _norm_mod_chunks(x_l, ng, mod_l[0], mod_l[1], g, S5_CHUNK)
    y_c, x_ctx = _s5_scan(u_c, ops, dsk, jnp.zeros((b, 2, g, LANES), F32))
    y_l, _ = _s5_scan(u_l, ops, dsk, x_ctx)
    new_l = _glu_out_chunks(y_l, w_glu, x_l, mod_l[2], g1, S5_CHUNK)
    new_c = _glu_out_chunks(y_c, w_glu, x_c, mod_c[2], g1, S5_CHUNK) if want_ctx else None
    return new_c, new_l


FFN_HALO = 16
FFN_TN = 256
FFN_TILES_PER_STEP = 11
FFN_ROW_SPLIT = 256


def _ffn_body(xp_ref, x_ref, xn_ref, g2_ref, sh_ref, sc_ref, *rest, per, n_tiles, tps):
    w_refs = rest[:2 * tps]
    cw_ref, cb_ref, wo_ref, gate_ref, g3_ref, o_ref, h_scr, gat_scr = rest[2 * tps:]
    i, j = pl.program_id(0), pl.program_id(1)
    tm = x_ref.shape[0]
    hl = FFN_HALO
    f = wo_ref.shape[0]
    n_steps = pl.cdiv(n_tiles, tps)
    when = (lambda cond: lambda fn: fn()) if n_steps == 1 else pl.when

    @when(j == 0)
    def _():
        g, sh, sc = g2_ref[...], sh_ref[0], sc_ref[0]
        keep_p = ((i % per) != 0).astype(F32)
        keep_n = ((i % per) != per - 1).astype(F32)
        h_scr[0:hl] = (_norm_mod(xp_ref[...], g, sh, sc) * keep_p).astype(BF16)
        h_scr[hl:hl + tm] = _norm_mod(x_ref[...], g, sh, sc).astype(BF16)
        h_scr[hl + tm:] = (_norm_mod(xn_ref[...], g, sh, sc) * keep_n).astype(BF16)

    sub = FFN_ROW_SPLIT if tm % FFN_ROW_SPLIT == 0 else tm
    ns = tm // sub
    edge = SUBLANES
    tn = w_refs[0].shape[1]

    def pieces(w_ref):
        out = []
        for s in range(ns):
            lo = hl + s * sub - (hl if s == 0 else 0)
            hi = hl + (s + 1) * sub + (hl if s == ns - 1 else 0)
            out.append(jnp.dot(h_scr[lo:hi], w_ref[...], preferred_element_type=F32))
        return out

    def conv(p, s, cw, cb):
        base = hl if s == 0 else 0
        left = p[s][hl - edge:hl] if s == 0 else p[s - 1][-edge:]
        right = p[s][base + sub:base + sub + edge] if s == ns - 1 else p[s + 1][:edge]
        ext = jnp.concatenate([left, p[s][base:base + sub], right], axis=0)
        um = pltpu.roll(ext, 1, 0)[edge:edge + sub]
        up = pltpu.roll(ext, sub + 2 * edge - 1, 0)[edge:edge + sub]
        return cb + um * cw[0:1] + ext[edge:edge + sub] * cw[1:2] + up * cw[2:3]

    def tile(k):
        col = k * tn if n_steps == 1 else pl.multiple_of((j * tps + k) * tn, tn)
        cols_a, cols_v = pl.ds(col, tn), pl.ds(pl.multiple_of(f + col, tn), tn)
        pa, pv = pieces(w_refs[2 * k]), pieces(w_refs[2 * k + 1])
        for s in range(ns):
            a = conv(pa, s, cw_ref[:, cols_a], cb_ref[:, cols_a])
            v = conv(pv, s, cw_ref[:, cols_v], cb_ref[:, cols_v])
            gat_scr[s * sub:(s + 1) * sub, cols_a] = (a * jax.nn.sigmoid(a) * v).astype(BF16)

    always = n_tiles - tps * (n_steps - 1)
    for k in range(tps):
        if k < always:
            tile(k)
        else:
            pl.when(j * tps + k < n_tiles)(functools.partial(tile, k))

    @when(j == n_steps - 1)
    def _():
        y = jnp.dot(gat_scr[...], wo_ref[...], preferred_element_type=F32)
        o_ref[...] = x_ref[...] + gate_ref[0] * _rms(y, g3_ref[...])


def _conv_ffn_block(x, ng2, mod, w_in, conv_w, conv_b, w_out, ng3):
    b, l, d = x.shape
    f = w_out.shape[0]
    tn, hl = FFN_TN, FFN_HALO
    assert f % tn == 0 and conv_w.shape[0] == 3 and w_in.shape[1] == 2 * f
    nj = f // tn
    tm = _row_tile(l, 1024)
    per = l // tm
    hb = tm // hl
    last_hb = b * l // hl - 1
    cb = conv_b.reshape(1, 2 * f)
    x2 = x.reshape(b * l, d)
    tps = min(FFN_TILES_PER_STEP, nj)
    once = dict(pipeline_mode=pl.Buffered(1)) if tps == nj else {}
    w_specs = []
    for k in range(tps):
        w_specs.append(_layer_spec(w_in, (d, tn), lambda i, j, k=k: (0, jnp.minimum(j * tps + k, nj - 1)), **once))
        w_specs.append(_layer_spec(w_in, (d, tn), lambda i, j, k=k: (0, nj + jnp.minimum(j * tps + k, nj - 1)),
                                   **once))
    whole = lambda arr: pl.BlockSpec(arr.shape, lambda i, j: (0,) * arr.ndim)
    out = pl.pallas_call(
        functools.partial(_ffn_body, per=per, n_tiles=nj, tps=tps),
        grid=(b * per, pl.cdiv(nj, tps)),
        in_specs=[pl.BlockSpec((hl, d), lambda i, j: (jnp.maximum(i * hb - 1, 0), 0)),
                  pl.BlockSpec((tm, d), lambda i, j: (i, 0)),
                  pl.BlockSpec((hl, d), lambda i, j: (jnp.minimum((i + 1) * hb, last_hb), 0)),
                  pl.BlockSpec((1, d), lambda i, j: (0, 0)),
                  _mod_spec(mod[3], per), _mod_spec(mod[4], per),
                  *w_specs, whole(conv_w), whole(cb),
                  _layer_spec(w_out, (f, d), lambda i, j: (0, 0), pipeline_mode=pl.Buffered(1)),
                  _mod_spec(mod[5], per),
                  pl.BlockSpec((1, d), lambda i, j: (0, 0))],
        out_specs=pl.BlockSpec((tm, d), lambda i, j: (i, 0)),
        out_shape=jax.ShapeDtypeStruct((b * l, d), F32),
        scratch_shapes=[pltpu.VMEM((tm + 2 * hl, d), BF16), pltpu.VMEM((tm, f), BF16)],
        compiler_params=_params("parallel", "arbitrary"),
        name="conv_ffn",
    )(x2, x2, x2, ng2.reshape(1, d), mod[3].table, mod[4].table, *([w_in.stack] * (2 * tps)), conv_w, cb,
      w_out.stack,
      mod[5].table, ng3.reshape(1, d))
    return out.reshape(b, l, d)


def kernel(x, c, ctx, c_ctx, w_mod, b_mod, norm_g, a_w_in, a_lower_logits, a_out_g, a_w_out, b_w_qkv, b_rpb,
           b_w_out, c_lam_re, c_lam_im, c_log_dt, c_b_re, c_b_im, c_c_re, c_c_im, c_d, c_w_glu, f_w_in,
           f_conv_w, f_conv_b, f_w_out):
    bsz, _, d = x.shape
    depth = w_mod.shape[0]
    p = jax.nn.softmax(a_lower_logits.astype(F32), axis=0)
    lower = jnp.cumsum(p, axis=0) - p[0]
    n_rows = -(-(bsz + 1) // SUBLANES) * SUBLANES
    c_rows = jnp.zeros((n_rows, d), F32).at[:bsz].set(c).at[bsz].set(c_ctx)
    table = _modulation(c_rows, w_mod, b_mod).reshape(depth * n_rows * N_MOD, 1, d)
    wa_in, wa_out, wb_qkv, wb_out, wc_glu, wf_in, wf_out = (
        w.astype(BF16) for w in (a_w_in, a_w_out, b_w_qkv, b_w_out, c_w_glu, f_w_in, f_w_out))
    lat, cx = x, ctx
    for i in range(depth):
        kind, j = i % N_MIXERS, i // N_MIXERS
        last = i == depth - 1
        mod_l = [ModRow(table, i * n_rows * N_MOD + k, N_MOD) for k in range(N_MOD)]
        mod_c = [ModRow(table, (i * n_rows + bsz) * N_MOD + k, 0) for k in range(N_MOD)]
        ng = norm_g[i]
        if kind == 0:
            cx1, lat = _hgrn2_mixer(cx, lat, ng[0], mod_c, mod_l, LayerOf(wa_in, j), lower[i], a_out_g[j],
                                    LayerOf(wa_out, j), ng[1], not last)
        elif kind == 1:
            cx1, lat = _na_mixer(cx, lat, ng[0], mod_c, mod_l, LayerOf(wb_qkv, j), b_rpb[j], LayerOf(wb_out, j),
                                 ng[1], not last)
        else:
            cx1, lat = _s5_mixer(cx, lat, ng[0], mod_c, mod_l, c_lam_re[j], c_lam_im[j], c_log_dt[j], c_b_re[j],
                                 c_b_im[j], c_c_re[j], c_c_im[j], c_d[j], LayerOf(wc_glu, j), ng[1], not last)
        ffn = (LayerOf(wf_in, i), f_conv_w[i], f_conv_b[i], LayerOf(wf_out, i), ng[3])
        lat = _conv_ffn_block(lat, ng[2], mod_l, *ffn)
        if not last:
            cx = _conv_ffn_block(cx1, ng[2], mod_c, *ffn)
    return lat
```

```python
import functools
from typing import NamedTuple

import jax
import jax.numpy as jnp
from jax import lax
from jax.experimental import pallas as pl
from jax.experimental.pallas import tpu as pltpu

F32 = jnp.float32
BF16 = jnp.bfloat16

EPS = 1e-6
F_MIN = 1e-30
N_MOD = 6
N_MIXERS = 3
A_HEADS = 8
GRID_W = 64
S5_CHUNK = 16
NEG_BIG = -1e30
NA_ROW_BLOCK = 4
SUBLANES = 8
LANES = 128
VMEM_LIMIT_BYTES = 56 * 1024 * 1024

NT_DIMS = (((1,), (1,)), ((), ()))
TN_DIMS = (((0,), (0,)), ((), ()))


def _params(*sem):
    return pltpu.CompilerParams(dimension_semantics=sem, vmem_limit_bytes=VMEM_LIMIT_BYTES)


def _row_tile(n, want):
    t = min(n, want)
    assert n % t == 0, (n, t)
    return t


def _rms(y, g):
    return y * lax.rsqrt(jnp.mean(y * y, axis=-1, keepdims=True) + EPS) * g


def _mod_body(c_ref, w_ref, b_ref, o_ref):
    c = c_ref[...]
    s = (c * jax.nn.sigmoid(c)).astype(BF16)
    o_ref[0] = jnp.dot(s, w_ref[0].astype(BF16), preferred_element_type=F32) + b_ref[0]


def _modulation(c_rows, w_mod, b_mod):
    depth, d, n = w_mod.shape
    r = c_rows.shape[0]
    tn = n // 4
    return pl.pallas_call(
        _mod_body,
        grid=(depth, n // tn),
        in_specs=[pl.BlockSpec((r, d), lambda i, j: (0, 0)),
                  pl.BlockSpec((1, d, tn), lambda i, j: (i, 0, j)),
                  pl.BlockSpec((1, 1, tn), lambda i, j: (i, 0, j))],
        out_specs=pl.BlockSpec((1, r, tn), lambda i, j: (i, 0, j)),
        out_shape=jax.ShapeDtypeStruct((depth, r, n), F32),
        compiler_params=_params("parallel", "parallel"),
        name="adaln_mod",
    )(c_rows, w_mod, b_mod.reshape(depth, 1, n))


def _norm_mod(x, g, sh, sc):
    return _rms(x, g) * (1.0 + sc) + sh


class ModRow(NamedTuple):
    table: jax.Array
    first: int
    step: int


def _mod_spec(m, per):
    return pl.BlockSpec((1, 1, m.table.shape[-1]), lambda *idx: (m.first + m.step * (idx[0] // per), 0, 0))


class LayerOf(NamedTuple):
    stack: jax.Array
    layer: int

    @property
    def shape(self):
        return self.stack.shape[1:]


def _layer_spec(w, block, index_map, **kw):
    return pl.BlockSpec((None,) + tuple(block), lambda *idx: (w.layer,) + tuple(index_map(*idx)), **kw)


def _proj_body(x_ref, g_ref, sh_ref, sc_ref, w_ref, o_ref):
    h = _norm_mod(x_ref[...], g_ref[...], sh_ref[0], sc_ref[0]).astype(BF16)
    o_ref[...] = jnp.dot(h, w_ref[...], preferred_element_type=F32).astype(o_ref.dtype)


def _norm_mod_proj(x, g, sh, sc, w, out_dtype):
    b, l, d = x.shape
    n = w.shape[1]
    tm = _row_tile(l, 512)
    per = l // tm
    out = pl.pallas_call(
        _proj_body,
        grid=(b * per,),
        in_specs=[pl.BlockSpec((tm, d), lambda i: (i, 0)),
                  pl.BlockSpec((1, d), lambda i: (0, 0)),
                  _mod_spec(sh, per), _mod_spec(sc, per),
                  _layer_spec(w, (d, n), lambda i: (0, 0), pipeline_mode=pl.Buffered(1))],
        out_specs=pl.BlockSpec((tm, n), lambda i: (i, 0)),
        out_shape=jax.ShapeDtypeStruct((b * l, n), out_dtype),
        compiler_params=_params("parallel"),
        name="norm_mod_proj",
    )(x.reshape(b * l, d), g.reshape(1, d), sh.table, sc.table, w.stack)
    return out.reshape(b, l, n)


def _residual(res_ref, gate_ref, g1_ref, y):
    return res_ref[...] + gate_ref[0] * _rms(y, g1_ref[...])


def _out_plain_body(a_ref, w_ref, res_ref, gate_ref, g1_ref, o_ref):
    y = jnp.dot(a_ref[...], w_ref[...], preferred_element_type=F32)
    o_ref[...] = _residual(res_ref, gate_ref, g1_ref, y)


def _out_hgrn_body(of_ref, ob_ref, gp_ref, og_ref, w_ref, res_ref, gate_ref, g1_ref, o_ref):
    o = of_ref[...].astype(F32) + ob_ref[...].astype(F32)
    parts = []
    for h in range(A_HEADS):
        oh = o[:, h * LANES:(h + 1) * LANES]
        parts.append(oh * lax.rsqrt(jnp.mean(oh * oh, axis=-1, keepdims=True) + EPS))
    gp = gp_ref[...]
    a = jnp.concatenate(parts, axis=-1) * og_ref[...] * (gp * jax.nn.sigmoid(gp))
    y = jnp.dot(a.astype(BF16), w_ref[...], preferred_element_type=F32)
    o_ref[...] = _residual(res_ref, gate_ref, g1_ref, y)


def _gelu_tanh(y):
    return 0.5 * y * (1.0 + jnp.tanh(0.7978845608028654 * (y + 0.044715 * (y * y * y))))


def _out_call(body, row_inputs, const_inputs, res, gate, g1, name):
    b, l, d = res.shape
    tm = _row_tile(l, 512)
    per = l // tm
    in_specs, args = [], []
    for arr, blk, width in row_inputs:
        in_specs.append(pl.BlockSpec((tm, width), lambda i, blk=blk: (i, blk)))
        args.append(arr.reshape(b * l, arr.shape[-1]))
    for arr in const_inputs:
        if isinstance(arr, LayerOf):
            in_specs.append(_layer_spec(arr, arr.shape, lambda i, nd=len(arr.shape): (0,) * nd))
            args.append(arr.stack)
        else:
            in_specs.append(pl.BlockSpec(arr.shape, lambda i, nd=arr.ndim: (0,) * nd))
            args.append(arr)
    in_specs += [pl.BlockSpec((tm, d), lambda i: (i, 0)),
                 _mod_spec(gate, per),
                 pl.BlockSpec((1, d), lambda i: (0, 0))]
    args += [res.reshape(b * l, d), gate.table, g1.reshape(1, d)]
    out = pl.pallas_call(
        body,
        grid=(b * per,),
        in_specs=in_specs,
        out_specs=pl.BlockSpec((tm, d), lambda i: (i, 0)),
        out_shape=jax.ShapeDtypeStruct((b * l, d), F32),
        compiler_params=_params("parallel"),
        name=name,
    )(*args)
    return out.reshape(b, l, d)


HGRN_CHUNK = 32
HGRN_SAFE_MIN = 1e-30
HGRN_Q_HEADROOM = 1e37


def _hgrn_gates(z, lb):
    f = lb + (1.0 - lb) * jax.nn.sigmoid(z)
    return jnp.maximum(f, F_MIN), 1.0 - f


def _hgrn_chunk_prep(q, v, z, lb, reverse):
    c = q.shape[0]
    half = c // 2
    fm, kk = _hgrn_gates(z, lb)
    pos = lax.broadcasted_iota(jnp.int32, (c, LANES), 0)
    if reverse:
        pos = (c - 1) - pos
    second = pos >= half
    hpos = jnp.where(second, pos - half, pos)
    ph = fm
    j = 1
    while j < half:
        ph = ph * jnp.where(hpos >= j, pltpu.roll(ph, (c - j) if reverse else j, 0), 1.0)
        j *= 2
    a_last = ph[half:half + 1] if reverse else ph[half - 1:half]
    r_last = ph[0:1] if reverse else ph[c - 1:c]
    p_mid = a_last
    p_last = a_last * r_last
    pp = jnp.where(second, ph * a_last, ph)
    r = jnp.where(second, ph, ph * (1.0 / a_last))
    kd = kk * (1.0 / r)
    q_max = jnp.max(jnp.abs(q), axis=0, keepdims=True)
    ok = jnp.where(p_mid >= HGRN_SAFE_MIN, r_last, 0.0) >= HGRN_SAFE_MIN
    ok = jnp.where(ok, p_mid * HGRN_Q_HEADROOM, -1.0) >= q_max
    return dict(qr=(q * r).astype(BF16), kd=kd.astype(BF16), qp=q * pp, kl=kd * r_last, vb=v.astype(BF16),
                dec=p_last, bad=jnp.where(ok, 0.0, 1.0))


def _hgrn_pair_dots(p1, p2, st):
    bf = lambda t: t.astype(BF16)
    sc1 = lax.dot_general(p1["qr"], p1["kd"], NT_DIMS, preferred_element_type=F32)
    sc2 = lax.dot_general(p2["qr"], p2["kd"], NT_DIMS, preferred_element_type=F32)
    cross = lax.dot_general(bf(p2["qp"]), bf(p1["kl"]), NT_DIMS, preferred_element_type=F32)
    q_all = jnp.concatenate([bf(p1["qp"]), bf(p2["qp"] * p1["dec"])], axis=0)
    o_st = lax.dot_general(q_all, st.astype(BF16), NT_DIMS, preferred_element_type=F32)
    k_all = jnp.concatenate([bf(p1["kl"] * p2["dec"]), bf(p2["kl"])], axis=0)
    u = lax.dot_general(jnp.concatenate([p1["vb"], p2["vb"]], axis=0), k_all, TN_DIMS, preferred_element_type=F32)
    return sc1, sc2, cross, o_st, u


def _hgrn_pair_out(p1, p2, sc1, sc2, cross, o_st, reverse):
    c = sc1.shape[0]
    row = lax.broadcasted_iota(jnp.int32, (c, c), 0)
    col = lax.broadcasted_iota(jnp.int32, (c, c), 1)
    keep = (col >= row) if reverse else (col <= row)
    tri = lambda sc: jnp.where(keep, sc, 0.0).astype(BF16)
    o1 = jnp.dot(tri(sc1), p1["vb"], preferred_element_type=F32) + o_st[:c]
    o2 = (jnp.dot(tri(sc2), p2["vb"], preferred_element_type=F32)
          + jnp.dot(cross.astype(BF16), p1["vb"], preferred_element_type=F32) + o_st[c:])
    return o1, o2


def _hgrn_exact_block(q_ref, v_ref, z_ref, lb_ref, o_scr, st_scr, dr, reverse, n_tiles):
    pos = lax.broadcasted_iota(jnp.int32, (SUBLANES, LANES), 0)
    if reverse:
        pos = (SUBLANES - 1) - pos

    def earlier(x, j):
        if j == 0:
            return x
        return pltpu.roll(x, (SUBLANES - j) if reverse else j, 0)

    def later(x, j):
        return pltpu.roll(x, j if reverse else (SUBLANES - j), 0)

    def tile(i, carry):
        ti = (n_tiles - 1 - i) if reverse else i
        r0 = pl.multiple_of(ti * SUBLANES, SUBLANES)
        for h in range(A_HEADS):
            sl = slice(h * LANES, (h + 1) * LANES)
            lb = lb_ref[:, sl]
            z = z_ref[0, pl.ds(r0, SUBLANES), sl]
            q = q_ref[0, pl.ds(r0, SUBLANES), sl]
            v = v_ref[0, pl.ds(r0, SUBLANES), sl]
            fm, kk = _hgrn_gates(z, lb)
            pp = fm
            for j in (1, 2, 4):
                pp = pp * jnp.where(pos >= j, earlier(pp, j), 1.0)
            qq = jnp.where(pos <= SUBLANES - 2, later(fm, 1), 1.0)
            for j in (1, 2, 4):
                qq = qq * jnp.where(pos <= SUBLANES - 1 - j, later(qq, j), 1.0)
            dec = pp[0:1] if reverse else pp[SUBLANES - 1:SUBLANES]
            st = st_scr[dr, h]
            o = lax.dot_general((q * pp).astype(BF16), st.astype(BF16), NT_DIMS,
                                preferred_element_type=F32)
            g = fm
            for d in range(SUBLANES):
                if d == 0:
                    e = q * kk
                else:
                    if d > 1:
                        g = g * earlier(fm, d - 1)
                    e = q * g * jnp.where(pos >= d, earlier(kk, d), 0.0)
                o = o + jnp.sum(e, axis=-1, keepdims=True) * earlier(v, d)
            o_scr[pl.ds(r0, SUBLANES), sl] = o
            u = lax.dot_general(v.astype(BF16), (kk * qq).astype(BF16), TN_DIMS,
                                preferred_element_type=F32)
            st_scr[dr, h] = dec * st + u
        return carry

    lax.fori_loop(0, n_tiles, tile, 0)


def _hgrn_scan_body(qf_ref, vf_ref, zf_ref, qb_ref, vb_ref, zb_ref, lb_ref, s0_ref, of_ref, ob_ref, sfin_ref,
                    st_scr, save_scr, ox_scr, *, n_chunks):
    step = pl.program_id(1)

    @pl.when(step == 0)
    def _():
        st_scr[...] = s0_ref[0]

    save_scr[...] = st_scr[...]
    dirs = ((qf_ref, vf_ref, zf_ref, of_ref, False), (qb_ref, vb_ref, zb_ref, ob_ref, True))
    c = HGRN_CHUNK

    def pair(i, bad):
        work, dots = [], []
        for dr, (q_ref, v_ref, z_ref, o_ref, reverse) in enumerate(dirs):
            base = pl.multiple_of(((n_chunks // 2 - 1 - i) if reverse else i) * 2 * c, 2 * c)
            rows = (pl.ds(base + c, c), pl.ds(base, c)) if reverse else (pl.ds(base, c), pl.ds(base + c, c))
            group = []
            for h in range(A_HEADS):
                sl = slice(h * LANES, (h + 1) * LANES)
                ps = [_hgrn_chunk_prep(q_ref[0, r, sl], v_ref[0, r, sl], z_ref[0, r, sl], lb_ref[:, sl], reverse)
                      for r in rows]
                bad = jnp.maximum(bad, jnp.maximum(ps[0]["bad"], ps[1]["bad"]))
                group.append((dr, h, o_ref, rows, sl, reverse, ps))
            dots += [_hgrn_pair_dots(ps[0], ps[1], st_scr[dr, h]) for dr, h, _, _, _, _, ps in group]
            work += group
        for (dr, h, _, _, _, _, ps), d in zip(work, dots):
            st_scr[dr, h] = (ps[0]["dec"] * ps[1]["dec"]) * st_scr[dr, h] + d[4]
        for (dr, h, o_ref, rows, sl, reverse, ps), d in zip(work, dots):
            o1, o2 = _hgrn_pair_out(ps[0], ps[1], *d[:4], reverse)
            o_ref[0, rows[0], sl] = o1.astype(o_ref.dtype)
            o_ref[0, rows[1], sl] = o2.astype(o_ref.dtype)
        return bad

    bad = lax.fori_loop(0, n_chunks // 2, pair, jnp.zeros((1, LANES), F32))

    @pl.when(jnp.max(bad) > 0.0)
    def _():
        st_scr[...] = save_scr[...]
        for dr, (q_ref, v_ref, z_ref, o_ref, reverse) in enumerate(dirs):
            _hgrn_exact_block(q_ref, v_ref, z_ref, lb_ref, ox_scr, st_scr, dr, reverse, n_chunks * c // SUBLANES)
            o_ref[0] = ox_scr[...].astype(o_ref.dtype)

    @pl.when(step == pl.num_programs(1) - 1)
    def _():
        sfin_ref[0] = st_scr[...]


def _hgrn_scan(proj, lb, s0):
    b, l, _ = proj.shape
    d = A_HEADS * LANES
    t = _row_tile(l, 512)
    nb = l // t
    fwd = lambda blk: pl.BlockSpec((1, t, d), lambda i, s: (i, s, blk))
    bwd = lambda blk: pl.BlockSpec((1, t, d), lambda i, s: (i, nb - 1 - s, blk))
    st_spec = pl.BlockSpec((1,) + s0.shape[1:], lambda i, s: (i, 0, 0, 0, 0))
    return pl.pallas_call(
        functools.partial(_hgrn_scan_body, n_chunks=t // HGRN_CHUNK),
        grid=(b, nb),
        in_specs=[fwd(0), fwd(1), fwd(3), bwd(0), bwd(1), bwd(4),
                  pl.BlockSpec((1, d), lambda i, s: (0, 0)), st_spec],
        out_specs=[fwd(0), bwd(0), st_spec],
        out_shape=[jax.ShapeDtypeStruct((b, l, d), BF16), jax.ShapeDtypeStruct((b, l, d), BF16),
                   jax.ShapeDtypeStruct(s0.shape, F32)],
        scratch_shapes=[pltpu.VMEM(s0.shape[1:], F32), pltpu.VMEM(s0.shape[1:], F32), pltpu.VMEM((t, d), F32)],
        compiler_params=_params("parallel", "arbitrary"),
        name="hgrn_scan",
    )(proj, proj, proj, proj, proj, proj, lb, s0)


def _hgrn2_mixer(x_c, x_l, ng, mod_c, mod_l, w_in, lower, out_g, w_out, g1, want_ctx):
    d = x_l.shape[-1]
    p_c = _norm_mod_proj(x_c, ng, mod_c[0], mod_c[1], w_in, F32)
    p_l = _norm_mod_proj(x_l, ng, mod_l[0], mod_l[1], w_in, F32)
    lb = lower.reshape(1, d)
    zero = jnp.zeros((x_l.shape[0], 2, A_HEADS, LANES, LANES), F32)
    ocf, ocb, s_ctx = _hgrn_scan(p_c, lb, zero)
    olf, olb, _ = _hgrn_scan(p_l, lb, s_ctx)
    o_c, o_l = (ocf, ocb), (olf, olb)
    og = out_g.reshape(1, d)
    new_l = _out_call(_out_hgrn_body, [(o_l[0], 0, d), (o_l[1], 0, d), (p_l, 2, d)], [og, w_out],
                      x_l, mod_l[2], g1, "hgrn_out")
    new_c = None
    if want_ctx:
        new_c = _out_call(_out_hgrn_body, [(o_c[0], 0, d), (o_c[1], 0, d), (p_c, 2, d)], [og, w_out],
                          x_c, mod_c[2], g1, "hgrn_out")
    return new_c, new_l


def _softmax_pv(s_list, v_list):
    mx = functools.reduce(jnp.maximum, [jnp.max(s, axis=-1, keepdims=True) for s in s_list])
    ps = [jnp.exp(s - mx) for s in s_list]
    den = functools.reduce(jnp.add, [jnp.sum(p, axis=-1, keepdims=True) for p in ps])
    acc = functools.reduce(jnp.add, [jnp.dot(p.astype(BF16), v, preferred_element_type=F32)
                                     for p, v in zip(ps, v_list)])
    return acc / den


def _na_lat_body(q_ref, k_ref, v_ref, kc_ref, vc_ref, t2_ref, o_ref, *, rows, kh, scale):
    w = GRID_W
    rb = NA_ROW_BLOCK
    nk = kh + rb
    n_rel = 2 * kh - 1
    first = lax.broadcasted_iota(jnp.int32, (rb * w, LANES), 1) < (LANES // 2)
    kc = kc_ref[0]
    vc = vc_ref[0]

    def bias_index(r, ks, m):
        r0 = jnp.clip(r - kh // 2, 0, rows - kh)
        ka = ks + 2 * m
        rel_a = ka - r + (kh - 1)
        in_a = (ka >= r0) & (ka < r0 + kh)
        in_b = (ka + 1 >= r0) & (ka + 1 < r0 + kh)
        both, only_b, only_a = rel_a, (n_rel - 1) + rel_a + 1, (2 * n_rel - 1) + rel_a
        return jnp.where(in_a, jnp.where(in_b, both, only_a), jnp.where(in_b, only_b, 3 * n_rel - 1))

    def block(i, carry):
        rq = i * rb
        ks = jnp.clip(rq - kh // 2, 0, rows - nk)
        q2 = q_ref[0, pl.ds(pl.multiple_of(rq * w, rb * w), rb * w), :] * scale
        k2 = k_ref[0, pl.ds(pl.multiple_of(ks * w, w), nk * w), :]
        v2 = v_ref[0, pl.ds(pl.multiple_of(ks * w, w), nk * w), :]
        outs = []
        for hh in range(2):
            qm = jnp.where(first if hh == 0 else jnp.logical_not(first), q2, jnp.zeros_like(q2))
            bias = jnp.concatenate(
                [jnp.concatenate([t2_ref[hh, bias_index(rq + a, ks, m)] for m in range(nk // 2)], axis=-1)
                 for a in range(rb)], axis=0)
            s_w = lax.dot_general(qm, k2, NT_DIMS, preferred_element_type=F32) + bias
            s_c = lax.dot_general(qm, kc, NT_DIMS, preferred_element_type=F32)
            outs.append(_softmax_pv([s_w, s_c], [v2, vc]))
        o_ref[0, pl.ds(pl.multiple_of(rq * w, rb * w), rb * w), :] = (
            jnp.where(first, outs[0], outs[1]).astype(o_ref.dtype))
        return carry

    lax.fori_loop(0, rows // rb, block, 0, unroll=2)


def _na_ctx_body(q_ref, k_ref, v_ref, o_ref, *, scale):
    n = q_ref.shape[1]
    first = lax.broadcasted_iota(jnp.int32, (n, LANES), 1) < (LANES // 2)
    q2 = q_ref[0] * scale
    k2 = k_ref[0]
    v2 = v_ref[0]
    outs = []
    for hh in range(2):
        qm = jnp.where(first if hh == 0 else jnp.logical_not(first), q2, jnp.zeros_like(q2))
        s = lax.dot_general(qm, k2, NT_DIMS, preferred_element_type=F32)
        outs.append(_softmax_pv([s], [v2]))
    o_ref[0] = jnp.where(first, outs[0], outs[1]).astype(o_ref.dtype)


def _na_bias_table(rpb, kh, kw):
    w = jnp.arange(GRID_W)[:, None]
    c = jnp.arange(GRID_W)[None, :]
    c0 = jnp.clip(w - kw // 2, 0, GRID_W - kw)
    inside = (c >= c0) & (c < c0 + kw)
    pad = GRID_W - kw
    padded = jnp.pad(rpb.astype(F32), ((0, 0), (0, 0), (pad, pad)))
    shifted = jnp.stack([padded[:, :, pad + kw - 1 - q:pad + kw - 1 - q + GRID_W] for q in range(GRID_W)], axis=2)
    t = jnp.where(inside[None, None], shifted, NEG_BIG)
    off = jnp.full_like(t, NEG_BIG)
    pair = lambda a, b: jnp.concatenate([a, b], axis=-1)
    return jnp.concatenate([pair(t[:, :-1], t[:, 1:]), pair(off, t), pair(t, off), pair(off, off)[:, :1]], axis=1)


def _na_mixer(x_c, x_l, ng, mod_c, mod_l, w_qkv, rpb, w_out, g1, want_ctx):
    b, l, d = x_l.shape
    lc = x_c.shape[1]
    heads = rpb.shape[0]
    na_rows, na_cols = (rpb.shape[1] + 1) // 2, (rpb.shape[2] + 1) // 2
    dh = d // heads
    assert 2 * dh == LANES and l % GRID_W == 0
    rows = l // GRID_W
    kh = min(na_rows, rows)
    assert kh == na_rows and (kh + NA_ROW_BLOCK) % 2 == 0
    assert rows % NA_ROW_BLOCK == 0 and rows >= kh + NA_ROW_BLOCK
    scale = dh ** -0.5
    nhp = heads // 2
    qkv_c = _norm_mod_proj(x_c, ng, mod_c[0], mod_c[1], w_qkv, BF16)
    qkv_l = _norm_mod_proj(x_l, ng, mod_l[0], mod_l[1], w_qkv, BF16)
    t2 = _na_bias_table(rpb, kh, na_cols)
    o_l = pl.pallas_call(
        functools.partial(_na_lat_body, rows=rows, kh=kh, scale=scale),
        grid=(b, nhp),
        in_specs=[pl.BlockSpec((1, l, LANES), lambda i, p: (i, 0, p)),
                  pl.BlockSpec((1, l, LANES), lambda i, p: (i, 0, nhp + p)),
                  pl.BlockSpec((1, l, LANES), lambda i, p: (i, 0, 2 * nhp + p)),
                  pl.BlockSpec((1, lc, LANES), lambda i, p: (i, 0, nhp + p)),
                  pl.BlockSpec((1, lc, LANES), lambda i, p: (i, 0, 2 * nhp + p)),
                  pl.BlockSpec((2,) + t2.shape[1:], lambda i, p: (p, 0, 0, 0))],
        out_specs=pl.BlockSpec((1, l, LANES), lambda i, p: (i, 0, p)),
        out_shape=jax.ShapeDtypeStruct((b, l, d), BF16),
        compiler_params=_params("parallel", "parallel"),
        name="na_latent",
    )(qkv_l, qkv_l, qkv_l, qkv_c, qkv_c, t2)
    new_l = _out_call(_out_plain_body, [(o_l, 0, d)], [w_out], x_l, mod_l[2], g1, "na_out")
    new_c = None
    if want_ctx:
        o_c = pl.pallas_call(
            functools.partial(_na_ctx_body, scale=scale),
            grid=(b, nhp),
            in_specs=[pl.BlockSpec((1, lc, LANES), lambda i, p: (i, 0, p)),
                      pl.BlockSpec((1, lc, LANES), lambda i, p: (i, 0, nhp + p)),
                      pl.BlockSpec((1, lc, LANES), lambda i, p: (i, 0, 2 * nhp + p))],
            out_specs=pl.BlockSpec((1, lc, LANES), lambda i, p: (i, 0, p)),
            out_shape=jax.ShapeDtypeStruct((b, lc, d), BF16),
            compiler_params=_params("parallel", "parallel"),
            name="na_context",
        )(qkv_c, qkv_c, qkv_c)
        new_c = _out_call(_out_plain_body, [(o_c, 0, d)], [w_out], x_c, mod_c[2], g1, "na_out")
    return new_c, new_l


S5_GROUP_BLOCK = 8


def _s5_kernel_body(c_ref, w_ref, o_ref):
    for i in range(c_ref.shape[0]):
        o_ref[i] = jnp.dot(c_ref[i], w_ref[i], preferred_element_type=F32, precision=lax.Precision.HIGHEST)


def _s5_impulse(cmat, wmat):
    n = cmat.shape[0]
    gb = S5_GROUP_BLOCK
    return pl.pallas_call(
        _s5_kernel_body,
        grid=(n // gb,),
        in_specs=[pl.BlockSpec((gb,) + cmat.shape[1:], lambda i: (i, 0, 0)),
                  pl.BlockSpec((gb,) + wmat.shape[1:], lambda i: (i, 0, 0))],
        out_specs=pl.BlockSpec((gb, cmat.shape[1], wmat.shape[2]), lambda i: (i, 0, 0)),
        out_shape=jax.ShapeDtypeStruct((n, cmat.shape[1], wmat.shape[2]), F32),
        compiler_params=_params("parallel"),
        name="s5_impulse",
    )(cmat, wmat)


def _s5_operators(lam_re, lam_im, log_dt, b_re, b_im, c_re, c_im, t):
    _, g, p = lam_re.shape
    cg = b_re.shape[-1]
    lam = lax.complex(lam_re.astype(F32), lam_im.astype(F32))
    ldt = lam * jnp.exp(log_dt.astype(F32))[..., None]
    a = jnp.exp(ldt)
    bbar = ((a - 1.0) / lam)[..., None] * lax.complex(b_re.astype(F32), b_im.astype(F32))
    cm = lax.complex(c_re.astype(F32), c_im.astype(F32))
    apow = jnp.exp(ldt[..., None] * jnp.arange(t + 1, dtype=F32))
    w = apow[..., :t, None] * bbar[:, :, :, None, :]
    wmat = jnp.concatenate([jnp.real(w), jnp.imag(w)], axis=2).reshape(2 * g, 2 * p, t * cg)
    cmat = jnp.concatenate([jnp.real(cm), -jnp.imag(cm)], axis=-1).reshape(2 * g, cg, 2 * p)
    k = _s5_impulse(cmat, wmat).reshape(2, g, cg, t, cg)
    k = jnp.transpose(k, (0, 1, 3, 4, 2))
    kfull = jnp.concatenate([k[1, :, :0:-1], (k[0, :, :1] + k[1, :, :1]), k[0, :, 1:]], axis=1)
    lagvec = jnp.transpose(kfull, (0, 2, 1, 3)).reshape(g, cg, (2 * t - 1) * cg).astype(BF16)
    toep = jnp.stack([lagvec[:, :, (t - 1 - s) * cg:(2 * t - 1 - s) * cg] for s in range(t)], axis=1)
    toep = toep.reshape(g, t * cg, t * cg)

    def state_in(wd, flip):
        wd = wd[:, :, ::-1] if flip else wd
        m = jnp.transpose(wd, (0, 2, 3, 1)).reshape(g, t * cg, p)
        return jnp.concatenate([jnp.real(m), jnp.imag(m)], axis=-1)

    def state_out(cd, pw):
        n = cd[:, :, :, None] * pw[:, None, :, :]
        n = jnp.transpose(n, (0, 2, 3, 1)).reshape(g, p, t * cg)
        return jnp.concatenate([jnp.real(n), -jnp.imag(n)], axis=1)

    m_f = state_in(w[0], True)
    m_b = state_in(w[1], False)
    n_f = state_out(cm[0], apow[0][..., 1:])
    n_b = state_out(cm[1], apow[1][..., :0:-1])
    at = apow[..., t]
    a1 = jnp.concatenate([jnp.real(at), jnp.real(at)], axis=-1)
    a2 = jnp.concatenate([-jnp.imag(at), jnp.imag(at)], axis=-1)
    bf = lambda m: m.astype(BF16)
    return dict(toep=bf(toep), m_f=bf(m_f), m_b=bf(m_b), n_f=bf(n_f), n_b=bf(n_b), a1=a1, a2=a2)


def _s5_body(u_ref, dsk_ref, toep_ref, mf_ref, mb_ref, nf_ref, nb_ref, a1_ref, a2_ref, x0_ref,
             y_ref, xfin_ref, z_scr, zs_scr, xin_scr):
    gb, nc = u_ref.shape[1], u_ref.shape[2]
    half = LANES // 2
    ubs = [u_ref[0, gi].astype(BF16) for gi in range(gb)]
    for dr, m_ref in enumerate((mf_ref, mb_ref)):
        z = jnp.stack([jnp.dot(ubs[gi], m_ref[gi], preferred_element_type=F32) for gi in range(gb)])
        z_scr[dr] = jnp.swapaxes(z, 0, 1)

    swap = lambda t: pltpu.roll(t, half, t.ndim - 1)
    for dr in range(2):
        zs_scr[dr] = swap(z_scr[dr].reshape(nc * gb, LANES)).reshape(nc, gb, LANES)

    coef = [(a1_ref[dr], a2_ref[dr]) for dr in range(2)]

    def scan(j, xs):
        out = []
        for dr in range(2):
            jj = j if dr == 0 else nc - 1 - j
            x, xw = xs[2 * dr], xs[2 * dr + 1]
            a1, a2 = coef[dr]
            xin_scr[dr, jj] = x
            out.append(a1 * x + a2 * xw + z_scr[dr, jj])
            out.append(a1 * xw - a2 * x + zs_scr[dr, jj])
        return tuple(out)

    x0f, x0b = x0_ref[0, 0], x0_ref[0, 1]
    xf, _, xb, _ = lax.fori_loop(0, nc, scan, (x0f, swap(x0f), x0b, swap(x0b)))
    xfin_ref[0, 0] = xf
    xfin_ref[0, 1] = xb
    xin = [jnp.swapaxes(xin_scr[dr], 0, 1).astype(BF16) for dr in range(2)]
    for gi in range(gb):
        u = u_ref[0, gi]
        y = jnp.dot(ubs[gi], toep_ref[gi], preferred_element_type=F32) + u * dsk_ref[gi]
        y = y + jnp.dot(xin[0][gi], nf_ref[gi], preferred_element_type=F32)
        y = y + jnp.dot(xin[1][gi], nb_ref[gi], preferred_element_type=F32)
        y_ref[0, gi] = y


def _piece_transpose(src, piece_of, cg):
    n = len(src)
    cur = list(src)
    d = n // 2
    while d >= 1:
        upper = (piece_of & d) != 0
        nxt = list(cur)
        for a in range(n):
            if a & d == 0:
                lo, hi = cur[a], cur[a + d]
                nxt[a] = jnp.where(upper, pltpu.roll(hi, d * cg, 1), lo)
                nxt[a + d] = jnp.where(upper, hi, pltpu.roll(lo, (n - d) * cg, 1))
        cur = nxt
        d //= 2
    return cur


def _to_chunks(h_scr, o_ref, t, cg):
    ncb = h_scr.shape[1] // t
    pieces = LANES // cg
    piece_of = lax.broadcasted_iota(jnp.int32, (ncb, LANES), 1) // cg

    def lane_block(lb, carry):
        for tq in range(t // pieces):
            steps = [h_scr[lb, pl.ds(tq * pieces + tr, ncb, stride=t), :] for tr in range(pieces)]
            for gl, col in enumerate(_piece_transpose(steps, piece_of, cg)):
                o_ref[0, lb * pieces + gl, :, tq * LANES:(tq + 1) * LANES] = col
        return carry

    lax.fori_loop(0, h_scr.shape[0], lane_block, 0)


def _from_chunks(y_ref, y_scr, t, cg):
    ncb = y_ref.shape[2]
    pieces = LANES // cg
    piece_of = lax.broadcasted_iota(jnp.int32, (ncb, LANES), 1) // cg

    def lane_block(lb, carry):
        for tq in range(t // pieces):
            cols = [y_ref[0, lb * pieces + gl, :, tq * LANES:(tq + 1) * LANES] for gl in range(pieces)]
            for tr, step in enumerate(_piece_transpose(cols, piece_of, cg)):
                y_scr[lb, pl.ds(tq * pieces + tr, ncb, stride=t), :] = step
        return carry

    lax.fori_loop(0, y_scr.shape[0], lane_block, 0)


def _norm_chunks_body(x_ref, g_ref, sh_ref, sc_ref, o_ref, h_scr, *, t, cg):
    h = _norm_mod(x_ref[...], g_ref[...], sh_ref[0], sc_ref[0])
    for lb in range(h_scr.shape[0]):
        h_scr[lb] = h[:, lb * LANES:(lb + 1) * LANES]
    _to_chunks(h_scr, o_ref, t, cg)


def _norm_mod_chunks(x, g, sh, sc, groups, t):
    b, l, d = x.shape
    cg = d // groups
    tm = _row_tile(l, 512)
    per = l // tm
    assert tm % t == 0 and (tm // t) % SUBLANES == 0 and LANES % cg == 0 and t % (LANES // cg) == 0
    return pl.pallas_call(
        functools.partial(_norm_chunks_body, t=t, cg=cg),
        grid=(b * per,),
        in_specs=[pl.BlockSpec((tm, d), lambda i: (i, 0)),
                  pl.BlockSpec((1, d), lambda i: (0, 0)),
                  _mod_spec(sh, per), _mod_spec(sc, per)],
        out_specs=pl.BlockSpec((1, groups, tm // t, t * cg), lambda i: (i // per, 0, i % per, 0)),
        out_shape=jax.ShapeDtypeStruct((b, groups, l // t, t * cg), F32),
        scratch_shapes=[pltpu.VMEM((d // LANES, tm, LANES), F32)],
        compiler_params=_params("parallel"),
        name="norm_mod_chunks",
    )(x.reshape(b * l, d), g.reshape(1, d), sh.table, sc.table)


def _glu_chunks_body(y_ref, w_ref, res_ref, gate_ref, g1_ref, o_ref, y_scr, *, t, cg):
    _from_chunks(y_ref, y_scr, t, cg)
    d = o_ref.shape[-1]
    y = jnp.concatenate([y_scr[lb] for lb in range(y_scr.shape[0])], axis=-1)
    a = _gelu_tanh(y).astype(BF16)
    ag = jnp.dot(a, w_ref[...], preferred_element_type=F32)
    o_ref[...] = _residual(res_ref, gate_ref, g1_ref, ag[:, :d] * jax.nn.sigmoid(ag[:, d:]))


def _glu_out_chunks(y, w_glu, res, gate, g1, t):
    b, l, d = res.shape
    groups = y.shape[1]
    cg = d // groups
    tm = _row_tile(l, 512)
    per = l // tm
    out = pl.pallas_call(
        functools.partial(_glu_chunks_body, t=t, cg=cg),
        grid=(b * per,),
        in_specs=[pl.BlockSpec((1, groups, tm // t, t * cg), lambda i: (i // per, 0, i % per, 0)),
                  _layer_spec(w_glu, w_glu.shape, lambda i: (0, 0)),
                  pl.BlockSpec((tm, d), lambda i: (i, 0)),
                  _mod_spec(gate, per),
                  pl.BlockSpec((1, d), lambda i: (0, 0))],
        out_specs=pl.BlockSpec((tm, d), lambda i: (i, 0)),
        out_shape=jax.ShapeDtypeStruct((b * l, d), F32),
        scratch_shapes=[pltpu.VMEM((d // LANES, tm, LANES), F32)],
        compiler_params=_params("parallel"),
        name="s5_glu_out",
    )(y, w_glu.stack, res.reshape(b * l, d), gate.table, g1.reshape(1, d))
    return out.reshape(b, l, d)


def _s5_scan(u, ops, dsk, x0):
    b, g, nc, tc = u.shape
    gb = S5_GROUP_BLOCK
    per_g = lambda shape: pl.BlockSpec((gb,) + shape, lambda j, i: (j,) + (0,) * len(shape))
    y, xfin = pl.pallas_call(
        _s5_body,
        grid=(g // gb, b),
        in_specs=[pl.BlockSpec((1, gb, nc, tc), lambda j, i: (i, j, 0, 0)),
                  per_g((1, tc)), per_g((tc, tc)), per_g((tc, LANES)), per_g((tc, LANES)),
                  per_g((LANES, tc)), per_g((LANES, tc)),
                  pl.BlockSpec((2, gb, LANES), lambda j, i: (0, j, 0)),
                  pl.BlockSpec((2, gb, LANES), lambda j, i: (0, j, 0)),
                  pl.BlockSpec((1, 2, gb, LANES), lambda j, i: (i, 0, j, 0))],
        out_specs=[pl.BlockSpec((1, gb, nc, tc), lambda j, i: (i, j, 0, 0)),
                   pl.BlockSpec((1, 2, gb, LANES), lambda j, i: (i, 0, j, 0))],
        out_shape=[jax.ShapeDtypeStruct((b, g, nc, tc), F32), jax.ShapeDtypeStruct((b, 2, g, LANES), F32)],
        scratch_shapes=[pltpu.VMEM((2, nc, gb, LANES), F32)] * 3,
        compiler_params=_params("parallel", "parallel"),
        name="s5_scan",
    )(u, dsk, ops["toep"], ops["m_f"], ops["m_b"], ops["n_f"], ops["n_b"], ops["a1"], ops["a2"], x0)
    return y, xfin


def _s5_mixer(x_c, x_l, ng, mod_c, mod_l, lam_re, lam_im, log_dt, b_re, b_im, c_re, c_im, d_skip, w_glu, g1,
              want_ctx):
    b, l, d = x_l.shape
    g, p = lam_re.shape[1], lam_re.shape[2]
    cg = d // g
    assert 2 * p == LANES and g % S5_GROUP_BLOCK == 0
    ops = _s5_operators(lam_re, lam_im, log_dt, b_re, b_im, c_re, c_im, S5_CHUNK)
    dsk = jnp.tile(d_skip.astype(F32).reshape(g, 1, cg), (1, 1, S5_CHUNK))
    u_c = _norm_mod_chunks(x_c, ng, mod_c[0], mod_c[1], g, S5_CHUNK)
    u_l = _norm_mod_chunks(x_l, ng, mod_l[0], mod_l[1], g, S5_CHUNK)
    y_c, x_ctx = _s5_scan(u_c, ops, dsk, jnp.zeros((b, 2, g, LANES), F32))
    y_l, _ = _s5_scan(u_l, ops, dsk, x_ctx)
    new_l = _glu_out_chunks(y_l, w_glu, x_l, mod_l[2], g1, S5_CHUNK)
    new_c = _glu_out_chunks(y_c, w_glu, x_c, mod_c[2], g1, S5_CHUNK) if want_ctx else None
    return new_c, new_l


FFN_HALO = 16
FFN_TN = 256
FFN_TILES_PER_STEP = 11
FFN_ROW_SPLIT = 256


def _ffn_body(xp_ref, x_ref, xn_ref, g2_ref, sh_ref, sc_ref, *rest, per, n_tiles, tps):
    w_refs = rest[:2 * tps]
    cw_ref, cb_ref, wo_ref, gate_ref, g3_ref, o_ref, h_scr, gat_scr = rest[2 * tps:]
    i, j = pl.program_id(0), pl.program_id(1)
    tm = x_ref.shape[0]
    hl = FFN_HALO
    f = wo_ref.shape[0]
    n_steps = pl.cdiv(n_tiles, tps)
    when = (lambda cond: lambda fn: fn()) if n_steps == 1 else pl.when

    @when(j == 0)
    def _():
        g, sh, sc = g2_ref[...], sh_ref[0], sc_ref[0]
        keep_p = ((i % per) != 0).astype(F32)
        keep_n = ((i % per) != per - 1).astype(F32)
        h_scr[0:hl] = (_norm_mod(xp_ref[...], g, sh, sc) * keep_p).astype(BF16)
        h_scr[hl:hl + tm] = _norm_mod(x_ref[...], g, sh, sc).astype(BF16)
        h_scr[hl + tm:] = (_norm_mod(xn_ref[...], g, sh, sc) * keep_n).astype(BF16)

    sub = FFN_ROW_SPLIT if tm % FFN_ROW_SPLIT == 0 else tm
    ns = tm // sub
    edge = SUBLANES
    tn = w_refs[0].shape[1]

    def pieces(w_ref):
        out = []
        for s in range(ns):
            lo = hl + s * sub - (hl if s == 0 else 0)
            hi = hl + (s + 1) * sub + (hl if s == ns - 1 else 0)
            out.append(jnp.dot(h_scr[lo:hi], w_ref[...], preferred_element_type=F32))
        return out

    def conv(p, s, cw, cb):
        base = hl if s == 0 else 0
        left = p[s][hl - edge:hl] if s == 0 else p[s - 1][-edge:]
        right = p[s][base + sub:base + sub + edge] if s == ns - 1 else p[s + 1][:edge]
        ext = jnp.concatenate([left, p[s][base:base + sub], right], axis=0)
        um = pltpu.roll(ext, 1, 0)[edge:edge + sub]
        up = pltpu.roll(ext, sub + 2 * edge - 1, 0)[edge:edge + sub]
        return cb + um * cw[0:1] + ext[edge:edge + sub] * cw[1:2] + up * cw[2:3]

    def tile(k):
        col = k * tn if n_steps == 1 else pl.multiple_of((j * tps + k) * tn, tn)
        cols_a, cols_v = pl.ds(col, tn), pl.ds(pl.multiple_of(f + col, tn), tn)
        pa, pv = pieces(w_refs[2 * k]), pieces(w_refs[2 * k + 1])
        for s in range(ns):
            a = conv(pa, s, cw_ref[:, cols_a], cb_ref[:, cols_a])
            v = conv(pv, s, cw_ref[:, cols_v], cb_ref[:, cols_v])
            gat_scr[s * sub:(s + 1) * sub, cols_a] = (a * jax.nn.sigmoid(a) * v).astype(BF16)

    always = n_tiles - tps * (n_steps - 1)
    for k in range(tps):
        if k < always:
            tile(k)
        else:
            pl.when(j * tps + k < n_tiles)(functools.partial(tile, k))

    @when(j == n_steps - 1)
    def _():
        y = jnp.dot(gat_scr[...], wo_ref[...], preferred_element_type=F32)
        o_ref[...] = x_ref[...] + gate_ref[0] * _rms(y, g3_ref[...])


def _conv_ffn_block(x, ng2, mod, w_in, conv_w, conv_b, w_out, ng3):
    b, l, d = x.shape
    f = w_out.shape[0]
    tn, hl = FFN_TN, FFN_HALO
    assert f % tn == 0 and conv_w.shape[0] == 3 and w_in.shape[1] == 2 * f
    nj = f // tn
    tm = _row_tile(l, 1024)
    per = l // tm
    hb = tm // hl
    last_hb = b * l // hl - 1
    cb = conv_b.reshape(1, 2 * f)
    x2 = x.reshape(b * l, d)
    tps = min(FFN_TILES_PER_STEP, nj)
    once = dict(pipeline_mode=pl.Buffered(1)) if tps == nj else {}
    w_specs = []
    for k in range(tps):
        w_specs.append(_layer_spec(w_in, (d, tn), lambda i, j, k=k: (0, jnp.minimum(j * tps + k, nj - 1)), **once))
        w_specs.append(_layer_spec(w_in, (d, tn), lambda i, j, k=k: (0, nj + jnp.minimum(j * tps + k, nj - 1)),
                                   **once))
    whole = lambda arr: pl.BlockSpec(arr.shape, lambda i, j: (0,) * arr.ndim)
    out = pl.pallas_call(
        functools.partial(_ffn_body, per=per, n_tiles=nj, tps=tps),
        grid=(b * per, pl.cdiv(nj, tps)),
        in_specs=[pl.BlockSpec((hl, d), lambda i, j: (jnp.maximum(i * hb - 1, 0), 0)),
                  pl.BlockSpec((tm, d), lambda i, j: (i, 0)),
                  pl.BlockSpec((hl, d), lambda i, j: (jnp.minimum((i + 1) * hb, last_hb), 0)),
                  pl.BlockSpec((1, d), lambda i, j: (0, 0)),
                  _mod_spec(mod[3], per), _mod_spec(mod[4], per),
                  *w_specs, whole(conv_w), whole(cb),
                  _layer_spec(w_out, (f, d), lambda i, j: (0, 0), pipeline_mode=pl.Buffered(1)),
                  _mod_spec(mod[5], per),
                  pl.BlockSpec((1, d), lambda i, j: (0, 0))],
        out_specs=pl.BlockSpec((tm, d), lambda i, j: (i, 0)),
        out_shape=jax.ShapeDtypeStruct((b * l, d), F32),
        scratch_shapes=[pltpu.VMEM((tm + 2 * hl, d), BF16), pltpu.VMEM((tm, f), BF16)],
        compiler_params=_params("parallel", "arbitrary"),
        name="conv_ffn",
    )(x2, x2, x2, ng2.reshape(1, d), mod[3].table, mod[4].table, *([w_in.stack] * (2 * tps)), conv_w, cb,
      w_out.stack,
      mod[5].table, ng3.reshape(1, d))
    return out.reshape(b, l, d)


def kernel(x, c, ctx, c_ctx, w_mod, b_mod, norm_g, a_w_in, a_lower_logits, a_out_g, a_w_out, b_w_qkv, b_rpb,
           b_w_out, c_lam_re, c_lam_im, c_log_dt, c_b_re, c_b_im, c_c_re, c_c_im, c_d, c_w_glu, f_w_in,
           f_conv_w, f_conv_b, f_w_out):
    bsz, _, d = x.shape
    depth = w_mod.shape[0]
    p = jax.nn.softmax(a_lower_logits.astype(F32), axis=0)
    lower = jnp.cumsum(p, axis=0) - p[0]
    n_rows = -(-(bsz + 1) // SUBLANES) * SUBLANES
    c_rows = jnp.zeros((n_rows, d), F32).at[:bsz].set(c).at[bsz].set(c_ctx)
    table = _modulation(c_rows, w_mod, b_mod).reshape(depth * n_rows * N_MOD, 1, d)
    wa_in, wa_out, wb_qkv, wb_out, wc_glu, wf_in, wf_out = (
        w.astype(BF16) for w in (a_w_in, a_w_out, b_w_qkv, b_w_out, c_w_glu, f_w_in, f_w_out))
    lat, cx = x, ctx
    for i in range(depth):
        kind, j = i % N_MIXERS, i // N_MIXERS
        last = i == depth - 1
        mod_l = [ModRow(table, i * n_rows * N_MOD + k, N_MOD) for k in range(N_MOD)]
        mod_c = [ModRow(table, (i * n_rows + bsz) * N_MOD + k, 0) for k in range(N_MOD)]
        ng = norm_g[i]
        if kind == 0:
            cx1, lat = _hgrn2_mixer(cx, lat, ng[0], mod_c, mod_l, LayerOf(wa_in, j), lower[i], a_out_g[j],
                                    LayerOf(wa_out, j), ng[1], not last)
        elif kind == 1:
            cx1, lat = _na_mixer(cx, lat, ng[0], mod_c, mod_l, LayerOf(wb_qkv, j), b_rpb[j], LayerOf(wb_out, j),
                                 ng[1], not last)
        else:
            cx1, lat = _s5_mixer(cx, lat, ng[0], mod_c, mod_l, c_lam_re[j], c_lam_im[j], c_log_dt[j], c_b_re[j],
                                 c_b_im[j], c_c_re[j], c_c_im[j], c_d[j], LayerOf(wc_glu, j), ng[1], not last)
        ffn = (LayerOf(wf_in, i), f_conv_w[i], f_conv_b[i], LayerOf(wf_out, i), ng[3])
        lat = _conv_ffn_block(lat, ng[2], mod_l, *ffn)
        if not last:
            cx = _conv_ffn_block(cx1, ng[2], mod_c, *ffn)
    return lat
```

```python
import functools
from typing import NamedTuple

import jax
import jax.numpy as jnp
from jax import lax
from jax.experimental import pallas as pl
from jax.experimental.pallas import tpu as pltpu

F32 = jnp.float32
BF16 = jnp.bfloat16

EPS = 1e-6
F_MIN = 1e-30
N_MOD = 6
N_MIXERS = 3
A_HEADS = 8
GRID_W = 64
S5_CHUNK = 16
NEG_BIG = -1e30
NA_ROW_BLOCK = 4
SUBLANES = 8
LANES = 128
VMEM_LIMIT_BYTES = 56 * 1024 * 1024

NT_DIMS = (((1,), (1,)), ((), ()))
TN_DIMS = (((0,), (0,)), ((), ()))


def _params(*sem):
    return pltpu.CompilerParams(dimension_semantics=sem, vmem_limit_bytes=VMEM_LIMIT_BYTES)


def _row_tile(n, want):
    t = min(n, want)
    assert n % t == 0, (n, t)
    return t


def _rms(y, g):
    return y * lax.rsqrt(jnp.mean(y * y, axis=-1, keepdims=True) + EPS) * g


def _mod_body(c_ref, w_ref, b_ref, o_ref):
    c = c_ref[...]
    s = (c * jax.nn.sigmoid(c)).astype(BF16)
    o_ref[0] = jnp.dot(s, w_ref[0].astype(BF16), preferred_element_type=F32) + b_ref[0]


def _modulation(c_rows, w_mod, b_mod):
    depth, d, n = w_mod.shape
    r = c_rows.shape[0]
    tn = n // 4
    return pl.pallas_call(
        _mod_body,
        grid=(depth, n // tn),
        in_specs=[pl.BlockSpec((r, d), lambda i, j: (0, 0)),
                  pl.BlockSpec((1, d, tn), lambda i, j: (i, 0, j)),
                  pl.BlockSpec((1, 1, tn), lambda i, j: (i, 0, j))],
        out_specs=pl.BlockSpec((1, r, tn), lambda i, j: (i, 0, j)),
        out_shape=jax.ShapeDtypeStruct((depth, r, n), F32),
        compiler_params=_params("parallel", "parallel"),
        name="adaln_mod",
    )(c_rows, w_mod, b_mod.reshape(depth, 1, n))


def _norm_mod(x, g, sh, sc):
    return _rms(x, g) * (1.0 + sc) + sh


class ModRow(NamedTuple):
    table: jax.Array
    first: int
    step: int


def _mod_spec(m, per):
    return pl.BlockSpec((1, 1, m.table.shape[-1]), lambda *idx: (m.first + m.step * (idx[0] // per), 0, 0))


class LayerOf(NamedTuple):
    stack: jax.Array
    layer: int

    @property
    def shape(self):
        return self.stack.shape[1:]


def _layer_spec(w, block, index_map, **kw):
    return pl.BlockSpec((None,) + tuple(block), lambda *idx: (w.layer,) + tuple(index_map(*idx)), **kw)


def _proj_body(x_ref, g_ref, sh_ref, sc_ref, w_ref, o_ref):
    h = _norm_mod(x_ref[...], g_ref[...], sh_ref[0], sc_ref[0]).astype(BF16)
    o_ref[...] = jnp.dot(h, w_ref[...], preferred_element_type=F32).astype(o_ref.dtype)


def _proj_silu_body(x_ref, g_ref, sh_ref, sc_ref, w_ref, o_ref, os_ref, *, lo, hi):
    h = _norm_mod(x_ref[...], g_ref[...], sh_ref[0], sc_ref[0]).astype(BF16)
    y = jnp.dot(h, w_ref[...], preferred_element_type=F32)
    o_ref[:, :lo] = y[:, :lo]
    o_ref[:, lo:] = y[:, hi:]
    gp = y[:, lo:hi]
    os_ref[...] = (gp * jax.nn.sigmoid(gp)).astype(os_ref.dtype)


def _norm_mod_proj(x, g, sh, sc, w, out_dtype, silu_cols=None):
    b, l, d = x.shape
    n = w.shape[1]
    tm = _row_tile(l, 512)
    per = l // tm
    body, widths, dtypes = _proj_body, [n], [out_dtype]
    if silu_cols is not None:
        lo, hi = silu_cols
        body, widths, dtypes = functools.partial(_proj_silu_body, lo=lo, hi=hi), [n - (hi - lo), hi - lo], [out_dtype, BF16]
    outs = pl.pallas_call(
        body,
        grid=(b * per,),
        in_specs=[pl.BlockSpec((tm, d), lambda i: (i, 0)),
                  pl.BlockSpec((1, d), lambda i: (0, 0)),
                  _mod_spec(sh, per), _mod_spec(sc, per),
                  _layer_spec(w, (d, n), lambda i: (0, 0), pipeline_mode=pl.Buffered(1))],
        out_specs=[pl.BlockSpec((tm, wd), lambda i: (i, 0)) for wd in widths],
        out_shape=[jax.ShapeDtypeStruct((b * l, wd), dt) for wd, dt in zip(widths, dtypes)],
        compiler_params=_params("parallel"),
        name="norm_mod_proj",
    )(x.reshape(b * l, d), g.reshape(1, d), sh.table, sc.table, w.stack)
    outs = [o.reshape(b, l, o.shape[-1]) for o in outs]
    return outs[0] if silu_cols is None else tuple(outs)


def _residual(res_ref, gate_ref, g1_ref, y):
    return res_ref[...] + gate_ref[0] * _rms(y, g1_ref[...])


def _out_plain_body(a_ref, w_ref, res_ref, gate_ref, g1_ref, o_ref):
    y = jnp.dot(a_ref[...], w_ref[...], preferred_element_type=F32)
    o_ref[...] = _residual(res_ref, gate_ref, g1_ref, y)


def _out_hgrn_body(of_ref, ob_ref, gp_ref, og_ref, w_ref, res_ref, gate_ref, g1_ref, o_ref):
    o = of_ref[...].astype(F32) + ob_ref[...].astype(F32)
    parts = []
    for h in range(A_HEADS):
        oh = o[:, h * LANES:(h + 1) * LANES]
        parts.append(oh * lax.rsqrt(jnp.mean(oh * oh, axis=-1, keepdims=True) + EPS))
    a = jnp.concatenate(parts, axis=-1) * og_ref[...] * gp_ref[...].astype(F32)
    y = jnp.dot(a.astype(BF16), w_ref[...], preferred_element_type=F32)
    o_ref[...] = _residual(res_ref, gate_ref, g1_ref, y)


def _gelu_tanh(y):
    return 0.5 * y * (1.0 + jnp.tanh(0.7978845608028654 * (y + 0.044715 * (y * y * y))))


def _out_call(body, row_inputs, const_inputs, res, gate, g1, name):
    b, l, d = res.shape
    tm = _row_tile(l, 512)
    per = l // tm
    in_specs, args = [], []
    for arr, blk, width in row_inputs:
        in_specs.append(pl.BlockSpec((tm, width), lambda i, blk=blk: (i, blk)))
        args.append(arr.reshape(b * l, arr.shape[-1]))
    for arr in const_inputs:
        if isinstance(arr, LayerOf):
            in_specs.append(_layer_spec(arr, arr.shape, lambda i, nd=len(arr.shape): (0,) * nd))
            args.append(arr.stack)
        else:
            in_specs.append(pl.BlockSpec(arr.shape, lambda i, nd=arr.ndim: (0,) * nd))
            args.append(arr)
    in_specs += [pl.BlockSpec((tm, d), lambda i: (i, 0)),
                 _mod_spec(gate, per),
                 pl.BlockSpec((1, d), lambda i: (0, 0))]
    args += [res.reshape(b * l, d), gate.table, g1.reshape(1, d)]
    out = pl.pallas_call(
        body,
        grid=(b * per,),
        in_specs=in_specs,
        out_specs=pl.BlockSpec((tm, d), lambda i: (i, 0)),
        out_shape=jax.ShapeDtypeStruct((b * l, d), F32),
        compiler_params=_params("parallel"),
        name=name,
    )(*args)
    return out.reshape(b, l, d)


HGRN_CHUNK = 32
HGRN_SAFE_MIN = 1e-30
HGRN_Q_HEADROOM = 1e37


def _hgrn_gates(z, lb):
    f = lb + (1.0 - lb) * jax.nn.sigmoid(z)
    return jnp.maximum(f, F_MIN), 1.0 - f


def _hgrn_chunk_prep(q, v, z, lb, reverse):
    c = q.shape[0]
    half = c // 2
    fm, kk = _hgrn_gates(z, lb)
    pos = lax.broadcasted_iota(jnp.int32, (c, LANES), 0)
    if reverse:
        pos = (c - 1) - pos
    second = pos >= half
    hpos = jnp.where(second, pos - half, pos)
    ph = fm
    j = 1
    while j < half:
        ph = ph * jnp.where(hpos >= j, pltpu.roll(ph, (c - j) if reverse else j, 0), 1.0)
        j *= 2
    a_last = ph[half:half + 1] if reverse else ph[half - 1:half]
    r_last = ph[0:1] if reverse else ph[c - 1:c]
    p_mid = a_last
    p_last = a_last * r_last
    pp = jnp.where(second, ph * a_last, ph)
    r = jnp.where(second, ph, ph * (1.0 / a_last))
    kd = kk * (1.0 / r)
    q_max = jnp.max(jnp.abs(q), axis=0, keepdims=True)
    ok = jnp.where(p_mid >= HGRN_SAFE_MIN, r_last, 0.0) >= HGRN_SAFE_MIN
    ok = jnp.where(ok, p_mid * HGRN_Q_HEADROOM, -1.0) >= q_max
    return dict(qr=(q * r).astype(BF16), kd=kd.astype(BF16), qp=q * pp, kl=kd * r_last, vb=v.astype(BF16),
                dec=p_last, bad=jnp.where(ok, 0.0, 1.0))


def _hgrn_pair_dots(p1, p2, st):
    bf = lambda t: t.astype(BF16)
    sc1 = lax.dot_general(p1["qr"], p1["kd"], NT_DIMS, preferred_element_type=F32)
    sc2 = lax.dot_general(p2["qr"], p2["kd"], NT_DIMS, preferred_element_type=F32)
    cross = lax.dot_general(bf(p2["qp"]), bf(p1["kl"]), NT_DIMS, preferred_element_type=F32)
    q_all = jnp.concatenate([bf(p1["qp"]), bf(p2["qp"] * p1["dec"])], axis=0)
    o_st = lax.dot_general(q_all, st.astype(BF16), NT_DIMS, preferred_element_type=F32)
    k_all = jnp.concatenate([bf(p1["kl"] * p2["dec"]), bf(p2["kl"])], axis=0)
    u = lax.dot_general(jnp.concatenate([p1["vb"], p2["vb"]], axis=0), k_all, TN_DIMS, preferred_element_type=F32)
    return sc1, sc2, cross, o_st, u


def _hgrn_pair_out(p1, p2, sc1, sc2, cross, o_st, reverse):
    c = sc1.shape[0]
    row = lax.broadcasted_iota(jnp.int32, (c, c), 0)
    col = lax.broadcasted_iota(jnp.int32, (c, c), 1)
    keep = (col >= row) if reverse else (col <= row)
    tri = lambda sc: jnp.where(keep, sc, 0.0).astype(BF16)
    o1 = jnp.dot(tri(sc1), p1["vb"], preferred_element_type=F32) + o_st[:c]
    o2 = (jnp.dot(tri(sc2), p2["vb"], preferred_element_type=F32)
          + jnp.dot(cross.astype(BF16), p1["vb"], preferred_element_type=F32) + o_st[c:])
    return o1, o2


def _hgrn_exact_block(q_ref, v_ref, z_ref, lb_ref, o_scr, st_scr, dr, reverse, n_tiles):
    pos = lax.broadcasted_iota(jnp.int32, (SUBLANES, LANES), 0)
    if reverse:
        pos = (SUBLANES - 1) - pos

    def earlier(x, j):
        if j == 0:
            return x
        return pltpu.roll(x, (SUBLANES - j) if reverse else j, 0)

    def later(x, j):
        return pltpu.roll(x, j if reverse else (SUBLANES - j), 0)

    def tile(i, carry):
        ti = (n_tiles - 1 - i) if reverse else i
        r0 = pl.multiple_of(ti * SUBLANES, SUBLANES)
        for h in range(A_HEADS):
            sl = slice(h * LANES, (h + 1) * LANES)
            lb = lb_ref[:, sl]
            z = z_ref[0, pl.ds(r0, SUBLANES), sl]
            q = q_ref[0, pl.ds(r0, SUBLANES), sl]
            v = v_ref[0, pl.ds(r0, SUBLANES), sl]
            fm, kk = _hgrn_gates(z, lb)
            pp = fm
            for j in (1, 2, 4):
                pp = pp * jnp.where(pos >= j, earlier(pp, j), 1.0)
            qq = jnp.where(pos <= SUBLANES - 2, later(fm, 1), 1.0)
            for j in (1, 2, 4):
                qq = qq * jnp.where(pos <= SUBLANES - 1 - j, later(qq, j), 1.0)
            dec = pp[0:1] if reverse else pp[SUBLANES - 1:SUBLANES]
            st = st_scr[dr, h]
            o = lax.dot_general((q * pp).astype(BF16), st.astype(BF16), NT_DIMS,
                                preferred_element_type=F32)
            g = fm
            for d in range(SUBLANES):
                if d == 0:
                    e = q * kk
                else:
                    if d > 1:
                        g = g * earlier(fm, d - 1)
                    e = q * g * jnp.where(pos >= d, earlier(kk, d), 0.0)
                o = o + jnp.sum(e, axis=-1, keepdims=True) * earlier(v, d)
            o_scr[pl.ds(r0, SUBLANES), sl] = o
            u = lax.dot_general(v.astype(BF16), (kk * qq).astype(BF16), TN_DIMS,
                                preferred_element_type=F32)
            st_scr[dr, h] = dec * st + u
        return carry

    lax.fori_loop(0, n_tiles, tile, 0)


def _hgrn_scan_body(qf_ref, vf_ref, zf_ref, qb_ref, vb_ref, zb_ref, lb_ref, s0_ref, of_ref, ob_ref, sfin_ref,
                    st_scr, save_scr, ox_scr, *, n_chunks):
    step = pl.program_id(1)

    @pl.when(step == 0)
    def _():
        st_scr[...] = s0_ref[0]

    save_scr[...] = st_scr[...]
    dirs = ((qf_ref, vf_ref, zf_ref, of_ref, False), (qb_ref, vb_ref, zb_ref, ob_ref, True))
    c = HGRN_CHUNK

    def pair(i, bad):
        work, dots = [], []
        for dr, (q_ref, v_ref, z_ref, o_ref, reverse) in enumerate(dirs):
            base = pl.multiple_of(((n_chunks // 2 - 1 - i) if reverse else i) * 2 * c, 2 * c)
            rows = (pl.ds(base + c, c), pl.ds(base, c)) if reverse else (pl.ds(base, c), pl.ds(base + c, c))
            group = []
            for h in range(A_HEADS):
                sl = slice(h * LANES, (h + 1) * LANES)
                ps = [_hgrn_chunk_prep(q_ref[0, r, sl], v_ref[0, r, sl], z_ref[0, r, sl], lb_ref[:, sl], reverse)
                      for r in rows]
                bad = jnp.maximum(bad, jnp.maximum(ps[0]["bad"], ps[1]["bad"]))
                group.append((dr, h, o_ref, rows, sl, reverse, ps))
            dots += [_hgrn_pair_dots(ps[0], ps[1], st_scr[dr, h]) for dr, h, _, _, _, _, ps in group]
            work += group
        for (dr, h, _, _, _, _, ps), d in zip(work, dots):
            st_scr[dr, h] = (ps[0]["dec"] * ps[1]["dec"]) * st_scr[dr, h] + d[4]
        for (dr, h, o_ref, rows, sl, reverse, ps), d in zip(work, dots):
            o1, o2 = _hgrn_pair_out(ps[0], ps[1], *d[:4], reverse)
            o_ref[0, rows[0], sl] = o1.astype(o_ref.dtype)
            o_ref[0, rows[1], sl] = o2.astype(o_ref.dtype)
        return bad

    bad = lax.fori_loop(0, n_chunks // 2, pair, jnp.zeros((1, LANES), F32))

    @pl.when(jnp.max(bad) > 0.0)
    def _():
        st_scr[...] = save_scr[...]
        for dr, (q_ref, v_ref, z_ref, o_ref, reverse) in enumerate(dirs):
            _hgrn_exact_block(q_ref, v_ref, z_ref, lb_ref, ox_scr, st_scr, dr, reverse, n_chunks * c // SUBLANES)
            o_ref[0] = ox_scr[...].astype(o_ref.dtype)

    @pl.when(step == pl.num_programs(1) - 1)
    def _():
        sfin_ref[0] = st_scr[...]


def _hgrn_scan(proj, lb, s0):
    b, l, _ = proj.shape
    d = A_HEADS * LANES
    t = _row_tile(l, 512)
    nb = l // t
    fwd = lambda blk: pl.BlockSpec((1, t, d), lambda i, s: (i, s, blk))
    bwd = lambda blk: pl.BlockSpec((1, t, d), lambda i, s: (i, nb - 1 - s, blk))
    st_spec = pl.BlockSpec((1,) + s0.shape[1:], lambda i, s: (i, 0, 0, 0, 0))
    return pl.pallas_call(
        functools.partial(_hgrn_scan_body, n_chunks=t // HGRN_CHUNK),
        grid=(b, nb),
        in_specs=[fwd(0), fwd(1), fwd(2), bwd(0), bwd(1), bwd(3),
                  pl.BlockSpec((1, d), lambda i, s: (0, 0)), st_spec],
        out_specs=[fwd(0), bwd(0), st_spec],
        out_shape=[jax.ShapeDtypeStruct((b, l, d), BF16), jax.ShapeDtypeStruct((b, l, d), BF16),
                   jax.ShapeDtypeStruct(s0.shape, F32)],
        scratch_shapes=[pltpu.VMEM(s0.shape[1:], F32), pltpu.VMEM(s0.shape[1:], F32), pltpu.VMEM((t, d), F32)],
        compiler_params=_params("parallel", "arbitrary"),
        name="hgrn_scan",
    )(proj, proj, proj, proj, proj, proj, lb, s0)


def _hgrn2_mixer(x_c, x_l, ng, mod_c, mod_l, w_in, lower, out_g, w_out, g1, want_ctx):
    d = x_l.shape[-1]
    p_c, g_c = _norm_mod_proj(x_c, ng, mod_c[0], mod_c[1], w_in, F32, silu_cols=(2 * d, 3 * d))
    p_l, g_l = _norm_mod_proj(x_l, ng, mod_l[0], mod_l[1], w_in, F32, silu_cols=(2 * d, 3 * d))
    lb = lower.reshape(1, d)
    zero = jnp.zeros((x_l.shape[0], 2, A_HEADS, LANES, LANES), F32)
    ocf, ocb, s_ctx = _hgrn_scan(p_c, lb, zero)
    olf, olb, _ = _hgrn_scan(p_l, lb, s_ctx)
    o_c, o_l = (ocf, ocb), (olf, olb)
    og = out_g.reshape(1, d)
    new_l = _out_call(_out_hgrn_body, [(o_l[0], 0, d), (o_l[1], 0, d), (g_l, 0, d)], [og, w_out],
                      x_l, mod_l[2], g1, "hgrn_out")
    new_c = None
    if want_ctx:
        new_c = _out_call(_out_hgrn_body, [(o_c[0], 0, d), (o_c[1], 0, d), (g_c, 0, d)], [og, w_out],
                          x_c, mod_c[2], g1, "hgrn_out")
    return new_c, new_l


def _softmax_pv(s_list, v_list):
    mx = functools.reduce(jnp.maximum, [jnp.max(s, axis=-1, keepdims=True) for s in s_list])
    ps = [jnp.exp(s - mx) for s in s_list]
    den = functools.reduce(jnp.add, [jnp.sum(p, axis=-1, keepdims=True) for p in ps])
    acc = functools.reduce(jnp.add, [jnp.dot(p.astype(BF16), v, preferred_element_type=F32)
                                     for p, v in zip(ps, v_list)])
    return acc / den


def _na_lat_body(q_ref, k_ref, v_ref, kc_ref, vc_ref, t2_ref, o_ref, *, rows, kh, scale):
    w = GRID_W
    rb = NA_ROW_BLOCK
    nk = kh + rb
    n_rel = 2 * kh - 1
    first = lax.broadcasted_iota(jnp.int32, (rb * w, LANES), 1) < (LANES // 2)
    kc = kc_ref[0]
    vc = vc_ref[0]

    def bias_index(r, ks, m):
        r0 = jnp.clip(r - kh // 2, 0, rows - kh)
        ka = ks + 2 * m
        rel_a = ka - r + (kh - 1)
        in_a = (ka >= r0) & (ka < r0 + kh)
        in_b = (ka + 1 >= r0) & (ka + 1 < r0 + kh)
        both, only_b, only_a = rel_a, (n_rel - 1) + rel_a + 1, (2 * n_rel - 1) + rel_a
        return jnp.where(in_a, jnp.where(in_b, both, only_a), jnp.where(in_b, only_b, 3 * n_rel - 1))

    def block(i, carry):
        rq = i * rb
        ks = jnp.clip(rq - kh // 2, 0, rows - nk)
        q2 = q_ref[0, pl.ds(pl.multiple_of(rq * w, rb * w), rb * w), :] * scale
        k2 = k_ref[0, pl.ds(pl.multiple_of(ks * w, w), nk * w), :]
        v2 = v_ref[0, pl.ds(pl.multiple_of(ks * w, w), nk * w), :]
        outs = []
        for hh in range(2):
            qm = jnp.where(first if hh == 0 else jnp.logical_not(first), q2, jnp.zeros_like(q2))
            bias = jnp.concatenate(
                [jnp.concatenate([t2_ref[hh, bias_index(rq + a, ks, m)] for m in range(nk // 2)], axis=-1)
                 for a in range(rb)], axis=0)
            s_w = lax.dot_general(qm, k2, NT_DIMS, preferred_element_type=F32) + bias
            s_c = lax.dot_general(qm, kc, NT_DIMS, preferred_element_type=F32)
            outs.append(_softmax_pv([s_w, s_c], [v2, vc]))
        o_ref[0, pl.ds(pl.multiple_of(rq * w, rb * w), rb * w), :] = (
            jnp.where(first, outs[0], outs[1]).astype(o_ref.dtype))
        return carry

    lax.fori_loop(0, rows // rb, block, 0, unroll=4)


def _na_ctx_body(q_ref, k_ref, v_ref, o_ref, *, scale):
    n = q_ref.shape[1]
    first = lax.broadcasted_iota(jnp.int32, (n, LANES), 1) < (LANES // 2)
    q2 = q_ref[0] * scale
    k2 = k_ref[0]
    v2 = v_ref[0]
    outs = []
    for hh in range(2):
        qm = jnp.where(first if hh == 0 else jnp.logical_not(first), q2, jnp.zeros_like(q2))
        s = lax.dot_general(qm, k2, NT_DIMS, preferred_element_type=F32)
        outs.append(_softmax_pv([s], [v2]))
    o_ref[0] = jnp.where(first, outs[0], outs[1]).astype(o_ref.dtype)


def _na_bias_table(rpb, kh, kw):
    w = jnp.arange(GRID_W)[:, None]
    c = jnp.arange(GRID_W)[None, :]
    c0 = jnp.clip(w - kw // 2, 0, GRID_W - kw)
    inside = (c >= c0) & (c < c0 + kw)
    pad = GRID_W - kw
    padded = jnp.pad(rpb.astype(F32), ((0, 0), (0, 0), (pad, pad)))
    shifted = jnp.stack([padded[:, :, pad + kw - 1 - q:pad + kw - 1 - q + GRID_W] for q in range(GRID_W)], axis=2)
    t = jnp.where(inside[None, None], shifted, NEG_BIG)
    off = jnp.full_like(t, NEG_BIG)
    pair = lambda a, b: jnp.concatenate([a, b], axis=-1)
    return jnp.concatenate([pair(t[:, :-1], t[:, 1:]), pair(off, t), pair(t, off), pair(off, off)[:, :1]], axis=1)


def _na_mixer(x_c, x_l, ng, mod_c, mod_l, w_qkv, rpb, w_out, g1, want_ctx):
    b, l, d = x_l.shape
    lc = x_c.shape[1]
    heads = rpb.shape[0]
    na_rows, na_cols = (rpb.shape[1] + 1) // 2, (rpb.shape[2] + 1) // 2
    dh = d // heads
    assert 2 * dh == LANES and l % GRID_W == 0
    rows = l // GRID_W
    kh = min(na_rows, rows)
    assert kh == na_rows and (kh + NA_ROW_BLOCK) % 2 == 0
    assert rows % NA_ROW_BLOCK == 0 and rows >= kh + NA_ROW_BLOCK
    scale = dh ** -0.5
    nhp = heads // 2
    qkv_c = _norm_mod_proj(x_c, ng, mod_c[0], mod_c[1], w_qkv, BF16)
    qkv_l = _norm_mod_proj(x_l, ng, mod_l[0], mod_l[1], w_qkv, BF16)
    t2 = _na_bias_table(rpb, kh, na_cols)
    o_l = pl.pallas_call(
        functools.partial(_na_lat_body, rows=rows, kh=kh, scale=scale),
        grid=(b, nhp),
        in_specs=[pl.BlockSpec((1, l, LANES), lambda i, p: (i, 0, p)),
                  pl.BlockSpec((1, l, LANES), lambda i, p: (i, 0, nhp + p)),
                  pl.BlockSpec((1, l, LANES), lambda i, p: (i, 0, 2 * nhp + p)),
                  pl.BlockSpec((1, lc, LANES), lambda i, p: (i, 0, nhp + p)),
                  pl.BlockSpec((1, lc, LANES), lambda i, p: (i, 0, 2 * nhp + p)),
                  pl.BlockSpec((2,) + t2.shape[1:], lambda i, p: (p, 0, 0, 0))],
        out_specs=pl.BlockSpec((1, l, LANES), lambda i, p: (i, 0, p)),
        out_shape=jax.ShapeDtypeStruct((b, l, d), BF16),
        compiler_params=_params("parallel", "parallel"),
        name="na_latent",
    )(qkv_l, qkv_l, qkv_l, qkv_c, qkv_c, t2)
    new_l = _out_call(_out_plain_body, [(o_l, 0, d)], [w_out], x_l, mod_l[2], g1, "na_out")
    new_c = None
    if want_ctx:
        o_c = pl.pallas_call(
            functools.partial(_na_ctx_body, scale=scale),
            grid=(b, nhp),
            in_specs=[pl.BlockSpec((1, lc, LANES), lambda i, p: (i, 0, p)),
                      pl.BlockSpec((1, lc, LANES), lambda i, p: (i, 0, nhp + p)),
                      pl.BlockSpec((1, lc, LANES), lambda i, p: (i, 0, 2 * nhp + p))],
            out_specs=pl.BlockSpec((1, lc, LANES), lambda i, p: (i, 0, p)),
            out_shape=jax.ShapeDtypeStruct((b, lc, d), BF16),
            compiler_params=_params("parallel", "parallel"),
            name="na_context",
        )(qkv_c, qkv_c, qkv_c)
        new_c = _out_call(_out_plain_body, [(o_c, 0, d)], [w_out], x_c, mod_c[2], g1, "na_out")
    return new_c, new_l


S5_GROUP_BLOCK = 8


def _s5_kernel_body(c_ref, w_ref, o_ref):
    for i in range(c_ref.shape[0]):
        o_ref[i] = jnp.dot(c_ref[i], w_ref[i], preferred_element_type=F32, precision=lax.Precision.HIGHEST)


def _s5_impulse(cmat, wmat):
    n = cmat.shape[0]
    gb = S5_GROUP_BLOCK
    return pl.pallas_call(
        _s5_kernel_body,
        grid=(n // gb,),
        in_specs=[pl.BlockSpec((gb,) + cmat.shape[1:], lambda i: (i, 0, 0)),
                  pl.BlockSpec((gb,) + wmat.shape[1:], lambda i: (i, 0, 0))],
        out_specs=pl.BlockSpec((gb, cmat.shape[1], wmat.shape[2]), lambda i: (i, 0, 0)),
        out_shape=jax.ShapeDtypeStruct((n, cmat.shape[1], wmat.shape[2]), F32),
        compiler_params=_params("parallel"),
        name="s5_impulse",
    )(cmat, wmat)


def _s5_operators(lam_re, lam_im, log_dt, b_re, b_im, c_re, c_im, t):
    _, g, p = lam_re.shape
    cg = b_re.shape[-1]
    lam = lax.complex(lam_re.astype(F32), lam_im.astype(F32))
    ldt = lam * jnp.exp(log_dt.astype(F32))[..., None]
    a = jnp.exp(ldt)
    bbar = ((a - 1.0) / lam)[..., None] * lax.complex(b_re.astype(F32), b_im.astype(F32))
    cm = lax.complex(c_re.astype(F32), c_im.astype(F32))
    apow = jnp.exp(ldt[..., None] * jnp.arange(t + 1, dtype=F32))
    w = apow[..., :t, None] * bbar[:, :, :, None, :]
    wmat = jnp.concatenate([jnp.real(w), jnp.imag(w)], axis=2).reshape(2 * g, 2 * p, t * cg)
    cmat = jnp.concatenate([jnp.real(cm), -jnp.imag(cm)], axis=-1).reshape(2 * g, cg, 2 * p)
    k = _s5_impulse(cmat, wmat).reshape(2, g, cg, t, cg)
    k = jnp.transpose(k, (0, 1, 3, 4, 2))
    kfull = jnp.concatenate([k[1, :, :0:-1], (k[0, :, :1] + k[1, :, :1]), k[0, :, 1:]], axis=1)
    lagvec = jnp.transpose(kfull, (0, 2, 1, 3)).reshape(g, cg, (2 * t - 1) * cg).astype(BF16)
    toep = jnp.stack([lagvec[:, :, (t - 1 - s) * cg:(2 * t - 1 - s) * cg] for s in range(t)], axis=1)
    toep = toep.reshape(g, t * cg, t * cg)

    def state_in(wd, flip):
        wd = wd[:, :, ::-1] if flip else wd
        m = jnp.transpose(wd, (0, 2, 3, 1)).reshape(g, t * cg, p)
        return jnp.concatenate([jnp.real(m), jnp.imag(m)], axis=-1)

    def state_out(cd, pw):
        n = cd[:, :, :, None] * pw[:, None, :, :]
        n = jnp.transpose(n, (0, 2, 3, 1)).reshape(g, p, t * cg)
        return jnp.concatenate([jnp.real(n), -jnp.imag(n)], axis=1)

    m_f = state_in(w[0], True)
    m_b = state_in(w[1], False)
    n_f = state_out(cm[0], apow[0][..., 1:])
    n_b = state_out(cm[1], apow[1][..., :0:-1])
    at = apow[..., t]
    a1 = jnp.concatenate([jnp.real(at), jnp.real(at)], axis=-1)
    a2 = jnp.concatenate([-jnp.imag(at), jnp.imag(at)], axis=-1)
    bf = lambda m: m.astype(BF16)
    return dict(toep=bf(toep), m_f=bf(m_f), m_b=bf(m_b), n_f=bf(n_f), n_b=bf(n_b), a1=a1, a2=a2)


def _s5_body(u_ref, dsk_ref, toep_ref, mf_ref, mb_ref, nf_ref, nb_ref, a1_ref, a2_ref, x0_ref,
             y_ref, xfin_ref, z_scr, zs_scr, xin_scr):
    gb, nc = u_ref.shape[1], u_ref.shape[2]
    half = LANES // 2
    ubs = [u_ref[0, gi].astype(BF16) for gi in range(gb)]
    for dr, m_ref in enumerate((mf_ref, mb_ref)):
        z = jnp.stack([jnp.dot(ubs[gi], m_ref[gi], preferred_element_type=F32) for gi in range(gb)])
        z_scr[dr] = jnp.swapaxes(z, 0, 1)

    swap = lambda t: pltpu.roll(t, half, t.ndim - 1)
    for dr in range(2):
        zs_scr[dr] = swap(z_scr[dr].reshape(nc * gb, LANES)).reshape(nc, gb, LANES)

    coef = [(a1_ref[dr], a2_ref[dr]) for dr in range(2)]

    def scan(j, xs):
        out = []
        for dr in range(2):
            jj = j if dr == 0 else nc - 1 - j
            x, xw = xs[2 * dr], xs[2 * dr + 1]
            a1, a2 = coef[dr]
            xin_scr[dr, jj] = x
            out.append(a1 * x + a2 * xw + z_scr[dr, jj])
            out.append(a1 * xw - a2 * x + zs_scr[dr, jj])
        return tuple(out)

    x0f, x0b = x0_ref[0, 0], x0_ref[0, 1]
    xf, _, xb, _ = lax.fori_loop(0, nc, scan, (x0f, swap(x0f), x0b, swap(x0b)), unroll=4)
    xfin_ref[0, 0] = xf
    xfin_ref[0, 1] = xb
    xin = [jnp.swapaxes(xin_scr[dr], 0, 1).astype(BF16) for dr in range(2)]
    for gi in range(gb):
        u = u_ref[0, gi]
        y = jnp.dot(ubs[gi], toep_ref[gi], preferred_element_type=F32) + u * dsk_ref[gi]
        y = y + jnp.dot(xin[0][gi], nf_ref[gi], preferred_element_type=F32)
        y = y + jnp.dot(xin[1][gi], nb_ref[gi], preferred_element_type=F32)
        y_ref[0, gi] = y


def _piece_transpose(src, piece_of, cg):
    n = len(src)
    out = [None] * n
    for s in range(n):
        c = src[s % n]
        for p in range(1, n):
            c = jnp.where(piece_of == p, src[(p + s) % n], c)
        r = c if s == 0 else pltpu.roll(c, s * cg, 1)
        for p in range(n):
            out[p] = r if s == 0 else jnp.where(piece_of == (p + s) % n, r, out[p])
    return out


def _to_chunks(h_scr, o_ref, t, cg):
    ncb = h_scr.shape[1] // t
    pieces = LANES // cg
    piece_of = lax.broadcasted_iota(jnp.int32, (ncb, LANES), 1) // cg

    def lane_block(lb, carry):
        for tq in range(t // pieces):
            steps = [h_scr[lb, pl.ds(tq * pieces + tr, ncb, stride=t), :] for tr in range(pieces)]
            for gl, col in enumerate(_piece_transpose(steps, piece_of, cg)):
                o_ref[0, lb * pieces + gl, :, tq * LANES:(tq + 1) * LANES] = col
        return carry

    lax.fori_loop(0, h_scr.shape[0], lane_block, 0)


def _from_chunks(y_ref, y_scr, t, cg):
    ncb = y_ref.shape[2]
    pieces = LANES // cg
    piece_of = lax.broadcasted_iota(jnp.int32, (ncb, LANES), 1) // cg

    def lane_block(lb, carry):
        for tq in range(t // pieces):
            cols = [y_ref[0, lb * pieces + gl, :, tq * LANES:(tq + 1) * LANES] for gl in range(pieces)]
            for tr, step in enumerate(_piece_transpose(cols, piece_of, cg)):
                y_scr[lb, pl.ds(tq * pieces + tr, ncb, stride=t), :] = step
        return carry

    lax.fori_loop(0, y_scr.shape[0], lane_block, 0)


def _norm_chunks_body(x_ref, g_ref, sh_ref, sc_ref, o_ref, h_scr, *, t, cg):
    h = _norm_mod(x_ref[...], g_ref[...], sh_ref[0], sc_ref[0])
    for lb in range(h_scr.shape[0]):
        h_scr[lb] = h[:, lb * LANES:(lb + 1) * LANES]
    _to_chunks(h_scr, o_ref, t, cg)


def _norm_mod_chunks(x, g, sh, sc, groups, t):
    b, l, d = x.shape
    cg = d // groups
    tm = _row_tile(l, 512)
    per = l // tm
    assert tm % t == 0 and (tm // t) % SUBLANES == 0 and LANES % cg == 0 and t % (LANES // cg) == 0
    return pl.pallas_call(
        functools.partial(_norm_chunks_body, t=t, cg=cg),
        grid=(b * per,),
        in_specs=[pl.BlockSpec((tm, d), lambda i: (i, 0)),
                  pl.BlockSpec((1, d), lambda i: (0, 0)),
                  _mod_spec(sh, per), _mod_spec(sc, per)],
        out_specs=pl.BlockSpec((1, groups, tm // t, t * cg), lambda i: (i // per, 0, i % per, 0)),
        out_shape=jax.ShapeDtypeStruct((b, groups, l // t, t * cg), F32),
        scratch_shapes=[pltpu.VMEM((d // LANES, tm, LANES), F32)],
        compiler_params=_params("parallel"),
        name="norm_mod_chunks",
    )(x.reshape(b * l, d), g.reshape(1, d), sh.table, sc.table)


def _glu_chunks_body(y_ref, w_ref, res_ref, gate_ref, g1_ref, o_ref, y_scr, *, t, cg):
    _from_chunks(y_ref, y_scr, t, cg)
    d = o_ref.shape[-1]
    y = jnp.concatenate([y_scr[lb] for lb in range(y_scr.shape[0])], axis=-1)
    a = _gelu_tanh(y).astype(BF16)
    ag = jnp.dot(a, w_ref[...], preferred_element_type=F32)
    o_ref[...] = _residual(res_ref, gate_ref, g1_ref, ag[:, :d] * jax.nn.sigmoid(ag[:, d:]))


def _glu_out_chunks(y, w_glu, res, gate, g1, t):
    b, l, d = res.shape
    groups = y.shape[1]
    cg = d // groups
    tm = _row_tile(l, 512)
    per = l // tm
    out = pl.pallas_call(
        functools.partial(_glu_chunks_body, t=t, cg=cg),
        grid=(b * per,),
        in_specs=[pl.BlockSpec((1, groups, tm // t, t * cg), lambda i: (i // per, 0, i % per, 0)),
                  _layer_spec(w_glu, w_glu.shape, lambda i: (0, 0)),
                  pl.BlockSpec((tm, d), lambda i: (i, 0)),
                  _mod_spec(gate, per),
                  pl.BlockSpec((1, d), lambda i: (0, 0))],
        out_specs=pl.BlockSpec((tm, d), lambda i: (i, 0)),
        out_shape=jax.ShapeDtypeStruct((b * l, d), F32),
        scratch_shapes=[pltpu.VMEM((d // LANES, tm, LANES), F32)],
        compiler_params=_params("parallel"),
        name="s5_glu_out",
    )(y, w_glu.stack, res.reshape(b * l, d), gate.table, g1.reshape(1, d))
    return out.reshape(b, l, d)


def _s5_scan(u, ops, dsk, x0):
    b, g, nc, tc = u.shape
    gb = S5_GROUP_BLOCK
    per_g = lambda shape: pl.BlockSpec((gb,) + shape, lambda j, i: (j,) + (0,) * len(shape))
    y, xfin = pl.pallas_call(
        _s5_body,
        grid=(g // gb, b),
        in_specs=[pl.BlockSpec((1, gb, nc, tc), lambda j, i: (i, j, 0, 0)),
                  per_g((1, tc)), per_g((tc, tc)), per_g((tc, LANES)), per_g((tc, LANES)),
                  per_g((LANES, tc)), per_g((LANES, tc)),
                  pl.BlockSpec((2, gb, LANES), lambda j, i: (0, j, 0)),
                  pl.BlockSpec((2, gb, LANES), lambda j, i: (0, j, 0)),
                  pl.BlockSpec((1, 2, gb, LANES), lambda j, i: (i, 0, j, 0))],
        out_specs=[pl.BlockSpec((1, gb, nc, tc), lambda j, i: (i, j, 0, 0)),
                   pl.BlockSpec((1, 2, gb, LANES), lambda j, i: (i, 0, j, 0))],
        out_shape=[jax.ShapeDtypeStruct((b, g, nc, tc), F32), jax.ShapeDtypeStruct((b, 2, g, LANES), F32)],
        scratch_shapes=[pltpu.VMEM((2, nc, gb, LANES), F32)] * 3,
        compiler_params=_params("parallel", "parallel"),
        name="s5_scan",
    )(u, dsk, ops["toep"], ops["m_f"], ops["m_b"], ops["n_f"], ops["n_b"], ops["a1"], ops["a2"], x0)
    return y, xfin


def _s5_mixer(x_c, x_l, ng, mod_c, mod_l, lam_re, lam_im, log_dt, b_re, b_im, c_re, c_im, d_skip, w_glu, g1,
              want_ctx):
    b, l, d = x_l.shape
    g, p = lam_re.shape[1], lam_re.shape[2]
    cg = d // g
    assert 2 * p == LANES and g % S5_GROUP_BLOCK == 0
    ops = _s5_operators(lam_re, lam_im, log_dt, b_re, b_im, c_re, c_im, S5_CHUNK)
    dsk = jnp.tile(d_skip.astype(F32).reshape(g, 1, cg), (1, 1, S5_CHUNK))
    u_c = _norm_mod_chunks(x_c, ng, mod_c[0], mod_c[1], g, S5_CHUNK)
    u_l = _norm_mod_chunks(x_l, ng, mod_l[0], mod_l[1], g, S5_CHUNK)
    y_c, x_ctx = _s5_scan(u_c, ops, dsk, jnp.zeros((b, 2, g, LANES), F32))
    y_l, _ = _s5_scan(u_l, ops, dsk, x_ctx)
    new_l = _glu_out_chunks(y_l, w_glu, x_l, mod_l[2], g1, S5_CHUNK)
    new_c = _glu_out_chunks(y_c, w_glu, x_c, mod_c[2], g1, S5_CHUNK) if want_ctx else None
    return new_c, new_l


FFN_HALO = 16
FFN_TN = 256
FFN_TILES_PER_STEP = 11
FFN_ROW_SPLIT = 256


def _ffn_body(xp_ref, x_ref, xn_ref, g2_ref, sh_ref, sc_ref, *rest, per, n_tiles, tps):
    w_refs = rest[:2 * tps]
    cw_ref, cb_ref, wo_ref, gate_ref, g3_ref, o_ref, h_scr, gat_scr = rest[2 * tps:]
    i, j = pl.program_id(0), pl.program_id(1)
    tm = x_ref.shape[0]
    hl = FFN_HALO
    f = wo_ref.shape[0]
    n_steps = pl.cdiv(n_tiles, tps)
    when = (lambda cond: lambda fn: fn()) if n_steps == 1 else pl.when

    @when(j == 0)
    def _():
        g, sh, sc = g2_ref[...], sh_ref[0], sc_ref[0]
        keep_p = ((i % per) != 0).astype(F32)
        keep_n = ((i % per) != per - 1).astype(F32)
        h_scr[0:hl] = (_norm_mod(xp_ref[...], g, sh, sc) * keep_p).astype(BF16)
        h_scr[hl:hl + tm] = _norm_mod(x_ref[...], g, sh, sc).astype(BF16)
        h_scr[hl + tm:] = (_norm_mod(xn_ref[...], g, sh, sc) * keep_n).astype(BF16)

    sub = FFN_ROW_SPLIT if tm % FFN_ROW_SPLIT == 0 else tm
    ns = tm // sub
    edge = SUBLANES
    tn = w_refs[0].shape[1]

    def pieces(w_ref):
        out = []
        for s in range(ns):
            lo = hl + s * sub - (hl if s == 0 else 0)
            hi = hl + (s + 1) * sub + (hl if s == ns - 1 else 0)
            out.append(jnp.dot(h_scr[lo:hi], w_ref[...], preferred_element_type=F32))
        return out

    def conv(p, s, cw, cb):
        base = hl if s == 0 else 0
        left = p[s][hl - edge:hl] if s == 0 else p[s - 1][-edge:]
        right = p[s][base + sub:base + sub + edge] if s == ns - 1 else p[s + 1][:edge]
        ext = jnp.concatenate([left, p[s][base:base + sub], right], axis=0)
        um = pltpu.roll(ext, 1, 0)[edge:edge + sub]
        up = pltpu.roll(ext, sub + 2 * edge - 1, 0)[edge:edge + sub]
        return cb + um * cw[0:1] + ext[edge:edge + sub] * cw[1:2] + up * cw[2:3]

    def tile(k):
        col = k * tn if n_steps == 1 else pl.multiple_of((j * tps + k) * tn, tn)
        cols_a, cols_v = pl.ds(col, tn), pl.ds(pl.multiple_of(f + col, tn), tn)
        pa, pv = pieces(w_refs[2 * k]), pieces(w_refs[2 * k + 1])
        for s in range(ns):
            a = conv(pa, s, cw_ref[:, cols_a], cb_ref[:, cols_a])
            v = conv(pv, s, cw_ref[:, cols_v], cb_ref[:, cols_v])
            gat_scr[s * sub:(s + 1) * sub, cols_a] = (a * jax.nn.sigmoid(a) * v).astype(BF16)

    always = n_tiles - tps * (n_steps - 1)
    for k in range(tps):
        if k < always:
            tile(k)
        else:
            pl.when(j * tps + k < n_tiles)(functools.partial(tile, k))

    @when(j == n_steps - 1)
    def _():
        y = jnp.dot(gat_scr[...], wo_ref[...], preferred_element_type=F32)
        o_ref[...] = x_ref[...] + gate_ref[0] * _rms(y, g3_ref[...])


def _conv_ffn_block(x, ng2, mod, w_in, conv_w, conv_b, w_out, ng3):
    b, l, d = x.shape
    f = w_out.shape[0]
    tn, hl = FFN_TN, FFN_HALO
    assert f % tn == 0 and conv_w.shape[0] == 3 and w_in.shape[1] == 2 * f
    nj = f // tn
    tm = _row_tile(l, 1024)
    per = l // tm
    hb = tm // hl
    last_hb = b * l // hl - 1
    cb = conv_b.reshape(1, 2 * f)
    x2 = x.reshape(b * l, d)
    tps = min(FFN_TILES_PER_STEP, nj)
    once = dict(pipeline_mode=pl.Buffered(1)) if tps == nj else {}
    w_specs = []
    for k in range(tps):
        w_specs.append(_layer_spec(w_in, (d, tn), lambda i, j, k=k: (0, jnp.minimum(j * tps + k, nj - 1)), **once))
        w_specs.append(_layer_spec(w_in, (d, tn), lambda i, j, k=k: (0, nj + jnp.minimum(j * tps + k, nj - 1)),
                                   **once))
    whole = lambda arr: pl.BlockSpec(arr.shape, lambda i, j: (0,) * arr.ndim)
    out = pl.pallas_call(
        functools.partial(_ffn_body, per=per, n_tiles=nj, tps=tps),
        grid=(b * per, pl.cdiv(nj, tps)),
        in_specs=[pl.BlockSpec((hl, d), lambda i, j: (jnp.maximum(i * hb - 1, 0), 0)),
                  pl.BlockSpec((tm, d), lambda i, j: (i, 0)),
                  pl.BlockSpec((hl, d), lambda i, j: (jnp.minimum((i + 1) * hb, last_hb), 0)),
                  pl.BlockSpec((1, d), lambda i, j: (0, 0)),
                  _mod_spec(mod[3], per), _mod_spec(mod[4], per),
                  *w_specs, whole(conv_w), whole(cb),
                  _layer_spec(w_out, (f, d), lambda i, j: (0, 0), pipeline_mode=pl.Buffered(1)),
                  _mod_spec(mod[5], per),
                  pl.BlockSpec((1, d), lambda i, j: (0, 0))],
        out_specs=pl.BlockSpec((tm, d), lambda i, j: (i, 0)),
        out_shape=jax.ShapeDtypeStruct((b * l, d), F32),
        scratch_shapes=[pltpu.VMEM((tm + 2 * hl, d), BF16), pltpu.VMEM((tm, f), BF16)],
        compiler_params=_params("parallel", "arbitrary"),
        name="conv_ffn",
    )(x2, x2, x2, ng2.reshape(1, d), mod[3].table, mod[4].table, *([w_in.stack] * (2 * tps)), conv_w, cb,
      w_out.stack,
      mod[5].table, ng3.reshape(1, d))
    return out.reshape(b, l, d)


def kernel(x, c, ctx, c_ctx, w_mod, b_mod, norm_g, a_w_in, a_lower_logits, a_out_g, a_w_out, b_w_qkv, b_rpb,
           b_w_out, c_lam_re, c_lam_im, c_log_dt, c_b_re, c_b_im, c_c_re, c_c_im, c_d, c_w_glu, f_w_in,
           f_conv_w, f_conv_b, f_w_out):
    bsz, _, d = x.shape
    depth = w_mod.shape[0]
    p = jax.nn.softmax(a_lower_logits.astype(F32), axis=0)
    lower = jnp.cumsum(p, axis=0) - p[0]
    n_rows = -(-(bsz + 1) // SUBLANES) * SUBLANES
    c_rows = jnp.zeros((n_rows, d), F32).at[:bsz].set(c).at[bsz].set(c_ctx)
    table = _modulation(c_rows, w_mod, b_mod).reshape(depth * n_rows * N_MOD, 1, d)
    wa_in, wa_out, wb_qkv, wb_out, wc_glu, wf_in, wf_out = (
        w.astype(BF16) for w in (a_w_in, a_w_out, b_w_qkv, b_w_out, c_w_glu, f_w_in, f_w_out))
    lat, cx = x, ctx
    for i in range(depth):
        kind, j = i % N_MIXERS, i // N_MIXERS
        last = i == depth - 1
        mod_l = [ModRow(table, i * n_rows * N_MOD + k, N_MOD) for k in range(N_MOD)]
        mod_c = [ModRow(table, (i * n_rows + bsz) * N_MOD + k, 0) for k in range(N_MOD)]
        ng = norm_g[i]
        if kind == 0:
            cx1, lat = _hgrn2_mixer(cx, lat, ng[0], mod_c, mod_l, LayerOf(wa_in, j), lower[i], a_out_g[j],
                                    LayerOf(wa_out, j), ng[1], not last)
        elif kind == 1:
            cx1, lat = _na_mixer(cx, lat, ng[0], mod_c, mod_l, LayerOf(wb_qkv, j), b_rpb[j], LayerOf(wb_out, j),
                                 ng[1], not last)
        else:
            cx1, lat = _s5_mixer(cx, lat, ng[0], mod_c, mod_l, c_lam_re[j], c_lam_im[j], c_log_dt[j], c_b_re[j],
                                 c_b_im[j], c_c_re[j], c_c_im[j], c_d[j], LayerOf(wc_glu, j), ng[1], not last)
        ffn = (LayerOf(wf_in, i), f_conv_w[i], f_conv_b[i], LayerOf(wf_out, i), ng[3])
        lat = _conv_ffn_block(lat, ng[2], mod_l, *ffn)
        if not last:
            cx = _conv_ffn_block(cx1, ng[2], mod_c, *ffn)
    return lat
```

```python
import functools
from typing import NamedTuple

import jax
import jax.numpy as jnp
from jax import lax
from jax.experimental import pallas as pl
from jax.experimental.pallas import tpu as pltpu

F32 = jnp.float32
BF16 = jnp.bfloat16

EPS = 1e-6
F_MIN = 1e-30
N_MOD = 6
N_MIXERS = 3
A_HEADS = 8
GRID_W = 64
S5_CHUNK = 16
NEG_BIG = -1e30
NA_ROW_BLOCK = 4
SUBLANES = 8
LANES = 128
VMEM_LIMIT_BYTES = 56 * 1024 * 1024

NT_DIMS = (((1,), (1,)), ((), ()))
TN_DIMS = (((0,), (0,)), ((), ()))


def _params(*sem):
    return pltpu.CompilerParams(dimension_semantics=sem, vmem_limit_bytes=VMEM_LIMIT_BYTES)


def _row_tile(n, want):
    t = min(n, want)
    assert n % t == 0, (n, t)
    return t


def _rms(y, g):
    return y * lax.rsqrt(jnp.mean(y * y, axis=-1, keepdims=True) + EPS) * g


def _mod_body(c_ref, w_ref, b_ref, o_ref):
    c = c_ref[...]
    s = (c * jax.nn.sigmoid(c)).astype(BF16)
    o_ref[0] = jnp.dot(s, w_ref[0].astype(BF16), preferred_element_type=F32) + b_ref[0]


def _modulation(c_rows, w_mod, b_mod):
    depth, d, n = w_mod.shape
    r = c_rows.shape[0]
    tn = n // 4
    return pl.pallas_call(
        _mod_body,
        grid=(depth, n // tn),
        in_specs=[pl.BlockSpec((r, d), lambda i, j: (0, 0)),
                  pl.BlockSpec((1, d, tn), lambda i, j: (i, 0, j)),
                  pl.BlockSpec((1, 1, tn), lambda i, j: (i, 0, j))],
        out_specs=pl.BlockSpec((1, r, tn), lambda i, j: (i, 0, j)),
        out_shape=jax.ShapeDtypeStruct((depth, r, n), F32),
        compiler_params=_params("parallel", "parallel"),
        name="adaln_mod",
    )(c_rows, w_mod, b_mod.reshape(depth, 1, n))


def _norm_mod(x, g, sh, sc):
    return _rms(x, g) * (1.0 + sc) + sh


class ModRow(NamedTuple):
    table: jax.Array
    first: int
    step: int


def _mod_spec(m, per):
    return pl.BlockSpec((1, 1, m.table.shape[-1]), lambda *idx: (m.first + m.step * (idx[0] // per), 0, 0))


class LayerOf(NamedTuple):
    stack: jax.Array
    layer: int

    @property
    def shape(self):
        return self.stack.shape[1:]


def _layer_spec(w, block, index_map, **kw):
    return pl.BlockSpec((None,) + tuple(block), lambda *idx: (w.layer,) + tuple(index_map(*idx)), **kw)


def _proj_body(x_ref, g_ref, sh_ref, sc_ref, w_ref, o_ref):
    h = _norm_mod(x_ref[...], g_ref[...], sh_ref[0], sc_ref[0]).astype(BF16)
    o_ref[...] = jnp.dot(h, w_ref[...], preferred_element_type=F32).astype(o_ref.dtype)


def _proj_silu_body(x_ref, g_ref, sh_ref, sc_ref, w_ref, o_ref, os_ref, *, lo, hi):
    h = _norm_mod(x_ref[...], g_ref[...], sh_ref[0], sc_ref[0]).astype(BF16)
    y = jnp.dot(h, w_ref[...], preferred_element_type=F32)
    o_ref[:, :lo] = y[:, :lo]
    o_ref[:, lo:] = y[:, hi:]
    gp = y[:, lo:hi]
    os_ref[...] = (gp * jax.nn.sigmoid(gp)).astype(os_ref.dtype)


def _norm_mod_proj(x, g, sh, sc, w, out_dtype, silu_cols=None):
    b, l, d = x.shape
    n = w.shape[1]
    tm = _row_tile(l, 512)
    per = l // tm
    body, widths, dtypes = _proj_body, [n], [out_dtype]
    if silu_cols is not None:
        lo, hi = silu_cols
        body, widths, dtypes = functools.partial(_proj_silu_body, lo=lo, hi=hi), [n - (hi - lo), hi - lo], [out_dtype, BF16]
    outs = pl.pallas_call(
        body,
        grid=(b * per,),
        in_specs=[pl.BlockSpec((tm, d), lambda i: (i, 0)),
                  pl.BlockSpec((1, d), lambda i: (0, 0)),
                  _mod_spec(sh, per), _mod_spec(sc, per),
                  _layer_spec(w, (d, n), lambda i: (0, 0), pipeline_mode=pl.Buffered(1))],
        out_specs=[pl.BlockSpec((tm, wd), lambda i: (i, 0)) for wd in widths],
        out_shape=[jax.ShapeDtypeStruct((b * l, wd), dt) for wd, dt in zip(widths, dtypes)],
        compiler_params=_params("parallel"),
        name="norm_mod_proj",
    )(x.reshape(b * l, d), g.reshape(1, d), sh.table, sc.table, w.stack)
    outs = [o.reshape(b, l, o.shape[-1]) for o in outs]
    return outs[0] if silu_cols is None else tuple(outs)


def _residual(res_ref, gate_ref, g1_ref, y):
    return res_ref[...] + gate_ref[0] * _rms(y, g1_ref[...])


def _out_plain_body(a_ref, w_ref, res_ref, gate_ref, g1_ref, o_ref):
    y = jnp.dot(a_ref[...], w_ref[...], preferred_element_type=F32)
    o_ref[...] = _residual(res_ref, gate_ref, g1_ref, y)


def _out_hgrn_body(of_ref, ob_ref, gp_ref, og_ref, w_ref, res_ref, gate_ref, g1_ref, o_ref):
    o = of_ref[...].astype(F32) + ob_ref[...].astype(F32)
    parts = []
    for h in range(A_HEADS):
        oh = o[:, h * LANES:(h + 1) * LANES]
        parts.append(oh * lax.rsqrt(jnp.mean(oh * oh, axis=-1, keepdims=True) + EPS))
    a = jnp.concatenate(parts, axis=-1) * og_ref[...] * gp_ref[...].astype(F32)
    y = jnp.dot(a.astype(BF16), w_ref[...], preferred_element_type=F32)
    o_ref[...] = _residual(res_ref, gate_ref, g1_ref, y)


def _gelu_tanh(y):
    return 0.5 * y * (1.0 + jnp.tanh(0.7978845608028654 * (y + 0.044715 * (y * y * y))))


def _out_call(body, row_inputs, const_inputs, res, gate, g1, name):
    b, l, d = res.shape
    tm = _row_tile(l, 512)
    per = l // tm
    in_specs, args = [], []
    for arr, blk, width in row_inputs:
        in_specs.append(pl.BlockSpec((tm, width), lambda i, blk=blk: (i, blk)))
        args.append(arr.reshape(b * l, arr.shape[-1]))
    for arr in const_inputs:
        if isinstance(arr, LayerOf):
            in_specs.append(_layer_spec(arr, arr.shape, lambda i, nd=len(arr.shape): (0,) * nd))
            args.append(arr.stack)
        else:
            in_specs.append(pl.BlockSpec(arr.shape, lambda i, nd=arr.ndim: (0,) * nd))
            args.append(arr)
    in_specs += [pl.BlockSpec((tm, d), lambda i: (i, 0)),
                 _mod_spec(gate, per),
                 pl.BlockSpec((1, d), lambda i: (0, 0))]
    args += [res.reshape(b * l, d), gate.table, g1.reshape(1, d)]
    out = pl.pallas_call(
        body,
        grid=(b * per,),
        in_specs=in_specs,
        out_specs=pl.BlockSpec((tm, d), lambda i: (i, 0)),
        out_shape=jax.ShapeDtypeStruct((b * l, d), F32),
        compiler_params=_params("parallel"),
        name=name,
    )(*args)
    return out.reshape(b, l, d)


HGRN_CHUNK = 32
HGRN_SAFE_MIN = 1e-30
HGRN_Q_HEADROOM = 1e37


def _hgrn_gates(z, lb):
    f = lb + (1.0 - lb) * jax.nn.sigmoid(z)
    return jnp.maximum(f, F_MIN), 1.0 - f


def _hgrn_chunk_prep(q, v, z, lb, reverse):
    c = q.shape[0]
    half = c // 2
    fm, kk = _hgrn_gates(z, lb)
    pos = lax.broadcasted_iota(jnp.int32, (c, LANES), 0)
    if reverse:
        pos = (c - 1) - pos
    second = pos >= half
    hpos = jnp.where(second, pos - half, pos)
    ph = fm
    j = 1
    while j < half:
        ph = ph * jnp.where(hpos >= j, pltpu.roll(ph, (c - j) if reverse else j, 0), 1.0)
        j *= 2
    a_last = ph[half:half + 1] if reverse else ph[half - 1:half]
    r_last = ph[0:1] if reverse else ph[c - 1:c]
    p_mid = a_last
    p_last = a_last * r_last
    pp = jnp.where(second, ph * a_last, ph)
    r = jnp.where(second, ph, ph * (1.0 / a_last))
    kd = kk * (1.0 / r)
    q_max = jnp.max(jnp.abs(q), axis=0, keepdims=True)
    ok = jnp.where(p_mid >= HGRN_SAFE_MIN, r_last, 0.0) >= HGRN_SAFE_MIN
    ok = jnp.where(ok, p_mid * HGRN_Q_HEADROOM, -1.0) >= q_max
    return dict(qr=(q * r).astype(BF16), kd=kd.astype(BF16), qp=q * pp, kl=kd * r_last, vb=v.astype(BF16),
                dec=p_last, bad=jnp.where(ok, 0.0, 1.0))


def _hgrn_pair_dots(p1, p2, st):
    bf = lambda t: t.astype(BF16)
    sc1 = lax.dot_general(p1["qr"], p1["kd"], NT_DIMS, preferred_element_type=F32)
    sc2 = lax.dot_general(p2["qr"], p2["kd"], NT_DIMS, preferred_element_type=F32)
    cross = lax.dot_general(bf(p2["qp"]), bf(p1["kl"]), NT_DIMS, preferred_element_type=F32)
    q_all = jnp.concatenate([bf(p1["qp"]), bf(p2["qp"] * p1["dec"])], axis=0)
    o_st = lax.dot_general(q_all, st.astype(BF16), NT_DIMS, preferred_element_type=F32)
    k_all = jnp.concatenate([bf(p1["kl"] * p2["dec"]), bf(p2["kl"])], axis=0)
    u = lax.dot_general(jnp.concatenate([p1["vb"], p2["vb"]], axis=0), k_all, TN_DIMS, preferred_element_type=F32)
    return sc1, sc2, cross, o_st, u


def _hgrn_pair_out(p1, p2, sc1, sc2, cross, o_st, reverse):
    c = sc1.shape[0]
    row = lax.broadcasted_iota(jnp.int32, (c, c), 0)
    col = lax.broadcasted_iota(jnp.int32, (c, c), 1)
    keep = (col >= row) if reverse else (col <= row)
    tri = lambda sc: jnp.where(keep, sc, 0.0).astype(BF16)
    o1 = jnp.dot(tri(sc1), p1["vb"], preferred_element_type=F32) + o_st[:c]
    o2 = (jnp.dot(tri(sc2), p2["vb"], preferred_element_type=F32)
          + jnp.dot(cross.astype(BF16), p1["vb"], preferred_element_type=F32) + o_st[c:])
    return o1, o2


def _hgrn_exact_block(q_ref, v_ref, z_ref, lb_ref, o_scr, st_scr, dr, reverse, n_tiles):
    pos = lax.broadcasted_iota(jnp.int32, (SUBLANES, LANES), 0)
    if reverse:
        pos = (SUBLANES - 1) - pos

    def earlier(x, j):
        if j == 0:
            return x
        return pltpu.roll(x, (SUBLANES - j) if reverse else j, 0)

    def later(x, j):
        return pltpu.roll(x, j if reverse else (SUBLANES - j), 0)

    def tile(i, carry):
        ti = (n_tiles - 1 - i) if reverse else i
        r0 = pl.multiple_of(ti * SUBLANES, SUBLANES)
        for h in range(A_HEADS):
            sl = slice(h * LANES, (h + 1) * LANES)
            lb = lb_ref[:, sl]
            z = z_ref[0, pl.ds(r0, SUBLANES), sl]
            q = q_ref[0, pl.ds(r0, SUBLANES), sl]
            v = v_ref[0, pl.ds(r0, SUBLANES), sl]
            fm, kk = _hgrn_gates(z, lb)
            pp = fm
            for j in (1, 2, 4):
                pp = pp * jnp.where(pos >= j, earlier(pp, j), 1.0)
            qq = jnp.where(pos <= SUBLANES - 2, later(fm, 1), 1.0)
            for j in (1, 2, 4):
                qq = qq * jnp.where(pos <= SUBLANES - 1 - j, later(qq, j), 1.0)
            dec = pp[0:1] if reverse else pp[SUBLANES - 1:SUBLANES]
            st = st_scr[dr, h]
            o = lax.dot_general((q * pp).astype(BF16), st.astype(BF16), NT_DIMS,
                                preferred_element_type=F32)
            g = fm
            for d in range(SUBLANES):
                if d == 0:
                    e = q * kk
                else:
                    if d > 1:
                        g = g * earlier(fm, d - 1)
                    e = q * g * jnp.where(pos >= d, earlier(kk, d), 0.0)
                o = o + jnp.sum(e, axis=-1, keepdims=True) * earlier(v, d)
            o_scr[pl.ds(r0, SUBLANES), sl] = o
            u = lax.dot_general(v.astype(BF16), (kk * qq).astype(BF16), TN_DIMS,
                                preferred_element_type=F32)
            st_scr[dr, h] = dec * st + u
        return carry

    lax.fori_loop(0, n_tiles, tile, 0)


def _hgrn_scan_body(qf_ref, vf_ref, zf_ref, qb_ref, vb_ref, zb_ref, lb_ref, s0_ref, of_ref, ob_ref, sfin_ref,
                    st_scr, save_scr, ox_scr, *, n_chunks):
    step = pl.program_id(1)

    @pl.when(step == 0)
    def _():
        st_scr[...] = s0_ref[0]

    save_scr[...] = st_scr[...]
    dirs = ((qf_ref, vf_ref, zf_ref, of_ref, False), (qb_ref, vb_ref, zb_ref, ob_ref, True))
    c = HGRN_CHUNK

    def pair(i, bad):
        work, dots = [], []
        for dr, (q_ref, v_ref, z_ref, o_ref, reverse) in enumerate(dirs):
            base = pl.multiple_of(((n_chunks // 2 - 1 - i) if reverse else i) * 2 * c, 2 * c)
            rows = (pl.ds(base + c, c), pl.ds(base, c)) if reverse else (pl.ds(base, c), pl.ds(base + c, c))
            group = []
            for h in range(A_HEADS):
                sl = slice(h * LANES, (h + 1) * LANES)
                ps = [_hgrn_chunk_prep(q_ref[0, r, sl], v_ref[0, r, sl], z_ref[0, r, sl], lb_ref[:, sl], reverse)
                      for r in rows]
                bad = jnp.maximum(bad, jnp.maximum(ps[0]["bad"], ps[1]["bad"]))
                group.append((dr, h, o_ref, rows, sl, reverse, ps))
            dots += [_hgrn_pair_dots(ps[0], ps[1], st_scr[dr, h]) for dr, h, _, _, _, _, ps in group]
            work += group
        for (dr, h, _, _, _, _, ps), d in zip(work, dots):
            st_scr[dr, h] = (ps[0]["dec"] * ps[1]["dec"]) * st_scr[dr, h] + d[4]
        for (dr, h, o_ref, rows, sl, reverse, ps), d in zip(work, dots):
            o1, o2 = _hgrn_pair_out(ps[0], ps[1], *d[:4], reverse)
            o_ref[0, rows[0], sl] = o1.astype(o_ref.dtype)
            o_ref[0, rows[1], sl] = o2.astype(o_ref.dtype)
        return bad

    bad = lax.fori_loop(0, n_chunks // 2, pair, jnp.zeros((1, LANES), F32))

    @pl.when(jnp.max(bad) > 0.0)
    def _():
        st_scr[...] = save_scr[...]
        for dr, (q_ref, v_ref, z_ref, o_ref, reverse) in enumerate(dirs):
            _hgrn_exact_block(q_ref, v_ref, z_ref, lb_ref, ox_scr, st_scr, dr, reverse, n_chunks * c // SUBLANES)
            o_ref[0] = ox_scr[...].astype(o_ref.dtype)

    @pl.when(step == pl.num_programs(1) - 1)
    def _():
        sfin_ref[0] = st_scr[...]


def _hgrn_scan(proj, lb, s0):
    b, l, _ = proj.shape
    d = A_HEADS * LANES
    t = _row_tile(l, 512)
    nb = l // t
    fwd = lambda blk: pl.BlockSpec((1, t, d), lambda i, s: (i, s, blk))
    bwd = lambda blk: pl.BlockSpec((1, t, d), lambda i, s: (i, nb - 1 - s, blk))
    st_spec = pl.BlockSpec((1,) + s0.shape[1:], lambda i, s: (i, 0, 0, 0, 0))
    return pl.pallas_call(
        functools.partial(_hgrn_scan_body, n_chunks=t // HGRN_CHUNK),
        grid=(b, nb),
        in_specs=[fwd(0), fwd(1), fwd(2), bwd(0), bwd(1), bwd(3),
                  pl.BlockSpec((1, d), lambda i, s: (0, 0)), st_spec],
        out_specs=[fwd(0), bwd(0), st_spec],
        out_shape=[jax.ShapeDtypeStruct((b, l, d), BF16), jax.ShapeDtypeStruct((b, l, d), BF16),
                   jax.ShapeDtypeStruct(s0.shape, F32)],
        scratch_shapes=[pltpu.VMEM(s0.shape[1:], F32), pltpu.VMEM(s0.shape[1:], F32), pltpu.VMEM((t, d), F32)],
        compiler_params=_params("parallel", "arbitrary"),
        name="hgrn_scan",
    )(proj, proj, proj, proj, proj, proj, lb, s0)


def _hgrn2_mixer(x_c, x_l, ng, mod_c, mod_l, w_in, lower, out_g, w_out, g1, want_ctx):
    d = x_l.shape[-1]
    p_c, g_c = _norm_mod_proj(x_c, ng, mod_c[0], mod_c[1], w_in, F32, silu_cols=(2 * d, 3 * d))
    p_l, g_l = _norm_mod_proj(x_l, ng, mod_l[0], mod_l[1], w_in, F32, silu_cols=(2 * d, 3 * d))
    lb = lower.reshape(1, d)
    zero = jnp.zeros((x_l.shape[0], 2, A_HEADS, LANES, LANES), F32)
    ocf, ocb, s_ctx = _hgrn_scan(p_c, lb, zero)
    olf, olb, _ = _hgrn_scan(p_l, lb, s_ctx)
    o_c, o_l = (ocf, ocb), (olf, olb)
    og = out_g.reshape(1, d)
    new_l = _out_call(_out_hgrn_body, [(o_l[0], 0, d), (o_l[1], 0, d), (g_l, 0, d)], [og, w_out],
                      x_l, mod_l[2], g1, "hgrn_out")
    new_c = None
    if want_ctx:
        new_c = _out_call(_out_hgrn_body, [(o_c[0], 0, d), (o_c[1], 0, d), (g_c, 0, d)], [og, w_out],
                          x_c, mod_c[2], g1, "hgrn_out")
    return new_c, new_l


def _softmax_pv(s_list, v_list):
    mx = functools.reduce(jnp.maximum, [jnp.max(s, axis=-1, keepdims=True) for s in s_list])
    ps = [jnp.exp(s - mx) for s in s_list]
    den = functools.reduce(jnp.add, [jnp.sum(p, axis=-1, keepdims=True) for p in ps])
    acc = functools.reduce(jnp.add, [jnp.dot(p.astype(BF16), v, preferred_element_type=F32)
                                     for p, v in zip(ps, v_list)])
    return acc / den


def _na_lat_body(q_ref, k_ref, v_ref, kc_ref, vc_ref, t2_ref, o_ref, *, rows, kh, scale):
    w = GRID_W
    rb = NA_ROW_BLOCK
    nk = kh + rb
    n_rel = 2 * kh - 1
    first = lax.broadcasted_iota(jnp.int32, (rb * w, LANES), 1) < (LANES // 2)
    kc = kc_ref[0]
    vc = vc_ref[0]

    def bias_index(r, ks, m):
        r0 = jnp.clip(r - kh // 2, 0, rows - kh)
        ka = ks + 2 * m
        rel_a = ka - r + (kh - 1)
        in_a = (ka >= r0) & (ka < r0 + kh)
        in_b = (ka + 1 >= r0) & (ka + 1 < r0 + kh)
        both, only_b, only_a = rel_a, (n_rel - 1) + rel_a + 1, (2 * n_rel - 1) + rel_a
        return jnp.where(in_a, jnp.where(in_b, both, only_a), jnp.where(in_b, only_b, 3 * n_rel - 1))

    def block(i, carry):
        rq = i * rb
        ks = jnp.clip(rq - kh // 2, 0, rows - nk)
        q2 = q_ref[0, pl.ds(pl.multiple_of(rq * w, rb * w), rb * w), :] * scale
        k2 = k_ref[0, pl.ds(pl.multiple_of(ks * w, w), nk * w), :]
        v2 = v_ref[0, pl.ds(pl.multiple_of(ks * w, w), nk * w), :]
        outs = []
        for hh in range(2):
            qm = jnp.where(first if hh == 0 else jnp.logical_not(first), q2, jnp.zeros_like(q2))
            bias = jnp.concatenate(
                [jnp.concatenate([t2_ref[hh, bias_index(rq + a, ks, m)] for m in range(nk // 2)], axis=-1)
                 for a in range(rb)], axis=0)
            s_w = lax.dot_general(qm, k2, NT_DIMS, preferred_element_type=F32) + bias
            s_c = lax.dot_general(qm, kc, NT_DIMS, preferred_element_type=F32)
            outs.append(_softmax_pv([s_w, s_c], [v2, vc]))
        o_ref[0, pl.ds(pl.multiple_of(rq * w, rb * w), rb * w), :] = (
            jnp.where(first, outs[0], outs[1]).astype(o_ref.dtype))
        return carry

    lax.fori_loop(0, rows // rb, block, 0, unroll=4)


def _na_ctx_body(q_ref, k_ref, v_ref, o_ref, *, scale):
    n = q_ref.shape[1]
    first = lax.broadcasted_iota(jnp.int32, (n, LANES), 1) < (LANES // 2)
    q2 = q_ref[0] * scale
    k2 = k_ref[0]
    v2 = v_ref[0]
    outs = []
    for hh in range(2):
        qm = jnp.where(first if hh == 0 else jnp.logical_not(first), q2, jnp.zeros_like(q2))
        s = lax.dot_general(qm, k2, NT_DIMS, preferred_element_type=F32)
        outs.append(_softmax_pv([s], [v2]))
    o_ref[0] = jnp.where(first, outs[0], outs[1]).astype(o_ref.dtype)


def _na_bias_table(rpb, kh, kw):
    w = jnp.arange(GRID_W)[:, None]
    c = jnp.arange(GRID_W)[None, :]
    c0 = jnp.clip(w - kw // 2, 0, GRID_W - kw)
    inside = (c >= c0) & (c < c0 + kw)
    pad = GRID_W - kw
    padded = jnp.pad(rpb.astype(F32), ((0, 0), (0, 0), (pad, pad)))
    shifted = jnp.stack([padded[:, :, pad + kw - 1 - q:pad + kw - 1 - q + GRID_W] for q in range(GRID_W)], axis=2)
    t = jnp.where(inside[None, None], shifted, NEG_BIG)
    off = jnp.full_like(t, NEG_BIG)
    pair = lambda a, b: jnp.concatenate([a, b], axis=-1)
    return jnp.concatenate([pair(t[:, :-1], t[:, 1:]), pair(off, t), pair(t, off), pair(off, off)[:, :1]], axis=1)


def _na_mixer(x_c, x_l, ng, mod_c, mod_l, w_qkv, rpb, w_out, g1, want_ctx):
    b, l, d = x_l.shape
    lc = x_c.shape[1]
    heads = rpb.shape[0]
    na_rows, na_cols = (rpb.shape[1] + 1) // 2, (rpb.shape[2] + 1) // 2
    dh = d // heads
    assert 2 * dh == LANES and l % GRID_W == 0
    rows = l // GRID_W
    kh = min(na_rows, rows)
    assert kh == na_rows and (kh + NA_ROW_BLOCK) % 2 == 0
    assert rows % NA_ROW_BLOCK == 0 and rows >= kh + NA_ROW_BLOCK
    scale = dh ** -0.5
    nhp = heads // 2
    qkv_c = _norm_mod_proj(x_c, ng, mod_c[0], mod_c[1], w_qkv, BF16)
    qkv_l = _norm_mod_proj(x_l, ng, mod_l[0], mod_l[1], w_qkv, BF16)
    t2 = _na_bias_table(rpb, kh, na_cols)
    o_l = pl.pallas_call(
        functools.partial(_na_lat_body, rows=rows, kh=kh, scale=scale),
        grid=(b, nhp),
        in_specs=[pl.BlockSpec((1, l, LANES), lambda i, p: (i, 0, p)),
                  pl.BlockSpec((1, l, LANES), lambda i, p: (i, 0, nhp + p)),
                  pl.BlockSpec((1, l, LANES), lambda i, p: (i, 0, 2 * nhp + p)),
                  pl.BlockSpec((1, lc, LANES), lambda i, p: (i, 0, nhp + p)),
                  pl.BlockSpec((1, lc, LANES), lambda i, p: (i, 0, 2 * nhp + p)),
                  pl.BlockSpec((2,) + t2.shape[1:], lambda i, p: (p, 0, 0, 0))],
        out_specs=pl.BlockSpec((1, l, LANES), lambda i, p: (i, 0, p)),
        out_shape=jax.ShapeDtypeStruct((b, l, d), BF16),
        compiler_params=_params("parallel", "parallel"),
        name="na_latent",
    )(qkv_l, qkv_l, qkv_l, qkv_c, qkv_c, t2)
    new_l = _out_call(_out_plain_body, [(o_l, 0, d)], [w_out], x_l, mod_l[2], g1, "na_out")
    new_c = None
    if want_ctx:
        o_c = pl.pallas_call(
            functools.partial(_na_ctx_body, scale=scale),
            grid=(b, nhp),
            in_specs=[pl.BlockSpec((1, lc, LANES), lambda i, p: (i, 0, p)),
                      pl.BlockSpec((1, lc, LANES), lambda i, p: (i, 0, nhp + p)),
                      pl.BlockSpec((1, lc, LANES), lambda i, p: (i, 0, 2 * nhp + p))],
            out_specs=pl.BlockSpec((1, lc, LANES), lambda i, p: (i, 0, p)),
            out_shape=jax.ShapeDtypeStruct((b, lc, d), BF16),
            compiler_params=_params("parallel", "parallel"),
            name="na_context",
        )(qkv_c, qkv_c, qkv_c)
        new_c = _out_call(_out_plain_body, [(o_c, 0, d)], [w_out], x_c, mod_c[2], g1, "na_out")
    return new_c, new_l


S5_GROUP_BLOCK = 8


def _s5_kernel_body(c_ref, w_ref, o_ref):
    for i in range(c_ref.shape[0]):
        o_ref[i] = jnp.dot(c_ref[i], w_ref[i], preferred_element_type=F32, precision=lax.Precision.HIGHEST)


def _s5_impulse(cmat, wmat):
    n = cmat.shape[0]
    gb = S5_GROUP_BLOCK
    return pl.pallas_call(
        _s5_kernel_body,
        grid=(n // gb,),
        in_specs=[pl.BlockSpec((gb,) + cmat.shape[1:], lambda i: (i, 0, 0)),
                  pl.BlockSpec((gb,) + wmat.shape[1:], lambda i: (i, 0, 0))],
        out_specs=pl.BlockSpec((gb, cmat.shape[1], wmat.shape[2]), lambda i: (i, 0, 0)),
        out_shape=jax.ShapeDtypeStruct((n, cmat.shape[1], wmat.shape[2]), F32),
        compiler_params=_params("parallel"),
        name="s5_impulse",
    )(cmat, wmat)


def _s5_operators(lam_re, lam_im, log_dt, b_re, b_im, c_re, c_im, t):
    _, g, p = lam_re.shape
    cg = b_re.shape[-1]
    lam = lax.complex(lam_re.astype(F32), lam_im.astype(F32))
    ldt = lam * jnp.exp(log_dt.astype(F32))[..., None]
    a = jnp.exp(ldt)
    bbar = ((a - 1.0) / lam)[..., None] * lax.complex(b_re.astype(F32), b_im.astype(F32))
    cm = lax.complex(c_re.astype(F32), c_im.astype(F32))
    apow = jnp.exp(ldt[..., None] * jnp.arange(t + 1, dtype=F32))
    w = apow[..., :t, None] * bbar[:, :, :, None, :]
    wmat = jnp.concatenate([jnp.real(w), jnp.imag(w)], axis=2).reshape(2 * g, 2 * p, t * cg)
    cmat = jnp.concatenate([jnp.real(cm), -jnp.imag(cm)], axis=-1).reshape(2 * g, cg, 2 * p)
    k = _s5_impulse(cmat, wmat).reshape(2, g, cg, t, cg)
    k = jnp.transpose(k, (0, 1, 3, 4, 2))
    kfull = jnp.concatenate([k[1, :, :0:-1], (k[0, :, :1] + k[1, :, :1]), k[0, :, 1:]], axis=1)
    lagvec = jnp.transpose(kfull, (0, 2, 1, 3)).reshape(g, cg, (2 * t - 1) * cg).astype(BF16)
    toep = jnp.stack([lagvec[:, :, (t - 1 - s) * cg:(2 * t - 1 - s) * cg] for s in range(t)], axis=1)
    toep = toep.reshape(g, t * cg, t * cg)

    def state_in(wd, flip):
        wd = wd[:, :, ::-1] if flip else wd
        m = jnp.transpose(wd, (0, 2, 3, 1)).reshape(g, t * cg, p)
        return jnp.concatenate([jnp.real(m), jnp.imag(m)], axis=-1)

    def state_out(cd, pw):
        n = cd[:, :, :, None] * pw[:, None, :, :]
        n = jnp.transpose(n, (0, 2, 3, 1)).reshape(g, p, t * cg)
        return jnp.concatenate([jnp.real(n), -jnp.imag(n)], axis=1)

    m_f = state_in(w[0], True)
    m_b = state_in(w[1], False)
    n_f = state_out(cm[0], apow[0][..., 1:])
    n_b = state_out(cm[1], apow[1][..., :0:-1])
    at = apow[..., t]
    a1 = jnp.concatenate([jnp.real(at), jnp.real(at)], axis=-1)
    a2 = jnp.concatenate([-jnp.imag(at), jnp.imag(at)], axis=-1)
    bf = lambda m: m.astype(BF16)
    return dict(toep=bf(toep), m_f=bf(m_f), m_b=bf(m_b), n_f=bf(n_f), n_b=bf(n_b), a1=a1, a2=a2)


def _s5_body(u_ref, dsk_ref, toep_ref, mf_ref, mb_ref, nf_ref, nb_ref, a1_ref, a2_ref, x0_ref,
             y_ref, xfin_ref, z_scr, zs_scr, xin_scr):
    gb, nc = u_ref.shape[1], u_ref.shape[2]
    half = LANES // 2
    ubs = [u_ref[0, gi].astype(BF16) for gi in range(gb)]
    for dr, m_ref in enumerate((mf_ref, mb_ref)):
        z = jnp.stack([jnp.dot(ubs[gi], m_ref[gi], preferred_element_type=F32) for gi in range(gb)])
        z_scr[dr] = jnp.swapaxes(z, 0, 1)

    swap = lambda t: pltpu.roll(t, half, t.ndim - 1)
    for dr in range(2):
        zs_scr[dr] = swap(z_scr[dr].reshape(nc * gb, LANES)).reshape(nc, gb, LANES)

    coef = [(a1_ref[dr], a2_ref[dr]) for dr in range(2)]

    def scan(j, xs):
        out = []
        for dr in range(2):
            jj = j if dr == 0 else nc - 1 - j
            x, xw = xs[2 * dr], xs[2 * dr + 1]
            a1, a2 = coef[dr]
            xin_scr[dr, jj] = x
            out.append(a1 * x + a2 * xw + z_scr[dr, jj])
            out.append(a1 * xw - a2 * x + zs_scr[dr, jj])
        return tuple(out)

    x0f, x0b = x0_ref[0, 0], x0_ref[0, 1]
    xf, _, xb, _ = lax.fori_loop(0, nc, scan, (x0f, swap(x0f), x0b, swap(x0b)), unroll=4)
    xfin_ref[0, 0] = xf
    xfin_ref[0, 1] = xb
    xin = [jnp.swapaxes(xin_scr[dr], 0, 1).astype(BF16) for dr in range(2)]
    for gi in range(gb):
        u = u_ref[0, gi]
        y = jnp.dot(ubs[gi], toep_ref[gi], preferred_element_type=F32) + u * dsk_ref[gi]
        y = y + jnp.dot(xin[0][gi], nf_ref[gi], preferred_element_type=F32)
        y = y + jnp.dot(xin[1][gi], nb_ref[gi], preferred_element_type=F32)
        y_ref[0, gi] = y


def _piece_transpose(src, piece_of, cg):
    n = len(src)
    out = [None] * n
    for s in range(n):
        c = src[s % n]
        for p in range(1, n):
            c = jnp.where(piece_of == p, src[(p + s) % n], c)
        r = c if s == 0 else pltpu.roll(c, s * cg, 1)
        for p in range(n):
            out[p] = r if s == 0 else jnp.where(piece_of == (p + s) % n, r, out[p])
    return out


def _to_chunks(h_scr, o_ref, t, cg):
    ncb = h_scr.shape[1] // t
    pieces = LANES // cg
    piece_of = lax.broadcasted_iota(jnp.int32, (ncb, LANES), 1) // cg

    def lane_block(lb, carry):
        for tq in range(t // pieces):
            steps = [h_scr[lb, pl.ds(tq * pieces + tr, ncb, stride=t), :] for tr in range(pieces)]
            for gl, col in enumerate(_piece_transpose(steps, piece_of, cg)):
                o_ref[0, lb * pieces + gl, :, tq * LANES:(tq + 1) * LANES] = col
        return carry

    lax.fori_loop(0, h_scr.shape[0], lane_block, 0, unroll=2)


def _from_chunks(y_ref, y_scr, t, cg):
    ncb = y_ref.shape[2]
    pieces = LANES // cg
    piece_of = lax.broadcasted_iota(jnp.int32, (ncb, LANES), 1) // cg

    def lane_block(lb, carry):
        for tq in range(t // pieces):
            cols = [y_ref[0, lb * pieces + gl, :, tq * LANES:(tq + 1) * LANES] for gl in range(pieces)]
            for tr, step in enumerate(_piece_transpose(cols, piece_of, cg)):
                y_scr[lb, pl.ds(tq * pieces + tr, ncb, stride=t), :] = step
        return carry

    lax.fori_loop(0, y_scr.shape[0], lane_block, 0, unroll=2)


def _norm_chunks_body(x_ref, g_ref, sh_ref, sc_ref, o_ref, h_scr, *, t, cg):
    h = _norm_mod(x_ref[...], g_ref[...], sh_ref[0], sc_ref[0])
    for lb in range(h_scr.shape[0]):
        h_scr[lb] = h[:, lb * LANES:(lb + 1) * LANES]
    _to_chunks(h_scr, o_ref, t, cg)


def _norm_mod_chunks(x, g, sh, sc, groups, t):
    b, l, d = x.shape
    cg = d // groups
    tm = _row_tile(l, 512)
    per = l // tm
    assert tm % t == 0 and (tm // t) % SUBLANES == 0 and LANES % cg == 0 and t % (LANES // cg) == 0
    return pl.pallas_call(
        functools.partial(_norm_chunks_body, t=t, cg=cg),
        grid=(b * per,),
        in_specs=[pl.BlockSpec((tm, d), lambda i: (i, 0)),
                  pl.BlockSpec((1, d), lambda i: (0, 0)),
                  _mod_spec(sh, per), _mod_spec(sc, per)],
        out_specs=pl.BlockSpec((1, groups, tm // t, t * cg), lambda i: (i // per, 0, i % per, 0)),
        out_shape=jax.ShapeDtypeStruct((b, groups, l // t, t * cg), F32),
        scratch_shapes=[pltpu.VMEM((d // LANES, tm, LANES), F32)],
        compiler_params=_params("parallel"),
        name="norm_mod_chunks",
    )(x.reshape(b * l, d), g.reshape(1, d), sh.table, sc.table)


def _glu_chunks_body(y_ref, w_ref, res_ref, gate_ref, g1_ref, o_ref, y_scr, *, t, cg):
    _from_chunks(y_ref, y_scr, t, cg)
    d = o_ref.shape[-1]
    y = jnp.concatenate([y_scr[lb] for lb in range(y_scr.shape[0])], axis=-1)
    a = _gelu_tanh(y).astype(BF16)
    ag = jnp.dot(a, w_ref[...], preferred_element_type=F32)
    o_ref[...] = _residual(res_ref, gate_ref, g1_ref, ag[:, :d] * jax.nn.sigmoid(ag[:, d:]))


def _glu_out_chunks(y, w_glu, res, gate, g1, t):
    b, l, d = res.shape
    groups = y.shape[1]
    cg = d // groups
    tm = _row_tile(l, 512)
    per = l // tm
    out = pl.pallas_call(
        functools.partial(_glu_chunks_body, t=t, cg=cg),
        grid=(b * per,),
        in_specs=[pl.BlockSpec((1, groups, tm // t, t * cg), lambda i: (i // per, 0, i % per, 0)),
                  _layer_spec(w_glu, w_glu.shape, lambda i: (0, 0)),
                  pl.BlockSpec((tm, d), lambda i: (i, 0)),
                  _mod_spec(gate, per),
                  pl.BlockSpec((1, d), lambda i: (0, 0))],
        out_specs=pl.BlockSpec((tm, d), lambda i: (i, 0)),
        out_shape=jax.ShapeDtypeStruct((b * l, d), F32),
        scratch_shapes=[pltpu.VMEM((d // LANES, tm, LANES), F32)],
        compiler_params=_params("parallel"),
        name="s5_glu_out",
    )(y, w_glu.stack, res.reshape(b * l, d), gate.table, g1.reshape(1, d))
    return out.reshape(b, l, d)


def _s5_scan(u, ops, dsk, x0):
    b, g, nc, tc = u.shape
    gb = S5_GROUP_BLOCK
    per_g = lambda shape: pl.BlockSpec((gb,) + shape, lambda j, i: (j,) + (0,) * len(shape))
    y, xfin = pl.pallas_call(
        _s5_body,
        grid=(g // gb, b),
        in_specs=[pl.BlockSpec((1, gb, nc, tc), lambda j, i: (i, j, 0, 0)),
                  per_g((1, tc)), per_g((tc, tc)), per_g((tc, LANES)), per_g((tc, LANES)),
                  per_g((LANES, tc)), per_g((LANES, tc)),
                  pl.BlockSpec((2, gb, LANES), lambda j, i: (0, j, 0)),
                  pl.BlockSpec((2, gb, LANES), lambda j, i: (0, j, 0)),
                  pl.BlockSpec((1, 2, gb, LANES), lambda j, i: (i, 0, j, 0))],
        out_specs=[pl.BlockSpec((1, gb, nc, tc), lambda j, i: (i, j, 0, 0)),
                   pl.BlockSpec((1, 2, gb, LANES), lambda j, i: (i, 0, j, 0))],
        out_shape=[jax.ShapeDtypeStruct((b, g, nc, tc), F32), jax.ShapeDtypeStruct((b, 2, g, LANES), F32)],
        scratch_shapes=[pltpu.VMEM((2, nc, gb, LANES), F32)] * 3,
        compiler_params=_params("parallel", "parallel"),
        name="s5_scan",
    )(u, dsk, ops["toep"], ops["m_f"], ops["m_b"], ops["n_f"], ops["n_b"], ops["a1"], ops["a2"], x0)
    return y, xfin


def _s5_mixer(x_c, x_l, ng, mod_c, mod_l, lam_re, lam_im, log_dt, b_re, b_im, c_re, c_im, d_skip, w_glu, g1,
              want_ctx):
    b, l, d = x_l.shape
    g, p = lam_re.shape[1], lam_re.shape[2]
    cg = d // g
    assert 2 * p == LANES and g % S5_GROUP_BLOCK == 0
    ops = _s5_operators(lam_re, lam_im, log_dt, b_re, b_im, c_re, c_im, S5_CHUNK)
    dsk = jnp.tile(d_skip.astype(F32).reshape(g, 1, cg), (1, 1, S5_CHUNK))
    u_c = _norm_mod_chunks(x_c, ng, mod_c[0], mod_c[1], g, S5_CHUNK)
    u_l = _norm_mod_chunks(x_l, ng, mod_l[0], mod_l[1], g, S5_CHUNK)
    y_c, x_ctx = _s5_scan(u_c, ops, dsk, jnp.zeros((b, 2, g, LANES), F32))
    y_l, _ = _s5_scan(u_l, ops, dsk, x_ctx)
    new_l = _glu_out_chunks(y_l, w_glu, x_l, mod_l[2], g1, S5_CHUNK)
    new_c = _glu_out_chunks(y_c, w_glu, x_c, mod_c[2], g1, S5_CHUNK) if want_ctx else None
    return new_c, new_l


FFN_HALO = 16
FFN_TN = 256
FFN_TILES_PER_STEP = 11
FFN_ROW_SPLIT = 256


def _ffn_body(xp_ref, x_ref, xn_ref, g2_ref, sh_ref, sc_ref, *rest, per, n_tiles, tps):
    w_refs = rest[:2 * tps]
    cw_ref, cb_ref, wo_ref, gate_ref, g3_ref, o_ref, h_scr, gat_scr = rest[2 * tps:]
    i, j = pl.program_id(0), pl.program_id(1)
    tm = x_ref.shape[0]
    hl = FFN_HALO
    f = wo_ref.shape[0]
    n_steps = pl.cdiv(n_tiles, tps)
    when = (lambda cond: lambda fn: fn()) if n_steps == 1 else pl.when

    @when(j == 0)
    def _():
        g, sh, sc = g2_ref[...], sh_ref[0], sc_ref[0]
        keep_p = ((i % per) != 0).astype(F32)
        keep_n = ((i % per) != per - 1).astype(F32)
        h_scr[0:hl] = (_norm_mod(xp_ref[...], g, sh, sc) * keep_p).astype(BF16)
        h_scr[hl:hl + tm] = _norm_mod(x_ref[...], g, sh, sc).astype(BF16)
        h_scr[hl + tm:] = (_norm_mod(xn_ref[...], g, sh, sc) * keep_n).astype(BF16)

    sub = FFN_ROW_SPLIT if tm % FFN_ROW_SPLIT == 0 else tm
    ns = tm // sub
    edge = SUBLANES
    tn = w_refs[0].shape[1]

    def pieces(w_ref):
        out = []
        for s in range(ns):
            lo = hl + s * sub - (hl if s == 0 else 0)
            hi = hl + (s + 1) * sub + (hl if s == ns - 1 else 0)
            out.append(jnp.dot(h_scr[lo:hi], w_ref[...], preferred_element_type=F32))
        return out

    def conv(p, s, cw, cb):
        base = hl if s == 0 else 0
        left = p[s][hl - edge:hl] if s == 0 else p[s - 1][-edge:]
        right = p[s][base + sub:base + sub + edge] if s == ns - 1 else p[s + 1][:edge]
        ext = jnp.concatenate([left, p[s][base:base + sub], right], axis=0)
        um = pltpu.roll(ext, 1, 0)[edge:edge + sub]
        up = pltpu.roll(ext, sub + 2 * edge - 1, 0)[edge:edge + sub]
        return cb + um * cw[0:1] + ext[edge:edge + sub] * cw[1:2] + up * cw[2:3]

    def tile(k):
        col = k * tn if n_steps == 1 else pl.multiple_of((j * tps + k) * tn, tn)
        cols_a, cols_v = pl.ds(col, tn), pl.ds(pl.multiple_of(f + col, tn), tn)
        pa, pv = pieces(w_refs[2 * k]), pieces(w_refs[2 * k + 1])
        for s in range(ns):
            a = conv(pa, s, cw_ref[:, cols_a], cb_ref[:, cols_a])
            v = conv(pv, s, cw_ref[:, cols_v], cb_ref[:, cols_v])
            gat_scr[s * sub:(s + 1) * sub, cols_a] = (a * jax.nn.sigmoid(a) * v).astype(BF16)

    always = n_tiles - tps * (n_steps - 1)
    for k in range(tps):
        if k < always:
            tile(k)
        else:
            pl.when(j * tps + k < n_tiles)(functools.partial(tile, k))

    @when(j == n_steps - 1)
    def _():
        y = jnp.dot(gat_scr[...], wo_ref[...], preferred_element_type=F32)
        o_ref[...] = x_ref[...] + gate_ref[0] * _rms(y, g3_ref[...])


def _conv_ffn_block(x, ng2, mod, w_in, conv_w, conv_b, w_out, ng3):
    b, l, d = x.shape
    f = w_out.shape[0]
    tn, hl = FFN_TN, FFN_HALO
    assert f % tn == 0 and conv_w.shape[0] == 3 and w_in.shape[1] == 2 * f
    nj = f // tn
    tm = _row_tile(l, 1024)
    per = l // tm
    hb = tm // hl
    last_hb = b * l // hl - 1
    cb = conv_b.reshape(1, 2 * f)
    x2 = x.reshape(b * l, d)
    tps = min(FFN_TILES_PER_STEP, nj)
    once = dict(pipeline_mode=pl.Buffered(1)) if tps == nj else {}
    w_specs = []
    for k in range(tps):
        w_specs.append(_layer_spec(w_in, (d, tn), lambda i, j, k=k: (0, jnp.minimum(j * tps + k, nj - 1)), **once))
        w_specs.append(_layer_spec(w_in, (d, tn), lambda i, j, k=k: (0, nj + jnp.minimum(j * tps + k, nj - 1)),
                                   **once))
    whole = lambda arr: pl.BlockSpec(arr.shape, lambda i, j: (0,) * arr.ndim)
    out = pl.pallas_call(
        functools.partial(_ffn_body, per=per, n_tiles=nj, tps=tps),
        grid=(b * per, pl.cdiv(nj, tps)),
        in_specs=[pl.BlockSpec((hl, d), lambda i, j: (jnp.maximum(i * hb - 1, 0), 0)),
                  pl.BlockSpec((tm, d), lambda i, j: (i, 0)),
                  pl.BlockSpec((hl, d), lambda i, j: (jnp.minimum((i + 1) * hb, last_hb), 0)),
                  pl.BlockSpec((1, d), lambda i, j: (0, 0)),
                  _mod_spec(mod[3], per), _mod_spec(mod[4], per),
                  *w_specs, whole(conv_w), whole(cb),
                  _layer_spec(w_out, (f, d), lambda i, j: (0, 0), pipeline_mode=pl.Buffered(1)),
                  _mod_spec(mod[5], per),
                  pl.BlockSpec((1, d), lambda i, j: (0, 0))],
        out_specs=pl.BlockSpec((tm, d), lambda i, j: (i, 0)),
        out_shape=jax.ShapeDtypeStruct((b * l, d), F32),
        scratch_shapes=[pltpu.VMEM((tm + 2 * hl, d), BF16), pltpu.VMEM((tm, f), BF16)],
        compiler_params=_params("parallel", "arbitrary"),
        name="conv_ffn",
    )(x2, x2, x2, ng2.reshape(1, d), mod[3].table, mod[4].table, *([w_in.stack] * (2 * tps)), conv_w, cb,
      w_out.stack,
      mod[5].table, ng3.reshape(1, d))
    return out.reshape(b, l, d)


def kernel(x, c, ctx, c_ctx, w_mod, b_mod, norm_g, a_w_in, a_lower_logits, a_out_g, a_w_out, b_w_qkv, b_rpb,
           b_w_out, c_lam_re, c_lam_im, c_log_dt, c_b_re, c_b_im, c_c_re, c_c_im, c_d, c_w_glu, f_w_in,
           f_conv_w, f_conv_b, f_w_out):
    bsz, _, d = x.shape
    depth = w_mod.shape[0]
    p = jax.nn.softmax(a_lower_logits.astype(F32), axis=0)
    lower = jnp.cumsum(p, axis=0) - p[0]
    n_rows = -(-(bsz + 1) // SUBLANES) * SUBLANES
    c_rows = jnp.zeros((n_rows, d), F32).at[:bsz].set(c).at[bsz].set(c_ctx)
    table = _modulation(c_rows, w_mod, b_mod).reshape(depth * n_rows * N_MOD, 1, d)
    wa_in, wa_out, wb_qkv, wb_out, wc_glu, wf_in, wf_out = (
        w.astype(BF16) for w in (a_w_in, a_w_out, b_w_qkv, b_w_out, c_w_glu, f_w_in, f_w_out))
    lat, cx = x, ctx
    for i in range(depth):
        kind, j = i % N_MIXERS, i // N_MIXERS
        last = i == depth - 1
        mod_l = [ModRow(table, i * n_rows * N_MOD + k, N_MOD) for k in range(N_MOD)]
        mod_c = [ModRow(table, (i * n_rows + bsz) * N_MOD + k, 0) for k in range(N_MOD)]
        ng = norm_g[i]
        if kind == 0:
            cx1, lat = _hgrn2_mixer(cx, lat, ng[0], mod_c, mod_l, LayerOf(wa_in, j), lower[i], a_out_g[j],
                                    LayerOf(wa_out, j), ng[1], not last)
        elif kind == 1:
            cx1, lat = _na_mixer(cx, lat, ng[0], mod_c, mod_l, LayerOf(wb_qkv, j), b_rpb[j], LayerOf(wb_out, j),
                                 ng[1], not last)
        else:
            cx1, lat = _s5_mixer(cx, lat, ng[0], mod_c, mod_l, c_lam_re[j], c_lam_im[j], c_log_dt[j], c_b_re[j],
                                 c_b_im[j], c_c_re[j], c_c_im[j], c_d[j], LayerOf(wc_glu, j), ng[1], not last)
        ffn = (LayerOf(wf_in, i), f_conv_w[i], f_conv_b[i], LayerOf(wf_out, i), ng[3])
        lat = _conv_ffn_block(lat, ng[2], mod_l, *ffn)
        if not last:
            cx = _conv_ffn_block(cx1, ng[2], mod_c, *ffn)
    return lat
```

```python
import functools
from typing import NamedTuple

import jax
import jax.numpy as jnp
from jax import lax
from jax.experimental import pallas as pl
from jax.experimental.pallas import tpu as pltpu

F32 = jnp.float32
BF16 = jnp.bfloat16

EPS = 1e-6
F_MIN = 1e-30
N_MOD = 6
N_MIXERS = 3
A_HEADS = 8
GRID_W = 64
S5_CHUNK = 16
NEG_BIG = -1e30
NA_ROW_BLOCK = 4
SUBLANES = 8
LANES = 128
VMEM_LIMIT_BYTES = 56 * 1024 * 1024

NT_DIMS = (((1,), (1,)), ((), ()))
TN_DIMS = (((0,), (0,)), ((), ()))


def _params(*sem):
    return pltpu.CompilerParams(dimension_semantics=sem, vmem_limit_bytes=VMEM_LIMIT_BYTES)


def _row_tile(n, want):
    t = min(n, want)
    assert n % t == 0, (n, t)
    return t


def _rms(y, g):
    return y * lax.rsqrt(jnp.mean(y * y, axis=-1, keepdims=True) + EPS) * g


def _mod_body(c_ref, w_ref, b_ref, o_ref):
    c = c_ref[...]
    s = (c * jax.nn.sigmoid(c)).astype(BF16)
    o_ref[0] = jnp.dot(s, w_ref[0].astype(BF16), preferred_element_type=F32) + b_ref[0]


def _modulation(c_rows, w_mod, b_mod):
    depth, d, n = w_mod.shape
    r = c_rows.shape[0]
    tn = n // 4
    return pl.pallas_call(
        _mod_body,
        grid=(depth, n // tn),
        in_specs=[pl.BlockSpec((r, d), lambda i, j: (0, 0)),
                  pl.BlockSpec((1, d, tn), lambda i, j: (i, 0, j)),
                  pl.BlockSpec((1, 1, tn), lambda i, j: (i, 0, j))],
        out_specs=pl.BlockSpec((1, r, tn), lambda i, j: (i, 0, j)),
        out_shape=jax.ShapeDtypeStruct((depth, r, n), F32),
        compiler_params=_params("parallel", "parallel"),
        name="adaln_mod",
    )(c_rows, w_mod, b_mod.reshape(depth, 1, n))


def _norm_mod(x, g, sh, sc):
    return _rms(x, g) * (1.0 + sc) + sh


class ModRow(NamedTuple):
    table: jax.Array
    first: int
    step: int


def _mod_spec(m, per):
    return pl.BlockSpec((1, 1, m.table.shape[-1]), lambda *idx: (m.first + m.step * (idx[0] // per), 0, 0))


class LayerOf(NamedTuple):
    stack: jax.Array
    layer: int

    @property
    def shape(self):
        return self.stack.shape[1:]


def _layer_spec(w, block, index_map, **kw):
    return pl.BlockSpec((None,) + tuple(block), lambda *idx: (w.layer,) + tuple(index_map(*idx)), **kw)


def _proj_body(x_ref, g_ref, sh_ref, sc_ref, w_ref, o_ref):
    h = _norm_mod(x_ref[...], g_ref[...], sh_ref[0], sc_ref[0]).astype(BF16)
    o_ref[...] = jnp.dot(h, w_ref[...], preferred_element_type=F32).astype(o_ref.dtype)


def _proj_silu_body(x_ref, g_ref, sh_ref, sc_ref, w_ref, o_ref, os_ref, *, lo, hi):
    h = _norm_mod(x_ref[...], g_ref[...], sh_ref[0], sc_ref[0]).astype(BF16)
    y = jnp.dot(h, w_ref[...], preferred_element_type=F32)
    o_ref[:, :lo] = y[:, :lo]
    o_ref[:, lo:] = y[:, hi:]
    gp = y[:, lo:hi]
    os_ref[...] = (gp * jax.nn.sigmoid(gp)).astype(os_ref.dtype)


def _norm_mod_proj(x, g, sh, sc, w, out_dtype, silu_cols=None):
    b, l, d = x.shape
    n = w.shape[1]
    tm = _row_tile(l, 512)
    per = l // tm
    body, widths, dtypes = _proj_body, [n], [out_dtype]
    if silu_cols is not None:
        lo, hi = silu_cols
        body, widths, dtypes = functools.partial(_proj_silu_body, lo=lo, hi=hi), [n - (hi - lo), hi - lo], [out_dtype, BF16]
    outs = pl.pallas_call(
        body,
        grid=(b * per,),
        in_specs=[pl.BlockSpec((tm, d), lambda i: (i, 0)),
                  pl.BlockSpec((1, d), lambda i: (0, 0)),
                  _mod_spec(sh, per), _mod_spec(sc, per),
                  _layer_spec(w, (d, n), lambda i: (0, 0), pipeline_mode=pl.Buffered(1))],
        out_specs=[pl.BlockSpec((tm, wd), lambda i: (i, 0)) for wd in widths],
        out_shape=[jax.ShapeDtypeStruct((b * l, wd), dt) for wd, dt in zip(widths, dtypes)],
        compiler_params=_params("parallel"),
        name="norm_mod_proj",
    )(x.reshape(b * l, d), g.reshape(1, d), sh.table, sc.table, w.stack)
    outs = [o.reshape(b, l, o.shape[-1]) for o in outs]
    return outs[0] if silu_cols is None else tuple(outs)


def _residual(res_ref, gate_ref, g1_ref, y):
    return res_ref[...] + gate_ref[0] * _rms(y, g1_ref[...])


def _out_plain_body(a_ref, w_ref, res_ref, gate_ref, g1_ref, o_ref):
    y = jnp.dot(a_ref[...], w_ref[...], preferred_element_type=F32)
    o_ref[...] = _residual(res_ref, gate_ref, g1_ref, y)


def _out_hgrn_body(of_ref, ob_ref, gp_ref, og_ref, w_ref, res_ref, gate_ref, g1_ref, o_ref):
    o = of_ref[...].astype(F32) + ob_ref[...].astype(F32)
    parts = []
    for h in range(A_HEADS):
        oh = o[:, h * LANES:(h + 1) * LANES]
        parts.append(oh * lax.rsqrt(jnp.mean(oh * oh, axis=-1, keepdims=True) + EPS))
    a = jnp.concatenate(parts, axis=-1) * og_ref[...] * gp_ref[...].astype(F32)
    y = jnp.dot(a.astype(BF16), w_ref[...], preferred_element_type=F32)
    o_ref[...] = _residual(res_ref, gate_ref, g1_ref, y)


def _gelu_tanh(y):
    return 0.5 * y * (1.0 + jnp.tanh(0.7978845608028654 * (y + 0.044715 * (y * y * y))))


def _out_call(body, row_inputs, const_inputs, res, gate, g1, name):
    b, l, d = res.shape
    tm = _row_tile(l, 512)
    per = l // tm
    in_specs, args = [], []
    for arr, blk, width in row_inputs:
        in_specs.append(pl.BlockSpec((tm, width), lambda i, blk=blk: (i, blk)))
        args.append(arr.reshape(b * l, arr.shape[-1]))
    for arr in const_inputs:
        if isinstance(arr, LayerOf):
            in_specs.append(_layer_spec(arr, arr.shape, lambda i, nd=len(arr.shape): (0,) * nd))
            args.append(arr.stack)
        else:
            in_specs.append(pl.BlockSpec(arr.shape, lambda i, nd=arr.ndim: (0,) * nd))
            args.append(arr)
    in_specs += [pl.BlockSpec((tm, d), lambda i: (i, 0)),
                 _mod_spec(gate, per),
                 pl.BlockSpec((1, d), lambda i: (0, 0))]
    args += [res.reshape(b * l, d), gate.table, g1.reshape(1, d)]
    out = pl.pallas_call(
        body,
        grid=(b * per,),
        in_specs=in_specs,
        out_specs=pl.BlockSpec((tm, d), lambda i: (i, 0)),
        out_shape=jax.ShapeDtypeStruct((b * l, d), F32),
        compiler_params=_params("parallel"),
        name=name,
    )(*args)
    return out.reshape(b, l, d)


HGRN_CHUNK = 32
HGRN_SAFE_MIN = 1e-30
HGRN_Q_HEADROOM = 1e37


def _hgrn_gates(z, lb):
    f = lb + (1.0 - lb) * jax.nn.sigmoid(z)
    return jnp.maximum(f, F_MIN), 1.0 - f


def _hgrn_chunk_prep(q, v, z, lb, reverse):
    c = q.shape[0]
    half = c // 2
    fm, kk = _hgrn_gates(z, lb)
    pos = lax.broadcasted_iota(jnp.int32, (c, LANES), 0)
    if reverse:
        pos = (c - 1) - pos
    second = pos >= half
    hpos = jnp.where(second, pos - half, pos)
    ph = fm
    j = 1
    while j < half:
        ph = ph * jnp.where(hpos >= j, pltpu.roll(ph, (c - j) if reverse else j, 0), 1.0)
        j *= 2
    a_last = ph[half:half + 1] if reverse else ph[half - 1:half]
    r_last = ph[0:1] if reverse else ph[c - 1:c]
    p_mid = a_last
    p_last = a_last * r_last
    pp = jnp.where(second, ph * a_last, ph)
    r = jnp.where(second, ph, ph * (1.0 / a_last))
    kd = kk * (1.0 / r)
    q_max = jnp.max(jnp.abs(q), axis=0, keepdims=True)
    ok = jnp.where(p_mid >= HGRN_SAFE_MIN, r_last, 0.0) >= HGRN_SAFE_MIN
    ok = jnp.where(ok, p_mid * HGRN_Q_HEADROOM, -1.0) >= q_max
    return dict(qr=(q * r).astype(BF16), kd=kd.astype(BF16), qp=q * pp, kl=kd * r_last, vb=v.astype(BF16),
                dec=p_last, bad=jnp.where(ok, 0.0, 1.0))


def _hgrn_pair_dots(p1, p2, st):
    bf = lambda t: t.astype(BF16)
    sc1 = lax.dot_general(p1["qr"], p1["kd"], NT_DIMS, preferred_element_type=F32)
    sc2 = lax.dot_general(p2["qr"], p2["kd"], NT_DIMS, preferred_element_type=F32)
    cross = lax.dot_general(bf(p2["qp"]), bf(p1["kl"]), NT_DIMS, preferred_element_type=F32)
    q_all = jnp.concatenate([bf(p1["qp"]), bf(p2["qp"] * p1["dec"])], axis=0)
    o_st = lax.dot_general(q_all, st.astype(BF16), NT_DIMS, preferred_element_type=F32)
    k_all = jnp.concatenate([bf(p1["kl"] * p2["dec"]), bf(p2["kl"])], axis=0)
    u = lax.dot_general(jnp.concatenate([p1["vb"], p2["vb"]], axis=0), k_all, TN_DIMS, preferred_element_type=F32)
    return sc1, sc2, cross, o_st, u


def _hgrn_pair_out(p1, p2, sc1, sc2, cross, o_st, reverse):
    c = sc1.shape[0]
    row = lax.broadcasted_iota(jnp.int32, (c, c), 0)
    col = lax.broadcasted_iota(jnp.int32, (c, c), 1)
    keep = (col >= row) if reverse else (col <= row)
    tri = lambda sc: jnp.where(keep, sc, 0.0).astype(BF16)
    o1 = jnp.dot(tri(sc1), p1["vb"], preferred_element_type=F32) + o_st[:c]
    o2 = (jnp.dot(tri(sc2), p2["vb"], preferred_element_type=F32)
          + jnp.dot(cross.astype(BF16), p1["vb"], preferred_element_type=F32) + o_st[c:])
    return o1, o2


def _hgrn_exact_block(q_ref, v_ref, z_ref, lb_ref, o_scr, st_scr, dr, reverse, n_tiles):
    pos = lax.broadcasted_iota(jnp.int32, (SUBLANES, LANES), 0)
    if reverse:
        pos = (SUBLANES - 1) - pos

    def earlier(x, j):
        if j == 0:
            return x
        return pltpu.roll(x, (SUBLANES - j) if reverse else j, 0)

    def later(x, j):
        return pltpu.roll(x, j if reverse else (SUBLANES - j), 0)

    def tile(i, carry):
        ti = (n_tiles - 1 - i) if reverse else i
        r0 = pl.multiple_of(ti * SUBLANES, SUBLANES)
        for h in range(A_HEADS):
            sl = slice(h * LANES, (h + 1) * LANES)
            lb = lb_ref[:, sl]
            z = z_ref[0, pl.ds(r0, SUBLANES), sl]
            q = q_ref[0, pl.ds(r0, SUBLANES), sl]
            v = v_ref[0, pl.ds(r0, SUBLANES), sl]
            fm, kk = _hgrn_gates(z, lb)
            pp = fm
            for j in (1, 2, 4):
                pp = pp * jnp.where(pos >= j, earlier(pp, j), 1.0)
            qq = jnp.where(pos <= SUBLANES - 2, later(fm, 1), 1.0)
            for j in (1, 2, 4):
                qq = qq * jnp.where(pos <= SUBLANES - 1 - j, later(qq, j), 1.0)
            dec = pp[0:1] if reverse else pp[SUBLANES - 1:SUBLANES]
            st = st_scr[dr, h]
            o = lax.dot_general((q * pp).astype(BF16), st.astype(BF16), NT_DIMS,
                                preferred_element_type=F32)
            g = fm
            for d in range(SUBLANES):
                if d == 0:
                    e = q * kk
                else:
                    if d > 1:
                        g = g * earlier(fm, d - 1)
                    e = q * g * jnp.where(pos >= d, earlier(kk, d), 0.0)
                o = o + jnp.sum(e, axis=-1, keepdims=True) * earlier(v, d)
            o_scr[pl.ds(r0, SUBLANES), sl] = o
            u = lax.dot_general(v.astype(BF16), (kk * qq).astype(BF16), TN_DIMS,
                                preferred_element_type=F32)
            st_scr[dr, h] = dec * st + u
        return carry

    lax.fori_loop(0, n_tiles, tile, 0)


def _hgrn_scan_body(qf_ref, vf_ref, zf_ref, qb_ref, vb_ref, zb_ref, lb_ref, s0_ref, of_ref, ob_ref, sfin_ref,
                    st_scr, save_scr, ox_scr, *, n_chunks):
    step = pl.program_id(1)

    @pl.when(step == 0)
    def _():
        st_scr[...] = s0_ref[0]

    save_scr[...] = st_scr[...]
    dirs = ((qf_ref, vf_ref, zf_ref, of_ref, False), (qb_ref, vb_ref, zb_ref, ob_ref, True))
    c = HGRN_CHUNK

    def pair(i, bad):
        work, dots = [], []
        for dr, (q_ref, v_ref, z_ref, o_ref, reverse) in enumerate(dirs):
            base = pl.multiple_of(((n_chunks // 2 - 1 - i) if reverse else i) * 2 * c, 2 * c)
            rows = (pl.ds(base + c, c), pl.ds(base, c)) if reverse else (pl.ds(base, c), pl.ds(base + c, c))
            group = []
            for h in range(A_HEADS):
                sl = slice(h * LANES, (h + 1) * LANES)
                ps = [_hgrn_chunk_prep(q_ref[0, r, sl], v_ref[0, r, sl], z_ref[0, r, sl], lb_ref[:, sl], reverse)
                      for r in rows]
                bad = jnp.maximum(bad, jnp.maximum(ps[0]["bad"], ps[1]["bad"]))
                group.append((dr, h, o_ref, rows, sl, reverse, ps))
            dots += [_hgrn_pair_dots(ps[0], ps[1], st_scr[dr, h]) for dr, h, _, _, _, _, ps in group]
            work += group
        for (dr, h, _, _, _, _, ps), d in zip(work, dots):
            st_scr[dr, h] = (ps[0]["dec"] * ps[1]["dec"]) * st_scr[dr, h] + d[4]
        for (dr, h, o_ref, rows, sl, reverse, ps), d in zip(work, dots):
            o1, o2 = _hgrn_pair_out(ps[0], ps[1], *d[:4], reverse)
            o_ref[0, rows[0], sl] = o1.astype(o_ref.dtype)
            o_ref[0, rows[1], sl] = o2.astype(o_ref.dtype)
        return bad

    bad = lax.fori_loop(0, n_chunks // 2, pair, jnp.zeros((1, LANES), F32), unroll=2)

    @pl.when(jnp.max(bad) > 0.0)
    def _():
        st_scr[...] = save_scr[...]
        for dr, (q_ref, v_ref, z_ref, o_ref, reverse) in enumerate(dirs):
            _hgrn_exact_block(q_ref, v_ref, z_ref, lb_ref, ox_scr, st_scr, dr, reverse, n_chunks * c // SUBLANES)
            o_ref[0] = ox_scr[...].astype(o_ref.dtype)

    @pl.when(step == pl.num_programs(1) - 1)
    def _():
        sfin_ref[0] = st_scr[...]


def _hgrn_scan(proj, lb, s0):
    b, l, _ = proj.shape
    d = A_HEADS * LANES
    t = _row_tile(l, 512)
    nb = l // t
    fwd = lambda blk: pl.BlockSpec((1, t, d), lambda i, s: (i, s, blk))
    bwd = lambda blk: pl.BlockSpec((1, t, d), lambda i, s: (i, nb - 1 - s, blk))
    st_spec = pl.BlockSpec((1,) + s0.shape[1:], lambda i, s: (i, 0, 0, 0, 0))
    return pl.pallas_call(
        functools.partial(_hgrn_scan_body, n_chunks=t // HGRN_CHUNK),
        grid=(b, nb),
        in_specs=[fwd(0), fwd(1), fwd(2), bwd(0), bwd(1), bwd(3),
                  pl.BlockSpec((1, d), lambda i, s: (0, 0)), st_spec],
        out_specs=[fwd(0), bwd(0), st_spec],
        out_shape=[jax.ShapeDtypeStruct((b, l, d), BF16), jax.ShapeDtypeStruct((b, l, d), BF16),
                   jax.ShapeDtypeStruct(s0.shape, F32)],
        scratch_shapes=[pltpu.VMEM(s0.shape[1:], F32), pltpu.VMEM(s0.shape[1:], F32), pltpu.VMEM((t, d), F32)],
        compiler_params=_params("parallel", "arbitrary"),
        name="hgrn_scan",
    )(proj, proj, proj, proj, proj, proj, lb, s0)


def _hgrn2_mixer(x_c, x_l, ng, mod_c, mod_l, w_in, lower, out_g, w_out, g1, want_ctx):
    d = x_l.shape[-1]
    p_c, g_c = _norm_mod_proj(x_c, ng, mod_c[0], mod_c[1], w_in, F32, silu_cols=(2 * d, 3 * d))
    p_l, g_l = _norm_mod_proj(x_l, ng, mod_l[0], mod_l[1], w_in, F32, silu_cols=(2 * d, 3 * d))
    lb = lower.reshape(1, d)
    zero = jnp.zeros((x_l.shape[0], 2, A_HEADS, LANES, LANES), F32)
    ocf, ocb, s_ctx = _hgrn_scan(p_c, lb, zero)
    olf, olb, _ = _hgrn_scan(p_l, lb, s_ctx)
    o_c, o_l = (ocf, ocb), (olf, olb)
    og = out_g.reshape(1, d)
    new_l = _out_call(_out_hgrn_body, [(o_l[0], 0, d), (o_l[1], 0, d), (g_l, 0, d)], [og, w_out],
                      x_l, mod_l[2], g1, "hgrn_out")
    new_c = None
    if want_ctx:
        new_c = _out_call(_out_hgrn_body, [(o_c[0], 0, d), (o_c[1], 0, d), (g_c, 0, d)], [og, w_out],
                          x_c, mod_c[2], g1, "hgrn_out")
    return new_c, new_l


def _softmax_pv(s_list, v_list):
    mx = functools.reduce(jnp.maximum, [jnp.max(s, axis=-1, keepdims=True) for s in s_list])
    ps = [jnp.exp(s - mx) for s in s_list]
    den = functools.reduce(jnp.add, [jnp.sum(p, axis=-1, keepdims=True) for p in ps])
    acc = functools.reduce(jnp.add, [jnp.dot(p.astype(BF16), v, preferred_element_type=F32)
                                     for p, v in zip(ps, v_list)])
    return acc / den


def _na_lat_body(q_ref, k_ref, v_ref, kc_ref, vc_ref, t2_ref, o_ref, *, rows, kh, scale):
    w = GRID_W
    rb = NA_ROW_BLOCK
    nk = kh + rb
    n_rel = 2 * kh - 1
    first = lax.broadcasted_iota(jnp.int32, (rb * w, LANES), 1) < (LANES // 2)
    kc = kc_ref[0]
    vc = vc_ref[0]

    def bias_index(r, ks, m):
        r0 = jnp.clip(r - kh // 2, 0, rows - kh)
        ka = ks + 2 * m
        rel_a = ka - r + (kh - 1)
        in_a = (ka >= r0) & (ka < r0 + kh)
        in_b = (ka + 1 >= r0) & (ka + 1 < r0 + kh)
        both, only_b, only_a = rel_a, (n_rel - 1) + rel_a + 1, (2 * n_rel - 1) + rel_a
        return jnp.where(in_a, jnp.where(in_b, both, only_a), jnp.where(in_b, only_b, 3 * n_rel - 1))

    def block(i, carry):
        rq = i * rb
        ks = jnp.clip(rq - kh // 2, 0, rows - nk)
        q2 = q_ref[0, pl.ds(pl.multiple_of(rq * w, rb * w), rb * w), :] * scale
        k2 = k_ref[0, pl.ds(pl.multiple_of(ks * w, w), nk * w), :]
        v2 = v_ref[0, pl.ds(pl.multiple_of(ks * w, w), nk * w), :]
        outs = []
        for hh in range(2):
            qm = jnp.where(first if hh == 0 else jnp.logical_not(first), q2, jnp.zeros_like(q2))
            bias = jnp.concatenate(
                [jnp.concatenate([t2_ref[hh, bias_index(rq + a, ks, m)] for m in range(nk // 2)], axis=-1)
                 for a in range(rb)], axis=0)
            s_w = lax.dot_general(qm, k2, NT_DIMS, preferred_element_type=F32) + bias
            s_c = lax.dot_general(qm, kc, NT_DIMS, preferred_element_type=F32)
            outs.append(_softmax_pv([s_w, s_c], [v2, vc]))
        o_ref[0, pl.ds(pl.multiple_of(rq * w, rb * w), rb * w), :] = (
            jnp.where(first, outs[0], outs[1]).astype(o_ref.dtype))
        return carry

    lax.fori_loop(0, rows // rb, block, 0, unroll=4)


def _na_ctx_body(q_ref, k_ref, v_ref, o_ref, *, scale):
    n = q_ref.shape[1]
    first = lax.broadcasted_iota(jnp.int32, (n, LANES), 1) < (LANES // 2)
    q2 = q_ref[0] * scale
    k2 = k_ref[0]
    v2 = v_ref[0]
    outs = []
    for hh in range(2):
        qm = jnp.where(first if hh == 0 else jnp.logical_not(first), q2, jnp.zeros_like(q2))
        s = lax.dot_general(qm, k2, NT_DIMS, preferred_element_type=F32)
        outs.append(_softmax_pv([s], [v2]))
    o_ref[0] = jnp.where(first, outs[0], outs[1]).astype(o_ref.dtype)


def _na_bias_table(rpb, kh, kw):
    w = jnp.arange(GRID_W)[:, None]
    c = jnp.arange(GRID_W)[None, :]
    c0 = jnp.clip(w - kw // 2, 0, GRID_W - kw)
    inside = (c >= c0) & (c < c0 + kw)
    pad = GRID_W - kw
    padded = jnp.pad(rpb.astype(F32), ((0, 0), (0, 0), (pad, pad)))
    shifted = jnp.stack([padded[:, :, pad + kw - 1 - q:pad + kw - 1 - q + GRID_W] for q in range(GRID_W)], axis=2)
    t = jnp.where(inside[None, None], shifted, NEG_BIG)
    off = jnp.full_like(t, NEG_BIG)
    pair = lambda a, b: jnp.concatenate([a, b], axis=-1)
    return jnp.concatenate([pair(t[:, :-1], t[:, 1:]), pair(off, t), pair(t, off), pair(off, off)[:, :1]], axis=1)


def _na_mixer(x_c, x_l, ng, mod_c, mod_l, w_qkv, rpb, w_out, g1, want_ctx):
    b, l, d = x_l.shape
    lc = x_c.shape[1]
    heads = rpb.shape[0]
    na_rows, na_cols = (rpb.shape[1] + 1) // 2, (rpb.shape[2] + 1) // 2
    dh = d // heads
    assert 2 * dh == LANES and l % GRID_W == 0
    rows = l // GRID_W
    kh = min(na_rows, rows)
    assert kh == na_rows and (kh + NA_ROW_BLOCK) % 2 == 0
    assert rows % NA_ROW_BLOCK == 0 and rows >= kh + NA_ROW_BLOCK
    scale = dh ** -0.5
    nhp = heads // 2
    qkv_c = _norm_mod_proj(x_c, ng, mod_c[0], mod_c[1], w_qkv, BF16)
    qkv_l = _norm_mod_proj(x_l, ng, mod_l[0], mod_l[1], w_qkv, BF16)
    t2 = _na_bias_table(rpb, kh, na_cols)
    o_l = pl.pallas_call(
        functools.partial(_na_lat_body, rows=rows, kh=kh, scale=scale),
        grid=(b, nhp),
        in_specs=[pl.BlockSpec((1, l, LANES), lambda i, p: (i, 0, p)),
                  pl.BlockSpec((1, l, LANES), lambda i, p: (i, 0, nhp + p)),
                  pl.BlockSpec((1, l, LANES), lambda i, p: (i, 0, 2 * nhp + p)),
                  pl.BlockSpec((1, lc, LANES), lambda i, p: (i, 0, nhp + p)),
                  pl.BlockSpec((1, lc, LANES), lambda i, p: (i, 0, 2 * nhp + p)),
                  pl.BlockSpec((2,) + t2.shape[1:], lambda i, p: (p, 0, 0, 0))],
        out_specs=pl.BlockSpec((1, l, LANES), lambda i, p: (i, 0, p)),
        out_shape=jax.ShapeDtypeStruct((b, l, d), BF16),
        compiler_params=_params("parallel", "parallel"),
        name="na_latent",
    )(qkv_l, qkv_l, qkv_l, qkv_c, qkv_c, t2)
    new_l = _out_call(_out_plain_body, [(o_l, 0, d)], [w_out], x_l, mod_l[2], g1, "na_out")
    new_c = None
    if want_ctx:
        o_c = pl.pallas_call(
            functools.partial(_na_ctx_body, scale=scale),
            grid=(b, nhp),
            in_specs=[pl.BlockSpec((1, lc, LANES), lambda i, p: (i, 0, p)),
                      pl.BlockSpec((1, lc, LANES), lambda i, p: (i, 0, nhp + p)),
                      pl.BlockSpec((1, lc, LANES), lambda i, p: (i, 0, 2 * nhp + p))],
            out_specs=pl.BlockSpec((1, lc, LANES), lambda i, p: (i, 0, p)),
            out_shape=jax.ShapeDtypeStruct((b, lc, d), BF16),
            compiler_params=_params("parallel", "parallel"),
            name="na_context",
        )(qkv_c, qkv_c, qkv_c)
        new_c = _out_call(_out_plain_body, [(o_c, 0, d)], [w_out], x_c, mod_c[2], g1, "na_out")
    return new_c, new_l


S5_GROUP_BLOCK = 8


def _s5_kernel_body(c_ref, w_ref, o_ref):
    for i in range(c_ref.shape[0]):
        o_ref[i] = jnp.dot(c_ref[i], w_ref[i], preferred_element_type=F32, precision=lax.Precision.HIGHEST)


def _s5_impulse(cmat, wmat):
    n = cmat.shape[0]
    gb = S5_GROUP_BLOCK
    return pl.pallas_call(
        _s5_kernel_body,
        grid=(n // gb,),
        in_specs=[pl.BlockSpec((gb,) + cmat.shape[1:], lambda i: (i, 0, 0)),
                  pl.BlockSpec((gb,) + wmat.shape[1:], lambda i: (i, 0, 0))],
        out_specs=pl.BlockSpec((gb, cmat.shape[1], wmat.shape[2]), lambda i: (i, 0, 0)),
        out_shape=jax.ShapeDtypeStruct((n, cmat.shape[1], wmat.shape[2]), F32),
        compiler_params=_params("parallel"),
        name="s5_impulse",
    )(cmat, wmat)


def _s5_operators(lam_re, lam_im, log_dt, b_re, b_im, c_re, c_im, t):
    _, g, p = lam_re.shape
    cg = b_re.shape[-1]
    lam = lax.complex(lam_re.astype(F32), lam_im.astype(F32))
    ldt = lam * jnp.exp(log_dt.astype(F32))[..., None]
    a = jnp.exp(ldt)
    bbar = ((a - 1.0) / lam)[..., None] * lax.complex(b_re.astype(F32), b_im.astype(F32))
    cm = lax.complex(c_re.astype(F32), c_im.astype(F32))
    apow = jnp.exp(ldt[..., None] * jnp.arange(t + 1, dtype=F32))
    w = apow[..., :t, None] * bbar[:, :, :, None, :]
    wmat = jnp.concatenate([jnp.real(w), jnp.imag(w)], axis=2).reshape(2 * g, 2 * p, t * cg)
    cmat = jnp.concatenate([jnp.real(cm), -jnp.imag(cm)], axis=-1).reshape(2 * g, cg, 2 * p)
    k = _s5_impulse(cmat, wmat).reshape(2, g, cg, t, cg)
    k = jnp.transpose(k, (0, 1, 3, 4, 2))
    kfull = jnp.concatenate([k[1, :, :0:-1], (k[0, :, :1] + k[1, :, :1]), k[0, :, 1:]], axis=1)
    lagvec = jnp.transpose(kfull, (0, 2, 1, 3)).reshape(g, cg, (2 * t - 1) * cg).astype(BF16)
    toep = jnp.stack([lagvec[:, :, (t - 1 - s) * cg:(2 * t - 1 - s) * cg] for s in range(t)], axis=1)
    toep = toep.reshape(g, t * cg, t * cg)

    def state_in(wd, flip):
        wd = wd[:, :, ::-1] if flip else wd
        m = jnp.transpose(wd, (0, 2, 3, 1)).reshape(g, t * cg, p)
        return jnp.concatenate([jnp.real(m), jnp.imag(m)], axis=-1)

    def state_out(cd, pw):
        n = cd[:, :, :, None] * pw[:, None, :, :]
        n = jnp.transpose(n, (0, 2, 3, 1)).reshape(g, p, t * cg)
        return jnp.concatenate([jnp.real(n), -jnp.imag(n)], axis=1)

    m_f = state_in(w[0], True)
    m_b = state_in(w[1], False)
    n_f = state_out(cm[0], apow[0][..., 1:])
    n_b = state_out(cm[1], apow[1][..., :0:-1])
    at = apow[..., t]
    a1 = jnp.concatenate([jnp.real(at), jnp.real(at)], axis=-1)
    a2 = jnp.concatenate([-jnp.imag(at), jnp.imag(at)], axis=-1)
    bf = lambda m: m.astype(BF16)
    return dict(toep=bf(toep), m_f=bf(m_f), m_b=bf(m_b), n_f=bf(n_f), n_b=bf(n_b), a1=a1, a2=a2)


def _s5_body(u_ref, dsk_ref, toep_ref, mf_ref, mb_ref, nf_ref, nb_ref, a1_ref, a2_ref, x0_ref,
             y_ref, xfin_ref, z_scr, zs_scr, xin_scr):
    gb, nc = u_ref.shape[1], u_ref.shape[2]
    half = LANES // 2
    ubs = [u_ref[0, gi].astype(BF16) for gi in range(gb)]
    for dr, m_ref in enumerate((mf_ref, mb_ref)):
        z = jnp.stack([jnp.dot(ubs[gi], m_ref[gi], preferred_element_type=F32) for gi in range(gb)])
        z_scr[dr] = jnp.swapaxes(z, 0, 1)

    swap = lambda t: pltpu.roll(t, half, t.ndim - 1)
    for dr in range(2):
        zs_scr[dr] = swap(z_scr[dr].reshape(nc * gb, LANES)).reshape(nc, gb, LANES)

    coef = [(a1_ref[dr], a2_ref[dr]) for dr in range(2)]

    def scan(j, xs):
        out = []
        for dr in range(2):
            jj = j if dr == 0 else nc - 1 - j
            x, xw = xs[2 * dr], xs[2 * dr + 1]
            a1, a2 = coef[dr]
            xin_scr[dr, jj] = x
            out.append(a1 * x + a2 * xw + z_scr[dr, jj])
            out.append(a1 * xw - a2 * x + zs_scr[dr, jj])
        return tuple(out)

    x0f, x0b = x0_ref[0, 0], x0_ref[0, 1]
    xf, _, xb, _ = lax.fori_loop(0, nc, scan, (x0f, swap(x0f), x0b, swap(x0b)), unroll=4)
    xfin_ref[0, 0] = xf
    xfin_ref[0, 1] = xb
    xin = [jnp.swapaxes(xin_scr[dr], 0, 1).astype(BF16) for dr in range(2)]
    for gi in range(gb):
        u = u_ref[0, gi]
        y = jnp.dot(ubs[gi], toep_ref[gi], preferred_element_type=F32) + u * dsk_ref[gi]
        y = y + jnp.dot(xin[0][gi], nf_ref[gi], preferred_element_type=F32)
        y = y + jnp.dot(xin[1][gi], nb_ref[gi], preferred_element_type=F32)
        y_ref[0, gi] = y


def _piece_transpose(src, piece_of, cg):
    n = len(src)
    out = [None] * n
    for s in range(n):
        c = src[s % n]
        for p in range(1, n):
            c = jnp.where(piece_of == p, src[(p + s) % n], c)
        r = c if s == 0 else pltpu.roll(c, s * cg, 1)
        for p in range(n):
            out[p] = r if s == 0 else jnp.where(piece_of == (p + s) % n, r, out[p])
    return out


def _to_chunks(h_scr, o_ref, t, cg):
    ncb = h_scr.shape[1] // t
    pieces = LANES // cg
    piece_of = lax.broadcasted_iota(jnp.int32, (ncb, LANES), 1) // cg

    def lane_block(lb, carry):
        for tq in range(t // pieces):
            steps = [h_scr[lb, pl.ds(tq * pieces + tr, ncb, stride=t), :] for tr in range(pieces)]
            for gl, col in enumerate(_piece_transpose(steps, piece_of, cg)):
                o_ref[0, lb * pieces + gl, :, tq * LANES:(tq + 1) * LANES] = col
        return carry

    lax.fori_loop(0, h_scr.shape[0], lane_block, 0, unroll=2)


def _from_chunks(y_ref, y_scr, t, cg):
    ncb = y_ref.shape[2]
    pieces = LANES // cg
    piece_of = lax.broadcasted_iota(jnp.int32, (ncb, LANES), 1) // cg

    def lane_block(lb, carry):
        for tq in range(t // pieces):
            cols = [y_ref[0, lb * pieces + gl, :, tq * LANES:(tq + 1) * LANES] for gl in range(pieces)]
            for tr, step in enumerate(_piece_transpose(cols, piece_of, cg)):
                y_scr[lb, pl.ds(tq * pieces + tr, ncb, stride=t), :] = step
        return carry

    lax.fori_loop(0, y_scr.shape[0], lane_block, 0, unroll=2)


def _norm_chunks_body(x_ref, g_ref, sh_ref, sc_ref, o_ref, h_scr, *, t, cg):
    h = _norm_mod(x_ref[...], g_ref[...], sh_ref[0], sc_ref[0])
    for lb in range(h_scr.shape[0]):
        h_scr[lb] = h[:, lb * LANES:(lb + 1) * LANES]
    _to_chunks(h_scr, o_ref, t, cg)


def _norm_mod_chunks(x, g, sh, sc, groups, t):
    b, l, d = x.shape
    cg = d // groups
    tm = _row_tile(l, 512)
    per = l // tm
    assert tm % t == 0 and (tm // t) % SUBLANES == 0 and LANES % cg == 0 and t % (LANES // cg) == 0
    return pl.pallas_call(
        functools.partial(_norm_chunks_body, t=t, cg=cg),
        grid=(b * per,),
        in_specs=[pl.BlockSpec((tm, d), lambda i: (i, 0)),
                  pl.BlockSpec((1, d), lambda i: (0, 0)),
                  _mod_spec(sh, per), _mod_spec(sc, per)],
        out_specs=pl.BlockSpec((1, groups, tm // t, t * cg), lambda i: (i // per, 0, i % per, 0)),
        out_shape=jax.ShapeDtypeStruct((b, groups, l // t, t * cg), F32),
        scratch_shapes=[pltpu.VMEM((d // LANES, tm, LANES), F32)],
        compiler_params=_params("parallel"),
        name="norm_mod_chunks",
    )(x.reshape(b * l, d), g.reshape(1, d), sh.table, sc.table)


def _glu_chunks_body(y_ref, w_ref, res_ref, gate_ref, g1_ref, o_ref, y_scr, *, t, cg):
    _from_chunks(y_ref, y_scr, t, cg)
    d = o_ref.shape[-1]
    y = jnp.concatenate([y_scr[lb] for lb in range(y_scr.shape[0])], axis=-1)
    a = _gelu_tanh(y).astype(BF16)
    ag = jnp.dot(a, w_ref[...], preferred_element_type=F32)
    o_ref[...] = _residual(res_ref, gate_ref, g1_ref, ag[:, :d] * jax.nn.sigmoid(ag[:, d:]))


def _glu_out_chunks(y, w_glu, res, gate, g1, t):
    b, l, d = res.shape
    groups = y.shape[1]
    cg = d // groups
    tm = _row_tile(l, 512)
    per = l // tm
    out = pl.pallas_call(
        functools.partial(_glu_chunks_body, t=t, cg=cg),
        grid=(b * per,),
        in_specs=[pl.BlockSpec((1, groups, tm // t, t * cg), lambda i: (i // per, 0, i % per, 0)),
                  _layer_spec(w_glu, w_glu.shape, lambda i: (0, 0)),
                  pl.BlockSpec((tm, d), lambda i: (i, 0)),
                  _mod_spec(gate, per),
                  pl.BlockSpec((1, d), lambda i: (0, 0))],
        out_specs=pl.BlockSpec((tm, d), lambda i: (i, 0)),
        out_shape=jax.ShapeDtypeStruct((b * l, d), F32),
        scratch_shapes=[pltpu.VMEM((d // LANES, tm, LANES), F32)],
        compiler_params=_params("parallel"),
        name="s5_glu_out",
    )(y, w_glu.stack, res.reshape(b * l, d), gate.table, g1.reshape(1, d))
    return out.reshape(b, l, d)


def _s5_scan(u, ops, dsk, x0):
    b, g, nc, tc = u.shape
    gb = S5_GROUP_BLOCK
    per_g = lambda shape: pl.BlockSpec((gb,) + shape, lambda j, i: (j,) + (0,) * len(shape))
    y, xfin = pl.pallas_call(
        _s5_body,
        grid=(g // gb, b),
        in_specs=[pl.BlockSpec((1, gb, nc, tc), lambda j, i: (i, j, 0, 0)),
                  per_g((1, tc)), per_g((tc, tc)), per_g((tc, LANES)), per_g((tc, LANES)),
                  per_g((LANES, tc)), per_g((LANES, tc)),
                  pl.BlockSpec((2, gb, LANES), lambda j, i: (0, j, 0)),
                  pl.BlockSpec((2, gb, LANES), lambda j, i: (0, j, 0)),
                  pl.BlockSpec((1, 2, gb, LANES), lambda j, i: (i, 0, j, 0))],
        out_specs=[pl.BlockSpec((1, gb, nc, tc), lambda j, i: (i, j, 0, 0)),
                   pl.BlockSpec((1, 2, gb, LANES), lambda j, i: (i, 0, j, 0))],
        out_shape=[jax.ShapeDtypeStruct((b, g, nc, tc), F32), jax.ShapeDtypeStruct((b, 2, g, LANES), F32)],
        scratch_shapes=[pltpu.VMEM((2, nc, gb, LANES), F32)] * 3,
        compiler_params=_params("parallel", "parallel"),
        name="s5_scan",
    )(u, dsk, ops["toep"], ops["m_f"], ops["m_b"], ops["n_f"], ops["n_b"], ops["a1"], ops["a2"], x0)
    return y, xfin


def _s5_mixer(x_c, x_l, ng, mod_c, mod_l, lam_re, lam_im, log_dt, b_re, b_im, c_re, c_im, d_skip, w_glu, g1,
              want_ctx):
    b, l, d = x_l.shape
    g, p = lam_re.shape[1], lam_re.shape[2]
    cg = d // g
    assert 2 * p == LANES and g % S5_GROUP_BLOCK == 0
    ops = _s5_operators(lam_re, lam_im, log_dt, b_re, b_im, c_re, c_im, S5_CHUNK)
    dsk = jnp.tile(d_skip.astype(F32).reshape(g, 1, cg), (1, 1, S5_CHUNK))
    u_c = _norm_mod_chunks(x_c, ng, mod_c[0], mod_c[1], g, S5_CHUNK)
    u_l = _norm_mod_chunks(x_l, ng, mod_l[0], mod_l[1], g, S5_CHUNK)
    y_c, x_ctx = _s5_scan(u_c, ops, dsk, jnp.zeros((b, 2, g, LANES), F32))
    y_l, _ = _s5_scan(u_l, ops, dsk, x_ctx)
    new_l = _glu_out_chunks(y_l, w_glu, x_l, mod_l[2], g1, S5_CHUNK)
    new_c = _glu_out_chunks(y_c, w_glu, x_c, mod_c[2], g1, S5_CHUNK) if want_ctx else None
    return new_c, new_l


FFN_HALO = 16
FFN_TN = 256
FFN_TILES_PER_STEP = 11
FFN_ROW_SPLIT = 256


def _ffn_body(xp_ref, x_ref, xn_ref, g2_ref, sh_ref, sc_ref, *rest, per, n_tiles, tps):
    w_refs = rest[:2 * tps]
    cw_ref, cb_ref, wo_ref, gate_ref, g3_ref, o_ref, h_scr, gat_scr = rest[2 * tps:]
    i, j = pl.program_id(0), pl.program_id(1)
    tm = x_ref.shape[0]
    hl = FFN_HALO
    f = wo_ref.shape[0]
    n_steps = pl.cdiv(n_tiles, tps)
    when = (lambda cond: lambda fn: fn()) if n_steps == 1 else pl.when

    @when(j == 0)
    def _():
        g, sh, sc = g2_ref[...], sh_ref[0], sc_ref[0]
        keep_p = ((i % per) != 0).astype(F32)
        keep_n = ((i % per) != per - 1).astype(F32)
        h_scr[0:hl] = (_norm_mod(xp_ref[...], g, sh, sc) * keep_p).astype(BF16)
        h_scr[hl:hl + tm] = _norm_mod(x_ref[...], g, sh, sc).astype(BF16)
        h_scr[hl + tm:] = (_norm_mod(xn_ref[...], g, sh, sc) * keep_n).astype(BF16)

    sub = FFN_ROW_SPLIT if tm % FFN_ROW_SPLIT == 0 else tm
    ns = tm // sub
    edge = SUBLANES
    tn = w_refs[0].shape[1]

    def pieces(w_ref):
        out = []
        for s in range(ns):
            lo = hl + s * sub - (hl if s == 0 else 0)
            hi = hl + (s + 1) * sub + (hl if s == ns - 1 else 0)
            out.append(jnp.dot(h_scr[lo:hi], w_ref[...], preferred_element_type=F32))
        return out

    def conv(p, s, cw, cb):
        base = hl if s == 0 else 0
        left = p[s][hl - edge:hl] if s == 0 else p[s - 1][-edge:]
        right = p[s][base + sub:base + sub + edge] if s == ns - 1 else p[s + 1][:edge]
        ext = jnp.concatenate([left, p[s][base:base + sub], right], axis=0)
        um = pltpu.roll(ext, 1, 0)[edge:edge + sub]
        up = pltpu.roll(ext, sub + 2 * edge - 1, 0)[edge:edge + sub]
        return cb + um * cw[0:1] + ext[edge:edge + sub] * cw[1:2] + up * cw[2:3]

    def tile(k):
        col = k * tn if n_steps == 1 else pl.multiple_of((j * tps + k) * tn, tn)
        cols_a, cols_v = pl.ds(col, tn), pl.ds(pl.multiple_of(f + col, tn), tn)
        pa, pv = pieces(w_refs[2 * k]), pieces(w_refs[2 * k + 1])
        for s in range(ns):
            a = conv(pa, s, cw_ref[:, cols_a], cb_ref[:, cols_a])
            v = conv(pv, s, cw_ref[:, cols_v], cb_ref[:, cols_v])
            gat_scr[s * sub:(s + 1) * sub, cols_a] = (a * jax.nn.sigmoid(a) * v).astype(BF16)

    always = n_tiles - tps * (n_steps - 1)
    for k in range(tps):
        if k < always:
            tile(k)
        else:
            pl.when(j * tps + k < n_tiles)(functools.partial(tile, k))

    @when(j == n_steps - 1)
    def _():
        y = jnp.dot(gat_scr[...], wo_ref[...], preferred_element_type=F32)
        o_ref[...] = x_ref[...] + gate_ref[0] * _rms(y, g3_ref[...])


def _conv_ffn_block(x, ng2, mod, w_in, conv_w, conv_b, w_out, ng3):
    b, l, d = x.shape
    f = w_out.shape[0]
    tn, hl = FFN_TN, FFN_HALO
    assert f % tn == 0 and conv_w.shape[0] == 3 and w_in.shape[1] == 2 * f
    nj = f // tn
    tm = _row_tile(l, 1024)
    per = l // tm
    hb = tm // hl
    last_hb = b * l // hl - 1
    cb = conv_b.reshape(1, 2 * f)
    x2 = x.reshape(b * l, d)
    tps = min(FFN_TILES_PER_STEP, nj)
    once = dict(pipeline_mode=pl.Buffered(1)) if tps == nj else {}
    w_specs = []
    for k in range(tps):
        w_specs.append(_layer_spec(w_in, (d, tn), lambda i, j, k=k: (0, jnp.minimum(j * tps + k, nj - 1)), **once))
        w_specs.append(_layer_spec(w_in, (d, tn), lambda i, j, k=k: (0, nj + jnp.minimum(j * tps + k, nj - 1)),
                                   **once))
    whole = lambda arr: pl.BlockSpec(arr.shape, lambda i, j: (0,) * arr.ndim)
    out = pl.pallas_call(
        functools.partial(_ffn_body, per=per, n_tiles=nj, tps=tps),
        grid=(b * per, pl.cdiv(nj, tps)),
        in_specs=[pl.BlockSpec((hl, d), lambda i, j: (jnp.maximum(i * hb - 1, 0), 0)),
                  pl.BlockSpec((tm, d), lambda i, j: (i, 0)),
                  pl.BlockSpec((hl, d), lambda i, j: (jnp.minimum((i + 1) * hb, last_hb), 0)),
                  pl.BlockSpec((1, d), lambda i, j: (0, 0)),
                  _mod_spec(mod[3], per), _mod_spec(mod[4], per),
                  *w_specs, whole(conv_w), whole(cb),
                  _layer_spec(w_out, (f, d), lambda i, j: (0, 0), pipeline_mode=pl.Buffered(1)),
                  _mod_spec(mod[5], per),
                  pl.BlockSpec((1, d), lambda i, j: (0, 0))],
        out_specs=pl.BlockSpec((tm, d), lambda i, j: (i, 0)),
        out_shape=jax.ShapeDtypeStruct((b * l, d), F32),
        scratch_shapes=[pltpu.VMEM((tm + 2 * hl, d), BF16), pltpu.VMEM((tm, f), BF16)],
        compiler_params=_params("parallel", "arbitrary"),
        name="conv_ffn",
    )(x2, x2, x2, ng2.reshape(1, d), mod[3].table, mod[4].table, *([w_in.stack] * (2 * tps)), conv_w, cb,
      w_out.stack,
      mod[5].table, ng3.reshape(1, d))
    return out.reshape(b, l, d)


def kernel(x, c, ctx, c_ctx, w_mod, b_mod, norm_g, a_w_in, a_lower_logits, a_out_g, a_w_out, b_w_qkv, b_rpb,
           b_w_out, c_lam_re, c_lam_im, c_log_dt, c_b_re, c_b_im, c_c_re, c_c_im, c_d, c_w_glu, f_w_in,
           f_conv_w, f_conv_b, f_w_out):
    bsz, _, d = x.shape
    depth = w_mod.shape[0]
    p = jax.nn.softmax(a_lower_logits.astype(F32), axis=0)
    lower = jnp.cumsum(p, axis=0) - p[0]
    n_rows = -(-(bsz + 1) // SUBLANES) * SUBLANES
    c_rows = jnp.zeros((n_rows, d), F32).at[:bsz].set(c).at[bsz].set(c_ctx)
    table = _modulation(c_rows, w_mod, b_mod).reshape(depth * n_rows * N_MOD, 1, d)
    wa_in, wa_out, wb_qkv, wb_out, wc_glu, wf_in, wf_out = (
        w.astype(BF16) for w in (a_w_in, a_w_out, b_w_qkv, b_w_out, c_w_glu, f_w_in, f_w_out))
    lat, cx = x, ctx
    for i in range(depth):
        kind, j = i % N_MIXERS, i // N_MIXERS
        last = i == depth - 1
        mod_l = [ModRow(table, i * n_rows * N_MOD + k, N_MOD) for k in range(N_MOD)]
        mod_c = [ModRow(table, (i * n_rows + bsz) * N_MOD + k, 0) for k in range(N_MOD)]
        ng = norm_g[i]
        if kind == 0:
            cx1, lat = _hgrn2_mixer(cx, lat, ng[0], mod_c, mod_l, LayerOf(wa_in, j), lower[i], a_out_g[j],
                                    LayerOf(wa_out, j), ng[1], not last)
        elif kind == 1:
            cx1, lat = _na_mixer(cx, lat, ng[0], mod_c, mod_l, LayerOf(wb_qkv, j), b_rpb[j], LayerOf(wb_out, j),
                                 ng[1], not last)
        else:
            cx1, lat = _s5_mixer(cx, lat, ng[0], mod_c, mod_l, c_lam_re[j], c_lam_im[j], c_log_dt[j], c_b_re[j],
                                 c_b_im[j], c_c_re[j], c_c_im[j], c_d[j], LayerOf(wc_glu, j), ng[1], not last)
        ffn = (LayerOf(wf_in, i), f_conv_w[i], f_conv_b[i], LayerOf(wf_out, i), ng[3])
        lat = _conv_ffn_block(lat, ng[2], mod_l, *ffn)
        if not last:
            cx = _conv_ffn_block(cx1, ng[2], mod_c, *ffn)
    return lat
```

```python
import functools
from typing import NamedTuple

import jax
import jax.numpy as jnp
from jax import lax
from jax.experimental import pallas as pl
from jax.experimental.pallas import tpu as pltpu

F32 = jnp.float32
BF16 = jnp.bfloat16

EPS = 1e-6
F_MIN = 1e-30
N_MOD = 6
N_MIXERS = 3
A_HEADS = 8
GRID_W = 64
S5_CHUNK = 16
NEG_BIG = -1e30
NA_ROW_BLOCK = 4
SUBLANES = 8
LANES = 128
VMEM_LIMIT_BYTES = 56 * 1024 * 1024

NT_DIMS = (((1,), (1,)), ((), ()))
TN_DIMS = (((0,), (0,)), ((), ()))


def _params(*sem):
    return pltpu.CompilerParams(dimension_semantics=sem, vmem_limit_bytes=VMEM_LIMIT_BYTES)


def _row_tile(n, want):
    t = min(n, want)
    assert n % t == 0, (n, t)
    return t


def _rms(y, g):
    return y * lax.rsqrt(jnp.mean(y * y, axis=-1, keepdims=True) + EPS) * g


def _mod_body(c_ref, w_ref, b_ref, o_ref):
    c = c_ref[...]
    s = (c * jax.nn.sigmoid(c)).astype(BF16)
    o_ref[0] = jnp.dot(s, w_ref[0].astype(BF16), preferred_element_type=F32) + b_ref[0]


def _modulation(c_rows, w_mod, b_mod):
    depth, d, n = w_mod.shape
    r = c_rows.shape[0]
    tn = n // 4
    return pl.pallas_call(
        _mod_body,
        grid=(depth, n // tn),
        in_specs=[pl.BlockSpec((r, d), lambda i, j: (0, 0)),
                  pl.BlockSpec((1, d, tn), lambda i, j: (i, 0, j)),
                  pl.BlockSpec((1, 1, tn), lambda i, j: (i, 0, j))],
        out_specs=pl.BlockSpec((1, r, tn), lambda i, j: (i, 0, j)),
        out_shape=jax.ShapeDtypeStruct((depth, r, n), F32),
        compiler_params=_params("parallel", "parallel"),
        name="adaln_mod",
    )(c_rows, w_mod, b_mod.reshape(depth, 1, n))


def _norm_mod(x, g, sh, sc):
    return _rms(x, g) * (1.0 + sc) + sh


class ModRow(NamedTuple):
    table: jax.Array
    first: int
    step: int


def _mod_spec(m, per):
    return pl.BlockSpec((1, 1, m.table.shape[-1]), lambda *idx: (m.first + m.step * (idx[0] // per), 0, 0))


class LayerOf(NamedTuple):
    stack: jax.Array
    layer: int

    @property
    def shape(self):
        return self.stack.shape[1:]


def _layer_spec(w, block, index_map, **kw):
    return pl.BlockSpec((None,) + tuple(block), lambda *idx: (w.layer,) + tuple(index_map(*idx)), **kw)


def _proj_body(x_ref, g_ref, sh_ref, sc_ref, w_ref, o_ref):
    h = _norm_mod(x_ref[...], g_ref[...], sh_ref[0], sc_ref[0]).astype(BF16)
    o_ref[...] = jnp.dot(h, w_ref[...], preferred_element_type=F32).astype(o_ref.dtype)


def _proj_silu_body(x_ref, g_ref, sh_ref, sc_ref, w_ref, o_ref, os_ref, *, lo, hi):
    h = _norm_mod(x_ref[...], g_ref[...], sh_ref[0], sc_ref[0]).astype(BF16)
    y = jnp.dot(h, w_ref[...], preferred_element_type=F32)
    o_ref[:, :lo] = y[:, :lo]
    o_ref[:, lo:] = y[:, hi:]
    gp = y[:, lo:hi]
    os_ref[...] = (gp * jax.nn.sigmoid(gp)).astype(os_ref.dtype)


def _norm_mod_proj(x, g, sh, sc, w, out_dtype, silu_cols=None):
    b, l, d = x.shape
    n = w.shape[1]
    tm = _row_tile(l, 512)
    per = l // tm
    body, widths, dtypes = _proj_body, [n], [out_dtype]
    if silu_cols is not None:
        lo, hi = silu_cols
        body, widths, dtypes = functools.partial(_proj_silu_body, lo=lo, hi=hi), [n - (hi - lo), hi - lo], [out_dtype, BF16]
    outs = pl.pallas_call(
        body,
        grid=(b * per,),
        in_specs=[pl.BlockSpec((tm, d), lambda i: (i, 0)),
                  pl.BlockSpec((1, d), lambda i: (0, 0)),
                  _mod_spec(sh, per), _mod_spec(sc, per),
                  _layer_spec(w, (d, n), lambda i: (0, 0), pipeline_mode=pl.Buffered(1))],
        out_specs=[pl.BlockSpec((tm, wd), lambda i: (i, 0)) for wd in widths],
        out_shape=[jax.ShapeDtypeStruct((b * l, wd), dt) for wd, dt in zip(widths, dtypes)],
        compiler_params=_params("parallel"),
        name="norm_mod_proj",
    )(x.reshape(b * l, d), g.reshape(1, d), sh.table, sc.table, w.stack)
    outs = [o.reshape(b, l, o.shape[-1]) for o in outs]
    return outs[0] if silu_cols is None else tuple(outs)


def _residual(res_ref, gate_ref, g1_ref, y):
    return res_ref[...] + gate_ref[0] * _rms(y, g1_ref[...])


def _out_plain_body(a_ref, w_ref, res_ref, gate_ref, g1_ref, o_ref):
    y = jnp.dot(a_ref[...], w_ref[...], preferred_element_type=F32)
    o_ref[...] = _residual(res_ref, gate_ref, g1_ref, y)


def _out_hgrn_body(of_ref, ob_ref, gp_ref, og_ref, w_ref, res_ref, gate_ref, g1_ref, o_ref):
    o = of_ref[...].astype(F32) + ob_ref[...].astype(F32)
    parts = []
    for h in range(A_HEADS):
        oh = o[:, h * LANES:(h + 1) * LANES]
        parts.append(oh * lax.rsqrt(jnp.mean(oh * oh, axis=-1, keepdims=True) + EPS))
    a = jnp.concatenate(parts, axis=-1) * og_ref[...] * gp_ref[...].astype(F32)
    y = jnp.dot(a.astype(BF16), w_ref[...], preferred_element_type=F32)
    o_ref[...] = _residual(res_ref, gate_ref, g1_ref, y)


def _gelu_tanh(y):
    return 0.5 * y * (1.0 + jnp.tanh(0.7978845608028654 * (y + 0.044715 * (y * y * y))))


def _out_call(body, row_inputs, const_inputs, res, gate, g1, name):
    b, l, d = res.shape
    tm = _row_tile(l, 1024)
    per = l // tm
    in_specs, args = [], []
    for arr, blk, width in row_inputs:
        in_specs.append(pl.BlockSpec((tm, width), lambda i, blk=blk: (i, blk)))
        args.append(arr.reshape(b * l, arr.shape[-1]))
    for arr in const_inputs:
        if isinstance(arr, LayerOf):
            in_specs.append(_layer_spec(arr, arr.shape, lambda i, nd=len(arr.shape): (0,) * nd))
            args.append(arr.stack)
        else:
            in_specs.append(pl.BlockSpec(arr.shape, lambda i, nd=arr.ndim: (0,) * nd))
            args.append(arr)
    in_specs += [pl.BlockSpec((tm, d), lambda i: (i, 0)),
                 _mod_spec(gate, per),
                 pl.BlockSpec((1, d), lambda i: (0, 0))]
    args += [res.reshape(b * l, d), gate.table, g1.reshape(1, d)]
    out = pl.pallas_call(
        body,
        grid=(b * per,),
        in_specs=in_specs,
        out_specs=pl.BlockSpec((tm, d), lambda i: (i, 0)),
        out_shape=jax.ShapeDtypeStruct((b * l, d), F32),
        compiler_params=_params("parallel"),
        name=name,
    )(*args)
    return out.reshape(b, l, d)


HGRN_CHUNK = 32
HGRN_SAFE_MIN = 1e-30
HGRN_Q_HEADROOM = 1e37


def _hgrn_gates(z, lb):
    f = lb + (1.0 - lb) * jax.nn.sigmoid(z)
    return jnp.maximum(f, F_MIN), 1.0 - f


def _hgrn_chunk_prep(q, v, z, lb, reverse):
    c = q.shape[0]
    half = c // 2
    fm, kk = _hgrn_gates(z, lb)
    pos = lax.broadcasted_iota(jnp.int32, (c, LANES), 0)
    if reverse:
        pos = (c - 1) - pos
    second = pos >= half
    hpos = jnp.where(second, pos - half, pos)
    ph = fm
    j = 1
    while j < half:
        ph = ph * jnp.where(hpos >= j, pltpu.roll(ph, (c - j) if reverse else j, 0), 1.0)
        j *= 2
    a_last = ph[half:half + 1] if reverse else ph[half - 1:half]
    r_last = ph[0:1] if reverse else ph[c - 1:c]
    p_mid = a_last
    p_last = a_last * r_last
    pp = jnp.where(second, ph * a_last, ph)
    r = jnp.where(second, ph, ph * (1.0 / a_last))
    kd = kk * (1.0 / r)
    q_max = jnp.max(jnp.abs(q), axis=0, keepdims=True)
    ok = jnp.where(p_mid >= HGRN_SAFE_MIN, r_last, 0.0) >= HGRN_SAFE_MIN
    ok = jnp.where(ok, p_mid * HGRN_Q_HEADROOM, -1.0) >= q_max
    return dict(qr=(q * r).astype(BF16), kd=kd.astype(BF16), qp=q * pp, kl=kd * r_last, vb=v.astype(BF16),
                dec=p_last, bad=jnp.where(ok, 0.0, 1.0))


def _hgrn_pair_dots(p1, p2, st):
    bf = lambda t: t.astype(BF16)
    sc1 = lax.dot_general(p1["qr"], p1["kd"], NT_DIMS, preferred_element_type=F32)
    sc2 = lax.dot_general(p2["qr"], p2["kd"], NT_DIMS, preferred_element_type=F32)
    cross = lax.dot_general(bf(p2["qp"]), bf(p1["kl"]), NT_DIMS, preferred_element_type=F32)
    q_all = jnp.concatenate([bf(p1["qp"]), bf(p2["qp"] * p1["dec"])], axis=0)
    o_st = lax.dot_general(q_all, st.astype(BF16), NT_DIMS, preferred_element_type=F32)
    k_all = jnp.concatenate([bf(p1["kl"] * p2["dec"]), bf(p2["kl"])], axis=0)
    u = lax.dot_general(jnp.concatenate([p1["vb"], p2["vb"]], axis=0), k_all, TN_DIMS, preferred_element_type=F32)
    return sc1, sc2, cross, o_st, u


def _hgrn_pair_out(p1, p2, sc1, sc2, cross, o_st, reverse):
    c = sc1.shape[0]
    row = lax.broadcasted_iota(jnp.int32, (c, c), 0)
    col = lax.broadcasted_iota(jnp.int32, (c, c), 1)
    keep = (col >= row) if reverse else (col <= row)
    tri = lambda sc: jnp.where(keep, sc, 0.0).astype(BF16)
    o1 = jnp.dot(tri(sc1), p1["vb"], preferred_element_type=F32) + o_st[:c]
    o2 = (jnp.dot(tri(sc2), p2["vb"], preferred_element_type=F32)
          + jnp.dot(cross.astype(BF16), p1["vb"], preferred_element_type=F32) + o_st[c:])
    return o1, o2


def _hgrn_exact_block(q_ref, v_ref, z_ref, lb_ref, o_scr, st_scr, dr, reverse, n_tiles):
    pos = lax.broadcasted_iota(jnp.int32, (SUBLANES, LANES), 0)
    if reverse:
        pos = (SUBLANES - 1) - pos

    def earlier(x, j):
        if j == 0:
            return x
        return pltpu.roll(x, (SUBLANES - j) if reverse else j, 0)

    def later(x, j):
        return pltpu.roll(x, j if reverse else (SUBLANES - j), 0)

    def tile(i, carry):
        ti = (n_tiles - 1 - i) if reverse else i
        r0 = pl.multiple_of(ti * SUBLANES, SUBLANES)
        for h in range(A_HEADS):
            sl = slice(h * LANES, (h + 1) * LANES)
            lb = lb_ref[:, sl]
            z = z_ref[0, pl.ds(r0, SUBLANES), sl]
            q = q_ref[0, pl.ds(r0, SUBLANES), sl]
            v = v_ref[0, pl.ds(r0, SUBLANES), sl]
            fm, kk = _hgrn_gates(z, lb)
            pp = fm
            for j in (1, 2, 4):
                pp = pp * jnp.where(pos >= j, earlier(pp, j), 1.0)
            qq = jnp.where(pos <= SUBLANES - 2, later(fm, 1), 1.0)
            for j in (1, 2, 4):
                qq = qq * jnp.where(pos <= SUBLANES - 1 - j, later(qq, j), 1.0)
            dec = pp[0:1] if reverse else pp[SUBLANES - 1:SUBLANES]
            st = st_scr[dr, h]
            o = lax.dot_general((q * pp).astype(BF16), st.astype(BF16), NT_DIMS,
                                preferred_element_type=F32)
            g = fm
            for d in range(SUBLANES):
                if d == 0:
                    e = q * kk
                else:
                    if d > 1:
                        g = g * earlier(fm, d - 1)
                    e = q * g * jnp.where(pos >= d, earlier(kk, d), 0.0)
                o = o + jnp.sum(e, axis=-1, keepdims=True) * earlier(v, d)
            o_scr[pl.ds(r0, SUBLANES), sl] = o
            u = lax.dot_general(v.astype(BF16), (kk * qq).astype(BF16), TN_DIMS,
                                preferred_element_type=F32)
            st_scr[dr, h] = dec * st + u
        return carry

    lax.fori_loop(0, n_tiles, tile, 0)


def _hgrn_scan_body(qf_ref, vf_ref, zf_ref, qb_ref, vb_ref, zb_ref, lb_ref, s0_ref, of_ref, ob_ref, sfin_ref,
                    st_scr, save_scr, ox_scr, *, n_chunks):
    step = pl.program_id(1)

    @pl.when(step == 0)
    def _():
        st_scr[...] = s0_ref[0]

    save_scr[...] = st_scr[...]
    dirs = ((qf_ref, vf_ref, zf_ref, of_ref, False), (qb_ref, vb_ref, zb_ref, ob_ref, True))
    c = HGRN_CHUNK

    def pair(i, bad):
        work, dots = [], []
        for dr, (q_ref, v_ref, z_ref, o_ref, reverse) in enumerate(dirs):
            base = pl.multiple_of(((n_chunks // 2 - 1 - i) if reverse else i) * 2 * c, 2 * c)
            rows = (pl.ds(base + c, c), pl.ds(base, c)) if reverse else (pl.ds(base, c), pl.ds(base + c, c))
            group = []
            for h in range(A_HEADS):
                sl = slice(h * LANES, (h + 1) * LANES)
                ps = [_hgrn_chunk_prep(q_ref[0, r, sl], v_ref[0, r, sl], z_ref[0, r, sl], lb_ref[:, sl], reverse)
                      for r in rows]
                bad = jnp.maximum(bad, jnp.maximum(ps[0]["bad"], ps[1]["bad"]))
                group.append((dr, h, o_ref, rows, sl, reverse, ps))
            dots += [_hgrn_pair_dots(ps[0], ps[1], st_scr[dr, h]) for dr, h, _, _, _, _, ps in group]
            work += group
        for (dr, h, _, _, _, _, ps), d in zip(work, dots):
            st_scr[dr, h] = (ps[0]["dec"] * ps[1]["dec"]) * st_scr[dr, h] + d[4]
        for (dr, h, o_ref, rows, sl, reverse, ps), d in zip(work, dots):
            o1, o2 = _hgrn_pair_out(ps[0], ps[1], *d[:4], reverse)
            o_ref[0, rows[0], sl] = o1.astype(o_ref.dtype)
            o_ref[0, rows[1], sl] = o2.astype(o_ref.dtype)
        return bad

    bad = lax.fori_loop(0, n_chunks // 2, pair, jnp.zeros((1, LANES), F32), unroll=4)

    @pl.when(jnp.max(bad) > 0.0)
    def _():
        st_scr[...] = save_scr[...]
        for dr, (q_ref, v_ref, z_ref, o_ref, reverse) in enumerate(dirs):
            _hgrn_exact_block(q_ref, v_ref, z_ref, lb_ref, ox_scr, st_scr, dr, reverse, n_chunks * c // SUBLANES)
            o_ref[0] = ox_scr[...].astype(o_ref.dtype)

    @pl.when(step == pl.num_programs(1) - 1)
    def _():
        sfin_ref[0] = st_scr[...]


def _hgrn_scan(proj, lb, s0):
    b, l, _ = proj.shape
    d = A_HEADS * LANES
    t = _row_tile(l, 512)
    nb = l // t
    fwd = lambda blk: pl.BlockSpec((1, t, d), lambda i, s: (i, s, blk))
    bwd = lambda blk: pl.BlockSpec((1, t, d), lambda i, s: (i, nb - 1 - s, blk))
    st_spec = pl.BlockSpec((1,) + s0.shape[1:], lambda i, s: (i, 0, 0, 0, 0))
    return pl.pallas_call(
        functools.partial(_hgrn_scan_body, n_chunks=t // HGRN_CHUNK),
        grid=(b, nb),
        in_specs=[fwd(0), fwd(1), fwd(2), bwd(0), bwd(1), bwd(3),
                  pl.BlockSpec((1, d), lambda i, s: (0, 0)), st_spec],
        out_specs=[fwd(0), bwd(0), st_spec],
        out_shape=[jax.ShapeDtypeStruct((b, l, d), BF16), jax.ShapeDtypeStruct((b, l, d), BF16),
                   jax.ShapeDtypeStruct(s0.shape, F32)],
        scratch_shapes=[pltpu.VMEM(s0.shape[1:], F32), pltpu.VMEM(s0.shape[1:], F32), pltpu.VMEM((t, d), F32)],
        compiler_params=_params("parallel", "arbitrary"),
        name="hgrn_scan",
    )(proj, proj, proj, proj, proj, proj, lb, s0)


def _hgrn2_mixer(x_c, x_l, ng, mod_c, mod_l, w_in, lower, out_g, w_out, g1, want_ctx):
    d = x_l.shape[-1]
    p_c, g_c = _norm_mod_proj(x_c, ng, mod_c[0], mod_c[1], w_in, F32, silu_cols=(2 * d, 3 * d))
    p_l, g_l = _norm_mod_proj(x_l, ng, mod_l[0], mod_l[1], w_in, F32, silu_cols=(2 * d, 3 * d))
    lb = lower.reshape(1, d)
    zero = jnp.zeros((x_l.shape[0], 2, A_HEADS, LANES, LANES), F32)
    ocf, ocb, s_ctx = _hgrn_scan(p_c, lb, zero)
    olf, olb, _ = _hgrn_scan(p_l, lb, s_ctx)
    o_c, o_l = (ocf, ocb), (olf, olb)
    og = out_g.reshape(1, d)
    new_l = _out_call(_out_hgrn_body, [(o_l[0], 0, d), (o_l[1], 0, d), (g_l, 0, d)], [og, w_out],
                      x_l, mod_l[2], g1, "hgrn_out")
    new_c = None
    if want_ctx:
        new_c = _out_call(_out_hgrn_body, [(o_c[0], 0, d), (o_c[1], 0, d), (g_c, 0, d)], [og, w_out],
                          x_c, mod_c[2], g1, "hgrn_out")
    return new_c, new_l


def _softmax_pv(s_list, v_list):
    mx = functools.reduce(jnp.maximum, [jnp.max(s, axis=-1, keepdims=True) for s in s_list])
    ps = [jnp.exp(s - mx) for s in s_list]
    den = functools.reduce(jnp.add, [jnp.sum(p, axis=-1, keepdims=True) for p in ps])
    acc = functools.reduce(jnp.add, [jnp.dot(p.astype(BF16), v, preferred_element_type=F32)
                                     for p, v in zip(ps, v_list)])
    return acc / den


def _na_lat_body(q_ref, k_ref, v_ref, kc_ref, vc_ref, t2_ref, o_ref, *, rows, kh, scale):
    w = GRID_W
    rb = NA_ROW_BLOCK
    nk = kh + rb
    n_rel = 2 * kh - 1
    first = lax.broadcasted_iota(jnp.int32, (rb * w, LANES), 1) < (LANES // 2)
    kc = kc_ref[0]
    vc = vc_ref[0]

    def bias_index(r, ks, m):
        r0 = jnp.clip(r - kh // 2, 0, rows - kh)
        ka = ks + 2 * m
        rel_a = ka - r + (kh - 1)
        in_a = (ka >= r0) & (ka < r0 + kh)
        in_b = (ka + 1 >= r0) & (ka + 1 < r0 + kh)
        both, only_b, only_a = rel_a, (n_rel - 1) + rel_a + 1, (2 * n_rel - 1) + rel_a
        return jnp.where(in_a, jnp.where(in_b, both, only_a), jnp.where(in_b, only_b, 3 * n_rel - 1))

    def block(i, carry):
        rq = i * rb
        ks = jnp.clip(rq - kh // 2, 0, rows - nk)
        q2 = q_ref[0, pl.ds(pl.multiple_of(rq * w, rb * w), rb * w), :] * scale
        k2 = k_ref[0, pl.ds(pl.multiple_of(ks * w, w), nk * w), :]
        v2 = v_ref[0, pl.ds(pl.multiple_of(ks * w, w), nk * w), :]
        outs = []
        for hh in range(2):
            qm = jnp.where(first if hh == 0 else jnp.logical_not(first), q2, jnp.zeros_like(q2))
            bias = jnp.concatenate(
                [jnp.concatenate([t2_ref[hh, bias_index(rq + a, ks, m)] for m in range(nk // 2)], axis=-1)
                 for a in range(rb)], axis=0)
            s_w = lax.dot_general(qm, k2, NT_DIMS, preferred_element_type=F32) + bias
            s_c = lax.dot_general(qm, kc, NT_DIMS, preferred_element_type=F32)
            outs.append(_softmax_pv([s_w, s_c], [v2, vc]))
        o_ref[0, pl.ds(pl.multiple_of(rq * w, rb * w), rb * w), :] = (
            jnp.where(first, outs[0], outs[1]).astype(o_ref.dtype))
        return carry

    lax.fori_loop(0, rows // rb, block, 0, unroll=4)


def _na_ctx_body(q_ref, k_ref, v_ref, o_ref, *, scale):
    n = q_ref.shape[1]
    first = lax.broadcasted_iota(jnp.int32, (n, LANES), 1) < (LANES // 2)
    q2 = q_ref[0] * scale
    k2 = k_ref[0]
    v2 = v_ref[0]
    outs = []
    for hh in range(2):
        qm = jnp.where(first if hh == 0 else jnp.logical_not(first), q2, jnp.zeros_like(q2))
        s = lax.dot_general(qm, k2, NT_DIMS, preferred_element_type=F32)
        outs.append(_softmax_pv([s], [v2]))
    o_ref[0] = jnp.where(first, outs[0], outs[1]).astype(o_ref.dtype)


def _na_bias_table(rpb, kh, kw):
    w = jnp.arange(GRID_W)[:, None]
    c = jnp.arange(GRID_W)[None, :]
    c0 = jnp.clip(w - kw // 2, 0, GRID_W - kw)
    inside = (c >= c0) & (c < c0 + kw)
    pad = GRID_W - kw
    padded = jnp.pad(rpb.astype(F32), ((0, 0), (0, 0), (pad, pad)))
    shifted = jnp.stack([padded[:, :, pad + kw - 1 - q:pad + kw - 1 - q + GRID_W] for q in range(GRID_W)], axis=2)
    t = jnp.where(inside[None, None], shifted, NEG_BIG)
    off = jnp.full_like(t, NEG_BIG)
    pair = lambda a, b: jnp.concatenate([a, b], axis=-1)
    return jnp.concatenate([pair(t[:, :-1], t[:, 1:]), pair(off, t), pair(t, off), pair(off, off)[:, :1]], axis=1)


def _na_mixer(x_c, x_l, ng, mod_c, mod_l, w_qkv, rpb, w_out, g1, want_ctx):
    b, l, d = x_l.shape
    lc = x_c.shape[1]
    heads = rpb.shape[0]
    na_rows, na_cols = (rpb.shape[1] + 1) // 2, (rpb.shape[2] + 1) // 2
    dh = d // heads
    assert 2 * dh == LANES and l % GRID_W == 0
    rows = l // GRID_W
    kh = min(na_rows, rows)
    assert kh == na_rows and (kh + NA_ROW_BLOCK) % 2 == 0
    assert rows % NA_ROW_BLOCK == 0 and rows >= kh + NA_ROW_BLOCK
    scale = dh ** -0.5
    nhp = heads // 2
    qkv_c = _norm_mod_proj(x_c, ng, mod_c[0], mod_c[1], w_qkv, BF16)
    qkv_l = _norm_mod_proj(x_l, ng, mod_l[0], mod_l[1], w_qkv, BF16)
    t2 = _na_bias_table(rpb, kh, na_cols)
    o_l = pl.pallas_call(
        functools.partial(_na_lat_body, rows=rows, kh=kh, scale=scale),
        grid=(b, nhp),
        in_specs=[pl.BlockSpec((1, l, LANES), lambda i, p: (i, 0, p)),
                  pl.BlockSpec((1, l, LANES), lambda i, p: (i, 0, nhp + p)),
                  pl.BlockSpec((1, l, LANES), lambda i, p: (i, 0, 2 * nhp + p)),
                  pl.BlockSpec((1, lc, LANES), lambda i, p: (i, 0, nhp + p)),
                  pl.BlockSpec((1, lc, LANES), lambda i, p: (i, 0, 2 * nhp + p)),
                  pl.BlockSpec((2,) + t2.shape[1:], lambda i, p: (p, 0, 0, 0))],
        out_specs=pl.BlockSpec((1, l, LANES), lambda i, p: (i, 0, p)),
        out_shape=jax.ShapeDtypeStruct((b, l, d), BF16),
        compiler_params=_params("parallel", "parallel"),
        name="na_latent",
    )(qkv_l, qkv_l, qkv_l, qkv_c, qkv_c, t2)
    new_l = _out_call(_out_plain_body, [(o_l, 0, d)], [w_out], x_l, mod_l[2], g1, "na_out")
    new_c = None
    if want_ctx:
        o_c = pl.pallas_call(
            functools.partial(_na_ctx_body, scale=scale),
            grid=(b, nhp),
            in_specs=[pl.BlockSpec((1, lc, LANES), lambda i, p: (i, 0, p)),
                      pl.BlockSpec((1, lc, LANES), lambda i, p: (i, 0, nhp + p)),
                      pl.BlockSpec((1, lc, LANES), lambda i, p: (i, 0, 2 * nhp + p))],
            out_specs=pl.BlockSpec((1, lc, LANES), lambda i, p: (i, 0, p)),
            out_shape=jax.ShapeDtypeStruct((b, lc, d), BF16),
            compiler_params=_params("parallel", "parallel"),
            name="na_context",
        )(qkv_c, qkv_c, qkv_c)
        new_c = _out_call(_out_plain_body, [(o_c, 0, d)], [w_out], x_c, mod_c[2], g1, "na_out")
    return new_c, new_l


S5_GROUP_BLOCK = 8


def _s5_kernel_body(c_ref, w_ref, o_ref):
    for i in range(c_ref.shape[0]):
        o_ref[i] = jnp.dot(c_ref[i], w_ref[i], preferred_element_type=F32, precision=lax.Precision.HIGHEST)


def _s5_impulse(cmat, wmat):
    n = cmat.shape[0]
    gb = S5_GROUP_BLOCK
    return pl.pallas_call(
        _s5_kernel_body,
        grid=(n // gb,),
        in_specs=[pl.BlockSpec((gb,) + cmat.shape[1:], lambda i: (i, 0, 0)),
                  pl.BlockSpec((gb,) + wmat.shape[1:], lambda i: (i, 0, 0))],
        out_specs=pl.BlockSpec((gb, cmat.shape[1], wmat.shape[2]), lambda i: (i, 0, 0)),
        out_shape=jax.ShapeDtypeStruct((n, cmat.shape[1], wmat.shape[2]), F32),
        compiler_params=_params("parallel"),
        name="s5_impulse",
    )(cmat, wmat)


def _s5_operators(lam_re, lam_im, log_dt, b_re, b_im, c_re, c_im, t):
    _, g, p = lam_re.shape
    cg = b_re.shape[-1]
    lam = lax.complex(lam_re.astype(F32), lam_im.astype(F32))
    ldt = lam * jnp.exp(log_dt.astype(F32))[..., None]
    a = jnp.exp(ldt)
    bbar = ((a - 1.0) / lam)[..., None] * lax.complex(b_re.astype(F32), b_im.astype(F32))
    cm = lax.complex(c_re.astype(F32), c_im.astype(F32))
    apow = jnp.exp(ldt[..., None] * jnp.arange(t + 1, dtype=F32))
    w = apow[..., :t, None] * bbar[:, :, :, None, :]
    wmat = jnp.concatenate([jnp.real(w), jnp.imag(w)], axis=2).reshape(2 * g, 2 * p, t * cg)
    cmat = jnp.concatenate([jnp.real(cm), -jnp.imag(cm)], axis=-1).reshape(2 * g, cg, 2 * p)
    k = _s5_impulse(cmat, wmat).reshape(2, g, cg, t, cg)
    k = jnp.transpose(k, (0, 1, 3, 4, 2))
    kfull = jnp.concatenate([k[1, :, :0:-1], (k[0, :, :1] + k[1, :, :1]), k[0, :, 1:]], axis=1)
    lagvec = jnp.transpose(kfull, (0, 2, 1, 3)).reshape(g, cg, (2 * t - 1) * cg).astype(BF16)
    toep = jnp.stack([lagvec[:, :, (t - 1 - s) * cg:(2 * t - 1 - s) * cg] for s in range(t)], axis=1)
    toep = toep.reshape(g, t * cg, t * cg)

    def state_in(wd, flip):
        wd = wd[:, :, ::-1] if flip else wd
        m = jnp.transpose(wd, (0, 2, 3, 1)).reshape(g, t * cg, p)
        return jnp.concatenate([jnp.real(m), jnp.imag(m)], axis=-1)

    def state_out(cd, pw):
        n = cd[:, :, :, None] * pw[:, None, :, :]
        n = jnp.transpose(n, (0, 2, 3, 1)).reshape(g, p, t * cg)
        return jnp.concatenate([jnp.real(n), -jnp.imag(n)], axis=1)

    m_f = state_in(w[0], True)
    m_b = state_in(w[1], False)
    n_f = state_out(cm[0], apow[0][..., 1:])
    n_b = state_out(cm[1], apow[1][..., :0:-1])
    at = apow[..., t]
    a1 = jnp.concatenate([jnp.real(at), jnp.real(at)], axis=-1)
    a2 = jnp.concatenate([-jnp.imag(at), jnp.imag(at)], axis=-1)
    bf = lambda m: m.astype(BF16)
    return dict(toep=bf(toep), m_f=bf(m_f), m_b=bf(m_b), n_f=bf(n_f), n_b=bf(n_b), a1=a1, a2=a2)


def _s5_body(u_ref, dsk_ref, toep_ref, mf_ref, mb_ref, nf_ref, nb_ref, a1_ref, a2_ref, x0_ref,
             y_ref, xfin_ref, z_scr, zs_scr, xin_scr):
    gb, nc = u_ref.shape[1], u_ref.shape[2]
    half = LANES // 2
    ubs = [u_ref[0, gi].astype(BF16) for gi in range(gb)]
    for dr, m_ref in enumerate((mf_ref, mb_ref)):
        z = jnp.stack([jnp.dot(ubs[gi], m_ref[gi], preferred_element_type=F32) for gi in range(gb)])
        z_scr[dr] = jnp.swapaxes(z, 0, 1)

    swap = lambda t: pltpu.roll(t, half, t.ndim - 1)
    for dr in range(2):
        zs_scr[dr] = swap(z_scr[dr].reshape(nc * gb, LANES)).reshape(nc, gb, LANES)

    coef = [(a1_ref[dr], a2_ref[dr]) for dr in range(2)]

    def scan(j, xs):
        out = []
        for dr in range(2):
            jj = j if dr == 0 else nc - 1 - j
            x, xw = xs[2 * dr], xs[2 * dr + 1]
            a1, a2 = coef[dr]
            xin_scr[dr, jj] = x
            out.append(a1 * x + a2 * xw + z_scr[dr, jj])
            out.append(a1 * xw - a2 * x + zs_scr[dr, jj])
        return tuple(out)

    x0f, x0b = x0_ref[0, 0], x0_ref[0, 1]
    xf, _, xb, _ = lax.fori_loop(0, nc, scan, (x0f, swap(x0f), x0b, swap(x0b)), unroll=4)
    xfin_ref[0, 0] = xf
    xfin_ref[0, 1] = xb
    xin = [jnp.swapaxes(xin_scr[dr], 0, 1).astype(BF16) for dr in range(2)]
    for gi in range(gb):
        u = u_ref[0, gi]
        y = jnp.dot(ubs[gi], toep_ref[gi], preferred_element_type=F32) + u * dsk_ref[gi]
        y = y + jnp.dot(xin[0][gi], nf_ref[gi], preferred_element_type=F32)
        y = y + jnp.dot(xin[1][gi], nb_ref[gi], preferred_element_type=F32)
        y_ref[0, gi] = y


def _piece_transpose(src, piece_of, cg):
    n = len(src)
    out = [None] * n
    for s in range(n):
        c = src[s % n]
        for p in range(1, n):
            c = jnp.where(piece_of == p, src[(p + s) % n], c)
        r = c if s == 0 else pltpu.roll(c, s * cg, 1)
        for p in range(n):
            out[p] = r if s == 0 else jnp.where(piece_of == (p + s) % n, r, out[p])
    return out


def _to_chunks(h_scr, o_ref, t, cg):
    ncb = h_scr.shape[1] // t
    pieces = LANES // cg
    piece_of = lax.broadcasted_iota(jnp.int32, (ncb, LANES), 1) // cg

    def lane_block(lb, carry):
        for tq in range(t // pieces):
            steps = [h_scr[lb, pl.ds(tq * pieces + tr, ncb, stride=t), :] for tr in range(pieces)]
            for gl, col in enumerate(_piece_transpose(steps, piece_of, cg)):
                o_ref[0, lb * pieces + gl, :, tq * LANES:(tq + 1) * LANES] = col
        return carry

    lax.fori_loop(0, h_scr.shape[0], lane_block, 0, unroll=2)


def _from_chunks(y_ref, y_scr, t, cg):
    ncb = y_ref.shape[2]
    pieces = LANES // cg
    piece_of = lax.broadcasted_iota(jnp.int32, (ncb, LANES), 1) // cg

    def lane_block(lb, carry):
        for tq in range(t // pieces):
            cols = [y_ref[0, lb * pieces + gl, :, tq * LANES:(tq + 1) * LANES] for gl in range(pieces)]
            for tr, step in enumerate(_piece_transpose(cols, piece_of, cg)):
                y_scr[lb, pl.ds(tq * pieces + tr, ncb, stride=t), :] = step
        return carry

    lax.fori_loop(0, y_scr.shape[0], lane_block, 0, unroll=2)


def _norm_chunks_body(x_ref, g_ref, sh_ref, sc_ref, o_ref, h_scr, *, t, cg):
    h = _norm_mod(x_ref[...], g_ref[...], sh_ref[0], sc_ref[0])
    for lb in range(h_scr.shape[0]):
        h_scr[lb] = h[:, lb * LANES:(lb + 1) * LANES]
    _to_chunks(h_scr, o_ref, t, cg)


def _norm_mod_chunks(x, g, sh, sc, groups, t):
    b, l, d = x.shape
    cg = d // groups
    tm = _row_tile(l, 512)
    per = l // tm
    assert tm % t == 0 and (tm // t) % SUBLANES == 0 and LANES % cg == 0 and t % (LANES // cg) == 0
    return pl.pallas_call(
        functools.partial(_norm_chunks_body, t=t, cg=cg),
        grid=(b * per,),
        in_specs=[pl.BlockSpec((tm, d), lambda i: (i, 0)),
                  pl.BlockSpec((1, d), lambda i: (0, 0)),
                  _mod_spec(sh, per), _mod_spec(sc, per)],
        out_specs=pl.BlockSpec((1, groups, tm // t, t * cg), lambda i: (i // per, 0, i % per, 0)),
        out_shape=jax.ShapeDtypeStruct((b, groups, l // t, t * cg), F32),
        scratch_shapes=[pltpu.VMEM((d // LANES, tm, LANES), F32)],
        compiler_params=_params("parallel"),
        name="norm_mod_chunks",
    )(x.reshape(b * l, d), g.reshape(1, d), sh.table, sc.table)


def _glu_chunks_body(y_ref, w_ref, res_ref, gate_ref, g1_ref, o_ref, y_scr, *, t, cg):
    _from_chunks(y_ref, y_scr, t, cg)
    d = o_ref.shape[-1]
    y = jnp.concatenate([y_scr[lb] for lb in range(y_scr.shape[0])], axis=-1)
    a = _gelu_tanh(y).astype(BF16)
    ag = jnp.dot(a, w_ref[...], preferred_element_type=F32)
    o_ref[...] = _residual(res_ref, gate_ref, g1_ref, ag[:, :d] * jax.nn.sigmoid(ag[:, d:]))


def _glu_out_chunks(y, w_glu, res, gate, g1, t):
    b, l, d = res.shape
    groups = y.shape[1]
    cg = d // groups
    tm = _row_tile(l, 512)
    per = l // tm
    out = pl.pallas_call(
        functools.partial(_glu_chunks_body, t=t, cg=cg),
        grid=(b * per,),
        in_specs=[pl.BlockSpec((1, groups, tm // t, t * cg), lambda i: (i // per, 0, i % per, 0)),
                  _layer_spec(w_glu, w_glu.shape, lambda i: (0, 0)),
                  pl.BlockSpec((tm, d), lambda i: (i, 0)),
                  _mod_spec(gate, per),
                  pl.BlockSpec((1, d), lambda i: (0, 0))],
        out_specs=pl.BlockSpec((tm, d), lambda i: (i, 0)),
        out_shape=jax.ShapeDtypeStruct((b * l, d), F32),
        scratch_shapes=[pltpu.VMEM((d // LANES, tm, LANES), F32)],
        compiler_params=_params("parallel"),
        name="s5_glu_out",
    )(y, w_glu.stack, res.reshape(b * l, d), gate.table, g1.reshape(1, d))
    return out.reshape(b, l, d)


def _s5_scan(u, ops, dsk, x0):
    b, g, nc, tc = u.shape
    gb = S5_GROUP_BLOCK
    per_g = lambda shape: pl.BlockSpec((gb,) + shape, lambda j, i: (j,) + (0,) * len(shape))
    y, xfin = pl.pallas_call(
        _s5_body,
        grid=(g // gb, b),
        in_specs=[pl.BlockSpec((1, gb, nc, tc), lambda j, i: (i, j, 0, 0)),
                  per_g((1, tc)), per_g((tc, tc)), per_g((tc, LANES)), per_g((tc, LANES)),
                  per_g((LANES, tc)), per_g((LANES, tc)),
                  pl.BlockSpec((2, gb, LANES), lambda j, i: (0, j, 0)),
                  pl.BlockSpec((2, gb, LANES), lambda j, i: (0, j, 0)),
                  pl.BlockSpec((1, 2, gb, LANES), lambda j, i: (i, 0, j, 0))],
        out_specs=[pl.BlockSpec((1, gb, nc, tc), lambda j, i: (i, j, 0, 0)),
                   pl.BlockSpec((1, 2, gb, LANES), lambda j, i: (i, 0, j, 0))],
        out_shape=[jax.ShapeDtypeStruct((b, g, nc, tc), F32), jax.ShapeDtypeStruct((b, 2, g, LANES), F32)],
        scratch_shapes=[pltpu.VMEM((2, nc, gb, LANES), F32)] * 3,
        compiler_params=_params("parallel", "parallel"),
        name="s5_scan",
    )(u, dsk, ops["toep"], ops["m_f"], ops["m_b"], ops["n_f"], ops["n_b"], ops["a1"], ops["a2"], x0)
    return y, xfin


def _s5_mixer(x_c, x_l, ng, mod_c, mod_l, lam_re, lam_im, log_dt, b_re, b_im, c_re, c_im, d_skip, w_glu, g1,
              want_ctx):
    b, l, d = x_l.shape
    g, p = lam_re.shape[1], lam_re.shape[2]
    cg = d // g
    assert 2 * p == LANES and g % S5_GROUP_BLOCK == 0
    ops = _s5_operators(lam_re, lam_im, log_dt, b_re, b_im, c_re, c_im, S5_CHUNK)
    dsk = jnp.tile(d_skip.astype(F32).reshape(g, 1, cg), (1, 1, S5_CHUNK))
    u_c = _norm_mod_chunks(x_c, ng, mod_c[0], mod_c[1], g, S5_CHUNK)
    u_l = _norm_mod_chunks(x_l, ng, mod_l[0], mod_l[1], g, S5_CHUNK)
    y_c, x_ctx = _s5_scan(u_c, ops, dsk, jnp.zeros((b, 2, g, LANES), F32))
    y_l, _ = _s5_scan(u_l, ops, dsk, x_ctx)
    new_l = _glu_out_chunks(y_l, w_glu, x_l, mod_l[2], g1, S5_CHUNK)
    new_c = _glu_out_chunks(y_c, w_glu, x_c, mod_c[2], g1, S5_CHUNK) if want_ctx else None
    return new_c, new_l


FFN_HALO = 16
FFN_TN = 256
FFN_TILES_PER_STEP = 11
FFN_ROW_SPLIT = 256


def _ffn_body(xp_ref, x_ref, xn_ref, g2_ref, sh_ref, sc_ref, *rest, per, n_tiles, tps):
    w_refs = rest[:2 * tps]
    cw_ref, cb_ref, wo_ref, gate_ref, g3_ref, o_ref, h_scr, gat_scr = rest[2 * tps:]
    i, j = pl.program_id(0), pl.program_id(1)
    tm = x_ref.shape[0]
    hl = FFN_HALO
    f = wo_ref.shape[0]
    n_steps = pl.cdiv(n_tiles, tps)
    when = (lambda cond: lambda fn: fn()) if n_steps == 1 else pl.when

    @when(j == 0)
    def _():
        g, sh, sc = g2_ref[...], sh_ref[0], sc_ref[0]
        keep_p = ((i % per) != 0).astype(F32)
        keep_n = ((i % per) != per - 1).astype(F32)
        h_scr[0:hl] = (_norm_mod(xp_ref[...], g, sh, sc) * keep_p).astype(BF16)
        h_scr[hl:hl + tm] = _norm_mod(x_ref[...], g, sh, sc).astype(BF16)
        h_scr[hl + tm:] = (_norm_mod(xn_ref[...], g, sh, sc) * keep_n).astype(BF16)

    sub = FFN_ROW_SPLIT if tm % FFN_ROW_SPLIT == 0 else tm
    ns = tm // sub
    edge = SUBLANES
    tn = w_refs[0].shape[1]

    def pieces(w_ref):
        out = []
        for s in range(ns):
            lo = hl + s * sub - (hl if s == 0 else 0)
            hi = hl + (s + 1) * sub + (hl if s == ns - 1 else 0)
            out.append(jnp.dot(h_scr[lo:hi], w_ref[...], preferred_element_type=F32))
        return out

    def conv(p, s, cw, cb):
        base = hl if s == 0 else 0
        left = p[s][hl - edge:hl] if s == 0 else p[s - 1][-edge:]
        right = p[s][base + sub:base + sub + edge] if s == ns - 1 else p[s + 1][:edge]
        ext = jnp.concatenate([left, p[s][base:base + sub], right], axis=0)
        um = pltpu.roll(ext, 1, 0)[edge:edge + sub]
        up = pltpu.roll(ext, sub + 2 * edge - 1, 0)[edge:edge + sub]
        return cb + um * cw[0:1] + ext[edge:edge + sub] * cw[1:2] + up * cw[2:3]

    def tile(k):
        col = k * tn if n_steps == 1 else pl.multiple_of((j * tps + k) * tn, tn)
        cols_a, cols_v = pl.ds(col, tn), pl.ds(pl.multiple_of(f + col, tn), tn)
        pa, pv = pieces(w_refs[2 * k]), pieces(w_refs[2 * k + 1])
        for s in range(ns):
            a = conv(pa, s, cw_ref[:, cols_a], cb_ref[:, cols_a])
            v = conv(pv, s, cw_ref[:, cols_v], cb_ref[:, cols_v])
            gat_scr[s * sub:(s + 1) * sub, cols_a] = (a * jax.nn.sigmoid(a) * v).astype(BF16)

    always = n_tiles - tps * (n_steps - 1)
    for k in range(tps):
        if k < always:
            tile(k)
        else:
            pl.when(j * tps + k < n_tiles)(functools.partial(tile, k))

    @when(j == n_steps - 1)
    def _():
        y = jnp.dot(gat_scr[...], wo_ref[...], preferred_element_type=F32)
        o_ref[...] = x_ref[...] + gate_ref[0] * _rms(y, g3_ref[...])


def _conv_ffn_block(x, ng2, mod, w_in, conv_w, conv_b, w_out, ng3):
    b, l, d = x.shape
    f = w_out.shape[0]
    tn, hl = FFN_TN, FFN_HALO
    assert f % tn == 0 and conv_w.shape[0] == 3 and w_in.shape[1] == 2 * f
    nj = f // tn
    tm = _row_tile(l, 1024)
    per = l // tm
    hb = tm // hl
    last_hb = b * l // hl - 1
    cb = conv_b.reshape(1, 2 * f)
    x2 = x.reshape(b * l, d)
    tps = min(FFN_TILES_PER_STEP, nj)
    once = dict(pipeline_mode=pl.Buffered(1)) if tps == nj else {}
    w_specs = []
    for k in range(tps):
        w_specs.append(_layer_spec(w_in, (d, tn), lambda i, j, k=k: (0, jnp.minimum(j * tps + k, nj - 1)), **once))
        w_specs.append(_layer_spec(w_in, (d, tn), lambda i, j, k=k: (0, nj + jnp.minimum(j * tps + k, nj - 1)),
                                   **once))
    whole = lambda arr: pl.BlockSpec(arr.shape, lambda i, j: (0,) * arr.ndim)
    out = pl.pallas_call(
        functools.partial(_ffn_body, per=per, n_tiles=nj, tps=tps),
        grid=(b * per, pl.cdiv(nj, tps)),
        in_specs=[pl.BlockSpec((hl, d), lambda i, j: (jnp.maximum(i * hb - 1, 0), 0)),
                  pl.BlockSpec((tm, d), lambda i, j: (i, 0)),
                  pl.BlockSpec((hl, d), lambda i, j: (jnp.minimum((i + 1) * hb, last_hb), 0)),
                  pl.BlockSpec((1, d), lambda i, j: (0, 0)),
                  _mod_spec(mod[3], per), _mod_spec(mod[4], per),
                  *w_specs, whole(conv_w), whole(cb),
                  _layer_spec(w_out, (f, d), lambda i, j: (0, 0), pipeline_mode=pl.Buffered(1)),
                  _mod_spec(mod[5], per),
                  pl.BlockSpec((1, d), lambda i, j: (0, 0))],
        out_specs=pl.BlockSpec((tm, d), lambda i, j: (i, 0)),
        out_shape=jax.ShapeDtypeStruct((b * l, d), F32),
        scratch_shapes=[pltpu.VMEM((tm + 2 * hl, d), BF16), pltpu.VMEM((tm, f), BF16)],
        compiler_params=_params("parallel", "arbitrary"),
        name="conv_ffn",
    )(x2, x2, x2, ng2.reshape(1, d), mod[3].table, mod[4].table, *([w_in.stack] * (2 * tps)), conv_w, cb,
      w_out.stack,
      mod[5].table, ng3.reshape(1, d))
    return out.reshape(b, l, d)


def kernel(x, c, ctx, c_ctx, w_mod, b_mod, norm_g, a_w_in, a_lower_logits, a_out_g, a_w_out, b_w_qkv, b_rpb,
           b_w_out, c_lam_re, c_lam_im, c_log_dt, c_b_re, c_b_im, c_c_re, c_c_im, c_d, c_w_glu, f_w_in,
           f_conv_w, f_conv_b, f_w_out):
    bsz, _, d = x.shape
    depth = w_mod.shape[0]
    p = jax.nn.softmax(a_lower_logits.astype(F32), axis=0)
    lower = jnp.cumsum(p, axis=0) - p[0]
    n_rows = -(-(bsz + 1) // SUBLANES) * SUBLANES
    c_rows = jnp.zeros((n_rows, d), F32).at[:bsz].set(c).at[bsz].set(c_ctx)
    table = _modulation(c_rows, w_mod, b_mod).reshape(depth * n_rows * N_MOD, 1, d)
    wa_in, wa_out, wb_qkv, wb_out, wc_glu, wf_in, wf_out = (
        w.astype(BF16) for w in (a_w_in, a_w_out, b_w_qkv, b_w_out, c_w_glu, f_w_in, f_w_out))
    lat, cx = x, ctx
    for i in range(depth):
        kind, j = i % N_MIXERS, i // N_MIXERS
        last = i == depth - 1
        mod_l = [ModRow(table, i * n_rows * N_MOD + k, N_MOD) for k in range(N_MOD)]
        mod_c = [ModRow(table, (i * n_rows + bsz) * N_MOD + k, 0) for k in range(N_MOD)]
        ng = norm_g[i]
        if kind == 0:
            cx1, lat = _hgrn2_mixer(cx, lat, ng[0], mod_c, mod_l, LayerOf(wa_in, j), lower[i], a_out_g[j],
                                    LayerOf(wa_out, j), ng[1], not last)
        elif kind == 1:
            cx1, lat = _na_mixer(cx, lat, ng[0], mod_c, mod_l, LayerOf(wb_qkv, j), b_rpb[j], LayerOf(wb_out, j),
                                 ng[1], not last)
        else:
            cx1, lat = _s5_mixer(cx, lat, ng[0], mod_c, mod_l, c_lam_re[j], c_lam_im[j], c_log_dt[j], c_b_re[j],
                                 c_b_im[j], c_c_re[j], c_c_im[j], c_d[j], LayerOf(wc_glu, j), ng[1], not last)
        ffn = (LayerOf(wf_in, i), f_conv_w[i], f_conv_b[i], LayerOf(wf_out, i), ng[3])
        lat = _conv_ffn_block(lat, ng[2], mod_l, *ffn)
        if not last:
            cx = _conv_ffn_block(cx1, ng[2], mod_c, *ffn)
    return lat
```

```python
import functools
from typing import NamedTuple

import jax
import jax.numpy as jnp
from jax import lax
from jax.experimental import pallas as pl
from jax.experimental.pallas import tpu as pltpu

F32 = jnp.float32
BF16 = jnp.bfloat16

EPS = 1e-6
F_MIN = 1e-30
N_MOD = 6
N_MIXERS = 3
A_HEADS = 8
GRID_W = 64
S5_CHUNK = 16
NEG_BIG = -1e30
NA_ROW_BLOCK = 4
SUBLANES = 8
LANES = 128
VMEM_LIMIT_BYTES = 56 * 1024 * 1024

NT_DIMS = (((1,), (1,)), ((), ()))
TN_DIMS = (((0,), (0,)), ((), ()))


def _params(*sem):
    return pltpu.CompilerParams(dimension_semantics=sem, vmem_limit_bytes=VMEM_LIMIT_BYTES)


def _row_tile(n, want):
    t = min(n, want)
    assert n % t == 0, (n, t)
    return t


def _rms(y, g):
    return y * lax.rsqrt(jnp.mean(y * y, axis=-1, keepdims=True) + EPS) * g


def _mod_body(c_ref, w_ref, b_ref, o_ref):
    c = c_ref[...]
    s = (c * jax.nn.sigmoid(c)).astype(BF16)
    o_ref[0] = jnp.dot(s, w_ref[0].astype(BF16), preferred_element_type=F32) + b_ref[0]


def _modulation(c_rows, w_mod, b_mod):
    depth, d, n = w_mod.shape
    r = c_rows.shape[0]
    tn = n // 4
    return pl.pallas_call(
        _mod_body,
        grid=(depth, n // tn),
        in_specs=[pl.BlockSpec((r, d), lambda i, j: (0, 0)),
                  pl.BlockSpec((1, d, tn), lambda i, j: (i, 0, j)),
                  pl.BlockSpec((1, 1, tn), lambda i, j: (i, 0, j))],
        out_specs=pl.BlockSpec((1, r, tn), lambda i, j: (i, 0, j)),
        out_shape=jax.ShapeDtypeStruct((depth, r, n), F32),
        compiler_params=_params("parallel", "parallel"),
        name="adaln_mod",
    )(c_rows, w_mod, b_mod.reshape(depth, 1, n))


def _norm_mod(x, g, sh, sc):
    return _rms(x, g) * (1.0 + sc) + sh


class ModRow(NamedTuple):
    table: jax.Array
    first: int
    step: int


def _mod_spec(m, per):
    return pl.BlockSpec((1, 1, m.table.shape[-1]), lambda *idx: (m.first + m.step * (idx[0] // per), 0, 0))


class LayerOf(NamedTuple):
    stack: jax.Array
    layer: int

    @property
    def shape(self):
        return self.stack.shape[1:]


def _layer_spec(w, block, index_map, **kw):
    return pl.BlockSpec((None,) + tuple(block), lambda *idx: (w.layer,) + tuple(index_map(*idx)), **kw)


def _proj_body(x_ref, g_ref, sh_ref, sc_ref, w_ref, o_ref):
    h = _norm_mod(x_ref[...], g_ref[...], sh_ref[0], sc_ref[0]).astype(BF16)
    o_ref[...] = jnp.dot(h, w_ref[...], preferred_element_type=F32).astype(o_ref.dtype)


def _proj_silu_body(x_ref, g_ref, sh_ref, sc_ref, w_ref, o_ref, os_ref, *, lo, hi):
    h = _norm_mod(x_ref[...], g_ref[...], sh_ref[0], sc_ref[0]).astype(BF16)
    y = jnp.dot(h, w_ref[...], preferred_element_type=F32)
    o_ref[:, :lo] = y[:, :lo]
    o_ref[:, lo:] = y[:, hi:]
    gp = y[:, lo:hi]
    os_ref[...] = (gp * jax.nn.sigmoid(gp)).astype(os_ref.dtype)


def _norm_mod_proj(x, g, sh, sc, w, out_dtype, silu_cols=None):
    b, l, d = x.shape
    n = w.shape[1]
    tm = _row_tile(l, 512)
    per = l // tm
    body, widths, dtypes = _proj_body, [n], [out_dtype]
    if silu_cols is not None:
        lo, hi = silu_cols
        body, widths, dtypes = functools.partial(_proj_silu_body, lo=lo, hi=hi), [n - (hi - lo), hi - lo], [out_dtype, BF16]
    outs = pl.pallas_call(
        body,
        grid=(b * per,),
        in_specs=[pl.BlockSpec((tm, d), lambda i: (i, 0)),
                  pl.BlockSpec((1, d), lambda i: (0, 0)),
                  _mod_spec(sh, per), _mod_spec(sc, per),
                  _layer_spec(w, (d, n), lambda i: (0, 0), pipeline_mode=pl.Buffered(1))],
        out_specs=[pl.BlockSpec((tm, wd), lambda i: (i, 0)) for wd in widths],
        out_shape=[jax.ShapeDtypeStruct((b * l, wd), dt) for wd, dt in zip(widths, dtypes)],
        compiler_params=_params("parallel"),
        name="norm_mod_proj",
    )(x.reshape(b * l, d), g.reshape(1, d), sh.table, sc.table, w.stack)
    outs = [o.reshape(b, l, o.shape[-1]) for o in outs]
    return outs[0] if silu_cols is None else tuple(outs)


def _residual(res_ref, gate_ref, g1_ref, y):
    return res_ref[...] + gate_ref[0] * _rms(y, g1_ref[...])


def _out_plain_body(a_ref, w_ref, res_ref, gate_ref, g1_ref, o_ref):
    y = jnp.dot(a_ref[...], w_ref[...], preferred_element_type=F32)
    o_ref[...] = _residual(res_ref, gate_ref, g1_ref, y)


def _out_hgrn_body(of_ref, ob_ref, gp_ref, og_ref, w_ref, res_ref, gate_ref, g1_ref, o_ref):
    o = of_ref[...].astype(F32) + ob_ref[...].astype(F32)
    parts = []
    for h in range(A_HEADS):
        oh = o[:, h * LANES:(h + 1) * LANES]
        parts.append(oh * lax.rsqrt(jnp.mean(oh * oh, axis=-1, keepdims=True) + EPS))
    a = jnp.concatenate(parts, axis=-1) * og_ref[...] * gp_ref[...].astype(F32)
    y = jnp.dot(a.astype(BF16), w_ref[...], preferred_element_type=F32)
    o_ref[...] = _residual(res_ref, gate_ref, g1_ref, y)


def _gelu_tanh(y):
    return 0.5 * y * (1.0 + jnp.tanh(0.7978845608028654 * (y + 0.044715 * (y * y * y))))


def _out_call(body, row_inputs, const_inputs, res, gate, g1, name):
    b, l, d = res.shape
    tm = _row_tile(l, 1024)
    per = l // tm
    in_specs, args = [], []
    for arr, blk, width in row_inputs:
        in_specs.append(pl.BlockSpec((tm, width), lambda i, blk=blk: (i, blk)))
        args.append(arr.reshape(b * l, arr.shape[-1]))
    for arr in const_inputs:
        if isinstance(arr, LayerOf):
            in_specs.append(_layer_spec(arr, arr.shape, lambda i, nd=len(arr.shape): (0,) * nd))
            args.append(arr.stack)
        else:
            in_specs.append(pl.BlockSpec(arr.shape, lambda i, nd=arr.ndim: (0,) * nd))
            args.append(arr)
    in_specs += [pl.BlockSpec((tm, d), lambda i: (i, 0)),
                 _mod_spec(gate, per),
                 pl.BlockSpec((1, d), lambda i: (0, 0))]
    args += [res.reshape(b * l, d), gate.table, g1.reshape(1, d)]
    out = pl.pallas_call(
        body,
        grid=(b * per,),
        in_specs=in_specs,
        out_specs=pl.BlockSpec((tm, d), lambda i: (i, 0)),
        out_shape=jax.ShapeDtypeStruct((b * l, d), F32),
        compiler_params=_params("parallel"),
        name=name,
    )(*args)
    return out.reshape(b, l, d)


HGRN_CHUNK = 32
HGRN_SAFE_MIN = 1e-30
HGRN_Q_HEADROOM = 1e37


def _hgrn_gates(z, lb):
    f = lb + (1.0 - lb) * jax.nn.sigmoid(z)
    return jnp.maximum(f, F_MIN), 1.0 - f


def _hgrn_chunk_prep(q, v, z, lb, reverse):
    c = q.shape[0]
    half = c // 2
    fm, kk = _hgrn_gates(z, lb)
    pos = lax.broadcasted_iota(jnp.int32, (c, LANES), 0)
    if reverse:
        pos = (c - 1) - pos
    second = pos >= half
    hpos = jnp.where(second, pos - half, pos)
    ph = fm
    j = 1
    while j < half:
        ph = ph * jnp.where(hpos >= j, pltpu.roll(ph, (c - j) if reverse else j, 0), 1.0)
        j *= 2
    a_last = ph[half:half + 1] if reverse else ph[half - 1:half]
    r_last = ph[0:1] if reverse else ph[c - 1:c]
    p_mid = a_last
    p_last = a_last * r_last
    pp = jnp.where(second, ph * a_last, ph)
    r = jnp.where(second, ph, ph * (1.0 / a_last))
    kd = kk * (1.0 / r)
    q_max = jnp.max(jnp.abs(q), axis=0, keepdims=True)
    ok = jnp.where(p_mid >= HGRN_SAFE_MIN, r_last, 0.0) >= HGRN_SAFE_MIN
    ok = jnp.where(ok, p_mid * HGRN_Q_HEADROOM, -1.0) >= q_max
    return dict(qr=(q * r).astype(BF16), kd=kd.astype(BF16), qp=q * pp, kl=kd * r_last, vb=v.astype(BF16),
                dec=p_last, bad=jnp.where(ok, 0.0, 1.0))


def _hgrn_pair_dots(p1, p2, st):
    bf = lambda t: t.astype(BF16)
    sc1 = lax.dot_general(p1["qr"], p1["kd"], NT_DIMS, preferred_element_type=F32)
    sc2 = lax.dot_general(p2["qr"], p2["kd"], NT_DIMS, preferred_element_type=F32)
    cross = lax.dot_general(bf(p2["qp"]), bf(p1["kl"]), NT_DIMS, preferred_element_type=F32)
    q_all = jnp.concatenate([bf(p1["qp"]), bf(p2["qp"] * p1["dec"])], axis=0)
    o_st = lax.dot_general(q_all, st.astype(BF16), NT_DIMS, preferred_element_type=F32)
    k_all = jnp.concatenate([bf(p1["kl"] * p2["dec"]), bf(p2["kl"])], axis=0)
    u = lax.dot_general(jnp.concatenate([p1["vb"], p2["vb"]], axis=0), k_all, TN_DIMS, preferred_element_type=F32)
    return sc1, sc2, cross, o_st, u


def _hgrn_pair_out(p1, p2, sc1, sc2, cross, o_st, reverse):
    c = sc1.shape[0]
    row = lax.broadcasted_iota(jnp.int32, (c, c), 0)
    col = lax.broadcasted_iota(jnp.int32, (c, c), 1)
    keep = (col >= row) if reverse else (col <= row)
    tri = lambda sc: jnp.where(keep, sc, 0.0).astype(BF16)
    o1 = jnp.dot(tri(sc1), p1["vb"], preferred_element_type=F32) + o_st[:c]
    o2 = (jnp.dot(tri(sc2), p2["vb"], preferred_element_type=F32)
          + jnp.dot(cross.astype(BF16), p1["vb"], preferred_element_type=F32) + o_st[c:])
    return o1, o2


def _hgrn_exact_block(q_ref, v_ref, z_ref, lb_ref, o_scr, st_scr, dr, reverse, n_tiles):
    pos = lax.broadcasted_iota(jnp.int32, (SUBLANES, LANES), 0)
    if reverse:
        pos = (SUBLANES - 1) - pos

    def earlier(x, j):
        if j == 0:
            return x
        return pltpu.roll(x, (SUBLANES - j) if reverse else j, 0)

    def later(x, j):
        return pltpu.roll(x, j if reverse else (SUBLANES - j), 0)

    def tile(i, carry):
        ti = (n_tiles - 1 - i) if reverse else i
        r0 = pl.multiple_of(ti * SUBLANES, SUBLANES)
        for h in range(A_HEADS):
            sl = slice(h * LANES, (h + 1) * LANES)
            lb = lb_ref[:, sl]
            z = z_ref[0, pl.ds(r0, SUBLANES), sl]
            q = q_ref[0, pl.ds(r0, SUBLANES), sl]
            v = v_ref[0, pl.ds(r0, SUBLANES), sl]
            fm, kk = _hgrn_gates(z, lb)
            pp = fm
            for j in (1, 2, 4):
                pp = pp * jnp.where(pos >= j, earlier(pp, j), 1.0)
            qq = jnp.where(pos <= SUBLANES - 2, later(fm, 1), 1.0)
            for j in (1, 2, 4):
                qq = qq * jnp.where(pos <= SUBLANES - 1 - j, later(qq, j), 1.0)
            dec = pp[0:1] if reverse else pp[SUBLANES - 1:SUBLANES]
            st = st_scr[dr, h]
            o = lax.dot_general((q * pp).astype(BF16), st.astype(BF16), NT_DIMS,
                                preferred_element_type=F32)
            g = fm
            for d in range(SUBLANES):
                if d == 0:
                    e = q * kk
                else:
                    if d > 1:
                        g = g * earlier(fm, d - 1)
                    e = q * g * jnp.where(pos >= d, earlier(kk, d), 0.0)
                o = o + jnp.sum(e, axis=-1, keepdims=True) * earlier(v, d)
            o_scr[pl.ds(r0, SUBLANES), sl] = o
            u = lax.dot_general(v.astype(BF16), (kk * qq).astype(BF16), TN_DIMS,
                                preferred_element_type=F32)
            st_scr[dr, h] = dec * st + u
        return carry

    lax.fori_loop(0, n_tiles, tile, 0)


def _hgrn_scan_body(qf_ref, vf_ref, zf_ref, qb_ref, vb_ref, zb_ref, lb_ref, s0_ref, of_ref, ob_ref, sfin_ref,
                    st_scr, save_scr, ox_scr, *, n_chunks):
    step = pl.program_id(1)

    @pl.when(step == 0)
    def _():
        st_scr[...] = s0_ref[0]

    save_scr[...] = st_scr[...]
    dirs = ((qf_ref, vf_ref, zf_ref, of_ref, False), (qb_ref, vb_ref, zb_ref, ob_ref, True))
    c = HGRN_CHUNK

    def pair(i, bad):
        work, dots = [], []
        for dr, (q_ref, v_ref, z_ref, o_ref, reverse) in enumerate(dirs):
            base = pl.multiple_of(((n_chunks // 2 - 1 - i) if reverse else i) * 2 * c, 2 * c)
            rows = (pl.ds(base + c, c), pl.ds(base, c)) if reverse else (pl.ds(base, c), pl.ds(base + c, c))
            group = []
            for h in range(A_HEADS):
                sl = slice(h * LANES, (h + 1) * LANES)
                ps = [_hgrn_chunk_prep(q_ref[0, r, sl], v_ref[0, r, sl], z_ref[0, r, sl], lb_ref[:, sl], reverse)
                      for r in rows]
                bad = jnp.maximum(bad, jnp.maximum(ps[0]["bad"], ps[1]["bad"]))
                group.append((dr, h, o_ref, rows, sl, reverse, ps))
            dots += [_hgrn_pair_dots(ps[0], ps[1], st_scr[dr, h]) for dr, h, _, _, _, _, ps in group]
            work += group
        for (dr, h, _, _, _, _, ps), d in zip(work, dots):
            st_scr[dr, h] = (ps[0]["dec"] * ps[1]["dec"]) * st_scr[dr, h] + d[4]
        for (dr, h, o_ref, rows, sl, reverse, ps), d in zip(work, dots):
            o1, o2 = _hgrn_pair_out(ps[0], ps[1], *d[:4], reverse)
            o_ref[0, rows[0], sl] = o1.astype(o_ref.dtype)
            o_ref[0, rows[1], sl] = o2.astype(o_ref.dtype)
        return bad

    bad = lax.fori_loop(0, n_chunks // 2, pair, jnp.zeros((1, LANES), F32), unroll=min(8, n_chunks // 2))

    @pl.when(jnp.max(bad) > 0.0)
    def _():
        st_scr[...] = save_scr[...]
        for dr, (q_ref, v_ref, z_ref, o_ref, reverse) in enumerate(dirs):
            _hgrn_exact_block(q_ref, v_ref, z_ref, lb_ref, ox_scr, st_scr, dr, reverse, n_chunks * c // SUBLANES)
            o_ref[0] = ox_scr[...].astype(o_ref.dtype)

    @pl.when(step == pl.num_programs(1) - 1)
    def _():
        sfin_ref[0] = st_scr[...]


def _hgrn_scan(proj, lb, s0):
    b, l, _ = proj.shape
    d = A_HEADS * LANES
    t = _row_tile(l, 512)
    nb = l // t
    fwd = lambda blk: pl.BlockSpec((1, t, d), lambda i, s: (i, s, blk))
    bwd = lambda blk: pl.BlockSpec((1, t, d), lambda i, s: (i, nb - 1 - s, blk))
    st_spec = pl.BlockSpec((1,) + s0.shape[1:], lambda i, s: (i, 0, 0, 0, 0))
    return pl.pallas_call(
        functools.partial(_hgrn_scan_body, n_chunks=t // HGRN_CHUNK),
        grid=(b, nb),
        in_specs=[fwd(0), fwd(1), fwd(2), bwd(0), bwd(1), bwd(3),
                  pl.BlockSpec((1, d), lambda i, s: (0, 0)), st_spec],
        out_specs=[fwd(0), bwd(0), st_spec],
        out_shape=[jax.ShapeDtypeStruct((b, l, d), BF16), jax.ShapeDtypeStruct((b, l, d), BF16),
                   jax.ShapeDtypeStruct(s0.shape, F32)],
        scratch_shapes=[pltpu.VMEM(s0.shape[1:], F32), pltpu.VMEM(s0.shape[1:], F32), pltpu.VMEM((t, d), F32)],
        compiler_params=_params("parallel", "arbitrary"),
        name="hgrn_scan",
    )(proj, proj, proj, proj, proj, proj, lb, s0)


def _hgrn2_mixer(x_c, x_l, ng, mod_c, mod_l, w_in, lower, out_g, w_out, g1, want_ctx):
    d = x_l.shape[-1]
    p_c, g_c = _norm_mod_proj(x_c, ng, mod_c[0], mod_c[1], w_in, F32, silu_cols=(2 * d, 3 * d))
    p_l, g_l = _norm_mod_proj(x_l, ng, mod_l[0], mod_l[1], w_in, F32, silu_cols=(2 * d, 3 * d))
    lb = lower.reshape(1, d)
    zero = jnp.zeros((x_l.shape[0], 2, A_HEADS, LANES, LANES), F32)
    ocf, ocb, s_ctx = _hgrn_scan(p_c, lb, zero)
    olf, olb, _ = _hgrn_scan(p_l, lb, s_ctx)
    o_c, o_l = (ocf, ocb), (olf, olb)
    og = out_g.reshape(1, d)
    new_l = _out_call(_out_hgrn_body, [(o_l[0], 0, d), (o_l[1], 0, d), (g_l, 0, d)], [og, w_out],
                      x_l, mod_l[2], g1, "hgrn_out")
    new_c = None
    if want_ctx:
        new_c = _out_call(_out_hgrn_body, [(o_c[0], 0, d), (o_c[1], 0, d), (g_c, 0, d)], [og, w_out],
                          x_c, mod_c[2], g1, "hgrn_out")
    return new_c, new_l


def _softmax_pv(s_list, v_list):
    mx = functools.reduce(jnp.maximum, [jnp.max(s, axis=-1, keepdims=True) for s in s_list])
    ps = [jnp.exp(s - mx) for s in s_list]
    den = functools.reduce(jnp.add, [jnp.sum(p, axis=-1, keepdims=True) for p in ps])
    acc = functools.reduce(jnp.add, [jnp.dot(p.astype(BF16), v, preferred_element_type=F32)
                                     for p, v in zip(ps, v_list)])
    return acc / den


def _na_lat_body(q_ref, k_ref, v_ref, kc_ref, vc_ref, t2_ref, o_ref, *, rows, kh, scale):
    w = GRID_W
    rb = NA_ROW_BLOCK
    nk = kh + rb
    n_rel = 2 * kh - 1
    first = lax.broadcasted_iota(jnp.int32, (rb * w, LANES), 1) < (LANES // 2)
    kc = kc_ref[0]
    vc = vc_ref[0]

    def bias_index(r, ks, m):
        r0 = jnp.clip(r - kh // 2, 0, rows - kh)
        ka = ks + 2 * m
        rel_a = ka - r + (kh - 1)
        in_a = (ka >= r0) & (ka < r0 + kh)
        in_b = (ka + 1 >= r0) & (ka + 1 < r0 + kh)
        both, only_b, only_a = rel_a, (n_rel - 1) + rel_a + 1, (2 * n_rel - 1) + rel_a
        return jnp.where(in_a, jnp.where(in_b, both, only_a), jnp.where(in_b, only_b, 3 * n_rel - 1))

    def block(i, carry):
        rq = i * rb
        ks = jnp.clip(rq - kh // 2, 0, rows - nk)
        q2 = q_ref[0, pl.ds(pl.multiple_of(rq * w, rb * w), rb * w), :] * scale
        k2 = k_ref[0, pl.ds(pl.multiple_of(ks * w, w), nk * w), :]
        v2 = v_ref[0, pl.ds(pl.multiple_of(ks * w, w), nk * w), :]
        outs = []
        for hh in range(2):
            qm = jnp.where(first if hh == 0 else jnp.logical_not(first), q2, jnp.zeros_like(q2))
            bias = jnp.concatenate(
                [jnp.concatenate([t2_ref[hh, bias_index(rq + a, ks, m)] for m in range(nk // 2)], axis=-1)
                 for a in range(rb)], axis=0)
            s_w = lax.dot_general(qm, k2, NT_DIMS, preferred_element_type=F32) + bias
            s_c = lax.dot_general(qm, kc, NT_DIMS, preferred_element_type=F32)
            outs.append(_softmax_pv([s_w, s_c], [v2, vc]))
        o_ref[0, pl.ds(pl.multiple_of(rq * w, rb * w), rb * w), :] = (
            jnp.where(first, outs[0], outs[1]).astype(o_ref.dtype))
        return carry

    lax.fori_loop(0, rows // rb, block, 0, unroll=4)


def _na_ctx_body(q_ref, k_ref, v_ref, o_ref, *, scale):
    n = q_ref.shape[1]
    first = lax.broadcasted_iota(jnp.int32, (n, LANES), 1) < (LANES // 2)
    q2 = q_ref[0] * scale
    k2 = k_ref[0]
    v2 = v_ref[0]
    outs = []
    for hh in range(2):
        qm = jnp.where(first if hh == 0 else jnp.logical_not(first), q2, jnp.zeros_like(q2))
        s = lax.dot_general(qm, k2, NT_DIMS, preferred_element_type=F32)
        outs.append(_softmax_pv([s], [v2]))
    o_ref[0] = jnp.where(first, outs[0], outs[1]).astype(o_ref.dtype)


def _na_bias_table(rpb, kh, kw):
    w = jnp.arange(GRID_W)[:, None]
    c = jnp.arange(GRID_W)[None, :]
    c0 = jnp.clip(w - kw // 2, 0, GRID_W - kw)
    inside = (c >= c0) & (c < c0 + kw)
    pad = GRID_W - kw
    padded = jnp.pad(rpb.astype(F32), ((0, 0), (0, 0), (pad, pad)))
    shifted = jnp.stack([padded[:, :, pad + kw - 1 - q:pad + kw - 1 - q + GRID_W] for q in range(GRID_W)], axis=2)
    t = jnp.where(inside[None, None], shifted, NEG_BIG)
    off = jnp.full_like(t, NEG_BIG)
    pair = lambda a, b: jnp.concatenate([a, b], axis=-1)
    return jnp.concatenate([pair(t[:, :-1], t[:, 1:]), pair(off, t), pair(t, off), pair(off, off)[:, :1]], axis=1)


def _na_mixer(x_c, x_l, ng, mod_c, mod_l, w_qkv, rpb, w_out, g1, want_ctx):
    b, l, d = x_l.shape
    lc = x_c.shape[1]
    heads = rpb.shape[0]
    na_rows, na_cols = (rpb.shape[1] + 1) // 2, (rpb.shape[2] + 1) // 2
    dh = d // heads
    assert 2 * dh == LANES and l % GRID_W == 0
    rows = l // GRID_W
    kh = min(na_rows, rows)
    assert kh == na_rows and (kh + NA_ROW_BLOCK) % 2 == 0
    assert rows % NA_ROW_BLOCK == 0 and rows >= kh + NA_ROW_BLOCK
    scale = dh ** -0.5
    nhp = heads // 2
    qkv_c = _norm_mod_proj(x_c, ng, mod_c[0], mod_c[1], w_qkv, BF16)
    qkv_l = _norm_mod_proj(x_l, ng, mod_l[0], mod_l[1], w_qkv, BF16)
    t2 = _na_bias_table(rpb, kh, na_cols)
    o_l = pl.pallas_call(
        functools.partial(_na_lat_body, rows=rows, kh=kh, scale=scale),
        grid=(b, nhp),
        in_specs=[pl.BlockSpec((1, l, LANES), lambda i, p: (i, 0, p)),
                  pl.BlockSpec((1, l, LANES), lambda i, p: (i, 0, nhp + p)),
                  pl.BlockSpec((1, l, LANES), lambda i, p: (i, 0, 2 * nhp + p)),
                  pl.BlockSpec((1, lc, LANES), lambda i, p: (i, 0, nhp + p)),
                  pl.BlockSpec((1, lc, LANES), lambda i, p: (i, 0, 2 * nhp + p)),
                  pl.BlockSpec((2,) + t2.shape[1:], lambda i, p: (p, 0, 0, 0))],
        out_specs=pl.BlockSpec((1, l, LANES), lambda i, p: (i, 0, p)),
        out_shape=jax.ShapeDtypeStruct((b, l, d), BF16),
        compiler_params=_params("parallel", "parallel"),
        name="na_latent",
    )(qkv_l, qkv_l, qkv_l, qkv_c, qkv_c, t2)
    new_l = _out_call(_out_plain_body, [(o_l, 0, d)], [w_out], x_l, mod_l[2], g1, "na_out")
    new_c = None
    if want_ctx:
        o_c = pl.pallas_call(
            functools.partial(_na_ctx_body, scale=scale),
            grid=(b, nhp),
            in_specs=[pl.BlockSpec((1, lc, LANES), lambda i, p: (i, 0, p)),
                      pl.BlockSpec((1, lc, LANES), lambda i, p: (i, 0, nhp + p)),
                      pl.BlockSpec((1, lc, LANES), lambda i, p: (i, 0, 2 * nhp + p))],
            out_specs=pl.BlockSpec((1, lc, LANES), lambda i, p: (i, 0, p)),
            out_shape=jax.ShapeDtypeStruct((b, lc, d), BF16),
            compiler_params=_params("parallel", "parallel"),
            name="na_context",
        )(qkv_c, qkv_c, qkv_c)
        new_c = _out_call(_out_plain_body, [(o_c, 0, d)], [w_out], x_c, mod_c[2], g1, "na_out")
    return new_c, new_l


S5_GROUP_BLOCK = 8


def _s5_kernel_body(c_ref, w_ref, o_ref):
    for i in range(c_ref.shape[0]):
        o_ref[i] = jnp.dot(c_ref[i], w_ref[i], preferred_element_type=F32, precision=lax.Precision.HIGHEST)


def _s5_impulse(cmat, wmat):
    n = cmat.shape[0]
    gb = S5_GROUP_BLOCK
    return pl.pallas_call(
        _s5_kernel_body,
        grid=(n // gb,),
        in_specs=[pl.BlockSpec((gb,) + cmat.shape[1:], lambda i: (i, 0, 0)),
                  pl.BlockSpec((gb,) + wmat.shape[1:], lambda i: (i, 0, 0))],
        out_specs=pl.BlockSpec((gb, cmat.shape[1], wmat.shape[2]), lambda i: (i, 0, 0)),
        out_shape=jax.ShapeDtypeStruct((n, cmat.shape[1], wmat.shape[2]), F32),
        compiler_params=_params("parallel"),
        name="s5_impulse",
    )(cmat, wmat)


def _s5_operators(lam_re, lam_im, log_dt, b_re, b_im, c_re, c_im, t):
    _, g, p = lam_re.shape
    cg = b_re.shape[-1]
    lam = lax.complex(lam_re.astype(F32), lam_im.astype(F32))
    ldt = lam * jnp.exp(log_dt.astype(F32))[..., None]
    a = jnp.exp(ldt)
    bbar = ((a - 1.0) / lam)[..., None] * lax.complex(b_re.astype(F32), b_im.astype(F32))
    cm = lax.complex(c_re.astype(F32), c_im.astype(F32))
    apow = jnp.exp(ldt[..., None] * jnp.arange(t + 1, dtype=F32))
    w = apow[..., :t, None] * bbar[:, :, :, None, :]
    wmat = jnp.concatenate([jnp.real(w), jnp.imag(w)], axis=2).reshape(2 * g, 2 * p, t * cg)
    cmat = jnp.concatenate([jnp.real(cm), -jnp.imag(cm)], axis=-1).reshape(2 * g, cg, 2 * p)
    k = _s5_impulse(cmat, wmat).reshape(2, g, cg, t, cg)
    k = jnp.transpose(k, (0, 1, 3, 4, 2))
    kfull = jnp.concatenate([k[1, :, :0:-1], (k[0, :, :1] + k[1, :, :1]), k[0, :, 1:]], axis=1)
    lagvec = jnp.transpose(kfull, (0, 2, 1, 3)).reshape(g, cg, (2 * t - 1) * cg).astype(BF16)
    toep = jnp.stack([lagvec[:, :, (t - 1 - s) * cg:(2 * t - 1 - s) * cg] for s in range(t)], axis=1)
    toep = toep.reshape(g, t * cg, t * cg)

    def state_in(wd, flip):
        wd = wd[:, :, ::-1] if flip else wd
        m = jnp.transpose(wd, (0, 2, 3, 1)).reshape(g, t * cg, p)
        return jnp.concatenate([jnp.real(m), jnp.imag(m)], axis=-1)

    def state_out(cd, pw):
        n = cd[:, :, :, None] * pw[:, None, :, :]
        n = jnp.transpose(n, (0, 2, 3, 1)).reshape(g, p, t * cg)
        return jnp.concatenate([jnp.real(n), -jnp.imag(n)], axis=1)

    m_f = state_in(w[0], True)
    m_b = state_in(w[1], False)
    n_f = state_out(cm[0], apow[0][..., 1:])
    n_b = state_out(cm[1], apow[1][..., :0:-1])
    at = apow[..., t]
    a1 = jnp.concatenate([jnp.real(at), jnp.real(at)], axis=-1)
    a2 = jnp.concatenate([-jnp.imag(at), jnp.imag(at)], axis=-1)
    bf = lambda m: m.astype(BF16)
    return dict(toep=bf(toep), m_f=bf(m_f), m_b=bf(m_b), n_f=bf(n_f), n_b=bf(n_b), a1=a1, a2=a2)


def _s5_body(u_ref, dsk_ref, toep_ref, mf_ref, mb_ref, nf_ref, nb_ref, a1_ref, a2_ref, x0_ref,
             y_ref, xfin_ref, z_scr, zs_scr, xin_scr):
    gb, nc = u_ref.shape[1], u_ref.shape[2]
    half = LANES // 2
    ubs = [u_ref[0, gi].astype(BF16) for gi in range(gb)]
    for dr, m_ref in enumerate((mf_ref, mb_ref)):
        z = jnp.stack([jnp.dot(ubs[gi], m_ref[gi], preferred_element_type=F32) for gi in range(gb)])
        z_scr[dr] = jnp.swapaxes(z, 0, 1)

    swap = lambda t: pltpu.roll(t, half, t.ndim - 1)
    for dr in range(2):
        zs_scr[dr] = swap(z_scr[dr].reshape(nc * gb, LANES)).reshape(nc, gb, LANES)

    coef = [(a1_ref[dr], a2_ref[dr]) for dr in range(2)]

    def scan(j, xs):
        out = []
        for dr in range(2):
            jj = j if dr == 0 else nc - 1 - j
            x, xw = xs[2 * dr], xs[2 * dr + 1]
            a1, a2 = coef[dr]
            xin_scr[dr, jj] = x
            out.append(a1 * x + a2 * xw + z_scr[dr, jj])
            out.append(a1 * xw - a2 * x + zs_scr[dr, jj])
        return tuple(out)

    x0f, x0b = x0_ref[0, 0], x0_ref[0, 1]
    xf, _, xb, _ = lax.fori_loop(0, nc, scan, (x0f, swap(x0f), x0b, swap(x0b)), unroll=4)
    xfin_ref[0, 0] = xf
    xfin_ref[0, 1] = xb
    xin = [jnp.swapaxes(xin_scr[dr], 0, 1).astype(BF16) for dr in range(2)]
    for gi in range(gb):
        u = u_ref[0, gi]
        y = jnp.dot(ubs[gi], toep_ref[gi], preferred_element_type=F32) + u * dsk_ref[gi]
        y = y + jnp.dot(xin[0][gi], nf_ref[gi], preferred_element_type=F32)
        y = y + jnp.dot(xin[1][gi], nb_ref[gi], preferred_element_type=F32)
        y_ref[0, gi] = y


def _piece_transpose(src, piece_of, cg):
    n = len(src)
    out = [None] * n
    for s in range(n):
        c = src[s % n]
        for p in range(1, n):
            c = jnp.where(piece_of == p, src[(p + s) % n], c)
        r = c if s == 0 else pltpu.roll(c, s * cg, 1)
        for p in range(n):
            out[p] = r if s == 0 else jnp.where(piece_of == (p + s) % n, r, out[p])
    return out


def _to_chunks(h_scr, o_ref, t, cg):
    ncb = h_scr.shape[1] // t
    pieces = LANES // cg
    piece_of = lax.broadcasted_iota(jnp.int32, (ncb, LANES), 1) // cg

    def lane_block(lb, carry):
        for tq in range(t // pieces):
            steps = [h_scr[lb, pl.ds(tq * pieces + tr, ncb, stride=t), :] for tr in range(pieces)]
            for gl, col in enumerate(_piece_transpose(steps, piece_of, cg)):
                o_ref[0, lb * pieces + gl, :, tq * LANES:(tq + 1) * LANES] = col
        return carry

    lax.fori_loop(0, h_scr.shape[0], lane_block, 0, unroll=2)


def _from_chunks(y_ref, y_scr, t, cg):
    ncb = y_ref.shape[2]
    pieces = LANES // cg
    piece_of = lax.broadcasted_iota(jnp.int32, (ncb, LANES), 1) // cg

    def lane_block(lb, carry):
        for tq in range(t // pieces):
            cols = [y_ref[0, lb * pieces + gl, :, tq * LANES:(tq + 1) * LANES] for gl in range(pieces)]
            for tr, step in enumerate(_piece_transpose(cols, piece_of, cg)):
                y_scr[lb, pl.ds(tq * pieces + tr, ncb, stride=t), :] = step
        return carry

    lax.fori_loop(0, y_scr.shape[0], lane_block, 0, unroll=2)


def _norm_chunks_body(x_ref, g_ref, sh_ref, sc_ref, o_ref, h_scr, *, t, cg):
    h = _norm_mod(x_ref[...], g_ref[...], sh_ref[0], sc_ref[0])
    for lb in range(h_scr.shape[0]):
        h_scr[lb] = h[:, lb * LANES:(lb + 1) * LANES]
    _to_chunks(h_scr, o_ref, t, cg)


def _norm_mod_chunks(x, g, sh, sc, groups, t):
    b, l, d = x.shape
    cg = d // groups
    tm = _row_tile(l, 512)
    per = l // tm
    assert tm % t == 0 and (tm // t) % SUBLANES == 0 and LANES % cg == 0 and t % (LANES // cg) == 0
    return pl.pallas_call(
        functools.partial(_norm_chunks_body, t=t, cg=cg),
        grid=(b * per,),
        in_specs=[pl.BlockSpec((tm, d), lambda i: (i, 0)),
                  pl.BlockSpec((1, d), lambda i: (0, 0)),
                  _mod_spec(sh, per), _mod_spec(sc, per)],
        out_specs=pl.BlockSpec((1, groups, tm // t, t * cg), lambda i: (i // per, 0, i % per, 0)),
        out_shape=jax.ShapeDtypeStruct((b, groups, l // t, t * cg), F32),
        scratch_shapes=[pltpu.VMEM((d // LANES, tm, LANES), F32)],
        compiler_params=_params("parallel"),
        name="norm_mod_chunks",
    )(x.reshape(b * l, d), g.reshape(1, d), sh.table, sc.table)


def _glu_chunks_body(y_ref, w_ref, res_ref, gate_ref, g1_ref, o_ref, y_scr, *, t, cg):
    _from_chunks(y_ref, y_scr, t, cg)
    d = o_ref.shape[-1]
    y = jnp.concatenate([y_scr[lb] for lb in range(y_scr.shape[0])], axis=-1)
    a = _gelu_tanh(y).astype(BF16)
    ag = jnp.dot(a, w_ref[...], preferred_element_type=F32)
    o_ref[...] = _residual(res_ref, gate_ref, g1_ref, ag[:, :d] * jax.nn.sigmoid(ag[:, d:]))


def _glu_out_chunks(y, w_glu, res, gate, g1, t):
    b, l, d = res.shape
    groups = y.shape[1]
    cg = d // groups
    tm = _row_tile(l, 512)
    per = l // tm
    out = pl.pallas_call(
        functools.partial(_glu_chunks_body, t=t, cg=cg),
        grid=(b * per,),
        in_specs=[pl.BlockSpec((1, groups, tm // t, t * cg), lambda i: (i // per, 0, i % per, 0)),
                  _layer_spec(w_glu, w_glu.shape, lambda i: (0, 0)),
                  pl.BlockSpec((tm, d), lambda i: (i, 0)),
                  _mod_spec(gate, per),
                  pl.BlockSpec((1, d), lambda i: (0, 0))],
        out_specs=pl.BlockSpec((tm, d), lambda i: (i, 0)),
        out_shape=jax.ShapeDtypeStruct((b * l, d), F32),
        scratch_shapes=[pltpu.VMEM((d // LANES, tm, LANES), F32)],
        compiler_params=_params("parallel"),
        name="s5_glu_out",
    )(y, w_glu.stack, res.reshape(b * l, d), gate.table, g1.reshape(1, d))
    return out.reshape(b, l, d)


def _s5_scan(u, ops, dsk, x0):
    b, g, nc, tc = u.shape
    gb = S5_GROUP_BLOCK
    per_g = lambda shape: pl.BlockSpec((gb,) + shape, lambda j, i: (j,) + (0,) * len(shape))
    y, xfin = pl.pallas_call(
        _s5_body,
        grid=(g // gb, b),
        in_specs=[pl.BlockSpec((1, gb, nc, tc), lambda j, i: (i, j, 0, 0)),
                  per_g((1, tc)), per_g((tc, tc)), per_g((tc, LANES)), per_g((tc, LANES)),
                  per_g((LANES, tc)), per_g((LANES, tc)),
                  pl.BlockSpec((2, gb, LANES), lambda j, i: (0, j, 0)),
                  pl.BlockSpec((2, gb, LANES), lambda j, i: (0, j, 0)),
                  pl.BlockSpec((1, 2, gb, LANES), lambda j, i: (i, 0, j, 0))],
        out_specs=[pl.BlockSpec((1, gb, nc, tc), lambda j, i: (i, j, 0, 0)),
                   pl.BlockSpec((1, 2, gb, LANES), lambda j, i: (i, 0, j, 0))],
        out_shape=[jax.ShapeDtypeStruct((b, g, nc, tc), F32), jax.ShapeDtypeStruct((b, 2, g, LANES), F32)],
        scratch_shapes=[pltpu.VMEM((2, nc, gb, LANES), F32)] * 3,
        compiler_params=_params("parallel", "parallel"),
        name="s5_scan",
    )(u, dsk, ops["toep"], ops["m_f"], ops["m_b"], ops["n_f"], ops["n_b"], ops["a1"], ops["a2"], x0)
    return y, xfin


def _s5_mixer(x_c, x_l, ng, mod_c, mod_l, lam_re, lam_im, log_dt, b_re, b_im, c_re, c_im, d_skip, w_glu, g1,
              want_ctx):
    b, l, d = x_l.shape
    g, p = lam_re.shape[1], lam_re.shape[2]
    cg = d // g
    assert 2 * p == LANES and g % S5_GROUP_BLOCK == 0
    ops = _s5_operators(lam_re, lam_im, log_dt, b_re, b_im, c_re, c_im, S5_CHUNK)
    dsk = jnp.tile(d_skip.astype(F32).reshape(g, 1, cg), (1, 1, S5_CHUNK))
    u_c = _norm_mod_chunks(x_c, ng, mod_c[0], mod_c[1], g, S5_CHUNK)
    u_l = _norm_mod_chunks(x_l, ng, mod_l[0], mod_l[1], g, S5_CHUNK)
    y_c, x_ctx = _s5_scan(u_c, ops, dsk, jnp.zeros((b, 2, g, LANES), F32))
    y_l, _ = _s5_scan(u_l, ops, dsk, x_ctx)
    new_l = _glu_out_chunks(y_l, w_glu, x_l, mod_l[2], g1, S5_CHUNK)
    new_c = _glu_out_chunks(y_c, w_glu, x_c, mod_c[2], g1, S5_CHUNK) if want_ctx else None
    return new_c, new_l


FFN_HALO = 16
FFN_TN = 256
FFN_TILES_PER_STEP = 11
FFN_ROW_SPLIT = 256


def _ffn_body(xp_ref, x_ref, xn_ref, g2_ref, sh_ref, sc_ref, *rest, per, n_tiles, tps):
    w_refs = rest[:2 * tps]
    cw_ref, cb_ref, wo_ref, gate_ref, g3_ref, o_ref, h_scr, gat_scr = rest[2 * tps:]
    i, j = pl.program_id(0), pl.program_id(1)
    tm = x_ref.shape[0]
    hl = FFN_HALO
    f = wo_ref.shape[0]
    n_steps = pl.cdiv(n_tiles, tps)
    when = (lambda cond: lambda fn: fn()) if n_steps == 1 else pl.when

    @when(j == 0)
    def _():
        g, sh, sc = g2_ref[...], sh_ref[0], sc_ref[0]
        keep_p = ((i % per) != 0).astype(F32)
        keep_n = ((i % per) != per - 1).astype(F32)
        h_scr[0:hl] = (_norm_mod(xp_ref[...], g, sh, sc) * keep_p).astype(BF16)
        h_scr[hl:hl + tm] = _norm_mod(x_ref[...], g, sh, sc).astype(BF16)
        h_scr[hl + tm:] = (_norm_mod(xn_ref[...], g, sh, sc) * keep_n).astype(BF16)

    sub = FFN_ROW_SPLIT if tm % FFN_ROW_SPLIT == 0 else tm
    ns = tm // sub
    edge = SUBLANES
    tn = w_refs[0].shape[1]

    def pieces(w_ref):
        out = []
        for s in range(ns):
            lo = hl + s * sub - (hl if s == 0 else 0)
            hi = hl + (s + 1) * sub + (hl if s == ns - 1 else 0)
            out.append(jnp.dot(h_scr[lo:hi], w_ref[...], preferred_element_type=F32))
        return out

    def conv(p, s, cw, cb):
        base = hl if s == 0 else 0
        left = p[s][hl - edge:hl] if s == 0 else p[s - 1][-edge:]
        right = p[s][base + sub:base + sub + edge] if s == ns - 1 else p[s + 1][:edge]
        ext = jnp.concatenate([left, p[s][base:base + sub], right], axis=0)
        um = pltpu.roll(ext, 1, 0)[edge:edge + sub]
        up = pltpu.roll(ext, sub + 2 * edge - 1, 0)[edge:edge + sub]
        return cb + um * cw[0:1] + ext[edge:edge + sub] * cw[1:2] + up * cw[2:3]

    def tile(k):
        col = k * tn if n_steps == 1 else pl.multiple_of((j * tps + k) * tn, tn)
        cols_a, cols_v = pl.ds(col, tn), pl.ds(pl.multiple_of(f + col, tn), tn)
        pa, pv = pieces(w_refs[2 * k]), pieces(w_refs[2 * k + 1])
        for s in range(ns):
            a = conv(pa, s, cw_ref[:, cols_a], cb_ref[:, cols_a])
            v = conv(pv, s, cw_ref[:, cols_v], cb_ref[:, cols_v])
            gat_scr[s * sub:(s + 1) * sub, cols_a] = (a * jax.nn.sigmoid(a) * v).astype(BF16)

    always = n_tiles - tps * (n_steps - 1)
    for k in range(tps):
        if k < always:
            tile(k)
        else:
            pl.when(j * tps + k < n_tiles)(functools.partial(tile, k))

    @when(j == n_steps - 1)
    def _():
        y = jnp.dot(gat_scr[...], wo_ref[...], preferred_element_type=F32)
        o_ref[...] = x_ref[...] + gate_ref[0] * _rms(y, g3_ref[...])


def _conv_ffn_block(x, ng2, mod, w_in, conv_w, conv_b, w_out, ng3):
    b, l, d = x.shape
    f = w_out.shape[0]
    tn, hl = FFN_TN, FFN_HALO
    assert f % tn == 0 and conv_w.shape[0] == 3 and w_in.shape[1] == 2 * f
    nj = f // tn
    tm = _row_tile(l, 1024)
    per = l // tm
    hb = tm // hl
    last_hb = b * l // hl - 1
    cb = conv_b.reshape(1, 2 * f)
    x2 = x.reshape(b * l, d)
    tps = min(FFN_TILES_PER_STEP, nj)
    once = dict(pipeline_mode=pl.Buffered(1)) if tps == nj else {}
    w_specs = []
    for k in range(tps):
        w_specs.append(_layer_spec(w_in, (d, tn), lambda i, j, k=k: (0, jnp.minimum(j * tps + k, nj - 1)), **once))
        w_specs.append(_layer_spec(w_in, (d, tn), lambda i, j, k=k: (0, nj + jnp.minimum(j * tps + k, nj - 1)),
                                   **once))
    whole = lambda arr: pl.BlockSpec(arr.shape, lambda i, j: (0,) * arr.ndim)
    out = pl.pallas_call(
        functools.partial(_ffn_body, per=per, n_tiles=nj, tps=tps),
        grid=(b * per, pl.cdiv(nj, tps)),
        in_specs=[pl.BlockSpec((hl, d), lambda i, j: (jnp.maximum(i * hb - 1, 0), 0)),
                  pl.BlockSpec((tm, d), lambda i, j: (i, 0)),
                  pl.BlockSpec((hl, d), lambda i, j: (jnp.minimum((i + 1) * hb, last_hb), 0)),
                  pl.BlockSpec((1, d), lambda i, j: (0, 0)),
                  _mod_spec(mod[3], per), _mod_spec(mod[4], per),
                  *w_specs, whole(conv_w), whole(cb),
                  _layer_spec(w_out, (f, d), lambda i, j: (0, 0), pipeline_mode=pl.Buffered(1)),
                  _mod_spec(mod[5], per),
                  pl.BlockSpec((1, d), lambda i, j: (0, 0))],
        out_specs=pl.BlockSpec((tm, d), lambda i, j: (i, 0)),
        out_shape=jax.ShapeDtypeStruct((b * l, d), F32),
        scratch_shapes=[pltpu.VMEM((tm + 2 * hl, d), BF16), pltpu.VMEM((tm, f), BF16)],
        compiler_params=_params("parallel", "arbitrary"),
        name="conv_ffn",
    )(x2, x2, x2, ng2.reshape(1, d), mod[3].table, mod[4].table, *([w_in.stack] * (2 * tps)), conv_w, cb,
      w_out.stack,
      mod[5].table, ng3.reshape(1, d))
    return out.reshape(b, l, d)


def kernel(x, c, ctx, c_ctx, w_mod, b_mod, norm_g, a_w_in, a_lower_logits, a_out_g, a_w_out, b_w_qkv, b_rpb,
           b_w_out, c_lam_re, c_lam_im, c_log_dt, c_b_re, c_b_im, c_c_re, c_c_im, c_d, c_w_glu, f_w_in,
           f_conv_w, f_conv_b, f_w_out):
    bsz, _, d = x.shape
    depth = w_mod.shape[0]
    p = jax.nn.softmax(a_lower_logits.astype(F32), axis=0)
    lower = jnp.cumsum(p, axis=0) - p[0]
    n_rows = -(-(bsz + 1) // SUBLANES) * SUBLANES
    c_rows = jnp.zeros((n_rows, d), F32).at[:bsz].set(c).at[bsz].set(c_ctx)
    table = _modulation(c_rows, w_mod, b_mod).reshape(depth * n_rows * N_MOD, 1, d)
    wa_in, wa_out, wb_qkv, wb_out, wc_glu, wf_in, wf_out = (
        w.astype(BF16) for w in (a_w_in, a_w_out, b_w_qkv, b_w_out, c_w_glu, f_w_in, f_w_out))
    lat, cx = x, ctx
    for i in range(depth):
        kind, j = i % N_MIXERS, i // N_MIXERS
        last = i == depth - 1
        mod_l = [ModRow(table, i * n_rows * N_MOD + k, N_MOD) for k in range(N_MOD)]
        mod_c = [ModRow(table, (i * n_rows + bsz) * N_MOD + k, 0) for k in range(N_MOD)]
        ng = norm_g[i]
        if kind == 0:
            cx1, lat = _hgrn2_mixer(cx, lat, ng[0], mod_c, mod_l, LayerOf(wa_in, j), lower[i], a_out_g[j],
                                    LayerOf(wa_out, j), ng[1], not last)
        elif kind == 1:
            cx1, lat = _na_mixer(cx, lat, ng[0], mod_c, mod_l, LayerOf(wb_qkv, j), b_rpb[j], LayerOf(wb_out, j),
                                 ng[1], not last)
        else:
            cx1, lat = _s5_mixer(cx, lat, ng[0], mod_c, mod_l, c_lam_re[j], c_lam_im[j], c_log_dt[j], c_b_re[j],
                                 c_b_im[j], c_c_re[j], c_c_im[j], c_d[j], LayerOf(wc_glu, j), ng[1], not last)
        ffn = (LayerOf(wf_in, i), f_conv_w[i], f_conv_b[i], LayerOf(wf_out, i), ng[3])
        lat = _conv_ffn_block(lat, ng[2], mod_l, *ffn)
        if not last:
            cx = _conv_ffn_block(cx1, ng[2], mod_c, *ffn)
    return lat
```
